```python
import math
import jax, jax.numpy as jnp
from jax import lax
import numpy as np

D_MODEL = 2048
BATCH = 8
SEQ = 4096
DEPTH = 1

GRID_W = 64
WIN_H = 8
WIN_W = 16
ATTN_WIDTH = D_MODEL // 2
SSM_WIDTH = D_MODEL - ATTN_WIDTH
MIX_WIDTH = ATTN_WIDTH + SSM_WIDTH
HEAD_DIM = 64
N_HEADS = ATTN_WIDTH // HEAD_DIM
SSM_GROUP_CH = 16
SSM_GROUPS = SSM_WIDTH // SSM_GROUP_CH
SSM_STATE = 64
N_DIRS = 2
D_FF = ((8 * D_MODEL + 3 * 256 - 1) // (3 * 256)) * 256
IN_WIDTH = 3 * ATTN_WIDTH + SSM_WIDTH
RMS_EPS = 1e-6
NEG_INF = -1e30

kernel_name = "hymba_natten_s5_encoder_block"


def rmsnorm(x, g):
    xf = x.astype(jnp.float32)
    y = xf * lax.rsqrt(jnp.mean(xf * xf, axis=-1, keepdims=True) + RMS_EPS)
    return (y * g.astype(jnp.float32)).astype(x.dtype)


def neighbourhood_attention(q, k, v, q_gain, k_gain, rpb):
    b, s, h, dh = q.shape
    rows = s // GRID_W
    kh = min(WIN_H, rows)
    q = rmsnorm(q, q_gain)
    k = rmsnorm(k, k_gain)
    qg = q.reshape(b, rows, GRID_W, h, dh)
    kg = k.reshape(b, rows, GRID_W, h, dh)
    vg = v.reshape(b, rows, GRID_W, h, dh)
    r_idx = jnp.arange(rows)
    row_start = jnp.clip(r_idx - kh // 2, 0, rows - kh)
    key_rows = row_start[:, None] + jnp.arange(kh)[None, :]
    k_blk = kg[:, key_rows]
    v_blk = vg[:, key_rows]
    c_idx = jnp.arange(GRID_W)
    col_start = jnp.clip(c_idx - WIN_W // 2, 0, GRID_W - WIN_W)
    col_in = (c_idx[None, :] >= col_start[:, None]) & (c_idx[None, :] < col_start[:, None] + WIN_W)
    dr_idx = key_rows - r_idx[:, None] + (WIN_H - 1)
    dc_idx = jnp.clip(c_idx[None, :] - c_idx[:, None], -(WIN_W - 1), WIN_W - 1) + (WIN_W - 1)
    bias = rpb[:, dr_idx[:, None, :, None], dc_idx[None, :, None, :]].astype(jnp.float32)
    scale = 1.0 / math.sqrt(dh)
    scores = jnp.einsum('brqhd,brikhd->bhrqik', qg, k_blk).astype(jnp.float32) * scale
    scores = jnp.where(col_in[None, None, None, :, None, :], scores + bias[None], NEG_INF)
    p = jax.nn.softmax(scores.reshape(b, h, rows, GRID_W, kh * GRID_W), axis=-1)
    p = p.reshape(b, h, rows, GRID_W, kh, GRID_W).astype(v.dtype)
    out = jnp.einsum('bhrqik,brikhd->brqhd', p, v_blk)
    return out.reshape(b, s, h * dh)


def _linear_recurrence(e1, e2):
    a1, b1 = e1
    a2, b2 = e2
    return a2 * a1, a2 * b1 + b2


def s5_bidirectional(u, a_re, a_im, b_re, b_im, c_re, c_im, log_step, d_skip, w_glu, b_glu):
    b, s, _ = u.shape
    ug = u.astype(jnp.float32).reshape(b, s, SSM_GROUPS, SSM_GROUP_CH)
    uc = lax.complex(ug, jnp.zeros_like(ug))
    y = jnp.zeros_like(ug)
    for d in range(N_DIRS):
        lam = lax.complex(jnp.minimum(a_re[d].astype(jnp.float32), -1e-4), a_im[d].astype(jnp.float32))
        dt = jnp.exp(log_step[d].astype(jnp.float32))[:, None]
        lam_bar = jnp.exp(lam * dt)
        b_mat = lax.complex(b_re[d].astype(jnp.float32), b_im[d].astype(jnp.float32))
        b_bar = ((lam_bar - 1.0) / lam)[:, :, None] * b_mat
        c_mat = lax.complex(c_re[d].astype(jnp.float32), c_im[d].astype(jnp.float32))
        bu = jnp.einsum('bsgc,gpc->bsgp', uc, b_bar)
        a_seq = jnp.broadcast_to(lam_bar[None, None], (1, s, SSM_GROUPS, SSM_STATE))
        _, states = lax.associative_scan(_linear_recurrence, (a_seq, bu), reverse=(d == 1), axis=1)
        y = y + jnp.real(jnp.einsum('bsgp,gcp->bsgc', states, c_mat))
    y = y.reshape(b, s, SSM_WIDTH) + d_skip.astype(jnp.float32) * ug.reshape(b, s, SSM_WIDTH)
    y = jax.nn.gelu(y.astype(u.dtype))
    return y * jax.nn.sigmoid(y @ w_glu + b_glu)


def _fwd_setup_inputs(seed: int = 0) -> dict:
    key = jax.random.key(seed)
    ks = jax.random.split(key, 24)
    f32 = jnp.float32
    L, G, P, C = DEPTH, SSM_GROUPS, SSM_STATE, SSM_GROUP_CH
    nrm = lambda k, shp, sc: jax.random.normal(k, shp, f32) * sc
    a_im_base = jnp.pi * jnp.arange(P, dtype=f32)
    return {
        "x": jax.random.normal(ks[0], (BATCH, SEQ, D_MODEL), f32),
        "g_mix": 1.0 + nrm(ks[1], (L, D_MODEL), 0.05),
        "w_in": nrm(ks[2], (L, D_MODEL, IN_WIDTH), D_MODEL ** -0.5),
        "q_gain": 1.0 + nrm(ks[3], (L, HEAD_DIM), 0.05),
        "k_gain": 1.0 + nrm(ks[4], (L, HEAD_DIM), 0.05),
        "rpb": nrm(ks[5], (L, N_HEADS, 2 * WIN_H - 1, 2 * WIN_W - 1), 0.5),
        "ssm_a_re": -0.5 + nrm(ks[6], (L, N_DIRS, G, P), 0.01),
        "ssm_a_im": a_im_base + nrm(ks[7], (L, N_DIRS, G, P), 0.01),
        "ssm_b_re": nrm(ks[8], (L, N_DIRS, G, P, C), (2 * C) ** -0.5),
        "ssm_b_im": nrm(ks[9], (L, N_DIRS, G, P, C), (2 * C) ** -0.5),
        "ssm_c_re": nrm(ks[10], (L, N_DIRS, G, C, P), (2 * P) ** -0.5),
        "ssm_c_im": nrm(ks[11], (L, N_DIRS, G, C, P), (2 * P) ** -0.5),
        "ssm_log_step": jax.random.uniform(ks[12], (L, N_DIRS, G), f32, math.log(1e-3), math.log(1e-1)),
        "ssm_d": nrm(ks[13], (L, SSM_WIDTH), 1.0),
        "w_glu": nrm(ks[14], (L, SSM_WIDTH, SSM_WIDTH), SSM_WIDTH ** -0.5),
        "b_glu": nrm(ks[15], (L, SSM_WIDTH), 0.02),
        "g_out_attn": 1.0 + nrm(ks[16], (L, ATTN_WIDTH), 0.05),
        "g_out_ssm": 1.0 + nrm(ks[17], (L, SSM_WIDTH), 0.05),
        "w_out": nrm(ks[18], (L, MIX_WIDTH, D_MODEL), MIX_WIDTH ** -0.5),
        "g_ffn": 1.0 + nrm(ks[19], (L, D_MODEL), 0.05),
        "w_ffn_gate": nrm(ks[20], (L, D_MODEL, D_FF), D_MODEL ** -0.5),
        "w_ffn_up": nrm(ks[21], (L, D_MODEL, D_FF), D_MODEL ** -0.5),
        "w_ffn_down": nrm(ks[22], (L, D_FF, D_MODEL), D_FF ** -0.5),
    }


def _fwd_reference(x, g_mix, w_in, q_gain, k_gain, rpb, ssm_a_re, ssm_a_im, ssm_b_re, ssm_b_im,
              ssm_c_re, ssm_c_im, ssm_log_step, ssm_d, w_glu, b_glu, g_out_attn, g_out_ssm,
              w_out, g_ffn, w_ffn_gate, w_ffn_up, w_ffn_down):
    b, s, _ = x.shape
    for l in range(DEPTH):
        h = rmsnorm(x, g_mix[l])
        z = h @ w_in[l]
        q, k, v, u = jnp.split(z, [ATTN_WIDTH, 2 * ATTN_WIDTH, 3 * ATTN_WIDTH], axis=-1)
        q = q.reshape(b, s, N_HEADS, HEAD_DIM)
        k = k.reshape(b, s, N_HEADS, HEAD_DIM)
        v = v.reshape(b, s, N_HEADS, HEAD_DIM)
        ya = neighbourhood_attention(q, k, v, q_gain[l], k_gain[l], rpb[l])
        ys = s5_bidirectional(u, ssm_a_re[l], ssm_a_im[l], ssm_b_re[l], ssm_b_im[l],
                              ssm_c_re[l], ssm_c_im[l], ssm_log_step[l], ssm_d[l],
                              w_glu[l], b_glu[l])
        y = jnp.concatenate([rmsnorm(ya, g_out_attn[l]), rmsnorm(ys, g_out_ssm[l])], axis=-1)
        x = x + y @ w_out[l]
        h = rmsnorm(x, g_ffn[l])
        x = x + (jax.nn.silu(h @ w_ffn_gate[l]) * (h @ w_ffn_up[l])) @ w_ffn_down[l]
    return x


import jax as _jax
import jax.numpy as _jnp

TWIN_FORMAT = 'train_step'
FWD_PARAMS = ['x', 'g_mix', 'w_in', 'q_gain', 'k_gain', 'rpb', 'ssm_a_re', 'ssm_a_im', 'ssm_b_re', 'ssm_b_im', 'ssm_c_re', 'ssm_c_im', 'ssm_log_step', 'ssm_d', 'w_glu', 'b_glu', 'g_out_attn', 'g_out_ssm', 'w_out', 'g_ffn', 'w_ffn_gate', 'w_ffn_up', 'w_ffn_down']
TWIN_WEIGHTS = ['g_mix', 'w_in', 'q_gain', 'k_gain', 'rpb', 'ssm_a_re', 'ssm_a_im', 'ssm_b_re', 'ssm_b_im', 'ssm_c_re', 'ssm_c_im', 'ssm_log_step', 'ssm_d', 'w_glu', 'b_glu', 'g_out_attn', 'g_out_ssm', 'w_out', 'g_ffn', 'w_ffn_gate', 'w_ffn_up', 'w_ffn_down']
TWIN_DIFF_INPUT = 'x'
TWIN_INPUTS = ['x', 'g_mix', 'w_in', 'q_gain', 'k_gain', 'rpb', 'ssm_a_re', 'ssm_a_im', 'ssm_b_re', 'ssm_b_im', 'ssm_c_re', 'ssm_c_im', 'ssm_log_step', 'ssm_d', 'w_glu', 'b_glu', 'g_out_attn', 'g_out_ssm', 'w_out', 'g_ffn', 'w_ffn_gate', 'w_ffn_up', 'w_ffn_down', 'loss_target', 'm_g_mix', 'm_w_in', 'm_q_gain', 'm_k_gain', 'm_rpb', 'm_ssm_a_re', 'm_ssm_a_im', 'm_ssm_b_re', 'm_ssm_b_im', 'm_ssm_c_re', 'm_ssm_c_im', 'm_ssm_log_step', 'm_ssm_d', 'm_w_glu', 'm_b_glu', 'm_g_out_attn', 'm_g_out_ssm', 'm_w_out', 'm_g_ffn', 'm_w_ffn_gate', 'm_w_ffn_up', 'm_w_ffn_down', 'v_g_mix', 'v_w_in', 'v_q_gain', 'v_k_gain', 'v_rpb', 'v_ssm_a_re', 'v_ssm_a_im', 'v_ssm_b_re', 'v_ssm_b_im', 'v_ssm_c_re', 'v_ssm_c_im', 'v_ssm_log_step', 'v_ssm_d', 'v_w_glu', 'v_b_glu', 'v_g_out_attn', 'v_g_out_ssm', 'v_w_out', 'v_g_ffn', 'v_w_ffn_gate', 'v_w_ffn_up', 'v_w_ffn_down']
TWIN_OUTPUTS = ['loss', 'grad_x', 'grad_g_mix', 'grad_w_in', 'grad_q_gain', 'grad_k_gain', 'grad_rpb', 'grad_ssm_a_re', 'grad_ssm_a_im', 'grad_ssm_b_re', 'grad_ssm_b_im', 'grad_ssm_c_re', 'grad_ssm_c_im', 'grad_ssm_log_step', 'grad_ssm_d', 'grad_w_glu', 'grad_b_glu', 'grad_g_out_attn', 'grad_g_out_ssm', 'grad_w_out', 'grad_g_ffn', 'grad_w_ffn_gate', 'grad_w_ffn_up', 'grad_w_ffn_down', 'delta_g_mix', 'delta_w_in', 'delta_q_gain', 'delta_k_gain', 'delta_rpb', 'delta_ssm_a_re', 'delta_ssm_a_im', 'delta_ssm_b_re', 'delta_ssm_b_im', 'delta_ssm_c_re', 'delta_ssm_c_im', 'delta_ssm_log_step', 'delta_ssm_d', 'delta_w_glu', 'delta_b_glu', 'delta_g_out_attn', 'delta_g_out_ssm', 'delta_w_out', 'delta_g_ffn', 'delta_w_ffn_gate', 'delta_w_ffn_up', 'delta_w_ffn_down', 'new_m_g_mix', 'new_m_w_in', 'new_m_q_gain', 'new_m_k_gain', 'new_m_rpb', 'new_m_ssm_a_re', 'new_m_ssm_a_im', 'new_m_ssm_b_re', 'new_m_ssm_b_im', 'new_m_ssm_c_re', 'new_m_ssm_c_im', 'new_m_ssm_log_step', 'new_m_ssm_d', 'new_m_w_glu', 'new_m_b_glu', 'new_m_g_out_attn', 'new_m_g_out_ssm', 'new_m_w_out', 'new_m_g_ffn', 'new_m_w_ffn_gate', 'new_m_w_ffn_up', 'new_m_w_ffn_down', 'new_v_g_mix', 'new_v_w_in', 'new_v_q_gain', 'new_v_k_gain', 'new_v_rpb', 'new_v_ssm_a_re', 'new_v_ssm_a_im', 'new_v_ssm_b_re', 'new_v_ssm_b_im', 'new_v_ssm_c_re', 'new_v_ssm_c_im', 'new_v_ssm_log_step', 'new_v_ssm_d', 'new_v_w_glu', 'new_v_b_glu', 'new_v_g_out_attn', 'new_v_g_out_ssm', 'new_v_w_out', 'new_v_g_ffn', 'new_v_w_ffn_gate', 'new_v_w_ffn_up', 'new_v_w_ffn_down']
TWIN_LEAF_KINDS = {'loss': 'loss', 'grad_x': 'grad_x', 'grad_g_mix': 'grad_w', 'grad_w_in': 'grad_w', 'grad_q_gain': 'grad_w', 'grad_k_gain': 'grad_w', 'grad_rpb': 'grad_w', 'grad_ssm_a_re': 'grad_w', 'grad_ssm_a_im': 'grad_w', 'grad_ssm_b_re': 'grad_w', 'grad_ssm_b_im': 'grad_w', 'grad_ssm_c_re': 'grad_w', 'grad_ssm_c_im': 'grad_w', 'grad_ssm_log_step': 'grad_w', 'grad_ssm_d': 'grad_w', 'grad_w_glu': 'grad_w', 'grad_b_glu': 'grad_w', 'grad_g_out_attn': 'grad_w', 'grad_g_out_ssm': 'grad_w', 'grad_w_out': 'grad_w', 'grad_g_ffn': 'grad_w', 'grad_w_ffn_gate': 'grad_w', 'grad_w_ffn_up': 'grad_w', 'grad_w_ffn_down': 'grad_w', 'delta_g_mix': 'delta_w', 'delta_w_in': 'delta_w', 'delta_q_gain': 'delta_w', 'delta_k_gain': 'delta_w', 'delta_rpb': 'delta_w', 'delta_ssm_a_re': 'delta_w', 'delta_ssm_a_im': 'delta_w', 'delta_ssm_b_re': 'delta_w', 'delta_ssm_b_im': 'delta_w', 'delta_ssm_c_re': 'delta_w', 'delta_ssm_c_im': 'delta_w', 'delta_ssm_log_step': 'delta_w', 'delta_ssm_d': 'delta_w', 'delta_w_glu': 'delta_w', 'delta_b_glu': 'delta_w', 'delta_g_out_attn': 'delta_w', 'delta_g_out_ssm': 'delta_w', 'delta_w_out': 'delta_w', 'delta_g_ffn': 'delta_w', 'delta_w_ffn_gate': 'delta_w', 'delta_w_ffn_up': 'delta_w', 'delta_w_ffn_down': 'delta_w', 'new_m_g_mix': 'new_m', 'new_m_w_in': 'new_m', 'new_m_q_gain': 'new_m', 'new_m_k_gain': 'new_m', 'new_m_rpb': 'new_m', 'new_m_ssm_a_re': 'new_m', 'new_m_ssm_a_im': 'new_m', 'new_m_ssm_b_re': 'new_m', 'new_m_ssm_b_im': 'new_m', 'new_m_ssm_c_re': 'new_m', 'new_m_ssm_c_im': 'new_m', 'new_m_ssm_log_step': 'new_m', 'new_m_ssm_d': 'new_m', 'new_m_w_glu': 'new_m', 'new_m_b_glu': 'new_m', 'new_m_g_out_attn': 'new_m', 'new_m_g_out_ssm': 'new_m', 'new_m_w_out': 'new_m', 'new_m_g_ffn': 'new_m', 'new_m_w_ffn_gate': 'new_m', 'new_m_w_ffn_up': 'new_m', 'new_m_w_ffn_down': 'new_m', 'new_v_g_mix': 'new_v', 'new_v_w_in': 'new_v', 'new_v_q_gain': 'new_v', 'new_v_k_gain': 'new_v', 'new_v_rpb': 'new_v', 'new_v_ssm_a_re': 'new_v', 'new_v_ssm_a_im': 'new_v', 'new_v_ssm_b_re': 'new_v', 'new_v_ssm_b_im': 'new_v', 'new_v_ssm_c_re': 'new_v', 'new_v_ssm_c_im': 'new_v', 'new_v_ssm_log_step': 'new_v', 'new_v_ssm_d': 'new_v', 'new_v_w_glu': 'new_v', 'new_v_b_glu': 'new_v', 'new_v_g_out_attn': 'new_v', 'new_v_g_out_ssm': 'new_v', 'new_v_w_out': 'new_v', 'new_v_g_ffn': 'new_v', 'new_v_w_ffn_gate': 'new_v', 'new_v_w_ffn_up': 'new_v', 'new_v_w_ffn_down': 'new_v'}


def _forward(args):
    return _fwd_reference(*[args[k] for k in FWD_PARAMS])


def _output_shape():
    def fwd():
        inp = _fwd_setup_inputs(0)
        return _fwd_reference(*[inp[k] for k in FWD_PARAMS])
    out = _jax.eval_shape(fwd)
    return out.shape, out.dtype

N_MICROBATCH = 1
ADAM_LR = 0.001
ADAM_B1 = 0.9
ADAM_B2 = 0.999
ADAM_EPS = 1e-08
ADAM_WD = 0.01
ADAM_STEP = 10
PER_EXAMPLE_BATCH_AXIS = {'x': 0, 'loss_target': 0}
SHARED_INPUTS = []
_WEIGHT_DTYPES = {'g_mix': _jnp.float32, 'w_in': _jnp.float32, 'q_gain': _jnp.float32, 'k_gain': _jnp.float32, 'rpb': _jnp.float32, 'ssm_a_re': _jnp.float32, 'ssm_a_im': _jnp.float32, 'ssm_b_re': _jnp.float32, 'ssm_b_im': _jnp.float32, 'ssm_c_re': _jnp.float32, 'ssm_c_im': _jnp.float32, 'ssm_log_step': _jnp.float32, 'ssm_d': _jnp.float32, 'w_glu': _jnp.float32, 'b_glu': _jnp.float32, 'g_out_attn': _jnp.float32, 'g_out_ssm': _jnp.float32, 'w_out': _jnp.float32, 'g_ffn': _jnp.float32, 'w_ffn_gate': _jnp.float32, 'w_ffn_up': _jnp.float32, 'w_ffn_down': _jnp.float32}
MOMENT_SCALE = {'g_mix': 4.542692e-01, 'w_in': 2.994272e-01, 'q_gain': 1.151682e+00, 'k_gain': 1.100683e+00, 'rpb': 7.970130e-02, 'ssm_a_re': 1.101914e-02, 'ssm_a_im': 1.049628e-02, 'ssm_b_re': 8.433018e-03, 'ssm_b_im': 8.469043e-03, 'ssm_c_re': 1.662058e-02, 'ssm_c_im': 1.676673e-02, 'ssm_log_step': 6.943681e+00, 'ssm_d': 3.199548e+00, 'w_glu': 4.907292e-01, 'b_glu': 1.484854e+00, 'g_out_attn': 1.598433e+01, 'g_out_ssm': 2.894271e+01, 'w_out': 1.919806e+00, 'g_ffn': 1.256931e+01, 'w_ffn_gate': 3.842876e-01, 'w_ffn_up': 2.173015e-01, 'w_ffn_down': 3.034643e-01}


def _to_microbatches(a, axis):
    t = _jnp.moveaxis(a, axis, 0)
    t = t.reshape((N_MICROBATCH, t.shape[0] // N_MICROBATCH) + t.shape[1:])
    return _jnp.moveaxis(t, 1, axis + 1)


def setup_inputs(seed: int = 0) -> dict:
    inp = _fwd_setup_inputs(seed)
    key = _jax.random.fold_in(_jax.random.key(seed), 7919)
    shape, _ = _output_shape()
    out = dict(inp)
    out["loss_target"] = _jax.random.normal(_jax.random.fold_in(key, 0), shape, _jnp.float32)
    for i, name in enumerate(TWIN_WEIGHTS):
        w = inp[name].astype(_jnp.float32)
        if MOMENT_SCALE is None:
            s = _jnp.sqrt(_jnp.mean(_jnp.square(w)) + 1e-30)
        else:
            s = MOMENT_SCALE[name]
        km, kv = _jax.random.split(_jax.random.fold_in(key, i + 1))
        out[name] = w
        out["m_" + name] = s * _jax.random.normal(km, w.shape, _jnp.float32)
        out["v_" + name] = (s * s) * _jax.random.uniform(kv, w.shape, _jnp.float32, 0.5, 1.5)
    if N_MICROBATCH > 1:
        for name, axis in PER_EXAMPLE_BATCH_AXIS.items():
            out[name] = _to_microbatches(out[name], axis)
    return {'x': out['x'], 'g_mix': out['g_mix'], 'w_in': out['w_in'], 'q_gain': out['q_gain'], 'k_gain': out['k_gain'], 'rpb': out['rpb'], 'ssm_a_re': out['ssm_a_re'], 'ssm_a_im': out['ssm_a_im'], 'ssm_b_re': out['ssm_b_re'], 'ssm_b_im': out['ssm_b_im'], 'ssm_c_re': out['ssm_c_re'], 'ssm_c_im': out['ssm_c_im'], 'ssm_log_step': out['ssm_log_step'], 'ssm_d': out['ssm_d'], 'w_glu': out['w_glu'], 'b_glu': out['b_glu'], 'g_out_attn': out['g_out_attn'], 'g_out_ssm': out['g_out_ssm'], 'w_out': out['w_out'], 'g_ffn': out['g_ffn'], 'w_ffn_gate': out['w_ffn_gate'], 'w_ffn_up': out['w_ffn_up'], 'w_ffn_down': out['w_ffn_down'], 'loss_target': out['loss_target'], 'm_g_mix': out['m_g_mix'], 'm_w_in': out['m_w_in'], 'm_q_gain': out['m_q_gain'], 'm_k_gain': out['m_k_gain'], 'm_rpb': out['m_rpb'], 'm_ssm_a_re': out['m_ssm_a_re'], 'm_ssm_a_im': out['m_ssm_a_im'], 'm_ssm_b_re': out['m_ssm_b_re'], 'm_ssm_b_im': out['m_ssm_b_im'], 'm_ssm_c_re': out['m_ssm_c_re'], 'm_ssm_c_im': out['m_ssm_c_im'], 'm_ssm_log_step': out['m_ssm_log_step'], 'm_ssm_d': out['m_ssm_d'], 'm_w_glu': out['m_w_glu'], 'm_b_glu': out['m_b_glu'], 'm_g_out_attn': out['m_g_out_attn'], 'm_g_out_ssm': out['m_g_out_ssm'], 'm_w_out': out['m_w_out'], 'm_g_ffn': out['m_g_ffn'], 'm_w_ffn_gate': out['m_w_ffn_gate'], 'm_w_ffn_up': out['m_w_ffn_up'], 'm_w_ffn_down': out['m_w_ffn_down'], 'v_g_mix': out['v_g_mix'], 'v_w_in': out['v_w_in'], 'v_q_gain': out['v_q_gain'], 'v_k_gain': out['v_k_gain'], 'v_rpb': out['v_rpb'], 'v_ssm_a_re': out['v_ssm_a_re'], 'v_ssm_a_im': out['v_ssm_a_im'], 'v_ssm_b_re': out['v_ssm_b_re'], 'v_ssm_b_im': out['v_ssm_b_im'], 'v_ssm_c_re': out['v_ssm_c_re'], 'v_ssm_c_im': out['v_ssm_c_im'], 'v_ssm_log_step': out['v_ssm_log_step'], 'v_ssm_d': out['v_ssm_d'], 'v_w_glu': out['v_w_glu'], 'v_b_glu': out['v_b_glu'], 'v_g_out_attn': out['v_g_out_attn'], 'v_g_out_ssm': out['v_g_out_ssm'], 'v_w_out': out['v_w_out'], 'v_g_ffn': out['v_g_ffn'], 'v_w_ffn_gate': out['v_w_ffn_gate'], 'v_w_ffn_up': out['v_w_ffn_up'], 'v_w_ffn_down': out['v_w_ffn_down']}


def _loss(weights, diff, rest, loss_target):
    with _jax.named_scope("forward"):
        args = {**rest, TWIN_DIFF_INPUT: diff, **{k: w.astype(_WEIGHT_DTYPES[k]) for k, w in weights.items()}}
        y = _forward(args)
    with _jax.named_scope("loss_head"):
        err = _jnp.square(y.astype(_jnp.float32) - loss_target)
        return 0.5 * _jnp.sum(_jnp.mean(err, axis=-1)) if err.ndim else 0.5 * err


def _adamw(w, g, m, v):
    m = ADAM_B1 * m + (1.0 - ADAM_B1) * g
    v = ADAM_B2 * v + (1.0 - ADAM_B2) * _jnp.square(g)
    m_hat = m / (1.0 - ADAM_B1 ** ADAM_STEP)
    v_hat = v / (1.0 - ADAM_B2 ** ADAM_STEP)
    delta = -ADAM_LR * (m_hat / (_jnp.sqrt(v_hat) + ADAM_EPS) + ADAM_WD * w)
    return delta, m, v


def reference(x, g_mix, w_in, q_gain, k_gain, rpb, ssm_a_re, ssm_a_im, ssm_b_re, ssm_b_im, ssm_c_re, ssm_c_im, ssm_log_step, ssm_d, w_glu, b_glu, g_out_attn, g_out_ssm, w_out, g_ffn, w_ffn_gate, w_ffn_up, w_ffn_down, loss_target, m_g_mix, m_w_in, m_q_gain, m_k_gain, m_rpb, m_ssm_a_re, m_ssm_a_im, m_ssm_b_re, m_ssm_b_im, m_ssm_c_re, m_ssm_c_im, m_ssm_log_step, m_ssm_d, m_w_glu, m_b_glu, m_g_out_attn, m_g_out_ssm, m_w_out, m_g_ffn, m_w_ffn_gate, m_w_ffn_up, m_w_ffn_down, v_g_mix, v_w_in, v_q_gain, v_k_gain, v_rpb, v_ssm_a_re, v_ssm_a_im, v_ssm_b_re, v_ssm_b_im, v_ssm_c_re, v_ssm_c_im, v_ssm_log_step, v_ssm_d, v_w_glu, v_b_glu, v_g_out_attn, v_g_out_ssm, v_w_out, v_g_ffn, v_w_ffn_gate, v_w_ffn_up, v_w_ffn_down):
    given = dict(x=x, g_mix=g_mix, w_in=w_in, q_gain=q_gain, k_gain=k_gain, rpb=rpb, ssm_a_re=ssm_a_re, ssm_a_im=ssm_a_im, ssm_b_re=ssm_b_re, ssm_b_im=ssm_b_im, ssm_c_re=ssm_c_re, ssm_c_im=ssm_c_im, ssm_log_step=ssm_log_step, ssm_d=ssm_d, w_glu=w_glu, b_glu=b_glu, g_out_attn=g_out_attn, g_out_ssm=g_out_ssm, w_out=w_out, g_ffn=g_ffn, w_ffn_gate=w_ffn_gate, w_ffn_up=w_ffn_up, w_ffn_down=w_ffn_down, loss_target=loss_target, m_g_mix=m_g_mix, m_w_in=m_w_in, m_q_gain=m_q_gain, m_k_gain=m_k_gain, m_rpb=m_rpb, m_ssm_a_re=m_ssm_a_re, m_ssm_a_im=m_ssm_a_im, m_ssm_b_re=m_ssm_b_re, m_ssm_b_im=m_ssm_b_im, m_ssm_c_re=m_ssm_c_re, m_ssm_c_im=m_ssm_c_im, m_ssm_log_step=m_ssm_log_step, m_ssm_d=m_ssm_d, m_w_glu=m_w_glu, m_b_glu=m_b_glu, m_g_out_attn=m_g_out_attn, m_g_out_ssm=m_g_out_ssm, m_w_out=m_w_out, m_g_ffn=m_g_ffn, m_w_ffn_gate=m_w_ffn_gate, m_w_ffn_up=m_w_ffn_up, m_w_ffn_down=m_w_ffn_down, v_g_mix=v_g_mix, v_w_in=v_w_in, v_q_gain=v_q_gain, v_k_gain=v_k_gain, v_rpb=v_rpb, v_ssm_a_re=v_ssm_a_re, v_ssm_a_im=v_ssm_a_im, v_ssm_b_re=v_ssm_b_re, v_ssm_b_im=v_ssm_b_im, v_ssm_c_re=v_ssm_c_re, v_ssm_c_im=v_ssm_c_im, v_ssm_log_step=v_ssm_log_step, v_ssm_d=v_ssm_d, v_w_glu=v_w_glu, v_b_glu=v_b_glu, v_g_out_attn=v_g_out_attn, v_g_out_ssm=v_g_out_ssm, v_w_out=v_w_out, v_g_ffn=v_g_ffn, v_w_ffn_gate=v_w_ffn_gate, v_w_ffn_up=v_w_ffn_up, v_w_ffn_down=v_w_ffn_down)
    weights = {n: given[n] for n in TWIN_WEIGHTS}
    shared = {n: given[n] for n in SHARED_INPUTS}
    per_example = {n: given[n] for n in ['x']}
    grad_fn = _jax.value_and_grad(_loss, argnums=(0, 1))

    def one_microbatch(ex, loss_target):
        ex = dict(ex)
        diff = ex.pop(TWIN_DIFF_INPUT)
        return grad_fn(weights, diff, {**shared, **ex}, loss_target)

    if N_MICROBATCH == 1:
        loss, (grad_w, grad_x) = one_microbatch(per_example, given["loss_target"])
    else:
        def body(carry, xs):
            loss_sum, grad_sum = carry
            l_k, (gw_k, gx_k) = one_microbatch(xs[0], xs[1])
            with _jax.named_scope("update"):
                return (loss_sum + l_k, _jax.tree.map(_jnp.add, grad_sum, gw_k)), gx_k

        init = (_jnp.zeros((), _jnp.float32), _jax.tree.map(_jnp.zeros_like, weights))
        (loss, grad_w), grad_x = _jax.lax.scan(body, init, (per_example, given["loss_target"]))
    with _jax.named_scope("update"):
        delta_w, new_m, new_v = {}, {}, {}
        for n in TWIN_WEIGHTS:
            delta_w[n], new_m[n], new_v[n] = _adamw(weights[n], grad_w[n], given["m_" + n], given["v_" + n])
    return (loss, grad_x, *[grad_w[n] for n in TWIN_WEIGHTS], *[delta_w[n] for n in TWIN_WEIGHTS],
            *[new_m[n] for n in TWIN_WEIGHTS], *[new_v[n] for n in TWIN_WEIGHTS])
```

```python
import functools
import math

import jax
import jax.numpy as jnp
from jax import lax
from jax.experimental import pallas as pl
from jax.experimental.pallas import tpu as pltpu

F32 = jnp.float32
BF16 = jnp.bfloat16
MXU_DTYPE = BF16
HI = lax.Precision.HIGHEST
VMEM_LIMIT_V7X = 56 * 1024 * 1024
LANES = 128
SUBLANES = 8

GRID_W = 64
WIN_H = 8
WIN_W = 16
HEAD_DIM = 64
SSM_GROUP_CH = 16
SSM_STATE = 64
S5_CHUNK = 16
RMS_EPS = 1e-6
NEG_INF = -1e30
N_CHIPS = 4
MESH = pl.DeviceIdType.MESH

ADAM_LR = 0.001
ADAM_B1 = 0.9
ADAM_B2 = 0.999
ADAM_EPS = 1e-08
ADAM_WD = 0.01
ADAM_STEP = 10

WEIGHT_NAMES = ['g_mix', 'w_in', 'q_gain', 'k_gain', 'rpb', 'ssm_a_re', 'ssm_a_im', 'ssm_b_re', 'ssm_b_im',
                'ssm_c_re', 'ssm_c_im', 'ssm_log_step', 'ssm_d', 'w_glu', 'b_glu', 'g_out_attn', 'g_out_ssm',
                'w_out', 'g_ffn', 'w_ffn_gate', 'w_ffn_up', 'w_ffn_down']
BIG_NAMES = ['w_in', 'w_glu', 'w_out', 'w_ffn_gate', 'w_ffn_up', 'w_ffn_down']
SMALL_NAMES = [n for n in WEIGHT_NAMES if n not in BIG_NAMES]


def _cparams(sem):
    return pltpu.CompilerParams(dimension_semantics=sem, vmem_limit_bytes=VMEM_LIMIT_V7X)


def _tile(n, want):
    if n <= want:
        return n
    t = (want // LANES) * LANES
    while t >= LANES:
        if n % t == 0:
            return t
        t -= LANES
    return n


def _mm(name, a, b, *, contract, a_mode='2', b_mode='2', o_mode='2', out_dtype=F32, add=None, exact=False,
        tm=1024, tn=1024, tk=512):
    dn = {'nn': (((1,), (0,)), ((), ())), 'nt': (((1,), (1,)), ((), ())), 'tn': (((0,), (0,)), ((), ()))}[contract]
    ar, ac = a.shape[-2:]
    br, bc = b.shape[-2:]
    m, kdim = (ar, ac) if contract != 'tn' else (ac, ar)
    n = bc if contract != 'nt' else br
    assert kdim == (br if contract != 'nt' else bc), (name, a.shape, b.shape)
    nbatch = 1
    for arr, mode in ((a, a_mode), (b, b_mode)):
        if mode == 'b':
            nbatch = arr.shape[0]
    nstack = 1
    for arr, mode in ((a, a_mode), (b, b_mode)):
        if mode == 'c':
            nstack = arr.shape[0]
    tm, tn, tk = _tile(m, tm), _tile(n, tn), _tile(kdim, tk)
    nkin = kdim // tk
    nk = nstack * nkin
    grid = (nbatch, m // tm, n // tn, nk)

    def spec(mode, block, rc):
        def imap(s, i, j, kk):
            r, c = rc(i, j, kk % nkin)
            if mode == '2':
                return (r, c)
            return (s if mode == 'b' else kk // nkin, r, c)
        return pl.BlockSpec(block if mode == '2' else (None,) + block, imap)

    a_spec = spec(a_mode, (tm, tk) if contract != 'tn' else (tk, tm),
                  (lambda i, j, k: (i, k)) if contract != 'tn' else (lambda i, j, k: (k, i)))
    b_spec = spec(b_mode, (tk, tn) if contract != 'nt' else (tn, tk),
                  (lambda i, j, k: (k, j)) if contract != 'nt' else (lambda i, j, k: (j, k)))
    o_spec = spec(o_mode, (tm, tn), lambda i, j, k: (i, j))
    out_shape = (m, n) if o_mode == '2' else (nbatch, m, n)
    has_add = add is not None

    def body(*refs):
        if has_add:
            a_ref, b_ref, add_ref, o_ref, acc_ref = refs
        else:
            a_ref, b_ref, o_ref, acc_ref = refs
        k = pl.program_id(3)

        @pl.when(k == 0)
        def _():
            acc_ref[...] = jnp.zeros_like(acc_ref)

        if exact:
            acc_ref[...] += lax.dot_general(a_ref[...].astype(F32), b_ref[...].astype(F32), dn, precision=HI,
                                            preferred_element_type=F32)
        else:
            acc_ref[...] += lax.dot_general(a_ref[...].astype(MXU_DTYPE), b_ref[...].astype(MXU_DTYPE), dn,
                                            preferred_element_type=F32)

        @pl.when(k == nk - 1)
        def _():
            r = acc_ref[...]
            if has_add:
                r = r + add_ref[...].astype(F32)
            o_ref[...] = r.astype(o_ref.dtype)

    in_specs = [a_spec, b_spec] + ([o_spec] if has_add else [])
    args = (a, b) + ((add,) if has_add else ())
    return pl.pallas_call(
        body, name=name, grid=grid, in_specs=in_specs, out_specs=o_spec,
        out_shape=jax.ShapeDtypeStruct(out_shape, out_dtype),
        scratch_shapes=[pltpu.VMEM((tm, tn), F32)],
        compiler_params=_cparams(("parallel", "parallel", "parallel", "arbitrary")),
    )(*args)


def _ew(name, fn, ins, outs, tr=256):
    rows = next(x[1].shape[0] for x in ins if x[0] == 'r')
    tr = min(tr, rows)
    assert rows % tr == 0 and tr % SUBLANES == 0, (name, rows, tr)
    in_specs, args = [], []
    for x in ins:
        if x[0] == 'r' and len(x) == 2:
            in_specs.append(pl.BlockSpec((tr, x[1].shape[1]), lambda i: (i, 0)))
        elif x[0] == 'r':
            in_specs.append(pl.BlockSpec((tr, x[3]), functools.partial(lambda cb, i: (i, cb), x[2])))
        else:
            in_specs.append(pl.BlockSpec(x[1].shape, lambda i: (0, 0)))
        args.append(x[1])
    out_specs, out_shapes = [], []
    for o in outs:
        if o[0] == 'r':
            out_specs.append(pl.BlockSpec((tr, o[1]), lambda i: (i, 0)))
            out_shapes.append(jax.ShapeDtypeStruct((rows, o[1]), o[2]))
        else:
            out_specs.append(pl.BlockSpec((SUBLANES, o[1]), lambda i: (0, 0)))
            out_shapes.append(jax.ShapeDtypeStruct((SUBLANES, o[1]), F32))
    nin = len(ins)
    has_acc = any(o[0] == 'a' for o in outs)

    def body(*refs):
        vals = fn(*[r[...] for r in refs[:nin]])
        if not isinstance(vals, (tuple, list)):
            vals = (vals,)
        i = pl.program_id(0)
        for o, ref, v in zip(outs, refs[nin:], vals):
            if o[0] == 'r':
                ref[...] = v.astype(ref.dtype)
            else:
                part = v.astype(F32).reshape(tr // SUBLANES, SUBLANES, o[1]).sum(axis=0)

                @pl.when(i == 0)
                def _(ref=ref, part=part):
                    ref[...] = part

                @pl.when(i > 0)
                def _(ref=ref, part=part):
                    ref[...] += part

    res = pl.pallas_call(
        body, name=name, grid=(rows // tr,), in_specs=in_specs, out_specs=out_specs, out_shape=out_shapes,
        compiler_params=_cparams(("arbitrary",) if has_acc else ("parallel",)),
    )(*args)
    return res


def _rms(x, g):
    r = lax.rsqrt(jnp.mean(x * x, axis=-1, keepdims=True) + RMS_EPS)
    xr = x * r
    return xr * g, xr


def _rms_bwd(x, g, dy):
    r = lax.rsqrt(jnp.mean(x * x, axis=-1, keepdims=True) + RMS_EPS)
    xr = x * r
    gdy = g * dy
    dx = r * (gdy - xr * jnp.mean(xr * gdy, axis=-1, keepdims=True))
    return dx, dy * xr


def _sigmoid(x):
    return 1.0 / (1.0 + jnp.exp(-x))


_GELU_C = math.sqrt(2.0 / math.pi)


def _gelu(x):
    return 0.5 * x * (1.0 + jnp.tanh(_GELU_C * (x + 0.044715 * x * x * x)))


def _gelu_grad(x):
    t = jnp.tanh(_GELU_C * (x + 0.044715 * x * x * x))
    return 0.5 * (1.0 + t) + 0.5 * x * (1.0 - t * t) * _GELU_C * (1.0 + 3 * 0.044715 * x * x)


def _attn_geometry(r, rows):
    row_start = jnp.clip(r - WIN_H // 2, 0, rows - WIN_H)
    key0 = pl.multiple_of(row_start * GRID_W, GRID_W)
    bias0 = pl.multiple_of((row_start - r + (WIN_H - 1)) * GRID_W, GRID_W)
    return key0, bias0


def _bias_table(rpb):
    c = jnp.arange(GRID_W)
    col_start = jnp.clip(c - WIN_W // 2, 0, GRID_W - WIN_W)
    col_in = (c[None, :] >= col_start[:, None]) & (c[None, :] < col_start[:, None] + WIN_W)
    dc = jnp.clip(c[None, :] - c[:, None], -(WIN_W - 1), WIN_W - 1) + (WIN_W - 1)
    tab = rpb[:, :, dc]
    tab = jnp.where(col_in[None, None], tab, NEG_INF)
    return tab.transpose(0, 1, 3, 2).reshape(rpb.shape[0], (2 * WIN_H - 1) * GRID_W, GRID_W)


def _bias_table_grad(dtab):
    h = dtab.shape[0]
    c = jnp.arange(GRID_W)
    col_start = jnp.clip(c - WIN_W // 2, 0, GRID_W - WIN_W)
    col_in = (c[None, :] >= col_start[:, None]) & (c[None, :] < col_start[:, None] + WIN_W)
    dc = jnp.clip(c[None, :] - c[:, None], -(WIN_W - 1), WIN_W - 1) + (WIN_W - 1)
    onehot = ((dc[:, :, None] == jnp.arange(2 * WIN_W - 1)[None, None, :]) & col_in[:, :, None]).astype(F32)
    d = dtab.reshape(h, 2 * WIN_H - 1, GRID_W, GRID_W)
    return jnp.einsum('hrkq,qkd->hrd', d, onehot, precision=HI)


def _attn_scores(qn, kb, bias):
    st = lax.dot_general(kb.astype(MXU_DTYPE), qn.astype(MXU_DTYPE), (((1,), (1,)), ((), ())),
                         preferred_element_type=F32)
    st = st * (1.0 / math.sqrt(HEAD_DIM)) + bias
    mx = jnp.max(st, axis=0, keepdims=True)
    p = jnp.exp(st - mx)
    return p / jnp.sum(p, axis=0, keepdims=True)


def _attn_fwd(z4, qg2, kg2, bias_t):
    _, t, aw = z4.shape
    rows = t // GRID_W
    npair = aw // (2 * HEAD_DIM)
    nkeys = WIN_H * GRID_W
    nb = bias_t.shape[1]

    def body(q_ref, k_ref, v_ref, qg_ref, kg_ref, b_ref, o_ref, kn_ref):
        r = pl.program_id(1)

        @pl.when(r == 0)
        def _():
            for e in range(2):
                sl = slice(e * HEAD_DIM, (e + 1) * HEAD_DIM)
                kn_ref[e] = _rms(k_ref[:, sl], kg_ref[:, sl])[0]

        key0, bias0 = _attn_geometry(r, rows)
        outs = []
        for e in range(2):
            sl = slice(e * HEAD_DIM, (e + 1) * HEAD_DIM)
            qn = _rms(q_ref[:, sl], qg_ref[:, sl])[0]
            kb = kn_ref[e, pl.ds(key0, nkeys), :]
            vb = v_ref[pl.ds(key0, nkeys), sl]
            pt = _attn_scores(qn, kb, b_ref[e, pl.ds(bias0, nkeys), :])
            outs.append(lax.dot_general(pt.astype(MXU_DTYPE), vb.astype(MXU_DTYPE), (((0,), (0,)), ((), ())),
                                        preferred_element_type=F32))
        o_ref[...] = jnp.concatenate(outs, axis=1)

    return pl.pallas_call(
        body, name="attn_fwd", grid=(npair, rows),
        in_specs=[pl.BlockSpec((None, GRID_W, 2 * HEAD_DIM), lambda p, r: (0, r, p)),
                  pl.BlockSpec((None, t, 2 * HEAD_DIM), lambda p, r: (1, 0, p)),
                  pl.BlockSpec((None, t, 2 * HEAD_DIM), lambda p, r: (2, 0, p)),
                  pl.BlockSpec((1, 2 * HEAD_DIM), lambda p, r: (0, 0)),
                  pl.BlockSpec((1, 2 * HEAD_DIM), lambda p, r: (0, 0)),
                  pl.BlockSpec((2, nb, GRID_W), lambda p, r: (p, 0, 0))],
        out_specs=pl.BlockSpec((GRID_W, 2 * HEAD_DIM), lambda p, r: (r, p)),
        out_shape=jax.ShapeDtypeStruct((t, aw), F32),
        scratch_shapes=[pltpu.VMEM((2, t, HEAD_DIM), F32)],
        compiler_params=_cparams(("parallel", "arbitrary")),
    )(z4, z4, z4, qg2, kg2, bias_t)


def _attn_bwd(z4, qg2, kg2, bias_t, dya):
    _, t, aw = z4.shape
    rows = t // GRID_W
    npair = aw // (2 * HEAD_DIM)
    nkeys = WIN_H * GRID_W
    nb = bias_t.shape[1]
    scale = 1.0 / math.sqrt(HEAD_DIM)

    def body(q_ref, k_ref, v_ref, qg_ref, kg_ref, b_ref, do_ref,
             dq_ref, dk_ref, dv_ref, db_ref, dqg_ref, dkg_ref, kn_ref, dkn_ref, dva_ref):
        r = pl.program_id(1)

        @pl.when(r == 0)
        def _():
            for e in range(2):
                sl = slice(e * HEAD_DIM, (e + 1) * HEAD_DIM)
                kn_ref[e] = _rms(k_ref[:, sl], kg_ref[:, sl])[0]
            dkn_ref[...] = jnp.zeros_like(dkn_ref)
            dva_ref[...] = jnp.zeros_like(dva_ref)
            db_ref[...] = jnp.zeros_like(db_ref)
            dqg_ref[...] = jnp.zeros_like(dqg_ref)

        key0, bias0 = _attn_geometry(r, rows)
        dqs, dqgs = [], []
        for e in range(2):
            sl = slice(e * HEAD_DIM, (e + 1) * HEAD_DIM)
            q = q_ref[:, sl]
            qn = _rms(q, qg_ref[:, sl])[0]
            kb = kn_ref[e, pl.ds(key0, nkeys), :]
            vb = v_ref[pl.ds(key0, nkeys), sl]
            do = do_ref[:, sl]
            pt = _attn_scores(qn, kb, b_ref[e, pl.ds(bias0, nkeys), :])
            dva_ref[e, pl.ds(key0, nkeys), :] += lax.dot_general(
                pt.astype(MXU_DTYPE), do.astype(MXU_DTYPE), (((1,), (0,)), ((), ())), preferred_element_type=F32)
            dpt = lax.dot_general(vb.astype(MXU_DTYPE), do.astype(MXU_DTYPE), (((1,), (1,)), ((), ())),
                                  preferred_element_type=F32)
            dst = pt * (dpt - jnp.sum(pt * dpt, axis=0, keepdims=True))
            db_ref[e, pl.ds(bias0, nkeys), :] += dst
            dsb = dst.astype(MXU_DTYPE)
            dqn = scale * lax.dot_general(dsb, kb.astype(MXU_DTYPE), (((0,), (0,)), ((), ())),
                                          preferred_element_type=F32)
            dkn_ref[e, pl.ds(key0, nkeys), :] += scale * lax.dot_general(
                dsb, qn.astype(MXU_DTYPE), (((1,), (0,)), ((), ())), preferred_element_type=F32)
            dq, dqg = _rms_bwd(q, qg_ref[:, sl], dqn)
            dqs.append(dq)
            dqgs.append(jnp.sum(dqg, axis=0, keepdims=True))
        dq_ref[...] = jnp.concatenate(dqs, axis=1)
        dqg_ref[...] += jnp.concatenate(dqgs, axis=1)

        @pl.when(r == rows - 1)
        def _():
            dks, dkgs = [], []
            for e in range(2):
                sl = slice(e * HEAD_DIM, (e + 1) * HEAD_DIM)
                dk, dkg = _rms_bwd(k_ref[:, sl], kg_ref[:, sl], dkn_ref[e])
                dks.append(dk)
                dkgs.append(jnp.sum(dkg, axis=0, keepdims=True))
            dk_ref[...] = jnp.concatenate(dks, axis=1)
            dv_ref[...] = jnp.concatenate([dva_ref[0], dva_ref[1]], axis=1)
            dkg_ref[...] = jnp.concatenate(dkgs, axis=1)

    pair_vec = pl.BlockSpec((None, 1, 2 * HEAD_DIM), lambda p, r: (p, 0, 0))
    return pl.pallas_call(
        body, name="attn_bwd", grid=(npair, rows),
        in_specs=[pl.BlockSpec((None, GRID_W, 2 * HEAD_DIM), lambda p, r: (0, r, p)),
                  pl.BlockSpec((None, t, 2 * HEAD_DIM), lambda p, r: (1, 0, p)),
                  pl.BlockSpec((None, t, 2 * HEAD_DIM), lambda p, r: (2, 0, p)),
                  pl.BlockSpec((1, 2 * HEAD_DIM), lambda p, r: (0, 0)),
                  pl.BlockSpec((1, 2 * HEAD_DIM), lambda p, r: (0, 0)),
                  pl.BlockSpec((2, nb, GRID_W), lambda p, r: (p, 0, 0)),
                  pl.BlockSpec((GRID_W, 2 * HEAD_DIM), lambda p, r: (r, p))],
        out_specs=[pl.BlockSpec((GRID_W, 2 * HEAD_DIM), lambda p, r: (r, p)),
                   pl.BlockSpec((t, 2 * HEAD_DIM), lambda p, r: (0, p)),
                   pl.BlockSpec((t, 2 * HEAD_DIM), lambda p, r: (0, p)),
                   pl.BlockSpec((2, nb, GRID_W), lambda p, r: (p, 0, 0)),
                   pair_vec, pair_vec],
        out_shape=[jax.ShapeDtypeStruct((t, aw), F32), jax.ShapeDtypeStruct((t, aw), F32),
                   jax.ShapeDtypeStruct((t, aw), F32), jax.ShapeDtypeStruct(bias_t.shape, F32),
                   jax.ShapeDtypeStruct((npair, 1, 2 * HEAD_DIM), F32),
                   jax.ShapeDtypeStruct((npair, 1, 2 * HEAD_DIM), F32)],
        scratch_shapes=[pltpu.VMEM((2, t, HEAD_DIM), F32), pltpu.VMEM((2, t, HEAD_DIM), F32),
                        pltpu.VMEM((2, t, HEAD_DIM), F32)],
        compiler_params=_cparams(("parallel", "arbitrary")),
    )(z4, z4, z4, qg2, kg2, bias_t, dya)


def _s5_mats(a_re, a_im, b_re, b_im, c_re, c_im, log_step, d_skip):
    nd, g, p = a_re.shape
    c = b_re.shape[-1]
    L = S5_CHUNK
    lr = jnp.minimum(a_re, -1e-4)
    li = a_im
    dt = jnp.exp(log_step)[..., None]
    n = jnp.arange(L + 1, dtype=F32)[:, None, None, None]
    mag = jnp.exp(n * (lr * dt)[None])
    ang = n * (li * dt)[None]
    pw_r, pw_i = mag * jnp.cos(ang), mag * jnp.sin(ang)
    den = lr * lr + li * li
    nr, ni = pw_r[1] - 1.0, pw_i[1]
    cr, ci = (nr * lr + ni * li) / den, (ni * lr - nr * li) / den
    bb_r = cr[..., None] * b_re - ci[..., None] * b_im
    bb_i = cr[..., None] * b_im + ci[..., None] * b_re
    pb_r = pw_r[..., None] * bb_r[None] - pw_i[..., None] * bb_i[None]
    pb_i = pw_r[..., None] * bb_i[None] + pw_i[..., None] * bb_r[None]
    kern = (jnp.einsum('dgop,ndgpi->ndgoi', c_re, pb_r[:L], precision=HI)
            - jnp.einsum('dgop,ndgpi->ndgoi', c_im, pb_i[:L], precision=HI))
    j = jnp.arange(L)
    e0 = ((j[None, None, :] - j[None, :, None]) == j[:, None, None]).astype(F32)
    e1 = ((j[None, :, None] - j[None, None, :]) == j[:, None, None]).astype(F32)
    mt = (jnp.einsum('njt,ngoi->gjito', e0, kern[:, 0], precision=HI)
          + jnp.einsum('njt,ngoi->gjito', e1, kern[:, 1], precision=HI))
    eye = jnp.eye(L, dtype=F32)[:, None, :, None] * jnp.eye(c, dtype=F32)[None, :, None, :]
    mt = mt + d_skip.reshape(g, 1, c, 1, 1) * eye[None]
    mt = mt.reshape(g, L * c, L * c)

    def rows_jc(x):
        return x.transpose(1, 0, 3, 2).reshape(g, L * c, p)
    ws = jnp.concatenate([rows_jc(pb_r[:L, 0][::-1]), rows_jc(pb_i[:L, 0][::-1]),
                          rows_jc(pb_r[:L, 1]), rows_jc(pb_i[:L, 1])], axis=-1)
    cp_r = c_re[None] * pw_r[:, :, :, None, :] - c_im[None] * pw_i[:, :, :, None, :]
    cp_i = c_re[None] * pw_i[:, :, :, None, :] + c_im[None] * pw_r[:, :, :, None, :]

    def cols_to(x):
        return x.transpose(1, 3, 0, 2).reshape(g, p, L * c)
    wo = jnp.concatenate([cols_to(cp_r[1:, 0]), cols_to(-cp_i[1:, 0]),
                          cols_to(cp_r[1:, 1][::-1]), cols_to(-cp_i[1:, 1][::-1])], axis=1)
    lr16, li16 = pw_r[L], pw_i[L]
    fa = jnp.concatenate([lr16[0], lr16[0], lr16[1], lr16[1]], axis=-1)
    fb = jnp.concatenate([-li16[0], li16[0], -li16[1], li16[1]], axis=-1)
    return mt, ws, wo, fa, fb


def _gmm(name, a, b, contract, a_stacked=False, b_stacked=False, o_stacked=False, add=None):
    w = S5_CHUNK * SSM_GROUP_CH
    g = (a.shape[0] if a_stacked else a.shape[1] // w)
    dn = {'nn': (((1,), (0,)), ((), ())), 'nt': (((1,), (1,)), ((), ())), 'tn': (((0,), (0,)), ((), ()))}[contract]

    def spec(arr, stacked):
        if stacked:
            return pl.BlockSpec((None,) + arr.shape[1:], lambda i: (i, 0, 0))
        return pl.BlockSpec((arr.shape[0], w), lambda i: (0, i))

    m = (a.shape[1] if a_stacked else a.shape[0]) if contract != 'tn' else w
    n = w
    if o_stacked:
        o_spec = pl.BlockSpec((None, m, n), lambda i: (i, 0, 0))
        o_shape = (g, m, n)
    else:
        o_spec = pl.BlockSpec((m, n), lambda i: (0, i))
        o_shape = (m, g * n)
    has_add = add is not None

    def body(*refs):
        if has_add:
            a_ref, b_ref, add_ref, o_ref = refs
        else:
            a_ref, b_ref, o_ref = refs
        r = lax.dot_general(a_ref[...], b_ref[...], dn, precision=HI, preferred_element_type=F32)
        if has_add:
            r = r + add_ref[...]
        o_ref[...] = r

    in_specs = [spec(a, a_stacked), spec(b, b_stacked)] + ([o_spec] if has_add else [])
    return pl.pallas_call(
        body, name=name, grid=(g,), in_specs=in_specs, out_specs=o_spec,
        out_shape=jax.ShapeDtypeStruct(o_shape, F32), compiler_params=_cparams(("parallel",)),
    )(*((a, b) + ((add,) if has_add else ())))


def _s5_scan(name, s, fa, fb, rev0, xin=None):
    nk, g, w = s.shape
    hw = w // 2
    gb = min(g, 16)
    with_acc = xin is not None

    def body(*refs):
        if with_acc:
            s_ref, a_ref, b_ref, x_ref, o_ref, pa_ref, pb_ref = refs
        else:
            s_ref, a_ref, b_ref, o_ref = refs
        fa0, fb0, fa1, fb1 = a_ref[:, :hw], b_ref[:, :hw], a_ref[:, hw:], b_ref[:, hw:]

        def step(i, carry):
            x0, x1, pa0, pb0, pa1, pb1 = carry
            k0 = (nk - 1 - i) if rev0 else i
            k1 = i if rev0 else (nk - 1 - i)
            o_ref[k0, :, :hw] = x0
            o_ref[k1, :, hw:] = x1
            if with_acc:
                xi0, xi1 = x_ref[k0, :, :hw], x_ref[k1, :, hw:]
                pa0 = pa0 + x0 * xi0
                pb0 = pb0 + x0 * pltpu.roll(xi0, hw // 2, 1)
                pa1 = pa1 + x1 * xi1
                pb1 = pb1 + x1 * pltpu.roll(xi1, hw // 2, 1)
            x0 = fa0 * x0 + fb0 * pltpu.roll(x0, hw // 2, 1) + s_ref[k0, :, :hw]
            x1 = fa1 * x1 + fb1 * pltpu.roll(x1, hw // 2, 1) + s_ref[k1, :, hw:]
            return x0, x1, pa0, pb0, pa1, pb1

        z = jnp.zeros((gb, hw), F32)
        res = lax.fori_loop(0, nk, step, (z, z, z, z, z, z))
        if with_acc:
            pa_ref[:, :hw] = res[2]
            pb_ref[:, :hw] = res[3]
            pa_ref[:, hw:] = res[4]
            pb_ref[:, hw:] = res[5]

    seq = pl.BlockSpec((nk, gb, w), lambda i: (0, i, 0))
    vec = pl.BlockSpec((gb, w), lambda i: (i, 0))
    in_specs = [seq, vec, vec] + ([seq] if with_acc else [])
    out_specs = [seq] + ([vec, vec] if with_acc else [])
    out_shape = [jax.ShapeDtypeStruct((nk, g, w), F32)] + (
        [jax.ShapeDtypeStruct((g, w), F32)] * 2 if with_acc else [])
    return pl.pallas_call(
        body, name=name, grid=(g // gb,), in_specs=in_specs, out_specs=out_specs, out_shape=out_shape,
        compiler_params=_cparams(("parallel",)),
    )(*((s, fa, fb) + ((xin,) if with_acc else ())))


def _to_groups(u):
    t, sw = u.shape
    g = sw // SSM_GROUP_CH
    return u.reshape(t // S5_CHUNK, S5_CHUNK, g, SSM_GROUP_CH).transpose(0, 2, 1, 3).reshape(t // S5_CHUNK, -1)


def _from_groups(y, sw):
    nk = y.shape[0]
    g = sw // SSM_GROUP_CH
    return y.reshape(nk, g, S5_CHUNK, SSM_GROUP_CH).transpose(0, 2, 1, 3).reshape(nk * S5_CHUNK, sw)


def _s5_fwd(u2, mats):
    mt, ws, wo, fa, fb = mats
    nk = u2.shape[0]
    g = mt.shape[0]
    y_intra = _gmm("s5_intra", u2, mt, 'nn', b_stacked=True)
    s = _gmm("s5_chunk_state", u2, ws, 'nn', b_stacked=True)
    (xin,) = _s5_scan("s5_scan", s.reshape(nk, g, -1), fa, fb, False)
    xin = xin.reshape(nk, -1)
    return _gmm("s5_inter", xin, wo, 'nn', b_stacked=True, add=y_intra), xin


def _s5_bwd(u2, xin, mats, dy2):
    mt, ws, wo, fa, fb = mats
    nk = u2.shape[0]
    g = mt.shape[0]
    dxin = _gmm("s5_dxin", dy2, wo, 'nt', b_stacked=True)
    ds, pa, pb = _s5_scan("s5_scan_adj", dxin.reshape(nk, g, -1), fa, -fb, True, xin=xin.reshape(nk, g, -1))
    ds = ds.reshape(nk, -1)
    du_a = _gmm("s5_du_intra", dy2, mt, 'nt', b_stacked=True)
    du2 = _gmm("s5_du_state", ds, ws, 'nt', b_stacked=True, add=du_a)
    dmt = _gmm("s5_dmt", u2, dy2, 'tn', o_stacked=True)
    dws = _gmm("s5_dws", u2, ds, 'tn', o_stacked=True)
    dwo = _gmm("s5_dwo", xin, dy2, 'tn', o_stacked=True)
    return du2, (dmt, dws, dwo, pa, pb)


def _local_step(x, target, big, small):
    t, d = x.shape
    w_in4, w_glu, w_out, w_gate4, w_up4, w_down4 = big
    aw = w_in4.shape[2]
    sw = w_glu.shape[0]
    nh = aw // HEAD_DIM
    ffs = w_gate4.shape[2]
    row = lambda v: v.reshape(1, -1)
    g_mix, g_ffn = row(small['g_mix']), row(small['g_ffn'])
    g_oa, g_os, b_glu = row(small['g_out_attn']), row(small['g_out_ssm']), row(small['b_glu'])
    qg2 = jnp.tile(row(small['q_gain']), (1, 2))
    kg2 = jnp.tile(row(small['k_gain']), (1, 2))

    (h,) = _ew("rms_mix", lambda xv, g: _rms(xv, g)[0], [('r', x), ('c', g_mix)], [('r', d, MXU_DTYPE)])
    z4 = _mm("in_proj", h, w_in4, contract='nn', b_mode='b', o_mode='b')
    bias_t = _bias_table(small['rpb'])
    ya = _attn_fwd(z4, qg2, kg2, bias_t)
    s5_params = tuple(small[n] for n in ('ssm_a_re', 'ssm_a_im', 'ssm_b_re', 'ssm_b_im', 'ssm_c_re', 'ssm_c_im',
                                         'ssm_log_step', 'ssm_d'))
    mats, mats_vjp = jax.vjp(_s5_mats, *s5_params)
    u2 = _to_groups(z4[3])
    ypre2, xin = _s5_fwd(u2, mats)
    ypre = _from_groups(ypre2, sw)
    (yb,) = _ew("gelu", _gelu, [('r', ypre)], [('r', sw, MXU_DTYPE)])
    a_glu = _mm("glu_proj", yb, w_glu, contract='nn')

    def mix_out(yav, ypv, av, bg, goa, gos):
        ys = _gelu(ypv) * _sigmoid(av + bg)
        return jnp.concatenate([_rms(yav, goa)[0], _rms(ys, gos)[0]], axis=1)
    (ycat,) = _ew("mix_out", mix_out, [('r', ya), ('r', ypre), ('r', a_glu), ('c', b_glu), ('c', g_oa), ('c', g_os)],
                  [('r', aw + sw, MXU_DTYPE)])
    x1 = _mm("out_proj", ycat, w_out, contract='nn', add=x)
    (h2,) = _ew("rms_ffn", lambda xv, g: _rms(xv, g)[0], [('r', x1), ('c', g_ffn)], [('r', d, MXU_DTYPE)])
    gate4 = _mm("ffn_gate", h2, w_gate4, contract='nn', b_mode='b', o_mode='b', tn=ffs)
    up4 = _mm("ffn_up", h2, w_up4, contract='nn', b_mode='b', o_mode='b', tn=ffs)
    gate_f, up_f = gate4.reshape(4 * t, ffs), up4.reshape(4 * t, ffs)
    (act,) = _ew("swiglu", lambda gv, uv: gv * _sigmoid(gv) * uv, [('r', gate_f), ('r', up_f)],
                 [('r', ffs, MXU_DTYPE)])
    act4 = act.reshape(4, t, ffs)
    x2 = _mm("ffn_down", act4, w_down4, contract='nn', a_mode='c', b_mode='c', add=x1, tk=ffs)

    def loss_fn(xv, tv):
        diff = xv - tv
        return diff * (1.0 / d), diff * diff
    dx2, sq = _ew("loss", loss_fn, [('r', x2), ('r', target)], [('r', d, F32), ('a', d)])

    dact4 = _mm("ffn_down_dx", dx2, w_down4, contract='nt', b_mode='b', o_mode='b', tn=ffs)
    d_w_down4 = _mm("ffn_down_dw", act4, dx2, contract='tn', a_mode='b', o_mode='b', tm=ffs)

    def swiglu_bwd(dav, gv, uv):
        s = _sigmoid(gv)
        return dav * uv * s * (1.0 + gv * (1.0 - s)), dav * gv * s
    dgate, dup = _ew("swiglu_bwd", swiglu_bwd, [('r', dact4.reshape(4 * t, ffs)), ('r', gate_f), ('r', up_f)],
                     [('r', ffs, MXU_DTYPE), ('r', ffs, MXU_DTYPE)])
    dgate4, dup4 = dgate.reshape(4, t, ffs), dup.reshape(4, t, ffs)
    dh2 = _mm("ffn_gate_dx", dgate4, w_gate4, contract='nt', a_mode='c', b_mode='c', tk=ffs)
    dh2 = _mm("ffn_up_dx", dup4, w_up4, contract='nt', a_mode='c', b_mode='c', add=dh2, tk=ffs)
    d_w_gate4 = _mm("ffn_gate_dw", h2, dgate4, contract='tn', b_mode='b', o_mode='b', tn=ffs)
    d_w_up4 = _mm("ffn_up_dw", h2, dup4, contract='tn', b_mode='b', o_mode='b', tn=ffs)

    def rms_res_bwd(xv, g, dyv, resv):
        dx, dg = _rms_bwd(xv, g, dyv)
        return resv + dx, dg
    dx1, d_g_ffn = _ew("rms_ffn_bwd", rms_res_bwd, [('r', x1), ('c', g_ffn), ('r', dh2), ('r', dx2)],
                       [('r', d, F32), ('a', d)])

    dycat = _mm("out_proj_dx", dx1, w_out, contract='nt')
    d_w_out = _mm("out_proj_dw", ycat, dx1, contract='tn')

    def mix_out_bwd(yav, ypv, av, bg, goa, gos, dca, dcs):
        dya, dgoa = _rms_bwd(yav, goa, dca)
        y = _gelu(ypv)
        s = _sigmoid(av + bg)
        dys, dgos = _rms_bwd(y * s, gos, dcs)
        da = dys * y * s * (1.0 - s)
        return dya, da, dys * s, dgoa, dgos, da
    dya, da, dy_direct, d_g_oa, d_g_os, d_b_glu = _ew(
        "mix_out_bwd", mix_out_bwd,
        [('r', ya), ('r', ypre), ('r', a_glu), ('c', b_glu), ('c', g_oa), ('c', g_os),
         ('r', dycat, 0, aw), ('r', dycat, 1, sw)],
        [('r', aw, F32), ('r', sw, MXU_DTYPE), ('r', sw, F32), ('a', aw), ('a', sw), ('a', sw)])
    dy = _mm("glu_proj_dx", da, w_glu, contract='nt', add=dy_direct)
    d_w_glu = _mm("glu_proj_dw", yb, da, contract='tn')
    (dypre,) = _ew("gelu_bwd", lambda dyv, ypv: dyv * _gelu_grad(ypv), [('r', dy), ('r', ypre)], [('r', sw, F32)])

    du2, dmats = _s5_bwd(u2, xin, mats, _to_groups(dypre))
    d_s5 = mats_vjp(dmats)
    du = _from_groups(du2, sw)
    dq, dk, dv, dbias_t, dqg, dkg = _attn_bwd(z4, qg2, kg2, bias_t, dya)
    d_rpb = _bias_table_grad(dbias_t)
    fold = lambda v: v.reshape(-1, 2, HEAD_DIM).sum(axis=(0, 1))
    dz4 = jnp.stack([dq, dk, dv, du])

    dh = _mm("in_proj_dx", dz4, w_in4, contract='nt', a_mode='c', b_mode='c')
    d_w_in4 = _mm("in_proj_dw", h, dz4, contract='tn', b_mode='b', o_mode='b')
    dx, d_g_mix = _ew("rms_mix_bwd", rms_res_bwd, [('r', x), ('c', g_mix), ('r', dh), ('r', dx1)],
                      [('r', d, F32), ('a', d)])

    colsum = lambda v: v.sum(axis=0)
    d_small = {
        'g_mix': colsum(d_g_mix), 'q_gain': fold(dqg), 'k_gain': fold(dkg), 'rpb': d_rpb,
        'ssm_a_re': d_s5[0], 'ssm_a_im': d_s5[1], 'ssm_b_re': d_s5[2], 'ssm_b_im': d_s5[3],
        'ssm_c_re': d_s5[4], 'ssm_c_im': d_s5[5], 'ssm_log_step': d_s5[6], 'ssm_d': d_s5[7],
        'b_glu': colsum(d_b_glu), 'g_out_attn': colsum(d_g_oa), 'g_out_ssm': colsum(d_g_os), 'g_ffn': colsum(d_g_ffn),
    }
    d_big = (d_w_in4, d_w_glu, d_w_out, d_w_gate4, d_w_up4, d_w_down4)
    return jnp.sum(sq), dx, d_big, d_small


ANY = pl.BlockSpec(memory_space=pl.ANY)


def _place():
    x, y, c = lax.axis_index("x"), lax.axis_index("y"), lax.axis_index("c")
    other_chips = [(1 - x, y), (x, 1 - y), (1 - x, 1 - y)]
    return x, y, c, 2 * x + y, (x, y, 1 - c), other_chips


def _gather_chips(name, arrs):
    n = len(arrs)

    def body(*refs):
        ins, outs = refs[:n], refs[n:2 * n]
        send_sems, recv_sems, local_sems = refs[2 * n:]
        x, y, c, me, sibling, chips = _place()

        def remote(a, k, src, dst, to):
            return pltpu.make_async_remote_copy(src_ref=src, dst_ref=dst, send_sem=send_sems.at[a, k],
                                                recv_sem=recv_sems.at[a, k], device_id=to, device_id_type=MESH)

        started = []
        for a in range(n):
            own = pltpu.make_async_copy(ins[a], outs[a].at[me], local_sems.at[a])
            own.start()
            started.append(own)
        sends = []
        for j, (px, py) in enumerate(chips):
            for a in range(n):
                cp = remote(a, j, ins[a].at[c], outs[a].at[me, c], (px, py, c))
                cp.start()
                sends.append(cp)
        for j, (px, py) in enumerate(chips):
            for a in range(n):
                landed = outs[a].at[2 * px + py, c]
                remote(a, j, landed, landed, (px, py, c)).wait_recv()
                cp = remote(a, 3 + j, landed, landed, sibling)
                cp.start()
                sends.append(cp)
        for j, (px, py) in enumerate(chips):
            for a in range(n):
                theirs = outs[a].at[2 * px + py, 1 - c]
                remote(a, 3 + j, theirs, theirs, sibling).wait_recv()
        for cp in sends:
            cp.wait_send()
        for cp in started:
            cp.wait()

    return pl.pallas_call(
        body, name=name, in_specs=[ANY] * n, out_specs=[ANY] * n,
        out_shape=[jax.ShapeDtypeStruct((N_CHIPS,) + a.shape, a.dtype) for a in arrs],
        scratch_shapes=[pltpu.SemaphoreType.DMA((n, 6)), pltpu.SemaphoreType.DMA((n, 6)),
                        pltpu.SemaphoreType.DMA((n,))],
    )(*arrs)


def _swap_halves(name, parts):
    n = len(parts)

    def body(*refs):
        ins, outs = refs[:n], refs[n:2 * n]
        send_sems, recv_sems = refs[2 * n:]
        x, y, c, me, sibling, chips = _place()
        cps = []
        for a in range(n):
            cp = pltpu.make_async_remote_copy(src_ref=ins[a].at[:, 1 - c], dst_ref=outs[a], send_sem=send_sems.at[a],
                                              recv_sem=recv_sems.at[a], device_id=sibling, device_id_type=MESH)
            cp.start()
            cps.append(cp)
        for cp in cps:
            cp.wait()

    return pl.pallas_call(
        body, name=name, in_specs=[ANY] * n, out_specs=[ANY] * n,
        out_shape=[jax.ShapeDtypeStruct((N_CHIPS,) + p.shape[2:], p.dtype) for p in parts],
        scratch_shapes=[pltpu.SemaphoreType.DMA((n,)), pltpu.SemaphoreType.DMA((n,))],
    )(*parts)


def _scatter_chips(name, sums):
    n = len(sums)

    def body(*refs):
        ins, outs = refs[:n], refs[n:2 * n]
        send_sems, recv_sems, local_sems = refs[2 * n:]
        x, y, c, me, sibling, chips = _place()
        started, sends = [], []
        for a in range(n):
            own = pltpu.make_async_copy(ins[a].at[me], outs[a].at[me], local_sems.at[a])
            own.start()
            started.append(own)
        for j, (px, py) in enumerate(chips):
            for a in range(n):
                cp = pltpu.make_async_remote_copy(
                    src_ref=ins[a].at[2 * px + py], dst_ref=outs[a].at[me], send_sem=send_sems.at[a, j],
                    recv_sem=recv_sems.at[a, j], device_id=(px, py, c), device_id_type=MESH)
                cp.start()
                sends.append(cp)
        for j, (px, py) in enumerate(chips):
            for a in range(n):
                slot = outs[a].at[2 * px + py]
                pltpu.make_async_remote_copy(
                    src_ref=slot, dst_ref=slot, send_sem=send_sems.at[a, j], recv_sem=recv_sems.at[a, j],
                    device_id=(px, py, c), device_id_type=MESH).wait_recv()
        for cp in sends:
            cp.wait_send()
        for cp in started:
            cp.wait()

    return pl.pallas_call(
        body, name=name, in_specs=[ANY] * n, out_specs=[ANY] * n,
        out_shape=[jax.ShapeDtypeStruct(s.shape, s.dtype) for s in sums],
        scratch_shapes=[pltpu.SemaphoreType.DMA((n, 3)), pltpu.SemaphoreType.DMA((n, 3)),
                        pltpu.SemaphoreType.DMA((n,))],
    )(*sums)


def _join_halves(name, halves):
    n = len(halves)

    def body(*refs):
        ins, outs = refs[:n], refs[n:2 * n]
        send_sems, recv_sems, local_sems = refs[2 * n:]
        x, y, c, me, sibling, chips = _place()
        started, cps = [], []
        for a in range(n):
            own = pltpu.make_async_copy(ins[a], outs[a].at[c], local_sems.at[a])
            own.start()
            started.append(own)
            cp = pltpu.make_async_remote_copy(src_ref=ins[a], dst_ref=outs[a].at[c], send_sem=send_sems.at[a],
                                              recv_sem=recv_sems.at[a], device_id=sibling, device_id_type=MESH)
            cp.start()
            cps.append(cp)
        for a in range(n):
            theirs = outs[a].at[1 - c]
            pltpu.make_async_remote_copy(src_ref=theirs, dst_ref=theirs, send_sem=send_sems.at[a],
                                         recv_sem=recv_sems.at[a], device_id=sibling, device_id_type=MESH).wait_recv()
        for cp in cps:
            cp.wait_send()
        for cp in started:
            cp.wait()

    return pl.pallas_call(
        body, name=name, in_specs=[ANY] * n, out_specs=[ANY] * n,
        out_shape=[jax.ShapeDtypeStruct((2,) + h.shape, h.dtype) for h in halves],
        scratch_shapes=[pltpu.SemaphoreType.DMA((n,)), pltpu.SemaphoreType.DMA((n,)),
                        pltpu.SemaphoreType.DMA((n,))],
    )(*halves)


def _row_tile(r, want=256):
    t = (min(r, want) // SUBLANES) * SUBLANES
    while r % t:
        t -= SUBLANES
    return t


def _add_own_half(name, part, got, c):
    _, _, r, cols = part.shape
    tr = _row_tile(r)

    def body(c_ref, p_ref, g_ref, o_ref):
        o_ref[...] = p_ref[...] + g_ref[...]

    return pl.pallas_call(
        body, name=name,
        grid_spec=pltpu.PrefetchScalarGridSpec(
            num_scalar_prefetch=1, grid=(N_CHIPS, r // tr),
            in_specs=[pl.BlockSpec((None, None, tr, cols), lambda s, i, c_ref: (s, c_ref[0], i, 0)),
                      pl.BlockSpec((None, tr, cols), lambda s, i, c_ref: (s, i, 0))],
            out_specs=pl.BlockSpec((None, tr, cols), lambda s, i, c_ref: (s, i, 0))),
        out_shape=jax.ShapeDtypeStruct(got.shape, got.dtype),
        compiler_params=_cparams(("parallel", "parallel")),
    )(c.reshape(1).astype(jnp.int32), part, got)


def _sum_chips(name, got):
    _, r, cols = got.shape
    tr = _row_tile(r)

    def body(r0, r1, r2, r3, o_ref):
        o_ref[...] = ((r0[...] + r1[...]) + r2[...]) + r3[...]

    return pl.pallas_call(
        body, name=name, grid=(r // tr,),
        in_specs=[pl.BlockSpec((None, tr, cols), functools.partial(lambda s, i: (s, i, 0), s)) for s in range(N_CHIPS)],
        out_specs=pl.BlockSpec((tr, cols), lambda i: (i, 0)),
        out_shape=jax.ShapeDtypeStruct((r, cols), got.dtype),
        compiler_params=_cparams(("parallel",)),
    )(got, got, got, got)


def _adamw(name, w, g, m, v):
    def fn(wv, gv, mv, vv):
        mv = ADAM_B1 * mv + (1.0 - ADAM_B1) * gv
        vv = ADAM_B2 * vv + (1.0 - ADAM_B2) * (gv * gv)
        m_hat = mv / (1.0 - ADAM_B1 ** ADAM_STEP)
        v_hat = vv / (1.0 - ADAM_B2 ** ADAM_STEP)
        return -ADAM_LR * (m_hat / (jnp.sqrt(v_hat) + ADAM_EPS) + ADAM_WD * wv), mv, vv
    cols = w.shape[1]
    return _ew(name, fn, [('r', w), ('r', g), ('r', m), ('r', v)], [('r', cols, F32)] * 3, tr=_row_tile(w.shape[0], 128))


SMALL_ROWS_ALIGN = 2 * N_CHIPS * SUBLANES


def _pack_small(d):
    flat = jnp.concatenate([d[n].reshape(-1).astype(F32) for n in SMALL_NAMES])
    rows = -(-flat.shape[0] // (LANES * SMALL_ROWS_ALIGN)) * SMALL_ROWS_ALIGN
    return jnp.pad(flat, (0, rows * LANES - flat.shape[0])).reshape(rows, LANES)


def _unpack_small(packed, like):
    flat = packed.reshape(-1)
    out, off = {}, 0
    for n in SMALL_NAMES:
        size = like[n].size
        out[n] = flat[off:off + size].reshape(like[n].shape)
        off += size
    return out


def kernel(x, g_mix, w_in, q_gain, k_gain, rpb, ssm_a_re, ssm_a_im, ssm_b_re, ssm_b_im, ssm_c_re, ssm_c_im, ssm_log_step, ssm_d, w_glu, b_glu, g_out_attn, g_out_ssm, w_out, g_ffn, w_ffn_gate, w_ffn_up, w_ffn_down, loss_target, m_g_mix, m_w_in, m_q_gain, m_k_gain, m_rpb, m_ssm_a_re, m_ssm_a_im, m_ssm_b_re, m_ssm_b_im, m_ssm_c_re, m_ssm_c_im, m_ssm_log_step, m_ssm_d, m_w_glu, m_b_glu, m_g_out_attn, m_g_out_ssm, m_w_out, m_g_ffn, m_w_ffn_gate, m_w_ffn_up, m_w_ffn_down, v_g_mix, v_w_in, v_q_gain, v_k_gain, v_rpb, v_ssm_a_re, v_ssm_a_im, v_ssm_b_re, v_ssm_b_im, v_ssm_c_re, v_ssm_c_im, v_ssm_log_step, v_ssm_d, v_w_glu, v_b_glu, v_g_out_attn, v_g_out_ssm, v_w_out, v_g_ffn, v_w_ffn_gate, v_w_ffn_up, v_w_ffn_down):
    given = dict(locals())
    w = {n: given[n][0] for n in WEIGHT_NAMES}
    mom = {n: given["m_" + n][0] for n in WEIGHT_NAMES}
    var = {n: given["v_" + n][0] for n in WEIGHT_NAMES}
    d = x.shape[-1]
    c = lax.axis_index("c")

    halves = [w[n].astype(MXU_DTYPE).reshape((2, w[n].shape[0] // 2, w[n].shape[1])) for n in BIG_NAMES]
    gathered = _gather_chips("gather_weights", halves)
    big = tuple(g4.reshape((N_CHIPS, -1, g4.shape[-1])) for g4 in gathered)
    big = (big[0], big[1].reshape(-1, big[1].shape[-1]), big[2].reshape(-1, big[2].shape[-1]), big[3], big[4], big[5])

    sq, dx, d_big, d_small = _local_step(x[0], loss_target[0], big, {n: w[n] for n in SMALL_NAMES})
    loss = lax.psum(0.5 * sq / d, ("x", "y", "c"))

    parts = [g.reshape((N_CHIPS, 2, -1, g.shape[-1])) for g in d_big]
    small_part = _pack_small(d_small)
    srows = small_part.shape[0]
    parts.append(small_part.reshape(N_CHIPS, 2, srows // (2 * N_CHIPS), LANES))
    got = _swap_halves("reduce_swap_halves", parts)
    sums = [_add_own_half("reduce_add_%d" % a, p, gt, c) for a, (p, gt) in enumerate(zip(parts, got))]
    got = _scatter_chips("reduce_scatter_chips", sums)
    mine = [_sum_chips("reduce_sum_%d" % a, gt) for a, gt in enumerate(got)]
    full = _join_halves("reduce_join_halves", mine)
    grad_big = {n: f.reshape(w[n].shape) for n, f in zip(BIG_NAMES, full[:-1])}
    (small_all,) = _gather_chips("gather_small", [full[-1]])
    small_all = small_all.reshape(srows, LANES)
    grad_small = _unpack_small(small_all, {n: w[n] for n in SMALL_NAMES})

    delta, new_m, new_v = {}, {}, {}
    for a, n in enumerate(BIG_NAMES):
        delta[n], new_m[n], new_v[n] = _adamw("adamw_%d" % a, w[n], grad_big[n], mom[n], var[n])
    sd, sm, sv = _adamw("adamw_small", _pack_small({n: w[n] for n in SMALL_NAMES}), small_all,
                        _pack_small({n: mom[n] for n in SMALL_NAMES}), _pack_small({n: var[n] for n in SMALL_NAMES}))
    like = {n: w[n] for n in SMALL_NAMES}
    delta.update(_unpack_small(sd, like))
    new_m.update(_unpack_small(sm, like))
    new_v.update(_unpack_small(sv, like))
    grads = {**grad_big, **grad_small}
    lead = lambda t: t[None]
    return (loss, dx[None], *[lead(grads[n]) for n in WEIGHT_NAMES], *[lead(delta[n]) for n in WEIGHT_NAMES],
            *[lead(new_m[n]) for n in WEIGHT_NAMES], *[lead(new_v[n]) for n in WEIGHT_NAMES])
```

```python
import functools
import math

import jax
import jax.numpy as jnp
from jax import lax
from jax.experimental import pallas as pl
from jax.experimental.pallas import tpu as pltpu

F32 = jnp.float32
BF16 = jnp.bfloat16
MXU_DTYPE = BF16
GRAD_PAYLOAD_DTYPE = BF16
HI = lax.Precision.HIGHEST
VMEM_LIMIT_V7X = 56 * 1024 * 1024
LANES = 128
SUBLANES = 8

GRID_W = 64
WIN_H = 8
WIN_W = 16
HEAD_DIM = 64
SSM_GROUP_CH = 16
SSM_STATE = 64
S5_CHUNK = 16
RMS_EPS = 1e-6
NEG_INF = -1e30
N_CHIPS = 4
MESH = pl.DeviceIdType.MESH

ADAM_LR = 0.001
ADAM_B1 = 0.9
ADAM_B2 = 0.999
ADAM_EPS = 1e-08
ADAM_WD = 0.01
ADAM_STEP = 10

WEIGHT_NAMES = ['g_mix', 'w_in', 'q_gain', 'k_gain', 'rpb', 'ssm_a_re', 'ssm_a_im', 'ssm_b_re', 'ssm_b_im',
                'ssm_c_re', 'ssm_c_im', 'ssm_log_step', 'ssm_d', 'w_glu', 'b_glu', 'g_out_attn', 'g_out_ssm',
                'w_out', 'g_ffn', 'w_ffn_gate', 'w_ffn_up', 'w_ffn_down']
BIG_NAMES = ['w_in', 'w_glu', 'w_out', 'w_ffn_gate', 'w_ffn_up', 'w_ffn_down']
SMALL_NAMES = [n for n in WEIGHT_NAMES if n not in BIG_NAMES]


def _cparams(sem):
    return pltpu.CompilerParams(dimension_semantics=sem, vmem_limit_bytes=VMEM_LIMIT_V7X)


def _tile(n, want):
    if n <= want:
        return n
    t = (want // LANES) * LANES
    while t >= LANES:
        if n % t == 0:
            return t
        t -= LANES
    return n


def _mm(name, a, b, *, contract, a_mode='2', b_mode='2', o_mode='2', out_dtype=F32, add=None, exact=False,
        tm=1024, tn=1024, tk=512):
    dn = {'nn': (((1,), (0,)), ((), ())), 'nt': (((1,), (1,)), ((), ())), 'tn': (((0,), (0,)), ((), ()))}[contract]
    ar, ac = a.shape[-2:]
    br, bc = b.shape[-2:]
    m, kdim = (ar, ac) if contract != 'tn' else (ac, ar)
    n = bc if contract != 'nt' else br
    assert kdim == (br if contract != 'nt' else bc), (name, a.shape, b.shape)
    nbatch = 1
    for arr, mode in ((a, a_mode), (b, b_mode)):
        if mode == 'b':
            nbatch = arr.shape[0]
    nstack = 1
    for arr, mode in ((a, a_mode), (b, b_mode)):
        if mode == 'c':
            nstack = arr.shape[0]
    tm, tn, tk = _tile(m, tm), _tile(n, tn), _tile(kdim, tk)
    nkin = kdim // tk
    nk = nstack * nkin
    grid = (nbatch, m // tm, n // tn, nk)

    def spec(mode, block, rc):
        def imap(s, i, j, kk):
            r, c = rc(i, j, kk % nkin)
            if mode == '2':
                return (r, c)
            return (s if mode == 'b' else kk // nkin, r, c)
        return pl.BlockSpec(block if mode == '2' else (None,) + block, imap)

    a_spec = spec(a_mode, (tm, tk) if contract != 'tn' else (tk, tm),
                  (lambda i, j, k: (i, k)) if contract != 'tn' else (lambda i, j, k: (k, i)))
    b_spec = spec(b_mode, (tk, tn) if contract != 'nt' else (tn, tk),
                  (lambda i, j, k: (k, j)) if contract != 'nt' else (lambda i, j, k: (j, k)))
    o_spec = spec(o_mode, (tm, tn), lambda i, j, k: (i, j))
    out_shape = (m, n) if o_mode == '2' else (nbatch, m, n)
    has_add = add is not None

    def body(*refs):
        if has_add:
            a_ref, b_ref, add_ref, o_ref, acc_ref = refs
        else:
            a_ref, b_ref, o_ref, acc_ref = refs
        k = pl.program_id(3)

        @pl.when(k == 0)
        def _():
            acc_ref[...] = jnp.zeros_like(acc_ref)

        if exact:
            acc_ref[...] += lax.dot_general(a_ref[...].astype(F32), b_ref[...].astype(F32), dn, precision=HI,
                                            preferred_element_type=F32)
        else:
            acc_ref[...] += lax.dot_general(a_ref[...].astype(MXU_DTYPE), b_ref[...].astype(MXU_DTYPE), dn,
                                            preferred_element_type=F32)

        @pl.when(k == nk - 1)
        def _():
            r = acc_ref[...]
            if has_add:
                r = r + add_ref[...].astype(F32)
            o_ref[...] = r.astype(o_ref.dtype)

    in_specs = [a_spec, b_spec] + ([o_spec] if has_add else [])
    args = (a, b) + ((add,) if has_add else ())
    return pl.pallas_call(
        body, name=name, grid=grid, in_specs=in_specs, out_specs=o_spec,
        out_shape=jax.ShapeDtypeStruct(out_shape, out_dtype),
        scratch_shapes=[pltpu.VMEM((tm, tn), F32)],
        compiler_params=_cparams(("parallel", "parallel", "parallel", "arbitrary")),
    )(*args)


def _ew(name, fn, ins, outs, tr=256):
    rows = next(x[1].shape[0] for x in ins if x[0] == 'r')
    tr = min(tr, rows)
    assert rows % tr == 0 and tr % SUBLANES == 0, (name, rows, tr)
    in_specs, args = [], []
    for x in ins:
        if x[0] == 'r' and len(x) == 2:
            in_specs.append(pl.BlockSpec((tr, x[1].shape[1]), lambda i: (i, 0)))
        elif x[0] == 'r':
            in_specs.append(pl.BlockSpec((tr, x[3]), functools.partial(lambda cb, i: (i, cb), x[2])))
        else:
            in_specs.append(pl.BlockSpec(x[1].shape, lambda i: (0, 0)))
        args.append(x[1])
    out_specs, out_shapes = [], []
    for o in outs:
        if o[0] == 'r':
            out_specs.append(pl.BlockSpec((tr, o[1]), lambda i: (i, 0)))
            out_shapes.append(jax.ShapeDtypeStruct((rows, o[1]), o[2]))
        else:
            out_specs.append(pl.BlockSpec((SUBLANES, o[1]), lambda i: (0, 0)))
            out_shapes.append(jax.ShapeDtypeStruct((SUBLANES, o[1]), F32))
    nin = len(ins)
    has_acc = any(o[0] == 'a' for o in outs)

    def body(*refs):
        vals = fn(*[r[...] for r in refs[:nin]])
        if not isinstance(vals, (tuple, list)):
            vals = (vals,)
        i = pl.program_id(0)
        for o, ref, v in zip(outs, refs[nin:], vals):
            if o[0] == 'r':
                ref[...] = v.astype(ref.dtype)
            else:
                part = v.astype(F32).reshape(tr // SUBLANES, SUBLANES, o[1]).sum(axis=0)

                @pl.when(i == 0)
                def _(ref=ref, part=part):
                    ref[...] = part

                @pl.when(i > 0)
                def _(ref=ref, part=part):
                    ref[...] += part

    res = pl.pallas_call(
        body, name=name, grid=(rows // tr,), in_specs=in_specs, out_specs=out_specs, out_shape=out_shapes,
        compiler_params=_cparams(("arbitrary",) if has_acc else ("parallel",)),
    )(*args)
    return res


def _rms(x, g):
    r = lax.rsqrt(jnp.mean(x * x, axis=-1, keepdims=True) + RMS_EPS)
    xr = x * r
    return xr * g, xr


def _rms_bwd(x, g, dy):
    r = lax.rsqrt(jnp.mean(x * x, axis=-1, keepdims=True) + RMS_EPS)
    xr = x * r
    gdy = g * dy
    dx = r * (gdy - xr * jnp.mean(xr * gdy, axis=-1, keepdims=True))
    return dx, dy * xr


def _sigmoid(x):
    return 1.0 / (1.0 + jnp.exp(-x))


_GELU_C = math.sqrt(2.0 / math.pi)


def _gelu(x):
    return 0.5 * x * (1.0 + jnp.tanh(_GELU_C * (x + 0.044715 * x * x * x)))


def _gelu_grad(x):
    t = jnp.tanh(_GELU_C * (x + 0.044715 * x * x * x))
    return 0.5 * (1.0 + t) + 0.5 * x * (1.0 - t * t) * _GELU_C * (1.0 + 3 * 0.044715 * x * x)


ATTN_ROWS_PER_STEP = 8
NT_DIMS = (((1,), (1,)), ((), ()))
NN_DIMS = (((1,), (0,)), ((), ()))
TN_DIMS = (((0,), (0,)), ((), ()))


def _attn_geometry(r, rows):
    row_start = jnp.clip(r - WIN_H // 2, 0, rows - WIN_H)
    key0 = pl.multiple_of(row_start * GRID_W, GRID_W)
    bias0 = pl.multiple_of((row_start - r + (WIN_H - 1)) * GRID_W, GRID_W)
    return key0, bias0


def _window_onehot():
    c = jnp.arange(GRID_W)
    col_start = jnp.clip(c - WIN_W // 2, 0, GRID_W - WIN_W)
    col_in = (c[None, :] >= col_start[:, None]) & (c[None, :] < col_start[:, None] + WIN_W)
    dc = jnp.clip(c[None, :] - c[:, None], -(WIN_W - 1), WIN_W - 1) + (WIN_W - 1)
    onehot = ((dc[:, :, None] == jnp.arange(2 * WIN_W - 1)[None, None, :]) & col_in[:, :, None]).astype(F32)
    return onehot, col_in


def _bias_table(rpb):
    onehot, col_in = _window_onehot()
    nh = rpb.shape[0]
    tab = jnp.einsum('perd,qkd->prkeq', rpb.reshape(nh // 2, 2, 2 * WIN_H - 1, 2 * WIN_W - 1), onehot, precision=HI)
    tab = tab + jnp.where(col_in, 0.0, NEG_INF).T[None, None, :, None, :]
    return tab.reshape(nh // 2, (2 * WIN_H - 1) * GRID_W, 2 * GRID_W)


def _bias_table_grad(dtab):
    onehot, _ = _window_onehot()
    npair = dtab.shape[0]
    d = dtab.reshape(npair, 2 * WIN_H - 1, GRID_W, 2, GRID_W)
    return jnp.einsum('prkeq,qkd->perd', d, onehot, precision=HI).reshape(2 * npair, 2 * WIN_H - 1, 2 * WIN_W - 1)


def _lane_lo(shape):
    return lax.broadcasted_iota(jnp.int32, shape, 1) < HEAD_DIM


def _half_sums(v):
    lo = _lane_lo(v.shape)
    s_lo = jnp.sum(jnp.where(lo, v, 0.0), axis=1, keepdims=True)
    s_hi = jnp.sum(jnp.where(lo, 0.0, v), axis=1, keepdims=True)
    return jnp.where(lo, s_lo, s_hi)


def _rms_pair(x, g):
    r = lax.rsqrt(_half_sums(x * x) * (1.0 / HEAD_DIM) + RMS_EPS)
    return x * r * g


def _rms_pair_bwd(x, g, dy):
    r = lax.rsqrt(_half_sums(x * x) * (1.0 / HEAD_DIM) + RMS_EPS)
    xr = x * r
    gdy = g * dy
    dx = r * (gdy - xr * (_half_sums(xr * gdy) * (1.0 / HEAD_DIM)))
    return dx, dy * xr


def _blockdiag(a):
    a2 = jnp.concatenate([a, a], axis=0)
    row_hi = lax.broadcasted_iota(jnp.int32, a2.shape, 0) >= GRID_W
    lane_hi = lax.broadcasted_iota(jnp.int32, a2.shape, 1) >= HEAD_DIM
    return jnp.where(row_hi == lane_hi, a2, 0.0).astype(MXU_DTYPE)


def _diag_blocks(m):
    return jnp.where(_lane_lo((GRID_W, 2 * HEAD_DIM)), m[:GRID_W], m[GRID_W:])


def _attn_scores(qb, kb, bias):
    st = lax.dot_general(kb, qb, NT_DIMS, preferred_element_type=F32)
    st = st * (1.0 / math.sqrt(HEAD_DIM)) + bias
    mx = jnp.max(st, axis=0, keepdims=True)
    p = jnp.exp(st - mx)
    return p * (1.0 / jnp.sum(p, axis=0, keepdims=True))


def _attn_fwd(z4, qg2, kg2, bias_t):
    _, t, aw = z4.shape
    rows = t // GRID_W
    npair = aw // (2 * HEAD_DIM)
    nkeys = WIN_H * GRID_W
    nb = bias_t.shape[1]
    rps = min(ATTN_ROWS_PER_STEP, rows)
    blk = rps * GRID_W

    def body(q_ref, k_ref, v_ref, qg_ref, kg_ref, b_ref, o_ref, kn_ref, vb_ref):
        rb = pl.program_id(1)

        @pl.when(rb == 0)
        def _():
            kn_ref[...] = _rms_pair(k_ref[...], kg_ref[...]).astype(MXU_DTYPE)
            vb_ref[...] = v_ref[...].astype(MXU_DTYPE)

        def row(i, carry):
            key0, bias0 = _attn_geometry(rb * rps + i, rows)
            at = pl.ds(pl.multiple_of(i * GRID_W, GRID_W), GRID_W)
            qb = _blockdiag(_rms_pair(q_ref[at, :], qg_ref[...]))
            pt = _attn_scores(qb, kn_ref[pl.ds(key0, nkeys), :], b_ref[pl.ds(bias0, nkeys), :])
            both = lax.dot_general(pt.astype(MXU_DTYPE), vb_ref[pl.ds(key0, nkeys), :], TN_DIMS,
                                   preferred_element_type=F32)
            o_ref[at, :] = _diag_blocks(both)
            return carry

        lax.fori_loop(0, rps, row, 0)

    pair_cols = lambda lead: pl.BlockSpec((None, t, 2 * HEAD_DIM), lambda p, r: (lead, 0, p))
    return pl.pallas_call(
        body, name="attn_fwd", grid=(npair, rows // rps),
        in_specs=[pl.BlockSpec((None, blk, 2 * HEAD_DIM), lambda p, r: (0, r, p)), pair_cols(1), pair_cols(2),
                  pl.BlockSpec((1, 2 * HEAD_DIM), lambda p, r: (0, 0)),
                  pl.BlockSpec((1, 2 * HEAD_DIM), lambda p, r: (0, 0)),
                  pl.BlockSpec((None, nb, 2 * GRID_W), lambda p, r: (p, 0, 0))],
        out_specs=pl.BlockSpec((blk, 2 * HEAD_DIM), lambda p, r: (r, p)),
        out_shape=jax.ShapeDtypeStruct((t, aw), F32),
        scratch_shapes=[pltpu.VMEM((t, 2 * HEAD_DIM), MXU_DTYPE), pltpu.VMEM((t, 2 * HEAD_DIM), MXU_DTYPE)],
        compiler_params=_cparams(("parallel", "arbitrary")),
    )(z4, z4, z4, qg2, kg2, bias_t)


def _attn_bwd(z4, qg2, kg2, bias_t, dya):
    _, t, aw = z4.shape
    rows = t // GRID_W
    npair = aw // (2 * HEAD_DIM)
    nkeys = WIN_H * GRID_W
    nb = bias_t.shape[1]
    rps = min(ATTN_ROWS_PER_STEP, rows)
    blk = rps * GRID_W
    scale = 1.0 / math.sqrt(HEAD_DIM)

    def body(q_ref, k_ref, v_ref, qg_ref, kg_ref, b_ref, do_ref, dz_ref, db_ref, dqg_ref, dkg_ref, kn_ref, vb_ref):
        rb = pl.program_id(1)

        @pl.when(rb == 0)
        def _():
            kn_ref[...] = _rms_pair(k_ref[...], kg_ref[...]).astype(MXU_DTYPE)
            vb_ref[...] = v_ref[...].astype(MXU_DTYPE)
            dz_ref[1] = jnp.zeros((t, 2 * HEAD_DIM), F32)
            dz_ref[2] = jnp.zeros((t, 2 * HEAD_DIM), F32)
            db_ref[...] = jnp.zeros_like(db_ref)
            dqg_ref[...] = jnp.zeros_like(dqg_ref)

        def row(i, dqg_sum):
            r = rb * rps + i
            key0, bias0 = _attn_geometry(r, rows)
            keys = pl.ds(key0, nkeys)
            at = pl.ds(pl.multiple_of(i * GRID_W, GRID_W), GRID_W)
            q = q_ref[at, :]
            qb = _blockdiag(_rms_pair(q, qg_ref[...]))
            dob = _blockdiag(do_ref[at, :])
            kb = kn_ref[keys, :]
            pt = _attn_scores(qb, kb, b_ref[pl.ds(bias0, nkeys), :])
            dz_ref[2, keys, :] += lax.dot_general(pt.astype(MXU_DTYPE), dob, NN_DIMS, preferred_element_type=F32)
            dpt = lax.dot_general(vb_ref[keys, :], dob, NT_DIMS, preferred_element_type=F32)
            dst = pt * (dpt - jnp.sum(pt * dpt, axis=0, keepdims=True))
            db_ref[pl.ds(bias0, nkeys), :] += dst
            dsb = dst.astype(MXU_DTYPE)
            dz_ref[1, keys, :] += scale * lax.dot_general(dsb, qb, NN_DIMS, preferred_element_type=F32)
            dqn = scale * _diag_blocks(lax.dot_general(dsb, kb, TN_DIMS, preferred_element_type=F32))
            dq, dqg = _rms_pair_bwd(q, qg_ref[...], dqn)
            dz_ref[0, pl.ds(pl.multiple_of(r * GRID_W, GRID_W), GRID_W), :] = dq
            return dqg_sum + jnp.sum(dqg, axis=0, keepdims=True)

        dqg_ref[...] += lax.fori_loop(0, rps, row, jnp.zeros((1, 2 * HEAD_DIM), F32))

        @pl.when(rb == rows // rps - 1)
        def _():
            dk, dkg = _rms_pair_bwd(k_ref[...], kg_ref[...], dz_ref[1])
            dz_ref[1] = dk
            dkg_ref[...] = jnp.sum(dkg, axis=0, keepdims=True)

    pair_cols = lambda lead: pl.BlockSpec((None, t, 2 * HEAD_DIM), lambda p, r: (lead, 0, p))
    pair_vec = pl.BlockSpec((None, 1, 2 * HEAD_DIM), lambda p, r: (p, 0, 0))
    return pl.pallas_call(
        body, name="attn_bwd", grid=(npair, rows // rps),
        in_specs=[pl.BlockSpec((None, blk, 2 * HEAD_DIM), lambda p, r: (0, r, p)), pair_cols(1), pair_cols(2),
                  pl.BlockSpec((1, 2 * HEAD_DIM), lambda p, r: (0, 0)),
                  pl.BlockSpec((1, 2 * HEAD_DIM), lambda p, r: (0, 0)),
                  pl.BlockSpec((None, nb, 2 * GRID_W), lambda p, r: (p, 0, 0)),
                  pl.BlockSpec((blk, 2 * HEAD_DIM), lambda p, r: (r, p))],
        out_specs=[pl.BlockSpec((3, t, 2 * HEAD_DIM), lambda p, r: (0, 0, p)),
                   pl.BlockSpec((None, nb, 2 * GRID_W), lambda p, r: (p, 0, 0)),
                   pair_vec, pair_vec],
        out_shape=[jax.ShapeDtypeStruct((4, t, aw), F32), jax.ShapeDtypeStruct(bias_t.shape, F32),
                   jax.ShapeDtypeStruct((npair, 1, 2 * HEAD_DIM), F32),
                   jax.ShapeDtypeStruct((npair, 1, 2 * HEAD_DIM), F32)],
        scratch_shapes=[pltpu.VMEM((t, 2 * HEAD_DIM), MXU_DTYPE), pltpu.VMEM((t, 2 * HEAD_DIM), MXU_DTYPE)],
        compiler_params=_cparams(("parallel", "arbitrary")),
    )(z4, z4, z4, qg2, kg2, bias_t, dya)


def _attn_fwd_old(z4, qg2, kg2, bias_t):
    _, t, aw = z4.shape
    rows = t // GRID_W
    npair = aw // (2 * HEAD_DIM)
    nkeys = WIN_H * GRID_W
    nb = bias_t.shape[1]

    def body(q_ref, k_ref, v_ref, qg_ref, kg_ref, b_ref, o_ref, kn_ref):
        r = pl.program_id(1)

        @pl.when(r == 0)
        def _():
            for e in range(2):
                sl = slice(e * HEAD_DIM, (e + 1) * HEAD_DIM)
                kn_ref[e] = _rms(k_ref[:, sl], kg_ref[:, sl])[0]

        key0, bias0 = _attn_geometry(r, rows)
        outs = []
        for e in range(2):
            sl = slice(e * HEAD_DIM, (e + 1) * HEAD_DIM)
            qn = _rms(q_ref[:, sl], qg_ref[:, sl])[0]
            kb = kn_ref[e, pl.ds(key0, nkeys), :]
            vb = v_ref[pl.ds(key0, nkeys), sl]
            pt = _attn_scores(qn, kb, b_ref[e, pl.ds(bias0, nkeys), :])
            outs.append(lax.dot_general(pt.astype(MXU_DTYPE), vb.astype(MXU_DTYPE), (((0,), (0,)), ((), ())),
                                        preferred_element_type=F32))
        o_ref[...] = jnp.concatenate(outs, axis=1)

    return pl.pallas_call(
        body, name="attn_fwd", grid=(npair, rows),
        in_specs=[pl.BlockSpec((None, GRID_W, 2 * HEAD_DIM), lambda p, r: (0, r, p)),
                  pl.BlockSpec((None, t, 2 * HEAD_DIM), lambda p, r: (1, 0, p)),
                  pl.BlockSpec((None, t, 2 * HEAD_DIM), lambda p, r: (2, 0, p)),
                  pl.BlockSpec((1, 2 * HEAD_DIM), lambda p, r: (0, 0)),
                  pl.BlockSpec((1, 2 * HEAD_DIM), lambda p, r: (0, 0)),
                  pl.BlockSpec((2, nb, GRID_W), lambda p, r: (p, 0, 0))],
        out_specs=pl.BlockSpec((GRID_W, 2 * HEAD_DIM), lambda p, r: (r, p)),
        out_shape=jax.ShapeDtypeStruct((t, aw), F32),
        scratch_shapes=[pltpu.VMEM((2, t, HEAD_DIM), F32)],
        compiler_params=_cparams(("parallel", "arbitrary")),
    )(z4, z4, z4, qg2, kg2, bias_t)


def _attn_bwd_old(z4, qg2, kg2, bias_t, dya):
    _, t, aw = z4.shape
    rows = t // GRID_W
    npair = aw // (2 * HEAD_DIM)
    nkeys = WIN_H * GRID_W
    nb = bias_t.shape[1]
    scale = 1.0 / math.sqrt(HEAD_DIM)

    def body(q_ref, k_ref, v_ref, qg_ref, kg_ref, b_ref, do_ref,
             dq_ref, dk_ref, dv_ref, db_ref, dqg_ref, dkg_ref, kn_ref, dkn_ref, dva_ref):
        r = pl.program_id(1)

        @pl.when(r == 0)
        def _():
            for e in range(2):
                sl = slice(e * HEAD_DIM, (e + 1) * HEAD_DIM)
                kn_ref[e] = _rms(k_ref[:, sl], kg_ref[:, sl])[0]
            dkn_ref[...] = jnp.zeros_like(dkn_ref)
            dva_ref[...] = jnp.zeros_like(dva_ref)
            db_ref[...] = jnp.zeros_like(db_ref)
            dqg_ref[...] = jnp.zeros_like(dqg_ref)

        key0, bias0 = _attn_geometry(r, rows)
        dqs, dqgs = [], []
        for e in range(2):
            sl = slice(e * HEAD_DIM, (e + 1) * HEAD_DIM)
            q = q_ref[:, sl]
            qn = _rms(q, qg_ref[:, sl])[0]
            kb = kn_ref[e, pl.ds(key0, nkeys), :]
            vb = v_ref[pl.ds(key0, nkeys), sl]
            do = do_ref[:, sl]
            pt = _attn_scores(qn, kb, b_ref[e, pl.ds(bias0, nkeys), :])
            dva_ref[e, pl.ds(key0, nkeys), :] += lax.dot_general(
                pt.astype(MXU_DTYPE), do.astype(MXU_DTYPE), (((1,), (0,)), ((), ())), preferred_element_type=F32)
            dpt = lax.dot_general(vb.astype(MXU_DTYPE), do.astype(MXU_DTYPE), (((1,), (1,)), ((), ())),
                                  preferred_element_type=F32)
            dst = pt * (dpt - jnp.sum(pt * dpt, axis=0, keepdims=True))
            db_ref[e, pl.ds(bias0, nkeys), :] += dst
            dsb = dst.astype(MXU_DTYPE)
            dqn = scale * lax.dot_general(dsb, kb.astype(MXU_DTYPE), (((0,), (0,)), ((), ())),
                                          preferred_element_type=F32)
            dkn_ref[e, pl.ds(key0, nkeys), :] += scale * lax.dot_general(
                dsb, qn.astype(MXU_DTYPE), (((1,), (0,)), ((), ())), preferred_element_type=F32)
            dq, dqg = _rms_bwd(q, qg_ref[:, sl], dqn)
            dqs.append(dq)
            dqgs.append(jnp.sum(dqg, axis=0, keepdims=True))
        dq_ref[...] = jnp.concatenate(dqs, axis=1)
        dqg_ref[...] += jnp.concatenate(dqgs, axis=1)

        @pl.when(r == rows - 1)
        def _():
            dks, dkgs = [], []
            for e in range(2):
                sl = slice(e * HEAD_DIM, (e + 1) * HEAD_DIM)
                dk, dkg = _rms_bwd(k_ref[:, sl], kg_ref[:, sl], dkn_ref[e])
                dks.append(dk)
                dkgs.append(jnp.sum(dkg, axis=0, keepdims=True))
            dk_ref[...] = jnp.concatenate(dks, axis=1)
            dv_ref[...] = jnp.concatenate([dva_ref[0], dva_ref[1]], axis=1)
            dkg_ref[...] = jnp.concatenate(dkgs, axis=1)

    pair_vec = pl.BlockSpec((None, 1, 2 * HEAD_DIM), lambda p, r: (p, 0, 0))
    return pl.pallas_call(
        body, name="attn_bwd", grid=(npair, rows),
        in_specs=[pl.BlockSpec((None, GRID_W, 2 * HEAD_DIM), lambda p, r: (0, r, p)),
                  pl.BlockSpec((None, t, 2 * HEAD_DIM), lambda p, r: (1, 0, p)),
                  pl.BlockSpec((None, t, 2 * HEAD_DIM), lambda p, r: (2, 0, p)),
                  pl.BlockSpec((1, 2 * HEAD_DIM), lambda p, r: (0, 0)),
                  pl.BlockSpec((1, 2 * HEAD_DIM), lambda p, r: (0, 0)),
                  pl.BlockSpec((2, nb, GRID_W), lambda p, r: (p, 0, 0)),
                  pl.BlockSpec((GRID_W, 2 * HEAD_DIM), lambda p, r: (r, p))],
        out_specs=[pl.BlockSpec((GRID_W, 2 * HEAD_DIM), lambda p, r: (r, p)),
                   pl.BlockSpec((t, 2 * HEAD_DIM), lambda p, r: (0, p)),
                   pl.BlockSpec((t, 2 * HEAD_DIM), lambda p, r: (0, p)),
                   pl.BlockSpec((2, nb, GRID_W), lambda p, r: (p, 0, 0)),
                   pair_vec, pair_vec],
        out_shape=[jax.ShapeDtypeStruct((t, aw), F32), jax.ShapeDtypeStruct((t, aw), F32),
                   jax.ShapeDtypeStruct((t, aw), F32), jax.ShapeDtypeStruct(bias_t.shape, F32),
                   jax.ShapeDtypeStruct((npair, 1, 2 * HEAD_DIM), F32),
                   jax.ShapeDtypeStruct((npair, 1, 2 * HEAD_DIM), F32)],
        scratch_shapes=[pltpu.VMEM((2, t, HEAD_DIM), F32), pltpu.VMEM((2, t, HEAD_DIM), F32),
                        pltpu.VMEM((2, t, HEAD_DIM), F32)],
        compiler_params=_cparams(("parallel", "arbitrary")),
    )(z4, z4, z4, qg2, kg2, bias_t, dya)


def _s5_mats(a_re, a_im, b_re, b_im, c_re, c_im, log_step, d_skip):
    nd, g, p = a_re.shape
    c = b_re.shape[-1]
    L = S5_CHUNK
    lr = jnp.minimum(a_re, -1e-4)
    li = a_im
    dt = jnp.exp(log_step)[..., None]
    n = jnp.arange(L + 1, dtype=F32)[:, None, None, None]
    mag = jnp.exp(n * (lr * dt)[None])
    ang = n * (li * dt)[None]
    pw_r, pw_i = mag * jnp.cos(ang), mag * jnp.sin(ang)
    den = lr * lr + li * li
    nr, ni = pw_r[1] - 1.0, pw_i[1]
    cr, ci = (nr * lr + ni * li) / den, (ni * lr - nr * li) / den
    bb_r = cr[..., None] * b_re - ci[..., None] * b_im
    bb_i = cr[..., None] * b_im + ci[..., None] * b_re
    pb_r = pw_r[..., None] * bb_r[None] - pw_i[..., None] * bb_i[None]
    pb_i = pw_r[..., None] * bb_i[None] + pw_i[..., None] * bb_r[None]
    kern = (jnp.einsum('dgop,ndgpi->ndgoi', c_re, pb_r[:L], precision=HI)
            - jnp.einsum('dgop,ndgpi->ndgoi', c_im, pb_i[:L], precision=HI))
    j = jnp.arange(L)
    e0 = ((j[None, None, :] - j[None, :, None]) == j[:, None, None]).astype(F32)
    e1 = ((j[None, :, None] - j[None, None, :]) == j[:, None, None]).astype(F32)
    mt = (jnp.einsum('njt,ngoi->gjito', e0, kern[:, 0], precision=HI)
          + jnp.einsum('njt,ngoi->gjito', e1, kern[:, 1], precision=HI))
    eye = jnp.eye(L, dtype=F32)[:, None, :, None] * jnp.eye(c, dtype=F32)[None, :, None, :]
    mt = mt + d_skip.reshape(g, 1, c, 1, 1) * eye[None]
    mt = mt.reshape(g, L * c, L * c)

    def rows_jc(x):
        return x.transpose(1, 0, 3, 2).reshape(g, L * c, p)
    ws = jnp.concatenate([rows_jc(pb_r[:L, 0][::-1]), rows_jc(pb_i[:L, 0][::-1]),
                          rows_jc(pb_r[:L, 1]), rows_jc(pb_i[:L, 1])], axis=-1)
    cp_r = c_re[None] * pw_r[:, :, :, None, :] - c_im[None] * pw_i[:, :, :, None, :]
    cp_i = c_re[None] * pw_i[:, :, :, None, :] + c_im[None] * pw_r[:, :, :, None, :]

    def cols_to(x):
        return x.transpose(1, 3, 0, 2).reshape(g, p, L * c)
    wo = jnp.concatenate([cols_to(cp_r[1:, 0]), cols_to(-cp_i[1:, 0]),
                          cols_to(cp_r[1:, 1][::-1]), cols_to(-cp_i[1:, 1][::-1])], axis=1)
    lr16, li16 = pw_r[L], pw_i[L]
    fa = jnp.concatenate([lr16[0], lr16[0], lr16[1], lr16[1]], axis=-1)
    fb = jnp.concatenate([-li16[0], li16[0], -li16[1], li16[1]], axis=-1)
    return mt, ws, wo, fa, fb


def _gmm(name, a, b, contract, a_stacked=False, b_stacked=False, o_stacked=False, add=None):
    w = S5_CHUNK * SSM_GROUP_CH
    g = (a.shape[0] if a_stacked else a.shape[1] // w)
    dn = {'nn': (((1,), (0,)), ((), ())), 'nt': (((1,), (1,)), ((), ())), 'tn': (((0,), (0,)), ((), ()))}[contract]

    def spec(arr, stacked):
        if stacked:
            return pl.BlockSpec((None,) + arr.shape[1:], lambda i: (i, 0, 0))
        return pl.BlockSpec((arr.shape[0], w), lambda i: (0, i))

    m = (a.shape[1] if a_stacked else a.shape[0]) if contract != 'tn' else w
    n = w
    if o_stacked:
        o_spec = pl.BlockSpec((None, m, n), lambda i: (i, 0, 0))
        o_shape = (g, m, n)
    else:
        o_spec = pl.BlockSpec((m, n), lambda i: (0, i))
        o_shape = (m, g * n)
    has_add = add is not None

    def body(*refs):
        if has_add:
            a_ref, b_ref, add_ref, o_ref = refs
        else:
            a_ref, b_ref, o_ref = refs
        r = lax.dot_general(a_ref[...], b_ref[...], dn, precision=HI, preferred_element_type=F32)
        if has_add:
            r = r + add_ref[...]
        o_ref[...] = r

    in_specs = [spec(a, a_stacked), spec(b, b_stacked)] + ([o_spec] if has_add else [])
    return pl.pallas_call(
        body, name=name, grid=(g,), in_specs=in_specs, out_specs=o_spec,
        out_shape=jax.ShapeDtypeStruct(o_shape, F32), compiler_params=_cparams(("parallel",)),
    )(*((a, b) + ((add,) if has_add else ())))


def _s5_scan(name, s, fa, fb, rev0, xin=None):
    nk, g, w = s.shape
    hw = w // 2
    gb = min(g, 16)
    with_acc = xin is not None

    def body(*refs):
        if with_acc:
            s_ref, a_ref, b_ref, x_ref, o_ref, pa_ref, pb_ref = refs
        else:
            s_ref, a_ref, b_ref, o_ref = refs
        fa0, fb0, fa1, fb1 = a_ref[:, :hw], b_ref[:, :hw], a_ref[:, hw:], b_ref[:, hw:]

        def step(i, carry):
            x0, x1, pa0, pb0, pa1, pb1 = carry
            k0 = (nk - 1 - i) if rev0 else i
            k1 = i if rev0 else (nk - 1 - i)
            o_ref[k0, :, :hw] = x0
            o_ref[k1, :, hw:] = x1
            if with_acc:
                xi0, xi1 = x_ref[k0, :, :hw], x_ref[k1, :, hw:]
                pa0 = pa0 + x0 * xi0
                pb0 = pb0 + x0 * pltpu.roll(xi0, hw // 2, 1)
                pa1 = pa1 + x1 * xi1
                pb1 = pb1 + x1 * pltpu.roll(xi1, hw // 2, 1)
            x0 = fa0 * x0 + fb0 * pltpu.roll(x0, hw // 2, 1) + s_ref[k0, :, :hw]
            x1 = fa1 * x1 + fb1 * pltpu.roll(x1, hw // 2, 1) + s_ref[k1, :, hw:]
            return x0, x1, pa0, pb0, pa1, pb1

        z = jnp.zeros((gb, hw), F32)
        res = lax.fori_loop(0, nk, step, (z, z, z, z, z, z))
        if with_acc:
            pa_ref[:, :hw] = res[2]
            pb_ref[:, :hw] = res[3]
            pa_ref[:, hw:] = res[4]
            pb_ref[:, hw:] = res[5]

    seq = pl.BlockSpec((nk, gb, w), lambda i: (0, i, 0))
    vec = pl.BlockSpec((gb, w), lambda i: (i, 0))
    in_specs = [seq, vec, vec] + ([seq] if with_acc else [])
    out_specs = [seq] + ([vec, vec] if with_acc else [])
    out_shape = [jax.ShapeDtypeStruct((nk, g, w), F32)] + (
        [jax.ShapeDtypeStruct((g, w), F32)] * 2 if with_acc else [])
    return pl.pallas_call(
        body, name=name, grid=(g // gb,), in_specs=in_specs, out_specs=out_specs, out_shape=out_shape,
        compiler_params=_cparams(("parallel",)),
    )(*((s, fa, fb) + ((xin,) if with_acc else ())))


def _to_groups(u):
    t, sw = u.shape
    g = sw // SSM_GROUP_CH
    return u.reshape(t // S5_CHUNK, S5_CHUNK, g, SSM_GROUP_CH).transpose(0, 2, 1, 3).reshape(t // S5_CHUNK, -1)


def _from_groups(y, sw):
    nk = y.shape[0]
    g = sw // SSM_GROUP_CH
    return y.reshape(nk, g, S5_CHUNK, SSM_GROUP_CH).transpose(0, 2, 1, 3).reshape(nk * S5_CHUNK, sw)


def _s5_fwd(u2, mats):
    mt, ws, wo, fa, fb = mats
    nk = u2.shape[0]
    g = mt.shape[0]
    y_intra = _gmm("s5_intra", u2, mt, 'nn', b_stacked=True)
    s = _gmm("s5_chunk_state", u2, ws, 'nn', b_stacked=True)
    (xin,) = _s5_scan("s5_scan", s.reshape(nk, g, -1), fa, fb, False)
    xin = xin.reshape(nk, -1)
    return _gmm("s5_inter", xin, wo, 'nn', b_stacked=True, add=y_intra), xin


def _s5_bwd(u2, xin, mats, dy2):
    mt, ws, wo, fa, fb = mats
    nk = u2.shape[0]
    g = mt.shape[0]
    dxin = _gmm("s5_dxin", dy2, wo, 'nt', b_stacked=True)
    ds, pa, pb = _s5_scan("s5_scan_adj", dxin.reshape(nk, g, -1), fa, -fb, True, xin=xin.reshape(nk, g, -1))
    ds = ds.reshape(nk, -1)
    du_a = _gmm("s5_du_intra", dy2, mt, 'nt', b_stacked=True)
    du2 = _gmm("s5_du_state", ds, ws, 'nt', b_stacked=True, add=du_a)
    dmt = _gmm("s5_dmt", u2, dy2, 'tn', o_stacked=True)
    dws = _gmm("s5_dws", u2, ds, 'tn', o_stacked=True)
    dwo = _gmm("s5_dwo", xin, dy2, 'tn', o_stacked=True)
    return du2, (dmt, dws, dwo, pa, pb)


def _local_step(x, target, big, small):
    t, d = x.shape
    w_in4, w_glu, w_out, w_gate4, w_up4, w_down4 = big
    aw = w_in4.shape[2]
    sw = w_glu.shape[0]
    nh = aw // HEAD_DIM
    ffs = w_gate4.shape[2]
    row = lambda v: v.reshape(1, -1)
    g_mix, g_ffn = row(small['g_mix']), row(small['g_ffn'])
    g_oa, g_os, b_glu = row(small['g_out_attn']), row(small['g_out_ssm']), row(small['b_glu'])
    qg2 = jnp.tile(row(small['q_gain']), (1, 2))
    kg2 = jnp.tile(row(small['k_gain']), (1, 2))

    (h,) = _ew("rms_mix", lambda xv, g: _rms(xv, g)[0], [('r', x), ('c', g_mix)], [('r', d, MXU_DTYPE)])
    z4 = _mm("in_proj", h, w_in4, contract='nn', b_mode='b', o_mode='b')
    bias_t = _bias_table(small['rpb'])
    ya = _attn_fwd(z4, qg2, kg2, bias_t)
    s5_params = tuple(small[n] for n in ('ssm_a_re', 'ssm_a_im', 'ssm_b_re', 'ssm_b_im', 'ssm_c_re', 'ssm_c_im',
                                         'ssm_log_step', 'ssm_d'))
    mats, mats_vjp = jax.vjp(_s5_mats, *s5_params)
    u2 = _to_groups(z4[3])
    ypre2, xin = _s5_fwd(u2, mats)
    ypre = _from_groups(ypre2, sw)
    (yb,) = _ew("gelu", _gelu, [('r', ypre)], [('r', sw, MXU_DTYPE)])
    a_glu = _mm("glu_proj", yb, w_glu, contract='nn')

    def mix_out(yav, ypv, av, bg, goa, gos):
        ys = _gelu(ypv) * _sigmoid(av + bg)
        return jnp.concatenate([_rms(yav, goa)[0], _rms(ys, gos)[0]], axis=1)
    (ycat,) = _ew("mix_out", mix_out, [('r', ya), ('r', ypre), ('r', a_glu), ('c', b_glu), ('c', g_oa), ('c', g_os)],
                  [('r', aw + sw, MXU_DTYPE)])
    x1 = _mm("out_proj", ycat, w_out, contract='nn', add=x)
    (h2,) = _ew("rms_ffn", lambda xv, g: _rms(xv, g)[0], [('r', x1), ('c', g_ffn)], [('r', d, MXU_DTYPE)])
    gate4 = _mm("ffn_gate", h2, w_gate4, contract='nn', b_mode='b', o_mode='b', tn=ffs)
    up4 = _mm("ffn_up", h2, w_up4, contract='nn', b_mode='b', o_mode='b', tn=ffs)
    gate_f, up_f = gate4.reshape(4 * t, ffs), up4.reshape(4 * t, ffs)
    (act,) = _ew("swiglu", lambda gv, uv: gv * _sigmoid(gv) * uv, [('r', gate_f), ('r', up_f)],
                 [('r', ffs, MXU_DTYPE)])
    act4 = act.reshape(4, t, ffs)
    x2 = _mm("ffn_down", act4, w_down4, contract='nn', a_mode='c', b_mode='c', add=x1, tk=ffs)

    def loss_fn(xv, tv):
        diff = xv - tv
        return diff * (1.0 / d), diff * diff
    dx2, sq = _ew("loss", loss_fn, [('r', x2), ('r', target)], [('r', d, F32), ('a', d)])

    dact4 = _mm("ffn_down_dx", dx2, w_down4, contract='nt', b_mode='b', o_mode='b', tn=ffs)
    d_w_down4 = _mm("ffn_down_dw", act4, dx2, contract='tn', a_mode='b', o_mode='b', tm=ffs)

    def swiglu_bwd(dav, gv, uv):
        s = _sigmoid(gv)
        return dav * uv * s * (1.0 + gv * (1.0 - s)), dav * gv * s
    dgate, dup = _ew("swiglu_bwd", swiglu_bwd, [('r', dact4.reshape(4 * t, ffs)), ('r', gate_f), ('r', up_f)],
                     [('r', ffs, MXU_DTYPE), ('r', ffs, MXU_DTYPE)])
    dgate4, dup4 = dgate.reshape(4, t, ffs), dup.reshape(4, t, ffs)
    dh2 = _mm("ffn_gate_dx", dgate4, w_gate4, contract='nt', a_mode='c', b_mode='c', tk=ffs)
    dh2 = _mm("ffn_up_dx", dup4, w_up4, contract='nt', a_mode='c', b_mode='c', add=dh2, tk=ffs)
    d_w_gate4 = _mm("ffn_gate_dw", h2, dgate4, contract='tn', b_mode='b', o_mode='b', tn=ffs)
    d_w_up4 = _mm("ffn_up_dw", h2, dup4, contract='tn', b_mode='b', o_mode='b', tn=ffs)

    def rms_res_bwd(xv, g, dyv, resv):
        dx, dg = _rms_bwd(xv, g, dyv)
        return resv + dx, dg
    dx1, d_g_ffn = _ew("rms_ffn_bwd", rms_res_bwd, [('r', x1), ('c', g_ffn), ('r', dh2), ('r', dx2)],
                       [('r', d, F32), ('a', d)])

    dycat = _mm("out_proj_dx", dx1, w_out, contract='nt')
    d_w_out = _mm("out_proj_dw", ycat, dx1, contract='tn')

    def mix_out_bwd(yav, ypv, av, bg, goa, gos, dca, dcs):
        dya, dgoa = _rms_bwd(yav, goa, dca)
        y = _gelu(ypv)
        s = _sigmoid(av + bg)
        dys, dgos = _rms_bwd(y * s, gos, dcs)
        da = dys * y * s * (1.0 - s)
        return dya, da, dys * s, dgoa, dgos, da
    dya, da, dy_direct, d_g_oa, d_g_os, d_b_glu = _ew(
        "mix_out_bwd", mix_out_bwd,
        [('r', ya), ('r', ypre), ('r', a_glu), ('c', b_glu), ('c', g_oa), ('c', g_os),
         ('r', dycat, 0, aw), ('r', dycat, 1, sw)],
        [('r', aw, F32), ('r', sw, MXU_DTYPE), ('r', sw, F32), ('a', aw), ('a', sw), ('a', sw)])
    dy = _mm("glu_proj_dx", da, w_glu, contract='nt', add=dy_direct)
    d_w_glu = _mm("glu_proj_dw", yb, da, contract='tn')
    (dypre,) = _ew("gelu_bwd", lambda dyv, ypv: dyv * _gelu_grad(ypv), [('r', dy), ('r', ypre)], [('r', sw, F32)])

    du2, dmats = _s5_bwd(u2, xin, mats, _to_groups(dypre))
    d_s5 = mats_vjp(dmats)
    du = _from_groups(du2, sw)
    dz4, dbias_t, dqg, dkg = _attn_bwd(z4, qg2, kg2, bias_t, dya)
    d_rpb = _bias_table_grad(dbias_t)
    fold = lambda v: v.reshape(-1, 2, HEAD_DIM).sum(axis=(0, 1))
    dz4 = dz4.at[3].set(du)

    dh = _mm("in_proj_dx", dz4, w_in4, contract='nt', a_mode='c', b_mode='c')
    d_w_in4 = _mm("in_proj_dw", h, dz4, contract='tn', b_mode='b', o_mode='b')
    dx, d_g_mix = _ew("rms_mix_bwd", rms_res_bwd, [('r', x), ('c', g_mix), ('r', dh), ('r', dx1)],
                      [('r', d, F32), ('a', d)])

    colsum = lambda v: v.sum(axis=0)
    d_small = {
        'g_mix': colsum(d_g_mix), 'q_gain': fold(dqg), 'k_gain': fold(dkg), 'rpb': d_rpb,
        'ssm_a_re': d_s5[0], 'ssm_a_im': d_s5[1], 'ssm_b_re': d_s5[2], 'ssm_b_im': d_s5[3],
        'ssm_c_re': d_s5[4], 'ssm_c_im': d_s5[5], 'ssm_log_step': d_s5[6], 'ssm_d': d_s5[7],
        'b_glu': colsum(d_b_glu), 'g_out_attn': colsum(d_g_oa), 'g_out_ssm': colsum(d_g_os), 'g_ffn': colsum(d_g_ffn),
    }
    d_big = (d_w_in4, d_w_glu, d_w_out, d_w_gate4, d_w_up4, d_w_down4)
    return jnp.sum(sq), dx, d_big, d_small


ANY = pl.BlockSpec(memory_space=pl.ANY)


def _place():
    x, y, c = lax.axis_index("x"), lax.axis_index("y"), lax.axis_index("c")
    other_chips = [(1 - x, y), (x, 1 - y), (1 - x, 1 - y)]
    return x, y, c, 2 * x + y, (x, y, 1 - c), other_chips


def _gather_chips(name, arrs):
    n = len(arrs)

    def body(*refs):
        ins, outs = refs[:n], refs[n:2 * n]
        send_sems, recv_sems, local_sems = refs[2 * n:]
        x, y, c, me, sibling, chips = _place()

        def remote(a, k, src, dst, to):
            return pltpu.make_async_remote_copy(src_ref=src, dst_ref=dst, send_sem=send_sems.at[a, k],
                                                recv_sem=recv_sems.at[a, k], device_id=to, device_id_type=MESH)

        started = []
        for a in range(n):
            own = pltpu.make_async_copy(ins[a], outs[a].at[me], local_sems.at[a])
            own.start()
            started.append(own)
        sends = []
        for j, (px, py) in enumerate(chips):
            for a in range(n):
                cp = remote(a, j, ins[a].at[c], outs[a].at[me, c], (px, py, c))
                cp.start()
                sends.append(cp)
        for j, (px, py) in enumerate(chips):
            for a in range(n):
                landed = outs[a].at[2 * px + py, c]
                remote(a, j, landed, landed, (px, py, c)).wait_recv()
                cp = remote(a, 3 + j, landed, landed, sibling)
                cp.start()
                sends.append(cp)
        for j, (px, py) in enumerate(chips):
            for a in range(n):
                theirs = outs[a].at[2 * px + py, 1 - c]
                remote(a, 3 + j, theirs, theirs, sibling).wait_recv()
        for cp in sends:
            cp.wait_send()
        for cp in started:
            cp.wait()

    return pl.pallas_call(
        body, name=name, in_specs=[ANY] * n, out_specs=[ANY] * n,
        out_shape=[jax.ShapeDtypeStruct((N_CHIPS,) + a.shape, a.dtype) for a in arrs],
        scratch_shapes=[pltpu.SemaphoreType.DMA((n, 6)), pltpu.SemaphoreType.DMA((n, 6)),
                        pltpu.SemaphoreType.DMA((n,))],
    )(*arrs)


def _swap_halves(name, parts):
    n = len(parts)

    def body(*refs):
        ins, outs = refs[:n], refs[n:2 * n]
        send_sems, recv_sems = refs[2 * n:]
        x, y, c, me, sibling, chips = _place()
        cps = []
        for a in range(n):
            cp = pltpu.make_async_remote_copy(src_ref=ins[a].at[:, 1 - c], dst_ref=outs[a], send_sem=send_sems.at[a],
                                              recv_sem=recv_sems.at[a], device_id=sibling, device_id_type=MESH)
            cp.start()
            cps.append(cp)
        for cp in cps:
            cp.wait()

    return pl.pallas_call(
        body, name=name, in_specs=[ANY] * n, out_specs=[ANY] * n,
        out_shape=[jax.ShapeDtypeStruct((N_CHIPS,) + p.shape[2:], p.dtype) for p in parts],
        scratch_shapes=[pltpu.SemaphoreType.DMA((n,)), pltpu.SemaphoreType.DMA((n,))],
    )(*parts)


def _scatter_chips(name, sums):
    n = len(sums)

    def body(*refs):
        ins, outs = refs[:n], refs[n:2 * n]
        send_sems, recv_sems, local_sems = refs[2 * n:]
        x, y, c, me, sibling, chips = _place()
        started, sends = [], []
        for a in range(n):
            own = pltpu.make_async_copy(ins[a].at[me], outs[a].at[me], local_sems.at[a])
            own.start()
            started.append(own)
        for j, (px, py) in enumerate(chips):
            for a in range(n):
                cp = pltpu.make_async_remote_copy(
                    src_ref=ins[a].at[2 * px + py], dst_ref=outs[a].at[me], send_sem=send_sems.at[a, j],
                    recv_sem=recv_sems.at[a, j], device_id=(px, py, c), device_id_type=MESH)
                cp.start()
                sends.append(cp)
        for j, (px, py) in enumerate(chips):
            for a in range(n):
                slot = outs[a].at[2 * px + py]
                pltpu.make_async_remote_copy(
                    src_ref=slot, dst_ref=slot, send_sem=send_sems.at[a, j], recv_sem=recv_sems.at[a, j],
                    device_id=(px, py, c), device_id_type=MESH).wait_recv()
        for cp in sends:
            cp.wait_send()
        for cp in started:
            cp.wait()

    return pl.pallas_call(
        body, name=name, in_specs=[ANY] * n, out_specs=[ANY] * n,
        out_shape=[jax.ShapeDtypeStruct(s.shape, s.dtype) for s in sums],
        scratch_shapes=[pltpu.SemaphoreType.DMA((n, 3)), pltpu.SemaphoreType.DMA((n, 3)),
                        pltpu.SemaphoreType.DMA((n,))],
    )(*sums)


def _join_halves(name, halves):
    n = len(halves)

    def body(*refs):
        ins, outs = refs[:n], refs[n:2 * n]
        send_sems, recv_sems, local_sems = refs[2 * n:]
        x, y, c, me, sibling, chips = _place()
        started, cps = [], []
        for a in range(n):
            own = pltpu.make_async_copy(ins[a], outs[a].at[c], local_sems.at[a])
            own.start()
            started.append(own)
            cp = pltpu.make_async_remote_copy(src_ref=ins[a], dst_ref=outs[a].at[c], send_sem=send_sems.at[a],
                                              recv_sem=recv_sems.at[a], device_id=sibling, device_id_type=MESH)
            cp.start()
            cps.append(cp)
        for a in range(n):
            theirs = outs[a].at[1 - c]
            pltpu.make_async_remote_copy(src_ref=theirs, dst_ref=theirs, send_sem=send_sems.at[a],
                                         recv_sem=recv_sems.at[a], device_id=sibling, device_id_type=MESH).wait_recv()
        for cp in cps:
            cp.wait_send()
        for cp in started:
            cp.wait()

    return pl.pallas_call(
        body, name=name, in_specs=[ANY] * n, out_specs=[ANY] * n,
        out_shape=[jax.ShapeDtypeStruct((2,) + h.shape, h.dtype) for h in halves],
        scratch_shapes=[pltpu.SemaphoreType.DMA((n,)), pltpu.SemaphoreType.DMA((n,)),
                        pltpu.SemaphoreType.DMA((n,))],
    )(*halves)


def _row_tile(r, want=256):
    t = (min(r, want) // SUBLANES) * SUBLANES
    while r % t:
        t -= SUBLANES
    return t


def _add_own_half(name, part, got, c, out_dtype):
    _, _, r, cols = part.shape
    tr = _row_tile(r)

    def body(c_ref, p_ref, g_ref, o_ref):
        o_ref[...] = (p_ref[...] + g_ref[...]).astype(o_ref.dtype)

    return pl.pallas_call(
        body, name=name,
        grid_spec=pltpu.PrefetchScalarGridSpec(
            num_scalar_prefetch=1, grid=(N_CHIPS, r // tr),
            in_specs=[pl.BlockSpec((None, None, tr, cols), lambda s, i, c_ref: (s, c_ref[0], i, 0)),
                      pl.BlockSpec((None, tr, cols), lambda s, i, c_ref: (s, i, 0))],
            out_specs=pl.BlockSpec((None, tr, cols), lambda s, i, c_ref: (s, i, 0))),
        out_shape=jax.ShapeDtypeStruct(got.shape, out_dtype),
        compiler_params=_cparams(("parallel", "parallel")),
    )(c.reshape(1).astype(jnp.int32), part, got)


def _sum_chips(name, got):
    _, r, cols = got.shape
    tr = _row_tile(r)

    def body(r0, r1, r2, r3, o_ref):
        f = lambda ref: ref[...].astype(F32)
        o_ref[...] = ((f(r0) + f(r1)) + f(r2)) + f(r3)

    return pl.pallas_call(
        body, name=name, grid=(r // tr,),
        in_specs=[pl.BlockSpec((None, tr, cols), functools.partial(lambda s, i: (s, i, 0), s)) for s in range(N_CHIPS)],
        out_specs=pl.BlockSpec((tr, cols), lambda i: (i, 0)),
        out_shape=jax.ShapeDtypeStruct((r, cols), F32),
        compiler_params=_cparams(("parallel",)),
    )(got, got, got, got)


def _adamw(name, w, g, m, v):
    def fn(wv, gv, mv, vv):
        mv = ADAM_B1 * mv + (1.0 - ADAM_B1) * gv
        vv = ADAM_B2 * vv + (1.0 - ADAM_B2) * (gv * gv)
        m_hat = mv / (1.0 - ADAM_B1 ** ADAM_STEP)
        v_hat = vv / (1.0 - ADAM_B2 ** ADAM_STEP)
        return -ADAM_LR * (m_hat / (jnp.sqrt(v_hat) + ADAM_EPS) + ADAM_WD * wv), mv, vv
    cols = w.shape[1]
    return _ew(name, fn, [('r', w), ('r', g), ('r', m), ('r', v)], [('r', cols, F32)] * 3, tr=_row_tile(w.shape[0], 128))


SMALL_ROWS_ALIGN = 2 * N_CHIPS * SUBLANES


def _pack_small(d):
    flat = jnp.concatenate([d[n].reshape(-1).astype(F32) for n in SMALL_NAMES])
    rows = -(-flat.shape[0] // (LANES * SMALL_ROWS_ALIGN)) * SMALL_ROWS_ALIGN
    return jnp.pad(flat, (0, rows * LANES - flat.shape[0])).reshape(rows, LANES)


def _unpack_small(packed, like):
    flat = packed.reshape(-1)
    out, off = {}, 0
    for n in SMALL_NAMES:
        size = like[n].size
        out[n] = flat[off:off + size].reshape(like[n].shape)
        off += size
    return out


def kernel(x, g_mix, w_in, q_gain, k_gain, rpb, ssm_a_re, ssm_a_im, ssm_b_re, ssm_b_im, ssm_c_re, ssm_c_im, ssm_log_step, ssm_d, w_glu, b_glu, g_out_attn, g_out_ssm, w_out, g_ffn, w_ffn_gate, w_ffn_up, w_ffn_down, loss_target, m_g_mix, m_w_in, m_q_gain, m_k_gain, m_rpb, m_ssm_a_re, m_ssm_a_im, m_ssm_b_re, m_ssm_b_im, m_ssm_c_re, m_ssm_c_im, m_ssm_log_step, m_ssm_d, m_w_glu, m_b_glu, m_g_out_attn, m_g_out_ssm, m_w_out, m_g_ffn, m_w_ffn_gate, m_w_ffn_up, m_w_ffn_down, v_g_mix, v_w_in, v_q_gain, v_k_gain, v_rpb, v_ssm_a_re, v_ssm_a_im, v_ssm_b_re, v_ssm_b_im, v_ssm_c_re, v_ssm_c_im, v_ssm_log_step, v_ssm_d, v_w_glu, v_b_glu, v_g_out_attn, v_g_out_ssm, v_w_out, v_g_ffn, v_w_ffn_gate, v_w_ffn_up, v_w_ffn_down):
    given = dict(locals())
    w = {n: given[n][0] for n in WEIGHT_NAMES}
    mom = {n: given["m_" + n][0] for n in WEIGHT_NAMES}
    var = {n: given["v_" + n][0] for n in WEIGHT_NAMES}
    d = x.shape[-1]
    c = lax.axis_index("c")

    halves = [w[n].astype(MXU_DTYPE).reshape((2, w[n].shape[0] // 2, w[n].shape[1])) for n in BIG_NAMES]
    gathered = _gather_chips("gather_weights", halves)
    big = tuple(g4.reshape((N_CHIPS, -1, g4.shape[-1])) for g4 in gathered)
    big = (big[0], big[1].reshape(-1, big[1].shape[-1]), big[2].reshape(-1, big[2].shape[-1]), big[3], big[4], big[5])

    sq, dx, d_big, d_small = _local_step(x[0], loss_target[0], big, {n: w[n] for n in SMALL_NAMES})
    loss = lax.psum(0.5 * sq / d, ("x", "y", "c"))

    parts = [g.reshape((N_CHIPS, 2, -1, g.shape[-1])) for g in d_big]
    small_part = _pack_small(d_small)
    srows = small_part.shape[0]
    parts.append(small_part.reshape(N_CHIPS, 2, srows // (2 * N_CHIPS), LANES))
    got = _swap_halves("reduce_swap_halves", parts)
    payload = [GRAD_PAYLOAD_DTYPE] * len(BIG_NAMES) + [F32]
    sums = [_add_own_half("reduce_add_%d" % a, p, gt, c, dt) for a, (p, gt, dt) in enumerate(zip(parts, got, payload))]
    got = _scatter_chips("reduce_scatter_chips", sums)
    mine = [_sum_chips("reduce_sum_%d" % a, gt) for a, gt in enumerate(got)]
    full = _join_halves("reduce_join_halves", mine)
    grad_big = {n: f.reshape(w[n].shape) for n, f in zip(BIG_NAMES, full[:-1])}
    (small_all,) = _gather_chips("gather_small", [full[-1]])
    small_all = small_all.reshape(srows, LANES)
    grad_small = _unpack_small(small_all, {n: w[n] for n in SMALL_NAMES})

    delta, new_m, new_v = {}, {}, {}
    for a, n in enumerate(BIG_NAMES):
        delta[n], new_m[n], new_v[n] = _adamw("adamw_%d" % a, w[n], grad_big[n], mom[n], var[n])
    sd, sm, sv = _adamw("adamw_small", _pack_small({n: w[n] for n in SMALL_NAMES}), small_all,
                        _pack_small({n: mom[n] for n in SMALL_NAMES}), _pack_small({n: var[n] for n in SMALL_NAMES}))
    like = {n: w[n] for n in SMALL_NAMES}
    delta.update(_unpack_small(sd, like))
    new_m.update(_unpack_small(sm, like))
    new_v.update(_unpack_small(sv, like))
    grads = {**grad_big, **grad_small}
    lead = lambda t: t[None]
    return (loss, dx[None], *[lead(grads[n]) for n in WEIGHT_NAMES], *[lead(delta[n]) for n in WEIGHT_NAMES],
            *[lead(new_m[n]) for n in WEIGHT_NAMES], *[lead(new_v[n]) for n in WEIGHT_NAMES])
```

```python
import functools
import math

import jax
import jax.numpy as jnp
from jax import lax
from jax.experimental import pallas as pl
from jax.experimental.pallas import tpu as pltpu

F32 = jnp.float32
BF16 = jnp.bfloat16
MXU_DTYPE = BF16
GRAD_PAYLOAD_DTYPE = BF16
HI = lax.Precision.HIGHEST
VMEM_LIMIT_V7X = 56 * 1024 * 1024
LANES = 128
SUBLANES = 8

GRID_W = 64
WIN_H = 8
WIN_W = 16
HEAD_DIM = 64
SSM_GROUP_CH = 16
SSM_STATE = 64
S5_CHUNK = 16
RMS_EPS = 1e-6
NEG_INF = -1e30
N_CHIPS = 4
MESH = pl.DeviceIdType.MESH

ADAM_LR = 0.001
ADAM_B1 = 0.9
ADAM_B2 = 0.999
ADAM_EPS = 1e-08
ADAM_WD = 0.01
ADAM_STEP = 10

WEIGHT_NAMES = ['g_mix', 'w_in', 'q_gain', 'k_gain', 'rpb', 'ssm_a_re', 'ssm_a_im', 'ssm_b_re', 'ssm_b_im',
                'ssm_c_re', 'ssm_c_im', 'ssm_log_step', 'ssm_d', 'w_glu', 'b_glu', 'g_out_attn', 'g_out_ssm',
                'w_out', 'g_ffn', 'w_ffn_gate', 'w_ffn_up', 'w_ffn_down']
BIG_NAMES = ['w_in', 'w_glu', 'w_out', 'w_ffn_gate', 'w_ffn_up', 'w_ffn_down']
SMALL_NAMES = [n for n in WEIGHT_NAMES if n not in BIG_NAMES]


def _cparams(sem):
    return pltpu.CompilerParams(dimension_semantics=sem, vmem_limit_bytes=VMEM_LIMIT_V7X)


def _tile(n, want):
    if n <= want:
        return n
    t = (want // LANES) * LANES
    while t >= LANES:
        if n % t == 0:
            return t
        t -= LANES
    return n


def _mm(name, a, b, *, contract, a_mode='2', b_mode='2', o_mode='2', out_dtype=F32, add=None, exact=False,
        tm=1024, tn=1024, tk=512):
    dn = {'nn': (((1,), (0,)), ((), ())), 'nt': (((1,), (1,)), ((), ())), 'tn': (((0,), (0,)), ((), ()))}[contract]
    ar, ac = a.shape[-2:]
    br, bc = b.shape[-2:]
    m, kdim = (ar, ac) if contract != 'tn' else (ac, ar)
    n = bc if contract != 'nt' else br
    assert kdim == (br if contract != 'nt' else bc), (name, a.shape, b.shape)
    nbatch = 1
    for arr, mode in ((a, a_mode), (b, b_mode)):
        if mode == 'b':
            nbatch = arr.shape[0]
    nstack = 1
    for arr, mode in ((a, a_mode), (b, b_mode)):
        if mode == 'c':
            nstack = arr.shape[0]
    tm, tn, tk = _tile(m, tm), _tile(n, tn), _tile(kdim, tk)
    nkin = kdim // tk
    nk = nstack * nkin
    grid = (nbatch, m // tm, n // tn, nk)

    def spec(mode, block, rc):
        def imap(s, i, j, kk):
            r, c = rc(i, j, kk % nkin)
            if mode == '2':
                return (r, c)
            return (s if mode == 'b' else kk // nkin, r, c)
        return pl.BlockSpec(block if mode == '2' else (None,) + block, imap)

    a_spec = spec(a_mode, (tm, tk) if contract != 'tn' else (tk, tm),
                  (lambda i, j, k: (i, k)) if contract != 'tn' else (lambda i, j, k: (k, i)))
    b_spec = spec(b_mode, (tk, tn) if contract != 'nt' else (tn, tk),
                  (lambda i, j, k: (k, j)) if contract != 'nt' else (lambda i, j, k: (j, k)))
    o_spec = spec(o_mode, (tm, tn), lambda i, j, k: (i, j))
    out_shape = (m, n) if o_mode == '2' else (nbatch, m, n)
    has_add = add is not None

    def body(*refs):
        if has_add:
            a_ref, b_ref, add_ref, o_ref, acc_ref = refs
        else:
            a_ref, b_ref, o_ref, acc_ref = refs
        k = pl.program_id(3)

        @pl.when(k == 0)
        def _():
            acc_ref[...] = jnp.zeros_like(acc_ref)

        if exact:
            acc_ref[...] += lax.dot_general(a_ref[...].astype(F32), b_ref[...].astype(F32), dn, precision=HI,
                                            preferred_element_type=F32)
        else:
            acc_ref[...] += lax.dot_general(a_ref[...].astype(MXU_DTYPE), b_ref[...].astype(MXU_DTYPE), dn,
                                            preferred_element_type=F32)

        @pl.when(k == nk - 1)
        def _():
            r = acc_ref[...]
            if has_add:
                r = r + add_ref[...].astype(F32)
            o_ref[...] = r.astype(o_ref.dtype)

    in_specs = [a_spec, b_spec] + ([o_spec] if has_add else [])
    args = (a, b) + ((add,) if has_add else ())
    return pl.pallas_call(
        body, name=name, grid=grid, in_specs=in_specs, out_specs=o_spec,
        out_shape=jax.ShapeDtypeStruct(out_shape, out_dtype),
        scratch_shapes=[pltpu.VMEM((tm, tn), F32)],
        compiler_params=_cparams(("parallel", "parallel", "parallel", "arbitrary")),
    )(*args)


def _ew(name, fn, ins, outs, tr=256):
    rows = next(x[1].shape[0] for x in ins if x[0] == 'r')
    tr = min(tr, rows)
    assert rows % tr == 0 and tr % SUBLANES == 0, (name, rows, tr)
    in_specs, args = [], []
    for x in ins:
        if x[0] == 'r' and len(x) == 2:
            in_specs.append(pl.BlockSpec((tr, x[1].shape[1]), lambda i: (i, 0)))
        elif x[0] == 'r':
            in_specs.append(pl.BlockSpec((tr, x[3]), functools.partial(lambda cb, i: (i, cb), x[2])))
        else:
            in_specs.append(pl.BlockSpec(x[1].shape, lambda i: (0, 0)))
        args.append(x[1])
    out_specs, out_shapes = [], []
    for o in outs:
        if o[0] == 'r':
            out_specs.append(pl.BlockSpec((tr, o[1]), lambda i: (i, 0)))
            out_shapes.append(jax.ShapeDtypeStruct((rows, o[1]), o[2]))
        else:
            out_specs.append(pl.BlockSpec((SUBLANES, o[1]), lambda i: (0, 0)))
            out_shapes.append(jax.ShapeDtypeStruct((SUBLANES, o[1]), F32))
    nin = len(ins)
    has_acc = any(o[0] == 'a' for o in outs)

    def body(*refs):
        vals = fn(*[r[...] for r in refs[:nin]])
        if not isinstance(vals, (tuple, list)):
            vals = (vals,)
        i = pl.program_id(0)
        for o, ref, v in zip(outs, refs[nin:], vals):
            if o[0] == 'r':
                ref[...] = v.astype(ref.dtype)
            else:
                part = v.astype(F32).reshape(tr // SUBLANES, SUBLANES, o[1]).sum(axis=0)

                @pl.when(i == 0)
                def _(ref=ref, part=part):
                    ref[...] = part

                @pl.when(i > 0)
                def _(ref=ref, part=part):
                    ref[...] += part

    res = pl.pallas_call(
        body, name=name, grid=(rows // tr,), in_specs=in_specs, out_specs=out_specs, out_shape=out_shapes,
        compiler_params=_cparams(("arbitrary",) if has_acc else ("parallel",)),
    )(*args)
    return res


def _rms(x, g):
    r = lax.rsqrt(jnp.mean(x * x, axis=-1, keepdims=True) + RMS_EPS)
    xr = x * r
    return xr * g, xr


def _rms_bwd(x, g, dy):
    r = lax.rsqrt(jnp.mean(x * x, axis=-1, keepdims=True) + RMS_EPS)
    xr = x * r
    gdy = g * dy
    dx = r * (gdy - xr * jnp.mean(xr * gdy, axis=-1, keepdims=True))
    return dx, dy * xr


def _sigmoid(x):
    return 1.0 / (1.0 + jnp.exp(-x))


_GELU_C = math.sqrt(2.0 / math.pi)


def _gelu(x):
    return 0.5 * x * (1.0 + jnp.tanh(_GELU_C * (x + 0.044715 * x * x * x)))


def _gelu_grad(x):
    t = jnp.tanh(_GELU_C * (x + 0.044715 * x * x * x))
    return 0.5 * (1.0 + t) + 0.5 * x * (1.0 - t * t) * _GELU_C * (1.0 + 3 * 0.044715 * x * x)


ATTN_ROWS_PER_STEP = 8
NT_DIMS = (((1,), (1,)), ((), ()))
NN_DIMS = (((1,), (0,)), ((), ()))
TN_DIMS = (((0,), (0,)), ((), ()))


def _attn_geometry(r, rows):
    row_start = jnp.clip(r - WIN_H // 2, 0, rows - WIN_H)
    key0 = pl.multiple_of(row_start * GRID_W, GRID_W)
    bias0 = pl.multiple_of((row_start - r + (WIN_H - 1)) * GRID_W, GRID_W)
    return key0, bias0


def _window_onehot():
    c = jnp.arange(GRID_W)
    col_start = jnp.clip(c - WIN_W // 2, 0, GRID_W - WIN_W)
    col_in = (c[None, :] >= col_start[:, None]) & (c[None, :] < col_start[:, None] + WIN_W)
    dc = jnp.clip(c[None, :] - c[:, None], -(WIN_W - 1), WIN_W - 1) + (WIN_W - 1)
    onehot = ((dc[:, :, None] == jnp.arange(2 * WIN_W - 1)[None, None, :]) & col_in[:, :, None]).astype(F32)
    return onehot, col_in


def _bias_table(rpb):
    onehot, col_in = _window_onehot()
    nh = rpb.shape[0]
    tab = jnp.einsum('perd,qkd->prkeq', rpb.reshape(nh // 2, 2, 2 * WIN_H - 1, 2 * WIN_W - 1), onehot, precision=HI)
    tab = tab + jnp.where(col_in, 0.0, NEG_INF).T[None, None, :, None, :]
    return tab.reshape(nh // 2, (2 * WIN_H - 1) * GRID_W, 2 * GRID_W)


def _bias_table_grad(dtab):
    onehot, _ = _window_onehot()
    npair = dtab.shape[0]
    d = dtab.reshape(npair, 2 * WIN_H - 1, GRID_W, 2, GRID_W)
    return jnp.einsum('prkeq,qkd->perd', d, onehot, precision=HI).reshape(2 * npair, 2 * WIN_H - 1, 2 * WIN_W - 1)


def _lane_lo(shape):
    return lax.broadcasted_iota(jnp.int32, shape, 1) < HEAD_DIM


def _half_sums(v):
    lo = _lane_lo(v.shape)
    s_lo = jnp.sum(jnp.where(lo, v, 0.0), axis=1, keepdims=True)
    s_hi = jnp.sum(jnp.where(lo, 0.0, v), axis=1, keepdims=True)
    return jnp.where(lo, s_lo, s_hi)


def _rms_pair(x, g):
    r = lax.rsqrt(_half_sums(x * x) * (1.0 / HEAD_DIM) + RMS_EPS)
    return x * r * g


def _rms_pair_bwd(x, g, dy):
    r = lax.rsqrt(_half_sums(x * x) * (1.0 / HEAD_DIM) + RMS_EPS)
    xr = x * r
    gdy = g * dy
    dx = r * (gdy - xr * (_half_sums(xr * gdy) * (1.0 / HEAD_DIM)))
    return dx, dy * xr


def _blockdiag(a):
    a2 = jnp.concatenate([a, a], axis=0)
    row_hi = lax.broadcasted_iota(jnp.int32, a2.shape, 0) >= GRID_W
    lane_hi = lax.broadcasted_iota(jnp.int32, a2.shape, 1) >= HEAD_DIM
    return jnp.where(row_hi == lane_hi, a2, 0.0).astype(MXU_DTYPE)


def _diag_blocks(m):
    return jnp.where(_lane_lo((GRID_W, 2 * HEAD_DIM)), m[:GRID_W], m[GRID_W:])


def _attn_scores(qb, kb, bias):
    st = lax.dot_general(kb, qb, NT_DIMS, preferred_element_type=F32)
    st = st * (1.0 / math.sqrt(HEAD_DIM)) + bias
    mx = jnp.max(st, axis=0, keepdims=True)
    p = jnp.exp(st - mx)
    return p * (1.0 / jnp.sum(p, axis=0, keepdims=True))


def _attn_fwd(z4, qg2, kg2, bias_t):
    _, t, aw = z4.shape
    rows = t // GRID_W
    npair = aw // (2 * HEAD_DIM)
    nkeys = WIN_H * GRID_W
    nb = bias_t.shape[1]
    rps = min(ATTN_ROWS_PER_STEP, rows)
    blk = rps * GRID_W

    def body(q_ref, k_ref, v_ref, qg_ref, kg_ref, b_ref, o_ref, kn_ref, vb_ref):
        rb = pl.program_id(1)

        @pl.when(rb == 0)
        def _():
            kn_ref[...] = _rms_pair(k_ref[...], kg_ref[...]).astype(MXU_DTYPE)
            vb_ref[...] = v_ref[...].astype(MXU_DTYPE)

        def row(i, carry):
            key0, bias0 = _attn_geometry(rb * rps + i, rows)
            at = pl.ds(pl.multiple_of(i * GRID_W, GRID_W), GRID_W)
            qb = _blockdiag(_rms_pair(q_ref[at, :], qg_ref[...]))
            pt = _attn_scores(qb, kn_ref[pl.ds(key0, nkeys), :], b_ref[pl.ds(bias0, nkeys), :])
            both = lax.dot_general(pt.astype(MXU_DTYPE), vb_ref[pl.ds(key0, nkeys), :], TN_DIMS,
                                   preferred_element_type=F32)
            o_ref[at, :] = _diag_blocks(both)
            return carry

        lax.fori_loop(0, rps, row, 0)

    pair_cols = lambda lead: pl.BlockSpec((None, t, 2 * HEAD_DIM), lambda p, r: (lead, 0, p))
    return pl.pallas_call(
        body, name="attn_fwd", grid=(npair, rows // rps),
        in_specs=[pl.BlockSpec((None, blk, 2 * HEAD_DIM), lambda p, r: (0, r, p)), pair_cols(1), pair_cols(2),
                  pl.BlockSpec((1, 2 * HEAD_DIM), lambda p, r: (0, 0)),
                  pl.BlockSpec((1, 2 * HEAD_DIM), lambda p, r: (0, 0)),
                  pl.BlockSpec((None, nb, 2 * GRID_W), lambda p, r: (p, 0, 0))],
        out_specs=pl.BlockSpec((blk, 2 * HEAD_DIM), lambda p, r: (r, p)),
        out_shape=jax.ShapeDtypeStruct((t, aw), F32),
        scratch_shapes=[pltpu.VMEM((t, 2 * HEAD_DIM), MXU_DTYPE), pltpu.VMEM((t, 2 * HEAD_DIM), MXU_DTYPE)],
        compiler_params=_cparams(("parallel", "arbitrary")),
    )(z4, z4, z4, qg2, kg2, bias_t)


def _attn_bwd(z4, qg2, kg2, bias_t, dya):
    _, t, aw = z4.shape
    rows = t // GRID_W
    npair = aw // (2 * HEAD_DIM)
    nkeys = WIN_H * GRID_W
    nb = bias_t.shape[1]
    rps = min(ATTN_ROWS_PER_STEP, rows)
    blk = rps * GRID_W
    scale = 1.0 / math.sqrt(HEAD_DIM)

    def body(q_ref, k_ref, v_ref, qg_ref, kg_ref, b_ref, do_ref, dz_ref, db_ref, dqg_ref, dkg_ref, kn_ref, vb_ref):
        rb = pl.program_id(1)

        @pl.when(rb == 0)
        def _():
            kn_ref[...] = _rms_pair(k_ref[...], kg_ref[...]).astype(MXU_DTYPE)
            vb_ref[...] = v_ref[...].astype(MXU_DTYPE)
            dz_ref[1] = jnp.zeros((t, 2 * HEAD_DIM), F32)
            dz_ref[2] = jnp.zeros((t, 2 * HEAD_DIM), F32)
            db_ref[...] = jnp.zeros_like(db_ref)
            dqg_ref[...] = jnp.zeros_like(dqg_ref)

        def row(i, dqg_sum):
            r = rb * rps + i
            key0, bias0 = _attn_geometry(r, rows)
            keys = pl.ds(key0, nkeys)
            at = pl.ds(pl.multiple_of(i * GRID_W, GRID_W), GRID_W)
            q = q_ref[at, :]
            qb = _blockdiag(_rms_pair(q, qg_ref[...]))
            dob = _blockdiag(do_ref[at, :])
            kb = kn_ref[keys, :]
            pt = _attn_scores(qb, kb, b_ref[pl.ds(bias0, nkeys), :])
            dz_ref[2, keys, :] += lax.dot_general(pt.astype(MXU_DTYPE), dob, NN_DIMS, preferred_element_type=F32)
            dpt = lax.dot_general(vb_ref[keys, :], dob, NT_DIMS, preferred_element_type=F32)
            dst = pt * (dpt - jnp.sum(pt * dpt, axis=0, keepdims=True))
            db_ref[pl.ds(bias0, nkeys), :] += dst
            dsb = dst.astype(MXU_DTYPE)
            dz_ref[1, keys, :] += scale * lax.dot_general(dsb, qb, NN_DIMS, preferred_element_type=F32)
            dqn = scale * _diag_blocks(lax.dot_general(dsb, kb, TN_DIMS, preferred_element_type=F32))
            dq, dqg = _rms_pair_bwd(q, qg_ref[...], dqn)
            dz_ref[0, pl.ds(pl.multiple_of(r * GRID_W, GRID_W), GRID_W), :] = dq
            return dqg_sum + jnp.sum(dqg, axis=0, keepdims=True)

        dqg_ref[...] += lax.fori_loop(0, rps, row, jnp.zeros((1, 2 * HEAD_DIM), F32))

        @pl.when(rb == rows // rps - 1)
        def _():
            dk, dkg = _rms_pair_bwd(k_ref[...], kg_ref[...], dz_ref[1])
            dz_ref[1] = dk
            dkg_ref[...] = jnp.sum(dkg, axis=0, keepdims=True)

    pair_cols = lambda lead: pl.BlockSpec((None, t, 2 * HEAD_DIM), lambda p, r: (lead, 0, p))
    pair_vec = pl.BlockSpec((None, 1, 2 * HEAD_DIM), lambda p, r: (p, 0, 0))
    return pl.pallas_call(
        body, name="attn_bwd", grid=(npair, rows // rps),
        in_specs=[pl.BlockSpec((None, blk, 2 * HEAD_DIM), lambda p, r: (0, r, p)), pair_cols(1), pair_cols(2),
                  pl.BlockSpec((1, 2 * HEAD_DIM), lambda p, r: (0, 0)),
                  pl.BlockSpec((1, 2 * HEAD_DIM), lambda p, r: (0, 0)),
                  pl.BlockSpec((None, nb, 2 * GRID_W), lambda p, r: (p, 0, 0)),
                  pl.BlockSpec((blk, 2 * HEAD_DIM), lambda p, r: (r, p))],
        out_specs=[pl.BlockSpec((3, t, 2 * HEAD_DIM), lambda p, r: (0, 0, p)),
                   pl.BlockSpec((None, nb, 2 * GRID_W), lambda p, r: (p, 0, 0)),
                   pair_vec, pair_vec],
        out_shape=[jax.ShapeDtypeStruct((4, t, aw), F32), jax.ShapeDtypeStruct(bias_t.shape, F32),
                   jax.ShapeDtypeStruct((npair, 1, 2 * HEAD_DIM), F32),
                   jax.ShapeDtypeStruct((npair, 1, 2 * HEAD_DIM), F32)],
        scratch_shapes=[pltpu.VMEM((t, 2 * HEAD_DIM), MXU_DTYPE), pltpu.VMEM((t, 2 * HEAD_DIM), MXU_DTYPE)],
        compiler_params=_cparams(("parallel", "arbitrary")),
    )(z4, z4, z4, qg2, kg2, bias_t, dya)


def _attn_fwd_old(z4, qg2, kg2, bias_t):
    _, t, aw = z4.shape
    rows = t // GRID_W
    npair = aw // (2 * HEAD_DIM)
    nkeys = WIN_H * GRID_W
    nb = bias_t.shape[1]

    def body(q_ref, k_ref, v_ref, qg_ref, kg_ref, b_ref, o_ref, kn_ref):
        r = pl.program_id(1)

        @pl.when(r == 0)
        def _():
            for e in range(2):
                sl = slice(e * HEAD_DIM, (e + 1) * HEAD_DIM)
                kn_ref[e] = _rms(k_ref[:, sl], kg_ref[:, sl])[0]

        key0, bias0 = _attn_geometry(r, rows)
        outs = []
        for e in range(2):
            sl = slice(e * HEAD_DIM, (e + 1) * HEAD_DIM)
            qn = _rms(q_ref[:, sl], qg_ref[:, sl])[0]
            kb = kn_ref[e, pl.ds(key0, nkeys), :]
            vb = v_ref[pl.ds(key0, nkeys), sl]
            pt = _attn_scores(qn, kb, b_ref[e, pl.ds(bias0, nkeys), :])
            outs.append(lax.dot_general(pt.astype(MXU_DTYPE), vb.astype(MXU_DTYPE), (((0,), (0,)), ((), ())),
                                        preferred_element_type=F32))
        o_ref[...] = jnp.concatenate(outs, axis=1)

    return pl.pallas_call(
        body, name="attn_fwd", grid=(npair, rows),
        in_specs=[pl.BlockSpec((None, GRID_W, 2 * HEAD_DIM), lambda p, r: (0, r, p)),
                  pl.BlockSpec((None, t, 2 * HEAD_DIM), lambda p, r: (1, 0, p)),
                  pl.BlockSpec((None, t, 2 * HEAD_DIM), lambda p, r: (2, 0, p)),
                  pl.BlockSpec((1, 2 * HEAD_DIM), lambda p, r: (0, 0)),
                  pl.BlockSpec((1, 2 * HEAD_DIM), lambda p, r: (0, 0)),
                  pl.BlockSpec((2, nb, GRID_W), lambda p, r: (p, 0, 0))],
        out_specs=pl.BlockSpec((GRID_W, 2 * HEAD_DIM), lambda p, r: (r, p)),
        out_shape=jax.ShapeDtypeStruct((t, aw), F32),
        scratch_shapes=[pltpu.VMEM((2, t, HEAD_DIM), F32)],
        compiler_params=_cparams(("parallel", "arbitrary")),
    )(z4, z4, z4, qg2, kg2, bias_t)


def _attn_bwd_old(z4, qg2, kg2, bias_t, dya):
    _, t, aw = z4.shape
    rows = t // GRID_W
    npair = aw // (2 * HEAD_DIM)
    nkeys = WIN_H * GRID_W
    nb = bias_t.shape[1]
    scale = 1.0 / math.sqrt(HEAD_DIM)

    def body(q_ref, k_ref, v_ref, qg_ref, kg_ref, b_ref, do_ref,
             dq_ref, dk_ref, dv_ref, db_ref, dqg_ref, dkg_ref, kn_ref, dkn_ref, dva_ref):
        r = pl.program_id(1)

        @pl.when(r == 0)
        def _():
            for e in range(2):
                sl = slice(e * HEAD_DIM, (e + 1) * HEAD_DIM)
                kn_ref[e] = _rms(k_ref[:, sl], kg_ref[:, sl])[0]
            dkn_ref[...] = jnp.zeros_like(dkn_ref)
            dva_ref[...] = jnp.zeros_like(dva_ref)
            db_ref[...] = jnp.zeros_like(db_ref)
            dqg_ref[...] = jnp.zeros_like(dqg_ref)

        key0, bias0 = _attn_geometry(r, rows)
        dqs, dqgs = [], []
        for e in range(2):
            sl = slice(e * HEAD_DIM, (e + 1) * HEAD_DIM)
            q = q_ref[:, sl]
            qn = _rms(q, qg_ref[:, sl])[0]
            kb = kn_ref[e, pl.ds(key0, nkeys), :]
            vb = v_ref[pl.ds(key0, nkeys), sl]
            do = do_ref[:, sl]
            pt = _attn_scores(qn, kb, b_ref[e, pl.ds(bias0, nkeys), :])
            dva_ref[e, pl.ds(key0, nkeys), :] += lax.dot_general(
                pt.astype(MXU_DTYPE), do.astype(MXU_DTYPE), (((1,), (0,)), ((), ())), preferred_element_type=F32)
            dpt = lax.dot_general(vb.astype(MXU_DTYPE), do.astype(MXU_DTYPE), (((1,), (1,)), ((), ())),
                                  preferred_element_type=F32)
            dst = pt * (dpt - jnp.sum(pt * dpt, axis=0, keepdims=True))
            db_ref[e, pl.ds(bias0, nkeys), :] += dst
            dsb = dst.astype(MXU_DTYPE)
            dqn = scale * lax.dot_general(dsb, kb.astype(MXU_DTYPE), (((0,), (0,)), ((), ())),
                                          preferred_element_type=F32)
            dkn_ref[e, pl.ds(key0, nkeys), :] += scale * lax.dot_general(
                dsb, qn.astype(MXU_DTYPE), (((1,), (0,)), ((), ())), preferred_element_type=F32)
            dq, dqg = _rms_bwd(q, qg_ref[:, sl], dqn)
            dqs.append(dq)
            dqgs.append(jnp.sum(dqg, axis=0, keepdims=True))
        dq_ref[...] = jnp.concatenate(dqs, axis=1)
        dqg_ref[...] += jnp.concatenate(dqgs, axis=1)

        @pl.when(r == rows - 1)
        def _():
            dks, dkgs = [], []
            for e in range(2):
                sl = slice(e * HEAD_DIM, (e + 1) * HEAD_DIM)
                dk, dkg = _rms_bwd(k_ref[:, sl], kg_ref[:, sl], dkn_ref[e])
                dks.append(dk)
                dkgs.append(jnp.sum(dkg, axis=0, keepdims=True))
            dk_ref[...] = jnp.concatenate(dks, axis=1)
            dv_ref[...] = jnp.concatenate([dva_ref[0], dva_ref[1]], axis=1)
            dkg_ref[...] = jnp.concatenate(dkgs, axis=1)

    pair_vec = pl.BlockSpec((None, 1, 2 * HEAD_DIM), lambda p, r: (p, 0, 0))
    return pl.pallas_call(
        body, name="attn_bwd", grid=(npair, rows),
        in_specs=[pl.BlockSpec((None, GRID_W, 2 * HEAD_DIM), lambda p, r: (0, r, p)),
                  pl.BlockSpec((None, t, 2 * HEAD_DIM), lambda p, r: (1, 0, p)),
                  pl.BlockSpec((None, t, 2 * HEAD_DIM), lambda p, r: (2, 0, p)),
                  pl.BlockSpec((1, 2 * HEAD_DIM), lambda p, r: (0, 0)),
                  pl.BlockSpec((1, 2 * HEAD_DIM), lambda p, r: (0, 0)),
                  pl.BlockSpec((2, nb, GRID_W), lambda p, r: (p, 0, 0)),
                  pl.BlockSpec((GRID_W, 2 * HEAD_DIM), lambda p, r: (r, p))],
        out_specs=[pl.BlockSpec((GRID_W, 2 * HEAD_DIM), lambda p, r: (r, p)),
                   pl.BlockSpec((t, 2 * HEAD_DIM), lambda p, r: (0, p)),
                   pl.BlockSpec((t, 2 * HEAD_DIM), lambda p, r: (0, p)),
                   pl.BlockSpec((2, nb, GRID_W), lambda p, r: (p, 0, 0)),
                   pair_vec, pair_vec],
        out_shape=[jax.ShapeDtypeStruct((t, aw), F32), jax.ShapeDtypeStruct((t, aw), F32),
                   jax.ShapeDtypeStruct((t, aw), F32), jax.ShapeDtypeStruct(bias_t.shape, F32),
                   jax.ShapeDtypeStruct((npair, 1, 2 * HEAD_DIM), F32),
                   jax.ShapeDtypeStruct((npair, 1, 2 * HEAD_DIM), F32)],
        scratch_shapes=[pltpu.VMEM((2, t, HEAD_DIM), F32), pltpu.VMEM((2, t, HEAD_DIM), F32),
                        pltpu.VMEM((2, t, HEAD_DIM), F32)],
        compiler_params=_cparams(("parallel", "arbitrary")),
    )(z4, z4, z4, qg2, kg2, bias_t, dya)


def _s5_mats(a_re, a_im, b_re, b_im, c_re, c_im, log_step, d_skip):
    nd, g, p = a_re.shape
    c = b_re.shape[-1]
    L = S5_CHUNK
    lr = jnp.minimum(a_re, -1e-4)
    li = a_im
    dt = jnp.exp(log_step)[..., None]
    n = jnp.arange(L + 1, dtype=F32)[:, None, None, None]
    mag = jnp.exp(n * (lr * dt)[None])
    ang = n * (li * dt)[None]
    pw_r, pw_i = mag * jnp.cos(ang), mag * jnp.sin(ang)
    den = lr * lr + li * li
    nr, ni = pw_r[1] - 1.0, pw_i[1]
    cr, ci = (nr * lr + ni * li) / den, (ni * lr - nr * li) / den
    bb_r = cr[..., None] * b_re - ci[..., None] * b_im
    bb_i = cr[..., None] * b_im + ci[..., None] * b_re
    pb_r = pw_r[..., None] * bb_r[None] - pw_i[..., None] * bb_i[None]
    pb_i = pw_r[..., None] * bb_i[None] + pw_i[..., None] * bb_r[None]
    kern = (jnp.einsum('dgop,ndgpi->ndgoi', c_re, pb_r[:L], precision=HI)
            - jnp.einsum('dgop,ndgpi->ndgoi', c_im, pb_i[:L], precision=HI))
    j = jnp.arange(L)
    e0 = ((j[None, None, :] - j[None, :, None]) == j[:, None, None]).astype(F32)
    e1 = ((j[None, :, None] - j[None, None, :]) == j[:, None, None]).astype(F32)
    mt = (jnp.einsum('njt,ngoi->gjito', e0, kern[:, 0], precision=HI)
          + jnp.einsum('njt,ngoi->gjito', e1, kern[:, 1], precision=HI))
    eye = jnp.eye(L, dtype=F32)[:, None, :, None] * jnp.eye(c, dtype=F32)[None, :, None, :]
    mt = mt + d_skip.reshape(g, 1, c, 1, 1) * eye[None]
    mt = mt.reshape(g, L * c, L * c)

    def rows_jc(x):
        return x.transpose(1, 0, 3, 2).reshape(g, L * c, p)
    ws = jnp.concatenate([rows_jc(pb_r[:L, 0][::-1]), rows_jc(pb_i[:L, 0][::-1]),
                          rows_jc(pb_r[:L, 1]), rows_jc(pb_i[:L, 1])], axis=-1)
    cp_r = c_re[None] * pw_r[:, :, :, None, :] - c_im[None] * pw_i[:, :, :, None, :]
    cp_i = c_re[None] * pw_i[:, :, :, None, :] + c_im[None] * pw_r[:, :, :, None, :]

    def cols_to(x):
        return x.transpose(1, 3, 0, 2).reshape(g, p, L * c)
    wo = jnp.concatenate([cols_to(cp_r[1:, 0]), cols_to(-cp_i[1:, 0]),
                          cols_to(cp_r[1:, 1][::-1]), cols_to(-cp_i[1:, 1][::-1])], axis=1)
    lr16, li16 = pw_r[L], pw_i[L]
    fa = jnp.concatenate([lr16[0], lr16[0], lr16[1], lr16[1]], axis=-1)
    fb = jnp.concatenate([-li16[0], li16[0], -li16[1], li16[1]], axis=-1)
    return mt, ws, wo, fa, fb


def _gmm(name, a, b, contract, a_stacked=False, b_stacked=False, o_stacked=False, add=None):
    w = S5_CHUNK * SSM_GROUP_CH
    g = (a.shape[0] if a_stacked else a.shape[1] // w)
    dn = {'nn': (((1,), (0,)), ((), ())), 'nt': (((1,), (1,)), ((), ())), 'tn': (((0,), (0,)), ((), ()))}[contract]

    def spec(arr, stacked):
        if stacked:
            return pl.BlockSpec((None,) + arr.shape[1:], lambda i: (i, 0, 0))
        return pl.BlockSpec((arr.shape[0], w), lambda i: (0, i))

    m = (a.shape[1] if a_stacked else a.shape[0]) if contract != 'tn' else w
    n = w
    if o_stacked:
        o_spec = pl.BlockSpec((None, m, n), lambda i: (i, 0, 0))
        o_shape = (g, m, n)
    else:
        o_spec = pl.BlockSpec((m, n), lambda i: (0, i))
        o_shape = (m, g * n)
    has_add = add is not None

    def body(*refs):
        if has_add:
            a_ref, b_ref, add_ref, o_ref = refs
        else:
            a_ref, b_ref, o_ref = refs
        r = lax.dot_general(a_ref[...], b_ref[...], dn, precision=HI, preferred_element_type=F32)
        if has_add:
            r = r + add_ref[...]
        o_ref[...] = r

    in_specs = [spec(a, a_stacked), spec(b, b_stacked)] + ([o_spec] if has_add else [])
    return pl.pallas_call(
        body, name=name, grid=(g,), in_specs=in_specs, out_specs=o_spec,
        out_shape=jax.ShapeDtypeStruct(o_shape, F32), compiler_params=_cparams(("parallel",)),
    )(*((a, b) + ((add,) if has_add else ())))


def _s5_scan(name, s, fa, fb, rev0, xin=None):
    nk, g, w = s.shape
    hw = w // 2
    gb = min(g, 16)
    with_acc = xin is not None

    def body(*refs):
        if with_acc:
            s_ref, a_ref, b_ref, x_ref, o_ref, pa_ref, pb_ref = refs
        else:
            s_ref, a_ref, b_ref, o_ref = refs
        fa0, fb0, fa1, fb1 = a_ref[:, :hw], b_ref[:, :hw], a_ref[:, hw:], b_ref[:, hw:]

        def step(i, carry):
            x0, x1, pa0, pb0, pa1, pb1 = carry
            k0 = (nk - 1 - i) if rev0 else i
            k1 = i if rev0 else (nk - 1 - i)
            o_ref[k0, :, :hw] = x0
            o_ref[k1, :, hw:] = x1
            if with_acc:
                xi0, xi1 = x_ref[k0, :, :hw], x_ref[k1, :, hw:]
                pa0 = pa0 + x0 * xi0
                pb0 = pb0 + x0 * pltpu.roll(xi0, hw // 2, 1)
                pa1 = pa1 + x1 * xi1
                pb1 = pb1 + x1 * pltpu.roll(xi1, hw // 2, 1)
            x0 = fa0 * x0 + fb0 * pltpu.roll(x0, hw // 2, 1) + s_ref[k0, :, :hw]
            x1 = fa1 * x1 + fb1 * pltpu.roll(x1, hw // 2, 1) + s_ref[k1, :, hw:]
            return x0, x1, pa0, pb0, pa1, pb1

        z = jnp.zeros((gb, hw), F32)
        res = lax.fori_loop(0, nk, step, (z, z, z, z, z, z))
        if with_acc:
            pa_ref[:, :hw] = res[2]
            pb_ref[:, :hw] = res[3]
            pa_ref[:, hw:] = res[4]
            pb_ref[:, hw:] = res[5]

    seq = pl.BlockSpec((nk, gb, w), lambda i: (0, i, 0))
    vec = pl.BlockSpec((gb, w), lambda i: (i, 0))
    in_specs = [seq, vec, vec] + ([seq] if with_acc else [])
    out_specs = [seq] + ([vec, vec] if with_acc else [])
    out_shape = [jax.ShapeDtypeStruct((nk, g, w), F32)] + (
        [jax.ShapeDtypeStruct((g, w), F32)] * 2 if with_acc else [])
    return pl.pallas_call(
        body, name=name, grid=(g // gb,), in_specs=in_specs, out_specs=out_specs, out_shape=out_shape,
        compiler_params=_cparams(("parallel",)),
    )(*((s, fa, fb) + ((xin,) if with_acc else ())))


def _to_groups(u):
    t, sw = u.shape
    g = sw // SSM_GROUP_CH
    return u.reshape(t // S5_CHUNK, S5_CHUNK, g, SSM_GROUP_CH).transpose(0, 2, 1, 3).reshape(t // S5_CHUNK, -1)


def _from_groups(y, sw):
    nk = y.shape[0]
    g = sw // SSM_GROUP_CH
    return y.reshape(nk, g, S5_CHUNK, SSM_GROUP_CH).transpose(0, 2, 1, 3).reshape(nk * S5_CHUNK, sw)


def _s5_fwd(u2, mats):
    mt, ws, wo, fa, fb = mats
    nk = u2.shape[0]
    g = mt.shape[0]
    y_intra = _gmm("s5_intra", u2, mt, 'nn', b_stacked=True)
    s = _gmm("s5_chunk_state", u2, ws, 'nn', b_stacked=True)
    (xin,) = _s5_scan("s5_scan", s.reshape(nk, g, -1), fa, fb, False)
    xin = xin.reshape(nk, -1)
    return _gmm("s5_inter", xin, wo, 'nn', b_stacked=True, add=y_intra), xin


def _s5_bwd(u2, xin, mats, dy2):
    mt, ws, wo, fa, fb = mats
    nk = u2.shape[0]
    g = mt.shape[0]
    dxin = _gmm("s5_dxin", dy2, wo, 'nt', b_stacked=True)
    ds, pa, pb = _s5_scan("s5_scan_adj", dxin.reshape(nk, g, -1), fa, -fb, True, xin=xin.reshape(nk, g, -1))
    ds = ds.reshape(nk, -1)
    du_a = _gmm("s5_du_intra", dy2, mt, 'nt', b_stacked=True)
    du2 = _gmm("s5_du_state", ds, ws, 'nt', b_stacked=True, add=du_a)
    dmt = _gmm("s5_dmt", u2, dy2, 'tn', o_stacked=True)
    dws = _gmm("s5_dws", u2, ds, 'tn', o_stacked=True)
    dwo = _gmm("s5_dwo", xin, dy2, 'tn', o_stacked=True)
    return du2, (dmt, dws, dwo, pa, pb)


def _local_step(x, target, big, small):
    t, d = x.shape
    w_in4, w_glu, w_out, w_gate4, w_up4, w_down4 = big
    aw = w_in4.shape[2]
    sw = w_glu.shape[0]
    nh = aw // HEAD_DIM
    ffs = w_gate4.shape[2]
    row = lambda v: v.reshape(1, -1)
    g_mix, g_ffn = row(small['g_mix']), row(small['g_ffn'])
    g_oa, g_os, b_glu = row(small['g_out_attn']), row(small['g_out_ssm']), row(small['b_glu'])
    qg2 = jnp.tile(row(small['q_gain']), (1, 2))
    kg2 = jnp.tile(row(small['k_gain']), (1, 2))

    (h,) = _ew("rms_mix", lambda xv, g: _rms(xv, g)[0], [('r', x), ('c', g_mix)], [('r', d, MXU_DTYPE)])
    z4 = _mm("in_proj", h, w_in4, contract='nn', b_mode='b', o_mode='b')
    bias_t = _bias_table(small['rpb'])
    ya = _attn_fwd(z4, qg2, kg2, bias_t)
    s5_params = tuple(small[n] for n in ('ssm_a_re', 'ssm_a_im', 'ssm_b_re', 'ssm_b_im', 'ssm_c_re', 'ssm_c_im',
                                         'ssm_log_step', 'ssm_d'))
    mats, mats_vjp = jax.vjp(_s5_mats, *s5_params)
    u2 = _to_groups(z4[3])
    ypre2, xin = _s5_fwd(u2, mats)
    ypre = _from_groups(ypre2, sw)
    (yb,) = _ew("gelu", _gelu, [('r', ypre)], [('r', sw, MXU_DTYPE)])
    a_glu = _mm("glu_proj", yb, w_glu, contract='nn')

    def mix_out(yav, ypv, av, bg, goa, gos):
        ys = _gelu(ypv) * _sigmoid(av + bg)
        return jnp.concatenate([_rms(yav, goa)[0], _rms(ys, gos)[0]], axis=1)
    (ycat,) = _ew("mix_out", mix_out, [('r', ya), ('r', ypre), ('r', a_glu), ('c', b_glu), ('c', g_oa), ('c', g_os)],
                  [('r', aw + sw, MXU_DTYPE)])
    x1 = _mm("out_proj", ycat, w_out, contract='nn', add=x)
    (h2,) = _ew("rms_ffn", lambda xv, g: _rms(xv, g)[0], [('r', x1), ('c', g_ffn)], [('r', d, MXU_DTYPE)])
    gate4 = _mm("ffn_gate", h2, w_gate4, contract='nn', b_mode='b', o_mode='b', tn=ffs)
    up4 = _mm("ffn_up", h2, w_up4, contract='nn', b_mode='b', o_mode='b', tn=ffs)
    gate_f, up_f = gate4.reshape(4 * t, ffs), up4.reshape(4 * t, ffs)
    (act,) = _ew("swiglu", lambda gv, uv: gv * _sigmoid(gv) * uv, [('r', gate_f), ('r', up_f)],
                 [('r', ffs, MXU_DTYPE)])
    act4 = act.reshape(4, t, ffs)
    x2 = _mm("ffn_down", act4, w_down4, contract='nn', a_mode='c', b_mode='c', add=x1, tk=ffs)

    def loss_fn(xv, tv):
        diff = xv - tv
        return diff * (1.0 / d), diff * diff
    dx2, sq = _ew("loss", loss_fn, [('r', x2), ('r', target)], [('r', d, F32), ('a', d)])

    dact4 = _mm("ffn_down_dx", dx2, w_down4, contract='nt', b_mode='b', o_mode='b', tn=ffs)
    d_w_down4 = _mm("ffn_down_dw", act4, dx2, contract='tn', a_mode='b', o_mode='b', tm=ffs)

    def swiglu_bwd(dav, gv, uv):
        s = _sigmoid(gv)
        return dav * uv * s * (1.0 + gv * (1.0 - s)), dav * gv * s
    dgate, dup = _ew("swiglu_bwd", swiglu_bwd, [('r', dact4.reshape(4 * t, ffs)), ('r', gate_f), ('r', up_f)],
                     [('r', ffs, MXU_DTYPE), ('r', ffs, MXU_DTYPE)])
    dgate4, dup4 = dgate.reshape(4, t, ffs), dup.reshape(4, t, ffs)
    dh2 = _mm("ffn_gate_dx", dgate4, w_gate4, contract='nt', a_mode='c', b_mode='c', tk=ffs)
    dh2 = _mm("ffn_up_dx", dup4, w_up4, contract='nt', a_mode='c', b_mode='c', add=dh2, tk=ffs)
    d_w_gate4 = _mm("ffn_gate_dw", h2, dgate4, contract='tn', b_mode='b', o_mode='b', tn=ffs)
    d_w_up4 = _mm("ffn_up_dw", h2, dup4, contract='tn', b_mode='b', o_mode='b', tn=ffs)

    def rms_res_bwd(xv, g, dyv, resv):
        dx, dg = _rms_bwd(xv, g, dyv)
        return resv + dx, dg
    dx1, d_g_ffn = _ew("rms_ffn_bwd", rms_res_bwd, [('r', x1), ('c', g_ffn), ('r', dh2), ('r', dx2)],
                       [('r', d, F32), ('a', d)])

    dycat = _mm("out_proj_dx", dx1, w_out, contract='nt')
    d_w_out = _mm("out_proj_dw", ycat, dx1, contract='tn')

    def mix_out_bwd(yav, ypv, av, bg, goa, gos, dca, dcs):
        dya, dgoa = _rms_bwd(yav, goa, dca)
        y = _gelu(ypv)
        s = _sigmoid(av + bg)
        dys, dgos = _rms_bwd(y * s, gos, dcs)
        da = dys * y * s * (1.0 - s)
        return dya, da, dys * s, dgoa, dgos, da
    dya, da, dy_direct, d_g_oa, d_g_os, d_b_glu = _ew(
        "mix_out_bwd", mix_out_bwd,
        [('r', ya), ('r', ypre), ('r', a_glu), ('c', b_glu), ('c', g_oa), ('c', g_os),
         ('r', dycat, 0, aw), ('r', dycat, 1, sw)],
        [('r', aw, F32), ('r', sw, MXU_DTYPE), ('r', sw, F32), ('a', aw), ('a', sw), ('a', sw)])
    dy = _mm("glu_proj_dx", da, w_glu, contract='nt', add=dy_direct)
    d_w_glu = _mm("glu_proj_dw", yb, da, contract='tn')
    (dypre,) = _ew("gelu_bwd", lambda dyv, ypv: dyv * _gelu_grad(ypv), [('r', dy), ('r', ypre)], [('r', sw, F32)])

    du2, dmats = _s5_bwd(u2, xin, mats, _to_groups(dypre))
    d_s5 = mats_vjp(dmats)
    du = _from_groups(du2, sw)
    dz4, dbias_t, dqg, dkg = _attn_bwd(z4, qg2, kg2, bias_t, dya)
    d_rpb = _bias_table_grad(dbias_t)
    fold = lambda v: v.reshape(-1, 2, HEAD_DIM).sum(axis=(0, 1))
    dz4 = dz4.at[3].set(du)

    dh = _mm("in_proj_dx", dz4, w_in4, contract='nt', a_mode='c', b_mode='c')
    d_w_in4 = _mm("in_proj_dw", h, dz4, contract='tn', b_mode='b', o_mode='b')
    dx, d_g_mix = _ew("rms_mix_bwd", rms_res_bwd, [('r', x), ('c', g_mix), ('r', dh), ('r', dx1)],
                      [('r', d, F32), ('a', d)])

    colsum = lambda v: v.sum(axis=0)
    d_small = {
        'g_mix': colsum(d_g_mix), 'q_gain': fold(dqg), 'k_gain': fold(dkg), 'rpb': d_rpb,
        'ssm_a_re': d_s5[0], 'ssm_a_im': d_s5[1], 'ssm_b_re': d_s5[2], 'ssm_b_im': d_s5[3],
        'ssm_c_re': d_s5[4], 'ssm_c_im': d_s5[5], 'ssm_log_step': d_s5[6], 'ssm_d': d_s5[7],
        'b_glu': colsum(d_b_glu), 'g_out_attn': colsum(d_g_oa), 'g_out_ssm': colsum(d_g_os), 'g_ffn': colsum(d_g_ffn),
    }
    d_big = (d_w_in4, d_w_glu, d_w_out, d_w_gate4, d_w_up4, d_w_down4)
    return jnp.sum(sq), dx, d_big, d_small


ANY = pl.BlockSpec(memory_space=pl.ANY)


def _place():
    x, y, c = lax.axis_index("x"), lax.axis_index("y"), lax.axis_index("c")
    other_chips = [(1 - x, y), (x, 1 - y), (1 - x, 1 - y)]
    return x, y, c, 2 * x + y, (x, y, 1 - c), other_chips


def _gather_chips(name, arrs):
    n = len(arrs)

    def body(*refs):
        ins, outs = refs[:n], refs[n:2 * n]
        send_sems, recv_sems = refs[2 * n:]
        x, y, c, me, sibling, chips = _place()

        def remote(a, k, src, dst, to):
            return pltpu.make_async_remote_copy(src_ref=src, dst_ref=dst, send_sem=send_sems.at[a, k],
                                                recv_sem=recv_sems.at[a, k], device_id=to, device_id_type=MESH)

        sends = []
        for a in range(n):
            cp = remote(a, 6, ins[a], outs[a].at[me], sibling)
            cp.start()
            sends.append(cp)
        for j, (px, py) in enumerate(chips):
            for a in range(n):
                cp = remote(a, j, ins[a].at[c], outs[a].at[me, c], (px, py, c))
                cp.start()
                sends.append(cp)
        for j, (px, py) in enumerate(chips):
            for a in range(n):
                landed = outs[a].at[2 * px + py, c]
                remote(a, j, landed, landed, (px, py, c)).wait_recv()
                cp = remote(a, 3 + j, landed, landed, sibling)
                cp.start()
                sends.append(cp)
        for j, (px, py) in enumerate(chips):
            for a in range(n):
                theirs = outs[a].at[2 * px + py, 1 - c]
                remote(a, 3 + j, theirs, theirs, sibling).wait_recv()
        for a in range(n):
            remote(a, 6, ins[a], outs[a].at[me], sibling).wait_recv()
        for cp in sends:
            cp.wait_send()

    return pl.pallas_call(
        body, name=name, in_specs=[ANY] * n, out_specs=[ANY] * n,
        out_shape=[jax.ShapeDtypeStruct((N_CHIPS,) + a.shape, a.dtype) for a in arrs],
        scratch_shapes=[pltpu.SemaphoreType.DMA((n, 7)), pltpu.SemaphoreType.DMA((n, 7))],
    )(*arrs)


def _swap_halves(name, parts):
    n = len(parts)

    def body(*refs):
        ins, outs = refs[:n], refs[n:2 * n]
        send_sems, recv_sems = refs[2 * n:]
        x, y, c, me, sibling, chips = _place()
        cps = []
        for a in range(n):
            cp = pltpu.make_async_remote_copy(src_ref=ins[a].at[:, 1 - c], dst_ref=outs[a], send_sem=send_sems.at[a],
                                              recv_sem=recv_sems.at[a], device_id=sibling, device_id_type=MESH)
            cp.start()
            cps.append(cp)
        for cp in cps:
            cp.wait()

    return pl.pallas_call(
        body, name=name, in_specs=[ANY] * n, out_specs=[ANY] * n,
        out_shape=[jax.ShapeDtypeStruct((N_CHIPS,) + p.shape[2:], p.dtype) for p in parts],
        scratch_shapes=[pltpu.SemaphoreType.DMA((n,)), pltpu.SemaphoreType.DMA((n,))],
    )(*parts)


def _scatter_chips(name, sums):
    n = len(sums)

    def body(*refs):
        ins, outs = refs[:n], refs[n:2 * n]
        send_sems, recv_sems = refs[2 * n:]
        x, y, c, me, sibling, chips = _place()
        sends = []
        for j, (px, py) in enumerate(chips):
            for a in range(n):
                cp = pltpu.make_async_remote_copy(
                    src_ref=ins[a].at[2 * px + py], dst_ref=outs[a].at[me], send_sem=send_sems.at[a, j],
                    recv_sem=recv_sems.at[a, j], device_id=(px, py, c), device_id_type=MESH)
                cp.start()
                sends.append(cp)
        for j, (px, py) in enumerate(chips):
            for a in range(n):
                slot = outs[a].at[2 * px + py]
                pltpu.make_async_remote_copy(
                    src_ref=slot, dst_ref=slot, send_sem=send_sems.at[a, j], recv_sem=recv_sems.at[a, j],
                    device_id=(px, py, c), device_id_type=MESH).wait_recv()
        for cp in sends:
            cp.wait_send()

    return pl.pallas_call(
        body, name=name, in_specs=[ANY] * n, out_specs=[ANY] * n,
        out_shape=[jax.ShapeDtypeStruct(s.shape, s.dtype) for s in sums],
        scratch_shapes=[pltpu.SemaphoreType.DMA((n, 3)), pltpu.SemaphoreType.DMA((n, 3))],
    )(*sums)


def _swap_reduced(name, halves):
    n = len(halves)

    def body(*refs):
        ins, outs = refs[:n], refs[n:2 * n]
        send_sems, recv_sems = refs[2 * n:]
        x, y, c, me, sibling, chips = _place()
        cps = []
        for a in range(n):
            cp = pltpu.make_async_remote_copy(src_ref=ins[a], dst_ref=outs[a], send_sem=send_sems.at[a],
                                              recv_sem=recv_sems.at[a], device_id=sibling, device_id_type=MESH)
            cp.start()
            cps.append(cp)
        for cp in cps:
            cp.wait()

    return pl.pallas_call(
        body, name=name, in_specs=[ANY] * n, out_specs=[ANY] * n,
        out_shape=[jax.ShapeDtypeStruct(h.shape, h.dtype) for h in halves],
        scratch_shapes=[pltpu.SemaphoreType.DMA((n,)), pltpu.SemaphoreType.DMA((n,))],
    )(*halves)


def _row_tile(r, want=256):
    t = (min(r, want) // SUBLANES) * SUBLANES
    while r % t:
        t -= SUBLANES
    return t


def _add_own_half(name, part, got, c, out_dtype):
    _, _, r, cols = part.shape
    tr = _row_tile(r)

    def body(c_ref, p_ref, g_ref, o_ref):
        o_ref[...] = (p_ref[...] + g_ref[...]).astype(o_ref.dtype)

    return pl.pallas_call(
        body, name=name,
        grid_spec=pltpu.PrefetchScalarGridSpec(
            num_scalar_prefetch=1, grid=(N_CHIPS, r // tr),
            in_specs=[pl.BlockSpec((None, None, tr, cols), lambda s, i, c_ref: (s, c_ref[0], i, 0)),
                      pl.BlockSpec((None, tr, cols), lambda s, i, c_ref: (s, i, 0))],
            out_specs=pl.BlockSpec((None, tr, cols), lambda s, i, c_ref: (s, i, 0))),
        out_shape=jax.ShapeDtypeStruct(got.shape, out_dtype),
        compiler_params=_cparams(("parallel", "parallel")),
    )(c.reshape(1).astype(jnp.int32), part, got)


def _sum_chips(name, got, own, me):
    _, r, cols = got.shape
    tr = _row_tile(r)

    def body(me_ref, r0, r1, r2, r3, own_ref, o_ref):
        pick = lambda s, ref: jnp.where(me_ref[0] == s, own_ref[...], ref[...]).astype(F32)
        o_ref[...] = ((pick(0, r0) + pick(1, r1)) + pick(2, r2)) + pick(3, r3)

    def slot(s):
        return pl.BlockSpec((None, tr, cols),
                            lambda i, me_ref: (jnp.where(me_ref[0] == s, (s + 1) % N_CHIPS, s), i, 0))

    return pl.pallas_call(
        body, name=name,
        grid_spec=pltpu.PrefetchScalarGridSpec(
            num_scalar_prefetch=1, grid=(r // tr,),
            in_specs=[slot(s) for s in range(N_CHIPS)]
            + [pl.BlockSpec((None, tr, cols), lambda i, me_ref: (me_ref[0], i, 0))],
            out_specs=pl.BlockSpec((tr, cols), lambda i, me_ref: (i, 0))),
        out_shape=jax.ShapeDtypeStruct((r, cols), F32),
        compiler_params=_cparams(("parallel",)),
    )(me.reshape(1).astype(jnp.int32), got, got, got, got, own)


def _adamw_math(wv, gv, mv, vv):
    mv = ADAM_B1 * mv + (1.0 - ADAM_B1) * gv
    vv = ADAM_B2 * vv + (1.0 - ADAM_B2) * (gv * gv)
    m_hat = mv / (1.0 - ADAM_B1 ** ADAM_STEP)
    v_hat = vv / (1.0 - ADAM_B2 ** ADAM_STEP)
    return -ADAM_LR * (m_hat / (jnp.sqrt(v_hat) + ADAM_EPS) + ADAM_WD * wv), mv, vv


def _adamw(name, w, g, m, v):
    cols = w.shape[1]
    return _ew(name, _adamw_math, [('r', w), ('r', g), ('r', m), ('r', v)], [('r', cols, F32)] * 3,
               tr=_row_tile(w.shape[0], 128))


def _adamw_halves(name, w, mine, theirs, m, v, c):
    r, cols = mine.shape
    tr = _row_tile(r, 128)
    nb = r // tr

    def body(c_ref, w_ref, a_ref, b_ref, m_ref, v_ref, g_out, d_out, m_out, v_out):
        g = jnp.where(pl.program_id(0) == c_ref[0], a_ref[...], b_ref[...])
        g_out[...] = g
        d_out[...], m_out[...], v_out[...] = _adamw_math(w_ref[...], g, m_ref[...], v_ref[...])

    whole = pl.BlockSpec((tr, cols), lambda h, i, c_ref: (h * nb + i, 0))
    half = pl.BlockSpec((tr, cols), lambda h, i, c_ref: (i, 0))
    return pl.pallas_call(
        body, name=name,
        grid_spec=pltpu.PrefetchScalarGridSpec(
            num_scalar_prefetch=1, grid=(2, nb),
            in_specs=[whole, half, half, whole, whole], out_specs=[whole] * 4),
        out_shape=[jax.ShapeDtypeStruct(w.shape, F32)] * 4,
        compiler_params=_cparams(("parallel", "parallel")),
    )(c.reshape(1).astype(jnp.int32), w, mine, theirs, m, v)


SMALL_ROWS_ALIGN = 2 * N_CHIPS * SUBLANES


def _pack_small(d):
    flat = jnp.concatenate([d[n].reshape(-1).astype(F32) for n in SMALL_NAMES])
    rows = -(-flat.shape[0] // (LANES * SMALL_ROWS_ALIGN)) * SMALL_ROWS_ALIGN
    return jnp.pad(flat, (0, rows * LANES - flat.shape[0])).reshape(rows, LANES)


def _unpack_small(packed, like):
    flat = packed.reshape(-1)
    out, off = {}, 0
    for n in SMALL_NAMES:
        size = like[n].size
        out[n] = flat[off:off + size].reshape(like[n].shape)
        off += size
    return out


def kernel(x, g_mix, w_in, q_gain, k_gain, rpb, ssm_a_re, ssm_a_im, ssm_b_re, ssm_b_im, ssm_c_re, ssm_c_im, ssm_log_step, ssm_d, w_glu, b_glu, g_out_attn, g_out_ssm, w_out, g_ffn, w_ffn_gate, w_ffn_up, w_ffn_down, loss_target, m_g_mix, m_w_in, m_q_gain, m_k_gain, m_rpb, m_ssm_a_re, m_ssm_a_im, m_ssm_b_re, m_ssm_b_im, m_ssm_c_re, m_ssm_c_im, m_ssm_log_step, m_ssm_d, m_w_glu, m_b_glu, m_g_out_attn, m_g_out_ssm, m_w_out, m_g_ffn, m_w_ffn_gate, m_w_ffn_up, m_w_ffn_down, v_g_mix, v_w_in, v_q_gain, v_k_gain, v_rpb, v_ssm_a_re, v_ssm_a_im, v_ssm_b_re, v_ssm_b_im, v_ssm_c_re, v_ssm_c_im, v_ssm_log_step, v_ssm_d, v_w_glu, v_b_glu, v_g_out_attn, v_g_out_ssm, v_w_out, v_g_ffn, v_w_ffn_gate, v_w_ffn_up, v_w_ffn_down):
    given = dict(locals())
    w = {n: given[n][0] for n in WEIGHT_NAMES}
    mom = {n: given["m_" + n][0] for n in WEIGHT_NAMES}
    var = {n: given["v_" + n][0] for n in WEIGHT_NAMES}
    d = x.shape[-1]
    c = lax.axis_index("c")

    halves = [w[n].astype(MXU_DTYPE).reshape((2, w[n].shape[0] // 2, w[n].shape[1])) for n in BIG_NAMES]
    gathered = _gather_chips("gather_weights", halves)
    big = tuple(g4.reshape((N_CHIPS, -1, g4.shape[-1])) for g4 in gathered)
    big = (big[0], big[1].reshape(-1, big[1].shape[-1]), big[2].reshape(-1, big[2].shape[-1]), big[3], big[4], big[5])

    sq, dx, d_big, d_small = _local_step(x[0], loss_target[0], big, {n: w[n] for n in SMALL_NAMES})
    loss = lax.psum(0.5 * sq / d, ("x", "y", "c"))

    parts = [g.reshape((N_CHIPS, 2, -1, g.shape[-1])) for g in d_big]
    small_part = _pack_small(d_small)
    srows = small_part.shape[0]
    parts.append(small_part.reshape(N_CHIPS, 2, srows // (2 * N_CHIPS), LANES))
    got = _swap_halves("reduce_swap_halves", parts)
    payload = [GRAD_PAYLOAD_DTYPE] * len(BIG_NAMES) + [F32]
    sums = [_add_own_half("reduce_add_%d" % a, p, gt, c, dt) for a, (p, gt, dt) in enumerate(zip(parts, got, payload))]
    got = _scatter_chips("reduce_scatter_chips", sums)
    me = 2 * lax.axis_index("x") + lax.axis_index("y")
    mine = [_sum_chips("reduce_sum_%d" % a, gt, sm_, me) for a, (gt, sm_) in enumerate(zip(got, sums))]
    theirs = _swap_reduced("reduce_swap_reduced", mine)
    small_halves = jnp.where(c == 0, jnp.stack([mine[-1], theirs[-1]]), jnp.stack([theirs[-1], mine[-1]]))
    (small_all,) = _gather_chips("gather_small", [small_halves])
    small_all = small_all.reshape(srows, LANES)
    grad_small = _unpack_small(small_all, {n: w[n] for n in SMALL_NAMES})

    grad_big, delta, new_m, new_v = {}, {}, {}, {}
    for a, n in enumerate(BIG_NAMES):
        grad_big[n], delta[n], new_m[n], new_v[n] = _adamw_halves("adamw_%d" % a, w[n], mine[a], theirs[a],
                                                                  mom[n], var[n], c)
    sd, sm, sv = _adamw("adamw_small", _pack_small({n: w[n] for n in SMALL_NAMES}), small_all,
                        _pack_small({n: mom[n] for n in SMALL_NAMES}), _pack_small({n: var[n] for n in SMALL_NAMES}))
    like = {n: w[n] for n in SMALL_NAMES}
    delta.update(_unpack_small(sd, like))
    new_m.update(_unpack_small(sm, like))
    new_v.update(_unpack_small(sv, like))
    grads = {**grad_big, **grad_small}
    lead = lambda t: t[None]
    return (loss, dx[None], *[lead(grads[n]) for n in WEIGHT_NAMES], *[lead(delta[n]) for n in WEIGHT_NAMES],
            *[lead(new_m[n]) for n in WEIGHT_NAMES], *[lead(new_v[n]) for n in WEIGHT_NAMES])
```

```python
import functools
import math

import jax
import jax.numpy as jnp
from jax import lax
from jax.experimental import pallas as pl
from jax.experimental.pallas import tpu as pltpu

F32 = jnp.float32
BF16 = jnp.bfloat16
MXU_DTYPE = BF16
GRAD_PAYLOAD_DTYPE = BF16
HI = lax.Precision.HIGHEST
VMEM_LIMIT_V7X = 56 * 1024 * 1024
LANES = 128
SUBLANES = 8

GRID_W = 64
WIN_H = 8
WIN_W = 16
HEAD_DIM = 64
SSM_GROUP_CH = 16
SSM_STATE = 64
S5_CHUNK = 16
RMS_EPS = 1e-6
NEG_INF = -1e30
N_CHIPS = 4
MESH = pl.DeviceIdType.MESH

ADAM_LR = 0.001
ADAM_B1 = 0.9
ADAM_B2 = 0.999
ADAM_EPS = 1e-08
ADAM_WD = 0.01
ADAM_STEP = 10

WEIGHT_NAMES = ['g_mix', 'w_in', 'q_gain', 'k_gain', 'rpb', 'ssm_a_re', 'ssm_a_im', 'ssm_b_re', 'ssm_b_im',
                'ssm_c_re', 'ssm_c_im', 'ssm_log_step', 'ssm_d', 'w_glu', 'b_glu', 'g_out_attn', 'g_out_ssm',
                'w_out', 'g_ffn', 'w_ffn_gate', 'w_ffn_up', 'w_ffn_down']
BIG_NAMES = ['w_in', 'w_glu', 'w_out', 'w_ffn_gate', 'w_ffn_up', 'w_ffn_down']
SMALL_NAMES = [n for n in WEIGHT_NAMES if n not in BIG_NAMES]


def _cparams(sem):
    return pltpu.CompilerParams(dimension_semantics=sem, vmem_limit_bytes=VMEM_LIMIT_V7X)


def _tile(n, want):
    if n <= want:
        return n
    t = (want // LANES) * LANES
    while t >= LANES:
        if n % t == 0:
            return t
        t -= LANES
    return n


def _mm(name, a, b, *, contract, a_mode='2', b_mode='2', o_mode='2', out_dtype=F32, add=None, exact=False,
        tm=1024, tn=1024, tk=512):
    dn = {'nn': (((1,), (0,)), ((), ())), 'nt': (((1,), (1,)), ((), ())), 'tn': (((0,), (0,)), ((), ()))}[contract]
    ar, ac = a.shape[-2:]
    br, bc = b.shape[-2:]
    m, kdim = (ar, ac) if contract != 'tn' else (ac, ar)
    n = bc if contract != 'nt' else br
    assert kdim == (br if contract != 'nt' else bc), (name, a.shape, b.shape)
    nbatch = 1
    for arr, mode in ((a, a_mode), (b, b_mode)):
        if mode == 'b':
            nbatch = arr.shape[0]
    nstack = 1
    for arr, mode in ((a, a_mode), (b, b_mode)):
        if mode == 'c':
            nstack = arr.shape[0]
    tm, tn, tk = _tile(m, tm), _tile(n, tn), _tile(kdim, tk)
    nkin = kdim // tk
    nk = nstack * nkin
    grid = (nbatch, m // tm, n // tn, nk)

    def spec(mode, block, rc):
        def imap(s, i, j, kk):
            r, c = rc(i, j, kk % nkin)
            if mode == '2':
                return (r, c)
            return (s if mode == 'b' else kk // nkin, r, c)
        return pl.BlockSpec(block if mode == '2' else (None,) + block, imap)

    a_spec = spec(a_mode, (tm, tk) if contract != 'tn' else (tk, tm),
                  (lambda i, j, k: (i, k)) if contract != 'tn' else (lambda i, j, k: (k, i)))
    b_spec = spec(b_mode, (tk, tn) if contract != 'nt' else (tn, tk),
                  (lambda i, j, k: (k, j)) if contract != 'nt' else (lambda i, j, k: (j, k)))
    o_spec = spec(o_mode, (tm, tn), lambda i, j, k: (i, j))
    out_shape = (m, n) if o_mode == '2' else (nbatch, m, n)
    has_add = add is not None

    def body(*refs):
        if has_add:
            a_ref, b_ref, add_ref, o_ref, acc_ref = refs
        else:
            a_ref, b_ref, o_ref, acc_ref = refs
        k = pl.program_id(3)

        @pl.when(k == 0)
        def _():
            acc_ref[...] = jnp.zeros_like(acc_ref)

        if exact:
            acc_ref[...] += lax.dot_general(a_ref[...].astype(F32), b_ref[...].astype(F32), dn, precision=HI,
                                            preferred_element_type=F32)
        else:
            acc_ref[...] += lax.dot_general(a_ref[...].astype(MXU_DTYPE), b_ref[...].astype(MXU_DTYPE), dn,
                                            preferred_element_type=F32)

        @pl.when(k == nk - 1)
        def _():
            r = acc_ref[...]
            if has_add:
                r = r + add_ref[...].astype(F32)
            o_ref[...] = r.astype(o_ref.dtype)

    in_specs = [a_spec, b_spec] + ([o_spec] if has_add else [])
    args = (a, b) + ((add,) if has_add else ())
    return pl.pallas_call(
        body, name=name, grid=grid, in_specs=in_specs, out_specs=o_spec,
        out_shape=jax.ShapeDtypeStruct(out_shape, out_dtype),
        scratch_shapes=[pltpu.VMEM((tm, tn), F32)],
        compiler_params=_cparams(("parallel", "parallel", "parallel", "arbitrary")),
    )(*args)


def _ew(name, fn, ins, outs, tr=256):
    rows = next(x[1].shape[0] for x in ins if x[0] == 'r')
    tr = min(tr, rows)
    assert rows % tr == 0 and tr % SUBLANES == 0, (name, rows, tr)
    in_specs, args = [], []
    for x in ins:
        if x[0] == 'r' and len(x) == 2:
            in_specs.append(pl.BlockSpec((tr, x[1].shape[1]), lambda i: (i, 0)))
        elif x[0] == 'r':
            in_specs.append(pl.BlockSpec((tr, x[3]), functools.partial(lambda cb, i: (i, cb), x[2])))
        else:
            in_specs.append(pl.BlockSpec(x[1].shape, lambda i: (0, 0)))
        args.append(x[1])
    out_specs, out_shapes = [], []
    for o in outs:
        if o[0] == 'r':
            out_specs.append(pl.BlockSpec((tr, o[1]), lambda i: (i, 0)))
            out_shapes.append(jax.ShapeDtypeStruct((rows, o[1]), o[2]))
        else:
            out_specs.append(pl.BlockSpec((SUBLANES, o[1]), lambda i: (0, 0)))
            out_shapes.append(jax.ShapeDtypeStruct((SUBLANES, o[1]), F32))
    nin = len(ins)
    has_acc = any(o[0] == 'a' for o in outs)

    def body(*refs):
        vals = fn(*[r[...] for r in refs[:nin]])
        if not isinstance(vals, (tuple, list)):
            vals = (vals,)
        i = pl.program_id(0)
        for o, ref, v in zip(outs, refs[nin:], vals):
            if o[0] == 'r':
                ref[...] = v.astype(ref.dtype)
            else:
                part = v.astype(F32).reshape(tr // SUBLANES, SUBLANES, o[1]).sum(axis=0)

                @pl.when(i == 0)
                def _(ref=ref, part=part):
                    ref[...] = part

                @pl.when(i > 0)
                def _(ref=ref, part=part):
                    ref[...] += part

    res = pl.pallas_call(
        body, name=name, grid=(rows // tr,), in_specs=in_specs, out_specs=out_specs, out_shape=out_shapes,
        compiler_params=_cparams(("arbitrary",) if has_acc else ("parallel",)),
    )(*args)
    return res


def _rms(x, g):
    r = lax.rsqrt(jnp.mean(x * x, axis=-1, keepdims=True) + RMS_EPS)
    xr = x * r
    return xr * g, xr


def _rms_bwd(x, g, dy):
    r = lax.rsqrt(jnp.mean(x * x, axis=-1, keepdims=True) + RMS_EPS)
    xr = x * r
    gdy = g * dy
    dx = r * (gdy - xr * jnp.mean(xr * gdy, axis=-1, keepdims=True))
    return dx, dy * xr


def _sigmoid(x):
    return 1.0 / (1.0 + jnp.exp(-x))


_GELU_C = math.sqrt(2.0 / math.pi)


def _gelu(x):
    return 0.5 * x * (1.0 + jnp.tanh(_GELU_C * (x + 0.044715 * x * x * x)))


def _gelu_grad(x):
    t = jnp.tanh(_GELU_C * (x + 0.044715 * x * x * x))
    return 0.5 * (1.0 + t) + 0.5 * x * (1.0 - t * t) * _GELU_C * (1.0 + 3 * 0.044715 * x * x)


ATTN_ROWS_PER_STEP = 8
NT_DIMS = (((1,), (1,)), ((), ()))
NN_DIMS = (((1,), (0,)), ((), ()))
TN_DIMS = (((0,), (0,)), ((), ()))


def _attn_geometry(r, rows):
    row_start = jnp.clip(r - WIN_H // 2, 0, rows - WIN_H)
    key0 = pl.multiple_of(row_start * GRID_W, GRID_W)
    bias0 = pl.multiple_of((row_start - r + (WIN_H - 1)) * GRID_W, GRID_W)
    return key0, bias0


def _window_onehot():
    c = jnp.arange(GRID_W)
    col_start = jnp.clip(c - WIN_W // 2, 0, GRID_W - WIN_W)
    col_in = (c[None, :] >= col_start[:, None]) & (c[None, :] < col_start[:, None] + WIN_W)
    dc = jnp.clip(c[None, :] - c[:, None], -(WIN_W - 1), WIN_W - 1) + (WIN_W - 1)
    onehot = ((dc[:, :, None] == jnp.arange(2 * WIN_W - 1)[None, None, :]) & col_in[:, :, None]).astype(F32)
    return onehot, col_in


def _bias_table(rpb):
    onehot, col_in = _window_onehot()
    nh = rpb.shape[0]
    tab = jnp.einsum('perd,qkd->prkeq', rpb.reshape(nh // 2, 2, 2 * WIN_H - 1, 2 * WIN_W - 1), onehot, precision=HI)
    tab = tab + jnp.where(col_in, 0.0, NEG_INF).T[None, None, :, None, :]
    return tab.reshape(nh // 2, (2 * WIN_H - 1) * GRID_W, 2 * GRID_W)


def _bias_table_grad(dtab):
    onehot, _ = _window_onehot()
    npair = dtab.shape[0]
    d = dtab.reshape(npair, 2 * WIN_H - 1, GRID_W, 2, GRID_W)
    return jnp.einsum('prkeq,qkd->perd', d, onehot, precision=HI).reshape(2 * npair, 2 * WIN_H - 1, 2 * WIN_W - 1)


def _lane_lo(shape):
    return lax.broadcasted_iota(jnp.int32, shape, 1) < HEAD_DIM


def _half_sums(v):
    lo = _lane_lo(v.shape)
    s_lo = jnp.sum(jnp.where(lo, v, 0.0), axis=1, keepdims=True)
    s_hi = jnp.sum(jnp.where(lo, 0.0, v), axis=1, keepdims=True)
    return jnp.where(lo, s_lo, s_hi)


def _rms_pair(x, g):
    r = lax.rsqrt(_half_sums(x * x) * (1.0 / HEAD_DIM) + RMS_EPS)
    return x * r * g


def _rms_pair_bwd(x, g, dy):
    r = lax.rsqrt(_half_sums(x * x) * (1.0 / HEAD_DIM) + RMS_EPS)
    xr = x * r
    gdy = g * dy
    dx = r * (gdy - xr * (_half_sums(xr * gdy) * (1.0 / HEAD_DIM)))
    return dx, dy * xr


def _blockdiag(a):
    a2 = jnp.concatenate([a, a], axis=0)
    row_hi = lax.broadcasted_iota(jnp.int32, a2.shape, 0) >= GRID_W
    lane_hi = lax.broadcasted_iota(jnp.int32, a2.shape, 1) >= HEAD_DIM
    return jnp.where(row_hi == lane_hi, a2, 0.0).astype(MXU_DTYPE)


def _diag_blocks(m):
    return jnp.where(_lane_lo((GRID_W, 2 * HEAD_DIM)), m[:GRID_W], m[GRID_W:])


def _attn_scores(qb, kb, bias):
    st = lax.dot_general(kb, qb, NT_DIMS, preferred_element_type=F32)
    st = st * (1.0 / math.sqrt(HEAD_DIM)) + bias
    mx = jnp.max(st, axis=0, keepdims=True)
    p = jnp.exp(st - mx)
    return p * (1.0 / jnp.sum(p, axis=0, keepdims=True))


def _attn_fwd(z4, qg2, kg2, bias_t):
    _, t, aw = z4.shape
    rows = t // GRID_W
    npair = aw // (2 * HEAD_DIM)
    nkeys = WIN_H * GRID_W
    nb = bias_t.shape[1]
    rps = min(ATTN_ROWS_PER_STEP, rows)
    blk = rps * GRID_W

    def body(q_ref, k_ref, v_ref, qg_ref, kg_ref, b_ref, o_ref, kn_ref, vb_ref):
        rb = pl.program_id(1)

        @pl.when(rb == 0)
        def _():
            kn_ref[...] = _rms_pair(k_ref[...], kg_ref[...]).astype(MXU_DTYPE)
            vb_ref[...] = v_ref[...].astype(MXU_DTYPE)

        def row(i, carry):
            key0, bias0 = _attn_geometry(rb * rps + i, rows)
            at = pl.ds(pl.multiple_of(i * GRID_W, GRID_W), GRID_W)
            qb = _blockdiag(_rms_pair(q_ref[at, :], qg_ref[...]))
            pt = _attn_scores(qb, kn_ref[pl.ds(key0, nkeys), :], b_ref[pl.ds(bias0, nkeys), :])
            both = lax.dot_general(pt.astype(MXU_DTYPE), vb_ref[pl.ds(key0, nkeys), :], TN_DIMS,
                                   preferred_element_type=F32)
            o_ref[at, :] = _diag_blocks(both)
            return carry

        lax.fori_loop(0, rps, row, 0, unroll=2)

    pair_cols = lambda lead: pl.BlockSpec((None, t, 2 * HEAD_DIM), lambda p, r: (lead, 0, p))
    return pl.pallas_call(
        body, name="attn_fwd", grid=(npair, rows // rps),
        in_specs=[pl.BlockSpec((None, blk, 2 * HEAD_DIM), lambda p, r: (0, r, p)), pair_cols(1), pair_cols(2),
                  pl.BlockSpec((1, 2 * HEAD_DIM), lambda p, r: (0, 0)),
                  pl.BlockSpec((1, 2 * HEAD_DIM), lambda p, r: (0, 0)),
                  pl.BlockSpec((None, nb, 2 * GRID_W), lambda p, r: (p, 0, 0))],
        out_specs=pl.BlockSpec((blk, 2 * HEAD_DIM), lambda p, r: (r, p)),
        out_shape=jax.ShapeDtypeStruct((t, aw), F32),
        scratch_shapes=[pltpu.VMEM((t, 2 * HEAD_DIM), MXU_DTYPE), pltpu.VMEM((t, 2 * HEAD_DIM), MXU_DTYPE)],
        compiler_params=_cparams(("parallel", "arbitrary")),
    )(z4, z4, z4, qg2, kg2, bias_t)


def _attn_bwd(z4, qg2, kg2, bias_t, dya):
    _, t, aw = z4.shape
    rows = t // GRID_W
    npair = aw // (2 * HEAD_DIM)
    nkeys = WIN_H * GRID_W
    nb = bias_t.shape[1]
    rps = min(ATTN_ROWS_PER_STEP, rows)
    blk = rps * GRID_W
    scale = 1.0 / math.sqrt(HEAD_DIM)

    def body(q_ref, k_ref, v_ref, qg_ref, kg_ref, b_ref, do_ref, dz_ref, db_ref, dqg_ref, dkg_ref, kn_ref, vb_ref):
        rb = pl.program_id(1)

        @pl.when(rb == 0)
        def _():
            kn_ref[...] = _rms_pair(k_ref[...], kg_ref[...]).astype(MXU_DTYPE)
            vb_ref[...] = v_ref[...].astype(MXU_DTYPE)
            dz_ref[1] = jnp.zeros((t, 2 * HEAD_DIM), F32)
            dz_ref[2] = jnp.zeros((t, 2 * HEAD_DIM), F32)
            db_ref[...] = jnp.zeros_like(db_ref)
            dqg_ref[...] = jnp.zeros_like(dqg_ref)

        def row(i, dqg_sum):
            r = rb * rps + i
            key0, bias0 = _attn_geometry(r, rows)
            keys = pl.ds(key0, nkeys)
            at = pl.ds(pl.multiple_of(i * GRID_W, GRID_W), GRID_W)
            q = q_ref[at, :]
            qb = _blockdiag(_rms_pair(q, qg_ref[...]))
            dob = _blockdiag(do_ref[at, :])
            kb = kn_ref[keys, :]
            pt = _attn_scores(qb, kb, b_ref[pl.ds(bias0, nkeys), :])
            dz_ref[2, keys, :] += lax.dot_general(pt.astype(MXU_DTYPE), dob, NN_DIMS, preferred_element_type=F32)
            dpt = lax.dot_general(vb_ref[keys, :], dob, NT_DIMS, preferred_element_type=F32)
            dst = pt * (dpt - jnp.sum(pt * dpt, axis=0, keepdims=True))
            db_ref[pl.ds(bias0, nkeys), :] += dst
            dsb = dst.astype(MXU_DTYPE)
            dz_ref[1, keys, :] += scale * lax.dot_general(dsb, qb, NN_DIMS, preferred_element_type=F32)
            dqn = scale * _diag_blocks(lax.dot_general(dsb, kb, TN_DIMS, preferred_element_type=F32))
            dq, dqg = _rms_pair_bwd(q, qg_ref[...], dqn)
            dz_ref[0, pl.ds(pl.multiple_of(r * GRID_W, GRID_W), GRID_W), :] = dq
            return dqg_sum + jnp.sum(dqg, axis=0, keepdims=True)

        dqg_ref[...] += lax.fori_loop(0, rps, row, jnp.zeros((1, 2 * HEAD_DIM), F32), unroll=2)

        @pl.when(rb == rows // rps - 1)
        def _():
            dk, dkg = _rms_pair_bwd(k_ref[...], kg_ref[...], dz_ref[1])
            dz_ref[1] = dk
            dkg_ref[...] = jnp.sum(dkg, axis=0, keepdims=True)

    pair_cols = lambda lead: pl.BlockSpec((None, t, 2 * HEAD_DIM), lambda p, r: (lead, 0, p))
    pair_vec = pl.BlockSpec((None, 1, 2 * HEAD_DIM), lambda p, r: (p, 0, 0))
    return pl.pallas_call(
        body, name="attn_bwd", grid=(npair, rows // rps),
        in_specs=[pl.BlockSpec((None, blk, 2 * HEAD_DIM), lambda p, r: (0, r, p)), pair_cols(1), pair_cols(2),
                  pl.BlockSpec((1, 2 * HEAD_DIM), lambda p, r: (0, 0)),
                  pl.BlockSpec((1, 2 * HEAD_DIM), lambda p, r: (0, 0)),
                  pl.BlockSpec((None, nb, 2 * GRID_W), lambda p, r: (p, 0, 0)),
                  pl.BlockSpec((blk, 2 * HEAD_DIM), lambda p, r: (r, p))],
        out_specs=[pl.BlockSpec((3, t, 2 * HEAD_DIM), lambda p, r: (0, 0, p)),
                   pl.BlockSpec((None, nb, 2 * GRID_W), lambda p, r: (p, 0, 0)),
                   pair_vec, pair_vec],
        out_shape=[jax.ShapeDtypeStruct((4, t, aw), F32), jax.ShapeDtypeStruct(bias_t.shape, F32),
                   jax.ShapeDtypeStruct((npair, 1, 2 * HEAD_DIM), F32),
                   jax.ShapeDtypeStruct((npair, 1, 2 * HEAD_DIM), F32)],
        scratch_shapes=[pltpu.VMEM((t, 2 * HEAD_DIM), MXU_DTYPE), pltpu.VMEM((t, 2 * HEAD_DIM), MXU_DTYPE)],
        compiler_params=_cparams(("parallel", "arbitrary")),
    )(z4, z4, z4, qg2, kg2, bias_t, dya)


def _attn_fwd_old(z4, qg2, kg2, bias_t):
    _, t, aw = z4.shape
    rows = t // GRID_W
    npair = aw // (2 * HEAD_DIM)
    nkeys = WIN_H * GRID_W
    nb = bias_t.shape[1]

    def body(q_ref, k_ref, v_ref, qg_ref, kg_ref, b_ref, o_ref, kn_ref):
        r = pl.program_id(1)

        @pl.when(r == 0)
        def _():
            for e in range(2):
                sl = slice(e * HEAD_DIM, (e + 1) * HEAD_DIM)
                kn_ref[e] = _rms(k_ref[:, sl], kg_ref[:, sl])[0]

        key0, bias0 = _attn_geometry(r, rows)
        outs = []
        for e in range(2):
            sl = slice(e * HEAD_DIM, (e + 1) * HEAD_DIM)
            qn = _rms(q_ref[:, sl], qg_ref[:, sl])[0]
            kb = kn_ref[e, pl.ds(key0, nkeys), :]
            vb = v_ref[pl.ds(key0, nkeys), sl]
            pt = _attn_scores(qn, kb, b_ref[e, pl.ds(bias0, nkeys), :])
            outs.append(lax.dot_general(pt.astype(MXU_DTYPE), vb.astype(MXU_DTYPE), (((0,), (0,)), ((), ())),
                                        preferred_element_type=F32))
        o_ref[...] = jnp.concatenate(outs, axis=1)

    return pl.pallas_call(
        body, name="attn_fwd", grid=(npair, rows),
        in_specs=[pl.BlockSpec((None, GRID_W, 2 * HEAD_DIM), lambda p, r: (0, r, p)),
                  pl.BlockSpec((None, t, 2 * HEAD_DIM), lambda p, r: (1, 0, p)),
                  pl.BlockSpec((None, t, 2 * HEAD_DIM), lambda p, r: (2, 0, p)),
                  pl.BlockSpec((1, 2 * HEAD_DIM), lambda p, r: (0, 0)),
                  pl.BlockSpec((1, 2 * HEAD_DIM), lambda p, r: (0, 0)),
                  pl.BlockSpec((2, nb, GRID_W), lambda p, r: (p, 0, 0))],
        out_specs=pl.BlockSpec((GRID_W, 2 * HEAD_DIM), lambda p, r: (r, p)),
        out_shape=jax.ShapeDtypeStruct((t, aw), F32),
        scratch_shapes=[pltpu.VMEM((2, t, HEAD_DIM), F32)],
        compiler_params=_cparams(("parallel", "arbitrary")),
    )(z4, z4, z4, qg2, kg2, bias_t)


def _attn_bwd_old(z4, qg2, kg2, bias_t, dya):
    _, t, aw = z4.shape
    rows = t // GRID_W
    npair = aw // (2 * HEAD_DIM)
    nkeys = WIN_H * GRID_W
    nb = bias_t.shape[1]
    scale = 1.0 / math.sqrt(HEAD_DIM)

    def body(q_ref, k_ref, v_ref, qg_ref, kg_ref, b_ref, do_ref,
             dq_ref, dk_ref, dv_ref, db_ref, dqg_ref, dkg_ref, kn_ref, dkn_ref, dva_ref):
        r = pl.program_id(1)

        @pl.when(r == 0)
        def _():
            for e in range(2):
                sl = slice(e * HEAD_DIM, (e + 1) * HEAD_DIM)
                kn_ref[e] = _rms(k_ref[:, sl], kg_ref[:, sl])[0]
            dkn_ref[...] = jnp.zeros_like(dkn_ref)
            dva_ref[...] = jnp.zeros_like(dva_ref)
            db_ref[...] = jnp.zeros_like(db_ref)
            dqg_ref[...] = jnp.zeros_like(dqg_ref)

        key0, bias0 = _attn_geometry(r, rows)
        dqs, dqgs = [], []
        for e in range(2):
            sl = slice(e * HEAD_DIM, (e + 1) * HEAD_DIM)
            q = q_ref[:, sl]
            qn = _rms(q, qg_ref[:, sl])[0]
            kb = kn_ref[e, pl.ds(key0, nkeys), :]
            vb = v_ref[pl.ds(key0, nkeys), sl]
            do = do_ref[:, sl]
            pt = _attn_scores(qn, kb, b_ref[e, pl.ds(bias0, nkeys), :])
            dva_ref[e, pl.ds(key0, nkeys), :] += lax.dot_general(
                pt.astype(MXU_DTYPE), do.astype(MXU_DTYPE), (((1,), (0,)), ((), ())), preferred_element_type=F32)
            dpt = lax.dot_general(vb.astype(MXU_DTYPE), do.astype(MXU_DTYPE), (((1,), (1,)), ((), ())),
                                  preferred_element_type=F32)
            dst = pt * (dpt - jnp.sum(pt * dpt, axis=0, keepdims=True))
            db_ref[e, pl.ds(bias0, nkeys), :] += dst
            dsb = dst.astype(MXU_DTYPE)
            dqn = scale * lax.dot_general(dsb, kb.astype(MXU_DTYPE), (((0,), (0,)), ((), ())),
                                          preferred_element_type=F32)
            dkn_ref[e, pl.ds(key0, nkeys), :] += scale * lax.dot_general(
                dsb, qn.astype(MXU_DTYPE), (((1,), (0,)), ((), ())), preferred_element_type=F32)
            dq, dqg = _rms_bwd(q, qg_ref[:, sl], dqn)
            dqs.append(dq)
            dqgs.append(jnp.sum(dqg, axis=0, keepdims=True))
        dq_ref[...] = jnp.concatenate(dqs, axis=1)
        dqg_ref[...] += jnp.concatenate(dqgs, axis=1)

        @pl.when(r == rows - 1)
        def _():
            dks, dkgs = [], []
            for e in range(2):
                sl = slice(e * HEAD_DIM, (e + 1) * HEAD_DIM)
                dk, dkg = _rms_bwd(k_ref[:, sl], kg_ref[:, sl], dkn_ref[e])
                dks.append(dk)
                dkgs.append(jnp.sum(dkg, axis=0, keepdims=True))
            dk_ref[...] = jnp.concatenate(dks, axis=1)
            dv_ref[...] = jnp.concatenate([dva_ref[0], dva_ref[1]], axis=1)
            dkg_ref[...] = jnp.concatenate(dkgs, axis=1)

    pair_vec = pl.BlockSpec((None, 1, 2 * HEAD_DIM), lambda p, r: (p, 0, 0))
    return pl.pallas_call(
        body, name="attn_bwd", grid=(npair, rows),
        in_specs=[pl.BlockSpec((None, GRID_W, 2 * HEAD_DIM), lambda p, r: (0, r, p)),
                  pl.BlockSpec((None, t, 2 * HEAD_DIM), lambda p, r: (1, 0, p)),
                  pl.BlockSpec((None, t, 2 * HEAD_DIM), lambda p, r: (2, 0, p)),
                  pl.BlockSpec((1, 2 * HEAD_DIM), lambda p, r: (0, 0)),
                  pl.BlockSpec((1, 2 * HEAD_DIM), lambda p, r: (0, 0)),
                  pl.BlockSpec((2, nb, GRID_W), lambda p, r: (p, 0, 0)),
                  pl.BlockSpec((GRID_W, 2 * HEAD_DIM), lambda p, r: (r, p))],
        out_specs=[pl.BlockSpec((GRID_W, 2 * HEAD_DIM), lambda p, r: (r, p)),
                   pl.BlockSpec((t, 2 * HEAD_DIM), lambda p, r: (0, p)),
                   pl.BlockSpec((t, 2 * HEAD_DIM), lambda p, r: (0, p)),
                   pl.BlockSpec((2, nb, GRID_W), lambda p, r: (p, 0, 0)),
                   pair_vec, pair_vec],
        out_shape=[jax.ShapeDtypeStruct((t, aw), F32), jax.ShapeDtypeStruct((t, aw), F32),
                   jax.ShapeDtypeStruct((t, aw), F32), jax.ShapeDtypeStruct(bias_t.shape, F32),
                   jax.ShapeDtypeStruct((npair, 1, 2 * HEAD_DIM), F32),
                   jax.ShapeDtypeStruct((npair, 1, 2 * HEAD_DIM), F32)],
        scratch_shapes=[pltpu.VMEM((2, t, HEAD_DIM), F32), pltpu.VMEM((2, t, HEAD_DIM), F32),
                        pltpu.VMEM((2, t, HEAD_DIM), F32)],
        compiler_params=_cparams(("parallel", "arbitrary")),
    )(z4, z4, z4, qg2, kg2, bias_t, dya)


def _s5_mats(a_re, a_im, b_re, b_im, c_re, c_im, log_step, d_skip):
    nd, g, p = a_re.shape
    c = b_re.shape[-1]
    L = S5_CHUNK
    lr = jnp.minimum(a_re, -1e-4).transpose(1, 0, 2)
    li = a_im.transpose(1, 0, 2)
    dt = jnp.exp(log_step).T[..., None]
    n = jnp.arange(L + 1, dtype=F32)[None, :, None, None]
    mag = jnp.exp(n * (lr * dt)[:, None])
    ang = n * (li * dt)[:, None]
    pw_r, pw_i = mag * jnp.cos(ang), mag * jnp.sin(ang)
    den = lr * lr + li * li
    nr, ni = pw_r[:, 1] - 1.0, pw_i[:, 1]
    cr, ci = (nr * lr + ni * li) / den, (ni * lr - nr * li) / den
    bt_r, bt_i = b_re.transpose(1, 3, 0, 2), b_im.transpose(1, 3, 0, 2)
    bb_r = cr[:, None] * bt_r - ci[:, None] * bt_i
    bb_i = cr[:, None] * bt_i + ci[:, None] * bt_r
    ct_r, ct_i = c_re.transpose(1, 0, 3, 2), c_im.transpose(1, 0, 3, 2)

    def times_b(qr, qi):
        qr, qi = qr[:, :, None], qi[:, :, None]
        return qr * bb_r[:, None] - qi * bb_i[:, None], qr * bb_i[:, None] + qi * bb_r[:, None]

    sel = lambda q: jnp.stack([q[:, :L, 0][:, ::-1], q[:, :L, 1]], axis=2)
    ws_r, ws_i = times_b(sel(pw_r), sel(pw_i))
    ws = jnp.stack([ws_r, ws_i], axis=4).reshape(g, L * c, 4 * p)
    sel = lambda q: jnp.stack([q[:, 1:, 0].transpose(0, 2, 1), q[:, 1:, 1][:, ::-1].transpose(0, 2, 1)], axis=1)
    qr, qi = sel(pw_r)[..., None], sel(pw_i)[..., None]
    wo_r = ct_r[:, :, :, None] * qr - ct_i[:, :, :, None] * qi
    wo_i = ct_r[:, :, :, None] * qi + ct_i[:, :, :, None] * qr
    wo = jnp.stack([wo_r, -wo_i], axis=2).reshape(g, 4 * p, L * c)
    kp_r, kp_i = times_b(pw_r[:, :L], pw_i[:, :L])
    kern = (jnp.einsum('gnidp,gdpo->gdnio', kp_r, ct_r, precision=HI)
            - jnp.einsum('gnidp,gdpo->gdnio', kp_i, ct_i, precision=HI))
    skip = d_skip.reshape(g, 1, c, 1) * jnp.eye(c, dtype=F32)[None, None]
    by_offset = jnp.concatenate([kern[:, 1, :0:-1], kern[:, 0, :1] + kern[:, 1, :1] + skip, kern[:, 0, 1:]], axis=1)
    by_offset = by_offset.transpose(0, 2, 1, 3)
    mt = jnp.stack([by_offset[:, :, L - 1 - j:2 * L - 1 - j] for j in range(L)], axis=1)
    mt = mt.reshape(g, L * c, L * c)
    lr16, li16 = pw_r[:, L], pw_i[:, L]
    fa = jnp.concatenate([lr16[:, 0], lr16[:, 0], lr16[:, 1], lr16[:, 1]], axis=-1)
    fb = jnp.concatenate([-li16[:, 0], li16[:, 0], -li16[:, 1], li16[:, 1]], axis=-1)
    return mt, ws, wo, fa, fb


def _gmm(name, a, b, contract, a_stacked=False, b_stacked=False, o_stacked=False, add=None):
    w = S5_CHUNK * SSM_GROUP_CH
    g = (a.shape[0] if a_stacked else a.shape[1] // w)
    dn = {'nn': (((1,), (0,)), ((), ())), 'nt': (((1,), (1,)), ((), ())), 'tn': (((0,), (0,)), ((), ()))}[contract]

    def spec(arr, stacked):
        if stacked:
            return pl.BlockSpec((None,) + arr.shape[1:], lambda i: (i, 0, 0))
        return pl.BlockSpec((arr.shape[0], w), lambda i: (0, i))

    m = (a.shape[1] if a_stacked else a.shape[0]) if contract != 'tn' else w
    n = w
    if o_stacked:
        o_spec = pl.BlockSpec((None, m, n), lambda i: (i, 0, 0))
        o_shape = (g, m, n)
    else:
        o_spec = pl.BlockSpec((m, n), lambda i: (0, i))
        o_shape = (m, g * n)
    has_add = add is not None

    def body(*refs):
        if has_add:
            a_ref, b_ref, add_ref, o_ref = refs
        else:
            a_ref, b_ref, o_ref = refs
        r = lax.dot_general(a_ref[...], b_ref[...], dn, precision=HI, preferred_element_type=F32)
        if has_add:
            r = r + add_ref[...]
        o_ref[...] = r

    in_specs = [spec(a, a_stacked), spec(b, b_stacked)] + ([o_spec] if has_add else [])
    return pl.pallas_call(
        body, name=name, grid=(g,), in_specs=in_specs, out_specs=o_spec,
        out_shape=jax.ShapeDtypeStruct(o_shape, F32), compiler_params=_cparams(("parallel",)),
    )(*((a, b) + ((add,) if has_add else ())))


def _s5_scan(name, s, fa, fb, rev0, xin=None):
    nk, g, w = s.shape
    hw = w // 2
    gb = min(g, 16)
    with_acc = xin is not None

    def body(*refs):
        if with_acc:
            s_ref, a_ref, b_ref, x_ref, o_ref, pa_ref, pb_ref = refs
        else:
            s_ref, a_ref, b_ref, o_ref = refs
        fa0, fb0, fa1, fb1 = a_ref[:, :hw], b_ref[:, :hw], a_ref[:, hw:], b_ref[:, hw:]

        def step(i, carry):
            x0, x1, pa0, pb0, pa1, pb1 = carry
            k0 = (nk - 1 - i) if rev0 else i
            k1 = i if rev0 else (nk - 1 - i)
            o_ref[k0, :, :hw] = x0
            o_ref[k1, :, hw:] = x1
            if with_acc:
                xi0, xi1 = x_ref[k0, :, :hw], x_ref[k1, :, hw:]
                pa0 = pa0 + x0 * xi0
                pb0 = pb0 + x0 * pltpu.roll(xi0, hw // 2, 1)
                pa1 = pa1 + x1 * xi1
                pb1 = pb1 + x1 * pltpu.roll(xi1, hw // 2, 1)
            x0 = fa0 * x0 + fb0 * pltpu.roll(x0, hw // 2, 1) + s_ref[k0, :, :hw]
            x1 = fa1 * x1 + fb1 * pltpu.roll(x1, hw // 2, 1) + s_ref[k1, :, hw:]
            return x0, x1, pa0, pb0, pa1, pb1

        z = jnp.zeros((gb, hw), F32)
        res = lax.fori_loop(0, nk, step, (z, z, z, z, z, z))
        if with_acc:
            pa_ref[:, :hw] = res[2]
            pb_ref[:, :hw] = res[3]
            pa_ref[:, hw:] = res[4]
            pb_ref[:, hw:] = res[5]

    seq = pl.BlockSpec((nk, gb, w), lambda i: (0, i, 0))
    vec = pl.BlockSpec((gb, w), lambda i: (i, 0))
    in_specs = [seq, vec, vec] + ([seq] if with_acc else [])
    out_specs = [seq] + ([vec, vec] if with_acc else [])
    out_shape = [jax.ShapeDtypeStruct((nk, g, w), F32)] + (
        [jax.ShapeDtypeStruct((g, w), F32)] * 2 if with_acc else [])
    return pl.pallas_call(
        body, name=name, grid=(g // gb,), in_specs=in_specs, out_specs=out_specs, out_shape=out_shape,
        compiler_params=_cparams(("parallel",)),
    )(*((s, fa, fb) + ((xin,) if with_acc else ())))


def _to_groups(u):
    t, sw = u.shape
    g = sw // SSM_GROUP_CH
    return u.reshape(t // S5_CHUNK, S5_CHUNK, g, SSM_GROUP_CH).transpose(0, 2, 1, 3).reshape(t // S5_CHUNK, -1)


def _from_groups(y, sw):
    nk = y.shape[0]
    g = sw // SSM_GROUP_CH
    return y.reshape(nk, g, S5_CHUNK, SSM_GROUP_CH).transpose(0, 2, 1, 3).reshape(nk * S5_CHUNK, sw)


def _s5_fwd(u2, mats):
    mt, ws, wo, fa, fb = mats
    nk = u2.shape[0]
    g = mt.shape[0]
    y_intra = _gmm("s5_intra", u2, mt, 'nn', b_stacked=True)
    s = _gmm("s5_chunk_state", u2, ws, 'nn', b_stacked=True)
    (xin,) = _s5_scan("s5_scan", s.reshape(nk, g, -1), fa, fb, False)
    xin = xin.reshape(nk, -1)
    return _gmm("s5_inter", xin, wo, 'nn', b_stacked=True, add=y_intra), xin


def _s5_bwd(u2, xin, mats, dy2):
    mt, ws, wo, fa, fb = mats
    nk = u2.shape[0]
    g = mt.shape[0]
    dxin = _gmm("s5_dxin", dy2, wo, 'nt', b_stacked=True)
    ds, pa, pb = _s5_scan("s5_scan_adj", dxin.reshape(nk, g, -1), fa, -fb, True, xin=xin.reshape(nk, g, -1))
    ds = ds.reshape(nk, -1)
    du_a = _gmm("s5_du_intra", dy2, mt, 'nt', b_stacked=True)
    du2 = _gmm("s5_du_state", ds, ws, 'nt', b_stacked=True, add=du_a)
    dmt = _gmm("s5_dmt", u2, dy2, 'tn', o_stacked=True)
    dws = _gmm("s5_dws", u2, ds, 'tn', o_stacked=True)
    dwo = _gmm("s5_dwo", xin, dy2, 'tn', o_stacked=True)
    return du2, (dmt, dws, dwo, pa, pb)


def _local_step(x, target, big, small):
    t, d = x.shape
    w_in4, w_glu, w_out, w_gate4, w_up4, w_down4 = big
    aw = w_in4.shape[2]
    sw = w_glu.shape[0]
    nh = aw // HEAD_DIM
    ffs = w_gate4.shape[2]
    row = lambda v: v.reshape(1, -1)
    g_mix, g_ffn = row(small['g_mix']), row(small['g_ffn'])
    g_oa, g_os, b_glu = row(small['g_out_attn']), row(small['g_out_ssm']), row(small['b_glu'])
    qg2 = jnp.tile(row(small['q_gain']), (1, 2))
    kg2 = jnp.tile(row(small['k_gain']), (1, 2))

    (h,) = _ew("rms_mix", lambda xv, g: _rms(xv, g)[0], [('r', x), ('c', g_mix)], [('r', d, MXU_DTYPE)])
    z4 = _mm("in_proj", h, w_in4, contract='nn', b_mode='b', o_mode='b')
    bias_t = _bias_table(small['rpb'])
    ya = _attn_fwd(z4, qg2, kg2, bias_t)
    s5_params = tuple(small[n] for n in ('ssm_a_re', 'ssm_a_im', 'ssm_b_re', 'ssm_b_im', 'ssm_c_re', 'ssm_c_im',
                                         'ssm_log_step', 'ssm_d'))
    mats, mats_vjp = jax.vjp(_s5_mats, *s5_params)
    u2 = _to_groups(z4[3])
    ypre2, xin = _s5_fwd(u2, mats)
    ypre = _from_groups(ypre2, sw)
    (yb,) = _ew("gelu", _gelu, [('r', ypre)], [('r', sw, MXU_DTYPE)])
    a_glu = _mm("glu_proj", yb, w_glu, contract='nn')

    def mix_out(yav, ypv, av, bg, goa, gos):
        ys = _gelu(ypv) * _sigmoid(av + bg)
        return jnp.concatenate([_rms(yav, goa)[0], _rms(ys, gos)[0]], axis=1)
    (ycat,) = _ew("mix_out", mix_out, [('r', ya), ('r', ypre), ('r', a_glu), ('c', b_glu), ('c', g_oa), ('c', g_os)],
                  [('r', aw + sw, MXU_DTYPE)])
    x1 = _mm("out_proj", ycat, w_out, contract='nn', add=x)
    (h2,) = _ew("rms_ffn", lambda xv, g: _rms(xv, g)[0], [('r', x1), ('c', g_ffn)], [('r', d, MXU_DTYPE)])
    gate4 = _mm("ffn_gate", h2, w_gate4, contract='nn', b_mode='b', o_mode='b', tn=ffs)
    up4 = _mm("ffn_up", h2, w_up4, contract='nn', b_mode='b', o_mode='b', tn=ffs)
    gate_f, up_f = gate4.reshape(4 * t, ffs), up4.reshape(4 * t, ffs)
    (act,) = _ew("swiglu", lambda gv, uv: gv * _sigmoid(gv) * uv, [('r', gate_f), ('r', up_f)],
                 [('r', ffs, MXU_DTYPE)])
    act4 = act.reshape(4, t, ffs)
    x2 = _mm("ffn_down", act4, w_down4, contract='nn', a_mode='c', b_mode='c', add=x1, tk=ffs)

    def loss_fn(xv, tv):
        diff = xv - tv
        return diff * (1.0 / d), diff * diff
    dx2, sq = _ew("loss", loss_fn, [('r', x2), ('r', target)], [('r', d, F32), ('a', d)])

    dact4 = _mm("ffn_down_dx", dx2, w_down4, contract='nt', b_mode='b', o_mode='b', tn=ffs)
    d_w_down4 = _mm("ffn_down_dw", act4, dx2, contract='tn', a_mode='b', o_mode='b', tm=ffs)

    def swiglu_bwd(dav, gv, uv):
        s = _sigmoid(gv)
        return dav * uv * s * (1.0 + gv * (1.0 - s)), dav * gv * s
    dgate, dup = _ew("swiglu_bwd", swiglu_bwd, [('r', dact4.reshape(4 * t, ffs)), ('r', gate_f), ('r', up_f)],
                     [('r', ffs, MXU_DTYPE), ('r', ffs, MXU_DTYPE)])
    dgate4, dup4 = dgate.reshape(4, t, ffs), dup.reshape(4, t, ffs)
    dh2 = _mm("ffn_gate_dx", dgate4, w_gate4, contract='nt', a_mode='c', b_mode='c', tk=ffs)
    dh2 = _mm("ffn_up_dx", dup4, w_up4, contract='nt', a_mode='c', b_mode='c', add=dh2, tk=ffs)
    d_w_gate4 = _mm("ffn_gate_dw", h2, dgate4, contract='tn', b_mode='b', o_mode='b', tn=ffs)
    d_w_up4 = _mm("ffn_up_dw", h2, dup4, contract='tn', b_mode='b', o_mode='b', tn=ffs)

    def rms_res_bwd(xv, g, dyv, resv):
        dx, dg = _rms_bwd(xv, g, dyv)
        return resv + dx, dg
    dx1, d_g_ffn = _ew("rms_ffn_bwd", rms_res_bwd, [('r', x1), ('c', g_ffn), ('r', dh2), ('r', dx2)],
                       [('r', d, F32), ('a', d)])

    dycat = _mm("out_proj_dx", dx1, w_out, contract='nt')
    d_w_out = _mm("out_proj_dw", ycat, dx1, contract='tn')

    def mix_out_bwd(yav, ypv, av, bg, goa, gos, dca, dcs):
        dya, dgoa = _rms_bwd(yav, goa, dca)
        y = _gelu(ypv)
        s = _sigmoid(av + bg)
        dys, dgos = _rms_bwd(y * s, gos, dcs)
        da = dys * y * s * (1.0 - s)
        return dya, da, dys * s, dgoa, dgos, da
    dya, da, dy_direct, d_g_oa, d_g_os, d_b_glu = _ew(
        "mix_out_bwd", mix_out_bwd,
        [('r', ya), ('r', ypre), ('r', a_glu), ('c', b_glu), ('c', g_oa), ('c', g_os),
         ('r', dycat, 0, aw), ('r', dycat, 1, sw)],
        [('r', aw, F32), ('r', sw, MXU_DTYPE), ('r', sw, F32), ('a', aw), ('a', sw), ('a', sw)])
    dy = _mm("glu_proj_dx", da, w_glu, contract='nt', add=dy_direct)
    d_w_glu = _mm("glu_proj_dw", yb, da, contract='tn')
    (dypre,) = _ew("gelu_bwd", lambda dyv, ypv: dyv * _gelu_grad(ypv), [('r', dy), ('r', ypre)], [('r', sw, F32)])

    du2, dmats = _s5_bwd(u2, xin, mats, _to_groups(dypre))
    d_s5 = mats_vjp(dmats)
    du = _from_groups(du2, sw)
    dz4, dbias_t, dqg, dkg = _attn_bwd(z4, qg2, kg2, bias_t, dya)
    d_rpb = _bias_table_grad(dbias_t)
    fold = lambda v: v.reshape(-1, 2, HEAD_DIM).sum(axis=(0, 1))
    dz4 = dz4.at[3].set(du)

    dh = _mm("in_proj_dx", dz4, w_in4, contract='nt', a_mode='c', b_mode='c')
    d_w_in4 = _mm("in_proj_dw", h, dz4, contract='tn', b_mode='b', o_mode='b')
    dx, d_g_mix = _ew("rms_mix_bwd", rms_res_bwd, [('r', x), ('c', g_mix), ('r', dh), ('r', dx1)],
                      [('r', d, F32), ('a', d)])

    colsum = lambda v: v.sum(axis=0)
    d_small = {
        'g_mix': colsum(d_g_mix), 'q_gain': fold(dqg), 'k_gain': fold(dkg), 'rpb': d_rpb,
        'ssm_a_re': d_s5[0], 'ssm_a_im': d_s5[1], 'ssm_b_re': d_s5[2], 'ssm_b_im': d_s5[3],
        'ssm_c_re': d_s5[4], 'ssm_c_im': d_s5[5], 'ssm_log_step': d_s5[6], 'ssm_d': d_s5[7],
        'b_glu': colsum(d_b_glu), 'g_out_attn': colsum(d_g_oa), 'g_out_ssm': colsum(d_g_os), 'g_ffn': colsum(d_g_ffn),
    }
    d_big = (d_w_in4, d_w_glu, d_w_out, d_w_gate4, d_w_up4, d_w_down4)
    return jnp.sum(sq), dx, d_big, d_small


ANY = pl.BlockSpec(memory_space=pl.ANY)


def _place():
    x, y, c = lax.axis_index("x"), lax.axis_index("y"), lax.axis_index("c")
    other_chips = [(1 - x, y), (x, 1 - y), (1 - x, 1 - y)]
    return x, y, c, 2 * x + y, (x, y, 1 - c), other_chips


def _gather_chips(name, arrs):
    n = len(arrs)

    def body(*refs):
        ins, outs = refs[:n], refs[n:2 * n]
        send_sems, recv_sems = refs[2 * n:]
        x, y, c, me, sibling, chips = _place()

        def remote(a, k, src, dst, to):
            return pltpu.make_async_remote_copy(src_ref=src, dst_ref=dst, send_sem=send_sems.at[a, k],
                                                recv_sem=recv_sems.at[a, k], device_id=to, device_id_type=MESH)

        sends = []
        for a in range(n):
            cp = remote(a, 6, ins[a], outs[a].at[me], sibling)
            cp.start()
            sends.append(cp)
        for j, (px, py) in enumerate(chips):
            for a in range(n):
                cp = remote(a, j, ins[a].at[c], outs[a].at[me, c], (px, py, c))
                cp.start()
                sends.append(cp)
        for j, (px, py) in enumerate(chips):
            for a in range(n):
                landed = outs[a].at[2 * px + py, c]
                remote(a, j, landed, landed, (px, py, c)).wait_recv()
                cp = remote(a, 3 + j, landed, landed, sibling)
                cp.start()
                sends.append(cp)
        for j, (px, py) in enumerate(chips):
            for a in range(n):
                theirs = outs[a].at[2 * px + py, 1 - c]
                remote(a, 3 + j, theirs, theirs, sibling).wait_recv()
        for a in range(n):
            remote(a, 6, ins[a], outs[a].at[me], sibling).wait_recv()
        for cp in sends:
            cp.wait_send()

    return pl.pallas_call(
        body, name=name, in_specs=[ANY] * n, out_specs=[ANY] * n,
        out_shape=[jax.ShapeDtypeStruct((N_CHIPS,) + a.shape, a.dtype) for a in arrs],
        scratch_shapes=[pltpu.SemaphoreType.DMA((n, 7)), pltpu.SemaphoreType.DMA((n, 7))],
    )(*arrs)


def _swap_halves(name, parts):
    n = len(parts)

    def body(*refs):
        ins, outs = refs[:n], refs[n:2 * n]
        send_sems, recv_sems = refs[2 * n:]
        x, y, c, me, sibling, chips = _place()
        cps = []
        for a in range(n):
            cp = pltpu.make_async_remote_copy(src_ref=ins[a].at[:, 1 - c], dst_ref=outs[a], send_sem=send_sems.at[a],
                                              recv_sem=recv_sems.at[a], device_id=sibling, device_id_type=MESH)
            cp.start()
            cps.append(cp)
        for cp in cps:
            cp.wait()

    return pl.pallas_call(
        body, name=name, in_specs=[ANY] * n, out_specs=[ANY] * n,
        out_shape=[jax.ShapeDtypeStruct((N_CHIPS,) + p.shape[2:], p.dtype) for p in parts],
        scratch_shapes=[pltpu.SemaphoreType.DMA((n,)), pltpu.SemaphoreType.DMA((n,))],
    )(*parts)


def _scatter_chips(name, sums):
    n = len(sums)

    def body(*refs):
        ins, outs = refs[:n], refs[n:2 * n]
        send_sems, recv_sems = refs[2 * n:]
        x, y, c, me, sibling, chips = _place()
        sends = []
        for j, (px, py) in enumerate(chips):
            for a in range(n):
                cp = pltpu.make_async_remote_copy(
                    src_ref=ins[a].at[2 * px + py], dst_ref=outs[a].at[me], send_sem=send_sems.at[a, j],
                    recv_sem=recv_sems.at[a, j], device_id=(px, py, c), device_id_type=MESH)
                cp.start()
                sends.append(cp)
        for j, (px, py) in enumerate(chips):
            for a in range(n):
                slot = outs[a].at[2 * px + py]
                pltpu.make_async_remote_copy(
                    src_ref=slot, dst_ref=slot, send_sem=send_sems.at[a, j], recv_sem=recv_sems.at[a, j],
                    device_id=(px, py, c), device_id_type=MESH).wait_recv()
        for cp in sends:
            cp.wait_send()

    return pl.pallas_call(
        body, name=name, in_specs=[ANY] * n, out_specs=[ANY] * n,
        out_shape=[jax.ShapeDtypeStruct(s.shape, s.dtype) for s in sums],
        scratch_shapes=[pltpu.SemaphoreType.DMA((n, 3)), pltpu.SemaphoreType.DMA((n, 3))],
    )(*sums)


def _swap_reduced(name, halves):
    n = len(halves)

    def body(*refs):
        ins, outs = refs[:n], refs[n:2 * n]
        send_sems, recv_sems = refs[2 * n:]
        x, y, c, me, sibling, chips = _place()
        cps = []
        for a in range(n):
            cp = pltpu.make_async_remote_copy(src_ref=ins[a], dst_ref=outs[a], send_sem=send_sems.at[a],
                                              recv_sem=recv_sems.at[a], device_id=sibling, device_id_type=MESH)
            cp.start()
            cps.append(cp)
        for cp in cps:
            cp.wait()

    return pl.pallas_call(
        body, name=name, in_specs=[ANY] * n, out_specs=[ANY] * n,
        out_shape=[jax.ShapeDtypeStruct(h.shape, h.dtype) for h in halves],
        scratch_shapes=[pltpu.SemaphoreType.DMA((n,)), pltpu.SemaphoreType.DMA((n,))],
    )(*halves)


def _row_tile(r, want=256):
    t = (min(r, want) // SUBLANES) * SUBLANES
    while r % t:
        t -= SUBLANES
    return t


def _add_own_half(name, part, got, c, out_dtype):
    _, _, r, cols = part.shape
    tr = _row_tile(r)

    def body(c_ref, p_ref, g_ref, o_ref):
        o_ref[...] = (p_ref[...] + g_ref[...]).astype(o_ref.dtype)

    return pl.pallas_call(
        body, name=name,
        grid_spec=pltpu.PrefetchScalarGridSpec(
            num_scalar_prefetch=1, grid=(N_CHIPS, r // tr),
            in_specs=[pl.BlockSpec((None, None, tr, cols), lambda s, i, c_ref: (s, c_ref[0], i, 0)),
                      pl.BlockSpec((None, tr, cols), lambda s, i, c_ref: (s, i, 0))],
            out_specs=pl.BlockSpec((None, tr, cols), lambda s, i, c_ref: (s, i, 0))),
        out_shape=jax.ShapeDtypeStruct(got.shape, out_dtype),
        compiler_params=_cparams(("parallel", "parallel")),
    )(c.reshape(1).astype(jnp.int32), part, got)


def _sum_chips(name, got, own, me):
    _, r, cols = got.shape
    tr = _row_tile(r)

    def body(me_ref, r0, r1, r2, r3, own_ref, o_ref):
        pick = lambda s, ref: jnp.where(me_ref[0] == s, own_ref[...], ref[...]).astype(F32)
        o_ref[...] = ((pick(0, r0) + pick(1, r1)) + pick(2, r2)) + pick(3, r3)

    def slot(s):
        return pl.BlockSpec((None, tr, cols),
                            lambda i, me_ref: (jnp.where(me_ref[0] == s, (s + 1) % N_CHIPS, s), i, 0))

    return pl.pallas_call(
        body, name=name,
        grid_spec=pltpu.PrefetchScalarGridSpec(
            num_scalar_prefetch=1, grid=(r // tr,),
            in_specs=[slot(s) for s in range(N_CHIPS)]
            + [pl.BlockSpec((None, tr, cols), lambda i, me_ref: (me_ref[0], i, 0))],
            out_specs=pl.BlockSpec((tr, cols), lambda i, me_ref: (i, 0))),
        out_shape=jax.ShapeDtypeStruct((r, cols), F32),
        compiler_params=_cparams(("parallel",)),
    )(me.reshape(1).astype(jnp.int32), got, got, got, got, own)


def _adamw_math(wv, gv, mv, vv):
    mv = ADAM_B1 * mv + (1.0 - ADAM_B1) * gv
    vv = ADAM_B2 * vv + (1.0 - ADAM_B2) * (gv * gv)
    m_hat = mv / (1.0 - ADAM_B1 ** ADAM_STEP)
    v_hat = vv / (1.0 - ADAM_B2 ** ADAM_STEP)
    return -ADAM_LR * (m_hat / (jnp.sqrt(v_hat) + ADAM_EPS) + ADAM_WD * wv), mv, vv


def _adamw(name, w, g, m, v):
    cols = w.shape[1]
    return _ew(name, _adamw_math, [('r', w), ('r', g), ('r', m), ('r', v)], [('r', cols, F32)] * 3,
               tr=_row_tile(w.shape[0], 128))


def _adamw_halves(name, w, mine, theirs, m, v, c):
    r, cols = mine.shape
    tr = _row_tile(r, 128)
    nb = r // tr

    def body(c_ref, w_ref, a_ref, b_ref, m_ref, v_ref, g_out, d_out, m_out, v_out):
        g = jnp.where(pl.program_id(0) == c_ref[0], a_ref[...], b_ref[...])
        g_out[...] = g
        d_out[...], m_out[...], v_out[...] = _adamw_math(w_ref[...], g, m_ref[...], v_ref[...])

    whole = pl.BlockSpec((tr, cols), lambda h, i, c_ref: (h * nb + i, 0))
    half = pl.BlockSpec((tr, cols), lambda h, i, c_ref: (i, 0))
    return pl.pallas_call(
        body, name=name,
        grid_spec=pltpu.PrefetchScalarGridSpec(
            num_scalar_prefetch=1, grid=(2, nb),
            in_specs=[whole, half, half, whole, whole], out_specs=[whole] * 4),
        out_shape=[jax.ShapeDtypeStruct(w.shape, F32)] * 4,
        compiler_params=_cparams(("parallel", "parallel")),
    )(c.reshape(1).astype(jnp.int32), w, mine, theirs, m, v)


SMALL_ROWS_ALIGN = 2 * N_CHIPS * SUBLANES


MEDIUM_NAMES = ['ssm_b_re', 'ssm_b_im', 'ssm_c_re', 'ssm_c_im']
PACKED_NAMES = [n for n in SMALL_NAMES if n not in MEDIUM_NAMES]


def _pack_small(d):
    flat = jnp.concatenate([d[n].reshape(-1).astype(F32) for n in PACKED_NAMES])
    rows = -(-flat.shape[0] // (LANES * SMALL_ROWS_ALIGN)) * SMALL_ROWS_ALIGN
    return jnp.pad(flat, (0, rows * LANES - flat.shape[0])).reshape(rows, LANES)


def _unpack_small(packed, like):
    flat = packed.reshape(-1)
    out, off = {}, 0
    for n in PACKED_NAMES:
        size = like[n].size
        out[n] = flat[off:off + size].reshape(like[n].shape)
        off += size
    return out


def kernel(x, g_mix, w_in, q_gain, k_gain, rpb, ssm_a_re, ssm_a_im, ssm_b_re, ssm_b_im, ssm_c_re, ssm_c_im, ssm_log_step, ssm_d, w_glu, b_glu, g_out_attn, g_out_ssm, w_out, g_ffn, w_ffn_gate, w_ffn_up, w_ffn_down, loss_target, m_g_mix, m_w_in, m_q_gain, m_k_gain, m_rpb, m_ssm_a_re, m_ssm_a_im, m_ssm_b_re, m_ssm_b_im, m_ssm_c_re, m_ssm_c_im, m_ssm_log_step, m_ssm_d, m_w_glu, m_b_glu, m_g_out_attn, m_g_out_ssm, m_w_out, m_g_ffn, m_w_ffn_gate, m_w_ffn_up, m_w_ffn_down, v_g_mix, v_w_in, v_q_gain, v_k_gain, v_rpb, v_ssm_a_re, v_ssm_a_im, v_ssm_b_re, v_ssm_b_im, v_ssm_c_re, v_ssm_c_im, v_ssm_log_step, v_ssm_d, v_w_glu, v_b_glu, v_g_out_attn, v_g_out_ssm, v_w_out, v_g_ffn, v_w_ffn_gate, v_w_ffn_up, v_w_ffn_down):
    given = dict(locals())
    w = {n: given[n][0] for n in WEIGHT_NAMES}
    mom = {n: given["m_" + n][0] for n in WEIGHT_NAMES}
    var = {n: given["v_" + n][0] for n in WEIGHT_NAMES}
    d = x.shape[-1]
    c = lax.axis_index("c")

    halves = [w[n].astype(MXU_DTYPE).reshape((2, w[n].shape[0] // 2, w[n].shape[1])) for n in BIG_NAMES]
    gathered = _gather_chips("gather_weights", halves)
    big = tuple(g4.reshape((N_CHIPS, -1, g4.shape[-1])) for g4 in gathered)
    big = (big[0], big[1].reshape(-1, big[1].shape[-1]), big[2].reshape(-1, big[2].shape[-1]), big[3], big[4], big[5])

    sq, dx, d_big, d_small = _local_step(x[0], loss_target[0], big, {n: w[n] for n in SMALL_NAMES})
    loss = lax.psum(0.5 * sq / d, ("x", "y", "c"))

    nbig = len(BIG_NAMES)
    parts = [g.reshape((N_CHIPS, 2, -1, g.shape[-1])) for g in d_big]
    parts += [d_small[n].reshape(N_CHIPS, 2, -1, LANES) for n in MEDIUM_NAMES]
    parts.append(_pack_small(d_small).reshape(N_CHIPS, 2, -1, LANES))
    got = _swap_halves("reduce_swap_halves", parts)
    payload = [GRAD_PAYLOAD_DTYPE] * nbig + [F32] * (len(parts) - nbig)
    sums = [_add_own_half("reduce_add_%d" % a, p, gt, c, dt) for a, (p, gt, dt) in enumerate(zip(parts, got, payload))]
    got = _scatter_chips("reduce_scatter_chips", sums)
    me = 2 * lax.axis_index("x") + lax.axis_index("y")
    mine = [_sum_chips("reduce_sum_%d" % a, gt, sm_, me) for a, (gt, sm_) in enumerate(zip(got, sums))]
    theirs = _swap_reduced("reduce_swap_reduced", mine)
    in_order = lambda a: jnp.where(c == 0, jnp.stack([mine[a], theirs[a]]), jnp.stack([theirs[a], mine[a]]))
    repl = _gather_chips("gather_small", [in_order(a) for a in range(nbig, len(parts))])
    repl = [r.reshape(-1, LANES) for r in repl]
    like = {n: w[n] for n in SMALL_NAMES}
    grad_small = _unpack_small(repl[-1], like)
    grad_small.update({n: r.reshape(w[n].shape) for n, r in zip(MEDIUM_NAMES, repl)})

    grad_big, delta, new_m, new_v = {}, {}, {}, {}
    for a, n in enumerate(BIG_NAMES):
        grad_big[n], delta[n], new_m[n], new_v[n] = _adamw_halves("adamw_%d" % a, w[n], mine[a], theirs[a],
                                                                  mom[n], var[n], c)
    for n, r in zip(MEDIUM_NAMES, repl):
        res = _adamw("adamw_" + n, w[n].reshape(-1, LANES), r, mom[n].reshape(-1, LANES), var[n].reshape(-1, LANES))
        delta[n], new_m[n], new_v[n] = (t.reshape(w[n].shape) for t in res)
    sd, sm, sv = _adamw("adamw_small", _pack_small(w), repl[-1], _pack_small(mom), _pack_small(var))
    delta.update(_unpack_small(sd, like))
    new_m.update(_unpack_small(sm, like))
    new_v.update(_unpack_small(sv, like))
    grads = {**grad_big, **grad_small}
    lead = lambda t: t[None]
    return (loss, dx[None], *[lead(grads[n]) for n in WEIGHT_NAMES], *[lead(delta[n]) for n in WEIGHT_NAMES],
            *[lead(new_m[n]) for n in WEIGHT_NAMES], *[lead(new_v[n]) for n in WEIGHT_NAMES])
```

```python
import functools
import math

import jax
import jax.numpy as jnp
from jax import lax
from jax.experimental import pallas as pl
from jax.experimental.pallas import tpu as pltpu

F32 = jnp.float32
BF16 = jnp.bfloat16
MXU_DTYPE = BF16
GRAD_PAYLOAD_DTYPE = BF16
S5_DTYPE = BF16
HI = lax.Precision.HIGHEST
VMEM_LIMIT_V7X = 56 * 1024 * 1024
LANES = 128
SUBLANES = 8

GRID_W = 64
WIN_H = 8
WIN_W = 16
HEAD_DIM = 64
SSM_GROUP_CH = 16
SSM_STATE = 64
S5_CHUNK = 16
RMS_EPS = 1e-6
NEG_INF = -1e30
N_CHIPS = 4
MESH = pl.DeviceIdType.MESH

ADAM_LR = 0.001
ADAM_B1 = 0.9
ADAM_B2 = 0.999
ADAM_EPS = 1e-08
ADAM_WD = 0.01
ADAM_STEP = 10

WEIGHT_NAMES = ['g_mix', 'w_in', 'q_gain', 'k_gain', 'rpb', 'ssm_a_re', 'ssm_a_im', 'ssm_b_re', 'ssm_b_im',
                'ssm_c_re', 'ssm_c_im', 'ssm_log_step', 'ssm_d', 'w_glu', 'b_glu', 'g_out_attn', 'g_out_ssm',
                'w_out', 'g_ffn', 'w_ffn_gate', 'w_ffn_up', 'w_ffn_down']
BIG_NAMES = ['w_in', 'w_glu', 'w_out', 'w_ffn_gate', 'w_ffn_up', 'w_ffn_down']
SMALL_NAMES = [n for n in WEIGHT_NAMES if n not in BIG_NAMES]


def _cparams(sem):
    return pltpu.CompilerParams(dimension_semantics=sem, vmem_limit_bytes=VMEM_LIMIT_V7X)


def _tile(n, want):
    if n <= want:
        return n
    t = (want // LANES) * LANES
    while t >= LANES:
        if n % t == 0:
            return t
        t -= LANES
    return n


def _mm(name, a, b, *, contract, a_mode='2', b_mode='2', o_mode='2', out_dtype=F32, add=None, exact=False,
        tm=1024, tn=1024, tk=512):
    dn = {'nn': (((1,), (0,)), ((), ())), 'nt': (((1,), (1,)), ((), ())), 'tn': (((0,), (0,)), ((), ()))}[contract]
    ar, ac = a.shape[-2:]
    br, bc = b.shape[-2:]
    m, kdim = (ar, ac) if contract != 'tn' else (ac, ar)
    n = bc if contract != 'nt' else br
    assert kdim == (br if contract != 'nt' else bc), (name, a.shape, b.shape)
    nbatch = 1
    for arr, mode in ((a, a_mode), (b, b_mode)):
        if mode == 'b':
            nbatch = arr.shape[0]
    nstack = 1
    for arr, mode in ((a, a_mode), (b, b_mode)):
        if mode == 'c':
            nstack = arr.shape[0]
    tm, tn, tk = _tile(m, tm), _tile(n, tn), _tile(kdim, tk)
    nkin = kdim // tk
    nk = nstack * nkin
    grid = (nbatch, m // tm, n // tn, nk)

    def spec(mode, block, rc):
        def imap(s, i, j, kk):
            r, c = rc(i, j, kk % nkin)
            if mode == '2':
                return (r, c)
            return (s if mode == 'b' else kk // nkin, r, c)
        return pl.BlockSpec(block if mode == '2' else (None,) + block, imap)

    a_spec = spec(a_mode, (tm, tk) if contract != 'tn' else (tk, tm),
                  (lambda i, j, k: (i, k)) if contract != 'tn' else (lambda i, j, k: (k, i)))
    b_spec = spec(b_mode, (tk, tn) if contract != 'nt' else (tn, tk),
                  (lambda i, j, k: (k, j)) if contract != 'nt' else (lambda i, j, k: (j, k)))
    o_spec = spec(o_mode, (tm, tn), lambda i, j, k: (i, j))
    out_shape = (m, n) if o_mode == '2' else (nbatch, m, n)
    has_add = add is not None

    def body(*refs):
        if has_add:
            a_ref, b_ref, add_ref, o_ref, acc_ref = refs
        else:
            a_ref, b_ref, o_ref, acc_ref = refs
        k = pl.program_id(3)

        @pl.when(k == 0)
        def _():
            acc_ref[...] = jnp.zeros_like(acc_ref)

        if exact:
            acc_ref[...] += lax.dot_general(a_ref[...].astype(F32), b_ref[...].astype(F32), dn, precision=HI,
                                            preferred_element_type=F32)
        else:
            acc_ref[...] += lax.dot_general(a_ref[...].astype(MXU_DTYPE), b_ref[...].astype(MXU_DTYPE), dn,
                                            preferred_element_type=F32)

        @pl.when(k == nk - 1)
        def _():
            r = acc_ref[...]
            if has_add:
                r = r + add_ref[...].astype(F32)
            o_ref[...] = r.astype(o_ref.dtype)

    in_specs = [a_spec, b_spec] + ([o_spec] if has_add else [])
    args = (a, b) + ((add,) if has_add else ())
    return pl.pallas_call(
        body, name=name, grid=grid, in_specs=in_specs, out_specs=o_spec,
        out_shape=jax.ShapeDtypeStruct(out_shape, out_dtype),
        scratch_shapes=[pltpu.VMEM((tm, tn), F32)],
        compiler_params=_cparams(("parallel", "parallel", "parallel", "arbitrary")),
    )(*args)


def _ew(name, fn, ins, outs, tr=256):
    rows = next(x[1].shape[0] for x in ins if x[0] == 'r')
    tr = min(tr, rows)
    assert rows % tr == 0 and tr % SUBLANES == 0, (name, rows, tr)
    in_specs, args = [], []
    for x in ins:
        if x[0] == 'r' and len(x) == 2:
            in_specs.append(pl.BlockSpec((tr, x[1].shape[1]), lambda i: (i, 0)))
        elif x[0] == 'r':
            in_specs.append(pl.BlockSpec((tr, x[3]), functools.partial(lambda cb, i: (i, cb), x[2])))
        else:
            in_specs.append(pl.BlockSpec(x[1].shape, lambda i: (0, 0)))
        args.append(x[1])
    out_specs, out_shapes = [], []
    for o in outs:
        if o[0] == 'r':
            out_specs.append(pl.BlockSpec((tr, o[1]), lambda i: (i, 0)))
            out_shapes.append(jax.ShapeDtypeStruct((rows, o[1]), o[2]))
        else:
            out_specs.append(pl.BlockSpec((SUBLANES, o[1]), lambda i: (0, 0)))
            out_shapes.append(jax.ShapeDtypeStruct((SUBLANES, o[1]), F32))
    nin = len(ins)
    has_acc = any(o[0] == 'a' for o in outs)

    def body(*refs):
        vals = fn(*[r[...] for r in refs[:nin]])
        if not isinstance(vals, (tuple, list)):
            vals = (vals,)
        i = pl.program_id(0)
        for o, ref, v in zip(outs, refs[nin:], vals):
            if o[0] == 'r':
                ref[...] = v.astype(ref.dtype)
            else:
                part = v.astype(F32).reshape(tr // SUBLANES, SUBLANES, o[1]).sum(axis=0)

                @pl.when(i == 0)
                def _(ref=ref, part=part):
                    ref[...] = part

                @pl.when(i > 0)
                def _(ref=ref, part=part):
                    ref[...] += part

    res = pl.pallas_call(
        body, name=name, grid=(rows // tr,), in_specs=in_specs, out_specs=out_specs, out_shape=out_shapes,
        compiler_params=_cparams(("arbitrary",) if has_acc else ("parallel",)),
    )(*args)
    return res


def _rms(x, g):
    r = lax.rsqrt(jnp.mean(x * x, axis=-1, keepdims=True) + RMS_EPS)
    xr = x * r
    return xr * g, xr


def _rms_bwd(x, g, dy):
    r = lax.rsqrt(jnp.mean(x * x, axis=-1, keepdims=True) + RMS_EPS)
    xr = x * r
    gdy = g * dy
    dx = r * (gdy - xr * jnp.mean(xr * gdy, axis=-1, keepdims=True))
    return dx, dy * xr


def _sigmoid(x):
    return 1.0 / (1.0 + jnp.exp(-x))


_GELU_C = math.sqrt(2.0 / math.pi)


def _gelu(x):
    return 0.5 * x * (1.0 + jnp.tanh(_GELU_C * (x + 0.044715 * x * x * x)))


def _gelu_grad(x):
    t = jnp.tanh(_GELU_C * (x + 0.044715 * x * x * x))
    return 0.5 * (1.0 + t) + 0.5 * x * (1.0 - t * t) * _GELU_C * (1.0 + 3 * 0.044715 * x * x)


ATTN_ROWS_PER_STEP = 8
NT_DIMS = (((1,), (1,)), ((), ()))
NN_DIMS = (((1,), (0,)), ((), ()))
TN_DIMS = (((0,), (0,)), ((), ()))


def _attn_geometry(r, rows):
    row_start = jnp.clip(r - WIN_H // 2, 0, rows - WIN_H)
    key0 = pl.multiple_of(row_start * GRID_W, GRID_W)
    bias0 = pl.multiple_of((row_start - r + (WIN_H - 1)) * GRID_W, GRID_W)
    return key0, bias0


def _window_onehot():
    c = jnp.arange(GRID_W)
    col_start = jnp.clip(c - WIN_W // 2, 0, GRID_W - WIN_W)
    col_in = (c[None, :] >= col_start[:, None]) & (c[None, :] < col_start[:, None] + WIN_W)
    dc = jnp.clip(c[None, :] - c[:, None], -(WIN_W - 1), WIN_W - 1) + (WIN_W - 1)
    onehot = ((dc[:, :, None] == jnp.arange(2 * WIN_W - 1)[None, None, :]) & col_in[:, :, None]).astype(F32)
    return onehot, col_in


def _bias_table(rpb):
    onehot, col_in = _window_onehot()
    nh = rpb.shape[0]
    tab = jnp.einsum('perd,qkd->prkeq', rpb.reshape(nh // 2, 2, 2 * WIN_H - 1, 2 * WIN_W - 1), onehot, precision=HI)
    tab = tab + jnp.where(col_in, 0.0, NEG_INF).T[None, None, :, None, :]
    return tab.reshape(nh // 2, (2 * WIN_H - 1) * GRID_W, 2 * GRID_W)


def _bias_table_grad(dtab):
    onehot, _ = _window_onehot()
    npair = dtab.shape[0]
    d = dtab.reshape(npair, 2 * WIN_H - 1, GRID_W, 2, GRID_W)
    return jnp.einsum('prkeq,qkd->perd', d, onehot, precision=HI).reshape(2 * npair, 2 * WIN_H - 1, 2 * WIN_W - 1)


def _lane_lo(shape):
    return lax.broadcasted_iota(jnp.int32, shape, 1) < HEAD_DIM


def _half_sums(v):
    lo = _lane_lo(v.shape)
    s_lo = jnp.sum(jnp.where(lo, v, 0.0), axis=1, keepdims=True)
    s_hi = jnp.sum(jnp.where(lo, 0.0, v), axis=1, keepdims=True)
    return jnp.where(lo, s_lo, s_hi)


def _rms_pair(x, g):
    r = lax.rsqrt(_half_sums(x * x) * (1.0 / HEAD_DIM) + RMS_EPS)
    return x * r * g


def _rms_pair_bwd(x, g, dy):
    r = lax.rsqrt(_half_sums(x * x) * (1.0 / HEAD_DIM) + RMS_EPS)
    xr = x * r
    gdy = g * dy
    dx = r * (gdy - xr * (_half_sums(xr * gdy) * (1.0 / HEAD_DIM)))
    return dx, dy * xr


def _blockdiag(a):
    a2 = jnp.concatenate([a, a], axis=0)
    row_hi = lax.broadcasted_iota(jnp.int32, a2.shape, 0) >= GRID_W
    lane_hi = lax.broadcasted_iota(jnp.int32, a2.shape, 1) >= HEAD_DIM
    return jnp.where(row_hi == lane_hi, a2, 0.0).astype(MXU_DTYPE)


def _diag_blocks(m):
    return jnp.where(_lane_lo((GRID_W, 2 * HEAD_DIM)), m[:GRID_W], m[GRID_W:])


def _attn_scores(qb, kb, bias):
    st = lax.dot_general(kb, qb, NT_DIMS, preferred_element_type=F32)
    st = st * (1.0 / math.sqrt(HEAD_DIM)) + bias
    mx = jnp.max(st, axis=0, keepdims=True)
    p = jnp.exp(st - mx)
    return p * (1.0 / jnp.sum(p, axis=0, keepdims=True))


def _attn_fwd(z4, qg2, kg2, bias_t):
    _, t, aw = z4.shape
    rows = t // GRID_W
    npair = aw // (2 * HEAD_DIM)
    nkeys = WIN_H * GRID_W
    nb = bias_t.shape[1]
    rps = min(ATTN_ROWS_PER_STEP, rows)
    blk = rps * GRID_W

    def body(q_ref, k_ref, v_ref, qg_ref, kg_ref, b_ref, o_ref, kn_ref, vb_ref):
        rb = pl.program_id(1)

        @pl.when(rb == 0)
        def _():
            kn_ref[...] = _rms_pair(k_ref[...], kg_ref[...]).astype(MXU_DTYPE)
            vb_ref[...] = v_ref[...].astype(MXU_DTYPE)

        def row(i, carry):
            key0, bias0 = _attn_geometry(rb * rps + i, rows)
            at = pl.ds(pl.multiple_of(i * GRID_W, GRID_W), GRID_W)
            qb = _blockdiag(_rms_pair(q_ref[at, :], qg_ref[...]))
            pt = _attn_scores(qb, kn_ref[pl.ds(key0, nkeys), :], b_ref[pl.ds(bias0, nkeys), :])
            both = lax.dot_general(pt.astype(MXU_DTYPE), vb_ref[pl.ds(key0, nkeys), :], TN_DIMS,
                                   preferred_element_type=F32)
            o_ref[at, :] = _diag_blocks(both)
            return carry

        lax.fori_loop(0, rps, row, 0, unroll=2)

    pair_cols = lambda lead: pl.BlockSpec((None, t, 2 * HEAD_DIM), lambda p, r: (lead, 0, p))
    return pl.pallas_call(
        body, name="attn_fwd", grid=(npair, rows // rps),
        in_specs=[pl.BlockSpec((None, blk, 2 * HEAD_DIM), lambda p, r: (0, r, p)), pair_cols(1), pair_cols(2),
                  pl.BlockSpec((1, 2 * HEAD_DIM), lambda p, r: (0, 0)),
                  pl.BlockSpec((1, 2 * HEAD_DIM), lambda p, r: (0, 0)),
                  pl.BlockSpec((None, nb, 2 * GRID_W), lambda p, r: (p, 0, 0))],
        out_specs=pl.BlockSpec((blk, 2 * HEAD_DIM), lambda p, r: (r, p)),
        out_shape=jax.ShapeDtypeStruct((t, aw), F32),
        scratch_shapes=[pltpu.VMEM((t, 2 * HEAD_DIM), MXU_DTYPE), pltpu.VMEM((t, 2 * HEAD_DIM), MXU_DTYPE)],
        compiler_params=_cparams(("parallel", "arbitrary")),
    )(z4, z4, z4, qg2, kg2, bias_t)


def _attn_bwd(z4, qg2, kg2, bias_t, dya):
    _, t, aw = z4.shape
    rows = t // GRID_W
    npair = aw // (2 * HEAD_DIM)
    nkeys = WIN_H * GRID_W
    nb = bias_t.shape[1]
    rps = min(ATTN_ROWS_PER_STEP, rows)
    blk = rps * GRID_W
    scale = 1.0 / math.sqrt(HEAD_DIM)

    def body(q_ref, k_ref, v_ref, qg_ref, kg_ref, b_ref, do_ref, dz_ref, db_ref, dqg_ref, dkg_ref, kn_ref, vb_ref):
        rb = pl.program_id(1)

        @pl.when(rb == 0)
        def _():
            kn_ref[...] = _rms_pair(k_ref[...], kg_ref[...]).astype(MXU_DTYPE)
            vb_ref[...] = v_ref[...].astype(MXU_DTYPE)
            dz_ref[1] = jnp.zeros((t, 2 * HEAD_DIM), F32)
            dz_ref[2] = jnp.zeros((t, 2 * HEAD_DIM), F32)
            db_ref[...] = jnp.zeros_like(db_ref)
            dqg_ref[...] = jnp.zeros_like(dqg_ref)

        def row(i, dqg_sum):
            r = rb * rps + i
            key0, bias0 = _attn_geometry(r, rows)
            keys = pl.ds(key0, nkeys)
            at = pl.ds(pl.multiple_of(i * GRID_W, GRID_W), GRID_W)
            q = q_ref[at, :]
            qb = _blockdiag(_rms_pair(q, qg_ref[...]))
            dob = _blockdiag(do_ref[at, :])
            kb = kn_ref[keys, :]
            pt = _attn_scores(qb, kb, b_ref[pl.ds(bias0, nkeys), :])
            dz_ref[2, keys, :] += lax.dot_general(pt.astype(MXU_DTYPE), dob, NN_DIMS, preferred_element_type=F32)
            dpt = lax.dot_general(vb_ref[keys, :], dob, NT_DIMS, preferred_element_type=F32)
            dst = pt * (dpt - jnp.sum(pt * dpt, axis=0, keepdims=True))
            db_ref[pl.ds(bias0, nkeys), :] += dst
            dsb = dst.astype(MXU_DTYPE)
            dz_ref[1, keys, :] += scale * lax.dot_general(dsb, qb, NN_DIMS, preferred_element_type=F32)
            dqn = scale * _diag_blocks(lax.dot_general(dsb, kb, TN_DIMS, preferred_element_type=F32))
            dq, dqg = _rms_pair_bwd(q, qg_ref[...], dqn)
            dz_ref[0, pl.ds(pl.multiple_of(r * GRID_W, GRID_W), GRID_W), :] = dq
            return dqg_sum + jnp.sum(dqg, axis=0, keepdims=True)

        dqg_ref[...] += lax.fori_loop(0, rps, row, jnp.zeros((1, 2 * HEAD_DIM), F32), unroll=2)

        @pl.when(rb == rows // rps - 1)
        def _():
            dk, dkg = _rms_pair_bwd(k_ref[...], kg_ref[...], dz_ref[1])
            dz_ref[1] = dk
            dkg_ref[...] = jnp.sum(dkg, axis=0, keepdims=True)

    pair_cols = lambda lead: pl.BlockSpec((None, t, 2 * HEAD_DIM), lambda p, r: (lead, 0, p))
    pair_vec = pl.BlockSpec((None, 1, 2 * HEAD_DIM), lambda p, r: (p, 0, 0))
    return pl.pallas_call(
        body, name="attn_bwd", grid=(npair, rows // rps),
        in_specs=[pl.BlockSpec((None, blk, 2 * HEAD_DIM), lambda p, r: (0, r, p)), pair_cols(1), pair_cols(2),
                  pl.BlockSpec((1, 2 * HEAD_DIM), lambda p, r: (0, 0)),
                  pl.BlockSpec((1, 2 * HEAD_DIM), lambda p, r: (0, 0)),
                  pl.BlockSpec((None, nb, 2 * GRID_W), lambda p, r: (p, 0, 0)),
                  pl.BlockSpec((blk, 2 * HEAD_DIM), lambda p, r: (r, p))],
        out_specs=[pl.BlockSpec((3, t, 2 * HEAD_DIM), lambda p, r: (0, 0, p)),
                   pl.BlockSpec((None, nb, 2 * GRID_W), lambda p, r: (p, 0, 0)),
                   pair_vec, pair_vec],
        out_shape=[jax.ShapeDtypeStruct((4, t, aw), F32), jax.ShapeDtypeStruct(bias_t.shape, F32),
                   jax.ShapeDtypeStruct((npair, 1, 2 * HEAD_DIM), F32),
                   jax.ShapeDtypeStruct((npair, 1, 2 * HEAD_DIM), F32)],
        scratch_shapes=[pltpu.VMEM((t, 2 * HEAD_DIM), MXU_DTYPE), pltpu.VMEM((t, 2 * HEAD_DIM), MXU_DTYPE)],
        compiler_params=_cparams(("parallel", "arbitrary")),
    )(z4, z4, z4, qg2, kg2, bias_t, dya)


def _attn_fwd_old(z4, qg2, kg2, bias_t):
    _, t, aw = z4.shape
    rows = t // GRID_W
    npair = aw // (2 * HEAD_DIM)
    nkeys = WIN_H * GRID_W
    nb = bias_t.shape[1]

    def body(q_ref, k_ref, v_ref, qg_ref, kg_ref, b_ref, o_ref, kn_ref):
        r = pl.program_id(1)

        @pl.when(r == 0)
        def _():
            for e in range(2):
                sl = slice(e * HEAD_DIM, (e + 1) * HEAD_DIM)
                kn_ref[e] = _rms(k_ref[:, sl], kg_ref[:, sl])[0]

        key0, bias0 = _attn_geometry(r, rows)
        outs = []
        for e in range(2):
            sl = slice(e * HEAD_DIM, (e + 1) * HEAD_DIM)
            qn = _rms(q_ref[:, sl], qg_ref[:, sl])[0]
            kb = kn_ref[e, pl.ds(key0, nkeys), :]
            vb = v_ref[pl.ds(key0, nkeys), sl]
            pt = _attn_scores(qn, kb, b_ref[e, pl.ds(bias0, nkeys), :])
            outs.append(lax.dot_general(pt.astype(MXU_DTYPE), vb.astype(MXU_DTYPE), (((0,), (0,)), ((), ())),
                                        preferred_element_type=F32))
        o_ref[...] = jnp.concatenate(outs, axis=1)

    return pl.pallas_call(
        body, name="attn_fwd", grid=(npair, rows),
        in_specs=[pl.BlockSpec((None, GRID_W, 2 * HEAD_DIM), lambda p, r: (0, r, p)),
                  pl.BlockSpec((None, t, 2 * HEAD_DIM), lambda p, r: (1, 0, p)),
                  pl.BlockSpec((None, t, 2 * HEAD_DIM), lambda p, r: (2, 0, p)),
                  pl.BlockSpec((1, 2 * HEAD_DIM), lambda p, r: (0, 0)),
                  pl.BlockSpec((1, 2 * HEAD_DIM), lambda p, r: (0, 0)),
                  pl.BlockSpec((2, nb, GRID_W), lambda p, r: (p, 0, 0))],
        out_specs=pl.BlockSpec((GRID_W, 2 * HEAD_DIM), lambda p, r: (r, p)),
        out_shape=jax.ShapeDtypeStruct((t, aw), F32),
        scratch_shapes=[pltpu.VMEM((2, t, HEAD_DIM), F32)],
        compiler_params=_cparams(("parallel", "arbitrary")),
    )(z4, z4, z4, qg2, kg2, bias_t)


def _attn_bwd_old(z4, qg2, kg2, bias_t, dya):
    _, t, aw = z4.shape
    rows = t // GRID_W
    npair = aw // (2 * HEAD_DIM)
    nkeys = WIN_H * GRID_W
    nb = bias_t.shape[1]
    scale = 1.0 / math.sqrt(HEAD_DIM)

    def body(q_ref, k_ref, v_ref, qg_ref, kg_ref, b_ref, do_ref,
             dq_ref, dk_ref, dv_ref, db_ref, dqg_ref, dkg_ref, kn_ref, dkn_ref, dva_ref):
        r = pl.program_id(1)

        @pl.when(r == 0)
        def _():
            for e in range(2):
                sl = slice(e * HEAD_DIM, (e + 1) * HEAD_DIM)
                kn_ref[e] = _rms(k_ref[:, sl], kg_ref[:, sl])[0]
            dkn_ref[...] = jnp.zeros_like(dkn_ref)
            dva_ref[...] = jnp.zeros_like(dva_ref)
            db_ref[...] = jnp.zeros_like(db_ref)
            dqg_ref[...] = jnp.zeros_like(dqg_ref)

        key0, bias0 = _attn_geometry(r, rows)
        dqs, dqgs = [], []
        for e in range(2):
            sl = slice(e * HEAD_DIM, (e + 1) * HEAD_DIM)
            q = q_ref[:, sl]
            qn = _rms(q, qg_ref[:, sl])[0]
            kb = kn_ref[e, pl.ds(key0, nkeys), :]
            vb = v_ref[pl.ds(key0, nkeys), sl]
            do = do_ref[:, sl]
            pt = _attn_scores(qn, kb, b_ref[e, pl.ds(bias0, nkeys), :])
            dva_ref[e, pl.ds(key0, nkeys), :] += lax.dot_general(
                pt.astype(MXU_DTYPE), do.astype(MXU_DTYPE), (((1,), (0,)), ((), ())), preferred_element_type=F32)
            dpt = lax.dot_general(vb.astype(MXU_DTYPE), do.astype(MXU_DTYPE), (((1,), (1,)), ((), ())),
                                  preferred_element_type=F32)
            dst = pt * (dpt - jnp.sum(pt * dpt, axis=0, keepdims=True))
            db_ref[e, pl.ds(bias0, nkeys), :] += dst
            dsb = dst.astype(MXU_DTYPE)
            dqn = scale * lax.dot_general(dsb, kb.astype(MXU_DTYPE), (((0,), (0,)), ((), ())),
                                          preferred_element_type=F32)
            dkn_ref[e, pl.ds(key0, nkeys), :] += scale * lax.dot_general(
                dsb, qn.astype(MXU_DTYPE), (((1,), (0,)), ((), ())), preferred_element_type=F32)
            dq, dqg = _rms_bwd(q, qg_ref[:, sl], dqn)
            dqs.append(dq)
            dqgs.append(jnp.sum(dqg, axis=0, keepdims=True))
        dq_ref[...] = jnp.concatenate(dqs, axis=1)
        dqg_ref[...] += jnp.concatenate(dqgs, axis=1)

        @pl.when(r == rows - 1)
        def _():
            dks, dkgs = [], []
            for e in range(2):
                sl = slice(e * HEAD_DIM, (e + 1) * HEAD_DIM)
                dk, dkg = _rms_bwd(k_ref[:, sl], kg_ref[:, sl], dkn_ref[e])
                dks.append(dk)
                dkgs.append(jnp.sum(dkg, axis=0, keepdims=True))
            dk_ref[...] = jnp.concatenate(dks, axis=1)
            dv_ref[...] = jnp.concatenate([dva_ref[0], dva_ref[1]], axis=1)
            dkg_ref[...] = jnp.concatenate(dkgs, axis=1)

    pair_vec = pl.BlockSpec((None, 1, 2 * HEAD_DIM), lambda p, r: (p, 0, 0))
    return pl.pallas_call(
        body, name="attn_bwd", grid=(npair, rows),
        in_specs=[pl.BlockSpec((None, GRID_W, 2 * HEAD_DIM), lambda p, r: (0, r, p)),
                  pl.BlockSpec((None, t, 2 * HEAD_DIM), lambda p, r: (1, 0, p)),
                  pl.BlockSpec((None, t, 2 * HEAD_DIM), lambda p, r: (2, 0, p)),
                  pl.BlockSpec((1, 2 * HEAD_DIM), lambda p, r: (0, 0)),
                  pl.BlockSpec((1, 2 * HEAD_DIM), lambda p, r: (0, 0)),
                  pl.BlockSpec((2, nb, GRID_W), lambda p, r: (p, 0, 0)),
                  pl.BlockSpec((GRID_W, 2 * HEAD_DIM), lambda p, r: (r, p))],
        out_specs=[pl.BlockSpec((GRID_W, 2 * HEAD_DIM), lambda p, r: (r, p)),
                   pl.BlockSpec((t, 2 * HEAD_DIM), lambda p, r: (0, p)),
                   pl.BlockSpec((t, 2 * HEAD_DIM), lambda p, r: (0, p)),
                   pl.BlockSpec((2, nb, GRID_W), lambda p, r: (p, 0, 0)),
                   pair_vec, pair_vec],
        out_shape=[jax.ShapeDtypeStruct((t, aw), F32), jax.ShapeDtypeStruct((t, aw), F32),
                   jax.ShapeDtypeStruct((t, aw), F32), jax.ShapeDtypeStruct(bias_t.shape, F32),
                   jax.ShapeDtypeStruct((npair, 1, 2 * HEAD_DIM), F32),
                   jax.ShapeDtypeStruct((npair, 1, 2 * HEAD_DIM), F32)],
        scratch_shapes=[pltpu.VMEM((2, t, HEAD_DIM), F32), pltpu.VMEM((2, t, HEAD_DIM), F32),
                        pltpu.VMEM((2, t, HEAD_DIM), F32)],
        compiler_params=_cparams(("parallel", "arbitrary")),
    )(z4, z4, z4, qg2, kg2, bias_t, dya)


def _s5_mats(a_re, a_im, b_re, b_im, c_re, c_im, log_step, d_skip):
    nd, g, p = a_re.shape
    c = b_re.shape[-1]
    L = S5_CHUNK
    lr = jnp.minimum(a_re, -1e-4).transpose(1, 0, 2)
    li = a_im.transpose(1, 0, 2)
    dt = jnp.exp(log_step).T[..., None]
    n = jnp.arange(L + 1, dtype=F32)[None, :, None, None]
    mag = jnp.exp(n * (lr * dt)[:, None])
    ang = n * (li * dt)[:, None]
    pw_r, pw_i = mag * jnp.cos(ang), mag * jnp.sin(ang)
    den = lr * lr + li * li
    nr, ni = pw_r[:, 1] - 1.0, pw_i[:, 1]
    cr, ci = (nr * lr + ni * li) / den, (ni * lr - nr * li) / den
    bt_r, bt_i = b_re.transpose(1, 3, 0, 2), b_im.transpose(1, 3, 0, 2)
    bb_r = cr[:, None] * bt_r - ci[:, None] * bt_i
    bb_i = cr[:, None] * bt_i + ci[:, None] * bt_r
    ct_r, ct_i = c_re.transpose(1, 2, 0, 3), c_im.transpose(1, 2, 0, 3)

    def cols(x_re, x_im):
        return jnp.concatenate([x_re[..., 0, :], x_im[..., 0, :], x_re[..., 1, :], x_im[..., 1, :]], axis=-1)

    e_r = jnp.stack([pw_r[:, :L, 0][:, ::-1], pw_r[:, :L, 1]], axis=2)
    e_i = jnp.stack([pw_i[:, :L, 0][:, ::-1], pw_i[:, :L, 1]], axis=2)
    ws = (cols(e_r, e_r)[:, :, None] * cols(bb_r, bb_i)[:, None]
          + cols(e_i, e_i)[:, :, None] * cols(-bb_i, bb_r)[:, None]).reshape(g, L * c, 4 * p)
    f_r = jnp.stack([pw_r[:, 1:, 0], pw_r[:, 1:, 1][:, ::-1]], axis=2)
    f_i = jnp.stack([pw_i[:, 1:, 0], pw_i[:, 1:, 1][:, ::-1]], axis=2)
    wot = (cols(f_r, f_i)[:, :, None] * cols(ct_r, -ct_r)[:, None]
           + cols(f_i, f_r)[:, :, None] * cols(-ct_i, -ct_i)[:, None]).reshape(g, L * c, 4 * p)
    qr, qi = pw_r[:, :L, None], pw_i[:, :L, None]
    kp_r, kp_i = qr * bb_r[:, None] - qi * bb_i[:, None], qr * bb_i[:, None] + qi * bb_r[:, None]
    kern = [jnp.einsum('gnip,gop->gino', kp_r[:, :, :, d], ct_r[:, :, d], precision=HI)
            - jnp.einsum('gnip,gop->gino', kp_i[:, :, :, d], ct_i[:, :, d], precision=HI) for d in range(2)]
    skip = d_skip.reshape(g, c, 1, 1) * jnp.eye(c, dtype=F32)[None, :, None, :]
    by_offset = jnp.concatenate([kern[1][:, :, :0:-1], kern[0][:, :, :1] + kern[1][:, :, :1] + skip,
                                 kern[0][:, :, 1:]], axis=2).reshape(g, c, (2 * L - 1) * c)
    mt = jnp.stack([by_offset[:, :, (L - 1 - j) * c:(2 * L - 1 - j) * c] for j in range(L)], axis=1)
    mt = mt.reshape(g, L * c, L * c)
    lr16, li16 = pw_r[:, L], pw_i[:, L]
    fa = jnp.concatenate([lr16[:, 0], lr16[:, 0], lr16[:, 1], lr16[:, 1]], axis=-1)
    fb = jnp.concatenate([-li16[:, 0], li16[:, 0], -li16[:, 1], li16[:, 1]], axis=-1)
    return mt, ws, wot, fa, fb


def _gmm(name, a, b, contract, a_stacked=False, b_stacked=False, o_stacked=False, add=None, out_dtype=F32):
    w = S5_CHUNK * SSM_GROUP_CH
    g = (a.shape[0] if a_stacked else a.shape[1] // w)
    dn = {'nn': (((1,), (0,)), ((), ())), 'nt': (((1,), (1,)), ((), ())), 'tn': (((0,), (0,)), ((), ()))}[contract]

    def spec(arr, stacked):
        if stacked:
            return pl.BlockSpec((None,) + arr.shape[1:], lambda i: (i, 0, 0))
        return pl.BlockSpec((arr.shape[0], w), lambda i: (0, i))

    m = (a.shape[1] if a_stacked else a.shape[0]) if contract != 'tn' else w
    n = w
    if o_stacked:
        o_spec = pl.BlockSpec((None, m, n), lambda i: (i, 0, 0))
        o_shape = (g, m, n)
    else:
        o_spec = pl.BlockSpec((m, n), lambda i: (0, i))
        o_shape = (m, g * n)
    has_add = add is not None

    def body(*refs):
        if has_add:
            a_ref, b_ref, add_ref, o_ref = refs
        else:
            a_ref, b_ref, o_ref = refs
        r = lax.dot_general(a_ref[...].astype(S5_DTYPE), b_ref[...].astype(S5_DTYPE), dn,
                            precision=HI if S5_DTYPE == F32 else None, preferred_element_type=F32)
        if has_add:
            r = r + add_ref[...]
        o_ref[...] = r.astype(o_ref.dtype)

    in_specs = [spec(a, a_stacked), spec(b, b_stacked)] + ([o_spec] if has_add else [])
    return pl.pallas_call(
        body, name=name, grid=(g,), in_specs=in_specs, out_specs=o_spec,
        out_shape=jax.ShapeDtypeStruct(o_shape, out_dtype), compiler_params=_cparams(("parallel",)),
    )(*((a, b) + ((add,) if has_add else ())))


def _s5_scan(name, s, fa, fb, rev0, xin=None):
    nk, g, w = s.shape
    hw = w // 2
    gb = min(g, 16)
    with_acc = xin is not None

    def body(*refs):
        if with_acc:
            s_ref, a_ref, b_ref, x_ref, o_ref, pa_ref, pb_ref = refs
        else:
            s_ref, a_ref, b_ref, o_ref = refs
        fa0, fb0, fa1, fb1 = a_ref[:, :hw], b_ref[:, :hw], a_ref[:, hw:], b_ref[:, hw:]

        def step(i, carry):
            x0, x1, pa0, pb0, pa1, pb1 = carry
            k0 = (nk - 1 - i) if rev0 else i
            k1 = i if rev0 else (nk - 1 - i)
            o_ref[k0, :, :hw] = x0
            o_ref[k1, :, hw:] = x1
            if with_acc:
                xi0, xi1 = x_ref[k0, :, :hw], x_ref[k1, :, hw:]
                pa0 = pa0 + x0 * xi0
                pb0 = pb0 + x0 * pltpu.roll(xi0, hw // 2, 1)
                pa1 = pa1 + x1 * xi1
                pb1 = pb1 + x1 * pltpu.roll(xi1, hw // 2, 1)
            x0 = fa0 * x0 + fb0 * pltpu.roll(x0, hw // 2, 1) + s_ref[k0, :, :hw]
            x1 = fa1 * x1 + fb1 * pltpu.roll(x1, hw // 2, 1) + s_ref[k1, :, hw:]
            return x0, x1, pa0, pb0, pa1, pb1

        z = jnp.zeros((gb, hw), F32)
        res = lax.fori_loop(0, nk, step, (z, z, z, z, z, z))
        if with_acc:
            pa_ref[:, :hw] = res[2]
            pb_ref[:, :hw] = res[3]
            pa_ref[:, hw:] = res[4]
            pb_ref[:, hw:] = res[5]

    seq = pl.BlockSpec((nk, gb, w), lambda i: (0, i, 0))
    vec = pl.BlockSpec((gb, w), lambda i: (i, 0))
    in_specs = [seq, vec, vec] + ([seq] if with_acc else [])
    out_specs = [seq] + ([vec, vec] if with_acc else [])
    out_shape = [jax.ShapeDtypeStruct((nk, g, w), F32)] + (
        [jax.ShapeDtypeStruct((g, w), F32)] * 2 if with_acc else [])
    return pl.pallas_call(
        body, name=name, grid=(g // gb,), in_specs=in_specs, out_specs=out_specs, out_shape=out_shape,
        compiler_params=_cparams(("parallel",)),
    )(*((s, fa, fb) + ((xin,) if with_acc else ())))


def _to_groups(u):
    t, sw = u.shape
    g = sw // SSM_GROUP_CH
    return u.reshape(t // S5_CHUNK, S5_CHUNK, g, SSM_GROUP_CH).transpose(0, 2, 1, 3).reshape(t // S5_CHUNK, -1)


def _from_groups(y, sw):
    nk = y.shape[0]
    g = sw // SSM_GROUP_CH
    return y.reshape(nk, g, S5_CHUNK, SSM_GROUP_CH).transpose(0, 2, 1, 3).reshape(nk * S5_CHUNK, sw)


def _s5_fwd(u2, mats):
    mt, ws, wot, fa, fb = mats
    nk = u2.shape[0]
    g = mt.shape[0]
    y_intra = _gmm("s5_intra", u2, mt, 'nn', b_stacked=True)
    s = _gmm("s5_chunk_state", u2, ws, 'nn', b_stacked=True)
    (xin,) = _s5_scan("s5_scan", s.reshape(nk, g, -1), fa, fb, False)
    xin = xin.reshape(nk, -1)
    return _gmm("s5_inter", xin, wot, 'nt', b_stacked=True, add=y_intra), xin


def _s5_bwd(u2, xin, mats, dy2):
    mt, ws, wot, fa, fb = mats
    nk = u2.shape[0]
    g = mt.shape[0]
    dxin = _gmm("s5_dxin", dy2, wot, 'nn', b_stacked=True)
    ds, pa, pb = _s5_scan("s5_scan_adj", dxin.reshape(nk, g, -1), fa, -fb, True, xin=xin.reshape(nk, g, -1))
    ds = ds.reshape(nk, -1)
    du_a = _gmm("s5_du_intra", dy2, mt, 'nt', b_stacked=True)
    du2 = _gmm("s5_du_state", ds, ws, 'nt', b_stacked=True, add=du_a, out_dtype=S5_DTYPE)
    dmt = _gmm("s5_dmt", u2, dy2, 'tn', o_stacked=True)
    dws = _gmm("s5_dws", u2, ds, 'tn', o_stacked=True)
    dwot = _gmm("s5_dwot", dy2, xin, 'tn', o_stacked=True)
    return du2, (dmt, dws, dwot, pa, pb)


def _local_step(x, target, big, small):
    t, d = x.shape
    w_in4, w_glu, w_out, w_gate4, w_up4, w_down4 = big
    aw = w_in4.shape[2]
    sw = w_glu.shape[0]
    nh = aw // HEAD_DIM
    ffs = w_gate4.shape[2]
    row = lambda v: v.reshape(1, -1)
    g_mix, g_ffn = row(small['g_mix']), row(small['g_ffn'])
    g_oa, g_os, b_glu = row(small['g_out_attn']), row(small['g_out_ssm']), row(small['b_glu'])
    qg2 = jnp.tile(row(small['q_gain']), (1, 2))
    kg2 = jnp.tile(row(small['k_gain']), (1, 2))

    (h,) = _ew("rms_mix", lambda xv, g: _rms(xv, g)[0], [('r', x), ('c', g_mix)], [('r', d, MXU_DTYPE)])
    z4 = _mm("in_proj", h, w_in4, contract='nn', b_mode='b', o_mode='b')
    bias_t = _bias_table(small['rpb'])
    ya = _attn_fwd(z4, qg2, kg2, bias_t)
    s5_params = tuple(small[n] for n in ('ssm_a_re', 'ssm_a_im', 'ssm_b_re', 'ssm_b_im', 'ssm_c_re', 'ssm_c_im',
                                         'ssm_log_step', 'ssm_d'))
    mats, mats_vjp = jax.vjp(_s5_mats, *s5_params)
    mats = tuple(m.astype(S5_DTYPE) for m in mats[:3]) + mats[3:]
    u2 = _to_groups(z4[3].astype(S5_DTYPE))
    ypre2, xin = _s5_fwd(u2, mats)
    ypre = _from_groups(ypre2, sw)
    (yb,) = _ew("gelu", _gelu, [('r', ypre)], [('r', sw, MXU_DTYPE)])
    a_glu = _mm("glu_proj", yb, w_glu, contract='nn')

    def mix_out(yav, ypv, av, bg, goa, gos):
        ys = _gelu(ypv) * _sigmoid(av + bg)
        return jnp.concatenate([_rms(yav, goa)[0], _rms(ys, gos)[0]], axis=1)
    (ycat,) = _ew("mix_out", mix_out, [('r', ya), ('r', ypre), ('r', a_glu), ('c', b_glu), ('c', g_oa), ('c', g_os)],
                  [('r', aw + sw, MXU_DTYPE)])
    x1 = _mm("out_proj", ycat, w_out, contract='nn', add=x)
    (h2,) = _ew("rms_ffn", lambda xv, g: _rms(xv, g)[0], [('r', x1), ('c', g_ffn)], [('r', d, MXU_DTYPE)])
    gate4 = _mm("ffn_gate", h2, w_gate4, contract='nn', b_mode='b', o_mode='b', tn=ffs)
    up4 = _mm("ffn_up", h2, w_up4, contract='nn', b_mode='b', o_mode='b', tn=ffs)
    gate_f, up_f = gate4.reshape(4 * t, ffs), up4.reshape(4 * t, ffs)
    (act,) = _ew("swiglu", lambda gv, uv: gv * _sigmoid(gv) * uv, [('r', gate_f), ('r', up_f)],
                 [('r', ffs, MXU_DTYPE)])
    act4 = act.reshape(4, t, ffs)
    x2 = _mm("ffn_down", act4, w_down4, contract='nn', a_mode='c', b_mode='c', add=x1, tk=ffs)

    def loss_fn(xv, tv):
        diff = xv - tv
        return diff * (1.0 / d), diff * diff
    dx2, sq = _ew("loss", loss_fn, [('r', x2), ('r', target)], [('r', d, F32), ('a', d)])

    dact4 = _mm("ffn_down_dx", dx2, w_down4, contract='nt', b_mode='b', o_mode='b', tn=ffs)
    d_w_down4 = _mm("ffn_down_dw", act4, dx2, contract='tn', a_mode='b', o_mode='b', tm=ffs)

    def swiglu_bwd(dav, gv, uv):
        s = _sigmoid(gv)
        return dav * uv * s * (1.0 + gv * (1.0 - s)), dav * gv * s
    dgate, dup = _ew("swiglu_bwd", swiglu_bwd, [('r', dact4.reshape(4 * t, ffs)), ('r', gate_f), ('r', up_f)],
                     [('r', ffs, MXU_DTYPE), ('r', ffs, MXU_DTYPE)])
    dgate4, dup4 = dgate.reshape(4, t, ffs), dup.reshape(4, t, ffs)
    dh2 = _mm("ffn_gate_dx", dgate4, w_gate4, contract='nt', a_mode='c', b_mode='c', tk=ffs)
    dh2 = _mm("ffn_up_dx", dup4, w_up4, contract='nt', a_mode='c', b_mode='c', add=dh2, tk=ffs)
    d_w_gate4 = _mm("ffn_gate_dw", h2, dgate4, contract='tn', b_mode='b', o_mode='b', tn=ffs)
    d_w_up4 = _mm("ffn_up_dw", h2, dup4, contract='tn', b_mode='b', o_mode='b', tn=ffs)

    def rms_res_bwd(xv, g, dyv, resv):
        dx, dg = _rms_bwd(xv, g, dyv)
        return resv + dx, dg
    dx1, d_g_ffn = _ew("rms_ffn_bwd", rms_res_bwd, [('r', x1), ('c', g_ffn), ('r', dh2), ('r', dx2)],
                       [('r', d, F32), ('a', d)])

    dycat = _mm("out_proj_dx", dx1, w_out, contract='nt')
    d_w_out = _mm("out_proj_dw", ycat, dx1, contract='tn')

    def mix_out_bwd(yav, ypv, av, bg, goa, gos, dca, dcs):
        dya, dgoa = _rms_bwd(yav, goa, dca)
        y = _gelu(ypv)
        s = _sigmoid(av + bg)
        dys, dgos = _rms_bwd(y * s, gos, dcs)
        da = dys * y * s * (1.0 - s)
        return dya, da, dys * s, dgoa, dgos, da
    dya, da, dy_direct, d_g_oa, d_g_os, d_b_glu = _ew(
        "mix_out_bwd", mix_out_bwd,
        [('r', ya), ('r', ypre), ('r', a_glu), ('c', b_glu), ('c', g_oa), ('c', g_os),
         ('r', dycat, 0, aw), ('r', dycat, 1, sw)],
        [('r', aw, F32), ('r', sw, MXU_DTYPE), ('r', sw, F32), ('a', aw), ('a', sw), ('a', sw)])
    dy = _mm("glu_proj_dx", da, w_glu, contract='nt', add=dy_direct)
    d_w_glu = _mm("glu_proj_dw", yb, da, contract='tn')
    (dypre,) = _ew("gelu_bwd", lambda dyv, ypv: dyv * _gelu_grad(ypv), [('r', dy), ('r', ypre)],
                   [('r', sw, S5_DTYPE)])

    du2, dmats = _s5_bwd(u2, xin, mats, _to_groups(dypre))
    d_s5 = mats_vjp(dmats)
    du = _from_groups(du2, sw)
    dz4, dbias_t, dqg, dkg = _attn_bwd(z4, qg2, kg2, bias_t, dya)
    d_rpb = _bias_table_grad(dbias_t)
    fold = lambda v: v.reshape(-1, 2, HEAD_DIM).sum(axis=(0, 1))
    dz4 = dz4.at[3].set(du)

    dh = _mm("in_proj_dx", dz4, w_in4, contract='nt', a_mode='c', b_mode='c')
    d_w_in4 = _mm("in_proj_dw", h, dz4, contract='tn', b_mode='b', o_mode='b')
    dx, d_g_mix = _ew("rms_mix_bwd", rms_res_bwd, [('r', x), ('c', g_mix), ('r', dh), ('r', dx1)],
                      [('r', d, F32), ('a', d)])

    colsum = lambda v: v.sum(axis=0)
    d_small = {
        'g_mix': colsum(d_g_mix), 'q_gain': fold(dqg), 'k_gain': fold(dkg), 'rpb': d_rpb,
        'ssm_a_re': d_s5[0], 'ssm_a_im': d_s5[1], 'ssm_b_re': d_s5[2], 'ssm_b_im': d_s5[3],
        'ssm_c_re': d_s5[4], 'ssm_c_im': d_s5[5], 'ssm_log_step': d_s5[6], 'ssm_d': d_s5[7],
        'b_glu': colsum(d_b_glu), 'g_out_attn': colsum(d_g_oa), 'g_out_ssm': colsum(d_g_os), 'g_ffn': colsum(d_g_ffn),
    }
    d_big = (d_w_in4, d_w_glu, d_w_out, d_w_gate4, d_w_up4, d_w_down4)
    return jnp.sum(sq), dx, d_big, d_small


ANY = pl.BlockSpec(memory_space=pl.ANY)


def _place():
    x, y, c = lax.axis_index("x"), lax.axis_index("y"), lax.axis_index("c")
    other_chips = [(1 - x, y), (x, 1 - y), (1 - x, 1 - y)]
    return x, y, c, 2 * x + y, (x, y, 1 - c), other_chips


def _gather_chips(name, arrs):
    n = len(arrs)

    def body(*refs):
        ins, outs = refs[:n], refs[n:2 * n]
        send_sems, recv_sems = refs[2 * n:]
        x, y, c, me, sibling, chips = _place()

        def remote(a, k, src, dst, to):
            return pltpu.make_async_remote_copy(src_ref=src, dst_ref=dst, send_sem=send_sems.at[a, k],
                                                recv_sem=recv_sems.at[a, k], device_id=to, device_id_type=MESH)

        sends = []
        for a in range(n):
            cp = remote(a, 6, ins[a], outs[a].at[me], sibling)
            cp.start()
            sends.append(cp)
        for j, (px, py) in enumerate(chips):
            for a in range(n):
                cp = remote(a, j, ins[a].at[c], outs[a].at[me, c], (px, py, c))
                cp.start()
                sends.append(cp)
        for j, (px, py) in enumerate(chips):
            for a in range(n):
                landed = outs[a].at[2 * px + py, c]
                remote(a, j, landed, landed, (px, py, c)).wait_recv()
                cp = remote(a, 3 + j, landed, landed, sibling)
                cp.start()
                sends.append(cp)
        for j, (px, py) in enumerate(chips):
            for a in range(n):
                theirs = outs[a].at[2 * px + py, 1 - c]
                remote(a, 3 + j, theirs, theirs, sibling).wait_recv()
        for a in range(n):
            remote(a, 6, ins[a], outs[a].at[me], sibling).wait_recv()
        for cp in sends:
            cp.wait_send()

    return pl.pallas_call(
        body, name=name, in_specs=[ANY] * n, out_specs=[ANY] * n,
        out_shape=[jax.ShapeDtypeStruct((N_CHIPS,) + a.shape, a.dtype) for a in arrs],
        scratch_shapes=[pltpu.SemaphoreType.DMA((n, 7)), pltpu.SemaphoreType.DMA((n, 7))],
    )(*arrs)


def _swap_halves(name, parts):
    n = len(parts)

    def body(*refs):
        ins, outs = refs[:n], refs[n:2 * n]
        send_sems, recv_sems = refs[2 * n:]
        x, y, c, me, sibling, chips = _place()
        cps = []
        for a in range(n):
            cp = pltpu.make_async_remote_copy(src_ref=ins[a].at[:, 1 - c], dst_ref=outs[a], send_sem=send_sems.at[a],
                                              recv_sem=recv_sems.at[a], device_id=sibling, device_id_type=MESH)
            cp.start()
            cps.append(cp)
        for cp in cps:
            cp.wait()

    return pl.pallas_call(
        body, name=name, in_specs=[ANY] * n, out_specs=[ANY] * n,
        out_shape=[jax.ShapeDtypeStruct((N_CHIPS,) + p.shape[2:], p.dtype) for p in parts],
        scratch_shapes=[pltpu.SemaphoreType.DMA((n,)), pltpu.SemaphoreType.DMA((n,))],
    )(*parts)


def _scatter_chips(name, sums):
    n = len(sums)

    def body(*refs):
        ins, outs = refs[:n], refs[n:2 * n]
        send_sems, recv_sems = refs[2 * n:]
        x, y, c, me, sibling, chips = _place()
        sends = []
        for j, (px, py) in enumerate(chips):
            for a in range(n):
                cp = pltpu.make_async_remote_copy(
                    src_ref=ins[a].at[2 * px + py], dst_ref=outs[a].at[me], send_sem=send_sems.at[a, j],
                    recv_sem=recv_sems.at[a, j], device_id=(px, py, c), device_id_type=MESH)
                cp.start()
                sends.append(cp)
        for j, (px, py) in enumerate(chips):
            for a in range(n):
                slot = outs[a].at[2 * px + py]
                pltpu.make_async_remote_copy(
                    src_ref=slot, dst_ref=slot, send_sem=send_sems.at[a, j], recv_sem=recv_sems.at[a, j],
                    device_id=(px, py, c), device_id_type=MESH).wait_recv()
        for cp in sends:
            cp.wait_send()

    return pl.pallas_call(
        body, name=name, in_specs=[ANY] * n, out_specs=[ANY] * n,
        out_shape=[jax.ShapeDtypeStruct(s.shape, s.dtype) for s in sums],
        scratch_shapes=[pltpu.SemaphoreType.DMA((n, 3)), pltpu.SemaphoreType.DMA((n, 3))],
    )(*sums)


def _swap_reduced(name, halves):
    n = len(halves)

    def body(*refs):
        ins, outs = refs[:n], refs[n:2 * n]
        send_sems, recv_sems = refs[2 * n:]
        x, y, c, me, sibling, chips = _place()
        cps = []
        for a in range(n):
            cp = pltpu.make_async_remote_copy(src_ref=ins[a], dst_ref=outs[a], send_sem=send_sems.at[a],
                                              recv_sem=recv_sems.at[a], device_id=sibling, device_id_type=MESH)
            cp.start()
            cps.append(cp)
        for cp in cps:
            cp.wait()

    return pl.pallas_call(
        body, name=name, in_specs=[ANY] * n, out_specs=[ANY] * n,
        out_shape=[jax.ShapeDtypeStruct(h.shape, h.dtype) for h in halves],
        scratch_shapes=[pltpu.SemaphoreType.DMA((n,)), pltpu.SemaphoreType.DMA((n,))],
    )(*halves)


def _row_tile(r, want=256):
    t = (min(r, want) // SUBLANES) * SUBLANES
    while r % t:
        t -= SUBLANES
    return t


def _add_own_half(name, part, got, c, out_dtype):
    _, _, r, cols = part.shape
    tr = _row_tile(r)

    def body(c_ref, p_ref, g_ref, o_ref):
        o_ref[...] = (p_ref[...] + g_ref[...]).astype(o_ref.dtype)

    return pl.pallas_call(
        body, name=name,
        grid_spec=pltpu.PrefetchScalarGridSpec(
            num_scalar_prefetch=1, grid=(N_CHIPS, r // tr),
            in_specs=[pl.BlockSpec((None, None, tr, cols), lambda s, i, c_ref: (s, c_ref[0], i, 0)),
                      pl.BlockSpec((None, tr, cols), lambda s, i, c_ref: (s, i, 0))],
            out_specs=pl.BlockSpec((None, tr, cols), lambda s, i, c_ref: (s, i, 0))),
        out_shape=jax.ShapeDtypeStruct(got.shape, out_dtype),
        compiler_params=_cparams(("parallel", "parallel")),
    )(c.reshape(1).astype(jnp.int32), part, got)


def _sum_chips(name, got, own, me):
    _, r, cols = got.shape
    tr = _row_tile(r)

    def body(me_ref, r0, r1, r2, r3, own_ref, o_ref):
        pick = lambda s, ref: jnp.where(me_ref[0] == s, own_ref[...], ref[...]).astype(F32)
        o_ref[...] = ((pick(0, r0) + pick(1, r1)) + pick(2, r2)) + pick(3, r3)

    def slot(s):
        return pl.BlockSpec((None, tr, cols),
                            lambda i, me_ref: (jnp.where(me_ref[0] == s, (s + 1) % N_CHIPS, s), i, 0))

    return pl.pallas_call(
        body, name=name,
        grid_spec=pltpu.PrefetchScalarGridSpec(
            num_scalar_prefetch=1, grid=(r // tr,),
            in_specs=[slot(s) for s in range(N_CHIPS)]
            + [pl.BlockSpec((None, tr, cols), lambda i, me_ref: (me_ref[0], i, 0))],
            out_specs=pl.BlockSpec((tr, cols), lambda i, me_ref: (i, 0))),
        out_shape=jax.ShapeDtypeStruct((r, cols), F32),
        compiler_params=_cparams(("parallel",)),
    )(me.reshape(1).astype(jnp.int32), got, got, got, got, own)


def _adamw_math(wv, gv, mv, vv):
    mv = ADAM_B1 * mv + (1.0 - ADAM_B1) * gv
    vv = ADAM_B2 * vv + (1.0 - ADAM_B2) * (gv * gv)
    m_hat = mv / (1.0 - ADAM_B1 ** ADAM_STEP)
    v_hat = vv / (1.0 - ADAM_B2 ** ADAM_STEP)
    return -ADAM_LR * (m_hat / (jnp.sqrt(v_hat) + ADAM_EPS) + ADAM_WD * wv), mv, vv


def _adamw(name, w, g, m, v):
    cols = w.shape[1]
    return _ew(name, _adamw_math, [('r', w), ('r', g), ('r', m), ('r', v)], [('r', cols, F32)] * 3,
               tr=_row_tile(w.shape[0], 128))


def _adamw_halves(name, w, mine, theirs, m, v, c):
    r, cols = mine.shape
    tr = _row_tile(r, 128)
    nb = r // tr

    def body(c_ref, w_ref, a_ref, b_ref, m_ref, v_ref, g_out, d_out, m_out, v_out):
        g = jnp.where(pl.program_id(0) == c_ref[0], a_ref[...], b_ref[...])
        g_out[...] = g
        d_out[...], m_out[...], v_out[...] = _adamw_math(w_ref[...], g, m_ref[...], v_ref[...])

    whole = pl.BlockSpec((tr, cols), lambda h, i, c_ref: (h * nb + i, 0))
    half = pl.BlockSpec((tr, cols), lambda h, i, c_ref: (i, 0))
    return pl.pallas_call(
        body, name=name,
        grid_spec=pltpu.PrefetchScalarGridSpec(
            num_scalar_prefetch=1, grid=(2, nb),
            in_specs=[whole, half, half, whole, whole], out_specs=[whole] * 4),
        out_shape=[jax.ShapeDtypeStruct(w.shape, F32)] * 4,
        compiler_params=_cparams(("parallel", "parallel")),
    )(c.reshape(1).astype(jnp.int32), w, mine, theirs, m, v)


SMALL_ROWS_ALIGN = 2 * N_CHIPS * SUBLANES


MEDIUM_NAMES = ['ssm_b_re', 'ssm_b_im', 'ssm_c_re', 'ssm_c_im']
PACKED_NAMES = [n for n in SMALL_NAMES if n not in MEDIUM_NAMES]


def _pack_small(d):
    flat = jnp.concatenate([d[n].reshape(-1).astype(F32) for n in PACKED_NAMES])
    rows = -(-flat.shape[0] // (LANES * SMALL_ROWS_ALIGN)) * SMALL_ROWS_ALIGN
    return jnp.pad(flat, (0, rows * LANES - flat.shape[0])).reshape(rows, LANES)


def _unpack_small(packed, like):
    flat = packed.reshape(-1)
    out, off = {}, 0
    for n in PACKED_NAMES:
        size = like[n].size
        out[n] = flat[off:off + size].reshape(like[n].shape)
        off += size
    return out


def kernel(x, g_mix, w_in, q_gain, k_gain, rpb, ssm_a_re, ssm_a_im, ssm_b_re, ssm_b_im, ssm_c_re, ssm_c_im, ssm_log_step, ssm_d, w_glu, b_glu, g_out_attn, g_out_ssm, w_out, g_ffn, w_ffn_gate, w_ffn_up, w_ffn_down, loss_target, m_g_mix, m_w_in, m_q_gain, m_k_gain, m_rpb, m_ssm_a_re, m_ssm_a_im, m_ssm_b_re, m_ssm_b_im, m_ssm_c_re, m_ssm_c_im, m_ssm_log_step, m_ssm_d, m_w_glu, m_b_glu, m_g_out_attn, m_g_out_ssm, m_w_out, m_g_ffn, m_w_ffn_gate, m_w_ffn_up, m_w_ffn_down, v_g_mix, v_w_in, v_q_gain, v_k_gain, v_rpb, v_ssm_a_re, v_ssm_a_im, v_ssm_b_re, v_ssm_b_im, v_ssm_c_re, v_ssm_c_im, v_ssm_log_step, v_ssm_d, v_w_glu, v_b_glu, v_g_out_attn, v_g_out_ssm, v_w_out, v_g_ffn, v_w_ffn_gate, v_w_ffn_up, v_w_ffn_down):
    given = dict(locals())
    w = {n: given[n][0] for n in WEIGHT_NAMES}
    mom = {n: given["m_" + n][0] for n in WEIGHT_NAMES}
    var = {n: given["v_" + n][0] for n in WEIGHT_NAMES}
    d = x.shape[-1]
    c = lax.axis_index("c")

    halves = [w[n].astype(MXU_DTYPE).reshape((2, w[n].shape[0] // 2, w[n].shape[1])) for n in BIG_NAMES]
    gathered = _gather_chips("gather_weights", halves)
    big = tuple(g4.reshape((N_CHIPS, -1, g4.shape[-1])) for g4 in gathered)
    big = (big[0], big[1].reshape(-1, big[1].shape[-1]), big[2].reshape(-1, big[2].shape[-1]), big[3], big[4], big[5])

    sq, dx, d_big, d_small = _local_step(x[0], loss_target[0], big, {n: w[n] for n in SMALL_NAMES})
    loss = lax.psum(0.5 * sq / d, ("x", "y", "c"))

    nbig = len(BIG_NAMES)
    parts = [g.reshape((N_CHIPS, 2, -1, g.shape[-1])) for g in d_big]
    parts += [d_small[n].reshape(N_CHIPS, 2, -1, LANES) for n in MEDIUM_NAMES]
    parts.append(_pack_small(d_small).reshape(N_CHIPS, 2, -1, LANES))
    got = _swap_halves("reduce_swap_halves", parts)
    payload = [GRAD_PAYLOAD_DTYPE] * nbig + [F32] * (len(parts) - nbig)
    sums = [_add_own_half("reduce_add_%d" % a, p, gt, c, dt) for a, (p, gt, dt) in enumerate(zip(parts, got, payload))]
    got = _scatter_chips("reduce_scatter_chips", sums)
    me = 2 * lax.axis_index("x") + lax.axis_index("y")
    mine = [_sum_chips("reduce_sum_%d" % a, gt, sm_, me) for a, (gt, sm_) in enumerate(zip(got, sums))]
    theirs = _swap_reduced("reduce_swap_reduced", mine)
    in_order = lambda a: jnp.where(c == 0, jnp.stack([mine[a], theirs[a]]), jnp.stack([theirs[a], mine[a]]))
    repl = _gather_chips("gather_small", [in_order(a) for a in range(nbig, len(parts))])
    repl = [r.reshape(-1, LANES) for r in repl]
    like = {n: w[n] for n in SMALL_NAMES}
    grad_small = _unpack_small(repl[-1], like)
    grad_small.update({n: r.reshape(w[n].shape) for n, r in zip(MEDIUM_NAMES, repl)})

    grad_big, delta, new_m, new_v = {}, {}, {}, {}
    for a, n in enumerate(BIG_NAMES):
        grad_big[n], delta[n], new_m[n], new_v[n] = _adamw_halves("adamw_%d" % a, w[n], mine[a], theirs[a],
                                                                  mom[n], var[n], c)
    for n, r in zip(MEDIUM_NAMES, repl):
        res = _adamw("adamw_" + n, w[n].reshape(-1, LANES), r, mom[n].reshape(-1, LANES), var[n].reshape(-1, LANES))
        delta[n], new_m[n], new_v[n] = (t.reshape(w[n].shape) for t in res)
    sd, sm, sv = _adamw("adamw_small", _pack_small(w), repl[-1], _pack_small(mom), _pack_small(var))
    delta.update(_unpack_small(sd, like))
    new_m.update(_unpack_small(sm, like))
    new_v.update(_unpack_small(sv, like))
    grads = {**grad_big, **grad_small}
    lead = lambda t: t[None]
    return (loss, dx[None], *[lead(grads[n]) for n in WEIGHT_NAMES], *[lead(delta[n]) for n in WEIGHT_NAMES],
            *[lead(new_m[n]) for n in WEIGHT_NAMES], *[lead(new_v[n]) for n in WEIGHT_NAMES])
```

```python
import functools
import math

import jax
import jax.numpy as jnp
from jax import lax
from jax.experimental import pallas as pl
from jax.experimental.pallas import tpu as pltpu

F32 = jnp.float32
BF16 = jnp.bfloat16
MXU_DTYPE = BF16
GRAD_PAYLOAD_DTYPE = BF16
S5_DTYPE = BF16
HI = lax.Precision.HIGHEST
VMEM_LIMIT_V7X = 56 * 1024 * 1024
LANES = 128
SUBLANES = 8

GRID_W = 64
WIN_H = 8
WIN_W = 16
HEAD_DIM = 64
SSM_GROUP_CH = 16
SSM_STATE = 64
S5_CHUNK = 16
RMS_EPS = 1e-6
NEG_INF = -1e30
N_CHIPS = 4
MESH = pl.DeviceIdType.MESH

ADAM_LR = 0.001
ADAM_B1 = 0.9
ADAM_B2 = 0.999
ADAM_EPS = 1e-08
ADAM_WD = 0.01
ADAM_STEP = 10

WEIGHT_NAMES = ['g_mix', 'w_in', 'q_gain', 'k_gain', 'rpb', 'ssm_a_re', 'ssm_a_im', 'ssm_b_re', 'ssm_b_im',
                'ssm_c_re', 'ssm_c_im', 'ssm_log_step', 'ssm_d', 'w_glu', 'b_glu', 'g_out_attn', 'g_out_ssm',
                'w_out', 'g_ffn', 'w_ffn_gate', 'w_ffn_up', 'w_ffn_down']
BIG_NAMES = ['w_in', 'w_glu', 'w_out', 'w_ffn_gate', 'w_ffn_up', 'w_ffn_down']
LATE_NAMES = BIG_NAMES[1:]
SMALL_NAMES = [n for n in WEIGHT_NAMES if n not in BIG_NAMES]


def _cparams(sem):
    return pltpu.CompilerParams(dimension_semantics=sem, vmem_limit_bytes=VMEM_LIMIT_V7X)


def _tile(n, want):
    if n <= want:
        return n
    t = (want // LANES) * LANES
    while t >= LANES:
        if n % t == 0:
            return t
        t -= LANES
    return n


def _mm(name, a, b, *, contract, a_mode='2', b_mode='2', o_mode='2', out_dtype=F32, add=None, exact=False,
        tm=1024, tn=1024, tk=512):
    dn = {'nn': (((1,), (0,)), ((), ())), 'nt': (((1,), (1,)), ((), ())), 'tn': (((0,), (0,)), ((), ()))}[contract]
    ar, ac = a.shape[-2:]
    br, bc = b.shape[-2:]
    m, kdim = (ar, ac) if contract != 'tn' else (ac, ar)
    n = bc if contract != 'nt' else br
    assert kdim == (br if contract != 'nt' else bc), (name, a.shape, b.shape)
    nbatch = 1
    for arr, mode in ((a, a_mode), (b, b_mode)):
        if mode == 'b':
            nbatch = arr.shape[0]
    nstack = 1
    for arr, mode in ((a, a_mode), (b, b_mode)):
        if mode == 'c':
            nstack = arr.shape[0]
    tm, tn, tk = _tile(m, tm), _tile(n, tn), _tile(kdim, tk)
    nkin = kdim // tk
    nk = nstack * nkin
    grid = (nbatch, m // tm, n // tn, nk)

    def spec(mode, block, rc):
        def imap(s, i, j, kk):
            r, c = rc(i, j, kk % nkin)
            if mode == '2':
                return (r, c)
            return (s if mode == 'b' else kk // nkin, r, c)
        return pl.BlockSpec(block if mode == '2' else (None,) + block, imap)

    a_spec = spec(a_mode, (tm, tk) if contract != 'tn' else (tk, tm),
                  (lambda i, j, k: (i, k)) if contract != 'tn' else (lambda i, j, k: (k, i)))
    b_spec = spec(b_mode, (tk, tn) if contract != 'nt' else (tn, tk),
                  (lambda i, j, k: (k, j)) if contract != 'nt' else (lambda i, j, k: (j, k)))
    o_spec = spec(o_mode, (tm, tn), lambda i, j, k: (i, j))
    out_shape = (m, n) if o_mode == '2' else (nbatch, m, n)
    has_add = add is not None

    def body(*refs):
        if has_add:
            a_ref, b_ref, add_ref, o_ref, acc_ref = refs
        else:
            a_ref, b_ref, o_ref, acc_ref = refs
        k = pl.program_id(3)

        @pl.when(k == 0)
        def _():
            acc_ref[...] = jnp.zeros_like(acc_ref)

        if exact:
            acc_ref[...] += lax.dot_general(a_ref[...].astype(F32), b_ref[...].astype(F32), dn, precision=HI,
                                            preferred_element_type=F32)
        else:
            acc_ref[...] += lax.dot_general(a_ref[...].astype(MXU_DTYPE), b_ref[...].astype(MXU_DTYPE), dn,
                                            preferred_element_type=F32)

        @pl.when(k == nk - 1)
        def _():
            r = acc_ref[...]
            if has_add:
                r = r + add_ref[...].astype(F32)
            o_ref[...] = r.astype(o_ref.dtype)

    in_specs = [a_spec, b_spec] + ([o_spec] if has_add else [])
    args = (a, b) + ((add,) if has_add else ())
    return pl.pallas_call(
        body, name=name, grid=grid, in_specs=in_specs, out_specs=o_spec,
        out_shape=jax.ShapeDtypeStruct(out_shape, out_dtype),
        scratch_shapes=[pltpu.VMEM((tm, tn), F32)],
        compiler_params=_cparams(("parallel", "parallel", "parallel", "arbitrary")),
    )(*args)


def _ew(name, fn, ins, outs, tr=256):
    rows = next(x[1].shape[0] for x in ins if x[0] == 'r')
    tr = min(tr, rows)
    assert rows % tr == 0 and tr % SUBLANES == 0, (name, rows, tr)
    in_specs, args = [], []
    for x in ins:
        if x[0] == 'r' and len(x) == 2:
            in_specs.append(pl.BlockSpec((tr, x[1].shape[1]), lambda i: (i, 0)))
        elif x[0] == 'r':
            in_specs.append(pl.BlockSpec((tr, x[3]), functools.partial(lambda cb, i: (i, cb), x[2])))
        else:
            in_specs.append(pl.BlockSpec(x[1].shape, lambda i: (0, 0)))
        args.append(x[1])
    out_specs, out_shapes = [], []
    for o in outs:
        if o[0] == 'r':
            out_specs.append(pl.BlockSpec((tr, o[1]), lambda i: (i, 0)))
            out_shapes.append(jax.ShapeDtypeStruct((rows, o[1]), o[2]))
        else:
            out_specs.append(pl.BlockSpec((SUBLANES, o[1]), lambda i: (0, 0)))
            out_shapes.append(jax.ShapeDtypeStruct((SUBLANES, o[1]), F32))
    nin = len(ins)
    has_acc = any(o[0] == 'a' for o in outs)

    def body(*refs):
        vals = fn(*[r[...].astype(F32) for r in refs[:nin]])
        if not isinstance(vals, (tuple, list)):
            vals = (vals,)
        i = pl.program_id(0)
        for o, ref, v in zip(outs, refs[nin:], vals):
            if o[0] == 'r':
                ref[...] = v.astype(ref.dtype)
            else:
                part = v.astype(F32).reshape(tr // SUBLANES, SUBLANES, o[1]).sum(axis=0)

                @pl.when(i == 0)
                def _(ref=ref, part=part):
                    ref[...] = part

                @pl.when(i > 0)
                def _(ref=ref, part=part):
                    ref[...] += part

    res = pl.pallas_call(
        body, name=name, grid=(rows // tr,), in_specs=in_specs, out_specs=out_specs, out_shape=out_shapes,
        compiler_params=_cparams(("arbitrary",) if has_acc else ("parallel",)),
    )(*args)
    return res


def _rms(x, g):
    r = lax.rsqrt(jnp.mean(x * x, axis=-1, keepdims=True) + RMS_EPS)
    xr = x * r
    return xr * g, xr


def _rms_bwd(x, g, dy):
    r = lax.rsqrt(jnp.mean(x * x, axis=-1, keepdims=True) + RMS_EPS)
    xr = x * r
    gdy = g * dy
    dx = r * (gdy - xr * jnp.mean(xr * gdy, axis=-1, keepdims=True))
    return dx, dy * xr


def _sigmoid(x):
    return 1.0 / (1.0 + jnp.exp(-x))


_GELU_C = math.sqrt(2.0 / math.pi)


def _gelu(x):
    return 0.5 * x * (1.0 + jnp.tanh(_GELU_C * (x + 0.044715 * x * x * x)))


def _gelu_grad(x):
    t = jnp.tanh(_GELU_C * (x + 0.044715 * x * x * x))
    return 0.5 * (1.0 + t) + 0.5 * x * (1.0 - t * t) * _GELU_C * (1.0 + 3 * 0.044715 * x * x)


ATTN_ROWS_PER_STEP = 8
NT_DIMS = (((1,), (1,)), ((), ()))
NN_DIMS = (((1,), (0,)), ((), ()))
TN_DIMS = (((0,), (0,)), ((), ()))


def _attn_geometry(r, rows):
    row_start = jnp.clip(r - WIN_H // 2, 0, rows - WIN_H)
    key0 = pl.multiple_of(row_start * GRID_W, GRID_W)
    bias0 = pl.multiple_of((row_start - r + (WIN_H - 1)) * GRID_W, GRID_W)
    return key0, bias0


def _window_onehot():
    c = jnp.arange(GRID_W)
    col_start = jnp.clip(c - WIN_W // 2, 0, GRID_W - WIN_W)
    col_in = (c[None, :] >= col_start[:, None]) & (c[None, :] < col_start[:, None] + WIN_W)
    dc = jnp.clip(c[None, :] - c[:, None], -(WIN_W - 1), WIN_W - 1) + (WIN_W - 1)
    onehot = ((dc[:, :, None] == jnp.arange(2 * WIN_W - 1)[None, None, :]) & col_in[:, :, None]).astype(F32)
    return onehot, col_in


def _bias_table(rpb):
    onehot, col_in = _window_onehot()
    nh = rpb.shape[0]
    tab = jnp.einsum('perd,qkd->prkeq', rpb.reshape(nh // 2, 2, 2 * WIN_H - 1, 2 * WIN_W - 1), onehot, precision=HI)
    tab = tab + jnp.where(col_in, 0.0, NEG_INF).T[None, None, :, None, :]
    return tab.reshape(nh // 2, (2 * WIN_H - 1) * GRID_W, 2 * GRID_W)


def _bias_table_grad(dtab):
    onehot, _ = _window_onehot()
    npair = dtab.shape[0]
    d = dtab.reshape(npair, 2 * WIN_H - 1, GRID_W, 2, GRID_W)
    return jnp.einsum('prkeq,qkd->perd', d, onehot, precision=HI).reshape(2 * npair, 2 * WIN_H - 1, 2 * WIN_W - 1)


def _lane_lo(shape):
    return lax.broadcasted_iota(jnp.int32, shape, 1) < HEAD_DIM


def _half_sums(v):
    lo = _lane_lo(v.shape)
    s_lo = jnp.sum(jnp.where(lo, v, 0.0), axis=1, keepdims=True)
    s_hi = jnp.sum(jnp.where(lo, 0.0, v), axis=1, keepdims=True)
    return jnp.where(lo, s_lo, s_hi)


def _rms_pair(x, g):
    r = lax.rsqrt(_half_sums(x * x) * (1.0 / HEAD_DIM) + RMS_EPS)
    return x * r * g


def _rms_pair_bwd(x, g, dy):
    r = lax.rsqrt(_half_sums(x * x) * (1.0 / HEAD_DIM) + RMS_EPS)
    xr = x * r
    gdy = g * dy
    dx = r * (gdy - xr * (_half_sums(xr * gdy) * (1.0 / HEAD_DIM)))
    return dx, dy * xr


def _blockdiag(a):
    a2 = jnp.concatenate([a, a], axis=0)
    row_hi = lax.broadcasted_iota(jnp.int32, a2.shape, 0) >= GRID_W
    lane_hi = lax.broadcasted_iota(jnp.int32, a2.shape, 1) >= HEAD_DIM
    return jnp.where(row_hi == lane_hi, a2, 0.0).astype(MXU_DTYPE)


def _diag_blocks(m):
    return jnp.where(_lane_lo((GRID_W, 2 * HEAD_DIM)), m[:GRID_W], m[GRID_W:])


def _attn_scores(qb, kb, bias):
    st = lax.dot_general(kb, qb, NT_DIMS, preferred_element_type=F32)
    st = st * (1.0 / math.sqrt(HEAD_DIM)) + bias
    mx = jnp.max(st, axis=0, keepdims=True)
    p = jnp.exp(st - mx)
    return p * (1.0 / jnp.sum(p, axis=0, keepdims=True))


def _attn_fwd(z4, qg2, kg2, bias_t, comm=None, comm_arrs=()):
    _, t, aw = z4.shape
    rows = t // GRID_W
    npair = aw // (2 * HEAD_DIM)
    nkeys = WIN_H * GRID_W
    nb = bias_t.shape[1]
    rps = min(ATTN_ROWS_PER_STEP, rows)
    blk = rps * GRID_W
    nsteps = rows // rps
    ncomm = len(comm_arrs)

    def body(*refs):
        q_ref, k_ref, v_ref, qg_ref, kg_ref, b_ref = refs[:6]
        c_ins, o_ref, c_outs = refs[6:6 + ncomm], refs[6 + ncomm], refs[7 + ncomm:7 + 2 * ncomm]
        kn_ref, vb_ref = refs[7 + 2 * ncomm:9 + 2 * ncomm]
        sems = refs[9 + 2 * ncomm:]
        pr, rb = pl.program_id(0), pl.program_id(1)
        if comm is not None:
            @pl.when((pr == 0) & (rb == 0))
            def _():
                comm.start(c_ins, c_outs, *sems)

        @pl.when(rb == 0)
        def _():
            kn_ref[...] = _rms_pair(k_ref[...], kg_ref[...]).astype(MXU_DTYPE)
            vb_ref[...] = v_ref[...].astype(MXU_DTYPE)

        def row(i, carry):
            key0, bias0 = _attn_geometry(rb * rps + i, rows)
            at = pl.ds(pl.multiple_of(i * GRID_W, GRID_W), GRID_W)
            qb = _blockdiag(_rms_pair(q_ref[at, :], qg_ref[...]))
            pt = _attn_scores(qb, kn_ref[pl.ds(key0, nkeys), :], b_ref[pl.ds(bias0, nkeys), :])
            both = lax.dot_general(pt.astype(MXU_DTYPE), vb_ref[pl.ds(key0, nkeys), :], TN_DIMS,
                                   preferred_element_type=F32)
            o_ref[at, :] = _diag_blocks(both)
            return carry

        lax.fori_loop(0, rps, row, 0, unroll=2)
        if comm is not None:
            @pl.when((pr == npair - 1) & (rb == nsteps - 1))
            def _():
                comm.finish(c_ins, c_outs, *sems)

    pair_cols = lambda lead: pl.BlockSpec((None, t, 2 * HEAD_DIM), lambda p, r: (lead, 0, p))
    res = pl.pallas_call(
        body, name="attn_fwd", grid=(npair, nsteps),
        in_specs=[pl.BlockSpec((None, blk, 2 * HEAD_DIM), lambda p, r: (0, r, p)), pair_cols(1), pair_cols(2),
                  pl.BlockSpec((1, 2 * HEAD_DIM), lambda p, r: (0, 0)),
                  pl.BlockSpec((1, 2 * HEAD_DIM), lambda p, r: (0, 0)),
                  pl.BlockSpec((None, nb, 2 * GRID_W), lambda p, r: (p, 0, 0))] + [ANY] * ncomm,
        out_specs=[pl.BlockSpec((blk, 2 * HEAD_DIM), lambda p, r: (r, p))] + [ANY] * ncomm,
        out_shape=[jax.ShapeDtypeStruct((t, aw), F32)] + (comm.out_shape if comm is not None else []),
        scratch_shapes=[pltpu.VMEM((t, 2 * HEAD_DIM), MXU_DTYPE), pltpu.VMEM((t, 2 * HEAD_DIM), MXU_DTYPE)]
        + (comm.scratch if comm is not None else []),
        compiler_params=_cparams(("arbitrary", "arbitrary")),
    )(z4, z4, z4, qg2, kg2, bias_t, *comm_arrs)
    return res[0], res[1:]


def _attn_bwd(z4, qg2, kg2, bias_t, dya, comm=None, comm_arrs=()):
    _, t, aw = z4.shape
    rows = t // GRID_W
    npair = aw // (2 * HEAD_DIM)
    nkeys = WIN_H * GRID_W
    nb = bias_t.shape[1]
    rps = min(ATTN_ROWS_PER_STEP, rows)
    blk = rps * GRID_W
    nsteps = rows // rps
    scale = 1.0 / math.sqrt(HEAD_DIM)
    ncomm = len(comm_arrs)

    def body(*refs):
        q_ref, k_ref, v_ref, qg_ref, kg_ref, b_ref, do_ref = refs[:7]
        c_ins = refs[7:7 + ncomm]
        dz_ref, db_ref, dqg_ref, dkg_ref = refs[7 + ncomm:11 + ncomm]
        c_outs = refs[11 + ncomm:11 + 2 * ncomm]
        kn_ref, vb_ref, dkn_ref, dv_ref = refs[11 + 2 * ncomm:15 + 2 * ncomm]
        sems = refs[15 + 2 * ncomm:]
        pr, rb = pl.program_id(0), pl.program_id(1)
        if comm is not None:
            @pl.when((pr == 0) & (rb == 0))
            def _():
                comm.start(c_ins, c_outs, *sems)

        @pl.when(rb == 0)
        def _():
            kn_ref[...] = _rms_pair(k_ref[...], kg_ref[...]).astype(MXU_DTYPE)
            vb_ref[...] = v_ref[...].astype(MXU_DTYPE)
            dkn_ref[...] = jnp.zeros_like(dkn_ref)
            dv_ref[...] = jnp.zeros_like(dv_ref)
            db_ref[...] = jnp.zeros_like(db_ref)
            dqg_ref[...] = jnp.zeros_like(dqg_ref)

        def row(i, dqg_sum):
            r = rb * rps + i
            key0, bias0 = _attn_geometry(r, rows)
            keys = pl.ds(key0, nkeys)
            at = pl.ds(pl.multiple_of(i * GRID_W, GRID_W), GRID_W)
            q = q_ref[at, :]
            qb = _blockdiag(_rms_pair(q, qg_ref[...]))
            dob = _blockdiag(do_ref[at, :])
            kb = kn_ref[keys, :]
            pt = _attn_scores(qb, kb, b_ref[pl.ds(bias0, nkeys), :])
            dv_ref[keys, :] += lax.dot_general(pt.astype(MXU_DTYPE), dob, NN_DIMS, preferred_element_type=F32)
            dpt = lax.dot_general(vb_ref[keys, :], dob, NT_DIMS, preferred_element_type=F32)
            dst = pt * (dpt - jnp.sum(pt * dpt, axis=0, keepdims=True))
            db_ref[pl.ds(bias0, nkeys), :] += dst
            dsb = dst.astype(MXU_DTYPE)
            dkn_ref[keys, :] += scale * lax.dot_general(dsb, qb, NN_DIMS, preferred_element_type=F32)
            dqn = scale * _diag_blocks(lax.dot_general(dsb, kb, TN_DIMS, preferred_element_type=F32))
            dq, dqg = _rms_pair_bwd(q, qg_ref[...], dqn)
            dz_ref[0, pl.ds(pl.multiple_of(r * GRID_W, GRID_W), GRID_W), :] = dq.astype(dz_ref.dtype)
            return dqg_sum + jnp.sum(dqg, axis=0, keepdims=True)

        dqg_ref[...] += lax.fori_loop(0, rps, row, jnp.zeros((1, 2 * HEAD_DIM), F32), unroll=2)

        @pl.when(rb == nsteps - 1)
        def _():
            dk, dkg = _rms_pair_bwd(k_ref[...], kg_ref[...], dkn_ref[...])
            dz_ref[1] = dk.astype(dz_ref.dtype)
            dz_ref[2] = dv_ref[...].astype(dz_ref.dtype)
            dkg_ref[...] = jnp.sum(dkg, axis=0, keepdims=True)

        if comm is not None:
            @pl.when((pr == npair - 1) & (rb == nsteps - 1))
            def _():
                comm.finish(c_ins, c_outs, *sems)

    pair_cols = lambda lead: pl.BlockSpec((None, t, 2 * HEAD_DIM), lambda p, r: (lead, 0, p))
    pair_vec = pl.BlockSpec((None, 1, 2 * HEAD_DIM), lambda p, r: (p, 0, 0))
    res = pl.pallas_call(
        body, name="attn_bwd", grid=(npair, nsteps),
        in_specs=[pl.BlockSpec((None, blk, 2 * HEAD_DIM), lambda p, r: (0, r, p)), pair_cols(1), pair_cols(2),
                  pl.BlockSpec((1, 2 * HEAD_DIM), lambda p, r: (0, 0)),
                  pl.BlockSpec((1, 2 * HEAD_DIM), lambda p, r: (0, 0)),
                  pl.BlockSpec((None, nb, 2 * GRID_W), lambda p, r: (p, 0, 0)),
                  pl.BlockSpec((blk, 2 * HEAD_DIM), lambda p, r: (r, p))] + [ANY] * ncomm,
        out_specs=[pl.BlockSpec((3, t, 2 * HEAD_DIM), lambda p, r: (0, 0, p)),
                   pl.BlockSpec((None, nb, 2 * GRID_W), lambda p, r: (p, 0, 0)),
                   pair_vec, pair_vec] + [ANY] * ncomm,
        out_shape=[jax.ShapeDtypeStruct((4, t, aw), MXU_DTYPE), jax.ShapeDtypeStruct(bias_t.shape, F32),
                   jax.ShapeDtypeStruct((npair, 1, 2 * HEAD_DIM), F32),
                   jax.ShapeDtypeStruct((npair, 1, 2 * HEAD_DIM), F32)] + (comm.out_shape if comm is not None else []),
        scratch_shapes=[pltpu.VMEM((t, 2 * HEAD_DIM), MXU_DTYPE), pltpu.VMEM((t, 2 * HEAD_DIM), MXU_DTYPE),
                        pltpu.VMEM((t, 2 * HEAD_DIM), F32), pltpu.VMEM((t, 2 * HEAD_DIM), F32)]
        + (comm.scratch if comm is not None else []),
        compiler_params=_cparams(("arbitrary", "arbitrary")),
    )(z4, z4, z4, qg2, kg2, bias_t, dya, *comm_arrs)
    return res[:4], res[4:]


def _s5_mats(a_re, a_im, b_re, b_im, c_re, c_im, log_step, d_skip):
    nd, g, p = a_re.shape
    c = b_re.shape[-1]
    L = S5_CHUNK
    lr = jnp.minimum(a_re, -1e-4).transpose(1, 0, 2)
    li = a_im.transpose(1, 0, 2)
    dt = jnp.exp(log_step).T[..., None]
    n = jnp.arange(L + 1, dtype=F32)[None, :, None, None]
    mag = jnp.exp(n * (lr * dt)[:, None])
    ang = n * (li * dt)[:, None]
    pw_r, pw_i = mag * jnp.cos(ang), mag * jnp.sin(ang)
    den = lr * lr + li * li
    nr, ni = pw_r[:, 1] - 1.0, pw_i[:, 1]
    cr, ci = (nr * lr + ni * li) / den, (ni * lr - nr * li) / den
    bt_r, bt_i = b_re.transpose(1, 3, 0, 2), b_im.transpose(1, 3, 0, 2)
    bb_r = cr[:, None] * bt_r - ci[:, None] * bt_i
    bb_i = cr[:, None] * bt_i + ci[:, None] * bt_r
    ct_r, ct_i = c_re.transpose(1, 2, 0, 3), c_im.transpose(1, 2, 0, 3)

    def cols(x_re, x_im):
        return jnp.concatenate([x_re[..., 0, :], x_im[..., 0, :], x_re[..., 1, :], x_im[..., 1, :]], axis=-1)

    e_r = jnp.stack([pw_r[:, :L, 0][:, ::-1], pw_r[:, :L, 1]], axis=2)
    e_i = jnp.stack([pw_i[:, :L, 0][:, ::-1], pw_i[:, :L, 1]], axis=2)
    ws = (cols(e_r, e_r)[:, :, None] * cols(bb_r, bb_i)[:, None]
          + cols(e_i, e_i)[:, :, None] * cols(-bb_i, bb_r)[:, None]).reshape(g, L * c, 4 * p)
    f_r = jnp.stack([pw_r[:, 1:, 0], pw_r[:, 1:, 1][:, ::-1]], axis=2)
    f_i = jnp.stack([pw_i[:, 1:, 0], pw_i[:, 1:, 1][:, ::-1]], axis=2)
    wot = (cols(f_r, f_i)[:, :, None] * cols(ct_r, -ct_r)[:, None]
           + cols(f_i, f_r)[:, :, None] * cols(-ct_i, -ct_i)[:, None]).reshape(g, L * c, 4 * p)
    qr, qi = pw_r[:, :L, None], pw_i[:, :L, None]
    kp_r, kp_i = qr * bb_r[:, None] - qi * bb_i[:, None], qr * bb_i[:, None] + qi * bb_r[:, None]
    kern = [jnp.einsum('gnip,gop->gino', kp_r[:, :, :, d], ct_r[:, :, d], precision=HI)
            - jnp.einsum('gnip,gop->gino', kp_i[:, :, :, d], ct_i[:, :, d], precision=HI) for d in range(2)]
    skip = d_skip.reshape(g, c, 1, 1) * jnp.eye(c, dtype=F32)[None, :, None, :]
    by_offset = jnp.concatenate([kern[1][:, :, :0:-1], kern[0][:, :, :1] + kern[1][:, :, :1] + skip,
                                 kern[0][:, :, 1:]], axis=2).reshape(g, c, (2 * L - 1) * c)
    mt = jnp.stack([by_offset[:, :, (L - 1 - j) * c:(2 * L - 1 - j) * c] for j in range(L)], axis=1)
    mt = mt.reshape(g, L * c, L * c)
    lr16, li16 = pw_r[:, L], pw_i[:, L]
    fa = jnp.concatenate([lr16[:, 0], lr16[:, 0], lr16[:, 1], lr16[:, 1]], axis=-1)
    fb = jnp.concatenate([-li16[:, 0], li16[:, 0], -li16[:, 1], li16[:, 1]], axis=-1)
    return mt, ws, wot, fa, fb


def _gmm(name, a, b, contract, a_stacked=False, b_stacked=False, o_stacked=False, add=None, out_dtype=F32):
    w = S5_CHUNK * SSM_GROUP_CH
    g = (a.shape[0] if a_stacked else a.shape[1] // w)
    dn = {'nn': (((1,), (0,)), ((), ())), 'nt': (((1,), (1,)), ((), ())), 'tn': (((0,), (0,)), ((), ()))}[contract]

    def spec(arr, stacked):
        if stacked:
            return pl.BlockSpec((None,) + arr.shape[1:], lambda i: (i, 0, 0))
        return pl.BlockSpec((arr.shape[0], w), lambda i: (0, i))

    m = (a.shape[1] if a_stacked else a.shape[0]) if contract != 'tn' else w
    n = w
    if o_stacked:
        o_spec = pl.BlockSpec((None, m, n), lambda i: (i, 0, 0))
        o_shape = (g, m, n)
    else:
        o_spec = pl.BlockSpec((m, n), lambda i: (0, i))
        o_shape = (m, g * n)
    has_add = add is not None

    def body(*refs):
        if has_add:
            a_ref, b_ref, add_ref, o_ref = refs
        else:
            a_ref, b_ref, o_ref = refs
        r = lax.dot_general(a_ref[...].astype(S5_DTYPE), b_ref[...].astype(S5_DTYPE), dn,
                            precision=HI if S5_DTYPE == F32 else None, preferred_element_type=F32)
        if has_add:
            r = r + add_ref[...]
        o_ref[...] = r.astype(o_ref.dtype)

    in_specs = [spec(a, a_stacked), spec(b, b_stacked)] + ([o_spec] if has_add else [])
    return pl.pallas_call(
        body, name=name, grid=(g,), in_specs=in_specs, out_specs=o_spec,
        out_shape=jax.ShapeDtypeStruct(o_shape, out_dtype), compiler_params=_cparams(("parallel",)),
    )(*((a, b) + ((add,) if has_add else ())))


def _s5_scan(name, s, fa, fb, rev0, xin=None):
    nk, g, w = s.shape
    hw = w // 2
    gb = min(g, 16)
    with_acc = xin is not None

    def body(*refs):
        if with_acc:
            s_ref, a_ref, b_ref, x_ref, o_ref, pa_ref, pb_ref = refs
        else:
            s_ref, a_ref, b_ref, o_ref = refs
        fa0, fb0, fa1, fb1 = a_ref[:, :hw], b_ref[:, :hw], a_ref[:, hw:], b_ref[:, hw:]

        def step(i, carry):
            x0, x1, pa0, pb0, pa1, pb1 = carry
            k0 = (nk - 1 - i) if rev0 else i
            k1 = i if rev0 else (nk - 1 - i)
            o_ref[k0, :, :hw] = x0
            o_ref[k1, :, hw:] = x1
            if with_acc:
                xi0, xi1 = x_ref[k0, :, :hw], x_ref[k1, :, hw:]
                pa0 = pa0 + x0 * xi0
                pb0 = pb0 + x0 * pltpu.roll(xi0, hw // 2, 1)
                pa1 = pa1 + x1 * xi1
                pb1 = pb1 + x1 * pltpu.roll(xi1, hw // 2, 1)
            x0 = fa0 * x0 + fb0 * pltpu.roll(x0, hw // 2, 1) + s_ref[k0, :, :hw]
            x1 = fa1 * x1 + fb1 * pltpu.roll(x1, hw // 2, 1) + s_ref[k1, :, hw:]
            return x0, x1, pa0, pb0, pa1, pb1

        z = jnp.zeros((gb, hw), F32)
        res = lax.fori_loop(0, nk, step, (z, z, z, z, z, z))
        if with_acc:
            pa_ref[:, :hw] = res[2]
            pb_ref[:, :hw] = res[3]
            pa_ref[:, hw:] = res[4]
            pb_ref[:, hw:] = res[5]

    seq = pl.BlockSpec((nk, gb, w), lambda i: (0, i, 0))
    vec = pl.BlockSpec((gb, w), lambda i: (i, 0))
    in_specs = [seq, vec, vec] + ([seq] if with_acc else [])
    out_specs = [seq] + ([vec, vec] if with_acc else [])
    out_shape = [jax.ShapeDtypeStruct((nk, g, w), F32)] + (
        [jax.ShapeDtypeStruct((g, w), F32)] * 2 if with_acc else [])
    return pl.pallas_call(
        body, name=name, grid=(g // gb,), in_specs=in_specs, out_specs=out_specs, out_shape=out_shape,
        compiler_params=_cparams(("parallel",)),
    )(*((s, fa, fb) + ((xin,) if with_acc else ())))


def _to_groups(u):
    t, sw = u.shape
    g = sw // SSM_GROUP_CH
    return u.reshape(t // S5_CHUNK, S5_CHUNK, g, SSM_GROUP_CH).transpose(0, 2, 1, 3).reshape(t // S5_CHUNK, -1)


def _from_groups(y, sw):
    nk = y.shape[0]
    g = sw // SSM_GROUP_CH
    return y.reshape(nk, g, S5_CHUNK, SSM_GROUP_CH).transpose(0, 2, 1, 3).reshape(nk * S5_CHUNK, sw)


def _s5_fwd(u2, mats):
    mt, ws, wot, fa, fb = mats
    nk = u2.shape[0]
    g = mt.shape[0]
    y_intra = _gmm("s5_intra", u2, mt, 'nn', b_stacked=True)
    s = _gmm("s5_chunk_state", u2, ws, 'nn', b_stacked=True)
    (xin,) = _s5_scan("s5_scan", s.reshape(nk, g, -1), fa, fb, False)
    xin = xin.reshape(nk, -1)
    return _gmm("s5_inter", xin, wot, 'nt', b_stacked=True, add=y_intra, out_dtype=S5_DTYPE), xin


def _s5_bwd(u2, xin, mats, dy2):
    mt, ws, wot, fa, fb = mats
    nk = u2.shape[0]
    g = mt.shape[0]
    dxin = _gmm("s5_dxin", dy2, wot, 'nn', b_stacked=True)
    ds, pa, pb = _s5_scan("s5_scan_adj", dxin.reshape(nk, g, -1), fa, -fb, True, xin=xin.reshape(nk, g, -1))
    ds = ds.reshape(nk, -1)
    du_a = _gmm("s5_du_intra", dy2, mt, 'nt', b_stacked=True)
    du2 = _gmm("s5_du_state", ds, ws, 'nt', b_stacked=True, add=du_a, out_dtype=S5_DTYPE)
    dmt = _gmm("s5_dmt", u2, dy2, 'tn', o_stacked=True)
    dws = _gmm("s5_dws", u2, ds, 'tn', o_stacked=True)
    dwot = _gmm("s5_dwot", dy2, xin, 'tn', o_stacked=True)
    return du2, (dmt, dws, dwot, pa, pb)


def _late_weights(gathered):
    w_glu, w_out, w_gate, w_up, w_down = (g4.reshape((N_CHIPS, -1, g4.shape[-1])) for g4 in gathered)
    return w_glu.reshape(-1, w_glu.shape[-1]), w_out.reshape(-1, w_out.shape[-1]), w_gate, w_up, w_down


def _local_step(x, target, w_in4, late, small, reduce_late=None):
    t, d = x.shape
    aw = w_in4.shape[2]
    sw = aw
    nh = aw // HEAD_DIM
    row = lambda v: v.reshape(1, -1)
    g_mix, g_ffn = row(small['g_mix']), row(small['g_ffn'])
    g_oa, g_os, b_glu = row(small['g_out_attn']), row(small['g_out_ssm']), row(small['b_glu'])
    qg2 = jnp.tile(row(small['q_gain']), (1, 2))
    kg2 = jnp.tile(row(small['k_gain']), (1, 2))

    (h,) = _ew("rms_mix", lambda xv, g: _rms(xv, g)[0], [('r', x), ('c', g_mix)], [('r', d, MXU_DTYPE)])
    z4 = _mm("in_proj", h, w_in4, contract='nn', b_mode='b', o_mode='b')
    bias_t = _bias_table(small['rpb'])
    if late[0] == 'halves':
        ya, gathered = _attn_fwd(z4, qg2, kg2, bias_t, comm=_GatherChips(late[1]), comm_arrs=late[1])
        w_glu, w_out, w_gate4, w_up4, w_down4 = _late_weights(gathered)
    else:
        ya, _ = _attn_fwd(z4, qg2, kg2, bias_t)
        w_glu, w_out, w_gate4, w_up4, w_down4 = late[1]
    ffs = w_gate4.shape[2]
    s5_params = tuple(small[n] for n in ('ssm_a_re', 'ssm_a_im', 'ssm_b_re', 'ssm_b_im', 'ssm_c_re', 'ssm_c_im',
                                         'ssm_log_step', 'ssm_d'))
    mats, mats_vjp = jax.vjp(_s5_mats, *s5_params)
    mats = tuple(m.astype(S5_DTYPE) for m in mats[:3]) + mats[3:]
    u2 = _to_groups(z4[3].astype(S5_DTYPE))
    ypre2, xin = _s5_fwd(u2, mats)
    ypre = _from_groups(ypre2, sw)
    (yb,) = _ew("gelu", _gelu, [('r', ypre)], [('r', sw, MXU_DTYPE)])
    a_glu = _mm("glu_proj", yb, w_glu, contract='nn')

    def mix_out(yav, ypv, av, bg, goa, gos):
        ys = _gelu(ypv) * _sigmoid(av + bg)
        return jnp.concatenate([_rms(yav, goa)[0], _rms(ys, gos)[0]], axis=1)
    (ycat,) = _ew("mix_out", mix_out, [('r', ya), ('r', ypre), ('r', a_glu), ('c', b_glu), ('c', g_oa), ('c', g_os)],
                  [('r', aw + sw, MXU_DTYPE)])
    x1 = _mm("out_proj", ycat, w_out, contract='nn', add=x)
    (h2,) = _ew("rms_ffn", lambda xv, g: _rms(xv, g)[0], [('r', x1), ('c', g_ffn)], [('r', d, MXU_DTYPE)])
    gate4 = _mm("ffn_gate", h2, w_gate4, contract='nn', b_mode='b', o_mode='b', tn=ffs)
    up4 = _mm("ffn_up", h2, w_up4, contract='nn', b_mode='b', o_mode='b', tn=ffs)
    gate_f, up_f = gate4.reshape(4 * t, ffs), up4.reshape(4 * t, ffs)
    (act,) = _ew("swiglu", lambda gv, uv: gv * _sigmoid(gv) * uv, [('r', gate_f), ('r', up_f)],
                 [('r', ffs, MXU_DTYPE)])
    act4 = act.reshape(4, t, ffs)
    x2 = _mm("ffn_down", act4, w_down4, contract='nn', a_mode='c', b_mode='c', add=x1, tk=ffs)

    def loss_fn(xv, tv):
        diff = xv - tv
        return diff * (1.0 / d), diff * diff
    dx2, sq = _ew("loss", loss_fn, [('r', x2), ('r', target)], [('r', d, F32), ('a', d)])

    dact4 = _mm("ffn_down_dx", dx2, w_down4, contract='nt', b_mode='b', o_mode='b', tn=ffs)
    d_w_down4 = _mm("ffn_down_dw", act4, dx2, contract='tn', a_mode='b', o_mode='b', tm=ffs)

    def swiglu_bwd(dav, gv, uv):
        s = _sigmoid(gv)
        return dav * uv * s * (1.0 + gv * (1.0 - s)), dav * gv * s
    dgate, dup = _ew("swiglu_bwd", swiglu_bwd, [('r', dact4.reshape(4 * t, ffs)), ('r', gate_f), ('r', up_f)],
                     [('r', ffs, MXU_DTYPE), ('r', ffs, MXU_DTYPE)])
    dgate4, dup4 = dgate.reshape(4, t, ffs), dup.reshape(4, t, ffs)
    dh2 = _mm("ffn_gate_dx", dgate4, w_gate4, contract='nt', a_mode='c', b_mode='c', tk=ffs)
    dh2 = _mm("ffn_up_dx", dup4, w_up4, contract='nt', a_mode='c', b_mode='c', add=dh2, tk=ffs)
    d_w_gate4 = _mm("ffn_gate_dw", h2, dgate4, contract='tn', b_mode='b', o_mode='b', tn=ffs)
    d_w_up4 = _mm("ffn_up_dw", h2, dup4, contract='tn', b_mode='b', o_mode='b', tn=ffs)

    def rms_res_bwd(xv, g, dyv, resv):
        dx, dg = _rms_bwd(xv, g, dyv)
        return resv + dx, dg
    dx1, d_g_ffn = _ew("rms_ffn_bwd", rms_res_bwd, [('r', x1), ('c', g_ffn), ('r', dh2), ('r', dx2)],
                       [('r', d, F32), ('a', d)])

    dycat = _mm("out_proj_dx", dx1, w_out, contract='nt')
    d_w_out = _mm("out_proj_dw", ycat, dx1, contract='tn')

    def mix_out_bwd(yav, ypv, av, bg, goa, gos, dca, dcs):
        dya, dgoa = _rms_bwd(yav, goa, dca)
        y = _gelu(ypv)
        s = _sigmoid(av + bg)
        dys, dgos = _rms_bwd(y * s, gos, dcs)
        da = dys * y * s * (1.0 - s)
        return dya, da, dys * s, dgoa, dgos, da
    dya, da, dy_direct, d_g_oa, d_g_os, d_b_glu = _ew(
        "mix_out_bwd", mix_out_bwd,
        [('r', ya), ('r', ypre), ('r', a_glu), ('c', b_glu), ('c', g_oa), ('c', g_os),
         ('r', dycat, 0, aw), ('r', dycat, 1, sw)],
        [('r', aw, F32), ('r', sw, MXU_DTYPE), ('r', sw, F32), ('a', aw), ('a', sw), ('a', sw)])
    dy = _mm("glu_proj_dx", da, w_glu, contract='nt', add=dy_direct)
    d_w_glu = _mm("glu_proj_dw", yb, da, contract='tn')
    (dypre,) = _ew("gelu_bwd", lambda dyv, ypv: dyv * _gelu_grad(ypv), [('r', dy), ('r', ypre)],
                   [('r', sw, S5_DTYPE)])

    du2, dmats = _s5_bwd(u2, xin, mats, _to_groups(dypre))
    d_s5 = mats_vjp(dmats)
    du = _from_groups(du2, sw)
    d_late = (d_w_glu, d_w_out, d_w_gate4, d_w_up4, d_w_down4)
    if reduce_late is not None:
        sums = reduce_late(d_late)
        (dz4, dbias_t, dqg, dkg), scattered = _attn_bwd(z4, qg2, kg2, bias_t, dya, comm=_ScatterChips(sums),
                                                       comm_arrs=sums)
        d_late = (sums, list(scattered))
    else:
        (dz4, dbias_t, dqg, dkg), _ = _attn_bwd(z4, qg2, kg2, bias_t, dya)
    d_rpb = _bias_table_grad(dbias_t)
    fold = lambda v: v.reshape(-1, 2, HEAD_DIM).sum(axis=(0, 1))
    dz4 = dz4.at[3].set(du)

    dh = _mm("in_proj_dx", dz4, w_in4, contract='nt', a_mode='c', b_mode='c')
    d_w_in4 = _mm("in_proj_dw", h, dz4, contract='tn', b_mode='b', o_mode='b')
    dx, d_g_mix = _ew("rms_mix_bwd", rms_res_bwd, [('r', x), ('c', g_mix), ('r', dh), ('r', dx1)],
                      [('r', d, F32), ('a', d)])

    colsum = lambda v: v.sum(axis=0)
    d_small = {
        'g_mix': colsum(d_g_mix), 'q_gain': fold(dqg), 'k_gain': fold(dkg), 'rpb': d_rpb,
        'ssm_a_re': d_s5[0], 'ssm_a_im': d_s5[1], 'ssm_b_re': d_s5[2], 'ssm_b_im': d_s5[3],
        'ssm_c_re': d_s5[4], 'ssm_c_im': d_s5[5], 'ssm_log_step': d_s5[6], 'ssm_d': d_s5[7],
        'b_glu': colsum(d_b_glu), 'g_out_attn': colsum(d_g_oa), 'g_out_ssm': colsum(d_g_os), 'g_ffn': colsum(d_g_ffn),
    }
    return jnp.sum(sq), dx, d_late, d_w_in4, d_small


ANY = pl.BlockSpec(memory_space=pl.ANY)


def _place():
    x, y, c = lax.axis_index("x"), lax.axis_index("y"), lax.axis_index("c")
    other_chips = [(1 - x, y), (x, 1 - y), (1 - x, 1 - y)]
    return x, y, c, 2 * x + y, (x, y, 1 - c), other_chips


class _GatherChips:
    KINDS = 7

    def __init__(self, arrs):
        self.n = len(arrs)
        self.out_shape = [jax.ShapeDtypeStruct((N_CHIPS,) + a.shape, a.dtype) for a in arrs]
        self.scratch = [pltpu.SemaphoreType.DMA((self.n, self.KINDS)), pltpu.SemaphoreType.DMA((self.n, self.KINDS))]

    def _copies(self, ins, outs, send_sems, recv_sems):
        x, y, c, me, sibling, chips = _place()

        def remote(a, k, src, dst, to):
            return lambda: pltpu.make_async_remote_copy(src_ref=src, dst_ref=dst, send_sem=send_sems.at[a, k],
                                                        recv_sem=recv_sems.at[a, k], device_id=to, device_id_type=MESH)
        own, out, landed, passed, theirs = [], [], [], [], []
        for a in range(self.n):
            own.append(remote(a, 6, ins[a], outs[a].at[me], sibling))
            for j, (px, py) in enumerate(chips):
                there, here = outs[a].at[2 * px + py, c], outs[a].at[2 * px + py, 1 - c]
                out.append(remote(a, j, ins[a].at[c], outs[a].at[me, c], (px, py, c)))
                landed.append(remote(a, j, there, there, (px, py, c)))
                passed.append(remote(a, 3 + j, there, there, sibling))
                theirs.append(remote(a, 3 + j, here, here, sibling))
        return own, out, landed, passed, theirs

    def start(self, ins, outs, send_sems, recv_sems):
        own, out, _, _, _ = self._copies(ins, outs, send_sems, recv_sems)
        for make in own + out:
            make().start()

    def finish(self, ins, outs, send_sems, recv_sems):
        own, out, landed, passed, theirs = self._copies(ins, outs, send_sems, recv_sems)
        for arrived, onward in zip(landed, passed):
            arrived().wait_recv()
            onward().start()
        for make in theirs + own:
            make().wait_recv()
        for make in own + out + passed:
            make().wait_send()


class _ScatterChips:
    def __init__(self, sums):
        self.n = len(sums)
        self.out_shape = [jax.ShapeDtypeStruct(s.shape, s.dtype) for s in sums]
        self.scratch = [pltpu.SemaphoreType.DMA((self.n, 3)), pltpu.SemaphoreType.DMA((self.n, 3))]

    def _copies(self, ins, outs, send_sems, recv_sems):
        x, y, c, me, sibling, chips = _place()
        out, landed = [], []

        def remote(a, j, src, dst, to):
            return lambda: pltpu.make_async_remote_copy(src_ref=src, dst_ref=dst, send_sem=send_sems.at[a, j],
                                                        recv_sem=recv_sems.at[a, j], device_id=to, device_id_type=MESH)
        for a in range(self.n):
            for j, (px, py) in enumerate(chips):
                slot = outs[a].at[2 * px + py]
                out.append(remote(a, j, ins[a].at[2 * px + py], outs[a].at[me], (px, py, c)))
                landed.append(remote(a, j, slot, slot, (px, py, c)))
        return out, landed

    def start(self, ins, outs, send_sems, recv_sems):
        for make in self._copies(ins, outs, send_sems, recv_sems)[0]:
            make().start()

    def finish(self, ins, outs, send_sems, recv_sems):
        out, landed = self._copies(ins, outs, send_sems, recv_sems)
        for make in landed:
            make().wait_recv()
        for make in out:
            make().wait_send()


def _comm_call(name, comm, arrs):
    n = comm.n

    def body(*refs):
        parts = (refs[:n], refs[n:2 * n]) + tuple(refs[2 * n:])
        comm.start(*parts)
        comm.finish(*parts)

    return pl.pallas_call(body, name=name, in_specs=[ANY] * n, out_specs=[ANY] * n, out_shape=comm.out_shape,
                          scratch_shapes=comm.scratch)(*arrs)


def _gather_chips(name, arrs):
    return _comm_call(name, _GatherChips(arrs), arrs)


def _swap_halves(name, parts):
    n = len(parts)

    def body(*refs):
        ins, outs = refs[:n], refs[n:2 * n]
        send_sems, recv_sems = refs[2 * n:]
        x, y, c, me, sibling, chips = _place()
        cps = []
        for a in range(n):
            cp = pltpu.make_async_remote_copy(src_ref=ins[a].at[:, 1 - c], dst_ref=outs[a], send_sem=send_sems.at[a],
                                              recv_sem=recv_sems.at[a], device_id=sibling, device_id_type=MESH)
            cp.start()
            cps.append(cp)
        for cp in cps:
            cp.wait()

    return pl.pallas_call(
        body, name=name, in_specs=[ANY] * n, out_specs=[ANY] * n,
        out_shape=[jax.ShapeDtypeStruct((N_CHIPS,) + p.shape[2:], p.dtype) for p in parts],
        scratch_shapes=[pltpu.SemaphoreType.DMA((n,)), pltpu.SemaphoreType.DMA((n,))],
    )(*parts)


def _scatter_chips(name, sums):
    return _comm_call(name, _ScatterChips(sums), sums)


def _swap_reduced(name, halves):
    n = len(halves)

    def body(*refs):
        ins, outs = refs[:n], refs[n:2 * n]
        send_sems, recv_sems = refs[2 * n:]
        x, y, c, me, sibling, chips = _place()
        cps = []
        for a in range(n):
            cp = pltpu.make_async_remote_copy(src_ref=ins[a], dst_ref=outs[a], send_sem=send_sems.at[a],
                                              recv_sem=recv_sems.at[a], device_id=sibling, device_id_type=MESH)
            cp.start()
            cps.append(cp)
        for cp in cps:
            cp.wait()

    return pl.pallas_call(
        body, name=name, in_specs=[ANY] * n, out_specs=[ANY] * n,
        out_shape=[jax.ShapeDtypeStruct(h.shape, h.dtype) for h in halves],
        scratch_shapes=[pltpu.SemaphoreType.DMA((n,)), pltpu.SemaphoreType.DMA((n,))],
    )(*halves)


def _row_tile(r, want=256):
    t = (min(r, want) // SUBLANES) * SUBLANES
    while r % t:
        t -= SUBLANES
    return t


def _add_own_half(name, part, got, c, out_dtype):
    _, _, r, cols = part.shape
    tr = _row_tile(r)

    def body(c_ref, p_ref, g_ref, o_ref):
        o_ref[...] = (p_ref[...] + g_ref[...]).astype(o_ref.dtype)

    return pl.pallas_call(
        body, name=name,
        grid_spec=pltpu.PrefetchScalarGridSpec(
            num_scalar_prefetch=1, grid=(N_CHIPS, r // tr),
            in_specs=[pl.BlockSpec((None, None, tr, cols), lambda s, i, c_ref: (s, c_ref[0], i, 0)),
                      pl.BlockSpec((None, tr, cols), lambda s, i, c_ref: (s, i, 0))],
            out_specs=pl.BlockSpec((None, tr, cols), lambda s, i, c_ref: (s, i, 0))),
        out_shape=jax.ShapeDtypeStruct(got.shape, out_dtype),
        compiler_params=_cparams(("parallel", "parallel")),
    )(c.reshape(1).astype(jnp.int32), part, got)


def _sum_chips(name, got, own, me):
    _, r, cols = got.shape
    tr = _row_tile(r)

    def body(me_ref, r0, r1, r2, r3, own_ref, o_ref):
        pick = lambda s, ref: jnp.where(me_ref[0] == s, own_ref[...], ref[...]).astype(F32)
        o_ref[...] = ((pick(0, r0) + pick(1, r1)) + pick(2, r2)) + pick(3, r3)

    def slot(s):
        return pl.BlockSpec((None, tr, cols),
                            lambda i, me_ref: (jnp.where(me_ref[0] == s, (s + 1) % N_CHIPS, s), i, 0))

    return pl.pallas_call(
        body, name=name,
        grid_spec=pltpu.PrefetchScalarGridSpec(
            num_scalar_prefetch=1, grid=(r // tr,),
            in_specs=[slot(s) for s in range(N_CHIPS)]
            + [pl.BlockSpec((None, tr, cols), lambda i, me_ref: (me_ref[0], i, 0))],
            out_specs=pl.BlockSpec((tr, cols), lambda i, me_ref: (i, 0))),
        out_shape=jax.ShapeDtypeStruct((r, cols), F32),
        compiler_params=_cparams(("parallel",)),
    )(me.reshape(1).astype(jnp.int32), got, got, got, got, own)


def _adamw_math(wv, gv, mv, vv):
    mv = ADAM_B1 * mv + (1.0 - ADAM_B1) * gv
    vv = ADAM_B2 * vv + (1.0 - ADAM_B2) * (gv * gv)
    m_hat = mv / (1.0 - ADAM_B1 ** ADAM_STEP)
    v_hat = vv / (1.0 - ADAM_B2 ** ADAM_STEP)
    return -ADAM_LR * (m_hat / (jnp.sqrt(v_hat) + ADAM_EPS) + ADAM_WD * wv), mv, vv


def _adamw(name, w, g, m, v):
    cols = w.shape[1]
    return _ew(name, _adamw_math, [('r', w), ('r', g), ('r', m), ('r', v)], [('r', cols, F32)] * 3,
               tr=_row_tile(w.shape[0], 128))


def _adamw_halves(name, w, mine, theirs, m, v, c):
    r, cols = mine.shape
    tr = _row_tile(r, 128)
    nb = r // tr

    def body(c_ref, w_ref, a_ref, b_ref, m_ref, v_ref, g_out, d_out, m_out, v_out):
        g = jnp.where(pl.program_id(0) == c_ref[0], a_ref[...], b_ref[...])
        g_out[...] = g
        d_out[...], m_out[...], v_out[...] = _adamw_math(w_ref[...], g, m_ref[...], v_ref[...])

    whole = pl.BlockSpec((tr, cols), lambda h, i, c_ref: (h * nb + i, 0))
    half = pl.BlockSpec((tr, cols), lambda h, i, c_ref: (i, 0))
    return pl.pallas_call(
        body, name=name,
        grid_spec=pltpu.PrefetchScalarGridSpec(
            num_scalar_prefetch=1, grid=(2, nb),
            in_specs=[whole, half, half, whole, whole], out_specs=[whole] * 4),
        out_shape=[jax.ShapeDtypeStruct(w.shape, F32)] * 4,
        compiler_params=_cparams(("parallel", "parallel")),
    )(c.reshape(1).astype(jnp.int32), w, mine, theirs, m, v)


SMALL_ROWS_ALIGN = 2 * N_CHIPS * SUBLANES


MEDIUM_NAMES = ['ssm_b_re', 'ssm_b_im', 'ssm_c_re', 'ssm_c_im']
PACKED_NAMES = [n for n in SMALL_NAMES if n not in MEDIUM_NAMES]


def _pack_small(d):
    flat = jnp.concatenate([d[n].reshape(-1).astype(F32) for n in PACKED_NAMES])
    rows = -(-flat.shape[0] // (LANES * SMALL_ROWS_ALIGN)) * SMALL_ROWS_ALIGN
    return jnp.pad(flat, (0, rows * LANES - flat.shape[0])).reshape(rows, LANES)


def _unpack_small(packed, like):
    flat = packed.reshape(-1)
    out, off = {}, 0
    for n in PACKED_NAMES:
        size = like[n].size
        out[n] = flat[off:off + size].reshape(like[n].shape)
        off += size
    return out


def kernel(x, g_mix, w_in, q_gain, k_gain, rpb, ssm_a_re, ssm_a_im, ssm_b_re, ssm_b_im, ssm_c_re, ssm_c_im, ssm_log_step, ssm_d, w_glu, b_glu, g_out_attn, g_out_ssm, w_out, g_ffn, w_ffn_gate, w_ffn_up, w_ffn_down, loss_target, m_g_mix, m_w_in, m_q_gain, m_k_gain, m_rpb, m_ssm_a_re, m_ssm_a_im, m_ssm_b_re, m_ssm_b_im, m_ssm_c_re, m_ssm_c_im, m_ssm_log_step, m_ssm_d, m_w_glu, m_b_glu, m_g_out_attn, m_g_out_ssm, m_w_out, m_g_ffn, m_w_ffn_gate, m_w_ffn_up, m_w_ffn_down, v_g_mix, v_w_in, v_q_gain, v_k_gain, v_rpb, v_ssm_a_re, v_ssm_a_im, v_ssm_b_re, v_ssm_b_im, v_ssm_c_re, v_ssm_c_im, v_ssm_log_step, v_ssm_d, v_w_glu, v_b_glu, v_g_out_attn, v_g_out_ssm, v_w_out, v_g_ffn, v_w_ffn_gate, v_w_ffn_up, v_w_ffn_down):
    given = dict(locals())
    w = {n: given[n][0] for n in WEIGHT_NAMES}
    mom = {n: given["m_" + n][0] for n in WEIGHT_NAMES}
    var = {n: given["v_" + n][0] for n in WEIGHT_NAMES}
    d = x.shape[-1]
    c = lax.axis_index("c")

    halves = {n: w[n].astype(MXU_DTYPE).reshape((2, w[n].shape[0] // 2, w[n].shape[1])) for n in BIG_NAMES}
    (w_in4,) = _gather_chips("gather_w_in", [halves['w_in']])
    w_in4 = w_in4.reshape((N_CHIPS, -1, w_in4.shape[-1]))

    def chip_sums(tag, grads, payload):
        parts = [g.reshape((N_CHIPS, 2, -1, g.shape[-1])) for g in grads]
        got = _swap_halves("reduce_swap_halves_" + tag, parts)
        return [_add_own_half("reduce_add_%s_%d" % (tag, a), p, gt, c, dt)
                for a, (p, gt, dt) in enumerate(zip(parts, got, payload))]

    reduce_late = lambda grads: chip_sums("late", grads, [GRAD_PAYLOAD_DTYPE] * len(grads))
    sq, dx, (sums_late, got_late), d_w_in4, d_small = _local_step(
        x[0], loss_target[0], w_in4, ('halves', [halves[n] for n in LATE_NAMES]), {n: w[n] for n in SMALL_NAMES},
        reduce_late)
    loss = lax.psum(0.5 * sq / d, ("x", "y", "c"))

    nbig = len(BIG_NAMES)
    rest = [d_w_in4] + [d_small[n].reshape(-1, LANES) for n in MEDIUM_NAMES] + [_pack_small(d_small)]
    sums_rest = chip_sums("rest", rest, [GRAD_PAYLOAD_DTYPE] + [F32] * (len(rest) - 1))
    got_rest = list(_scatter_chips("reduce_scatter_chips", sums_rest))
    sums = sums_rest[:1] + sums_late + sums_rest[1:]
    got = got_rest[:1] + got_late + got_rest[1:]
    me = 2 * lax.axis_index("x") + lax.axis_index("y")
    mine = [_sum_chips("reduce_sum_%d" % a, gt, sm_, me) for a, (gt, sm_) in enumerate(zip(got, sums))]
    theirs = _swap_reduced("reduce_swap_reduced", mine)
    in_order = lambda a: jnp.where(c == 0, jnp.stack([mine[a], theirs[a]]), jnp.stack([theirs[a], mine[a]]))
    repl = _gather_chips("gather_small", [in_order(a) for a in range(nbig, len(mine))])
    repl = [r.reshape(-1, LANES) for r in repl]
    like = {n: w[n] for n in SMALL_NAMES}
    grad_small = _unpack_small(repl[-1], like)
    grad_small.update({n: r.reshape(w[n].shape) for n, r in zip(MEDIUM_NAMES, repl)})

    grad_big, delta, new_m, new_v = {}, {}, {}, {}
    for a, n in enumerate(BIG_NAMES):
        grad_big[n], delta[n], new_m[n], new_v[n] = _adamw_halves("adamw_%d" % a, w[n], mine[a], theirs[a],
                                                                  mom[n], var[n], c)
    for n, r in zip(MEDIUM_NAMES, repl):
        res = _adamw("adamw_" + n, w[n].reshape(-1, LANES), r, mom[n].reshape(-1, LANES), var[n].reshape(-1, LANES))
        delta[n], new_m[n], new_v[n] = (t.reshape(w[n].shape) for t in res)
    sd, sm, sv = _adamw("adamw_small", _pack_small(w), repl[-1], _pack_small(mom), _pack_small(var))
    delta.update(_unpack_small(sd, like))
    new_m.update(_unpack_small(sm, like))
    new_v.update(_unpack_small(sv, like))
    grads = {**grad_big, **grad_small}
    lead = lambda t: t[None]
    return (loss, dx[None], *[lead(grads[n]) for n in WEIGHT_NAMES], *[lead(delta[n]) for n in WEIGHT_NAMES],
            *[lead(new_m[n]) for n in WEIGHT_NAMES], *[lead(new_v[n]) for n in WEIGHT_NAMES])
```

```python
import functools
import math

import jax
import jax.numpy as jnp
from jax import lax
from jax.experimental import pallas as pl
from jax.experimental.pallas import tpu as pltpu

F32 = jnp.float32
BF16 = jnp.bfloat16
MXU_DTYPE = BF16
GRAD_PAYLOAD_DTYPE = BF16
DW_DTYPE = BF16
S5_DTYPE = BF16
HI = lax.Precision.HIGHEST
VMEM_LIMIT_V7X = 56 * 1024 * 1024
LANES = 128
SUBLANES = 8

GRID_W = 64
WIN_H = 8
WIN_W = 16
HEAD_DIM = 64
SSM_GROUP_CH = 16
SSM_STATE = 64
S5_CHUNK = 16
S5_GROUPS_PER_STEP = 8
RMS_EPS = 1e-6
NEG_INF = -1e30
N_CHIPS = 4
MESH = pl.DeviceIdType.MESH

ADAM_LR = 0.001
ADAM_B1 = 0.9
ADAM_B2 = 0.999
ADAM_EPS = 1e-08
ADAM_WD = 0.01
ADAM_STEP = 10

WEIGHT_NAMES = ['g_mix', 'w_in', 'q_gain', 'k_gain', 'rpb', 'ssm_a_re', 'ssm_a_im', 'ssm_b_re', 'ssm_b_im',
                'ssm_c_re', 'ssm_c_im', 'ssm_log_step', 'ssm_d', 'w_glu', 'b_glu', 'g_out_attn', 'g_out_ssm',
                'w_out', 'g_ffn', 'w_ffn_gate', 'w_ffn_up', 'w_ffn_down']
BIG_NAMES = ['w_in', 'w_glu', 'w_out', 'w_ffn_gate', 'w_ffn_up', 'w_ffn_down']
LATE_NAMES = BIG_NAMES[1:]
SMALL_NAMES = [n for n in WEIGHT_NAMES if n not in BIG_NAMES]


def _cparams(sem):
    return pltpu.CompilerParams(dimension_semantics=sem, vmem_limit_bytes=VMEM_LIMIT_V7X)


def _tile(n, want):
    if n <= want:
        return n
    t = (want // LANES) * LANES
    while t >= LANES:
        if n % t == 0:
            return t
        t -= LANES
    return n


def _mm(name, a, b, *, contract, a_mode='2', b_mode='2', o_mode='2', out_dtype=F32, add=None, exact=False,
        tm=1024, tn=1024, tk=512):
    dn = {'nn': (((1,), (0,)), ((), ())), 'nt': (((1,), (1,)), ((), ())), 'tn': (((0,), (0,)), ((), ()))}[contract]
    ar, ac = a.shape[-2:]
    br, bc = b.shape[-2:]
    m, kdim = (ar, ac) if contract != 'tn' else (ac, ar)
    n = bc if contract != 'nt' else br
    assert kdim == (br if contract != 'nt' else bc), (name, a.shape, b.shape)
    nbatch = 1
    for arr, mode in ((a, a_mode), (b, b_mode)):
        if mode == 'b':
            nbatch = arr.shape[0]
    nstack = 1
    for arr, mode in ((a, a_mode), (b, b_mode)):
        if mode == 'c':
            nstack = arr.shape[0]
    tm, tn, tk = _tile(m, tm), _tile(n, tn), _tile(kdim, tk)
    nkin = kdim // tk
    nk = nstack * nkin
    grid = (nbatch, m // tm, n // tn, nk)

    def spec(mode, block, rc):
        def imap(s, i, j, kk):
            r, c = rc(i, j, kk % nkin)
            if mode == '2':
                return (r, c)
            return (s if mode == 'b' else kk // nkin, r, c)
        return pl.BlockSpec(block if mode == '2' else (None,) + block, imap)

    a_spec = spec(a_mode, (tm, tk) if contract != 'tn' else (tk, tm),
                  (lambda i, j, k: (i, k)) if contract != 'tn' else (lambda i, j, k: (k, i)))
    b_spec = spec(b_mode, (tk, tn) if contract != 'nt' else (tn, tk),
                  (lambda i, j, k: (k, j)) if contract != 'nt' else (lambda i, j, k: (j, k)))
    o_spec = spec(o_mode, (tm, tn), lambda i, j, k: (i, j))
    out_shape = (m, n) if o_mode == '2' else (nbatch, m, n)
    has_add = add is not None

    def body(*refs):
        if has_add:
            a_ref, b_ref, add_ref, o_ref, acc_ref = refs
        else:
            a_ref, b_ref, o_ref, acc_ref = refs
        k = pl.program_id(3)

        @pl.when(k == 0)
        def _():
            acc_ref[...] = jnp.zeros_like(acc_ref)

        if exact:
            acc_ref[...] += lax.dot_general(a_ref[...].astype(F32), b_ref[...].astype(F32), dn, precision=HI,
                                            preferred_element_type=F32)
        else:
            acc_ref[...] += lax.dot_general(a_ref[...].astype(MXU_DTYPE), b_ref[...].astype(MXU_DTYPE), dn,
                                            preferred_element_type=F32)

        @pl.when(k == nk - 1)
        def _():
            r = acc_ref[...]
            if has_add:
                r = r + add_ref[...].astype(F32)
            o_ref[...] = r.astype(o_ref.dtype)

    in_specs = [a_spec, b_spec] + ([o_spec] if has_add else [])
    args = (a, b) + ((add,) if has_add else ())
    return pl.pallas_call(
        body, name=name, grid=grid, in_specs=in_specs, out_specs=o_spec,
        out_shape=jax.ShapeDtypeStruct(out_shape, out_dtype),
        scratch_shapes=[pltpu.VMEM((tm, tn), F32)],
        compiler_params=_cparams(("parallel", "parallel", "parallel", "arbitrary")),
    )(*args)


def _ew(name, fn, ins, outs, tr=256):
    rows = next(x[1].shape[0] for x in ins if x[0] == 'r')
    tr = min(tr, rows)
    assert rows % tr == 0 and tr % SUBLANES == 0, (name, rows, tr)
    in_specs, args = [], []
    for x in ins:
        if x[0] == 'r' and len(x) == 2:
            in_specs.append(pl.BlockSpec((tr, x[1].shape[1]), lambda i: (i, 0)))
        elif x[0] == 'r':
            in_specs.append(pl.BlockSpec((tr, x[3]), functools.partial(lambda cb, i: (i, cb), x[2])))
        else:
            in_specs.append(pl.BlockSpec(x[1].shape, lambda i: (0, 0)))
        args.append(x[1])
    out_specs, out_shapes = [], []
    for o in outs:
        if o[0] == 'r':
            out_specs.append(pl.BlockSpec((tr, o[1]), lambda i: (i, 0)))
            out_shapes.append(jax.ShapeDtypeStruct((rows, o[1]), o[2]))
        else:
            out_specs.append(pl.BlockSpec((SUBLANES, o[1]), lambda i: (0, 0)))
            out_shapes.append(jax.ShapeDtypeStruct((SUBLANES, o[1]), F32))
    nin = len(ins)
    has_acc = any(o[0] == 'a' for o in outs)

    def body(*refs):
        vals = fn(*[r[...].astype(F32) for r in refs[:nin]])
        if not isinstance(vals, (tuple, list)):
            vals = (vals,)
        i = pl.program_id(0)
        for o, ref, v in zip(outs, refs[nin:], vals):
            if o[0] == 'r':
                ref[...] = v.astype(ref.dtype)
            else:
                part = v.astype(F32).reshape(tr // SUBLANES, SUBLANES, o[1]).sum(axis=0)

                @pl.when(i == 0)
                def _(ref=ref, part=part):
                    ref[...] = part

                @pl.when(i > 0)
                def _(ref=ref, part=part):
                    ref[...] += part

    res = pl.pallas_call(
        body, name=name, grid=(rows // tr,), in_specs=in_specs, out_specs=out_specs, out_shape=out_shapes,
        compiler_params=_cparams(("arbitrary",) if has_acc else ("parallel",)),
    )(*args)
    return res


def _rms(x, g):
    r = lax.rsqrt(jnp.mean(x * x, axis=-1, keepdims=True) + RMS_EPS)
    xr = x * r
    return xr * g, xr


def _rms_bwd(x, g, dy):
    r = lax.rsqrt(jnp.mean(x * x, axis=-1, keepdims=True) + RMS_EPS)
    xr = x * r
    gdy = g * dy
    dx = r * (gdy - xr * jnp.mean(xr * gdy, axis=-1, keepdims=True))
    return dx, dy * xr


def _sigmoid(x):
    return 1.0 / (1.0 + jnp.exp(-x))


_GELU_C = math.sqrt(2.0 / math.pi)


def _gelu(x):
    return 0.5 * x * (1.0 + jnp.tanh(_GELU_C * (x + 0.044715 * x * x * x)))


def _gelu_grad(x):
    t = jnp.tanh(_GELU_C * (x + 0.044715 * x * x * x))
    return 0.5 * (1.0 + t) + 0.5 * x * (1.0 - t * t) * _GELU_C * (1.0 + 3 * 0.044715 * x * x)


ATTN_ROWS_PER_STEP = 8
NT_DIMS = (((1,), (1,)), ((), ()))
NN_DIMS = (((1,), (0,)), ((), ()))
TN_DIMS = (((0,), (0,)), ((), ()))


def _attn_geometry(r, rows):
    row_start = jnp.clip(r - WIN_H // 2, 0, rows - WIN_H)
    key0 = pl.multiple_of(row_start * GRID_W, GRID_W)
    bias0 = pl.multiple_of((row_start - r + (WIN_H - 1)) * GRID_W, GRID_W)
    return key0, bias0


def _window_onehot():
    c = jnp.arange(GRID_W)
    col_start = jnp.clip(c - WIN_W // 2, 0, GRID_W - WIN_W)
    col_in = (c[None, :] >= col_start[:, None]) & (c[None, :] < col_start[:, None] + WIN_W)
    dc = jnp.clip(c[None, :] - c[:, None], -(WIN_W - 1), WIN_W - 1) + (WIN_W - 1)
    onehot = ((dc[:, :, None] == jnp.arange(2 * WIN_W - 1)[None, None, :]) & col_in[:, :, None]).astype(F32)
    return onehot, col_in


def _bias_table(rpb):
    onehot, col_in = _window_onehot()
    nh = rpb.shape[0]
    tab = jnp.einsum('perd,qkd->prkeq', rpb.reshape(nh // 2, 2, 2 * WIN_H - 1, 2 * WIN_W - 1), onehot, precision=HI)
    tab = tab + jnp.where(col_in, 0.0, NEG_INF).T[None, None, :, None, :]
    return tab.reshape(nh // 2, (2 * WIN_H - 1) * GRID_W, 2 * GRID_W)


def _bias_table_grad(dtab):
    onehot, _ = _window_onehot()
    npair = dtab.shape[0]
    d = dtab.reshape(npair, 2 * WIN_H - 1, GRID_W, 2, GRID_W)
    return jnp.einsum('prkeq,qkd->perd', d, onehot, precision=HI).reshape(2 * npair, 2 * WIN_H - 1, 2 * WIN_W - 1)


def _lane_lo(shape):
    return lax.broadcasted_iota(jnp.int32, shape, 1) < HEAD_DIM


def _half_sums(v):
    lo = _lane_lo(v.shape)
    s_lo = jnp.sum(jnp.where(lo, v, 0.0), axis=1, keepdims=True)
    s_hi = jnp.sum(jnp.where(lo, 0.0, v), axis=1, keepdims=True)
    return jnp.where(lo, s_lo, s_hi)


def _rms_pair(x, g):
    r = lax.rsqrt(_half_sums(x * x) * (1.0 / HEAD_DIM) + RMS_EPS)
    return x * r * g


def _rms_pair_bwd(x, g, dy):
    r = lax.rsqrt(_half_sums(x * x) * (1.0 / HEAD_DIM) + RMS_EPS)
    xr = x * r
    gdy = g * dy
    dx = r * (gdy - xr * (_half_sums(xr * gdy) * (1.0 / HEAD_DIM)))
    return dx, dy * xr


def _blockdiag(a):
    a2 = jnp.concatenate([a, a], axis=0)
    row_hi = lax.broadcasted_iota(jnp.int32, a2.shape, 0) >= GRID_W
    lane_hi = lax.broadcasted_iota(jnp.int32, a2.shape, 1) >= HEAD_DIM
    return jnp.where(row_hi == lane_hi, a2, 0.0).astype(MXU_DTYPE)


def _diag_blocks(m):
    return jnp.where(_lane_lo((GRID_W, 2 * HEAD_DIM)), m[:GRID_W], m[GRID_W:])


def _attn_scores(qb, kb, bias):
    st = lax.dot_general(kb, qb, NT_DIMS, preferred_element_type=F32)
    st = st * (1.0 / math.sqrt(HEAD_DIM)) + bias
    mx = jnp.max(st, axis=0, keepdims=True)
    p = jnp.exp(st - mx)
    return p * (1.0 / jnp.sum(p, axis=0, keepdims=True))


def _attn_fwd(z4, qg2, kg2, bias_t, comm=None, comm_arrs=()):
    _, t, aw = z4.shape
    rows = t // GRID_W
    npair = aw // (2 * HEAD_DIM)
    nkeys = WIN_H * GRID_W
    nb = bias_t.shape[1]
    rps = min(ATTN_ROWS_PER_STEP, rows)
    blk = rps * GRID_W
    nsteps = rows // rps
    ncomm = len(comm_arrs)

    def body(*refs):
        q_ref, k_ref, v_ref, qg_ref, kg_ref, b_ref = refs[:6]
        c_ins, o_ref, c_outs = refs[6:6 + ncomm], refs[6 + ncomm], refs[7 + ncomm:7 + 2 * ncomm]
        kn_ref, vb_ref = refs[7 + 2 * ncomm:9 + 2 * ncomm]
        sems = refs[9 + 2 * ncomm:]
        pr, rb = pl.program_id(0), pl.program_id(1)
        if comm is not None:
            @pl.when((pr == 0) & (rb == 0))
            def _():
                comm.start(c_ins, c_outs, *sems)

        @pl.when(rb == 0)
        def _():
            kn_ref[...] = _rms_pair(k_ref[...], kg_ref[...]).astype(MXU_DTYPE)
            vb_ref[...] = v_ref[...].astype(MXU_DTYPE)

        def row(i, carry):
            key0, bias0 = _attn_geometry(rb * rps + i, rows)
            at = pl.ds(pl.multiple_of(i * GRID_W, GRID_W), GRID_W)
            qb = _blockdiag(_rms_pair(q_ref[at, :], qg_ref[...]))
            pt = _attn_scores(qb, kn_ref[pl.ds(key0, nkeys), :], b_ref[pl.ds(bias0, nkeys), :])
            both = lax.dot_general(pt.astype(MXU_DTYPE), vb_ref[pl.ds(key0, nkeys), :], TN_DIMS,
                                   preferred_element_type=F32)
            o_ref[at, :] = _diag_blocks(both)
            return carry

        lax.fori_loop(0, rps, row, 0, unroll=2)
        if comm is not None:
            @pl.when((pr == npair - 1) & (rb == nsteps - 1))
            def _():
                comm.finish(c_ins, c_outs, *sems)

    pair_cols = lambda lead: pl.BlockSpec((None, t, 2 * HEAD_DIM), lambda p, r: (lead, 0, p))
    res = pl.pallas_call(
        body, name="attn_fwd", grid=(npair, nsteps),
        in_specs=[pl.BlockSpec((None, blk, 2 * HEAD_DIM), lambda p, r: (0, r, p)), pair_cols(1), pair_cols(2),
                  pl.BlockSpec((1, 2 * HEAD_DIM), lambda p, r: (0, 0)),
                  pl.BlockSpec((1, 2 * HEAD_DIM), lambda p, r: (0, 0)),
                  pl.BlockSpec((None, nb, 2 * GRID_W), lambda p, r: (p, 0, 0))] + [ANY] * ncomm,
        out_specs=[pl.BlockSpec((blk, 2 * HEAD_DIM), lambda p, r: (r, p))] + [ANY] * ncomm,
        out_shape=[jax.ShapeDtypeStruct((t, aw), F32)] + (comm.out_shape if comm is not None else []),
        scratch_shapes=[pltpu.VMEM((t, 2 * HEAD_DIM), MXU_DTYPE), pltpu.VMEM((t, 2 * HEAD_DIM), MXU_DTYPE)]
        + (comm.scratch if comm is not None else []),
        compiler_params=_cparams(("arbitrary", "arbitrary")),
    )(z4, z4, z4, qg2, kg2, bias_t, *comm_arrs)
    return res[0], res[1:]


def _attn_bwd(z4, qg2, kg2, bias_t, dya, comm=None, comm_arrs=()):
    _, t, aw = z4.shape
    rows = t // GRID_W
    npair = aw // (2 * HEAD_DIM)
    nkeys = WIN_H * GRID_W
    nb = bias_t.shape[1]
    rps = min(ATTN_ROWS_PER_STEP, rows)
    blk = rps * GRID_W
    nsteps = rows // rps
    scale = 1.0 / math.sqrt(HEAD_DIM)
    ncomm = len(comm_arrs)

    def body(*refs):
        q_ref, k_ref, v_ref, qg_ref, kg_ref, b_ref, do_ref = refs[:7]
        c_ins = refs[7:7 + ncomm]
        dz_ref, db_ref, dqg_ref, dkg_ref = refs[7 + ncomm:11 + ncomm]
        c_outs = refs[11 + ncomm:11 + 2 * ncomm]
        kn_ref, vb_ref, dkn_ref, dv_ref = refs[11 + 2 * ncomm:15 + 2 * ncomm]
        sems = refs[15 + 2 * ncomm:]
        pr, rb = pl.program_id(0), pl.program_id(1)
        if comm is not None:
            @pl.when((pr == 0) & (rb == 0))
            def _():
                comm.start(c_ins, c_outs, *sems)

        @pl.when(rb == 0)
        def _():
            kn_ref[...] = _rms_pair(k_ref[...], kg_ref[...]).astype(MXU_DTYPE)
            vb_ref[...] = v_ref[...].astype(MXU_DTYPE)
            dkn_ref[...] = jnp.zeros_like(dkn_ref)
            dv_ref[...] = jnp.zeros_like(dv_ref)
            db_ref[...] = jnp.zeros_like(db_ref)
            dqg_ref[...] = jnp.zeros_like(dqg_ref)

        def row(i, dqg_sum):
            r = rb * rps + i
            key0, bias0 = _attn_geometry(r, rows)
            keys = pl.ds(key0, nkeys)
            at = pl.ds(pl.multiple_of(i * GRID_W, GRID_W), GRID_W)
            q = q_ref[at, :]
            qb = _blockdiag(_rms_pair(q, qg_ref[...]))
            dob = _blockdiag(do_ref[at, :])
            kb = kn_ref[keys, :]
            pt = _attn_scores(qb, kb, b_ref[pl.ds(bias0, nkeys), :])
            dv_ref[keys, :] += lax.dot_general(pt.astype(MXU_DTYPE), dob, NN_DIMS, preferred_element_type=F32)
            dpt = lax.dot_general(vb_ref[keys, :], dob, NT_DIMS, preferred_element_type=F32)
            dst = pt * (dpt - jnp.sum(pt * dpt, axis=0, keepdims=True))
            db_ref[pl.ds(bias0, nkeys), :] += dst
            dsb = dst.astype(MXU_DTYPE)
            dkn_ref[keys, :] += scale * lax.dot_general(dsb, qb, NN_DIMS, preferred_element_type=F32)
            dqn = scale * _diag_blocks(lax.dot_general(dsb, kb, TN_DIMS, preferred_element_type=F32))
            dq, dqg = _rms_pair_bwd(q, qg_ref[...], dqn)
            dz_ref[0, pl.ds(pl.multiple_of(r * GRID_W, GRID_W), GRID_W), :] = dq.astype(dz_ref.dtype)
            return dqg_sum + jnp.sum(dqg, axis=0, keepdims=True)

        dqg_ref[...] += lax.fori_loop(0, rps, row, jnp.zeros((1, 2 * HEAD_DIM), F32), unroll=2)

        @pl.when(rb == nsteps - 1)
        def _():
            dk, dkg = _rms_pair_bwd(k_ref[...], kg_ref[...], dkn_ref[...])
            dz_ref[1] = dk.astype(dz_ref.dtype)
            dz_ref[2] = dv_ref[...].astype(dz_ref.dtype)
            dkg_ref[...] = jnp.sum(dkg, axis=0, keepdims=True)

        if comm is not None:
            @pl.when((pr == npair - 1) & (rb == nsteps - 1))
            def _():
                comm.finish(c_ins, c_outs, *sems)

    pair_cols = lambda lead: pl.BlockSpec((None, t, 2 * HEAD_DIM), lambda p, r: (lead, 0, p))
    pair_vec = pl.BlockSpec((None, 1, 2 * HEAD_DIM), lambda p, r: (p, 0, 0))
    res = pl.pallas_call(
        body, name="attn_bwd", grid=(npair, nsteps),
        in_specs=[pl.BlockSpec((None, blk, 2 * HEAD_DIM), lambda p, r: (0, r, p)), pair_cols(1), pair_cols(2),
                  pl.BlockSpec((1, 2 * HEAD_DIM), lambda p, r: (0, 0)),
                  pl.BlockSpec((1, 2 * HEAD_DIM), lambda p, r: (0, 0)),
                  pl.BlockSpec((None, nb, 2 * GRID_W), lambda p, r: (p, 0, 0)),
                  pl.BlockSpec((blk, 2 * HEAD_DIM), lambda p, r: (r, p))] + [ANY] * ncomm,
        out_specs=[pl.BlockSpec((3, t, 2 * HEAD_DIM), lambda p, r: (0, 0, p)),
                   pl.BlockSpec((None, nb, 2 * GRID_W), lambda p, r: (p, 0, 0)),
                   pair_vec, pair_vec] + [ANY] * ncomm,
        out_shape=[jax.ShapeDtypeStruct((4, t, aw), MXU_DTYPE), jax.ShapeDtypeStruct(bias_t.shape, F32),
                   jax.ShapeDtypeStruct((npair, 1, 2 * HEAD_DIM), F32),
                   jax.ShapeDtypeStruct((npair, 1, 2 * HEAD_DIM), F32)] + (comm.out_shape if comm is not None else []),
        scratch_shapes=[pltpu.VMEM((t, 2 * HEAD_DIM), MXU_DTYPE), pltpu.VMEM((t, 2 * HEAD_DIM), MXU_DTYPE),
                        pltpu.VMEM((t, 2 * HEAD_DIM), F32), pltpu.VMEM((t, 2 * HEAD_DIM), F32)]
        + (comm.scratch if comm is not None else []),
        compiler_params=_cparams(("arbitrary", "arbitrary")),
    )(z4, z4, z4, qg2, kg2, bias_t, dya, *comm_arrs)
    return res[:4], res[4:]


def _s5_mats(a_re, a_im, b_re, b_im, c_re, c_im, log_step, d_skip):
    nd, g, p = a_re.shape
    c = b_re.shape[-1]
    L = S5_CHUNK
    lr = jnp.minimum(a_re, -1e-4).transpose(1, 0, 2)
    li = a_im.transpose(1, 0, 2)
    dt = jnp.exp(log_step).T[..., None]
    n = jnp.arange(L + 1, dtype=F32)[None, :, None, None]
    mag = jnp.exp(n * (lr * dt)[:, None])
    ang = n * (li * dt)[:, None]
    pw_r, pw_i = mag * jnp.cos(ang), mag * jnp.sin(ang)
    den = lr * lr + li * li
    nr, ni = pw_r[:, 1] - 1.0, pw_i[:, 1]
    cr, ci = (nr * lr + ni * li) / den, (ni * lr - nr * li) / den
    bt_r, bt_i = b_re.transpose(1, 3, 0, 2), b_im.transpose(1, 3, 0, 2)
    bb_r = cr[:, None] * bt_r - ci[:, None] * bt_i
    bb_i = cr[:, None] * bt_i + ci[:, None] * bt_r
    ct_r, ct_i = c_re.transpose(1, 2, 0, 3), c_im.transpose(1, 2, 0, 3)

    def cols(x_re, x_im):
        return jnp.concatenate([x_re[..., 0, :], x_im[..., 0, :], x_re[..., 1, :], x_im[..., 1, :]], axis=-1)

    e_r = jnp.stack([pw_r[:, :L, 0][:, ::-1], pw_r[:, :L, 1]], axis=2)
    e_i = jnp.stack([pw_i[:, :L, 0][:, ::-1], pw_i[:, :L, 1]], axis=2)
    ws = (cols(e_r, e_r)[:, :, None] * cols(bb_r, bb_i)[:, None]
          + cols(e_i, e_i)[:, :, None] * cols(-bb_i, bb_r)[:, None]).reshape(g, L * c, 4 * p)
    f_r = jnp.stack([pw_r[:, 1:, 0], pw_r[:, 1:, 1][:, ::-1]], axis=2)
    f_i = jnp.stack([pw_i[:, 1:, 0], pw_i[:, 1:, 1][:, ::-1]], axis=2)
    wot = (cols(f_r, f_i)[:, :, None] * cols(ct_r, -ct_r)[:, None]
           + cols(f_i, f_r)[:, :, None] * cols(-ct_i, -ct_i)[:, None]).reshape(g, L * c, 4 * p)
    qr, qi = pw_r[:, :L, None], pw_i[:, :L, None]
    kp_r, kp_i = qr * bb_r[:, None] - qi * bb_i[:, None], qr * bb_i[:, None] + qi * bb_r[:, None]
    kern = [jnp.einsum('gnip,gop->gino', kp_r[:, :, :, d], ct_r[:, :, d], precision=HI)
            - jnp.einsum('gnip,gop->gino', kp_i[:, :, :, d], ct_i[:, :, d], precision=HI) for d in range(2)]
    skip = d_skip.reshape(g, c, 1, 1) * jnp.eye(c, dtype=F32)[None, :, None, :]
    by_offset = jnp.concatenate([kern[1][:, :, :0:-1], kern[0][:, :, :1] + kern[1][:, :, :1] + skip,
                                 kern[0][:, :, 1:]], axis=2).reshape(g, c, (2 * L - 1) * c)
    mt = jnp.stack([by_offset[:, :, (L - 1 - j) * c:(2 * L - 1 - j) * c] for j in range(L)], axis=1)
    mt = mt.reshape(g, L * c, L * c)
    lr16, li16 = pw_r[:, L], pw_i[:, L]
    fa = jnp.concatenate([lr16[:, 0], lr16[:, 0], lr16[:, 1], lr16[:, 1]], axis=-1)
    fb = jnp.concatenate([-li16[:, 0], li16[:, 0], -li16[:, 1], li16[:, 1]], axis=-1)
    return mt, ws, wot, fa, fb


def _gmm(name, a, b, contract, a_stacked=False, b_stacked=False, o_stacked=False, add=None, out_dtype=F32):
    w = S5_CHUNK * SSM_GROUP_CH
    g = (a.shape[0] if a_stacked else a.shape[1] // w)
    gpb = math.gcd(g, S5_GROUPS_PER_STEP)
    dn = {'nn': (((1,), (0,)), ((), ())), 'nt': (((1,), (1,)), ((), ())), 'tn': (((0,), (0,)), ((), ()))}[contract]

    def spec(arr, stacked):
        if stacked:
            return pl.BlockSpec((gpb,) + arr.shape[1:], lambda i: (i, 0, 0))
        return pl.BlockSpec((arr.shape[0], gpb * w), lambda i: (0, i))

    def take(ref, stacked, e):
        return ref[e] if stacked else ref[:, e * w:(e + 1) * w]

    m = (a.shape[1] if a_stacked else a.shape[0]) if contract != 'tn' else w
    n = w
    if o_stacked:
        o_spec = pl.BlockSpec((gpb, m, n), lambda i: (i, 0, 0))
        o_shape = (g, m, n)
    else:
        o_spec = pl.BlockSpec((m, gpb * n), lambda i: (0, i))
        o_shape = (m, g * n)
    has_add = add is not None

    def body(*refs):
        if has_add:
            a_ref, b_ref, add_ref, o_ref = refs
        else:
            a_ref, b_ref, o_ref = refs
        for e in range(gpb):
            r = lax.dot_general(take(a_ref, a_stacked, e).astype(S5_DTYPE), take(b_ref, b_stacked, e).astype(S5_DTYPE),
                                dn, precision=HI if S5_DTYPE == F32 else None, preferred_element_type=F32)
            if has_add:
                r = r + take(add_ref, o_stacked, e)
            if o_stacked:
                o_ref[e] = r.astype(o_ref.dtype)
            else:
                o_ref[:, e * w:(e + 1) * w] = r.astype(o_ref.dtype)

    in_specs = [spec(a, a_stacked), spec(b, b_stacked)] + ([o_spec] if has_add else [])
    return pl.pallas_call(
        body, name=name, grid=(g // gpb,), in_specs=in_specs, out_specs=o_spec,
        out_shape=jax.ShapeDtypeStruct(o_shape, out_dtype), compiler_params=_cparams(("parallel",)),
    )(*((a, b) + ((add,) if has_add else ())))


def _s5_scan(name, s, fa, fb, rev0, xin=None):
    nk, g, w = s.shape
    hw = w // 2
    gb = min(g, 16)
    with_acc = xin is not None

    def body(*refs):
        if with_acc:
            s_ref, a_ref, b_ref, x_ref, o_ref, pa_ref, pb_ref = refs
        else:
            s_ref, a_ref, b_ref, o_ref = refs
        fa0, fb0, fa1, fb1 = a_ref[:, :hw], b_ref[:, :hw], a_ref[:, hw:], b_ref[:, hw:]

        def step(i, carry):
            x0, x1, pa0, pb0, pa1, pb1 = carry
            k0 = (nk - 1 - i) if rev0 else i
            k1 = i if rev0 else (nk - 1 - i)
            o_ref[k0, :, :hw] = x0
            o_ref[k1, :, hw:] = x1
            if with_acc:
                xi0, xi1 = x_ref[k0, :, :hw], x_ref[k1, :, hw:]
                pa0 = pa0 + x0 * xi0
                pb0 = pb0 + x0 * pltpu.roll(xi0, hw // 2, 1)
                pa1 = pa1 + x1 * xi1
                pb1 = pb1 + x1 * pltpu.roll(xi1, hw // 2, 1)
            x0 = fa0 * x0 + fb0 * pltpu.roll(x0, hw // 2, 1) + s_ref[k0, :, :hw]
            x1 = fa1 * x1 + fb1 * pltpu.roll(x1, hw // 2, 1) + s_ref[k1, :, hw:]
            return x0, x1, pa0, pb0, pa1, pb1

        z = jnp.zeros((gb, hw), F32)
        res = lax.fori_loop(0, nk, step, (z, z, z, z, z, z), unroll=4)
        if with_acc:
            pa_ref[:, :hw] = res[2]
            pb_ref[:, :hw] = res[3]
            pa_ref[:, hw:] = res[4]
            pb_ref[:, hw:] = res[5]

    seq = pl.BlockSpec((nk, gb, w), lambda i: (0, i, 0))
    vec = pl.BlockSpec((gb, w), lambda i: (i, 0))
    in_specs = [seq, vec, vec] + ([seq] if with_acc else [])
    out_specs = [seq] + ([vec, vec] if with_acc else [])
    out_shape = [jax.ShapeDtypeStruct((nk, g, w), F32)] + (
        [jax.ShapeDtypeStruct((g, w), F32)] * 2 if with_acc else [])
    return pl.pallas_call(
        body, name=name, grid=(g // gb,), in_specs=in_specs, out_specs=out_specs, out_shape=out_shape,
        compiler_params=_cparams(("parallel",)),
    )(*((s, fa, fb) + ((xin,) if with_acc else ())))


def _to_groups(u):
    t, sw = u.shape
    g = sw // SSM_GROUP_CH
    return u.reshape(t // S5_CHUNK, S5_CHUNK, g, SSM_GROUP_CH).transpose(0, 2, 1, 3).reshape(t // S5_CHUNK, -1)


def _from_groups(y, sw):
    nk = y.shape[0]
    g = sw // SSM_GROUP_CH
    return y.reshape(nk, g, S5_CHUNK, SSM_GROUP_CH).transpose(0, 2, 1, 3).reshape(nk * S5_CHUNK, sw)


def _s5_fwd(u2, mats):
    mt, ws, wot, fa, fb = mats
    nk = u2.shape[0]
    g = mt.shape[0]
    y_intra = _gmm("s5_intra", u2, mt, 'nn', b_stacked=True)
    s = _gmm("s5_chunk_state", u2, ws, 'nn', b_stacked=True)
    (xin,) = _s5_scan("s5_scan", s.reshape(nk, g, -1), fa, fb, False)
    xin = xin.reshape(nk, -1)
    return _gmm("s5_inter", xin, wot, 'nt', b_stacked=True, add=y_intra, out_dtype=S5_DTYPE), xin


def _s5_bwd(u2, xin, mats, dy2):
    mt, ws, wot, fa, fb = mats
    nk = u2.shape[0]
    g = mt.shape[0]
    dxin = _gmm("s5_dxin", dy2, wot, 'nn', b_stacked=True)
    ds, pa, pb = _s5_scan("s5_scan_adj", dxin.reshape(nk, g, -1), fa, -fb, True, xin=xin.reshape(nk, g, -1))
    ds = ds.reshape(nk, -1)
    du_a = _gmm("s5_du_intra", dy2, mt, 'nt', b_stacked=True)
    du2 = _gmm("s5_du_state", ds, ws, 'nt', b_stacked=True, add=du_a, out_dtype=S5_DTYPE)
    dmt = _gmm("s5_dmt", u2, dy2, 'tn', o_stacked=True)
    dws = _gmm("s5_dws", u2, ds, 'tn', o_stacked=True)
    dwot = _gmm("s5_dwot", dy2, xin, 'tn', o_stacked=True)
    return du2, (dmt, dws, dwot, pa, pb)


def _late_weights(gathered):
    w_glu, w_out, w_gate, w_up, w_down = (g4.reshape((N_CHIPS, -1, g4.shape[-1])) for g4 in gathered)
    return w_glu.reshape(-1, w_glu.shape[-1]), w_out.reshape(-1, w_out.shape[-1]), w_gate, w_up, w_down


def _local_step(x, target, w_in4, late, small, reduce_late=None):
    t, d = x.shape
    aw = w_in4.shape[2]
    sw = aw
    nh = aw // HEAD_DIM
    row = lambda v: v.reshape(1, -1)
    g_mix, g_ffn = row(small['g_mix']), row(small['g_ffn'])
    g_oa, g_os, b_glu = row(small['g_out_attn']), row(small['g_out_ssm']), row(small['b_glu'])
    qg2 = jnp.tile(row(small['q_gain']), (1, 2))
    kg2 = jnp.tile(row(small['k_gain']), (1, 2))

    (h,) = _ew("rms_mix", lambda xv, g: _rms(xv, g)[0], [('r', x), ('c', g_mix)], [('r', d, MXU_DTYPE)])
    z4 = _mm("in_proj", h, w_in4, contract='nn', b_mode='b', o_mode='b')
    bias_t = _bias_table(small['rpb'])
    if late[0] == 'halves':
        ya, gathered = _attn_fwd(z4, qg2, kg2, bias_t, comm=_GatherChips(late[1]), comm_arrs=late[1])
        w_glu, w_out, w_gate4, w_up4, w_down4 = _late_weights(gathered)
    else:
        ya, _ = _attn_fwd(z4, qg2, kg2, bias_t)
        w_glu, w_out, w_gate4, w_up4, w_down4 = late[1]
    ffs = w_gate4.shape[2]
    s5_params = tuple(small[n] for n in ('ssm_a_re', 'ssm_a_im', 'ssm_b_re', 'ssm_b_im', 'ssm_c_re', 'ssm_c_im',
                                         'ssm_log_step', 'ssm_d'))
    mats, mats_vjp = jax.vjp(_s5_mats, *s5_params)
    mats = tuple(m.astype(S5_DTYPE) for m in mats[:3]) + mats[3:]
    u2 = _to_groups(z4[3].astype(S5_DTYPE))
    ypre2, xin = _s5_fwd(u2, mats)
    ypre = _from_groups(ypre2, sw)
    (yb,) = _ew("gelu", _gelu, [('r', ypre)], [('r', sw, MXU_DTYPE)])
    a_glu = _mm("glu_proj", yb, w_glu, contract='nn')

    def mix_out(yav, ypv, av, bg, goa, gos):
        ys = _gelu(ypv) * _sigmoid(av + bg)
        return jnp.concatenate([_rms(yav, goa)[0], _rms(ys, gos)[0]], axis=1)
    (ycat,) = _ew("mix_out", mix_out, [('r', ya), ('r', ypre), ('r', a_glu), ('c', b_glu), ('c', g_oa), ('c', g_os)],
                  [('r', aw + sw, MXU_DTYPE)])
    x1 = _mm("out_proj", ycat, w_out, contract='nn', add=x)
    (h2,) = _ew("rms_ffn", lambda xv, g: _rms(xv, g)[0], [('r', x1), ('c', g_ffn)], [('r', d, MXU_DTYPE)])
    gate4 = _mm("ffn_gate", h2, w_gate4, contract='nn', b_mode='b', o_mode='b', tn=ffs)
    up4 = _mm("ffn_up", h2, w_up4, contract='nn', b_mode='b', o_mode='b', tn=ffs)
    gate_f, up_f = gate4.reshape(4 * t, ffs), up4.reshape(4 * t, ffs)
    (act,) = _ew("swiglu", lambda gv, uv: gv * _sigmoid(gv) * uv, [('r', gate_f), ('r', up_f)],
                 [('r', ffs, MXU_DTYPE)])
    act4 = act.reshape(4, t, ffs)
    x2 = _mm("ffn_down", act4, w_down4, contract='nn', a_mode='c', b_mode='c', add=x1, tk=ffs)

    def loss_fn(xv, tv):
        diff = xv - tv
        return diff * (1.0 / d), diff * diff
    dx2, sq = _ew("loss", loss_fn, [('r', x2), ('r', target)], [('r', d, F32), ('a', d)])

    dact4 = _mm("ffn_down_dx", dx2, w_down4, contract='nt', b_mode='b', o_mode='b', tn=ffs)
    d_w_down4 = _mm("ffn_down_dw", act4, dx2, contract='tn', a_mode='b', o_mode='b', tm=ffs, out_dtype=DW_DTYPE)

    def swiglu_bwd(dav, gv, uv):
        s = _sigmoid(gv)
        return dav * uv * s * (1.0 + gv * (1.0 - s)), dav * gv * s
    dgate, dup = _ew("swiglu_bwd", swiglu_bwd, [('r', dact4.reshape(4 * t, ffs)), ('r', gate_f), ('r', up_f)],
                     [('r', ffs, MXU_DTYPE), ('r', ffs, MXU_DTYPE)])
    dgate4, dup4 = dgate.reshape(4, t, ffs), dup.reshape(4, t, ffs)
    dh2 = _mm("ffn_gate_dx", dgate4, w_gate4, contract='nt', a_mode='c', b_mode='c', tk=ffs)
    dh2 = _mm("ffn_up_dx", dup4, w_up4, contract='nt', a_mode='c', b_mode='c', add=dh2, tk=ffs)
    d_w_gate4 = _mm("ffn_gate_dw", h2, dgate4, contract='tn', b_mode='b', o_mode='b', tn=ffs, out_dtype=DW_DTYPE)
    d_w_up4 = _mm("ffn_up_dw", h2, dup4, contract='tn', b_mode='b', o_mode='b', tn=ffs, out_dtype=DW_DTYPE)

    def rms_res_bwd(xv, g, dyv, resv):
        dx, dg = _rms_bwd(xv, g, dyv)
        return resv + dx, dg
    dx1, d_g_ffn = _ew("rms_ffn_bwd", rms_res_bwd, [('r', x1), ('c', g_ffn), ('r', dh2), ('r', dx2)],
                       [('r', d, F32), ('a', d)])

    dycat = _mm("out_proj_dx", dx1, w_out, contract='nt')
    d_w_out = _mm("out_proj_dw", ycat, dx1, contract='tn', out_dtype=DW_DTYPE)

    def mix_out_bwd(yav, ypv, av, bg, goa, gos, dca, dcs):
        dya, dgoa = _rms_bwd(yav, goa, dca)
        y = _gelu(ypv)
        s = _sigmoid(av + bg)
        dys, dgos = _rms_bwd(y * s, gos, dcs)
        da = dys * y * s * (1.0 - s)
        return dya, da, dys * s, dgoa, dgos, da
    dya, da, dy_direct, d_g_oa, d_g_os, d_b_glu = _ew(
        "mix_out_bwd", mix_out_bwd,
        [('r', ya), ('r', ypre), ('r', a_glu), ('c', b_glu), ('c', g_oa), ('c', g_os),
         ('r', dycat, 0, aw), ('r', dycat, 1, sw)],
        [('r', aw, F32), ('r', sw, MXU_DTYPE), ('r', sw, F32), ('a', aw), ('a', sw), ('a', sw)])
    dy = _mm("glu_proj_dx", da, w_glu, contract='nt', add=dy_direct)
    d_w_glu = _mm("glu_proj_dw", yb, da, contract='tn', out_dtype=DW_DTYPE)
    (dypre,) = _ew("gelu_bwd", lambda dyv, ypv: dyv * _gelu_grad(ypv), [('r', dy), ('r', ypre)],
                   [('r', sw, S5_DTYPE)])

    du2, dmats = _s5_bwd(u2, xin, mats, _to_groups(dypre))
    d_s5 = mats_vjp(dmats)
    du = _from_groups(du2, sw)
    d_late = (d_w_glu, d_w_out, d_w_gate4, d_w_up4, d_w_down4)
    if reduce_late is not None:
        sums = reduce_late(d_late)
        (dz4, dbias_t, dqg, dkg), scattered = _attn_bwd(z4, qg2, kg2, bias_t, dya, comm=_ScatterChips(sums),
                                                       comm_arrs=sums)
        d_late = (sums, list(scattered))
    else:
        (dz4, dbias_t, dqg, dkg), _ = _attn_bwd(z4, qg2, kg2, bias_t, dya)
    d_rpb = _bias_table_grad(dbias_t)
    fold = lambda v: v.reshape(-1, 2, HEAD_DIM).sum(axis=(0, 1))
    dz4 = dz4.at[3].set(du)

    dh = _mm("in_proj_dx", dz4, w_in4, contract='nt', a_mode='c', b_mode='c')
    d_w_in4 = _mm("in_proj_dw", h, dz4, contract='tn', b_mode='b', o_mode='b', out_dtype=DW_DTYPE)
    dx, d_g_mix = _ew("rms_mix_bwd", rms_res_bwd, [('r', x), ('c', g_mix), ('r', dh), ('r', dx1)],
                      [('r', d, F32), ('a', d)])

    colsum = lambda v: v.sum(axis=0)
    d_small = {
        'g_mix': colsum(d_g_mix), 'q_gain': fold(dqg), 'k_gain': fold(dkg), 'rpb': d_rpb,
        'ssm_a_re': d_s5[0], 'ssm_a_im': d_s5[1], 'ssm_b_re': d_s5[2], 'ssm_b_im': d_s5[3],
        'ssm_c_re': d_s5[4], 'ssm_c_im': d_s5[5], 'ssm_log_step': d_s5[6], 'ssm_d': d_s5[7],
        'b_glu': colsum(d_b_glu), 'g_out_attn': colsum(d_g_oa), 'g_out_ssm': colsum(d_g_os), 'g_ffn': colsum(d_g_ffn),
    }
    return jnp.sum(sq), dx, d_late, d_w_in4, d_small


ANY = pl.BlockSpec(memory_space=pl.ANY)


def _place():
    x, y, c = lax.axis_index("x"), lax.axis_index("y"), lax.axis_index("c")
    other_chips = [(1 - x, y), (x, 1 - y), (1 - x, 1 - y)]
    return x, y, c, 2 * x + y, (x, y, 1 - c), other_chips


class _GatherChips:
    KINDS = 7

    def __init__(self, arrs):
        self.n = len(arrs)
        self.out_shape = [jax.ShapeDtypeStruct((N_CHIPS,) + a.shape, a.dtype) for a in arrs]
        self.scratch = [pltpu.SemaphoreType.DMA((self.n, self.KINDS)), pltpu.SemaphoreType.DMA((self.n, self.KINDS))]

    def _copies(self, ins, outs, send_sems, recv_sems):
        x, y, c, me, sibling, chips = _place()

        def remote(a, k, src, dst, to):
            return lambda: pltpu.make_async_remote_copy(src_ref=src, dst_ref=dst, send_sem=send_sems.at[a, k],
                                                        recv_sem=recv_sems.at[a, k], device_id=to, device_id_type=MESH)
        own, out, landed, passed, theirs = [], [], [], [], []
        for a in range(self.n):
            own.append(remote(a, 6, ins[a], outs[a].at[me], sibling))
            for j, (px, py) in enumerate(chips):
                there, here = outs[a].at[2 * px + py, c], outs[a].at[2 * px + py, 1 - c]
                out.append(remote(a, j, ins[a].at[c], outs[a].at[me, c], (px, py, c)))
                landed.append(remote(a, j, there, there, (px, py, c)))
                passed.append(remote(a, 3 + j, there, there, sibling))
                theirs.append(remote(a, 3 + j, here, here, sibling))
        return own, out, landed, passed, theirs

    def start(self, ins, outs, send_sems, recv_sems):
        own, out, _, _, _ = self._copies(ins, outs, send_sems, recv_sems)
        for make in own + out:
            make().start()

    def finish(self, ins, outs, send_sems, recv_sems):
        own, out, landed, passed, theirs = self._copies(ins, outs, send_sems, recv_sems)
        for arrived, onward in zip(landed, passed):
            arrived().wait_recv()
            onward().start()
        for make in theirs + own:
            make().wait_recv()
        for make in own + out + passed:
            make().wait_send()


class _ScatterChips:
    def __init__(self, sums):
        self.n = len(sums)
        self.out_shape = [jax.ShapeDtypeStruct(s.shape, s.dtype) for s in sums]
        self.scratch = [pltpu.SemaphoreType.DMA((self.n, 3)), pltpu.SemaphoreType.DMA((self.n, 3))]

    def _copies(self, ins, outs, send_sems, recv_sems):
        x, y, c, me, sibling, chips = _place()
        out, landed = [], []

        def remote(a, j, src, dst, to):
            return lambda: pltpu.make_async_remote_copy(src_ref=src, dst_ref=dst, send_sem=send_sems.at[a, j],
                                                        recv_sem=recv_sems.at[a, j], device_id=to, device_id_type=MESH)
        for a in range(self.n):
            for j, (px, py) in enumerate(chips):
                slot = outs[a].at[2 * px + py]
                out.append(remote(a, j, ins[a].at[2 * px + py], outs[a].at[me], (px, py, c)))
                landed.append(remote(a, j, slot, slot, (px, py, c)))
        return out, landed

    def start(self, ins, outs, send_sems, recv_sems):
        for make in self._copies(ins, outs, send_sems, recv_sems)[0]:
            make().start()

    def finish(self, ins, outs, send_sems, recv_sems):
        out, landed = self._copies(ins, outs, send_sems, recv_sems)
        for make in landed:
            make().wait_recv()
        for make in out:
            make().wait_send()


def _comm_call(name, comm, arrs):
    n = comm.n

    def body(*refs):
        parts = (refs[:n], refs[n:2 * n]) + tuple(refs[2 * n:])
        comm.start(*parts)
        comm.finish(*parts)

    return pl.pallas_call(body, name=name, in_specs=[ANY] * n, out_specs=[ANY] * n, out_shape=comm.out_shape,
                          scratch_shapes=comm.scratch)(*arrs)


def _gather_chips(name, arrs):
    return _comm_call(name, _GatherChips(arrs), arrs)


def _swap_halves(name, parts):
    n = len(parts)

    def body(*refs):
        ins, outs = refs[:n], refs[n:2 * n]
        send_sems, recv_sems = refs[2 * n:]
        x, y, c, me, sibling, chips = _place()
        cps = []
        for a in range(n):
            cp = pltpu.make_async_remote_copy(src_ref=ins[a].at[:, 1 - c], dst_ref=outs[a], send_sem=send_sems.at[a],
                                              recv_sem=recv_sems.at[a], device_id=sibling, device_id_type=MESH)
            cp.start()
            cps.append(cp)
        for cp in cps:
            cp.wait()

    return pl.pallas_call(
        body, name=name, in_specs=[ANY] * n, out_specs=[ANY] * n,
        out_shape=[jax.ShapeDtypeStruct((N_CHIPS,) + p.shape[2:], p.dtype) for p in parts],
        scratch_shapes=[pltpu.SemaphoreType.DMA((n,)), pltpu.SemaphoreType.DMA((n,))],
    )(*parts)


def _scatter_chips(name, sums):
    return _comm_call(name, _ScatterChips(sums), sums)


def _swap_reduced(name, halves):
    n = len(halves)

    def body(*refs):
        ins, outs = refs[:n], refs[n:2 * n]
        send_sems, recv_sems = refs[2 * n:]
        x, y, c, me, sibling, chips = _place()
        cps = []
        for a in range(n):
            cp = pltpu.make_async_remote_copy(src_ref=ins[a], dst_ref=outs[a], send_sem=send_sems.at[a],
                                              recv_sem=recv_sems.at[a], device_id=sibling, device_id_type=MESH)
            cp.start()
            cps.append(cp)
        for cp in cps:
            cp.wait()

    return pl.pallas_call(
        body, name=name, in_specs=[ANY] * n, out_specs=[ANY] * n,
        out_shape=[jax.ShapeDtypeStruct(h.shape, h.dtype) for h in halves],
        scratch_shapes=[pltpu.SemaphoreType.DMA((n,)), pltpu.SemaphoreType.DMA((n,))],
    )(*halves)


def _row_tile(r, want=256):
    t = (min(r, want) // SUBLANES) * SUBLANES
    while r % t:
        t -= SUBLANES
    return t


def _add_own_half(name, part, got, c, out_dtype):
    _, _, r, cols = part.shape
    tr = _row_tile(r)

    def body(c_ref, p_ref, g_ref, o_ref):
        o_ref[...] = (p_ref[...].astype(F32) + g_ref[...].astype(F32)).astype(o_ref.dtype)

    return pl.pallas_call(
        body, name=name,
        grid_spec=pltpu.PrefetchScalarGridSpec(
            num_scalar_prefetch=1, grid=(N_CHIPS, r // tr),
            in_specs=[pl.BlockSpec((None, None, tr, cols), lambda s, i, c_ref: (s, c_ref[0], i, 0)),
                      pl.BlockSpec((None, tr, cols), lambda s, i, c_ref: (s, i, 0))],
            out_specs=pl.BlockSpec((None, tr, cols), lambda s, i, c_ref: (s, i, 0))),
        out_shape=jax.ShapeDtypeStruct(got.shape, out_dtype),
        compiler_params=_cparams(("parallel", "parallel")),
    )(c.reshape(1).astype(jnp.int32), part, got)


def _sum_chips(name, got, own, me):
    _, r, cols = got.shape
    tr = _row_tile(r)

    def body(me_ref, r0, r1, r2, r3, own_ref, o_ref):
        pick = lambda s, ref: jnp.where(me_ref[0] == s, own_ref[...], ref[...]).astype(F32)
        o_ref[...] = ((pick(0, r0) + pick(1, r1)) + pick(2, r2)) + pick(3, r3)

    def slot(s):
        return pl.BlockSpec((None, tr, cols),
                            lambda i, me_ref: (jnp.where(me_ref[0] == s, (s + 1) % N_CHIPS, s), i, 0))

    return pl.pallas_call(
        body, name=name,
        grid_spec=pltpu.PrefetchScalarGridSpec(
            num_scalar_prefetch=1, grid=(r // tr,),
            in_specs=[slot(s) for s in range(N_CHIPS)]
            + [pl.BlockSpec((None, tr, cols), lambda i, me_ref: (me_ref[0], i, 0))],
            out_specs=pl.BlockSpec((tr, cols), lambda i, me_ref: (i, 0))),
        out_shape=jax.ShapeDtypeStruct((r, cols), F32),
        compiler_params=_cparams(("parallel",)),
    )(me.reshape(1).astype(jnp.int32), got, got, got, got, own)


def _adamw_math(wv, gv, mv, vv):
    mv = ADAM_B1 * mv + (1.0 - ADAM_B1) * gv
    vv = ADAM_B2 * vv + (1.0 - ADAM_B2) * (gv * gv)
    m_hat = mv / (1.0 - ADAM_B1 ** ADAM_STEP)
    v_hat = vv / (1.0 - ADAM_B2 ** ADAM_STEP)
    return -ADAM_LR * (m_hat / (jnp.sqrt(v_hat) + ADAM_EPS) + ADAM_WD * wv), mv, vv


def _adamw(name, w, g, m, v):
    cols = w.shape[1]
    return _ew(name, _adamw_math, [('r', w), ('r', g), ('r', m), ('r', v)], [('r', cols, F32)] * 3,
               tr=_row_tile(w.shape[0], 128))


def _adamw_halves(name, w, mine, theirs, m, v, c):
    r, cols = mine.shape
    tr = _row_tile(r, 128)
    nb = r // tr

    def body(c_ref, w_ref, a_ref, b_ref, m_ref, v_ref, g_out, d_out, m_out, v_out):
        g = jnp.where(pl.program_id(0) == c_ref[0], a_ref[...], b_ref[...])
        g_out[...] = g
        d_out[...], m_out[...], v_out[...] = _adamw_math(w_ref[...], g, m_ref[...], v_ref[...])

    whole = pl.BlockSpec((tr, cols), lambda h, i, c_ref: (h * nb + i, 0))
    half = pl.BlockSpec((tr, cols), lambda h, i, c_ref: (i, 0))
    return pl.pallas_call(
        body, name=name,
        grid_spec=pltpu.PrefetchScalarGridSpec(
            num_scalar_prefetch=1, grid=(2, nb),
            in_specs=[whole, half, half, whole, whole], out_specs=[whole] * 4),
        out_shape=[jax.ShapeDtypeStruct(w.shape, F32)] * 4,
        compiler_params=_cparams(("parallel", "parallel")),
    )(c.reshape(1).astype(jnp.int32), w, mine, theirs, m, v)


SMALL_ROWS_ALIGN = 2 * N_CHIPS * SUBLANES


MEDIUM_NAMES = ['ssm_b_re', 'ssm_b_im', 'ssm_c_re', 'ssm_c_im']
PACKED_NAMES = [n for n in SMALL_NAMES if n not in MEDIUM_NAMES]


def _pack_small(d):
    flat = jnp.concatenate([d[n].reshape(-1).astype(F32) for n in PACKED_NAMES])
    rows = -(-flat.shape[0] // (LANES * SMALL_ROWS_ALIGN)) * SMALL_ROWS_ALIGN
    return jnp.pad(flat, (0, rows * LANES - flat.shape[0])).reshape(rows, LANES)


def _unpack_small(packed, like):
    flat = packed.reshape(-1)
    out, off = {}, 0
    for n in PACKED_NAMES:
        size = like[n].size
        out[n] = flat[off:off + size].reshape(like[n].shape)
        off += size
    return out


def kernel(x, g_mix, w_in, q_gain, k_gain, rpb, ssm_a_re, ssm_a_im, ssm_b_re, ssm_b_im, ssm_c_re, ssm_c_im, ssm_log_step, ssm_d, w_glu, b_glu, g_out_attn, g_out_ssm, w_out, g_ffn, w_ffn_gate, w_ffn_up, w_ffn_down, loss_target, m_g_mix, m_w_in, m_q_gain, m_k_gain, m_rpb, m_ssm_a_re, m_ssm_a_im, m_ssm_b_re, m_ssm_b_im, m_ssm_c_re, m_ssm_c_im, m_ssm_log_step, m_ssm_d, m_w_glu, m_b_glu, m_g_out_attn, m_g_out_ssm, m_w_out, m_g_ffn, m_w_ffn_gate, m_w_ffn_up, m_w_ffn_down, v_g_mix, v_w_in, v_q_gain, v_k_gain, v_rpb, v_ssm_a_re, v_ssm_a_im, v_ssm_b_re, v_ssm_b_im, v_ssm_c_re, v_ssm_c_im, v_ssm_log_step, v_ssm_d, v_w_glu, v_b_glu, v_g_out_attn, v_g_out_ssm, v_w_out, v_g_ffn, v_w_ffn_gate, v_w_ffn_up, v_w_ffn_down):
    given = dict(locals())
    w = {n: given[n][0] for n in WEIGHT_NAMES}
    mom = {n: given["m_" + n][0] for n in WEIGHT_NAMES}
    var = {n: given["v_" + n][0] for n in WEIGHT_NAMES}
    d = x.shape[-1]
    c = lax.axis_index("c")

    halves = {n: w[n].astype(MXU_DTYPE).reshape((2, w[n].shape[0] // 2, w[n].shape[1])) for n in BIG_NAMES}
    (w_in4,) = _gather_chips("gather_w_in", [halves['w_in']])
    w_in4 = w_in4.reshape((N_CHIPS, -1, w_in4.shape[-1]))

    def chip_sums(tag, grads, payload):
        parts = [g.reshape((N_CHIPS, 2, -1, g.shape[-1])) for g in grads]
        got = _swap_halves("reduce_swap_halves_" + tag, parts)
        return [_add_own_half("reduce_add_%s_%d" % (tag, a), p, gt, c, dt)
                for a, (p, gt, dt) in enumerate(zip(parts, got, payload))]

    reduce_late = lambda grads: chip_sums("late", grads, [GRAD_PAYLOAD_DTYPE] * len(grads))
    sq, dx, (sums_late, got_late), d_w_in4, d_small = _local_step(
        x[0], loss_target[0], w_in4, ('halves', [halves[n] for n in LATE_NAMES]), {n: w[n] for n in SMALL_NAMES},
        reduce_late)
    loss = lax.psum(0.5 * sq / d, ("x", "y", "c"))

    nbig = len(BIG_NAMES)
    rest = [d_w_in4] + [d_small[n].reshape(-1, LANES) for n in MEDIUM_NAMES] + [_pack_small(d_small)]
    sums_rest = chip_sums("rest", rest, [GRAD_PAYLOAD_DTYPE] + [F32] * (len(rest) - 1))
    got_rest = list(_scatter_chips("reduce_scatter_chips", sums_rest))
    sums = sums_rest[:1] + sums_late + sums_rest[1:]
    got = got_rest[:1] + got_late + got_rest[1:]
    me = 2 * lax.axis_index("x") + lax.axis_index("y")
    mine = [_sum_chips("reduce_sum_%d" % a, gt, sm_, me) for a, (gt, sm_) in enumerate(zip(got, sums))]
    theirs = _swap_reduced("reduce_swap_reduced", mine)
    in_order = lambda a: jnp.where(c == 0, jnp.stack([mine[a], theirs[a]]), jnp.stack([theirs[a], mine[a]]))
    repl = _gather_chips("gather_small", [in_order(a) for a in range(nbig, len(mine))])
    repl = [r.reshape(-1, LANES) for r in repl]
    like = {n: w[n] for n in SMALL_NAMES}
    grad_small = _unpack_small(repl[-1], like)
    grad_small.update({n: r.reshape(w[n].shape) for n, r in zip(MEDIUM_NAMES, repl)})

    grad_big, delta, new_m, new_v = {}, {}, {}, {}
    for a, n in enumerate(BIG_NAMES):
        grad_big[n], delta[n], new_m[n], new_v[n] = _adamw_halves("adamw_%d" % a, w[n], mine[a], theirs[a],
                                                                  mom[n], var[n], c)
    for n, r in zip(MEDIUM_NAMES, repl):
        res = _adamw("adamw_" + n, w[n].reshape(-1, LANES), r, mom[n].reshape(-1, LANES), var[n].reshape(-1, LANES))
        delta[n], new_m[n], new_v[n] = (t.reshape(w[n].shape) for t in res)
    sd, sm, sv = _adamw("adamw_small", _pack_small(w), repl[-1], _pack_small(mom), _pack_small(var))
    delta.update(_unpack_small(sd, like))
    new_m.update(_unpack_small(sm, like))
    new_v.update(_unpack_small(sv, like))
    grads = {**grad_big, **grad_small}
    lead = lambda t: t[None]
    return (loss, dx[None], *[lead(grads[n]) for n in WEIGHT_NAMES], *[lead(delta[n]) for n in WEIGHT_NAMES],
            *[lead(new_m[n]) for n in WEIGHT_NAMES], *[lead(new_v[n]) for n in WEIGHT_NAMES])
```

```python
import functools
import math

import jax
import jax.numpy as jnp
from jax import lax
from jax.experimental import pallas as pl
from jax.experimental.pallas import tpu as pltpu

F32 = jnp.float32
BF16 = jnp.bfloat16
MXU_DTYPE = BF16
GRAD_PAYLOAD_DTYPE = BF16
DW_DTYPE = BF16
S5_DTYPE = BF16
HI = lax.Precision.HIGHEST
VMEM_LIMIT_V7X = 56 * 1024 * 1024
LANES = 128
SUBLANES = 8

GRID_W = 64
WIN_H = 8
WIN_W = 16
HEAD_DIM = 64
SSM_GROUP_CH = 16
SSM_STATE = 64
S5_CHUNK = 16
S5_GROUPS_PER_STEP = 8
RMS_EPS = 1e-6
NEG_INF = -1e30
N_CHIPS = 4
MESH = pl.DeviceIdType.MESH

ADAM_LR = 0.001
ADAM_B1 = 0.9
ADAM_B2 = 0.999
ADAM_EPS = 1e-08
ADAM_WD = 0.01
ADAM_STEP = 10

WEIGHT_NAMES = ['g_mix', 'w_in', 'q_gain', 'k_gain', 'rpb', 'ssm_a_re', 'ssm_a_im', 'ssm_b_re', 'ssm_b_im',
                'ssm_c_re', 'ssm_c_im', 'ssm_log_step', 'ssm_d', 'w_glu', 'b_glu', 'g_out_attn', 'g_out_ssm',
                'w_out', 'g_ffn', 'w_ffn_gate', 'w_ffn_up', 'w_ffn_down']
BIG_NAMES = ['w_in', 'w_glu', 'w_out', 'w_ffn_gate', 'w_ffn_up', 'w_ffn_down']
LATE_NAMES = BIG_NAMES[1:]
SMALL_NAMES = [n for n in WEIGHT_NAMES if n not in BIG_NAMES]


def _cparams(sem):
    return pltpu.CompilerParams(dimension_semantics=sem, vmem_limit_bytes=VMEM_LIMIT_V7X)


def _tile(n, want):
    if n <= want:
        return n
    t = (want // LANES) * LANES
    while t >= LANES:
        if n % t == 0:
            return t
        t -= LANES
    return n


def _mm(name, a, b, *, contract, a_mode='2', b_mode='2', o_mode='2', out_dtype=F32, add=None, exact=False,
        tm=1024, tn=1024, tk=2048):
    dn = {'nn': (((1,), (0,)), ((), ())), 'nt': (((1,), (1,)), ((), ())), 'tn': (((0,), (0,)), ((), ()))}[contract]
    ar, ac = a.shape[-2:]
    br, bc = b.shape[-2:]
    m, kdim = (ar, ac) if contract != 'tn' else (ac, ar)
    n = bc if contract != 'nt' else br
    assert kdim == (br if contract != 'nt' else bc), (name, a.shape, b.shape)
    nbatch = 1
    for arr, mode in ((a, a_mode), (b, b_mode)):
        if mode == 'b':
            nbatch = arr.shape[0]
    nstack = 1
    for arr, mode in ((a, a_mode), (b, b_mode)):
        if mode == 'c':
            nstack = arr.shape[0]
    tm, tn, tk = _tile(m, tm), _tile(n, tn), _tile(kdim, tk)
    nkin = kdim // tk
    nk = nstack * nkin
    grid = (nbatch, m // tm, n // tn, nk)

    def spec(mode, block, rc):
        def imap(s, i, j, kk):
            r, c = rc(i, j, kk % nkin)
            if mode == '2':
                return (r, c)
            return (s if mode == 'b' else kk // nkin, r, c)
        return pl.BlockSpec(block if mode == '2' else (None,) + block, imap)

    a_spec = spec(a_mode, (tm, tk) if contract != 'tn' else (tk, tm),
                  (lambda i, j, k: (i, k)) if contract != 'tn' else (lambda i, j, k: (k, i)))
    b_spec = spec(b_mode, (tk, tn) if contract != 'nt' else (tn, tk),
                  (lambda i, j, k: (k, j)) if contract != 'nt' else (lambda i, j, k: (j, k)))
    o_spec = spec(o_mode, (tm, tn), lambda i, j, k: (i, j))
    out_shape = (m, n) if o_mode == '2' else (nbatch, m, n)
    has_add = add is not None

    def product(a_ref, b_ref):
        if exact:
            return lax.dot_general(a_ref[...].astype(F32), b_ref[...].astype(F32), dn, precision=HI,
                                   preferred_element_type=F32)
        return lax.dot_general(a_ref[...].astype(MXU_DTYPE), b_ref[...].astype(MXU_DTYPE), dn,
                               preferred_element_type=F32)

    def body(*refs):
        a_ref, b_ref = refs[:2]
        add_ref = refs[2] if has_add else None
        o_ref = refs[2 + has_add]

        def write(r):
            if has_add:
                r = r + add_ref[...].astype(F32)
            o_ref[...] = r.astype(o_ref.dtype)

        if nk == 1:
            write(product(a_ref, b_ref))
            return
        acc_ref = refs[3 + has_add]
        k = pl.program_id(3)

        @pl.when(k == 0)
        def _():
            acc_ref[...] = jnp.zeros_like(acc_ref)

        acc_ref[...] += product(a_ref, b_ref)

        @pl.when(k == nk - 1)
        def _():
            write(acc_ref[...])

    in_specs = [a_spec, b_spec] + ([o_spec] if has_add else [])
    args = (a, b) + ((add,) if has_add else ())
    return pl.pallas_call(
        body, name=name, grid=grid, in_specs=in_specs, out_specs=o_spec,
        out_shape=jax.ShapeDtypeStruct(out_shape, out_dtype),
        scratch_shapes=[pltpu.VMEM((tm, tn), F32)] if nk > 1 else [],
        compiler_params=_cparams(("parallel", "parallel", "parallel", "arbitrary")),
    )(*args)


def _ew(name, fn, ins, outs, tr=256):
    rows = next(x[1].shape[0] for x in ins if x[0] == 'r')
    tr = min(tr, rows)
    assert rows % tr == 0 and tr % SUBLANES == 0, (name, rows, tr)
    in_specs, args = [], []
    for x in ins:
        if x[0] == 'r' and len(x) == 2:
            in_specs.append(pl.BlockSpec((tr, x[1].shape[1]), lambda i: (i, 0)))
        elif x[0] == 'r':
            in_specs.append(pl.BlockSpec((tr, x[3]), functools.partial(lambda cb, i: (i, cb), x[2])))
        else:
            in_specs.append(pl.BlockSpec(x[1].shape, lambda i: (0, 0)))
        args.append(x[1])
    out_specs, out_shapes = [], []
    for o in outs:
        if o[0] == 'r':
            out_specs.append(pl.BlockSpec((tr, o[1]), lambda i: (i, 0)))
            out_shapes.append(jax.ShapeDtypeStruct((rows, o[1]), o[2]))
        else:
            out_specs.append(pl.BlockSpec((SUBLANES, o[1]), lambda i: (0, 0)))
            out_shapes.append(jax.ShapeDtypeStruct((SUBLANES, o[1]), F32))
    nin = len(ins)
    has_acc = any(o[0] == 'a' for o in outs)

    def body(*refs):
        vals = fn(*[r[...].astype(F32) for r in refs[:nin]])
        if not isinstance(vals, (tuple, list)):
            vals = (vals,)
        i = pl.program_id(0)
        for o, ref, v in zip(outs, refs[nin:], vals):
            if o[0] == 'r':
                ref[...] = v.astype(ref.dtype)
            else:
                part = v.astype(F32).reshape(tr // SUBLANES, SUBLANES, o[1]).sum(axis=0)

                @pl.when(i == 0)
                def _(ref=ref, part=part):
                    ref[...] = part

                @pl.when(i > 0)
                def _(ref=ref, part=part):
                    ref[...] += part

    res = pl.pallas_call(
        body, name=name, grid=(rows // tr,), in_specs=in_specs, out_specs=out_specs, out_shape=out_shapes,
        compiler_params=_cparams(("arbitrary",) if has_acc else ("parallel",)),
    )(*args)
    return res


def _rms(x, g):
    r = lax.rsqrt(jnp.mean(x * x, axis=-1, keepdims=True) + RMS_EPS)
    xr = x * r
    return xr * g, xr


def _rms_bwd(x, g, dy):
    r = lax.rsqrt(jnp.mean(x * x, axis=-1, keepdims=True) + RMS_EPS)
    xr = x * r
    gdy = g * dy
    dx = r * (gdy - xr * jnp.mean(xr * gdy, axis=-1, keepdims=True))
    return dx, dy * xr


def _sigmoid(x):
    return 1.0 / (1.0 + jnp.exp(-x))


_GELU_C = math.sqrt(2.0 / math.pi)


def _gelu(x):
    return 0.5 * x * (1.0 + jnp.tanh(_GELU_C * (x + 0.044715 * x * x * x)))


def _gelu_grad(x):
    t = jnp.tanh(_GELU_C * (x + 0.044715 * x * x * x))
    return 0.5 * (1.0 + t) + 0.5 * x * (1.0 - t * t) * _GELU_C * (1.0 + 3 * 0.044715 * x * x)


ATTN_ROWS_PER_STEP = 8
NT_DIMS = (((1,), (1,)), ((), ()))
NN_DIMS = (((1,), (0,)), ((), ()))
TN_DIMS = (((0,), (0,)), ((), ()))


def _attn_geometry(r, rows):
    row_start = jnp.clip(r - WIN_H // 2, 0, rows - WIN_H)
    key0 = pl.multiple_of(row_start * GRID_W, GRID_W)
    bias0 = pl.multiple_of((row_start - r + (WIN_H - 1)) * GRID_W, GRID_W)
    return key0, bias0


def _window_onehot():
    c = jnp.arange(GRID_W)
    col_start = jnp.clip(c - WIN_W // 2, 0, GRID_W - WIN_W)
    col_in = (c[None, :] >= col_start[:, None]) & (c[None, :] < col_start[:, None] + WIN_W)
    dc = jnp.clip(c[None, :] - c[:, None], -(WIN_W - 1), WIN_W - 1) + (WIN_W - 1)
    onehot = ((dc[:, :, None] == jnp.arange(2 * WIN_W - 1)[None, None, :]) & col_in[:, :, None]).astype(F32)
    return onehot, col_in


def _bias_table(rpb):
    onehot, col_in = _window_onehot()
    nh = rpb.shape[0]
    tab = jnp.einsum('perd,qkd->prkeq', rpb.reshape(nh // 2, 2, 2 * WIN_H - 1, 2 * WIN_W - 1), onehot, precision=HI)
    tab = tab + jnp.where(col_in, 0.0, NEG_INF).T[None, None, :, None, :]
    return tab.reshape(nh // 2, (2 * WIN_H - 1) * GRID_W, 2 * GRID_W)


def _bias_table_grad(dtab):
    onehot, _ = _window_onehot()
    npair = dtab.shape[0]
    d = dtab.reshape(npair, 2 * WIN_H - 1, GRID_W, 2, GRID_W)
    return jnp.einsum('prkeq,qkd->perd', d, onehot, precision=HI).reshape(2 * npair, 2 * WIN_H - 1, 2 * WIN_W - 1)


def _lane_lo(shape):
    return lax.broadcasted_iota(jnp.int32, shape, 1) < HEAD_DIM


def _half_sums(v):
    lo = _lane_lo(v.shape)
    s_lo = jnp.sum(jnp.where(lo, v, 0.0), axis=1, keepdims=True)
    s_hi = jnp.sum(jnp.where(lo, 0.0, v), axis=1, keepdims=True)
    return jnp.where(lo, s_lo, s_hi)


def _rms_pair(x, g):
    r = lax.rsqrt(_half_sums(x * x) * (1.0 / HEAD_DIM) + RMS_EPS)
    return x * r * g


def _rms_pair_bwd(x, g, dy):
    r = lax.rsqrt(_half_sums(x * x) * (1.0 / HEAD_DIM) + RMS_EPS)
    xr = x * r
    gdy = g * dy
    dx = r * (gdy - xr * (_half_sums(xr * gdy) * (1.0 / HEAD_DIM)))
    return dx, dy * xr


def _blockdiag(a):
    a2 = jnp.concatenate([a, a], axis=0)
    row_hi = lax.broadcasted_iota(jnp.int32, a2.shape, 0) >= GRID_W
    lane_hi = lax.broadcasted_iota(jnp.int32, a2.shape, 1) >= HEAD_DIM
    return jnp.where(row_hi == lane_hi, a2, 0.0).astype(MXU_DTYPE)


def _diag_blocks(m):
    return jnp.where(_lane_lo((GRID_W, 2 * HEAD_DIM)), m[:GRID_W], m[GRID_W:])


def _attn_scores(qb, kb, bias):
    st = lax.dot_general(kb, qb, NT_DIMS, preferred_element_type=F32)
    st = st * (1.0 / math.sqrt(HEAD_DIM)) + bias
    mx = jnp.max(st, axis=0, keepdims=True)
    p = jnp.exp(st - mx)
    return p * (1.0 / jnp.sum(p, axis=0, keepdims=True))


def _attn_fwd(z4, qg2, kg2, bias_t, comm=None, comm_arrs=()):
    _, t, aw = z4.shape
    rows = t // GRID_W
    npair = aw // (2 * HEAD_DIM)
    nkeys = WIN_H * GRID_W
    nb = bias_t.shape[1]
    rps = min(ATTN_ROWS_PER_STEP, rows)
    blk = rps * GRID_W
    nsteps = rows // rps
    ncomm = len(comm_arrs)

    def body(*refs):
        q_ref, k_ref, v_ref, qg_ref, kg_ref, b_ref = refs[:6]
        c_ins, o_ref, c_outs = refs[6:6 + ncomm], refs[6 + ncomm], refs[7 + ncomm:7 + 2 * ncomm]
        kn_ref, vb_ref = refs[7 + 2 * ncomm:9 + 2 * ncomm]
        sems = refs[9 + 2 * ncomm:]
        pr, rb = pl.program_id(0), pl.program_id(1)
        if comm is not None:
            @pl.when((pr == 0) & (rb == 0))
            def _():
                comm.start(c_ins, c_outs, *sems)

        @pl.when(rb == 0)
        def _():
            kn_ref[...] = _rms_pair(k_ref[...], kg_ref[...]).astype(MXU_DTYPE)
            vb_ref[...] = v_ref[...].astype(MXU_DTYPE)

        def row(i, carry):
            key0, bias0 = _attn_geometry(rb * rps + i, rows)
            at = pl.ds(pl.multiple_of(i * GRID_W, GRID_W), GRID_W)
            qb = _blockdiag(_rms_pair(q_ref[at, :], qg_ref[...]))
            pt = _attn_scores(qb, kn_ref[pl.ds(key0, nkeys), :], b_ref[pl.ds(bias0, nkeys), :])
            both = lax.dot_general(pt.astype(MXU_DTYPE), vb_ref[pl.ds(key0, nkeys), :], TN_DIMS,
                                   preferred_element_type=F32)
            o_ref[at, :] = _diag_blocks(both)
            return carry

        lax.fori_loop(0, rps, row, 0, unroll=2)
        if comm is not None:
            @pl.when((pr == npair - 1) & (rb == nsteps - 1))
            def _():
                comm.finish(c_ins, c_outs, *sems)

    pair_cols = lambda lead: pl.BlockSpec((None, t, 2 * HEAD_DIM), lambda p, r: (lead, 0, p))
    res = pl.pallas_call(
        body, name="attn_fwd", grid=(npair, nsteps),
        in_specs=[pl.BlockSpec((None, blk, 2 * HEAD_DIM), lambda p, r: (0, r, p)), pair_cols(1), pair_cols(2),
                  pl.BlockSpec((1, 2 * HEAD_DIM), lambda p, r: (0, 0)),
                  pl.BlockSpec((1, 2 * HEAD_DIM), lambda p, r: (0, 0)),
                  pl.BlockSpec((None, nb, 2 * GRID_W), lambda p, r: (p, 0, 0))] + [ANY] * ncomm,
        out_specs=[pl.BlockSpec((blk, 2 * HEAD_DIM), lambda p, r: (r, p))] + [ANY] * ncomm,
        out_shape=[jax.ShapeDtypeStruct((t, aw), F32)] + (comm.out_shape if comm is not None else []),
        scratch_shapes=[pltpu.VMEM((t, 2 * HEAD_DIM), MXU_DTYPE), pltpu.VMEM((t, 2 * HEAD_DIM), MXU_DTYPE)]
        + (comm.scratch if comm is not None else []),
        compiler_params=_cparams(("arbitrary", "arbitrary")),
    )(z4, z4, z4, qg2, kg2, bias_t, *comm_arrs)
    return res[0], res[1:]


def _attn_bwd(z4, qg2, kg2, bias_t, dya, comm=None, comm_arrs=()):
    _, t, aw = z4.shape
    rows = t // GRID_W
    npair = aw // (2 * HEAD_DIM)
    nkeys = WIN_H * GRID_W
    nb = bias_t.shape[1]
    rps = min(ATTN_ROWS_PER_STEP, rows)
    blk = rps * GRID_W
    nsteps = rows // rps
    scale = 1.0 / math.sqrt(HEAD_DIM)
    ncomm = len(comm_arrs)

    def body(*refs):
        q_ref, k_ref, v_ref, qg_ref, kg_ref, b_ref, do_ref = refs[:7]
        c_ins = refs[7:7 + ncomm]
        dz_ref, db_ref, dqg_ref, dkg_ref = refs[7 + ncomm:11 + ncomm]
        c_outs = refs[11 + ncomm:11 + 2 * ncomm]
        kn_ref, vb_ref, dkn_ref, dv_ref = refs[11 + 2 * ncomm:15 + 2 * ncomm]
        sems = refs[15 + 2 * ncomm:]
        pr, rb = pl.program_id(0), pl.program_id(1)
        if comm is not None:
            @pl.when((pr == 0) & (rb == 0))
            def _():
                comm.start(c_ins, c_outs, *sems)

        @pl.when(rb == 0)
        def _():
            kn_ref[...] = _rms_pair(k_ref[...], kg_ref[...]).astype(MXU_DTYPE)
            vb_ref[...] = v_ref[...].astype(MXU_DTYPE)
            dkn_ref[...] = jnp.zeros_like(dkn_ref)
            dv_ref[...] = jnp.zeros_like(dv_ref)
            db_ref[...] = jnp.zeros_like(db_ref)
            dqg_ref[...] = jnp.zeros_like(dqg_ref)

        def row(i, dqg_sum):
            r = rb * rps + i
            key0, bias0 = _attn_geometry(r, rows)
            keys = pl.ds(key0, nkeys)
            at = pl.ds(pl.multiple_of(i * GRID_W, GRID_W), GRID_W)
            q = q_ref[at, :]
            qb = _blockdiag(_rms_pair(q, qg_ref[...]))
            dob = _blockdiag(do_ref[at, :])
            kb = kn_ref[keys, :]
            pt = _attn_scores(qb, kb, b_ref[pl.ds(bias0, nkeys), :])
            dv_ref[keys, :] += lax.dot_general(pt.astype(MXU_DTYPE), dob, NN_DIMS, preferred_element_type=F32)
            dpt = lax.dot_general(vb_ref[keys, :], dob, NT_DIMS, preferred_element_type=F32)
            dst = pt * (dpt - jnp.sum(pt * dpt, axis=0, keepdims=True))
            db_ref[pl.ds(bias0, nkeys), :] += dst
            dsb = dst.astype(MXU_DTYPE)
            dkn_ref[keys, :] += scale * lax.dot_general(dsb, qb, NN_DIMS, preferred_element_type=F32)
            dqn = scale * _diag_blocks(lax.dot_general(dsb, kb, TN_DIMS, preferred_element_type=F32))
            dq, dqg = _rms_pair_bwd(q, qg_ref[...], dqn)
            dz_ref[0, pl.ds(pl.multiple_of(r * GRID_W, GRID_W), GRID_W), :] = dq.astype(dz_ref.dtype)
            return dqg_sum + jnp.sum(dqg, axis=0, keepdims=True)

        dqg_ref[...] += lax.fori_loop(0, rps, row, jnp.zeros((1, 2 * HEAD_DIM), F32), unroll=2)

        @pl.when(rb == nsteps - 1)
        def _():
            dk, dkg = _rms_pair_bwd(k_ref[...], kg_ref[...], dkn_ref[...])
            dz_ref[1] = dk.astype(dz_ref.dtype)
            dz_ref[2] = dv_ref[...].astype(dz_ref.dtype)
            dkg_ref[...] = jnp.sum(dkg, axis=0, keepdims=True)

        if comm is not None:
            @pl.when((pr == npair - 1) & (rb == nsteps - 1))
            def _():
                comm.finish(c_ins, c_outs, *sems)

    pair_cols = lambda lead: pl.BlockSpec((None, t, 2 * HEAD_DIM), lambda p, r: (lead, 0, p))
    pair_vec = pl.BlockSpec((None, 1, 2 * HEAD_DIM), lambda p, r: (p, 0, 0))
    res = pl.pallas_call(
        body, name="attn_bwd", grid=(npair, nsteps),
        in_specs=[pl.BlockSpec((None, blk, 2 * HEAD_DIM), lambda p, r: (0, r, p)), pair_cols(1), pair_cols(2),
                  pl.BlockSpec((1, 2 * HEAD_DIM), lambda p, r: (0, 0)),
                  pl.BlockSpec((1, 2 * HEAD_DIM), lambda p, r: (0, 0)),
                  pl.BlockSpec((None, nb, 2 * GRID_W), lambda p, r: (p, 0, 0)),
                  pl.BlockSpec((blk, 2 * HEAD_DIM), lambda p, r: (r, p))] + [ANY] * ncomm,
        out_specs=[pl.BlockSpec((3, t, 2 * HEAD_DIM), lambda p, r: (0, 0, p)),
                   pl.BlockSpec((None, nb, 2 * GRID_W), lambda p, r: (p, 0, 0)),
                   pair_vec, pair_vec] + [ANY] * ncomm,
        out_shape=[jax.ShapeDtypeStruct((4, t, aw), MXU_DTYPE), jax.ShapeDtypeStruct(bias_t.shape, F32),
                   jax.ShapeDtypeStruct((npair, 1, 2 * HEAD_DIM), F32),
                   jax.ShapeDtypeStruct((npair, 1, 2 * HEAD_DIM), F32)] + (comm.out_shape if comm is not None else []),
        scratch_shapes=[pltpu.VMEM((t, 2 * HEAD_DIM), MXU_DTYPE), pltpu.VMEM((t, 2 * HEAD_DIM), MXU_DTYPE),
                        pltpu.VMEM((t, 2 * HEAD_DIM), F32), pltpu.VMEM((t, 2 * HEAD_DIM), F32)]
        + (comm.scratch if comm is not None else []),
        compiler_params=_cparams(("arbitrary", "arbitrary")),
    )(z4, z4, z4, qg2, kg2, bias_t, dya, *comm_arrs)
    return res[:4], res[4:]


def _s5_mats(a_re, a_im, b_re, b_im, c_re, c_im, log_step, d_skip):
    nd, g, p = a_re.shape
    c = b_re.shape[-1]
    L = S5_CHUNK
    lr = jnp.minimum(a_re, -1e-4).transpose(1, 0, 2)
    li = a_im.transpose(1, 0, 2)
    dt = jnp.exp(log_step).T[..., None]
    n = jnp.arange(L + 1, dtype=F32)[None, :, None, None]
    mag = jnp.exp(n * (lr * dt)[:, None])
    ang = n * (li * dt)[:, None]
    pw_r, pw_i = mag * jnp.cos(ang), mag * jnp.sin(ang)
    den = lr * lr + li * li
    nr, ni = pw_r[:, 1] - 1.0, pw_i[:, 1]
    cr, ci = (nr * lr + ni * li) / den, (ni * lr - nr * li) / den
    bt_r, bt_i = b_re.transpose(1, 3, 0, 2), b_im.transpose(1, 3, 0, 2)
    bb_r = cr[:, None] * bt_r - ci[:, None] * bt_i
    bb_i = cr[:, None] * bt_i + ci[:, None] * bt_r
    ct_r, ct_i = c_re.transpose(1, 2, 0, 3), c_im.transpose(1, 2, 0, 3)

    def cols(x_re, x_im):
        return jnp.concatenate([x_re[..., 0, :], x_re[..., 1, :], x_im[..., 0, :], x_im[..., 1, :]], axis=-1)

    e_r = jnp.stack([pw_r[:, :L, 0][:, ::-1], pw_r[:, :L, 1]], axis=2)
    e_i = jnp.stack([pw_i[:, :L, 0][:, ::-1], pw_i[:, :L, 1]], axis=2)
    ws = (cols(e_r, e_r)[:, :, None] * cols(bb_r, bb_i)[:, None]
          + cols(e_i, e_i)[:, :, None] * cols(-bb_i, bb_r)[:, None]).reshape(g, L * c, 4 * p)
    f_r = jnp.stack([pw_r[:, 1:, 0], pw_r[:, 1:, 1][:, ::-1]], axis=2)
    f_i = jnp.stack([pw_i[:, 1:, 0], pw_i[:, 1:, 1][:, ::-1]], axis=2)
    wot = (cols(f_r, f_i)[:, :, None] * cols(ct_r, -ct_r)[:, None]
           + cols(f_i, f_r)[:, :, None] * cols(-ct_i, -ct_i)[:, None]).reshape(g, L * c, 4 * p)
    qr, qi = pw_r[:, :L, None], pw_i[:, :L, None]
    kp_r, kp_i = qr * bb_r[:, None] - qi * bb_i[:, None], qr * bb_i[:, None] + qi * bb_r[:, None]
    kern = [jnp.einsum('gnip,gop->gino', kp_r[:, :, :, d], ct_r[:, :, d], precision=HI)
            - jnp.einsum('gnip,gop->gino', kp_i[:, :, :, d], ct_i[:, :, d], precision=HI) for d in range(2)]
    skip = d_skip.reshape(g, c, 1, 1) * jnp.eye(c, dtype=F32)[None, :, None, :]
    by_offset = jnp.concatenate([kern[1][:, :, :0:-1], kern[0][:, :, :1] + kern[1][:, :, :1] + skip,
                                 kern[0][:, :, 1:]], axis=2).reshape(g, c, (2 * L - 1) * c)
    mt = jnp.stack([by_offset[:, :, (L - 1 - j) * c:(2 * L - 1 - j) * c] for j in range(L)], axis=1)
    mt = mt.reshape(g, L * c, L * c)
    lr16, li16 = pw_r[:, L], pw_i[:, L]
    fa = jnp.concatenate([lr16[:, 0], lr16[:, 1], lr16[:, 0], lr16[:, 1]], axis=-1)
    fb = jnp.concatenate([-li16[:, 0], -li16[:, 1], li16[:, 0], li16[:, 1]], axis=-1)
    return mt, ws, wot, fa, fb


def _gmm(name, a, b, contract, a_stacked=False, b_stacked=False, o_stacked=False, add=None, out_dtype=F32):
    w = S5_CHUNK * SSM_GROUP_CH
    g = (a.shape[0] if a_stacked else a.shape[1] // w)
    gpb = math.gcd(g, S5_GROUPS_PER_STEP)
    dn = {'nn': (((1,), (0,)), ((), ())), 'nt': (((1,), (1,)), ((), ())), 'tn': (((0,), (0,)), ((), ()))}[contract]

    def spec(arr, stacked):
        if stacked:
            return pl.BlockSpec((gpb,) + arr.shape[1:], lambda i: (i, 0, 0))
        return pl.BlockSpec((arr.shape[0], gpb * w), lambda i: (0, i))

    def take(ref, stacked, e):
        return ref[e] if stacked else ref[:, e * w:(e + 1) * w]

    m = (a.shape[1] if a_stacked else a.shape[0]) if contract != 'tn' else w
    n = w
    if o_stacked:
        o_spec = pl.BlockSpec((gpb, m, n), lambda i: (i, 0, 0))
        o_shape = (g, m, n)
    else:
        o_spec = pl.BlockSpec((m, gpb * n), lambda i: (0, i))
        o_shape = (m, g * n)
    has_add = add is not None

    def body(*refs):
        if has_add:
            a_ref, b_ref, add_ref, o_ref = refs
        else:
            a_ref, b_ref, o_ref = refs
        for e in range(gpb):
            r = lax.dot_general(take(a_ref, a_stacked, e).astype(S5_DTYPE), take(b_ref, b_stacked, e).astype(S5_DTYPE),
                                dn, precision=HI if S5_DTYPE == F32 else None, preferred_element_type=F32)
            if has_add:
                r = r + take(add_ref, o_stacked, e)
            if o_stacked:
                o_ref[e] = r.astype(o_ref.dtype)
            else:
                o_ref[:, e * w:(e + 1) * w] = r.astype(o_ref.dtype)

    in_specs = [spec(a, a_stacked), spec(b, b_stacked)] + ([o_spec] if has_add else [])
    return pl.pallas_call(
        body, name=name, grid=(g // gpb,), in_specs=in_specs, out_specs=o_spec,
        out_shape=jax.ShapeDtypeStruct(o_shape, out_dtype), compiler_params=_cparams(("parallel",)),
    )(*((a, b) + ((add,) if has_add else ())))


def _s5_scan(name, s, fa, fb, rev0, xin=None):
    nk, g, w = s.shape
    hw, qw = w // 2, w // 4
    gb = min(g, 16)
    with_acc = xin is not None

    def body(*refs):
        if with_acc:
            s_ref, a_ref, b_ref, x_ref, o_ref, pa_ref, pb_ref = refs
        else:
            s_ref, a_ref, b_ref, o_ref = refs
        fa_v, fb_v = a_ref[...], b_ref[...]
        dir0 = lax.broadcasted_iota(jnp.int32, (gb, w), 1) % hw < qw
        swap = lambda v: jnp.concatenate([v[:, hw:], v[:, :hw]], axis=1)

        def step(i, carry):
            x, pa, pb = carry
            k0 = (nk - 1 - i) if rev0 else i
            k1 = i if rev0 else (nk - 1 - i)
            for lo in (0, hw):
                o_ref[k0, :, lo:lo + qw] = x[:, lo:lo + qw]
                o_ref[k1, :, lo + qw:lo + hw] = x[:, lo + qw:lo + hw]
            if with_acc:
                xi = jnp.where(dir0, x_ref[k0], x_ref[k1])
                pa = pa + x * xi
                pb = pb + x * swap(xi)
            x = fa_v * x + fb_v * swap(x) + jnp.where(dir0, s_ref[k0], s_ref[k1])
            return x, pa, pb

        z = jnp.zeros((gb, w), F32)
        res = lax.fori_loop(0, nk, step, (z, z, z), unroll=2)
        if with_acc:
            pa_ref[...] = res[1]
            pb_ref[...] = res[2]

    seq = pl.BlockSpec((nk, gb, w), lambda i: (0, i, 0))
    vec = pl.BlockSpec((gb, w), lambda i: (i, 0))
    in_specs = [seq, vec, vec] + ([seq] if with_acc else [])
    out_specs = [seq] + ([vec, vec] if with_acc else [])
    out_shape = [jax.ShapeDtypeStruct((nk, g, w), F32)] + (
        [jax.ShapeDtypeStruct((g, w), F32)] * 2 if with_acc else [])
    return pl.pallas_call(
        body, name=name, grid=(g // gb,), in_specs=in_specs, out_specs=out_specs, out_shape=out_shape,
        compiler_params=_cparams(("parallel",)),
    )(*((s, fa, fb) + ((xin,) if with_acc else ())))


def _to_groups(u):
    t, sw = u.shape
    g = sw // SSM_GROUP_CH
    return u.reshape(t // S5_CHUNK, S5_CHUNK, g, SSM_GROUP_CH).transpose(0, 2, 1, 3).reshape(t // S5_CHUNK, -1)


def _from_groups(y, sw):
    nk = y.shape[0]
    g = sw // SSM_GROUP_CH
    return y.reshape(nk, g, S5_CHUNK, SSM_GROUP_CH).transpose(0, 2, 1, 3).reshape(nk * S5_CHUNK, sw)


def _s5_fwd(u2, mats):
    mt, ws, wot, fa, fb = mats
    nk = u2.shape[0]
    g = mt.shape[0]
    y_intra = _gmm("s5_intra", u2, mt, 'nn', b_stacked=True)
    s = _gmm("s5_chunk_state", u2, ws, 'nn', b_stacked=True)
    (xin,) = _s5_scan("s5_scan", s.reshape(nk, g, -1), fa, fb, False)
    xin = xin.reshape(nk, -1)
    return _gmm("s5_inter", xin, wot, 'nt', b_stacked=True, add=y_intra, out_dtype=S5_DTYPE), xin


def _s5_bwd(u2, xin, mats, dy2):
    mt, ws, wot, fa, fb = mats
    nk = u2.shape[0]
    g = mt.shape[0]
    dxin = _gmm("s5_dxin", dy2, wot, 'nn', b_stacked=True)
    ds, pa, pb = _s5_scan("s5_scan_adj", dxin.reshape(nk, g, -1), fa, -fb, True, xin=xin.reshape(nk, g, -1))
    ds = ds.reshape(nk, -1)
    du_a = _gmm("s5_du_intra", dy2, mt, 'nt', b_stacked=True)
    du2 = _gmm("s5_du_state", ds, ws, 'nt', b_stacked=True, add=du_a, out_dtype=S5_DTYPE)
    dmt = _gmm("s5_dmt", u2, dy2, 'tn', o_stacked=True)
    dws = _gmm("s5_dws", u2, ds, 'tn', o_stacked=True)
    dwot = _gmm("s5_dwot", dy2, xin, 'tn', o_stacked=True)
    return du2, (dmt, dws, dwot, pa, pb)


def _late_weights(gathered):
    w_glu, w_out, w_gate, w_up, w_down = (g4.reshape((N_CHIPS, -1, g4.shape[-1])) for g4 in gathered)
    return w_glu.reshape(-1, w_glu.shape[-1]), w_out.reshape(-1, w_out.shape[-1]), w_gate, w_up, w_down


def _local_step(x, target, w_in4, late, small, reduce_late=None):
    t, d = x.shape
    aw = w_in4.shape[2]
    sw = aw
    nh = aw // HEAD_DIM
    row = lambda v: v.reshape(1, -1)
    g_mix, g_ffn = row(small['g_mix']), row(small['g_ffn'])
    g_oa, g_os, b_glu = row(small['g_out_attn']), row(small['g_out_ssm']), row(small['b_glu'])
    qg2 = jnp.tile(row(small['q_gain']), (1, 2))
    kg2 = jnp.tile(row(small['k_gain']), (1, 2))

    (h,) = _ew("rms_mix", lambda xv, g: _rms(xv, g)[0], [('r', x), ('c', g_mix)], [('r', d, MXU_DTYPE)])
    z4 = _mm("in_proj", h, w_in4, contract='nn', b_mode='b', o_mode='b')
    bias_t = _bias_table(small['rpb'])
    if late[0] == 'halves':
        ya, gathered = _attn_fwd(z4, qg2, kg2, bias_t, comm=_GatherChips(late[1]), comm_arrs=late[1])
        w_glu, w_out, w_gate4, w_up4, w_down4 = _late_weights(gathered)
    else:
        ya, _ = _attn_fwd(z4, qg2, kg2, bias_t)
        w_glu, w_out, w_gate4, w_up4, w_down4 = late[1]
    ffs = w_gate4.shape[2]
    s5_params = tuple(small[n] for n in ('ssm_a_re', 'ssm_a_im', 'ssm_b_re', 'ssm_b_im', 'ssm_c_re', 'ssm_c_im',
                                         'ssm_log_step', 'ssm_d'))
    mats, mats_vjp = jax.vjp(_s5_mats, *s5_params)
    mats = tuple(m.astype(S5_DTYPE) for m in mats[:3]) + mats[3:]
    u2 = _to_groups(z4[3].astype(S5_DTYPE))
    ypre2, xin = _s5_fwd(u2, mats)
    ypre = _from_groups(ypre2, sw)
    (yb,) = _ew("gelu", _gelu, [('r', ypre)], [('r', sw, MXU_DTYPE)])
    a_glu = _mm("glu_proj", yb, w_glu, contract='nn')

    def mix_out(yav, ypv, av, bg, goa, gos):
        ys = _gelu(ypv) * _sigmoid(av + bg)
        return jnp.concatenate([_rms(yav, goa)[0], _rms(ys, gos)[0]], axis=1)
    (ycat,) = _ew("mix_out", mix_out, [('r', ya), ('r', ypre), ('r', a_glu), ('c', b_glu), ('c', g_oa), ('c', g_os)],
                  [('r', aw + sw, MXU_DTYPE)])
    x1 = _mm("out_proj", ycat, w_out, contract='nn', add=x)
    (h2,) = _ew("rms_ffn", lambda xv, g: _rms(xv, g)[0], [('r', x1), ('c', g_ffn)], [('r', d, MXU_DTYPE)])
    gate4 = _mm("ffn_gate", h2, w_gate4, contract='nn', b_mode='b', o_mode='b', tn=ffs)
    up4 = _mm("ffn_up", h2, w_up4, contract='nn', b_mode='b', o_mode='b', tn=ffs)
    gate_f, up_f = gate4.reshape(4 * t, ffs), up4.reshape(4 * t, ffs)
    (act,) = _ew("swiglu", lambda gv, uv: gv * _sigmoid(gv) * uv, [('r', gate_f), ('r', up_f)],
                 [('r', ffs, MXU_DTYPE)])
    act4 = act.reshape(4, t, ffs)
    x2 = _mm("ffn_down", act4, w_down4, contract='nn', a_mode='c', b_mode='c', add=x1, tk=ffs)

    def loss_fn(xv, tv):
        diff = xv - tv
        return diff * (1.0 / d), diff * (1.0 / d), diff * diff
    dx2, dx2_b, sq = _ew("loss", loss_fn, [('r', x2), ('r', target)], [('r', d, F32), ('r', d, MXU_DTYPE), ('a', d)])

    dact4 = _mm("ffn_down_dx", dx2_b, w_down4, contract='nt', b_mode='b', o_mode='b', tn=ffs)
    d_w_down4 = _mm("ffn_down_dw", act4, dx2_b, contract='tn', a_mode='b', o_mode='b', tm=ffs, out_dtype=DW_DTYPE)

    def swiglu_bwd(dav, gv, uv):
        s = _sigmoid(gv)
        return dav * uv * s * (1.0 + gv * (1.0 - s)), dav * gv * s
    dgate, dup = _ew("swiglu_bwd", swiglu_bwd, [('r', dact4.reshape(4 * t, ffs)), ('r', gate_f), ('r', up_f)],
                     [('r', ffs, MXU_DTYPE), ('r', ffs, MXU_DTYPE)])
    dgate4, dup4 = dgate.reshape(4, t, ffs), dup.reshape(4, t, ffs)
    dh2 = _mm("ffn_gate_dx", dgate4, w_gate4, contract='nt', a_mode='c', b_mode='c', tk=ffs)
    dh2 = _mm("ffn_up_dx", dup4, w_up4, contract='nt', a_mode='c', b_mode='c', add=dh2, tk=ffs)
    d_w_gate4 = _mm("ffn_gate_dw", h2, dgate4, contract='tn', b_mode='b', o_mode='b', tn=ffs, out_dtype=DW_DTYPE)
    d_w_up4 = _mm("ffn_up_dw", h2, dup4, contract='tn', b_mode='b', o_mode='b', tn=ffs, out_dtype=DW_DTYPE)

    def rms_res_bwd(xv, g, dyv, resv):
        dx, dg = _rms_bwd(xv, g, dyv)
        return resv + dx, dg
    dx1, d_g_ffn = _ew("rms_ffn_bwd", rms_res_bwd, [('r', x1), ('c', g_ffn), ('r', dh2), ('r', dx2)],
                       [('r', d, F32), ('a', d)])

    dycat = _mm("out_proj_dx", dx1, w_out, contract='nt')
    d_w_out = _mm("out_proj_dw", ycat, dx1, contract='tn', out_dtype=DW_DTYPE)

    def mix_out_bwd(yav, ypv, av, bg, goa, gos, dca, dcs):
        dya, dgoa = _rms_bwd(yav, goa, dca)
        y = _gelu(ypv)
        s = _sigmoid(av + bg)
        dys, dgos = _rms_bwd(y * s, gos, dcs)
        da = dys * y * s * (1.0 - s)
        return dya, da, dys * s, dgoa, dgos, da
    dya, da, dy_direct, d_g_oa, d_g_os, d_b_glu = _ew(
        "mix_out_bwd", mix_out_bwd,
        [('r', ya), ('r', ypre), ('r', a_glu), ('c', b_glu), ('c', g_oa), ('c', g_os),
         ('r', dycat, 0, aw), ('r', dycat, 1, sw)],
        [('r', aw, F32), ('r', sw, MXU_DTYPE), ('r', sw, F32), ('a', aw), ('a', sw), ('a', sw)])
    dy = _mm("glu_proj_dx", da, w_glu, contract='nt', add=dy_direct)
    d_w_glu = _mm("glu_proj_dw", yb, da, contract='tn', out_dtype=DW_DTYPE)
    (dypre,) = _ew("gelu_bwd", lambda dyv, ypv: dyv * _gelu_grad(ypv), [('r', dy), ('r', ypre)],
                   [('r', sw, S5_DTYPE)])

    du2, dmats = _s5_bwd(u2, xin, mats, _to_groups(dypre))
    d_s5 = mats_vjp(dmats)
    du = _from_groups(du2, sw)
    d_late = (d_w_glu, d_w_out, d_w_gate4, d_w_up4, d_w_down4)
    if reduce_late is not None:
        sums = reduce_late(d_late)
        (dz4, dbias_t, dqg, dkg), scattered = _attn_bwd(z4, qg2, kg2, bias_t, dya, comm=_ScatterChips(sums),
                                                       comm_arrs=sums)
        d_late = (sums, list(scattered))
    else:
        (dz4, dbias_t, dqg, dkg), _ = _attn_bwd(z4, qg2, kg2, bias_t, dya)
    d_rpb = _bias_table_grad(dbias_t)
    fold = lambda v: v.reshape(-1, 2, HEAD_DIM).sum(axis=(0, 1))
    dz4 = dz4.at[3].set(du)

    dh = _mm("in_proj_dx", dz4, w_in4, contract='nt', a_mode='c', b_mode='c')
    d_w_in4 = _mm("in_proj_dw", h, dz4, contract='tn', b_mode='b', o_mode='b', out_dtype=DW_DTYPE)
    dx, d_g_mix = _ew("rms_mix_bwd", rms_res_bwd, [('r', x), ('c', g_mix), ('r', dh), ('r', dx1)],
                      [('r', d, F32), ('a', d)])

    colsum = lambda v: v.sum(axis=0)
    d_small = {
        'g_mix': colsum(d_g_mix), 'q_gain': fold(dqg), 'k_gain': fold(dkg), 'rpb': d_rpb,
        'ssm_a_re': d_s5[0], 'ssm_a_im': d_s5[1], 'ssm_b_re': d_s5[2], 'ssm_b_im': d_s5[3],
        'ssm_c_re': d_s5[4], 'ssm_c_im': d_s5[5], 'ssm_log_step': d_s5[6], 'ssm_d': d_s5[7],
        'b_glu': colsum(d_b_glu), 'g_out_attn': colsum(d_g_oa), 'g_out_ssm': colsum(d_g_os), 'g_ffn': colsum(d_g_ffn),
    }
    return jnp.sum(sq), dx, d_late, d_w_in4, d_small


ANY = pl.BlockSpec(memory_space=pl.ANY)


def _place():
    x, y, c = lax.axis_index("x"), lax.axis_index("y"), lax.axis_index("c")
    other_chips = [(1 - x, y), (x, 1 - y), (1 - x, 1 - y)]
    return x, y, c, 2 * x + y, (x, y, 1 - c), other_chips


class _GatherChips:
    KINDS = 7

    def __init__(self, arrs):
        self.n = len(arrs)
        self.out_shape = [jax.ShapeDtypeStruct((N_CHIPS,) + a.shape, a.dtype) for a in arrs]
        self.scratch = [pltpu.SemaphoreType.DMA((self.n, self.KINDS)), pltpu.SemaphoreType.DMA((self.n, self.KINDS))]

    def _copies(self, ins, outs, send_sems, recv_sems):
        x, y, c, me, sibling, chips = _place()

        def remote(a, k, src, dst, to):
            return lambda: pltpu.make_async_remote_copy(src_ref=src, dst_ref=dst, send_sem=send_sems.at[a, k],
                                                        recv_sem=recv_sems.at[a, k], device_id=to, device_id_type=MESH)
        own, out, landed, passed, theirs = [], [], [], [], []
        for a in range(self.n):
            own.append(remote(a, 6, ins[a], outs[a].at[me], sibling))
            for j, (px, py) in enumerate(chips):
                there, here = outs[a].at[2 * px + py, c], outs[a].at[2 * px + py, 1 - c]
                out.append(remote(a, j, ins[a].at[c], outs[a].at[me, c], (px, py, c)))
                landed.append(remote(a, j, there, there, (px, py, c)))
                passed.append(remote(a, 3 + j, there, there, sibling))
                theirs.append(remote(a, 3 + j, here, here, sibling))
        return own, out, landed, passed, theirs

    def start(self, ins, outs, send_sems, recv_sems):
        own, out, _, _, _ = self._copies(ins, outs, send_sems, recv_sems)
        for make in own + out:
            make().start()

    def finish(self, ins, outs, send_sems, recv_sems):
        own, out, landed, passed, theirs = self._copies(ins, outs, send_sems, recv_sems)
        for arrived, onward in zip(landed, passed):
            arrived().wait_recv()
            onward().start()
        for make in theirs + own:
            make().wait_recv()
        for make in own + out + passed:
            make().wait_send()


class _ScatterChips:
    def __init__(self, sums):
        self.n = len(sums)
        self.out_shape = [jax.ShapeDtypeStruct(s.shape, s.dtype) for s in sums]
        self.scratch = [pltpu.SemaphoreType.DMA((self.n, 3)), pltpu.SemaphoreType.DMA((self.n, 3))]

    def _copies(self, ins, outs, send_sems, recv_sems):
        x, y, c, me, sibling, chips = _place()
        out, landed = [], []

        def remote(a, j, src, dst, to):
            return lambda: pltpu.make_async_remote_copy(src_ref=src, dst_ref=dst, send_sem=send_sems.at[a, j],
                                                        recv_sem=recv_sems.at[a, j], device_id=to, device_id_type=MESH)
        for a in range(self.n):
            for j, (px, py) in enumerate(chips):
                slot = outs[a].at[2 * px + py]
                out.append(remote(a, j, ins[a].at[2 * px + py], outs[a].at[me], (px, py, c)))
                landed.append(remote(a, j, slot, slot, (px, py, c)))
        return out, landed

    def start(self, ins, outs, send_sems, recv_sems):
        for make in self._copies(ins, outs, send_sems, recv_sems)[0]:
            make().start()

    def finish(self, ins, outs, send_sems, recv_sems):
        out, landed = self._copies(ins, outs, send_sems, recv_sems)
        for make in landed:
            make().wait_recv()
        for make in out:
            make().wait_send()


def _comm_call(name, comm, arrs):
    n = comm.n

    def body(*refs):
        parts = (refs[:n], refs[n:2 * n]) + tuple(refs[2 * n:])
        comm.start(*parts)
        comm.finish(*parts)

    return pl.pallas_call(body, name=name, in_specs=[ANY] * n, out_specs=[ANY] * n, out_shape=comm.out_shape,
                          scratch_shapes=comm.scratch)(*arrs)


def _gather_chips(name, arrs):
    return _comm_call(name, _GatherChips(arrs), arrs)


def _swap_halves(name, parts):
    n = len(parts)

    def body(*refs):
        ins, outs = refs[:n], refs[n:2 * n]
        send_sems, recv_sems = refs[2 * n:]
        x, y, c, me, sibling, chips = _place()
        cps = []
        for a in range(n):
            cp = pltpu.make_async_remote_copy(src_ref=ins[a].at[:, 1 - c], dst_ref=outs[a], send_sem=send_sems.at[a],
                                              recv_sem=recv_sems.at[a], device_id=sibling, device_id_type=MESH)
            cp.start()
            cps.append(cp)
        for cp in cps:
            cp.wait()

    return pl.pallas_call(
        body, name=name, in_specs=[ANY] * n, out_specs=[ANY] * n,
        out_shape=[jax.ShapeDtypeStruct((N_CHIPS,) + p.shape[2:], p.dtype) for p in parts],
        scratch_shapes=[pltpu.SemaphoreType.DMA((n,)), pltpu.SemaphoreType.DMA((n,))],
    )(*parts)


def _scatter_chips(name, sums):
    return _comm_call(name, _ScatterChips(sums), sums)


def _swap_reduced(name, halves):
    n = len(halves)

    def body(*refs):
        ins, outs = refs[:n], refs[n:2 * n]
        send_sems, recv_sems = refs[2 * n:]
        x, y, c, me, sibling, chips = _place()
        cps = []
        for a in range(n):
            cp = pltpu.make_async_remote_copy(src_ref=ins[a], dst_ref=outs[a], send_sem=send_sems.at[a],
                                              recv_sem=recv_sems.at[a], device_id=sibling, device_id_type=MESH)
            cp.start()
            cps.append(cp)
        for cp in cps:
            cp.wait()

    return pl.pallas_call(
        body, name=name, in_specs=[ANY] * n, out_specs=[ANY] * n,
        out_shape=[jax.ShapeDtypeStruct(h.shape, h.dtype) for h in halves],
        scratch_shapes=[pltpu.SemaphoreType.DMA((n,)), pltpu.SemaphoreType.DMA((n,))],
    )(*halves)


def _row_tile(r, want=256):
    t = (min(r, want) // SUBLANES) * SUBLANES
    while r % t:
        t -= SUBLANES
    return t


def _add_own_half(name, part, got, c, out_dtype):
    _, _, r, cols = part.shape
    tr = _row_tile(r)

    def body(c_ref, p_ref, g_ref, o_ref):
        o_ref[...] = (p_ref[...].astype(F32) + g_ref[...].astype(F32)).astype(o_ref.dtype)

    return pl.pallas_call(
        body, name=name,
        grid_spec=pltpu.PrefetchScalarGridSpec(
            num_scalar_prefetch=1, grid=(N_CHIPS, r // tr),
            in_specs=[pl.BlockSpec((None, None, tr, cols), lambda s, i, c_ref: (s, c_ref[0], i, 0)),
                      pl.BlockSpec((None, tr, cols), lambda s, i, c_ref: (s, i, 0))],
            out_specs=pl.BlockSpec((None, tr, cols), lambda s, i, c_ref: (s, i, 0))),
        out_shape=jax.ShapeDtypeStruct(got.shape, out_dtype),
        compiler_params=_cparams(("parallel", "parallel")),
    )(c.reshape(1).astype(jnp.int32), part, got)


def _sum_chips(name, got, own, me):
    _, r, cols = got.shape
    tr = _row_tile(r)

    def body(me_ref, r0, r1, r2, r3, own_ref, o_ref):
        pick = lambda s, ref: jnp.where(me_ref[0] == s, own_ref[...], ref[...]).astype(F32)
        o_ref[...] = ((pick(0, r0) + pick(1, r1)) + pick(2, r2)) + pick(3, r3)

    def slot(s):
        return pl.BlockSpec((None, tr, cols),
                            lambda i, me_ref: (jnp.where(me_ref[0] == s, (s + 1) % N_CHIPS, s), i, 0))

    return pl.pallas_call(
        body, name=name,
        grid_spec=pltpu.PrefetchScalarGridSpec(
            num_scalar_prefetch=1, grid=(r // tr,),
            in_specs=[slot(s) for s in range(N_CHIPS)]
            + [pl.BlockSpec((None, tr, cols), lambda i, me_ref: (me_ref[0], i, 0))],
            out_specs=pl.BlockSpec((tr, cols), lambda i, me_ref: (i, 0))),
        out_shape=jax.ShapeDtypeStruct((r, cols), F32),
        compiler_params=_cparams(("parallel",)),
    )(me.reshape(1).astype(jnp.int32), got, got, got, got, own)


def _adamw_math(wv, gv, mv, vv):
    mv = ADAM_B1 * mv + (1.0 - ADAM_B1) * gv
    vv = ADAM_B2 * vv + (1.0 - ADAM_B2) * (gv * gv)
    m_hat = mv / (1.0 - ADAM_B1 ** ADAM_STEP)
    v_hat = vv / (1.0 - ADAM_B2 ** ADAM_STEP)
    return -ADAM_LR * (m_hat / (jnp.sqrt(v_hat) + ADAM_EPS) + ADAM_WD * wv), mv, vv


def _adamw(name, w, g, m, v):
    cols = w.shape[1]
    return _ew(name, _adamw_math, [('r', w), ('r', g), ('r', m), ('r', v)], [('r', cols, F32)] * 3,
               tr=_row_tile(w.shape[0], 128))


def _adamw_halves(name, w, mine, theirs, m, v, c):
    r, cols = mine.shape
    tr = _row_tile(r, 128)
    nb = r // tr

    def body(c_ref, w_ref, a_ref, b_ref, m_ref, v_ref, g_out, d_out, m_out, v_out):
        g = jnp.where(pl.program_id(0) == c_ref[0], a_ref[...], b_ref[...])
        g_out[...] = g
        d_out[...], m_out[...], v_out[...] = _adamw_math(w_ref[...], g, m_ref[...], v_ref[...])

    whole = pl.BlockSpec((tr, cols), lambda h, i, c_ref: (h * nb + i, 0))
    half = pl.BlockSpec((tr, cols), lambda h, i, c_ref: (i, 0))
    return pl.pallas_call(
        body, name=name,
        grid_spec=pltpu.PrefetchScalarGridSpec(
            num_scalar_prefetch=1, grid=(2, nb),
            in_specs=[whole, half, half, whole, whole], out_specs=[whole] * 4),
        out_shape=[jax.ShapeDtypeStruct(w.shape, F32)] * 4,
        compiler_params=_cparams(("parallel", "parallel")),
    )(c.reshape(1).astype(jnp.int32), w, mine, theirs, m, v)


SMALL_ROWS_ALIGN = 2 * N_CHIPS * SUBLANES


MEDIUM_NAMES = ['ssm_b_re', 'ssm_b_im', 'ssm_c_re', 'ssm_c_im']
PACKED_NAMES = [n for n in SMALL_NAMES if n not in MEDIUM_NAMES]


def _pack_small(d):
    flat = jnp.concatenate([d[n].reshape(-1).astype(F32) for n in PACKED_NAMES])
    rows = -(-flat.shape[0] // (LANES * SMALL_ROWS_ALIGN)) * SMALL_ROWS_ALIGN
    return jnp.pad(flat, (0, rows * LANES - flat.shape[0])).reshape(rows, LANES)


def _unpack_small(packed, like):
    flat = packed.reshape(-1)
    out, off = {}, 0
    for n in PACKED_NAMES:
        size = like[n].size
        out[n] = flat[off:off + size].reshape(like[n].shape)
        off += size
    return out


def kernel(x, g_mix, w_in, q_gain, k_gain, rpb, ssm_a_re, ssm_a_im, ssm_b_re, ssm_b_im, ssm_c_re, ssm_c_im, ssm_log_step, ssm_d, w_glu, b_glu, g_out_attn, g_out_ssm, w_out, g_ffn, w_ffn_gate, w_ffn_up, w_ffn_down, loss_target, m_g_mix, m_w_in, m_q_gain, m_k_gain, m_rpb, m_ssm_a_re, m_ssm_a_im, m_ssm_b_re, m_ssm_b_im, m_ssm_c_re, m_ssm_c_im, m_ssm_log_step, m_ssm_d, m_w_glu, m_b_glu, m_g_out_attn, m_g_out_ssm, m_w_out, m_g_ffn, m_w_ffn_gate, m_w_ffn_up, m_w_ffn_down, v_g_mix, v_w_in, v_q_gain, v_k_gain, v_rpb, v_ssm_a_re, v_ssm_a_im, v_ssm_b_re, v_ssm_b_im, v_ssm_c_re, v_ssm_c_im, v_ssm_log_step, v_ssm_d, v_w_glu, v_b_glu, v_g_out_attn, v_g_out_ssm, v_w_out, v_g_ffn, v_w_ffn_gate, v_w_ffn_up, v_w_ffn_down):
    given = dict(locals())
    w = {n: given[n][0] for n in WEIGHT_NAMES}
    mom = {n: given["m_" + n][0] for n in WEIGHT_NAMES}
    var = {n: given["v_" + n][0] for n in WEIGHT_NAMES}
    d = x.shape[-1]
    c = lax.axis_index("c")

    halves = {n: w[n].astype(MXU_DTYPE).reshape((2, w[n].shape[0] // 2, w[n].shape[1])) for n in BIG_NAMES}
    (w_in4,) = _gather_chips("gather_w_in", [halves['w_in']])
    w_in4 = w_in4.reshape((N_CHIPS, -1, w_in4.shape[-1]))

    def chip_sums(tag, grads, payload):
        parts = [g.reshape((N_CHIPS, 2, -1, g.shape[-1])) for g in grads]
        got = _swap_halves("reduce_swap_halves_" + tag, parts)
        return [_add_own_half("reduce_add_%s_%d" % (tag, a), p, gt, c, dt)
                for a, (p, gt, dt) in enumerate(zip(parts, got, payload))]

    reduce_late = lambda grads: chip_sums("late", grads, [GRAD_PAYLOAD_DTYPE] * len(grads))
    sq, dx, (sums_late, got_late), d_w_in4, d_small = _local_step(
        x[0], loss_target[0], w_in4, ('halves', [halves[n] for n in LATE_NAMES]), {n: w[n] for n in SMALL_NAMES},
        reduce_late)
    loss = lax.psum(0.5 * sq / d, ("x", "y", "c"))

    nbig = len(BIG_NAMES)
    rest = [d_w_in4] + [d_small[n].reshape(-1, LANES) for n in MEDIUM_NAMES] + [_pack_small(d_small)]
    sums_rest = chip_sums("rest", rest, [GRAD_PAYLOAD_DTYPE] + [F32] * (len(rest) - 1))
    got_rest = list(_scatter_chips("reduce_scatter_chips", sums_rest))
    sums = sums_rest[:1] + sums_late + sums_rest[1:]
    got = got_rest[:1] + got_late + got_rest[1:]
    me = 2 * lax.axis_index("x") + lax.axis_index("y")
    mine = [_sum_chips("reduce_sum_%d" % a, gt, sm_, me) for a, (gt, sm_) in enumerate(zip(got, sums))]
    theirs = _swap_reduced("reduce_swap_reduced", mine)
    in_order = lambda a: jnp.where(c == 0, jnp.stack([mine[a], theirs[a]]), jnp.stack([theirs[a], mine[a]]))
    repl = _gather_chips("gather_small", [in_order(a) for a in range(nbig, len(mine))])
    repl = [r.reshape(-1, LANES) for r in repl]
    like = {n: w[n] for n in SMALL_NAMES}
    grad_small = _unpack_small(repl[-1], like)
    grad_small.update({n: r.reshape(w[n].shape) for n, r in zip(MEDIUM_NAMES, repl)})

    grad_big, delta, new_m, new_v = {}, {}, {}, {}
    for a, n in enumerate(BIG_NAMES):
        grad_big[n], delta[n], new_m[n], new_v[n] = _adamw_halves("adamw_%d" % a, w[n], mine[a], theirs[a],
                                                                  mom[n], var[n], c)
    for n, r in zip(MEDIUM_NAMES, repl):
        res = _adamw("adamw_" + n, w[n].reshape(-1, LANES), r, mom[n].reshape(-1, LANES), var[n].reshape(-1, LANES))
        delta[n], new_m[n], new_v[n] = (t.reshape(w[n].shape) for t in res)
    sd, sm, sv = _adamw("adamw_small", _pack_small(w), repl[-1], _pack_small(mom), _pack_small(var))
    delta.update(_unpack_small(sd, like))
    new_m.update(_unpack_small(sm, like))
    new_v.update(_unpack_small(sv, like))
    grads = {**grad_big, **grad_small}
    lead = lambda t: t[None]
    return (loss, dx[None], *[lead(grads[n]) for n in WEIGHT_NAMES], *[lead(delta[n]) for n in WEIGHT_NAMES],
            *[lead(new_m[n]) for n in WEIGHT_NAMES], *[lead(new_v[n]) for n in WEIGHT_NAMES])
```

```python
import functools
import math

import jax
import jax.numpy as jnp
from jax import lax
from jax.experimental import pallas as pl
from jax.experimental.pallas import tpu as pltpu

F32 = jnp.float32
BF16 = jnp.bfloat16
MXU_DTYPE = BF16
GRAD_PAYLOAD_DTYPE = BF16
DW_DTYPE = BF16
S5_DTYPE = BF16
HI = lax.Precision.HIGHEST
VMEM_LIMIT_V7X = 56 * 1024 * 1024
LANES = 128
SUBLANES = 8

GRID_W = 64
WIN_H = 8
WIN_W = 16
HEAD_DIM = 64
SSM_GROUP_CH = 16
SSM_STATE = 64
S5_CHUNK = 16
S5_GROUPS_PER_STEP = 8
RMS_EPS = 1e-6
NEG_INF = -1e30
N_CHIPS = 4
MESH = pl.DeviceIdType.MESH

ADAM_LR = 0.001
ADAM_B1 = 0.9
ADAM_B2 = 0.999
ADAM_EPS = 1e-08
ADAM_WD = 0.01
ADAM_STEP = 10

WEIGHT_NAMES = ['g_mix', 'w_in', 'q_gain', 'k_gain', 'rpb', 'ssm_a_re', 'ssm_a_im', 'ssm_b_re', 'ssm_b_im',
                'ssm_c_re', 'ssm_c_im', 'ssm_log_step', 'ssm_d', 'w_glu', 'b_glu', 'g_out_attn', 'g_out_ssm',
                'w_out', 'g_ffn', 'w_ffn_gate', 'w_ffn_up', 'w_ffn_down']
BIG_NAMES = ['w_in', 'w_glu', 'w_out', 'w_ffn_gate', 'w_ffn_up', 'w_ffn_down']
LATE_NAMES = BIG_NAMES[1:]
SMALL_NAMES = [n for n in WEIGHT_NAMES if n not in BIG_NAMES]


def _cparams(sem):
    return pltpu.CompilerParams(dimension_semantics=sem, vmem_limit_bytes=VMEM_LIMIT_V7X)


def _tile(n, want):
    if n <= want:
        return n
    t = (want // LANES) * LANES
    while t >= LANES:
        if n % t == 0:
            return t
        t -= LANES
    return n


def _mm(name, a, b, *, contract, a_mode='2', b_mode='2', o_mode='2', out_dtype=F32, add=None, exact=False,
        tm=1024, tn=1024, tk=2048):
    dn = {'nn': (((1,), (0,)), ((), ())), 'nt': (((1,), (1,)), ((), ())), 'tn': (((0,), (0,)), ((), ()))}[contract]
    ar, ac = a.shape[-2:]
    br, bc = b.shape[-2:]
    m, kdim = (ar, ac) if contract != 'tn' else (ac, ar)
    n = bc if contract != 'nt' else br
    assert kdim == (br if contract != 'nt' else bc), (name, a.shape, b.shape)
    nbatch = 1
    for arr, mode in ((a, a_mode), (b, b_mode)):
        if mode == 'b':
            nbatch = arr.shape[0]
    nstack = 1
    for arr, mode in ((a, a_mode), (b, b_mode)):
        if mode == 'c':
            nstack = arr.shape[0]
    tm, tn, tk = _tile(m, tm), _tile(n, tn), _tile(kdim, tk)
    nkin = kdim // tk
    nk = nstack * nkin
    grid = (nbatch, m // tm, n // tn, nk)

    def spec(mode, block, rc):
        def imap(s, i, j, kk):
            r, c = rc(i, j, kk % nkin)
            if mode == '2':
                return (r, c)
            return (s if mode == 'b' else kk // nkin, r, c)
        return pl.BlockSpec(block if mode == '2' else (None,) + block, imap)

    a_spec = spec(a_mode, (tm, tk) if contract != 'tn' else (tk, tm),
                  (lambda i, j, k: (i, k)) if contract != 'tn' else (lambda i, j, k: (k, i)))
    b_spec = spec(b_mode, (tk, tn) if contract != 'nt' else (tn, tk),
                  (lambda i, j, k: (k, j)) if contract != 'nt' else (lambda i, j, k: (j, k)))
    o_spec = spec(o_mode, (tm, tn), lambda i, j, k: (i, j))
    out_shape = (m, n) if o_mode == '2' else (nbatch, m, n)
    has_add = add is not None

    def product(a_ref, b_ref):
        if exact:
            return lax.dot_general(a_ref[...].astype(F32), b_ref[...].astype(F32), dn, precision=HI,
                                   preferred_element_type=F32)
        return lax.dot_general(a_ref[...].astype(MXU_DTYPE), b_ref[...].astype(MXU_DTYPE), dn,
                               preferred_element_type=F32)

    def body(*refs):
        a_ref, b_ref = refs[:2]
        add_ref = refs[2] if has_add else None
        o_ref = refs[2 + has_add]

        def write(r):
            if has_add:
                r = r + add_ref[...].astype(F32)
            o_ref[...] = r.astype(o_ref.dtype)

        if nk == 1:
            write(product(a_ref, b_ref))
            return
        acc_ref = refs[3 + has_add]
        k = pl.program_id(3)

        @pl.when(k == 0)
        def _():
            acc_ref[...] = jnp.zeros_like(acc_ref)

        acc_ref[...] += product(a_ref, b_ref)

        @pl.when(k == nk - 1)
        def _():
            write(acc_ref[...])

    in_specs = [a_spec, b_spec] + ([o_spec] if has_add else [])
    args = (a, b) + ((add,) if has_add else ())
    return pl.pallas_call(
        body, name=name, grid=grid, in_specs=in_specs, out_specs=o_spec,
        out_shape=jax.ShapeDtypeStruct(out_shape, out_dtype),
        scratch_shapes=[pltpu.VMEM((tm, tn), F32)] if nk > 1 else [],
        compiler_params=_cparams(("parallel", "parallel", "parallel", "arbitrary")),
    )(*args)


def _ew(name, fn, ins, outs, tr=256):
    rows = next(x[1].shape[0] for x in ins if x[0] == 'r')
    tr = min(tr, rows)
    assert rows % tr == 0 and tr % SUBLANES == 0, (name, rows, tr)
    in_specs, args = [], []
    for x in ins:
        if x[0] == 'r' and len(x) == 2:
            in_specs.append(pl.BlockSpec((tr, x[1].shape[1]), lambda i: (i, 0)))
        elif x[0] == 'r':
            in_specs.append(pl.BlockSpec((tr, x[3]), functools.partial(lambda cb, i: (i, cb), x[2])))
        else:
            in_specs.append(pl.BlockSpec(x[1].shape, lambda i: (0, 0)))
        args.append(x[1])
    out_specs, out_shapes = [], []
    for o in outs:
        if o[0] == 'r':
            out_specs.append(pl.BlockSpec((tr, o[1]), lambda i: (i, 0)))
            out_shapes.append(jax.ShapeDtypeStruct((rows, o[1]), o[2]))
        else:
            out_specs.append(pl.BlockSpec((SUBLANES, o[1]), lambda i: (0, 0)))
            out_shapes.append(jax.ShapeDtypeStruct((SUBLANES, o[1]), F32))
    nin = len(ins)
    has_acc = any(o[0] == 'a' for o in outs)

    def body(*refs):
        vals = fn(*[r[...].astype(F32) for r in refs[:nin]])
        if not isinstance(vals, (tuple, list)):
            vals = (vals,)
        i = pl.program_id(0)
        for o, ref, v in zip(outs, refs[nin:], vals):
            if o[0] == 'r':
                ref[...] = v.astype(ref.dtype)
            else:
                part = v.astype(F32).reshape(tr // SUBLANES, SUBLANES, o[1]).sum(axis=0)

                @pl.when(i == 0)
                def _(ref=ref, part=part):
                    ref[...] = part

                @pl.when(i > 0)
                def _(ref=ref, part=part):
                    ref[...] += part

    res = pl.pallas_call(
        body, name=name, grid=(rows // tr,), in_specs=in_specs, out_specs=out_specs, out_shape=out_shapes,
        compiler_params=_cparams(("arbitrary",) if has_acc else ("parallel",)),
    )(*args)
    return res


def _rms(x, g):
    r = lax.rsqrt(jnp.mean(x * x, axis=-1, keepdims=True) + RMS_EPS)
    xr = x * r
    return xr * g, xr


def _rms_bwd(x, g, dy):
    r = lax.rsqrt(jnp.mean(x * x, axis=-1, keepdims=True) + RMS_EPS)
    xr = x * r
    gdy = g * dy
    dx = r * (gdy - xr * jnp.mean(xr * gdy, axis=-1, keepdims=True))
    return dx, dy * xr


def _sigmoid(x):
    return 1.0 / (1.0 + jnp.exp(-x))


_GELU_C = math.sqrt(2.0 / math.pi)


def _gelu(x):
    return 0.5 * x * (1.0 + jnp.tanh(_GELU_C * (x + 0.044715 * x * x * x)))


def _gelu_grad(x):
    t = jnp.tanh(_GELU_C * (x + 0.044715 * x * x * x))
    return 0.5 * (1.0 + t) + 0.5 * x * (1.0 - t * t) * _GELU_C * (1.0 + 3 * 0.044715 * x * x)


ATTN_ROWS_PER_STEP = 8
NT_DIMS = (((1,), (1,)), ((), ()))
NN_DIMS = (((1,), (0,)), ((), ()))
TN_DIMS = (((0,), (0,)), ((), ()))


def _attn_geometry(r, rows):
    row_start = jnp.clip(r - WIN_H // 2, 0, rows - WIN_H)
    key0 = pl.multiple_of(row_start * GRID_W, GRID_W)
    bias0 = pl.multiple_of((row_start - r + (WIN_H - 1)) * GRID_W, GRID_W)
    return key0, bias0


def _window_onehot():
    c = jnp.arange(GRID_W)
    col_start = jnp.clip(c - WIN_W // 2, 0, GRID_W - WIN_W)
    col_in = (c[None, :] >= col_start[:, None]) & (c[None, :] < col_start[:, None] + WIN_W)
    dc = jnp.clip(c[None, :] - c[:, None], -(WIN_W - 1), WIN_W - 1) + (WIN_W - 1)
    onehot = ((dc[:, :, None] == jnp.arange(2 * WIN_W - 1)[None, None, :]) & col_in[:, :, None]).astype(F32)
    return onehot, col_in


def _bias_table(rpb):
    onehot, col_in = _window_onehot()
    nh = rpb.shape[0]
    tab = jnp.einsum('perd,qkd->prkeq', rpb.reshape(nh // 2, 2, 2 * WIN_H - 1, 2 * WIN_W - 1), onehot, precision=HI)
    tab = tab + jnp.where(col_in, 0.0, NEG_INF).T[None, None, :, None, :]
    return tab.reshape(nh // 2, (2 * WIN_H - 1) * GRID_W, 2 * GRID_W)


def _bias_table_grad(dtab):
    onehot, _ = _window_onehot()
    npair = dtab.shape[0]
    d = dtab.reshape(npair, 2 * WIN_H - 1, GRID_W, 2, GRID_W)
    return jnp.einsum('prkeq,qkd->perd', d, onehot, precision=HI).reshape(2 * npair, 2 * WIN_H - 1, 2 * WIN_W - 1)


def _lane_lo(shape):
    return lax.broadcasted_iota(jnp.int32, shape, 1) < HEAD_DIM


def _half_sums(v):
    lo = _lane_lo(v.shape)
    s_lo = jnp.sum(jnp.where(lo, v, 0.0), axis=1, keepdims=True)
    s_hi = jnp.sum(jnp.where(lo, 0.0, v), axis=1, keepdims=True)
    return jnp.where(lo, s_lo, s_hi)


def _rms_pair(x, g):
    r = lax.rsqrt(_half_sums(x * x) * (1.0 / HEAD_DIM) + RMS_EPS)
    return x * r * g


def _rms_pair_bwd(x, g, dy):
    r = lax.rsqrt(_half_sums(x * x) * (1.0 / HEAD_DIM) + RMS_EPS)
    xr = x * r
    gdy = g * dy
    dx = r * (gdy - xr * (_half_sums(xr * gdy) * (1.0 / HEAD_DIM)))
    return dx, dy * xr


def _blockdiag(a):
    a2 = jnp.concatenate([a, a], axis=0)
    row_hi = lax.broadcasted_iota(jnp.int32, a2.shape, 0) >= GRID_W
    lane_hi = lax.broadcasted_iota(jnp.int32, a2.shape, 1) >= HEAD_DIM
    return jnp.where(row_hi == lane_hi, a2, 0.0).astype(MXU_DTYPE)


def _diag_blocks(m):
    return jnp.where(_lane_lo((GRID_W, 2 * HEAD_DIM)), m[:GRID_W], m[GRID_W:])


def _attn_scores(qb, kb, bias):
    st = lax.dot_general(kb, qb, NT_DIMS, preferred_element_type=F32)
    st = st * (1.0 / math.sqrt(HEAD_DIM)) + bias
    mx = jnp.max(st, axis=0, keepdims=True)
    p = jnp.exp(st - mx)
    return p * (1.0 / jnp.sum(p, axis=0, keepdims=True))


def _attn_fwd(z4, qg2, kg2, bias_t, comm=None, comm_arrs=()):
    _, t, aw = z4.shape
    rows = t // GRID_W
    npair = aw // (2 * HEAD_DIM)
    nkeys = WIN_H * GRID_W
    nb = bias_t.shape[1]
    rps = min(ATTN_ROWS_PER_STEP, rows)
    blk = rps * GRID_W
    nsteps = rows // rps
    ncomm = len(comm_arrs)

    def body(*refs):
        q_ref, k_ref, v_ref, qg_ref, kg_ref, b_ref = refs[:6]
        c_ins, o_ref, c_outs = refs[6:6 + ncomm], refs[6 + ncomm], refs[7 + ncomm:7 + 2 * ncomm]
        kn_ref, vb_ref = refs[7 + 2 * ncomm:9 + 2 * ncomm]
        sems = refs[9 + 2 * ncomm:]
        pr, rb = pl.program_id(0), pl.program_id(1)
        if comm is not None:
            @pl.when((pr == 0) & (rb == 0))
            def _():
                comm.start(c_ins, c_outs, *sems)

        @pl.when(rb == 0)
        def _():
            kn_ref[...] = _rms_pair(k_ref[...], kg_ref[...]).astype(MXU_DTYPE)
            vb_ref[...] = v_ref[...].astype(MXU_DTYPE)

        def row(i, carry):
            key0, bias0 = _attn_geometry(rb * rps + i, rows)
            at = pl.ds(pl.multiple_of(i * GRID_W, GRID_W), GRID_W)
            qb = _blockdiag(_rms_pair(q_ref[at, :], qg_ref[...]))
            pt = _attn_scores(qb, kn_ref[pl.ds(key0, nkeys), :], b_ref[pl.ds(bias0, nkeys), :])
            both = lax.dot_general(pt.astype(MXU_DTYPE), vb_ref[pl.ds(key0, nkeys), :], TN_DIMS,
                                   preferred_element_type=F32)
            o_ref[at, :] = _diag_blocks(both)
            return carry

        lax.fori_loop(0, rps, row, 0, unroll=2)
        if comm is not None:
            @pl.when((pr == npair - 1) & (rb == nsteps - 1))
            def _():
                comm.finish(c_ins, c_outs, *sems)

    pair_cols = lambda lead: pl.BlockSpec((None, t, 2 * HEAD_DIM), lambda p, r: (lead, 0, p))
    res = pl.pallas_call(
        body, name="attn_fwd", grid=(npair, nsteps),
        in_specs=[pl.BlockSpec((None, blk, 2 * HEAD_DIM), lambda p, r: (0, r, p)), pair_cols(1), pair_cols(2),
                  pl.BlockSpec((1, 2 * HEAD_DIM), lambda p, r: (0, 0)),
                  pl.BlockSpec((1, 2 * HEAD_DIM), lambda p, r: (0, 0)),
                  pl.BlockSpec((None, nb, 2 * GRID_W), lambda p, r: (p, 0, 0))] + [ANY] * ncomm,
        out_specs=[pl.BlockSpec((blk, 2 * HEAD_DIM), lambda p, r: (r, p))] + [ANY] * ncomm,
        out_shape=[jax.ShapeDtypeStruct((t, aw), F32)] + (comm.out_shape if comm is not None else []),
        scratch_shapes=[pltpu.VMEM((t, 2 * HEAD_DIM), MXU_DTYPE), pltpu.VMEM((t, 2 * HEAD_DIM), MXU_DTYPE)]
        + (comm.scratch if comm is not None else []),
        compiler_params=_cparams(("arbitrary", "arbitrary")),
    )(z4, z4, z4, qg2, kg2, bias_t, *comm_arrs)
    return res[0], res[1:]


def _attn_bwd(z4, qg2, kg2, bias_t, dya, comm=None, comm_arrs=()):
    _, t, aw = z4.shape
    rows = t // GRID_W
    npair = aw // (2 * HEAD_DIM)
    nkeys = WIN_H * GRID_W
    nb = bias_t.shape[1]
    rps = min(ATTN_ROWS_PER_STEP, rows)
    blk = rps * GRID_W
    nsteps = rows // rps
    scale = 1.0 / math.sqrt(HEAD_DIM)
    ncomm = len(comm_arrs)

    def body(*refs):
        q_ref, k_ref, v_ref, qg_ref, kg_ref, b_ref, do_ref = refs[:7]
        c_ins = refs[7:7 + ncomm]
        dz_ref, db_ref, dqg_ref, dkg_ref = refs[7 + ncomm:11 + ncomm]
        c_outs = refs[11 + ncomm:11 + 2 * ncomm]
        kn_ref, vb_ref, dkn_ref, dv_ref = refs[11 + 2 * ncomm:15 + 2 * ncomm]
        sems = refs[15 + 2 * ncomm:]
        pr, rb = pl.program_id(0), pl.program_id(1)
        if comm is not None:
            @pl.when((pr == 0) & (rb == 0))
            def _():
                comm.start(c_ins, c_outs, *sems)

        @pl.when(rb == 0)
        def _():
            kn_ref[...] = _rms_pair(k_ref[...], kg_ref[...]).astype(MXU_DTYPE)
            vb_ref[...] = v_ref[...].astype(MXU_DTYPE)
            dkn_ref[...] = jnp.zeros_like(dkn_ref)
            dv_ref[...] = jnp.zeros_like(dv_ref)
            db_ref[...] = jnp.zeros_like(db_ref)
            dqg_ref[...] = jnp.zeros_like(dqg_ref)

        def row(i, dqg_sum):
            r = rb * rps + i
            key0, bias0 = _attn_geometry(r, rows)
            keys = pl.ds(key0, nkeys)
            at = pl.ds(pl.multiple_of(i * GRID_W, GRID_W), GRID_W)
            q = q_ref[at, :]
            qb = _blockdiag(_rms_pair(q, qg_ref[...]))
            dob = _blockdiag(do_ref[at, :])
            kb = kn_ref[keys, :]
            pt = _attn_scores(qb, kb, b_ref[pl.ds(bias0, nkeys), :])
            dv_ref[keys, :] += lax.dot_general(pt.astype(MXU_DTYPE), dob, NN_DIMS, preferred_element_type=F32)
            dpt = lax.dot_general(vb_ref[keys, :], dob, NT_DIMS, preferred_element_type=F32)
            dst = pt * (dpt - jnp.sum(pt * dpt, axis=0, keepdims=True))
            db_ref[pl.ds(bias0, nkeys), :] += dst
            dsb = dst.astype(MXU_DTYPE)
            dkn_ref[keys, :] += scale * lax.dot_general(dsb, qb, NN_DIMS, preferred_element_type=F32)
            dqn = scale * _diag_blocks(lax.dot_general(dsb, kb, TN_DIMS, preferred_element_type=F32))
            dq, dqg = _rms_pair_bwd(q, qg_ref[...], dqn)
            dz_ref[0, pl.ds(pl.multiple_of(r * GRID_W, GRID_W), GRID_W), :] = dq.astype(dz_ref.dtype)
            return dqg_sum + jnp.sum(dqg, axis=0, keepdims=True)

        dqg_ref[...] += lax.fori_loop(0, rps, row, jnp.zeros((1, 2 * HEAD_DIM), F32), unroll=2)

        @pl.when(rb == nsteps - 1)
        def _():
            dk, dkg = _rms_pair_bwd(k_ref[...], kg_ref[...], dkn_ref[...])
            dz_ref[1] = dk.astype(dz_ref.dtype)
            dz_ref[2] = dv_ref[...].astype(dz_ref.dtype)
            dkg_ref[...] = jnp.sum(dkg, axis=0, keepdims=True)

        if comm is not None:
            @pl.when((pr == npair - 1) & (rb == nsteps - 1))
            def _():
                comm.finish(c_ins, c_outs, *sems)

    pair_cols = lambda lead: pl.BlockSpec((None, t, 2 * HEAD_DIM), lambda p, r: (lead, 0, p))
    pair_vec = pl.BlockSpec((None, 1, 2 * HEAD_DIM), lambda p, r: (p, 0, 0))
    res = pl.pallas_call(
        body, name="attn_bwd", grid=(npair, nsteps),
        in_specs=[pl.BlockSpec((None, blk, 2 * HEAD_DIM), lambda p, r: (0, r, p)), pair_cols(1), pair_cols(2),
                  pl.BlockSpec((1, 2 * HEAD_DIM), lambda p, r: (0, 0)),
                  pl.BlockSpec((1, 2 * HEAD_DIM), lambda p, r: (0, 0)),
                  pl.BlockSpec((None, nb, 2 * GRID_W), lambda p, r: (p, 0, 0)),
                  pl.BlockSpec((blk, 2 * HEAD_DIM), lambda p, r: (r, p))] + [ANY] * ncomm,
        out_specs=[pl.BlockSpec((3, t, 2 * HEAD_DIM), lambda p, r: (0, 0, p)),
                   pl.BlockSpec((None, nb, 2 * GRID_W), lambda p, r: (p, 0, 0)),
                   pair_vec, pair_vec] + [ANY] * ncomm,
        out_shape=[jax.ShapeDtypeStruct((4, t, aw), MXU_DTYPE), jax.ShapeDtypeStruct(bias_t.shape, F32),
                   jax.ShapeDtypeStruct((npair, 1, 2 * HEAD_DIM), F32),
                   jax.ShapeDtypeStruct((npair, 1, 2 * HEAD_DIM), F32)] + (comm.out_shape if comm is not None else []),
        scratch_shapes=[pltpu.VMEM((t, 2 * HEAD_DIM), MXU_DTYPE), pltpu.VMEM((t, 2 * HEAD_DIM), MXU_DTYPE),
                        pltpu.VMEM((t, 2 * HEAD_DIM), F32), pltpu.VMEM((t, 2 * HEAD_DIM), F32)]
        + (comm.scratch if comm is not None else []),
        compiler_params=_cparams(("arbitrary", "arbitrary")),
    )(z4, z4, z4, qg2, kg2, bias_t, dya, *comm_arrs)
    return res[:4], res[4:]


def _s5_mats(a_re, a_im, b_re, b_im, c_re, c_im, log_step, d_skip):
    nd, g, p = a_re.shape
    c = b_re.shape[-1]
    L = S5_CHUNK
    lr = jnp.minimum(a_re, -1e-4).transpose(1, 0, 2)
    li = a_im.transpose(1, 0, 2)
    dt = jnp.exp(log_step).T[..., None]
    n = jnp.arange(L + 1, dtype=F32)[None, :, None, None]
    mag = jnp.exp(n * (lr * dt)[:, None])
    ang = n * (li * dt)[:, None]
    pw_r, pw_i = mag * jnp.cos(ang), mag * jnp.sin(ang)
    den = lr * lr + li * li
    nr, ni = pw_r[:, 1] - 1.0, pw_i[:, 1]
    cr, ci = (nr * lr + ni * li) / den, (ni * lr - nr * li) / den
    bt_r, bt_i = b_re.transpose(1, 3, 0, 2), b_im.transpose(1, 3, 0, 2)
    bb_r = cr[:, None] * bt_r - ci[:, None] * bt_i
    bb_i = cr[:, None] * bt_i + ci[:, None] * bt_r
    ct_r, ct_i = c_re.transpose(1, 2, 0, 3), c_im.transpose(1, 2, 0, 3)

    def cols(x_re, x_im):
        return jnp.concatenate([x_re[..., 0, :], x_re[..., 1, :], x_im[..., 0, :], x_im[..., 1, :]], axis=-1)

    e_r = jnp.stack([pw_r[:, :L, 0][:, ::-1], pw_r[:, :L, 1]], axis=2)
    e_i = jnp.stack([pw_i[:, :L, 0][:, ::-1], pw_i[:, :L, 1]], axis=2)
    ws = (cols(e_r, e_r)[:, :, None] * cols(bb_r, bb_i)[:, None]
          + cols(e_i, e_i)[:, :, None] * cols(-bb_i, bb_r)[:, None]).reshape(g, L * c, 4 * p)
    f_r = jnp.stack([pw_r[:, 1:, 0], pw_r[:, 1:, 1][:, ::-1]], axis=2)
    f_i = jnp.stack([pw_i[:, 1:, 0], pw_i[:, 1:, 1][:, ::-1]], axis=2)
    wot = (cols(f_r, f_i)[:, :, None] * cols(ct_r, -ct_r)[:, None]
           + cols(f_i, f_r)[:, :, None] * cols(-ct_i, -ct_i)[:, None]).reshape(g, L * c, 4 * p)
    qr, qi = pw_r[:, :L, None], pw_i[:, :L, None]
    kp_r, kp_i = qr * bb_r[:, None] - qi * bb_i[:, None], qr * bb_i[:, None] + qi * bb_r[:, None]
    kern = [jnp.einsum('gnip,gop->gino', kp_r[:, :, :, d], ct_r[:, :, d], precision=HI)
            - jnp.einsum('gnip,gop->gino', kp_i[:, :, :, d], ct_i[:, :, d], precision=HI) for d in range(2)]
    skip = d_skip.reshape(g, c, 1, 1) * jnp.eye(c, dtype=F32)[None, :, None, :]
    by_offset = jnp.concatenate([kern[1][:, :, :0:-1], kern[0][:, :, :1] + kern[1][:, :, :1] + skip,
                                 kern[0][:, :, 1:]], axis=2).reshape(g, c, (2 * L - 1) * c)
    mt = jnp.stack([by_offset[:, :, (L - 1 - j) * c:(2 * L - 1 - j) * c] for j in range(L)], axis=1)
    mt = mt.reshape(g, L * c, L * c)
    lr16, li16 = pw_r[:, L], pw_i[:, L]
    fa = jnp.concatenate([lr16[:, 0], lr16[:, 1], lr16[:, 0], lr16[:, 1]], axis=-1)
    fb = jnp.concatenate([-li16[:, 0], -li16[:, 1], li16[:, 0], li16[:, 1]], axis=-1)
    return mt, ws, wot, fa, fb


def _gmm(name, a, b, contract, a_stacked=False, b_stacked=False, o_stacked=False, add=None, out_dtype=F32):
    w = S5_CHUNK * SSM_GROUP_CH
    g = (a.shape[0] if a_stacked else a.shape[1] // w)
    gpb = math.gcd(g, S5_GROUPS_PER_STEP)
    dn = {'nn': (((1,), (0,)), ((), ())), 'nt': (((1,), (1,)), ((), ())), 'tn': (((0,), (0,)), ((), ()))}[contract]

    def spec(arr, stacked):
        if stacked:
            return pl.BlockSpec((gpb,) + arr.shape[1:], lambda i: (i, 0, 0))
        return pl.BlockSpec((arr.shape[0], gpb * w), lambda i: (0, i))

    def take(ref, stacked, e):
        return ref[e] if stacked else ref[:, e * w:(e + 1) * w]

    m = (a.shape[1] if a_stacked else a.shape[0]) if contract != 'tn' else w
    n = w
    if o_stacked:
        o_spec = pl.BlockSpec((gpb, m, n), lambda i: (i, 0, 0))
        o_shape = (g, m, n)
    else:
        o_spec = pl.BlockSpec((m, gpb * n), lambda i: (0, i))
        o_shape = (m, g * n)
    has_add = add is not None

    def body(*refs):
        if has_add:
            a_ref, b_ref, add_ref, o_ref = refs
        else:
            a_ref, b_ref, o_ref = refs
        for e in range(gpb):
            r = lax.dot_general(take(a_ref, a_stacked, e).astype(S5_DTYPE), take(b_ref, b_stacked, e).astype(S5_DTYPE),
                                dn, precision=HI if S5_DTYPE == F32 else None, preferred_element_type=F32)
            if has_add:
                r = r + take(add_ref, o_stacked, e)
            if o_stacked:
                o_ref[e] = r.astype(o_ref.dtype)
            else:
                o_ref[:, e * w:(e + 1) * w] = r.astype(o_ref.dtype)

    in_specs = [spec(a, a_stacked), spec(b, b_stacked)] + ([o_spec] if has_add else [])
    return pl.pallas_call(
        body, name=name, grid=(g // gpb,), in_specs=in_specs, out_specs=o_spec,
        out_shape=jax.ShapeDtypeStruct(o_shape, out_dtype), compiler_params=_cparams(("parallel",)),
    )(*((a, b) + ((add,) if has_add else ())))


def _s5_scan(name, s, fa, fb, rev0, xin=None):
    nk, g, w = s.shape
    hw, qw = w // 2, w // 4
    gb = min(g, 16)
    with_acc = xin is not None

    def body(*refs):
        if with_acc:
            s_ref, a_ref, b_ref, x_ref, o_ref, pa_ref, pb_ref = refs
        else:
            s_ref, a_ref, b_ref, o_ref = refs
        fa_v, fb_v = a_ref[...], b_ref[...]
        dir0 = lax.broadcasted_iota(jnp.int32, (gb, w), 1) % hw < qw
        swap = lambda v: jnp.concatenate([v[:, hw:], v[:, :hw]], axis=1)

        def step(i, carry):
            x, pa, pb = carry
            k0 = (nk - 1 - i) if rev0 else i
            k1 = i if rev0 else (nk - 1 - i)
            for lo in (0, hw):
                o_ref[k0, :, lo:lo + qw] = x[:, lo:lo + qw]
                o_ref[k1, :, lo + qw:lo + hw] = x[:, lo + qw:lo + hw]
            if with_acc:
                xi = jnp.where(dir0, x_ref[k0], x_ref[k1])
                pa = pa + x * xi
                pb = pb + x * swap(xi)
            x = fa_v * x + fb_v * swap(x) + jnp.where(dir0, s_ref[k0], s_ref[k1])
            return x, pa, pb

        z = jnp.zeros((gb, w), F32)
        res = lax.fori_loop(0, nk, step, (z, z, z), unroll=2)
        if with_acc:
            pa_ref[...] = res[1]
            pb_ref[...] = res[2]

    seq = pl.BlockSpec((nk, gb, w), lambda i: (0, i, 0))
    vec = pl.BlockSpec((gb, w), lambda i: (i, 0))
    in_specs = [seq, vec, vec] + ([seq] if with_acc else [])
    out_specs = [seq] + ([vec, vec] if with_acc else [])
    out_shape = [jax.ShapeDtypeStruct((nk, g, w), F32)] + (
        [jax.ShapeDtypeStruct((g, w), F32)] * 2 if with_acc else [])
    return pl.pallas_call(
        body, name=name, grid=(g // gb,), in_specs=in_specs, out_specs=out_specs, out_shape=out_shape,
        compiler_params=_cparams(("parallel",)),
    )(*((s, fa, fb) + ((xin,) if with_acc else ())))


def _regroup(name, x, to_groups):
    if to_groups:
        t, sw = x.shape
    else:
        t, sw = x.shape[0] * S5_CHUNK, x.shape[1] // S5_CHUNK
    nk = t // S5_CHUNK
    wide = LANES * S5_CHUNK

    def place(tok):
        r = lax.broadcasted_iota(jnp.int32, (LANES, wide), 0)
        col = lax.broadcasted_iota(jnp.int32, (LANES, wide), 1)
        want = (r // SSM_GROUP_CH) * (S5_CHUNK * SSM_GROUP_CH) + tok * SSM_GROUP_CH + r % SSM_GROUP_CH
        return (col == want).astype(S5_DTYPE)

    def body(x_ref, o_ref):
        if to_groups:
            acc = jnp.zeros((nk, wide), F32)
            for tok in range(S5_CHUNK):
                rows = x_ref[pl.ds(tok, nk, stride=S5_CHUNK), :].astype(S5_DTYPE)
                acc = acc + lax.dot_general(rows, place(tok), NN_DIMS, preferred_element_type=F32)
            o_ref[...] = acc.astype(o_ref.dtype)
        else:
            xv = x_ref[...].astype(S5_DTYPE)
            for tok in range(S5_CHUNK):
                o_ref[pl.ds(tok, nk, stride=S5_CHUNK), :] = lax.dot_general(
                    xv, place(tok), NT_DIMS, preferred_element_type=F32).astype(o_ref.dtype)

    tokens = pl.BlockSpec((t, LANES), lambda i: (0, i))
    groups = pl.BlockSpec((nk, wide), lambda i: (0, i))
    return pl.pallas_call(
        body, name=name, grid=(sw // LANES,), in_specs=[tokens if to_groups else groups],
        out_specs=groups if to_groups else tokens,
        out_shape=jax.ShapeDtypeStruct((nk, sw * S5_CHUNK), S5_DTYPE) if to_groups else jax.ShapeDtypeStruct((t, sw), F32),
        compiler_params=_cparams(("parallel",)),
    )(x)


def _s5_fwd(u2, mats):
    mt, ws, wot, fa, fb = mats
    nk = u2.shape[0]
    g = mt.shape[0]
    y_intra = _gmm("s5_intra", u2, mt, 'nn', b_stacked=True)
    s = _gmm("s5_chunk_state", u2, ws, 'nn', b_stacked=True)
    (xin,) = _s5_scan("s5_scan", s.reshape(nk, g, -1), fa, fb, False)
    xin = xin.reshape(nk, -1)
    return _gmm("s5_inter", xin, wot, 'nt', b_stacked=True, add=y_intra, out_dtype=S5_DTYPE), xin


def _s5_bwd(u2, xin, mats, dy2):
    mt, ws, wot, fa, fb = mats
    nk = u2.shape[0]
    g = mt.shape[0]
    dxin = _gmm("s5_dxin", dy2, wot, 'nn', b_stacked=True)
    ds, pa, pb = _s5_scan("s5_scan_adj", dxin.reshape(nk, g, -1), fa, -fb, True, xin=xin.reshape(nk, g, -1))
    ds = ds.reshape(nk, -1)
    du_a = _gmm("s5_du_intra", dy2, mt, 'nt', b_stacked=True)
    du2 = _gmm("s5_du_state", ds, ws, 'nt', b_stacked=True, add=du_a, out_dtype=S5_DTYPE)
    dmt = _gmm("s5_dmt", u2, dy2, 'tn', o_stacked=True)
    dws = _gmm("s5_dws", u2, ds, 'tn', o_stacked=True)
    dwot = _gmm("s5_dwot", dy2, xin, 'tn', o_stacked=True)
    return du2, (dmt, dws, dwot, pa, pb)


def _late_weights(gathered):
    w_glu, w_out, w_gate, w_up, w_down = (g4.reshape((N_CHIPS, -1, g4.shape[-1])) for g4 in gathered)
    return w_glu.reshape(-1, w_glu.shape[-1]), w_out.reshape(-1, w_out.shape[-1]), w_gate, w_up, w_down


def _local_step(x, target, w_in4, late, small, reduce_late=None):
    t, d = x.shape
    aw = w_in4.shape[2]
    sw = aw
    nh = aw // HEAD_DIM
    row = lambda v: v.reshape(1, -1)
    g_mix, g_ffn = row(small['g_mix']), row(small['g_ffn'])
    g_oa, g_os, b_glu = row(small['g_out_attn']), row(small['g_out_ssm']), row(small['b_glu'])
    qg2 = jnp.tile(row(small['q_gain']), (1, 2))
    kg2 = jnp.tile(row(small['k_gain']), (1, 2))

    (h,) = _ew("rms_mix", lambda xv, g: _rms(xv, g)[0], [('r', x), ('c', g_mix)], [('r', d, MXU_DTYPE)])
    z4 = _mm("in_proj", h, w_in4, contract='nn', b_mode='b', o_mode='b')
    bias_t = _bias_table(small['rpb'])
    if late[0] == 'halves':
        ya, gathered = _attn_fwd(z4, qg2, kg2, bias_t, comm=_GatherChips(late[1]), comm_arrs=late[1])
        w_glu, w_out, w_gate4, w_up4, w_down4 = _late_weights(gathered)
    else:
        ya, _ = _attn_fwd(z4, qg2, kg2, bias_t)
        w_glu, w_out, w_gate4, w_up4, w_down4 = late[1]
    ffs = w_gate4.shape[2]
    s5_params = tuple(small[n] for n in ('ssm_a_re', 'ssm_a_im', 'ssm_b_re', 'ssm_b_im', 'ssm_c_re', 'ssm_c_im',
                                         'ssm_log_step', 'ssm_d'))
    mats, mats_vjp = jax.vjp(_s5_mats, *s5_params)
    mats = tuple(m.astype(S5_DTYPE) for m in mats[:3]) + mats[3:]
    u2 = _regroup("s5_group_u", z4[3], True)
    ypre2, xin = _s5_fwd(u2, mats)
    ypre = _regroup("s5_ungroup_y", ypre2, False)
    (yb,) = _ew("gelu", _gelu, [('r', ypre)], [('r', sw, MXU_DTYPE)])
    a_glu = _mm("glu_proj", yb, w_glu, contract='nn')

    def mix_out(yav, ypv, av, bg, goa, gos):
        ys = _gelu(ypv) * _sigmoid(av + bg)
        return jnp.concatenate([_rms(yav, goa)[0], _rms(ys, gos)[0]], axis=1)
    (ycat,) = _ew("mix_out", mix_out, [('r', ya), ('r', ypre), ('r', a_glu), ('c', b_glu), ('c', g_oa), ('c', g_os)],
                  [('r', aw + sw, MXU_DTYPE)])
    x1 = _mm("out_proj", ycat, w_out, contract='nn', add=x)
    (h2,) = _ew("rms_ffn", lambda xv, g: _rms(xv, g)[0], [('r', x1), ('c', g_ffn)], [('r', d, MXU_DTYPE)])
    gate4 = _mm("ffn_gate", h2, w_gate4, contract='nn', b_mode='b', o_mode='b', tn=ffs)
    up4 = _mm("ffn_up", h2, w_up4, contract='nn', b_mode='b', o_mode='b', tn=ffs)
    gate_f, up_f = gate4.reshape(4 * t, ffs), up4.reshape(4 * t, ffs)
    (act,) = _ew("swiglu", lambda gv, uv: gv * _sigmoid(gv) * uv, [('r', gate_f), ('r', up_f)],
                 [('r', ffs, MXU_DTYPE)])
    act4 = act.reshape(4, t, ffs)
    x2 = _mm("ffn_down", act4, w_down4, contract='nn', a_mode='c', b_mode='c', add=x1, tk=ffs)

    def loss_fn(xv, tv):
        diff = xv - tv
        return diff * (1.0 / d), diff * (1.0 / d), diff * diff
    dx2, dx2_b, sq = _ew("loss", loss_fn, [('r', x2), ('r', target)], [('r', d, F32), ('r', d, MXU_DTYPE), ('a', d)])

    dact4 = _mm("ffn_down_dx", dx2_b, w_down4, contract='nt', b_mode='b', o_mode='b', tn=ffs)
    d_w_down4 = _mm("ffn_down_dw", act4, dx2_b, contract='tn', a_mode='b', o_mode='b', tm=ffs, out_dtype=DW_DTYPE)

    def swiglu_bwd(dav, gv, uv):
        s = _sigmoid(gv)
        return dav * uv * s * (1.0 + gv * (1.0 - s)), dav * gv * s
    dgate, dup = _ew("swiglu_bwd", swiglu_bwd, [('r', dact4.reshape(4 * t, ffs)), ('r', gate_f), ('r', up_f)],
                     [('r', ffs, MXU_DTYPE), ('r', ffs, MXU_DTYPE)])
    dgate4, dup4 = dgate.reshape(4, t, ffs), dup.reshape(4, t, ffs)
    dh2 = _mm("ffn_gate_dx", dgate4, w_gate4, contract='nt', a_mode='c', b_mode='c', tk=ffs)
    dh2 = _mm("ffn_up_dx", dup4, w_up4, contract='nt', a_mode='c', b_mode='c', add=dh2, tk=ffs)
    d_w_gate4 = _mm("ffn_gate_dw", h2, dgate4, contract='tn', b_mode='b', o_mode='b', tn=ffs, out_dtype=DW_DTYPE)
    d_w_up4 = _mm("ffn_up_dw", h2, dup4, contract='tn', b_mode='b', o_mode='b', tn=ffs, out_dtype=DW_DTYPE)

    def rms_res_bwd(xv, g, dyv, resv):
        dx, dg = _rms_bwd(xv, g, dyv)
        return resv + dx, dg
    dx1, d_g_ffn = _ew("rms_ffn_bwd", rms_res_bwd, [('r', x1), ('c', g_ffn), ('r', dh2), ('r', dx2)],
                       [('r', d, F32), ('a', d)])

    dycat = _mm("out_proj_dx", dx1, w_out, contract='nt')
    d_w_out = _mm("out_proj_dw", ycat, dx1, contract='tn', out_dtype=DW_DTYPE)

    def mix_out_bwd(yav, ypv, av, bg, goa, gos, dca, dcs):
        dya, dgoa = _rms_bwd(yav, goa, dca)
        y = _gelu(ypv)
        s = _sigmoid(av + bg)
        dys, dgos = _rms_bwd(y * s, gos, dcs)
        da = dys * y * s * (1.0 - s)
        return dya, da, dys * s, dgoa, dgos, da
    dya, da, dy_direct, d_g_oa, d_g_os, d_b_glu = _ew(
        "mix_out_bwd", mix_out_bwd,
        [('r', ya), ('r', ypre), ('r', a_glu), ('c', b_glu), ('c', g_oa), ('c', g_os),
         ('r', dycat, 0, aw), ('r', dycat, 1, sw)],
        [('r', aw, F32), ('r', sw, MXU_DTYPE), ('r', sw, F32), ('a', aw), ('a', sw), ('a', sw)])
    dy = _mm("glu_proj_dx", da, w_glu, contract='nt', add=dy_direct)
    d_w_glu = _mm("glu_proj_dw", yb, da, contract='tn', out_dtype=DW_DTYPE)
    (dypre,) = _ew("gelu_bwd", lambda dyv, ypv: dyv * _gelu_grad(ypv), [('r', dy), ('r', ypre)],
                   [('r', sw, F32)])

    du2, dmats = _s5_bwd(u2, xin, mats, _regroup("s5_group_dy", dypre, True))
    d_s5 = mats_vjp(dmats)
    du = _regroup("s5_ungroup_du", du2, False)
    d_late = (d_w_glu, d_w_out, d_w_gate4, d_w_up4, d_w_down4)
    if reduce_late is not None:
        sums = reduce_late(d_late)
        (dz4, dbias_t, dqg, dkg), scattered = _attn_bwd(z4, qg2, kg2, bias_t, dya, comm=_ScatterChips(sums),
                                                       comm_arrs=sums)
        d_late = (sums, list(scattered))
    else:
        (dz4, dbias_t, dqg, dkg), _ = _attn_bwd(z4, qg2, kg2, bias_t, dya)
    d_rpb = _bias_table_grad(dbias_t)
    fold = lambda v: v.reshape(-1, 2, HEAD_DIM).sum(axis=(0, 1))
    dz4 = dz4.at[3].set(du)

    dh = _mm("in_proj_dx", dz4, w_in4, contract='nt', a_mode='c', b_mode='c')
    d_w_in4 = _mm("in_proj_dw", h, dz4, contract='tn', b_mode='b', o_mode='b', out_dtype=DW_DTYPE)
    dx, d_g_mix = _ew("rms_mix_bwd", rms_res_bwd, [('r', x), ('c', g_mix), ('r', dh), ('r', dx1)],
                      [('r', d, F32), ('a', d)])

    colsum = lambda v: v.sum(axis=0)
    d_small = {
        'g_mix': colsum(d_g_mix), 'q_gain': fold(dqg), 'k_gain': fold(dkg), 'rpb': d_rpb,
        'ssm_a_re': d_s5[0], 'ssm_a_im': d_s5[1], 'ssm_b_re': d_s5[2], 'ssm_b_im': d_s5[3],
        'ssm_c_re': d_s5[4], 'ssm_c_im': d_s5[5], 'ssm_log_step': d_s5[6], 'ssm_d': d_s5[7],
        'b_glu': colsum(d_b_glu), 'g_out_attn': colsum(d_g_oa), 'g_out_ssm': colsum(d_g_os), 'g_ffn': colsum(d_g_ffn),
    }
    return jnp.sum(sq), dx, d_late, d_w_in4, d_small


ANY = pl.BlockSpec(memory_space=pl.ANY)


def _place():
    x, y, c = lax.axis_index("x"), lax.axis_index("y"), lax.axis_index("c")
    other_chips = [(1 - x, y), (x, 1 - y), (1 - x, 1 - y)]
    return x, y, c, 2 * x + y, (x, y, 1 - c), other_chips


class _GatherChips:
    KINDS = 7

    def __init__(self, arrs):
        self.n = len(arrs)
        self.out_shape = [jax.ShapeDtypeStruct((N_CHIPS,) + a.shape, a.dtype) for a in arrs]
        self.scratch = [pltpu.SemaphoreType.DMA((self.n, self.KINDS)), pltpu.SemaphoreType.DMA((self.n, self.KINDS))]

    def _copies(self, ins, outs, send_sems, recv_sems):
        x, y, c, me, sibling, chips = _place()

        def remote(a, k, src, dst, to):
            return lambda: pltpu.make_async_remote_copy(src_ref=src, dst_ref=dst, send_sem=send_sems.at[a, k],
                                                        recv_sem=recv_sems.at[a, k], device_id=to, device_id_type=MESH)
        own, out, landed, passed, theirs = [], [], [], [], []
        for a in range(self.n):
            own.append(remote(a, 6, ins[a], outs[a].at[me], sibling))
            for j, (px, py) in enumerate(chips):
                there, here = outs[a].at[2 * px + py, c], outs[a].at[2 * px + py, 1 - c]
                out.append(remote(a, j, ins[a].at[c], outs[a].at[me, c], (px, py, c)))
                landed.append(remote(a, j, there, there, (px, py, c)))
                passed.append(remote(a, 3 + j, there, there, sibling))
                theirs.append(remote(a, 3 + j, here, here, sibling))
        return own, out, landed, passed, theirs

    def start(self, ins, outs, send_sems, recv_sems):
        own, out, _, _, _ = self._copies(ins, outs, send_sems, recv_sems)
        for make in own + out:
            make().start()

    def finish(self, ins, outs, send_sems, recv_sems):
        own, out, landed, passed, theirs = self._copies(ins, outs, send_sems, recv_sems)
        for arrived, onward in zip(landed, passed):
            arrived().wait_recv()
            onward().start()
        for make in theirs + own:
            make().wait_recv()
        for make in own + out + passed:
            make().wait_send()


class _ScatterChips:
    def __init__(self, sums):
        self.n = len(sums)
        self.out_shape = [jax.ShapeDtypeStruct(s.shape, s.dtype) for s in sums]
        self.scratch = [pltpu.SemaphoreType.DMA((self.n, 3)), pltpu.SemaphoreType.DMA((self.n, 3))]

    def _copies(self, ins, outs, send_sems, recv_sems):
        x, y, c, me, sibling, chips = _place()
        out, landed = [], []

        def remote(a, j, src, dst, to):
            return lambda: pltpu.make_async_remote_copy(src_ref=src, dst_ref=dst, send_sem=send_sems.at[a, j],
                                                        recv_sem=recv_sems.at[a, j], device_id=to, device_id_type=MESH)
        for a in range(self.n):
            for j, (px, py) in enumerate(chips):
                slot = outs[a].at[2 * px + py]
                out.append(remote(a, j, ins[a].at[2 * px + py], outs[a].at[me], (px, py, c)))
                landed.append(remote(a, j, slot, slot, (px, py, c)))
        return out, landed

    def start(self, ins, outs, send_sems, recv_sems):
        for make in self._copies(ins, outs, send_sems, recv_sems)[0]:
            make().start()

    def finish(self, ins, outs, send_sems, recv_sems):
        out, landed = self._copies(ins, outs, send_sems, recv_sems)
        for make in landed:
            make().wait_recv()
        for make in out:
            make().wait_send()


def _comm_call(name, comm, arrs):
    n = comm.n

    def body(*refs):
        parts = (refs[:n], refs[n:2 * n]) + tuple(refs[2 * n:])
        comm.start(*parts)
        comm.finish(*parts)

    return pl.pallas_call(body, name=name, in_specs=[ANY] * n, out_specs=[ANY] * n, out_shape=comm.out_shape,
                          scratch_shapes=comm.scratch)(*arrs)


def _gather_chips(name, arrs):
    return _comm_call(name, _GatherChips(arrs), arrs)


def _swap_halves(name, parts):
    n = len(parts)

    def body(*refs):
        ins, outs = refs[:n], refs[n:2 * n]
        send_sems, recv_sems = refs[2 * n:]
        x, y, c, me, sibling, chips = _place()
        cps = []
        for a in range(n):
            cp = pltpu.make_async_remote_copy(src_ref=ins[a].at[:, 1 - c], dst_ref=outs[a], send_sem=send_sems.at[a],
                                              recv_sem=recv_sems.at[a], device_id=sibling, device_id_type=MESH)
            cp.start()
            cps.append(cp)
        for cp in cps:
            cp.wait()

    return pl.pallas_call(
        body, name=name, in_specs=[ANY] * n, out_specs=[ANY] * n,
        out_shape=[jax.ShapeDtypeStruct((N_CHIPS,) + p.shape[2:], p.dtype) for p in parts],
        scratch_shapes=[pltpu.SemaphoreType.DMA((n,)), pltpu.SemaphoreType.DMA((n,))],
    )(*parts)


def _scatter_chips(name, sums):
    return _comm_call(name, _ScatterChips(sums), sums)


def _swap_reduced(name, halves):
    n = len(halves)

    def body(*refs):
        ins, outs = refs[:n], refs[n:2 * n]
        send_sems, recv_sems = refs[2 * n:]
        x, y, c, me, sibling, chips = _place()
        cps = []
        for a in range(n):
            cp = pltpu.make_async_remote_copy(src_ref=ins[a], dst_ref=outs[a], send_sem=send_sems.at[a],
                                              recv_sem=recv_sems.at[a], device_id=sibling, device_id_type=MESH)
            cp.start()
            cps.append(cp)
        for cp in cps:
            cp.wait()

    return pl.pallas_call(
        body, name=name, in_specs=[ANY] * n, out_specs=[ANY] * n,
        out_shape=[jax.ShapeDtypeStruct(h.shape, h.dtype) for h in halves],
        scratch_shapes=[pltpu.SemaphoreType.DMA((n,)), pltpu.SemaphoreType.DMA((n,))],
    )(*halves)


def _row_tile(r, want=256):
    t = (min(r, want) // SUBLANES) * SUBLANES
    while r % t:
        t -= SUBLANES
    return t


def _add_own_half(name, part, got, c, out_dtype):
    _, _, r, cols = part.shape
    tr = _row_tile(r)

    def body(c_ref, p_ref, g_ref, o_ref):
        o_ref[...] = (p_ref[...].astype(F32) + g_ref[...].astype(F32)).astype(o_ref.dtype)

    return pl.pallas_call(
        body, name=name,
        grid_spec=pltpu.PrefetchScalarGridSpec(
            num_scalar_prefetch=1, grid=(N_CHIPS, r // tr),
            in_specs=[pl.BlockSpec((None, None, tr, cols), lambda s, i, c_ref: (s, c_ref[0], i, 0)),
                      pl.BlockSpec((None, tr, cols), lambda s, i, c_ref: (s, i, 0))],
            out_specs=pl.BlockSpec((None, tr, cols), lambda s, i, c_ref: (s, i, 0))),
        out_shape=jax.ShapeDtypeStruct(got.shape, out_dtype),
        compiler_params=_cparams(("parallel", "parallel")),
    )(c.reshape(1).astype(jnp.int32), part, got)


def _sum_chips(name, got, own, me):
    _, r, cols = got.shape
    tr = _row_tile(r)

    def body(me_ref, r0, r1, r2, r3, own_ref, o_ref):
        pick = lambda s, ref: jnp.where(me_ref[0] == s, own_ref[...], ref[...]).astype(F32)
        o_ref[...] = ((pick(0, r0) + pick(1, r1)) + pick(2, r2)) + pick(3, r3)

    def slot(s):
        return pl.BlockSpec((None, tr, cols),
                            lambda i, me_ref: (jnp.where(me_ref[0] == s, (s + 1) % N_CHIPS, s), i, 0))

    return pl.pallas_call(
        body, name=name,
        grid_spec=pltpu.PrefetchScalarGridSpec(
            num_scalar_prefetch=1, grid=(r // tr,),
            in_specs=[slot(s) for s in range(N_CHIPS)]
            + [pl.BlockSpec((None, tr, cols), lambda i, me_ref: (me_ref[0], i, 0))],
            out_specs=pl.BlockSpec((tr, cols), lambda i, me_ref: (i, 0))),
        out_shape=jax.ShapeDtypeStruct((r, cols), F32),
        compiler_params=_cparams(("parallel",)),
    )(me.reshape(1).astype(jnp.int32), got, got, got, got, own)


def _adamw_math(wv, gv, mv, vv):
    mv = ADAM_B1 * mv + (1.0 - ADAM_B1) * gv
    vv = ADAM_B2 * vv + (1.0 - ADAM_B2) * (gv * gv)
    m_hat = mv / (1.0 - ADAM_B1 ** ADAM_STEP)
    v_hat = vv / (1.0 - ADAM_B2 ** ADAM_STEP)
    return -ADAM_LR * (m_hat / (jnp.sqrt(v_hat) + ADAM_EPS) + ADAM_WD * wv), mv, vv


def _adamw(name, w, g, m, v):
    cols = w.shape[1]
    return _ew(name, _adamw_math, [('r', w), ('r', g), ('r', m), ('r', v)], [('r', cols, F32)] * 3,
               tr=_row_tile(w.shape[0], 128))


def _adamw_halves(name, w, mine, theirs, m, v, c):
    r, cols = mine.shape
    tr = _row_tile(r, 128)
    nb = r // tr

    def body(c_ref, w_ref, a_ref, b_ref, m_ref, v_ref, g_out, d_out, m_out, v_out):
        g = jnp.where(pl.program_id(0) == c_ref[0], a_ref[...], b_ref[...])
        g_out[...] = g
        d_out[...], m_out[...], v_out[...] = _adamw_math(w_ref[...], g, m_ref[...], v_ref[...])

    whole = pl.BlockSpec((tr, cols), lambda h, i, c_ref: (h * nb + i, 0))
    half = pl.BlockSpec((tr, cols), lambda h, i, c_ref: (i, 0))
    return pl.pallas_call(
        body, name=name,
        grid_spec=pltpu.PrefetchScalarGridSpec(
            num_scalar_prefetch=1, grid=(2, nb),
            in_specs=[whole, half, half, whole, whole], out_specs=[whole] * 4),
        out_shape=[jax.ShapeDtypeStruct(w.shape, F32)] * 4,
        compiler_params=_cparams(("parallel", "parallel")),
    )(c.reshape(1).astype(jnp.int32), w, mine, theirs, m, v)


SMALL_ROWS_ALIGN = 2 * N_CHIPS * SUBLANES


MEDIUM_NAMES = ['ssm_b_re', 'ssm_b_im', 'ssm_c_re', 'ssm_c_im']
PACKED_NAMES = [n for n in SMALL_NAMES if n not in MEDIUM_NAMES]


def _pack_small(d):
    flat = jnp.concatenate([d[n].reshape(-1).astype(F32) for n in PACKED_NAMES])
    rows = -(-flat.shape[0] // (LANES * SMALL_ROWS_ALIGN)) * SMALL_ROWS_ALIGN
    return jnp.pad(flat, (0, rows * LANES - flat.shape[0])).reshape(rows, LANES)


def _unpack_small(packed, like):
    flat = packed.reshape(-1)
    out, off = {}, 0
    for n in PACKED_NAMES:
        size = like[n].size
        out[n] = flat[off:off + size].reshape(like[n].shape)
        off += size
    return out


def kernel(x, g_mix, w_in, q_gain, k_gain, rpb, ssm_a_re, ssm_a_im, ssm_b_re, ssm_b_im, ssm_c_re, ssm_c_im, ssm_log_step, ssm_d, w_glu, b_glu, g_out_attn, g_out_ssm, w_out, g_ffn, w_ffn_gate, w_ffn_up, w_ffn_down, loss_target, m_g_mix, m_w_in, m_q_gain, m_k_gain, m_rpb, m_ssm_a_re, m_ssm_a_im, m_ssm_b_re, m_ssm_b_im, m_ssm_c_re, m_ssm_c_im, m_ssm_log_step, m_ssm_d, m_w_glu, m_b_glu, m_g_out_attn, m_g_out_ssm, m_w_out, m_g_ffn, m_w_ffn_gate, m_w_ffn_up, m_w_ffn_down, v_g_mix, v_w_in, v_q_gain, v_k_gain, v_rpb, v_ssm_a_re, v_ssm_a_im, v_ssm_b_re, v_ssm_b_im, v_ssm_c_re, v_ssm_c_im, v_ssm_log_step, v_ssm_d, v_w_glu, v_b_glu, v_g_out_attn, v_g_out_ssm, v_w_out, v_g_ffn, v_w_ffn_gate, v_w_ffn_up, v_w_ffn_down):
    given = dict(locals())
    w = {n: given[n][0] for n in WEIGHT_NAMES}
    mom = {n: given["m_" + n][0] for n in WEIGHT_NAMES}
    var = {n: given["v_" + n][0] for n in WEIGHT_NAMES}
    d = x.shape[-1]
    c = lax.axis_index("c")

    halves = {n: w[n].astype(MXU_DTYPE).reshape((2, w[n].shape[0] // 2, w[n].shape[1])) for n in BIG_NAMES}
    (w_in4,) = _gather_chips("gather_w_in", [halves['w_in']])
    w_in4 = w_in4.reshape((N_CHIPS, -1, w_in4.shape[-1]))

    def chip_sums(tag, grads, payload):
        parts = [g.reshape((N_CHIPS, 2, -1, g.shape[-1])) for g in grads]
        got = _swap_halves("reduce_swap_halves_" + tag, parts)
        return [_add_own_half("reduce_add_%s_%d" % (tag, a), p, gt, c, dt)
                for a, (p, gt, dt) in enumerate(zip(parts, got, payload))]

    reduce_late = lambda grads: chip_sums("late", grads, [GRAD_PAYLOAD_DTYPE] * len(grads))
    sq, dx, (sums_late, got_late), d_w_in4, d_small = _local_step(
        x[0], loss_target[0], w_in4, ('halves', [halves[n] for n in LATE_NAMES]), {n: w[n] for n in SMALL_NAMES},
        reduce_late)
    loss = lax.psum(0.5 * sq / d, ("x", "y", "c"))

    nbig = len(BIG_NAMES)
    rest = [d_w_in4] + [d_small[n].reshape(-1, LANES) for n in MEDIUM_NAMES] + [_pack_small(d_small)]
    sums_rest = chip_sums("rest", rest, [GRAD_PAYLOAD_DTYPE] + [F32] * (len(rest) - 1))
    got_rest = list(_scatter_chips("reduce_scatter_chips", sums_rest))
    sums = sums_rest[:1] + sums_late + sums_rest[1:]
    got = got_rest[:1] + got_late + got_rest[1:]
    me = 2 * lax.axis_index("x") + lax.axis_index("y")
    mine = [_sum_chips("reduce_sum_%d" % a, gt, sm_, me) for a, (gt, sm_) in enumerate(zip(got, sums))]
    theirs = _swap_reduced("reduce_swap_reduced", mine)
    in_order = lambda a: jnp.where(c == 0, jnp.stack([mine[a], theirs[a]]), jnp.stack([theirs[a], mine[a]]))
    repl = _gather_chips("gather_small", [in_order(a) for a in range(nbig, len(mine))])
    repl = [r.reshape(-1, LANES) for r in repl]
    like = {n: w[n] for n in SMALL_NAMES}
    grad_small = _unpack_small(repl[-1], like)
    grad_small.update({n: r.reshape(w[n].shape) for n, r in zip(MEDIUM_NAMES, repl)})

    grad_big, delta, new_m, new_v = {}, {}, {}, {}
    for a, n in enumerate(BIG_NAMES):
        grad_big[n], delta[n], new_m[n], new_v[n] = _adamw_halves("adamw_%d" % a, w[n], mine[a], theirs[a],
                                                                  mom[n], var[n], c)
    for n, r in zip(MEDIUM_NAMES, repl):
        res = _adamw("adamw_" + n, w[n].reshape(-1, LANES), r, mom[n].reshape(-1, LANES), var[n].reshape(-1, LANES))
        delta[n], new_m[n], new_v[n] = (t.reshape(w[n].shape) for t in res)
    sd, sm, sv = _adamw("adamw_small", _pack_small(w), repl[-1], _pack_small(mom), _pack_small(var))
    delta.update(_unpack_small(sd, like))
    new_m.update(_unpack_small(sm, like))
    new_v.update(_unpack_small(sv, like))
    grads = {**grad_big, **grad_small}
    lead = lambda t: t[None]
    return (loss, dx[None], *[lead(grads[n]) for n in WEIGHT_NAMES], *[lead(delta[n]) for n in WEIGHT_NAMES],
            *[lead(new_m[n]) for n in WEIGHT_NAMES], *[lead(new_v[n]) for n in WEIGHT_NAMES])
```

```python
import functools
import math

import jax
import jax.numpy as jnp
from jax import lax
from jax.experimental import pallas as pl
from jax.experimental.pallas import tpu as pltpu

F32 = jnp.float32
BF16 = jnp.bfloat16
MXU_DTYPE = BF16
GRAD_PAYLOAD_DTYPE = BF16
DW_DTYPE = BF16
S5_DTYPE = BF16
HI = lax.Precision.HIGHEST
VMEM_LIMIT_V7X = 56 * 1024 * 1024
LANES = 128
SUBLANES = 8

GRID_W = 64
WIN_H = 8
WIN_W = 16
HEAD_DIM = 64
SSM_GROUP_CH = 16
SSM_STATE = 64
S5_CHUNK = 16
S5_GROUPS_PER_STEP = 8
RMS_EPS = 1e-6
NEG_INF = -1e30
N_CHIPS = 4
MESH = pl.DeviceIdType.MESH

ADAM_LR = 0.001
ADAM_B1 = 0.9
ADAM_B2 = 0.999
ADAM_EPS = 1e-08
ADAM_WD = 0.01
ADAM_STEP = 10

WEIGHT_NAMES = ['g_mix', 'w_in', 'q_gain', 'k_gain', 'rpb', 'ssm_a_re', 'ssm_a_im', 'ssm_b_re', 'ssm_b_im',
                'ssm_c_re', 'ssm_c_im', 'ssm_log_step', 'ssm_d', 'w_glu', 'b_glu', 'g_out_attn', 'g_out_ssm',
                'w_out', 'g_ffn', 'w_ffn_gate', 'w_ffn_up', 'w_ffn_down']
BIG_NAMES = ['w_in', 'w_glu', 'w_out', 'w_ffn_gate', 'w_ffn_up', 'w_ffn_down']
LATE_NAMES = BIG_NAMES[1:]
SMALL_NAMES = [n for n in WEIGHT_NAMES if n not in BIG_NAMES]


def _cparams(sem):
    return pltpu.CompilerParams(dimension_semantics=sem, vmem_limit_bytes=VMEM_LIMIT_V7X)


def _tile(n, want):
    if n <= want:
        return n
    t = (want // LANES) * LANES
    while t >= LANES:
        if n % t == 0:
            return t
        t -= LANES
    return n


def _mm(name, a, b, *, contract, a_mode='2', b_mode='2', o_mode='2', out_dtype=F32, add=None, exact=False,
        tm=1024, tn=1024, tk=2048):
    dn = {'nn': (((1,), (0,)), ((), ())), 'nt': (((1,), (1,)), ((), ())), 'tn': (((0,), (0,)), ((), ()))}[contract]
    ar, ac = a.shape[-2:]
    br, bc = b.shape[-2:]
    m, kdim = (ar, ac) if contract != 'tn' else (ac, ar)
    n = bc if contract != 'nt' else br
    assert kdim == (br if contract != 'nt' else bc), (name, a.shape, b.shape)
    nbatch = 1
    for arr, mode in ((a, a_mode), (b, b_mode)):
        if mode == 'b':
            nbatch = arr.shape[0]
    nstack = 1
    for arr, mode in ((a, a_mode), (b, b_mode)):
        if mode == 'c':
            nstack = arr.shape[0]
    tm, tn, tk = _tile(m, tm), _tile(n, tn), _tile(kdim, tk)
    nkin = kdim // tk
    nk = nstack * nkin
    grid = (nbatch, m // tm, n // tn, nk)

    def spec(mode, block, rc):
        def imap(s, i, j, kk):
            r, c = rc(i, j, kk % nkin)
            if mode == '2':
                return (r, c)
            return (s if mode == 'b' else kk // nkin, r, c)
        return pl.BlockSpec(block if mode == '2' else (None,) + block, imap)

    a_spec = spec(a_mode, (tm, tk) if contract != 'tn' else (tk, tm),
                  (lambda i, j, k: (i, k)) if contract != 'tn' else (lambda i, j, k: (k, i)))
    b_spec = spec(b_mode, (tk, tn) if contract != 'nt' else (tn, tk),
                  (lambda i, j, k: (k, j)) if contract != 'nt' else (lambda i, j, k: (j, k)))
    o_spec = spec(o_mode, (tm, tn), lambda i, j, k: (i, j))
    out_shape = (m, n) if o_mode == '2' else (nbatch, m, n)
    has_add = add is not None

    def product(a_ref, b_ref):
        if exact:
            return lax.dot_general(a_ref[...].astype(F32), b_ref[...].astype(F32), dn, precision=HI,
                                   preferred_element_type=F32)
        return lax.dot_general(a_ref[...].astype(MXU_DTYPE), b_ref[...].astype(MXU_DTYPE), dn,
                               preferred_element_type=F32)

    def body(*refs):
        a_ref, b_ref = refs[:2]
        add_ref = refs[2] if has_add else None
        o_ref = refs[2 + has_add]

        def write(r):
            if has_add:
                r = r + add_ref[...].astype(F32)
            o_ref[...] = r.astype(o_ref.dtype)

        if nk == 1:
            write(product(a_ref, b_ref))
            return
        acc_ref = refs[3 + has_add]
        k = pl.program_id(3)

        @pl.when(k == 0)
        def _():
            acc_ref[...] = jnp.zeros_like(acc_ref)

        acc_ref[...] += product(a_ref, b_ref)

        @pl.when(k == nk - 1)
        def _():
            write(acc_ref[...])

    in_specs = [a_spec, b_spec] + ([o_spec] if has_add else [])
    args = (a, b) + ((add,) if has_add else ())
    return pl.pallas_call(
        body, name=name, grid=grid, in_specs=in_specs, out_specs=o_spec,
        out_shape=jax.ShapeDtypeStruct(out_shape, out_dtype),
        scratch_shapes=[pltpu.VMEM((tm, tn), F32)] if nk > 1 else [],
        compiler_params=_cparams(("parallel", "parallel", "parallel", "arbitrary")),
    )(*args)


def _ew(name, fn, ins, outs, tr=256):
    rows = next(x[1].shape[0] for x in ins if x[0] == 'r')
    tr = min(tr, rows)
    assert rows % tr == 0 and tr % SUBLANES == 0, (name, rows, tr)
    in_specs, args = [], []
    for x in ins:
        if x[0] == 'r' and len(x) == 2:
            in_specs.append(pl.BlockSpec((tr, x[1].shape[1]), lambda i: (i, 0)))
        elif x[0] == 'r':
            in_specs.append(pl.BlockSpec((tr, x[3]), functools.partial(lambda cb, i: (i, cb), x[2])))
        else:
            in_specs.append(pl.BlockSpec(x[1].shape, lambda i: (0, 0)))
        args.append(x[1])
    out_specs, out_shapes = [], []
    for o in outs:
        if o[0] == 'r':
            out_specs.append(pl.BlockSpec((tr, o[1]), lambda i: (i, 0)))
            out_shapes.append(jax.ShapeDtypeStruct((rows, o[1]), o[2]))
        else:
            out_specs.append(pl.BlockSpec((SUBLANES, o[1]), lambda i: (0, 0)))
            out_shapes.append(jax.ShapeDtypeStruct((SUBLANES, o[1]), F32))
    nin = len(ins)
    has_acc = any(o[0] == 'a' for o in outs)

    def body(*refs):
        vals = fn(*[r[...].astype(F32) for r in refs[:nin]])
        if not isinstance(vals, (tuple, list)):
            vals = (vals,)
        i = pl.program_id(0)
        for o, ref, v in zip(outs, refs[nin:], vals):
            if o[0] == 'r':
                ref[...] = v.astype(ref.dtype)
            else:
                part = v.astype(F32).reshape(tr // SUBLANES, SUBLANES, o[1]).sum(axis=0)

                @pl.when(i == 0)
                def _(ref=ref, part=part):
                    ref[...] = part

                @pl.when(i > 0)
                def _(ref=ref, part=part):
                    ref[...] += part

    res = pl.pallas_call(
        body, name=name, grid=(rows // tr,), in_specs=in_specs, out_specs=out_specs, out_shape=out_shapes,
        compiler_params=_cparams(("arbitrary",) if has_acc else ("parallel",)),
    )(*args)
    return res


def _rms(x, g):
    r = lax.rsqrt(jnp.mean(x * x, axis=-1, keepdims=True) + RMS_EPS)
    xr = x * r
    return xr * g, xr


def _rms_bwd(x, g, dy):
    r = lax.rsqrt(jnp.mean(x * x, axis=-1, keepdims=True) + RMS_EPS)
    xr = x * r
    gdy = g * dy
    dx = r * (gdy - xr * jnp.mean(xr * gdy, axis=-1, keepdims=True))
    return dx, dy * xr


def _sigmoid(x):
    return 1.0 / (1.0 + jnp.exp(-x))


_GELU_C = math.sqrt(2.0 / math.pi)


def _gelu(x):
    return 0.5 * x * (1.0 + jnp.tanh(_GELU_C * (x + 0.044715 * x * x * x)))


def _gelu_grad(x):
    t = jnp.tanh(_GELU_C * (x + 0.044715 * x * x * x))
    return 0.5 * (1.0 + t) + 0.5 * x * (1.0 - t * t) * _GELU_C * (1.0 + 3 * 0.044715 * x * x)


ATTN_ROWS_PER_STEP = 8
NT_DIMS = (((1,), (1,)), ((), ()))
NN_DIMS = (((1,), (0,)), ((), ()))
TN_DIMS = (((0,), (0,)), ((), ()))


def _attn_geometry(r, rows):
    row_start = jnp.clip(r - WIN_H // 2, 0, rows - WIN_H)
    key0 = pl.multiple_of(row_start * GRID_W, GRID_W)
    bias0 = pl.multiple_of((row_start - r + (WIN_H - 1)) * GRID_W, GRID_W)
    return key0, bias0


def _window_onehot():
    c = jnp.arange(GRID_W)
    col_start = jnp.clip(c - WIN_W // 2, 0, GRID_W - WIN_W)
    col_in = (c[None, :] >= col_start[:, None]) & (c[None, :] < col_start[:, None] + WIN_W)
    dc = jnp.clip(c[None, :] - c[:, None], -(WIN_W - 1), WIN_W - 1) + (WIN_W - 1)
    onehot = ((dc[:, :, None] == jnp.arange(2 * WIN_W - 1)[None, None, :]) & col_in[:, :, None]).astype(F32)
    return onehot, col_in


def _bias_table(rpb):
    onehot, col_in = _window_onehot()
    nh = rpb.shape[0]
    tab = jnp.einsum('perd,qkd->prkeq', rpb.reshape(nh // 2, 2, 2 * WIN_H - 1, 2 * WIN_W - 1), onehot, precision=HI)
    tab = tab + jnp.where(col_in, 0.0, NEG_INF).T[None, None, :, None, :]
    return tab.reshape(nh // 2, (2 * WIN_H - 1) * GRID_W, 2 * GRID_W)


def _bias_table_grad(dtab):
    onehot, _ = _window_onehot()
    npair = dtab.shape[0]
    d = dtab.reshape(npair, 2 * WIN_H - 1, GRID_W, 2, GRID_W)
    return jnp.einsum('prkeq,qkd->perd', d, onehot, precision=HI).reshape(2 * npair, 2 * WIN_H - 1, 2 * WIN_W - 1)


def _lane_lo(shape):
    return lax.broadcasted_iota(jnp.int32, shape, 1) < HEAD_DIM


def _half_sums(v):
    lo = _lane_lo(v.shape)
    s_lo = jnp.sum(jnp.where(lo, v, 0.0), axis=1, keepdims=True)
    s_hi = jnp.sum(jnp.where(lo, 0.0, v), axis=1, keepdims=True)
    return jnp.where(lo, s_lo, s_hi)


def _rms_pair(x, g):
    r = lax.rsqrt(_half_sums(x * x) * (1.0 / HEAD_DIM) + RMS_EPS)
    return x * r * g


def _rms_pair_bwd(x, g, dy):
    r = lax.rsqrt(_half_sums(x * x) * (1.0 / HEAD_DIM) + RMS_EPS)
    xr = x * r
    gdy = g * dy
    dx = r * (gdy - xr * (_half_sums(xr * gdy) * (1.0 / HEAD_DIM)))
    return dx, dy * xr


def _blockdiag(a):
    a2 = jnp.concatenate([a, a], axis=0)
    row_hi = lax.broadcasted_iota(jnp.int32, a2.shape, 0) >= GRID_W
    lane_hi = lax.broadcasted_iota(jnp.int32, a2.shape, 1) >= HEAD_DIM
    return jnp.where(row_hi == lane_hi, a2, 0.0).astype(MXU_DTYPE)


def _diag_blocks(m):
    return jnp.where(_lane_lo((GRID_W, 2 * HEAD_DIM)), m[:GRID_W], m[GRID_W:])


def _attn_scores(qb, kb, bias):
    st = lax.dot_general(kb, qb, NT_DIMS, preferred_element_type=F32)
    st = st * (1.0 / math.sqrt(HEAD_DIM)) + bias
    mx = jnp.max(st, axis=0, keepdims=True)
    p = jnp.exp(st - mx)
    return p * (1.0 / jnp.sum(p, axis=0, keepdims=True))


def _attn_fwd(z4, qg2, kg2, bias_t, comm=None, comm_arrs=()):
    _, t, aw = z4.shape
    rows = t // GRID_W
    npair = aw // (2 * HEAD_DIM)
    nkeys = WIN_H * GRID_W
    nb = bias_t.shape[1]
    rps = min(ATTN_ROWS_PER_STEP, rows)
    blk = rps * GRID_W
    nsteps = rows // rps
    ncomm = len(comm_arrs)

    def body(*refs):
        q_ref, k_ref, v_ref, qg_ref, kg_ref, b_ref = refs[:6]
        c_ins, o_ref, c_outs = refs[6:6 + ncomm], refs[6 + ncomm], refs[7 + ncomm:7 + 2 * ncomm]
        kn_ref, vb_ref = refs[7 + 2 * ncomm:9 + 2 * ncomm]
        sems = refs[9 + 2 * ncomm:]
        pr, rb = pl.program_id(0), pl.program_id(1)
        if comm is not None:
            @pl.when((pr == 0) & (rb == 0))
            def _():
                comm.start(c_ins, c_outs, *sems)

        @pl.when(rb == 0)
        def _():
            kn_ref[...] = _rms_pair(k_ref[...], kg_ref[...]).astype(MXU_DTYPE)
            vb_ref[...] = v_ref[...].astype(MXU_DTYPE)

        def row(i, carry):
            key0, bias0 = _attn_geometry(rb * rps + i, rows)
            at = pl.ds(pl.multiple_of(i * GRID_W, GRID_W), GRID_W)
            qb = _blockdiag(_rms_pair(q_ref[at, :], qg_ref[...]))
            pt = _attn_scores(qb, kn_ref[pl.ds(key0, nkeys), :], b_ref[pl.ds(bias0, nkeys), :])
            both = lax.dot_general(pt.astype(MXU_DTYPE), vb_ref[pl.ds(key0, nkeys), :], TN_DIMS,
                                   preferred_element_type=F32)
            o_ref[at, :] = _diag_blocks(both)
            return carry

        lax.fori_loop(0, rps, row, 0, unroll=2)
        if comm is not None:
            @pl.when((pr == npair - 1) & (rb == nsteps - 1))
            def _():
                comm.finish(c_ins, c_outs, *sems)

    pair_cols = lambda lead: pl.BlockSpec((None, t, 2 * HEAD_DIM), lambda p, r: (lead, 0, p))
    res = pl.pallas_call(
        body, name="attn_fwd", grid=(npair, nsteps),
        in_specs=[pl.BlockSpec((None, blk, 2 * HEAD_DIM), lambda p, r: (0, r, p)), pair_cols(1), pair_cols(2),
                  pl.BlockSpec((1, 2 * HEAD_DIM), lambda p, r: (0, 0)),
                  pl.BlockSpec((1, 2 * HEAD_DIM), lambda p, r: (0, 0)),
                  pl.BlockSpec((None, nb, 2 * GRID_W), lambda p, r: (p, 0, 0))] + [ANY] * ncomm,
        out_specs=[pl.BlockSpec((blk, 2 * HEAD_DIM), lambda p, r: (r, p))] + [ANY] * ncomm,
        out_shape=[jax.ShapeDtypeStruct((t, aw), F32)] + (comm.out_shape if comm is not None else []),
        scratch_shapes=[pltpu.VMEM((t, 2 * HEAD_DIM), MXU_DTYPE), pltpu.VMEM((t, 2 * HEAD_DIM), MXU_DTYPE)]
        + (comm.scratch if comm is not None else []),
        compiler_params=_cparams(("arbitrary", "arbitrary")),
    )(z4, z4, z4, qg2, kg2, bias_t, *comm_arrs)
    return res[0], res[1:]


def _attn_bwd(z4, qg2, kg2, bias_t, dya, comm=None, comm_arrs=()):
    _, t, aw = z4.shape
    rows = t // GRID_W
    npair = aw // (2 * HEAD_DIM)
    nkeys = WIN_H * GRID_W
    nb = bias_t.shape[1]
    rps = min(ATTN_ROWS_PER_STEP, rows)
    blk = rps * GRID_W
    nsteps = rows // rps
    scale = 1.0 / math.sqrt(HEAD_DIM)
    ncomm = len(comm_arrs)

    def body(*refs):
        q_ref, k_ref, v_ref, qg_ref, kg_ref, b_ref, do_ref = refs[:7]
        c_ins = refs[7:7 + ncomm]
        dz_ref, db_ref, dqg_ref, dkg_ref = refs[7 + ncomm:11 + ncomm]
        c_outs = refs[11 + ncomm:11 + 2 * ncomm]
        kn_ref, vb_ref, dkn_ref, dv_ref = refs[11 + 2 * ncomm:15 + 2 * ncomm]
        sems = refs[15 + 2 * ncomm:]
        pr, rb = pl.program_id(0), pl.program_id(1)
        if comm is not None:
            @pl.when((pr == 0) & (rb == 0))
            def _():
                comm.start(c_ins, c_outs, *sems)

        @pl.when(rb == 0)
        def _():
            kn_ref[...] = _rms_pair(k_ref[...], kg_ref[...]).astype(MXU_DTYPE)
            vb_ref[...] = v_ref[...].astype(MXU_DTYPE)
            dkn_ref[...] = jnp.zeros_like(dkn_ref)
            dv_ref[...] = jnp.zeros_like(dv_ref)
            db_ref[...] = jnp.zeros_like(db_ref)
            dqg_ref[...] = jnp.zeros_like(dqg_ref)

        def row(i, dqg_sum):
            r = rb * rps + i
            key0, bias0 = _attn_geometry(r, rows)
            keys = pl.ds(key0, nkeys)
            at = pl.ds(pl.multiple_of(i * GRID_W, GRID_W), GRID_W)
            q = q_ref[at, :]
            qb = _blockdiag(_rms_pair(q, qg_ref[...]))
            dob = _blockdiag(do_ref[at, :])
            kb = kn_ref[keys, :]
            pt = _attn_scores(qb, kb, b_ref[pl.ds(bias0, nkeys), :])
            dv_ref[keys, :] += lax.dot_general(pt.astype(MXU_DTYPE), dob, NN_DIMS, preferred_element_type=F32)
            dpt = lax.dot_general(vb_ref[keys, :], dob, NT_DIMS, preferred_element_type=F32)
            dst = pt * (dpt - jnp.sum(pt * dpt, axis=0, keepdims=True))
            db_ref[pl.ds(bias0, nkeys), :] += dst
            dsb = dst.astype(MXU_DTYPE)
            dkn_ref[keys, :] += scale * lax.dot_general(dsb, qb, NN_DIMS, preferred_element_type=F32)
            dqn = scale * _diag_blocks(lax.dot_general(dsb, kb, TN_DIMS, preferred_element_type=F32))
            dq, dqg = _rms_pair_bwd(q, qg_ref[...], dqn)
            dz_ref[0, pl.ds(pl.multiple_of(r * GRID_W, GRID_W), GRID_W), :] = dq.astype(dz_ref.dtype)
            return dqg_sum + jnp.sum(dqg, axis=0, keepdims=True)

        dqg_ref[...] += lax.fori_loop(0, rps // 2, lambda i, acc: row(2 * i + 1, row(2 * i, acc)),
                                      jnp.zeros((1, 2 * HEAD_DIM), F32))

        @pl.when(rb == nsteps - 1)
        def _():
            dk, dkg = _rms_pair_bwd(k_ref[...], kg_ref[...], dkn_ref[...])
            dz_ref[1] = dk.astype(dz_ref.dtype)
            dz_ref[2] = dv_ref[...].astype(dz_ref.dtype)
            dkg_ref[...] = jnp.sum(dkg, axis=0, keepdims=True)

        if comm is not None:
            @pl.when((pr == npair - 1) & (rb == nsteps - 1))
            def _():
                comm.finish(c_ins, c_outs, *sems)

    pair_cols = lambda lead: pl.BlockSpec((None, t, 2 * HEAD_DIM), lambda p, r: (lead, 0, p))
    pair_vec = pl.BlockSpec((None, 1, 2 * HEAD_DIM), lambda p, r: (p, 0, 0))
    res = pl.pallas_call(
        body, name="attn_bwd", grid=(npair, nsteps),
        in_specs=[pl.BlockSpec((None, blk, 2 * HEAD_DIM), lambda p, r: (0, r, p)), pair_cols(1), pair_cols(2),
                  pl.BlockSpec((1, 2 * HEAD_DIM), lambda p, r: (0, 0)),
                  pl.BlockSpec((1, 2 * HEAD_DIM), lambda p, r: (0, 0)),
                  pl.BlockSpec((None, nb, 2 * GRID_W), lambda p, r: (p, 0, 0)),
                  pl.BlockSpec((blk, 2 * HEAD_DIM), lambda p, r: (r, p))] + [ANY] * ncomm,
        out_specs=[pl.BlockSpec((3, t, 2 * HEAD_DIM), lambda p, r: (0, 0, p)),
                   pl.BlockSpec((None, nb, 2 * GRID_W), lambda p, r: (p, 0, 0)),
                   pair_vec, pair_vec] + [ANY] * ncomm,
        out_shape=[jax.ShapeDtypeStruct((4, t, aw), MXU_DTYPE), jax.ShapeDtypeStruct(bias_t.shape, F32),
                   jax.ShapeDtypeStruct((npair, 1, 2 * HEAD_DIM), F32),
                   jax.ShapeDtypeStruct((npair, 1, 2 * HEAD_DIM), F32)] + (comm.out_shape if comm is not None else []),
        scratch_shapes=[pltpu.VMEM((t, 2 * HEAD_DIM), MXU_DTYPE), pltpu.VMEM((t, 2 * HEAD_DIM), MXU_DTYPE),
                        pltpu.VMEM((t, 2 * HEAD_DIM), F32), pltpu.VMEM((t, 2 * HEAD_DIM), F32)]
        + (comm.scratch if comm is not None else []),
        compiler_params=_cparams(("arbitrary", "arbitrary")),
    )(z4, z4, z4, qg2, kg2, bias_t, dya, *comm_arrs)
    return res[:4], res[4:]


def _s5_mats(a_re, a_im, b_re, b_im, c_re, c_im, log_step, d_skip):
    nd, g, p = a_re.shape
    c = b_re.shape[-1]
    L = S5_CHUNK
    lr = jnp.minimum(a_re, -1e-4).transpose(1, 0, 2)
    li = a_im.transpose(1, 0, 2)
    dt = jnp.exp(log_step).T[..., None]
    n = jnp.arange(L + 1, dtype=F32)[None, :, None, None]
    mag = jnp.exp(n * (lr * dt)[:, None])
    ang = n * (li * dt)[:, None]
    pw_r, pw_i = mag * jnp.cos(ang), mag * jnp.sin(ang)
    den = lr * lr + li * li
    nr, ni = pw_r[:, 1] - 1.0, pw_i[:, 1]
    cr, ci = (nr * lr + ni * li) / den, (ni * lr - nr * li) / den
    bt_r, bt_i = b_re.transpose(1, 3, 0, 2), b_im.transpose(1, 3, 0, 2)
    bb_r = cr[:, None] * bt_r - ci[:, None] * bt_i
    bb_i = cr[:, None] * bt_i + ci[:, None] * bt_r
    ct_r, ct_i = c_re.transpose(1, 2, 0, 3), c_im.transpose(1, 2, 0, 3)

    def cols(x_re, x_im):
        return jnp.concatenate([x_re[..., 0, :], x_re[..., 1, :], x_im[..., 0, :], x_im[..., 1, :]], axis=-1)

    e_r = jnp.stack([pw_r[:, :L, 0][:, ::-1], pw_r[:, :L, 1]], axis=2)
    e_i = jnp.stack([pw_i[:, :L, 0][:, ::-1], pw_i[:, :L, 1]], axis=2)
    ws = (cols(e_r, e_r)[:, :, None] * cols(bb_r, bb_i)[:, None]
          + cols(e_i, e_i)[:, :, None] * cols(-bb_i, bb_r)[:, None]).reshape(g, L * c, 4 * p)
    f_r = jnp.stack([pw_r[:, 1:, 0], pw_r[:, 1:, 1][:, ::-1]], axis=2)
    f_i = jnp.stack([pw_i[:, 1:, 0], pw_i[:, 1:, 1][:, ::-1]], axis=2)
    wot = (cols(f_r, f_i)[:, :, None] * cols(ct_r, -ct_r)[:, None]
           + cols(f_i, f_r)[:, :, None] * cols(-ct_i, -ct_i)[:, None]).reshape(g, L * c, 4 * p)
    qr, qi = pw_r[:, :L, None], pw_i[:, :L, None]
    kp_r, kp_i = qr * bb_r[:, None] - qi * bb_i[:, None], qr * bb_i[:, None] + qi * bb_r[:, None]
    kern = [jnp.einsum('gnip,gop->gino', kp_r[:, :, :, d], ct_r[:, :, d], precision=HI)
            - jnp.einsum('gnip,gop->gino', kp_i[:, :, :, d], ct_i[:, :, d], precision=HI) for d in range(2)]
    skip = d_skip.reshape(g, c, 1, 1) * jnp.eye(c, dtype=F32)[None, :, None, :]
    by_offset = jnp.concatenate([kern[1][:, :, :0:-1], kern[0][:, :, :1] + kern[1][:, :, :1] + skip,
                                 kern[0][:, :, 1:]], axis=2).reshape(g, c, (2 * L - 1) * c)
    mt = jnp.stack([by_offset[:, :, (L - 1 - j) * c:(2 * L - 1 - j) * c] for j in range(L)], axis=1)
    mt = mt.reshape(g, L * c, L * c)
    lr16, li16 = pw_r[:, L], pw_i[:, L]
    fa = jnp.concatenate([lr16[:, 0], lr16[:, 1], lr16[:, 0], lr16[:, 1]], axis=-1)
    fb = jnp.concatenate([-li16[:, 0], -li16[:, 1], li16[:, 0], li16[:, 1]], axis=-1)
    return mt, ws, wot, fa, fb


def _gmm(name, a, b, contract, a_stacked=False, b_stacked=False, o_stacked=False, add=None, out_dtype=F32):
    w = S5_CHUNK * SSM_GROUP_CH
    g = (a.shape[0] if a_stacked else a.shape[1] // w)
    gpb = math.gcd(g, S5_GROUPS_PER_STEP)
    dn = {'nn': (((1,), (0,)), ((), ())), 'nt': (((1,), (1,)), ((), ())), 'tn': (((0,), (0,)), ((), ()))}[contract]

    def spec(arr, stacked):
        if stacked:
            return pl.BlockSpec((gpb,) + arr.shape[1:], lambda i: (i, 0, 0))
        return pl.BlockSpec((arr.shape[0], gpb * w), lambda i: (0, i))

    def take(ref, stacked, e):
        return ref[e] if stacked else ref[:, e * w:(e + 1) * w]

    m = (a.shape[1] if a_stacked else a.shape[0]) if contract != 'tn' else w
    n = w
    if o_stacked:
        o_spec = pl.BlockSpec((gpb, m, n), lambda i: (i, 0, 0))
        o_shape = (g, m, n)
    else:
        o_spec = pl.BlockSpec((m, gpb * n), lambda i: (0, i))
        o_shape = (m, g * n)
    has_add = add is not None

    def body(*refs):
        if has_add:
            a_ref, b_ref, add_ref, o_ref = refs
        else:
            a_ref, b_ref, o_ref = refs
        for e in range(gpb):
            r = lax.dot_general(take(a_ref, a_stacked, e).astype(S5_DTYPE), take(b_ref, b_stacked, e).astype(S5_DTYPE),
                                dn, precision=HI if S5_DTYPE == F32 else None, preferred_element_type=F32)
            if has_add:
                r = r + take(add_ref, o_stacked, e)
            if o_stacked:
                o_ref[e] = r.astype(o_ref.dtype)
            else:
                o_ref[:, e * w:(e + 1) * w] = r.astype(o_ref.dtype)

    in_specs = [spec(a, a_stacked), spec(b, b_stacked)] + ([o_spec] if has_add else [])
    return pl.pallas_call(
        body, name=name, grid=(g // gpb,), in_specs=in_specs, out_specs=o_spec,
        out_shape=jax.ShapeDtypeStruct(o_shape, out_dtype), compiler_params=_cparams(("parallel",)),
    )(*((a, b) + ((add,) if has_add else ())))


def _s5_scan(name, s, fa, fb, rev0, xin=None):
    nk, g, w = s.shape
    hw, qw = w // 2, w // 4
    gb = min(g, 16)
    with_acc = xin is not None

    def body(*refs):
        if with_acc:
            s_ref, a_ref, b_ref, x_ref, o_ref, pa_ref, pb_ref = refs
        else:
            s_ref, a_ref, b_ref, o_ref = refs
        fa_v, fb_v = a_ref[...], b_ref[...]
        dir0 = lax.broadcasted_iota(jnp.int32, (gb, w), 1) % hw < qw
        swap = lambda v: jnp.concatenate([v[:, hw:], v[:, :hw]], axis=1)

        def step(i, carry):
            x, pa, pb = carry
            k0 = (nk - 1 - i) if rev0 else i
            k1 = i if rev0 else (nk - 1 - i)
            for lo in (0, hw):
                o_ref[k0, :, lo:lo + qw] = x[:, lo:lo + qw]
                o_ref[k1, :, lo + qw:lo + hw] = x[:, lo + qw:lo + hw]
            if with_acc:
                xi = jnp.where(dir0, x_ref[k0], x_ref[k1])
                pa = pa + x * xi
                pb = pb + x * swap(xi)
            x = fa_v * x + fb_v * swap(x) + jnp.where(dir0, s_ref[k0], s_ref[k1])
            return x, pa, pb

        z = jnp.zeros((gb, w), F32)
        res = lax.fori_loop(0, nk, step, (z, z, z), unroll=2)
        if with_acc:
            pa_ref[...] = res[1]
            pb_ref[...] = res[2]

    seq = pl.BlockSpec((nk, gb, w), lambda i: (0, i, 0))
    vec = pl.BlockSpec((gb, w), lambda i: (i, 0))
    in_specs = [seq, vec, vec] + ([seq] if with_acc else [])
    out_specs = [seq] + ([vec, vec] if with_acc else [])
    out_shape = [jax.ShapeDtypeStruct((nk, g, w), F32)] + (
        [jax.ShapeDtypeStruct((g, w), F32)] * 2 if with_acc else [])
    return pl.pallas_call(
        body, name=name, grid=(g // gb,), in_specs=in_specs, out_specs=out_specs, out_shape=out_shape,
        compiler_params=_cparams(("parallel",)),
    )(*((s, fa, fb) + ((xin,) if with_acc else ())))


def _regroup(name, x, to_groups):
    if to_groups:
        t, sw = x.shape
    else:
        t, sw = x.shape[0] * S5_CHUNK, x.shape[1] // S5_CHUNK
    nk = t // S5_CHUNK
    wide = LANES * S5_CHUNK

    def place(tok):
        r = lax.broadcasted_iota(jnp.int32, (LANES, wide), 0)
        col = lax.broadcasted_iota(jnp.int32, (LANES, wide), 1)
        want = (r // SSM_GROUP_CH) * (S5_CHUNK * SSM_GROUP_CH) + tok * SSM_GROUP_CH + r % SSM_GROUP_CH
        return (col == want).astype(S5_DTYPE)

    def body(x_ref, o_ref):
        if to_groups:
            acc = jnp.zeros((nk, wide), F32)
            for tok in range(S5_CHUNK):
                rows = x_ref[pl.ds(tok, nk, stride=S5_CHUNK), :].astype(S5_DTYPE)
                acc = acc + lax.dot_general(rows, place(tok), NN_DIMS, preferred_element_type=F32)
            o_ref[...] = acc.astype(o_ref.dtype)
        else:
            xv = x_ref[...].astype(S5_DTYPE)
            for tok in range(S5_CHUNK):
                o_ref[pl.ds(tok, nk, stride=S5_CHUNK), :] = lax.dot_general(
                    xv, place(tok), NT_DIMS, preferred_element_type=F32).astype(o_ref.dtype)

    tokens = pl.BlockSpec((t, LANES), lambda i: (0, i))
    groups = pl.BlockSpec((nk, wide), lambda i: (0, i))
    return pl.pallas_call(
        body, name=name, grid=(sw // LANES,), in_specs=[tokens if to_groups else groups],
        out_specs=groups if to_groups else tokens,
        out_shape=jax.ShapeDtypeStruct((nk, sw * S5_CHUNK), S5_DTYPE) if to_groups else jax.ShapeDtypeStruct((t, sw), F32),
        compiler_params=_cparams(("parallel",)),
    )(x)


def _s5_fwd(u2, mats):
    mt, ws, wot, fa, fb = mats
    nk = u2.shape[0]
    g = mt.shape[0]
    y_intra = _gmm("s5_intra", u2, mt, 'nn', b_stacked=True)
    s = _gmm("s5_chunk_state", u2, ws, 'nn', b_stacked=True)
    (xin,) = _s5_scan("s5_scan", s.reshape(nk, g, -1), fa, fb, False)
    xin = xin.reshape(nk, -1)
    return _gmm("s5_inter", xin, wot, 'nt', b_stacked=True, add=y_intra, out_dtype=S5_DTYPE), xin


def _s5_bwd(u2, xin, mats, dy2):
    mt, ws, wot, fa, fb = mats
    nk = u2.shape[0]
    g = mt.shape[0]
    dxin = _gmm("s5_dxin", dy2, wot, 'nn', b_stacked=True)
    ds, pa, pb = _s5_scan("s5_scan_adj", dxin.reshape(nk, g, -1), fa, -fb, True, xin=xin.reshape(nk, g, -1))
    ds = ds.reshape(nk, -1)
    du_a = _gmm("s5_du_intra", dy2, mt, 'nt', b_stacked=True)
    du2 = _gmm("s5_du_state", ds, ws, 'nt', b_stacked=True, add=du_a, out_dtype=S5_DTYPE)
    dmt = _gmm("s5_dmt", u2, dy2, 'tn', o_stacked=True)
    dws = _gmm("s5_dws", u2, ds, 'tn', o_stacked=True)
    dwot = _gmm("s5_dwot", dy2, xin, 'tn', o_stacked=True)
    return du2, (dmt, dws, dwot, pa, pb)


def _late_weights(gathered):
    w_glu, w_out, w_gate, w_up, w_down = (g4.reshape((N_CHIPS, -1, g4.shape[-1])) for g4 in gathered)
    return w_glu.reshape(-1, w_glu.shape[-1]), w_out.reshape(-1, w_out.shape[-1]), w_gate, w_up, w_down


def _local_step(x, target, w_in4, late, small, reduce_late=None):
    t, d = x.shape
    aw = w_in4.shape[2]
    sw = aw
    nh = aw // HEAD_DIM
    row = lambda v: v.reshape(1, -1)
    g_mix, g_ffn = row(small['g_mix']), row(small['g_ffn'])
    g_oa, g_os, b_glu = row(small['g_out_attn']), row(small['g_out_ssm']), row(small['b_glu'])
    qg2 = jnp.tile(row(small['q_gain']), (1, 2))
    kg2 = jnp.tile(row(small['k_gain']), (1, 2))

    (h,) = _ew("rms_mix", lambda xv, g: _rms(xv, g)[0], [('r', x), ('c', g_mix)], [('r', d, MXU_DTYPE)])
    z4 = _mm("in_proj", h, w_in4, contract='nn', b_mode='b', o_mode='b')
    bias_t = _bias_table(small['rpb'])
    if late[0] == 'halves':
        ya, gathered = _attn_fwd(z4, qg2, kg2, bias_t, comm=_GatherChips(late[1]), comm_arrs=late[1])
        w_glu, w_out, w_gate4, w_up4, w_down4 = _late_weights(gathered)
    else:
        ya, _ = _attn_fwd(z4, qg2, kg2, bias_t)
        w_glu, w_out, w_gate4, w_up4, w_down4 = late[1]
    ffs = w_gate4.shape[2]
    s5_params = tuple(small[n] for n in ('ssm_a_re', 'ssm_a_im', 'ssm_b_re', 'ssm_b_im', 'ssm_c_re', 'ssm_c_im',
                                         'ssm_log_step', 'ssm_d'))
    mats, mats_vjp = jax.vjp(_s5_mats, *s5_params)
    mats = tuple(m.astype(S5_DTYPE) for m in mats[:3]) + mats[3:]
    u2 = _regroup("s5_group_u", z4[3], True)
    ypre2, xin = _s5_fwd(u2, mats)
    ypre = _regroup("s5_ungroup_y", ypre2, False)
    (yb,) = _ew("gelu", _gelu, [('r', ypre)], [('r', sw, MXU_DTYPE)])
    a_glu = _mm("glu_proj", yb, w_glu, contract='nn')

    def mix_out(yav, ypv, av, bg, goa, gos):
        ys = _gelu(ypv) * _sigmoid(av + bg)
        return jnp.concatenate([_rms(yav, goa)[0], _rms(ys, gos)[0]], axis=1)
    (ycat,) = _ew("mix_out", mix_out, [('r', ya), ('r', ypre), ('r', a_glu), ('c', b_glu), ('c', g_oa), ('c', g_os)],
                  [('r', aw + sw, MXU_DTYPE)])
    x1 = _mm("out_proj", ycat, w_out, contract='nn', add=x)
    (h2,) = _ew("rms_ffn", lambda xv, g: _rms(xv, g)[0], [('r', x1), ('c', g_ffn)], [('r', d, MXU_DTYPE)])
    gate4 = _mm("ffn_gate", h2, w_gate4, contract='nn', b_mode='b', o_mode='b', tn=ffs, out_dtype=MXU_DTYPE)
    up4 = _mm("ffn_up", h2, w_up4, contract='nn', b_mode='b', o_mode='b', tn=ffs, out_dtype=MXU_DTYPE)
    gate_f, up_f = gate4.reshape(4 * t, ffs), up4.reshape(4 * t, ffs)
    (act,) = _ew("swiglu", lambda gv, uv: gv * _sigmoid(gv) * uv, [('r', gate_f), ('r', up_f)],
                 [('r', ffs, MXU_DTYPE)])
    act4 = act.reshape(4, t, ffs)
    x2 = _mm("ffn_down", act4, w_down4, contract='nn', a_mode='c', b_mode='c', add=x1, tk=ffs)

    def loss_fn(xv, tv):
        diff = xv - tv
        return diff * (1.0 / d), diff * (1.0 / d), diff * diff
    dx2, dx2_b, sq = _ew("loss", loss_fn, [('r', x2), ('r', target)], [('r', d, F32), ('r', d, MXU_DTYPE), ('a', d)])

    dact4 = _mm("ffn_down_dx", dx2_b, w_down4, contract='nt', b_mode='b', o_mode='b', tn=ffs, out_dtype=MXU_DTYPE)
    d_w_down4 = _mm("ffn_down_dw", act4, dx2_b, contract='tn', a_mode='b', o_mode='b', tm=ffs, out_dtype=DW_DTYPE)

    def swiglu_bwd(dav, gv, uv):
        s = _sigmoid(gv)
        return dav * uv * s * (1.0 + gv * (1.0 - s)), dav * gv * s
    dgate, dup = _ew("swiglu_bwd", swiglu_bwd, [('r', dact4.reshape(4 * t, ffs)), ('r', gate_f), ('r', up_f)],
                     [('r', ffs, MXU_DTYPE), ('r', ffs, MXU_DTYPE)])
    dgate4, dup4 = dgate.reshape(4, t, ffs), dup.reshape(4, t, ffs)
    dh2 = _mm("ffn_gate_dx", dgate4, w_gate4, contract='nt', a_mode='c', b_mode='c', tk=ffs)
    dh2 = _mm("ffn_up_dx", dup4, w_up4, contract='nt', a_mode='c', b_mode='c', add=dh2, tk=ffs)
    d_w_gate4 = _mm("ffn_gate_dw", h2, dgate4, contract='tn', b_mode='b', o_mode='b', tn=ffs, out_dtype=DW_DTYPE)
    d_w_up4 = _mm("ffn_up_dw", h2, dup4, contract='tn', b_mode='b', o_mode='b', tn=ffs, out_dtype=DW_DTYPE)

    def rms_res_bwd(xv, g, dyv, resv):
        dx, dg = _rms_bwd(xv, g, dyv)
        return resv + dx, dg
    dx1, d_g_ffn = _ew("rms_ffn_bwd", rms_res_bwd, [('r', x1), ('c', g_ffn), ('r', dh2), ('r', dx2)],
                       [('r', d, F32), ('a', d)])

    dycat = _mm("out_proj_dx", dx1, w_out, contract='nt', out_dtype=MXU_DTYPE)
    d_w_out = _mm("out_proj_dw", ycat, dx1, contract='tn', out_dtype=DW_DTYPE)

    def mix_out_bwd(yav, ypv, av, bg, goa, gos, dca, dcs):
        dya, dgoa = _rms_bwd(yav, goa, dca)
        y = _gelu(ypv)
        s = _sigmoid(av + bg)
        dys, dgos = _rms_bwd(y * s, gos, dcs)
        da = dys * y * s * (1.0 - s)
        return dya, da, dys * s, dgoa, dgos, da
    dya, da, dy_direct, d_g_oa, d_g_os, d_b_glu = _ew(
        "mix_out_bwd", mix_out_bwd,
        [('r', ya), ('r', ypre), ('r', a_glu), ('c', b_glu), ('c', g_oa), ('c', g_os),
         ('r', dycat, 0, aw), ('r', dycat, 1, sw)],
        [('r', aw, F32), ('r', sw, MXU_DTYPE), ('r', sw, F32), ('a', aw), ('a', sw), ('a', sw)])
    dy = _mm("glu_proj_dx", da, w_glu, contract='nt', add=dy_direct)
    d_w_glu = _mm("glu_proj_dw", yb, da, contract='tn', out_dtype=DW_DTYPE)
    (dypre,) = _ew("gelu_bwd", lambda dyv, ypv: dyv * _gelu_grad(ypv), [('r', dy), ('r', ypre)],
                   [('r', sw, F32)])

    du2, dmats = _s5_bwd(u2, xin, mats, _regroup("s5_group_dy", dypre, True))
    d_s5 = mats_vjp(dmats)
    du = _regroup("s5_ungroup_du", du2, False)
    d_late = (d_w_glu, d_w_out, d_w_gate4, d_w_up4, d_w_down4)
    if reduce_late is not None:
        sums = reduce_late(d_late)
        (dz4, dbias_t, dqg, dkg), scattered = _attn_bwd(z4, qg2, kg2, bias_t, dya, comm=_ScatterChips(sums),
                                                       comm_arrs=sums)
        d_late = (sums, list(scattered))
    else:
        (dz4, dbias_t, dqg, dkg), _ = _attn_bwd(z4, qg2, kg2, bias_t, dya)
    d_rpb = _bias_table_grad(dbias_t)
    fold = lambda v: v.reshape(-1, 2, HEAD_DIM).sum(axis=(0, 1))
    dz4 = dz4.at[3].set(du)

    dh = _mm("in_proj_dx", dz4, w_in4, contract='nt', a_mode='c', b_mode='c')
    d_w_in4 = _mm("in_proj_dw", h, dz4, contract='tn', b_mode='b', o_mode='b', out_dtype=DW_DTYPE)
    dx, d_g_mix = _ew("rms_mix_bwd", rms_res_bwd, [('r', x), ('c', g_mix), ('r', dh), ('r', dx1)],
                      [('r', d, F32), ('a', d)])

    colsum = lambda v: v.sum(axis=0)
    d_small = {
        'g_mix': colsum(d_g_mix), 'q_gain': fold(dqg), 'k_gain': fold(dkg), 'rpb': d_rpb,
        'ssm_a_re': d_s5[0], 'ssm_a_im': d_s5[1], 'ssm_b_re': d_s5[2], 'ssm_b_im': d_s5[3],
        'ssm_c_re': d_s5[4], 'ssm_c_im': d_s5[5], 'ssm_log_step': d_s5[6], 'ssm_d': d_s5[7],
        'b_glu': colsum(d_b_glu), 'g_out_attn': colsum(d_g_oa), 'g_out_ssm': colsum(d_g_os), 'g_ffn': colsum(d_g_ffn),
    }
    return jnp.sum(sq), dx, d_late, d_w_in4, d_small


ANY = pl.BlockSpec(memory_space=pl.ANY)


def _place():
    x, y, c = lax.axis_index("x"), lax.axis_index("y"), lax.axis_index("c")
    other_chips = [(1 - x, y), (x, 1 - y), (1 - x, 1 - y)]
    return x, y, c, 2 * x + y, (x, y, 1 - c), other_chips


class _GatherChips:
    KINDS = 7

    def __init__(self, arrs):
        self.n = len(arrs)
        self.out_shape = [jax.ShapeDtypeStruct((N_CHIPS,) + a.shape, a.dtype) for a in arrs]
        self.scratch = [pltpu.SemaphoreType.DMA((self.n, self.KINDS)), pltpu.SemaphoreType.DMA((self.n, self.KINDS))]

    def _copies(self, ins, outs, send_sems, recv_sems):
        x, y, c, me, sibling, chips = _place()

        def remote(a, k, src, dst, to):
            return lambda: pltpu.make_async_remote_copy(src_ref=src, dst_ref=dst, send_sem=send_sems.at[a, k],
                                                        recv_sem=recv_sems.at[a, k], device_id=to, device_id_type=MESH)
        own, out, landed, passed, theirs = [], [], [], [], []
        for a in range(self.n):
            own.append(remote(a, 6, ins[a], outs[a].at[me], sibling))
            for j, (px, py) in enumerate(chips):
                there, here = outs[a].at[2 * px + py, c], outs[a].at[2 * px + py, 1 - c]
                out.append(remote(a, j, ins[a].at[c], outs[a].at[me, c], (px, py, c)))
                landed.append(remote(a, j, there, there, (px, py, c)))
                passed.append(remote(a, 3 + j, there, there, sibling))
                theirs.append(remote(a, 3 + j, here, here, sibling))
        return own, out, landed, passed, theirs

    def start(self, ins, outs, send_sems, recv_sems):
        own, out, _, _, _ = self._copies(ins, outs, send_sems, recv_sems)
        for make in own + out:
            make().start()

    def finish(self, ins, outs, send_sems, recv_sems):
        own, out, landed, passed, theirs = self._copies(ins, outs, send_sems, recv_sems)
        for arrived, onward in zip(landed, passed):
            arrived().wait_recv()
            onward().start()
        for make in theirs + own:
            make().wait_recv()
        for make in own + out + passed:
            make().wait_send()


class _ScatterChips:
    def __init__(self, sums):
        self.n = len(sums)
        self.out_shape = [jax.ShapeDtypeStruct(s.shape, s.dtype) for s in sums]
        self.scratch = [pltpu.SemaphoreType.DMA((self.n, 3)), pltpu.SemaphoreType.DMA((self.n, 3))]

    def _copies(self, ins, outs, send_sems, recv_sems):
        x, y, c, me, sibling, chips = _place()
        out, landed = [], []

        def remote(a, j, src, dst, to):
            return lambda: pltpu.make_async_remote_copy(src_ref=src, dst_ref=dst, send_sem=send_sems.at[a, j],
                                                        recv_sem=recv_sems.at[a, j], device_id=to, device_id_type=MESH)
        for a in range(self.n):
            for j, (px, py) in enumerate(chips):
                slot = outs[a].at[2 * px + py]
                out.append(remote(a, j, ins[a].at[2 * px + py], outs[a].at[me], (px, py, c)))
                landed.append(remote(a, j, slot, slot, (px, py, c)))
        return out, landed

    def start(self, ins, outs, send_sems, recv_sems):
        for make in self._copies(ins, outs, send_sems, recv_sems)[0]:
            make().start()

    def finish(self, ins, outs, send_sems, recv_sems):
        out, landed = self._copies(ins, outs, send_sems, recv_sems)
        for make in landed:
            make().wait_recv()
        for make in out:
            make().wait_send()


def _comm_call(name, comm, arrs):
    n = comm.n

    def body(*refs):
        parts = (refs[:n], refs[n:2 * n]) + tuple(refs[2 * n:])
        comm.start(*parts)
        comm.finish(*parts)

    return pl.pallas_call(body, name=name, in_specs=[ANY] * n, out_specs=[ANY] * n, out_shape=comm.out_shape,
                          scratch_shapes=comm.scratch)(*arrs)


def _gather_chips(name, arrs):
    return _comm_call(name, _GatherChips(arrs), arrs)


def _swap_halves(name, parts):
    n = len(parts)

    def body(*refs):
        ins, outs = refs[:n], refs[n:2 * n]
        send_sems, recv_sems = refs[2 * n:]
        x, y, c, me, sibling, chips = _place()
        cps = []
        for a in range(n):
            cp = pltpu.make_async_remote_copy(src_ref=ins[a].at[:, 1 - c], dst_ref=outs[a], send_sem=send_sems.at[a],
                                              recv_sem=recv_sems.at[a], device_id=sibling, device_id_type=MESH)
            cp.start()
            cps.append(cp)
        for cp in cps:
            cp.wait()

    return pl.pallas_call(
        body, name=name, in_specs=[ANY] * n, out_specs=[ANY] * n,
        out_shape=[jax.ShapeDtypeStruct((N_CHIPS,) + p.shape[2:], p.dtype) for p in parts],
        scratch_shapes=[pltpu.SemaphoreType.DMA((n,)), pltpu.SemaphoreType.DMA((n,))],
    )(*parts)


def _scatter_chips(name, sums):
    return _comm_call(name, _ScatterChips(sums), sums)


def _swap_reduced(name, halves):
    n = len(halves)

    def body(*refs):
        ins, outs = refs[:n], refs[n:2 * n]
        send_sems, recv_sems = refs[2 * n:]
        x, y, c, me, sibling, chips = _place()
        cps = []
        for a in range(n):
            cp = pltpu.make_async_remote_copy(src_ref=ins[a], dst_ref=outs[a], send_sem=send_sems.at[a],
                                              recv_sem=recv_sems.at[a], device_id=sibling, device_id_type=MESH)
            cp.start()
            cps.append(cp)
        for cp in cps:
            cp.wait()

    return pl.pallas_call(
        body, name=name, in_specs=[ANY] * n, out_specs=[ANY] * n,
        out_shape=[jax.ShapeDtypeStruct(h.shape, h.dtype) for h in halves],
        scratch_shapes=[pltpu.SemaphoreType.DMA((n,)), pltpu.SemaphoreType.DMA((n,))],
    )(*halves)


def _row_tile(r, want=256):
    t = (min(r, want) // SUBLANES) * SUBLANES
    while r % t:
        t -= SUBLANES
    return t


def _add_own_half(name, part, got, c, out_dtype):
    _, _, r, cols = part.shape
    tr = _row_tile(r)

    def body(c_ref, p_ref, g_ref, o_ref):
        o_ref[...] = (p_ref[...].astype(F32) + g_ref[...].astype(F32)).astype(o_ref.dtype)

    return pl.pallas_call(
        body, name=name,
        grid_spec=pltpu.PrefetchScalarGridSpec(
            num_scalar_prefetch=1, grid=(N_CHIPS, r // tr),
            in_specs=[pl.BlockSpec((None, None, tr, cols), lambda s, i, c_ref: (s, c_ref[0], i, 0)),
                      pl.BlockSpec((None, tr, cols), lambda s, i, c_ref: (s, i, 0))],
            out_specs=pl.BlockSpec((None, tr, cols), lambda s, i, c_ref: (s, i, 0))),
        out_shape=jax.ShapeDtypeStruct(got.shape, out_dtype),
        compiler_params=_cparams(("parallel", "parallel")),
    )(c.reshape(1).astype(jnp.int32), part, got)


def _sum_chips(name, got, own, me):
    _, r, cols = got.shape
    tr = _row_tile(r)

    def body(me_ref, r0, r1, r2, r3, own_ref, o_ref):
        pick = lambda s, ref: jnp.where(me_ref[0] == s, own_ref[...], ref[...]).astype(F32)
        o_ref[...] = ((pick(0, r0) + pick(1, r1)) + pick(2, r2)) + pick(3, r3)

    def slot(s):
        return pl.BlockSpec((None, tr, cols),
                            lambda i, me_ref: (jnp.where(me_ref[0] == s, (s + 1) % N_CHIPS, s), i, 0))

    return pl.pallas_call(
        body, name=name,
        grid_spec=pltpu.PrefetchScalarGridSpec(
            num_scalar_prefetch=1, grid=(r // tr,),
            in_specs=[slot(s) for s in range(N_CHIPS)]
            + [pl.BlockSpec((None, tr, cols), lambda i, me_ref: (me_ref[0], i, 0))],
            out_specs=pl.BlockSpec((tr, cols), lambda i, me_ref: (i, 0))),
        out_shape=jax.ShapeDtypeStruct((r, cols), F32),
        compiler_params=_cparams(("parallel",)),
    )(me.reshape(1).astype(jnp.int32), got, got, got, got, own)


def _adamw_math(wv, gv, mv, vv):
    mv = ADAM_B1 * mv + (1.0 - ADAM_B1) * gv
    vv = ADAM_B2 * vv + (1.0 - ADAM_B2) * (gv * gv)
    m_hat = mv / (1.0 - ADAM_B1 ** ADAM_STEP)
    v_hat = vv / (1.0 - ADAM_B2 ** ADAM_STEP)
    return -ADAM_LR * (m_hat / (jnp.sqrt(v_hat) + ADAM_EPS) + ADAM_WD * wv), mv, vv


def _adamw(name, w, g, m, v):
    cols = w.shape[1]
    return _ew(name, _adamw_math, [('r', w), ('r', g), ('r', m), ('r', v)], [('r', cols, F32)] * 3,
               tr=_row_tile(w.shape[0], 128))


def _adamw_halves(name, w, mine, theirs, m, v, c):
    r, cols = mine.shape
    tr = _row_tile(r, 128)
    nb = r // tr

    def body(c_ref, w_ref, a_ref, b_ref, m_ref, v_ref, g_out, d_out, m_out, v_out):
        g = jnp.where(pl.program_id(0) == c_ref[0], a_ref[...], b_ref[...])
        g_out[...] = g
        d_out[...], m_out[...], v_out[...] = _adamw_math(w_ref[...], g, m_ref[...], v_ref[...])

    whole = pl.BlockSpec((tr, cols), lambda h, i, c_ref: (h * nb + i, 0))
    half = pl.BlockSpec((tr, cols), lambda h, i, c_ref: (i, 0))
    return pl.pallas_call(
        body, name=name,
        grid_spec=pltpu.PrefetchScalarGridSpec(
            num_scalar_prefetch=1, grid=(2, nb),
            in_specs=[whole, half, half, whole, whole], out_specs=[whole] * 4),
        out_shape=[jax.ShapeDtypeStruct(w.shape, F32)] * 4,
        compiler_params=_cparams(("parallel", "parallel")),
    )(c.reshape(1).astype(jnp.int32), w, mine, theirs, m, v)


SMALL_ROWS_ALIGN = 2 * N_CHIPS * SUBLANES


MEDIUM_NAMES = ['ssm_b_re', 'ssm_b_im', 'ssm_c_re', 'ssm_c_im']
PACKED_NAMES = [n for n in SMALL_NAMES if n not in MEDIUM_NAMES]


def _pack_small(d):
    flat = jnp.concatenate([d[n].reshape(-1).astype(F32) for n in PACKED_NAMES])
    rows = -(-flat.shape[0] // (LANES * SMALL_ROWS_ALIGN)) * SMALL_ROWS_ALIGN
    return jnp.pad(flat, (0, rows * LANES - flat.shape[0])).reshape(rows, LANES)


def _unpack_small(packed, like):
    flat = packed.reshape(-1)
    out, off = {}, 0
    for n in PACKED_NAMES:
        size = like[n].size
        out[n] = flat[off:off + size].reshape(like[n].shape)
        off += size
    return out


def kernel(x, g_mix, w_in, q_gain, k_gain, rpb, ssm_a_re, ssm_a_im, ssm_b_re, ssm_b_im, ssm_c_re, ssm_c_im, ssm_log_step, ssm_d, w_glu, b_glu, g_out_attn, g_out_ssm, w_out, g_ffn, w_ffn_gate, w_ffn_up, w_ffn_down, loss_target, m_g_mix, m_w_in, m_q_gain, m_k_gain, m_rpb, m_ssm_a_re, m_ssm_a_im, m_ssm_b_re, m_ssm_b_im, m_ssm_c_re, m_ssm_c_im, m_ssm_log_step, m_ssm_d, m_w_glu, m_b_glu, m_g_out_attn, m_g_out_ssm, m_w_out, m_g_ffn, m_w_ffn_gate, m_w_ffn_up, m_w_ffn_down, v_g_mix, v_w_in, v_q_gain, v_k_gain, v_rpb, v_ssm_a_re, v_ssm_a_im, v_ssm_b_re, v_ssm_b_im, v_ssm_c_re, v_ssm_c_im, v_ssm_log_step, v_ssm_d, v_w_glu, v_b_glu, v_g_out_attn, v_g_out_ssm, v_w_out, v_g_ffn, v_w_ffn_gate, v_w_ffn_up, v_w_ffn_down):
    given = dict(locals())
    w = {n: given[n][0] for n in WEIGHT_NAMES}
    mom = {n: given["m_" + n][0] for n in WEIGHT_NAMES}
    var = {n: given["v_" + n][0] for n in WEIGHT_NAMES}
    d = x.shape[-1]
    c = lax.axis_index("c")

    halves = {n: w[n].astype(MXU_DTYPE).reshape((2, w[n].shape[0] // 2, w[n].shape[1])) for n in BIG_NAMES}
    (w_in4,) = _gather_chips("gather_w_in", [halves['w_in']])
    w_in4 = w_in4.reshape((N_CHIPS, -1, w_in4.shape[-1]))

    def chip_sums(tag, grads, payload):
        parts = [g.reshape((N_CHIPS, 2, -1, g.shape[-1])) for g in grads]
        got = _swap_halves("reduce_swap_halves_" + tag, parts)
        return [_add_own_half("reduce_add_%s_%d" % (tag, a), p, gt, c, dt)
                for a, (p, gt, dt) in enumerate(zip(parts, got, payload))]

    reduce_late = lambda grads: chip_sums("late", grads, [GRAD_PAYLOAD_DTYPE] * len(grads))
    sq, dx, (sums_late, got_late), d_w_in4, d_small = _local_step(
        x[0], loss_target[0], w_in4, ('halves', [halves[n] for n in LATE_NAMES]), {n: w[n] for n in SMALL_NAMES},
        reduce_late)
    loss = lax.psum(0.5 * sq / d, ("x", "y", "c"))

    nbig = len(BIG_NAMES)
    rest = [d_w_in4] + [d_small[n].reshape(-1, LANES) for n in MEDIUM_NAMES] + [_pack_small(d_small)]
    sums_rest = chip_sums("rest", rest, [GRAD_PAYLOAD_DTYPE] + [F32] * (len(rest) - 1))
    got_rest = list(_scatter_chips("reduce_scatter_chips", sums_rest))
    sums = sums_rest[:1] + sums_late + sums_rest[1:]
    got = got_rest[:1] + got_late + got_rest[1:]
    me = 2 * lax.axis_index("x") + lax.axis_index("y")
    mine = [_sum_chips("reduce_sum_%d" % a, gt, sm_, me) for a, (gt, sm_) in enumerate(zip(got, sums))]
    theirs = _swap_reduced("reduce_swap_reduced", mine)
    in_order = lambda a: jnp.where(c == 0, jnp.stack([mine[a], theirs[a]]), jnp.stack([theirs[a], mine[a]]))
    repl = _gather_chips("gather_small", [in_order(a) for a in range(nbig, len(mine))])
    repl = [r.reshape(-1, LANES) for r in repl]
    like = {n: w[n] for n in SMALL_NAMES}
    grad_small = _unpack_small(repl[-1], like)
    grad_small.update({n: r.reshape(w[n].shape) for n, r in zip(MEDIUM_NAMES, repl)})

    grad_big, delta, new_m, new_v = {}, {}, {}, {}
    for a, n in enumerate(BIG_NAMES):
        grad_big[n], delta[n], new_m[n], new_v[n] = _adamw_halves("adamw_%d" % a, w[n], mine[a], theirs[a],
                                                                  mom[n], var[n], c)
    for n, r in zip(MEDIUM_NAMES, repl):
        res = _adamw("adamw_" + n, w[n].reshape(-1, LANES), r, mom[n].reshape(-1, LANES), var[n].reshape(-1, LANES))
        delta[n], new_m[n], new_v[n] = (t.reshape(w[n].shape) for t in res)
    sd, sm, sv = _adamw("adamw_small", _pack_small(w), repl[-1], _pack_small(mom), _pack_small(var))
    delta.update(_unpack_small(sd, like))
    new_m.update(_unpack_small(sm, like))
    new_v.update(_unpack_small(sv, like))
    grads = {**grad_big, **grad_small}
    lead = lambda t: t[None]
    return (loss, dx[None], *[lead(grads[n]) for n in WEIGHT_NAMES], *[lead(delta[n]) for n in WEIGHT_NAMES],
            *[lead(new_m[n]) for n in WEIGHT_NAMES], *[lead(new_v[n]) for n in WEIGHT_NAMES])
```

```python
import functools
import math

import jax
import jax.numpy as jnp
from jax import lax
from jax.experimental import pallas as pl
from jax.experimental.pallas import tpu as pltpu

F32 = jnp.float32
BF16 = jnp.bfloat16
MXU_DTYPE = BF16
GRAD_PAYLOAD_DTYPE = BF16
DW_DTYPE = BF16
S5_DTYPE = BF16
HI = lax.Precision.HIGHEST
VMEM_LIMIT_V7X = 56 * 1024 * 1024
LANES = 128
SUBLANES = 8

GRID_W = 64
WIN_H = 8
WIN_W = 16
HEAD_DIM = 64
SSM_GROUP_CH = 16
SSM_STATE = 64
S5_CHUNK = 16
S5_GROUPS_PER_STEP = 8
RMS_EPS = 1e-6
NEG_INF = -1e30
N_CHIPS = 4
MESH = pl.DeviceIdType.MESH

ADAM_LR = 0.001
ADAM_B1 = 0.9
ADAM_B2 = 0.999
ADAM_EPS = 1e-08
ADAM_WD = 0.01
ADAM_STEP = 10

WEIGHT_NAMES = ['g_mix', 'w_in', 'q_gain', 'k_gain', 'rpb', 'ssm_a_re', 'ssm_a_im', 'ssm_b_re', 'ssm_b_im',
                'ssm_c_re', 'ssm_c_im', 'ssm_log_step', 'ssm_d', 'w_glu', 'b_glu', 'g_out_attn', 'g_out_ssm',
                'w_out', 'g_ffn', 'w_ffn_gate', 'w_ffn_up', 'w_ffn_down']
BIG_NAMES = ['w_in', 'w_glu', 'w_out', 'w_ffn_gate', 'w_ffn_up', 'w_ffn_down']
LATE_NAMES = BIG_NAMES[1:]
SMALL_NAMES = [n for n in WEIGHT_NAMES if n not in BIG_NAMES]


def _cparams(sem):
    return pltpu.CompilerParams(dimension_semantics=sem, vmem_limit_bytes=VMEM_LIMIT_V7X)


def _tile(n, want):
    if n <= want:
        return n
    t = (want // LANES) * LANES
    while t >= LANES:
        if n % t == 0:
            return t
        t -= LANES
    return n


def _mm(name, a, b, *, contract, a_mode='2', b_mode='2', o_mode='2', out_dtype=F32, add=None, exact=False,
        tm=1024, tn=1024, tk=2048, comm=None, comm_arrs=()):
    dn = {'nn': (((1,), (0,)), ((), ())), 'nt': (((1,), (1,)), ((), ())), 'tn': (((0,), (0,)), ((), ()))}[contract]
    ar, ac = a.shape[-2:]
    br, bc = b.shape[-2:]
    m, kdim = (ar, ac) if contract != 'tn' else (ac, ar)
    n = bc if contract != 'nt' else br
    assert kdim == (br if contract != 'nt' else bc), (name, a.shape, b.shape)
    nbatch = 1
    for arr, mode in ((a, a_mode), (b, b_mode)):
        if mode == 'b':
            nbatch = arr.shape[0]
    nstack = 1
    for arr, mode in ((a, a_mode), (b, b_mode)):
        if mode == 'c':
            nstack = arr.shape[0]
    tm, tn, tk = _tile(m, tm), _tile(n, tn), _tile(kdim, tk)
    nkin = kdim // tk
    nk = nstack * nkin
    grid = (nbatch, m // tm, n // tn, nk)

    def spec(mode, block, rc):
        def imap(s, i, j, kk):
            r, c = rc(i, j, kk % nkin)
            if mode == '2':
                return (r, c)
            return (s if mode == 'b' else kk // nkin, r, c)
        return pl.BlockSpec(block if mode == '2' else (None,) + block, imap)

    a_spec = spec(a_mode, (tm, tk) if contract != 'tn' else (tk, tm),
                  (lambda i, j, k: (i, k)) if contract != 'tn' else (lambda i, j, k: (k, i)))
    b_spec = spec(b_mode, (tk, tn) if contract != 'nt' else (tn, tk),
                  (lambda i, j, k: (k, j)) if contract != 'nt' else (lambda i, j, k: (j, k)))
    o_spec = spec(o_mode, (tm, tn), lambda i, j, k: (i, j))
    out_shape = (m, n) if o_mode == '2' else (nbatch, m, n)
    has_add = add is not None

    def product(a_ref, b_ref):
        if exact:
            return lax.dot_general(a_ref[...].astype(F32), b_ref[...].astype(F32), dn, precision=HI,
                                   preferred_element_type=F32)
        return lax.dot_general(a_ref[...].astype(MXU_DTYPE), b_ref[...].astype(MXU_DTYPE), dn,
                               preferred_element_type=F32)

    ncomm = len(comm_arrs)
    nacc = int(nk > 1)

    def body(*refs):
        a_ref, b_ref = refs[:2]
        add_ref = refs[2] if has_add else None
        c_ins = refs[2 + has_add:2 + has_add + ncomm]
        o_ref = refs[2 + has_add + ncomm]
        c_outs = refs[3 + has_add + ncomm:3 + has_add + 2 * ncomm]
        sems = refs[3 + has_add + 2 * ncomm + nacc:]
        ids = [pl.program_id(ax) for ax in range(4)]
        if comm is not None:
            @pl.when((ids[0] == 0) & (ids[1] == 0) & (ids[2] == 0) & (ids[3] == 0))
            def _():
                comm.start(c_ins, c_outs, *sems)

        def write(r):
            if has_add:
                r = r + add_ref[...].astype(F32)
            o_ref[...] = r.astype(o_ref.dtype)

        if nk == 1:
            write(product(a_ref, b_ref))
        else:
            acc_ref = refs[3 + has_add + 2 * ncomm]

            @pl.when(ids[3] == 0)
            def _():
                acc_ref[...] = jnp.zeros_like(acc_ref)

            acc_ref[...] += product(a_ref, b_ref)

            @pl.when(ids[3] == nk - 1)
            def _():
                write(acc_ref[...])

        if comm is not None:
            @pl.when((ids[0] == grid[0] - 1) & (ids[1] == grid[1] - 1) & (ids[2] == grid[2] - 1) & (ids[3] == nk - 1))
            def _():
                comm.finish(c_ins, c_outs, *sems)

    in_specs = [a_spec, b_spec] + ([o_spec] if has_add else []) + [ANY] * ncomm
    args = (a, b) + ((add,) if has_add else ()) + tuple(comm_arrs)
    res = pl.pallas_call(
        body, name=name, grid=grid, in_specs=in_specs, out_specs=[o_spec] + [ANY] * ncomm,
        out_shape=[jax.ShapeDtypeStruct(out_shape, out_dtype)] + (comm.out_shape if comm is not None else []),
        scratch_shapes=([pltpu.VMEM((tm, tn), F32)] if nk > 1 else []) + (comm.scratch if comm is not None else []),
        compiler_params=_cparams(("parallel", "parallel", "parallel", "arbitrary") if comm is None
                                 else ("arbitrary",) * 4),
    )(*args)
    return res[0] if comm is None else (res[0], res[1:])


def _ew(name, fn, ins, outs, tr=256):
    rows = next(x[1].shape[0] for x in ins if x[0] == 'r')
    tr = min(tr, rows)
    assert rows % tr == 0 and tr % SUBLANES == 0, (name, rows, tr)
    in_specs, args = [], []
    for x in ins:
        if x[0] == 'r' and len(x) == 2:
            in_specs.append(pl.BlockSpec((tr, x[1].shape[1]), lambda i: (i, 0)))
        elif x[0] == 'r':
            in_specs.append(pl.BlockSpec((tr, x[3]), functools.partial(lambda cb, i: (i, cb), x[2])))
        else:
            in_specs.append(pl.BlockSpec(x[1].shape, lambda i: (0, 0)))
        args.append(x[1])
    out_specs, out_shapes = [], []
    for o in outs:
        if o[0] == 'r':
            out_specs.append(pl.BlockSpec((tr, o[1]), lambda i: (i, 0)))
            out_shapes.append(jax.ShapeDtypeStruct((rows, o[1]), o[2]))
        else:
            out_specs.append(pl.BlockSpec((SUBLANES, o[1]), lambda i: (0, 0)))
            out_shapes.append(jax.ShapeDtypeStruct((SUBLANES, o[1]), F32))
    nin = len(ins)
    has_acc = any(o[0] == 'a' for o in outs)

    def body(*refs):
        vals = fn(*[r[...].astype(F32) for r in refs[:nin]])
        if not isinstance(vals, (tuple, list)):
            vals = (vals,)
        i = pl.program_id(0)
        for o, ref, v in zip(outs, refs[nin:], vals):
            if o[0] == 'r':
                ref[...] = v.astype(ref.dtype)
            else:
                part = v.astype(F32).reshape(tr // SUBLANES, SUBLANES, o[1]).sum(axis=0)

                @pl.when(i == 0)
                def _(ref=ref, part=part):
                    ref[...] = part

                @pl.when(i > 0)
                def _(ref=ref, part=part):
                    ref[...] += part

    res = pl.pallas_call(
        body, name=name, grid=(rows // tr,), in_specs=in_specs, out_specs=out_specs, out_shape=out_shapes,
        compiler_params=_cparams(("arbitrary",) if has_acc else ("parallel",)),
    )(*args)
    return res


def _rms(x, g):
    r = lax.rsqrt(jnp.mean(x * x, axis=-1, keepdims=True) + RMS_EPS)
    xr = x * r
    return xr * g, xr


def _rms_bwd(x, g, dy):
    r = lax.rsqrt(jnp.mean(x * x, axis=-1, keepdims=True) + RMS_EPS)
    xr = x * r
    gdy = g * dy
    dx = r * (gdy - xr * jnp.mean(xr * gdy, axis=-1, keepdims=True))
    return dx, dy * xr


def _sigmoid(x):
    return 0.5 * (jnp.tanh(0.5 * x) + 1.0)


_GELU_C = math.sqrt(2.0 / math.pi)


def _gelu(x):
    return 0.5 * x * (1.0 + jnp.tanh(_GELU_C * (x + 0.044715 * x * x * x)))


def _gelu_grad(x):
    t = jnp.tanh(_GELU_C * (x + 0.044715 * x * x * x))
    return 0.5 * (1.0 + t) + 0.5 * x * (1.0 - t * t) * _GELU_C * (1.0 + 3 * 0.044715 * x * x)


ATTN_ROWS_PER_STEP = 8
NT_DIMS = (((1,), (1,)), ((), ()))
NN_DIMS = (((1,), (0,)), ((), ()))
TN_DIMS = (((0,), (0,)), ((), ()))


def _attn_geometry(r, rows):
    row_start = jnp.clip(r - WIN_H // 2, 0, rows - WIN_H)
    key0 = pl.multiple_of(row_start * GRID_W, GRID_W)
    bias0 = pl.multiple_of((row_start - r + (WIN_H - 1)) * GRID_W, GRID_W)
    return key0, bias0


def _window_onehot():
    c = jnp.arange(GRID_W)
    col_start = jnp.clip(c - WIN_W // 2, 0, GRID_W - WIN_W)
    col_in = (c[None, :] >= col_start[:, None]) & (c[None, :] < col_start[:, None] + WIN_W)
    dc = jnp.clip(c[None, :] - c[:, None], -(WIN_W - 1), WIN_W - 1) + (WIN_W - 1)
    onehot = ((dc[:, :, None] == jnp.arange(2 * WIN_W - 1)[None, None, :]) & col_in[:, :, None]).astype(F32)
    return onehot, col_in


def _bias_table(rpb):
    onehot, col_in = _window_onehot()
    nh = rpb.shape[0]
    tab = jnp.einsum('perd,qkd->prkeq', rpb.reshape(nh // 2, 2, 2 * WIN_H - 1, 2 * WIN_W - 1), onehot, precision=HI)
    tab = tab + jnp.where(col_in, 0.0, NEG_INF).T[None, None, :, None, :]
    return tab.reshape(nh // 2, (2 * WIN_H - 1) * GRID_W, 2 * GRID_W)


def _bias_table_grad(dtab):
    onehot, _ = _window_onehot()
    npair = dtab.shape[0]
    d = dtab.reshape(npair, 2 * WIN_H - 1, GRID_W, 2, GRID_W)
    return jnp.einsum('prkeq,qkd->perd', d, onehot, precision=HI).reshape(2 * npair, 2 * WIN_H - 1, 2 * WIN_W - 1)


def _lane_lo(shape):
    return lax.broadcasted_iota(jnp.int32, shape, 1) < HEAD_DIM


def _half_sums(v):
    lo = _lane_lo(v.shape)
    s_lo = jnp.sum(jnp.where(lo, v, 0.0), axis=1, keepdims=True)
    s_hi = jnp.sum(jnp.where(lo, 0.0, v), axis=1, keepdims=True)
    return jnp.where(lo, s_lo, s_hi)


def _rms_pair(x, g):
    r = lax.rsqrt(_half_sums(x * x) * (1.0 / HEAD_DIM) + RMS_EPS)
    return x * r * g


def _rms_pair_bwd(x, g, dy):
    r = lax.rsqrt(_half_sums(x * x) * (1.0 / HEAD_DIM) + RMS_EPS)
    xr = x * r
    gdy = g * dy
    dx = r * (gdy - xr * (_half_sums(xr * gdy) * (1.0 / HEAD_DIM)))
    return dx, dy * xr


def _blockdiag(a):
    a2 = jnp.concatenate([a, a], axis=0)
    row_hi = lax.broadcasted_iota(jnp.int32, a2.shape, 0) >= GRID_W
    lane_hi = lax.broadcasted_iota(jnp.int32, a2.shape, 1) >= HEAD_DIM
    return jnp.where(row_hi == lane_hi, a2, 0.0).astype(MXU_DTYPE)


def _diag_blocks(m):
    return jnp.where(_lane_lo((GRID_W, 2 * HEAD_DIM)), m[:GRID_W], m[GRID_W:])


def _attn_scores(qb, kb, bias):
    st = lax.dot_general(kb, qb, NT_DIMS, preferred_element_type=F32)
    st = st * (1.0 / math.sqrt(HEAD_DIM)) + bias
    mx = jnp.max(st, axis=0, keepdims=True)
    p = jnp.exp(st - mx)
    return p * (1.0 / jnp.sum(p, axis=0, keepdims=True))


def _attn_fwd(z4, qg2, kg2, bias_t, comm=None, comm_arrs=()):
    _, t, aw = z4.shape
    rows = t // GRID_W
    npair = aw // (2 * HEAD_DIM)
    nkeys = WIN_H * GRID_W
    nb = bias_t.shape[1]
    rps = min(ATTN_ROWS_PER_STEP, rows)
    blk = rps * GRID_W
    nsteps = rows // rps
    ncomm = len(comm_arrs)

    def body(*refs):
        q_ref, k_ref, v_ref, qg_ref, kg_ref, b_ref = refs[:6]
        c_ins, o_ref, c_outs = refs[6:6 + ncomm], refs[6 + ncomm], refs[7 + ncomm:7 + 2 * ncomm]
        kn_ref, vb_ref = refs[7 + 2 * ncomm:9 + 2 * ncomm]
        sems = refs[9 + 2 * ncomm:]
        pr, rb = pl.program_id(0), pl.program_id(1)
        if comm is not None:
            @pl.when((pr == 0) & (rb == 0))
            def _():
                comm.start(c_ins, c_outs, *sems)

        @pl.when(rb == 0)
        def _():
            kn_ref[...] = _rms_pair(k_ref[...], kg_ref[...]).astype(MXU_DTYPE)
            vb_ref[...] = v_ref[...].astype(MXU_DTYPE)

        def row(i, carry):
            key0, bias0 = _attn_geometry(rb * rps + i, rows)
            at = pl.ds(pl.multiple_of(i * GRID_W, GRID_W), GRID_W)
            qb = _blockdiag(_rms_pair(q_ref[at, :], qg_ref[...]))
            pt = _attn_scores(qb, kn_ref[pl.ds(key0, nkeys), :], b_ref[pl.ds(bias0, nkeys), :])
            both = lax.dot_general(pt.astype(MXU_DTYPE), vb_ref[pl.ds(key0, nkeys), :], TN_DIMS,
                                   preferred_element_type=F32)
            o_ref[at, :] = _diag_blocks(both)
            return carry

        lax.fori_loop(0, rps, row, 0, unroll=2)
        if comm is not None:
            @pl.when((pr == npair - 1) & (rb == nsteps - 1))
            def _():
                comm.finish(c_ins, c_outs, *sems)

    pair_cols = lambda lead: pl.BlockSpec((None, t, 2 * HEAD_DIM), lambda p, r: (lead, 0, p))
    res = pl.pallas_call(
        body, name="attn_fwd", grid=(npair, nsteps),
        in_specs=[pl.BlockSpec((None, blk, 2 * HEAD_DIM), lambda p, r: (0, r, p)), pair_cols(1), pair_cols(2),
                  pl.BlockSpec((1, 2 * HEAD_DIM), lambda p, r: (0, 0)),
                  pl.BlockSpec((1, 2 * HEAD_DIM), lambda p, r: (0, 0)),
                  pl.BlockSpec((None, nb, 2 * GRID_W), lambda p, r: (p, 0, 0))] + [ANY] * ncomm,
        out_specs=[pl.BlockSpec((blk, 2 * HEAD_DIM), lambda p, r: (r, p))] + [ANY] * ncomm,
        out_shape=[jax.ShapeDtypeStruct((t, aw), F32)] + (comm.out_shape if comm is not None else []),
        scratch_shapes=[pltpu.VMEM((t, 2 * HEAD_DIM), MXU_DTYPE), pltpu.VMEM((t, 2 * HEAD_DIM), MXU_DTYPE)]
        + (comm.scratch if comm is not None else []),
        compiler_params=_cparams(("arbitrary", "arbitrary")),
    )(z4, z4, z4, qg2, kg2, bias_t, *comm_arrs)
    return res[0], res[1:]


def _attn_bwd(z4, qg2, kg2, bias_t, dya, comm=None, comm_arrs=()):
    _, t, aw = z4.shape
    rows = t // GRID_W
    npair = aw // (2 * HEAD_DIM)
    nkeys = WIN_H * GRID_W
    nb = bias_t.shape[1]
    rps = min(ATTN_ROWS_PER_STEP, rows)
    blk = rps * GRID_W
    nsteps = rows // rps
    scale = 1.0 / math.sqrt(HEAD_DIM)
    ncomm = len(comm_arrs)

    def body(*refs):
        q_ref, k_ref, v_ref, qg_ref, kg_ref, b_ref, do_ref = refs[:7]
        c_ins = refs[7:7 + ncomm]
        dz_ref, db_ref, dqg_ref, dkg_ref = refs[7 + ncomm:11 + ncomm]
        c_outs = refs[11 + ncomm:11 + 2 * ncomm]
        kn_ref, vb_ref, dkn_ref, dv_ref = refs[11 + 2 * ncomm:15 + 2 * ncomm]
        sems = refs[15 + 2 * ncomm:]
        pr, rb = pl.program_id(0), pl.program_id(1)
        if comm is not None:
            @pl.when((pr == 0) & (rb == 0))
            def _():
                comm.start(c_ins, c_outs, *sems)

        @pl.when(rb == 0)
        def _():
            kn_ref[...] = _rms_pair(k_ref[...], kg_ref[...]).astype(MXU_DTYPE)
            vb_ref[...] = v_ref[...].astype(MXU_DTYPE)
            dkn_ref[...] = jnp.zeros_like(dkn_ref)
            dv_ref[...] = jnp.zeros_like(dv_ref)
            db_ref[...] = jnp.zeros_like(db_ref)
            dqg_ref[...] = jnp.zeros_like(dqg_ref)

        def row(i, dqg_sum):
            r = rb * rps + i
            key0, bias0 = _attn_geometry(r, rows)
            keys = pl.ds(key0, nkeys)
            at = pl.ds(pl.multiple_of(i * GRID_W, GRID_W), GRID_W)
            q = q_ref[at, :]
            qb = _blockdiag(_rms_pair(q, qg_ref[...]))
            dob = _blockdiag(do_ref[at, :])
            kb = kn_ref[keys, :]
            pt = _attn_scores(qb, kb, b_ref[pl.ds(bias0, nkeys), :])
            dv_ref[keys, :] += lax.dot_general(pt.astype(MXU_DTYPE), dob, NN_DIMS, preferred_element_type=F32)
            dpt = lax.dot_general(vb_ref[keys, :], dob, NT_DIMS, preferred_element_type=F32)
            dst = pt * (dpt - jnp.sum(pt * dpt, axis=0, keepdims=True))
            db_ref[pl.ds(bias0, nkeys), :] += dst
            dsb = dst.astype(MXU_DTYPE)
            dkn_ref[keys, :] += scale * lax.dot_general(dsb, qb, NN_DIMS, preferred_element_type=F32)
            dqn = scale * _diag_blocks(lax.dot_general(dsb, kb, TN_DIMS, preferred_element_type=F32))
            dq, dqg = _rms_pair_bwd(q, qg_ref[...], dqn)
            dz_ref[0, pl.ds(pl.multiple_of(r * GRID_W, GRID_W), GRID_W), :] = dq.astype(dz_ref.dtype)
            return dqg_sum + jnp.sum(dqg, axis=0, keepdims=True)

        dqg_ref[...] += lax.fori_loop(0, rps // 2, lambda i, acc: row(2 * i + 1, row(2 * i, acc)),
                                      jnp.zeros((1, 2 * HEAD_DIM), F32))

        @pl.when(rb == nsteps - 1)
        def _():
            dk, dkg = _rms_pair_bwd(k_ref[...], kg_ref[...], dkn_ref[...])
            dz_ref[1] = dk.astype(dz_ref.dtype)
            dz_ref[2] = dv_ref[...].astype(dz_ref.dtype)
            dkg_ref[...] = jnp.sum(dkg, axis=0, keepdims=True)

        if comm is not None:
            @pl.when((pr == npair - 1) & (rb == nsteps - 1))
            def _():
                comm.finish(c_ins, c_outs, *sems)

    pair_cols = lambda lead: pl.BlockSpec((None, t, 2 * HEAD_DIM), lambda p, r: (lead, 0, p))
    pair_vec = pl.BlockSpec((None, 1, 2 * HEAD_DIM), lambda p, r: (p, 0, 0))
    res = pl.pallas_call(
        body, name="attn_bwd", grid=(npair, nsteps),
        in_specs=[pl.BlockSpec((None, blk, 2 * HEAD_DIM), lambda p, r: (0, r, p)), pair_cols(1), pair_cols(2),
                  pl.BlockSpec((1, 2 * HEAD_DIM), lambda p, r: (0, 0)),
                  pl.BlockSpec((1, 2 * HEAD_DIM), lambda p, r: (0, 0)),
                  pl.BlockSpec((None, nb, 2 * GRID_W), lambda p, r: (p, 0, 0)),
                  pl.BlockSpec((blk, 2 * HEAD_DIM), lambda p, r: (r, p))] + [ANY] * ncomm,
        out_specs=[pl.BlockSpec((3, t, 2 * HEAD_DIM), lambda p, r: (0, 0, p)),
                   pl.BlockSpec((None, nb, 2 * GRID_W), lambda p, r: (p, 0, 0)),
                   pair_vec, pair_vec] + [ANY] * ncomm,
        out_shape=[jax.ShapeDtypeStruct((4, t, aw), MXU_DTYPE), jax.ShapeDtypeStruct(bias_t.shape, F32),
                   jax.ShapeDtypeStruct((npair, 1, 2 * HEAD_DIM), F32),
                   jax.ShapeDtypeStruct((npair, 1, 2 * HEAD_DIM), F32)] + (comm.out_shape if comm is not None else []),
        scratch_shapes=[pltpu.VMEM((t, 2 * HEAD_DIM), MXU_DTYPE), pltpu.VMEM((t, 2 * HEAD_DIM), MXU_DTYPE),
                        pltpu.VMEM((t, 2 * HEAD_DIM), F32), pltpu.VMEM((t, 2 * HEAD_DIM), F32)]
        + (comm.scratch if comm is not None else []),
        compiler_params=_cparams(("arbitrary", "arbitrary")),
    )(z4, z4, z4, qg2, kg2, bias_t, dya, *comm_arrs)
    return res[:4], res[4:]


def _s5_mats(a_re, a_im, b_re, b_im, c_re, c_im, log_step, d_skip):
    nd, g, p = a_re.shape
    c = b_re.shape[-1]
    L = S5_CHUNK
    lr = jnp.minimum(a_re, -1e-4).transpose(1, 0, 2)
    li = a_im.transpose(1, 0, 2)
    dt = jnp.exp(log_step).T[..., None]
    n = jnp.arange(L + 1, dtype=F32)[None, :, None, None]
    mag = jnp.exp(n * (lr * dt)[:, None])
    ang = n * (li * dt)[:, None]
    pw_r, pw_i = mag * jnp.cos(ang), mag * jnp.sin(ang)
    den = lr * lr + li * li
    nr, ni = pw_r[:, 1] - 1.0, pw_i[:, 1]
    cr, ci = (nr * lr + ni * li) / den, (ni * lr - nr * li) / den
    bt_r, bt_i = b_re.transpose(1, 3, 0, 2), b_im.transpose(1, 3, 0, 2)
    bb_r = cr[:, None] * bt_r - ci[:, None] * bt_i
    bb_i = cr[:, None] * bt_i + ci[:, None] * bt_r
    ct_r, ct_i = c_re.transpose(1, 2, 0, 3), c_im.transpose(1, 2, 0, 3)

    def cols(x_re, x_im):
        return jnp.concatenate([x_re[..., 0, :], x_re[..., 1, :], x_im[..., 0, :], x_im[..., 1, :]], axis=-1)

    e_r = jnp.stack([pw_r[:, :L, 0][:, ::-1], pw_r[:, :L, 1]], axis=2)
    e_i = jnp.stack([pw_i[:, :L, 0][:, ::-1], pw_i[:, :L, 1]], axis=2)
    ws = (cols(e_r, e_r)[:, :, None] * cols(bb_r, bb_i)[:, None]
          + cols(e_i, e_i)[:, :, None] * cols(-bb_i, bb_r)[:, None]).reshape(g, L * c, 4 * p)
    f_r = jnp.stack([pw_r[:, 1:, 0], pw_r[:, 1:, 1][:, ::-1]], axis=2)
    f_i = jnp.stack([pw_i[:, 1:, 0], pw_i[:, 1:, 1][:, ::-1]], axis=2)
    wot = (cols(f_r, f_i)[:, :, None] * cols(ct_r, -ct_r)[:, None]
           + cols(f_i, f_r)[:, :, None] * cols(-ct_i, -ct_i)[:, None]).reshape(g, L * c, 4 * p)
    qr, qi = pw_r[:, :L, None], pw_i[:, :L, None]
    kp_r, kp_i = qr * bb_r[:, None] - qi * bb_i[:, None], qr * bb_i[:, None] + qi * bb_r[:, None]
    kern = [jnp.einsum('gnip,gop->gino', kp_r[:, :, :, d], ct_r[:, :, d], precision=HI)
            - jnp.einsum('gnip,gop->gino', kp_i[:, :, :, d], ct_i[:, :, d], precision=HI) for d in range(2)]
    skip = d_skip.reshape(g, c, 1, 1) * jnp.eye(c, dtype=F32)[None, :, None, :]
    by_offset = jnp.concatenate([kern[1][:, :, :0:-1], kern[0][:, :, :1] + kern[1][:, :, :1] + skip,
                                 kern[0][:, :, 1:]], axis=2).reshape(g, c, (2 * L - 1) * c)
    mt = jnp.stack([by_offset[:, :, (L - 1 - j) * c:(2 * L - 1 - j) * c] for j in range(L)], axis=1)
    mt = mt.reshape(g, L * c, L * c)
    lr16, li16 = pw_r[:, L], pw_i[:, L]
    fa = jnp.concatenate([lr16[:, 0], lr16[:, 1], lr16[:, 0], lr16[:, 1]], axis=-1)
    fb = jnp.concatenate([-li16[:, 0], -li16[:, 1], li16[:, 0], li16[:, 1]], axis=-1)
    return mt, ws, wot, fa, fb


def _gmm(name, a, b, contract, a_stacked=False, b_stacked=False, o_stacked=False, add=None, out_dtype=F32):
    w = S5_CHUNK * SSM_GROUP_CH
    g = (a.shape[0] if a_stacked else a.shape[1] // w)
    gpb = math.gcd(g, S5_GROUPS_PER_STEP)
    dn = {'nn': (((1,), (0,)), ((), ())), 'nt': (((1,), (1,)), ((), ())), 'tn': (((0,), (0,)), ((), ()))}[contract]

    def spec(arr, stacked):
        if stacked:
            return pl.BlockSpec((gpb,) + arr.shape[1:], lambda i: (i, 0, 0))
        return pl.BlockSpec((arr.shape[0], gpb * w), lambda i: (0, i))

    def take(ref, stacked, e):
        return ref[e] if stacked else ref[:, e * w:(e + 1) * w]

    m = (a.shape[1] if a_stacked else a.shape[0]) if contract != 'tn' else w
    n = w
    if o_stacked:
        o_spec = pl.BlockSpec((gpb, m, n), lambda i: (i, 0, 0))
        o_shape = (g, m, n)
    else:
        o_spec = pl.BlockSpec((m, gpb * n), lambda i: (0, i))
        o_shape = (m, g * n)
    has_add = add is not None

    def body(*refs):
        if has_add:
            a_ref, b_ref, add_ref, o_ref = refs
        else:
            a_ref, b_ref, o_ref = refs
        for e in range(gpb):
            r = lax.dot_general(take(a_ref, a_stacked, e).astype(S5_DTYPE), take(b_ref, b_stacked, e).astype(S5_DTYPE),
                                dn, precision=HI if S5_DTYPE == F32 else None, preferred_element_type=F32)
            if has_add:
                r = r + take(add_ref, o_stacked, e)
            if o_stacked:
                o_ref[e] = r.astype(o_ref.dtype)
            else:
                o_ref[:, e * w:(e + 1) * w] = r.astype(o_ref.dtype)

    in_specs = [spec(a, a_stacked), spec(b, b_stacked)] + ([o_spec] if has_add else [])
    return pl.pallas_call(
        body, name=name, grid=(g // gpb,), in_specs=in_specs, out_specs=o_spec,
        out_shape=jax.ShapeDtypeStruct(o_shape, out_dtype), compiler_params=_cparams(("parallel",)),
    )(*((a, b) + ((add,) if has_add else ())))


def _s5_scan(name, s, fa, fb, rev0, xin=None):
    nk, g, w = s.shape
    hw, qw = w // 2, w // 4
    gb = min(g, 16)
    with_acc = xin is not None

    def body(*refs):
        if with_acc:
            s_ref, a_ref, b_ref, x_ref, o_ref, pa_ref, pb_ref = refs
        else:
            s_ref, a_ref, b_ref, o_ref = refs
        fa_v, fb_v = a_ref[...], b_ref[...]
        dir0 = lax.broadcasted_iota(jnp.int32, (gb, w), 1) % hw < qw
        swap = lambda v: jnp.concatenate([v[:, hw:], v[:, :hw]], axis=1)

        def step(i, carry):
            x, pa, pb = carry
            k0 = (nk - 1 - i) if rev0 else i
            k1 = i if rev0 else (nk - 1 - i)
            for lo in (0, hw):
                o_ref[k0, :, lo:lo + qw] = x[:, lo:lo + qw]
                o_ref[k1, :, lo + qw:lo + hw] = x[:, lo + qw:lo + hw]
            if with_acc:
                xi = jnp.where(dir0, x_ref[k0], x_ref[k1])
                pa = pa + x * xi
                pb = pb + x * swap(xi)
            x = fa_v * x + fb_v * swap(x) + jnp.where(dir0, s_ref[k0], s_ref[k1])
            return x, pa, pb

        z = jnp.zeros((gb, w), F32)
        res = lax.fori_loop(0, nk, step, (z, z, z), unroll=2)
        if with_acc:
            pa_ref[...] = res[1]
            pb_ref[...] = res[2]

    seq = pl.BlockSpec((nk, gb, w), lambda i: (0, i, 0))
    vec = pl.BlockSpec((gb, w), lambda i: (i, 0))
    in_specs = [seq, vec, vec] + ([seq] if with_acc else [])
    out_specs = [seq] + ([vec, vec] if with_acc else [])
    out_shape = [jax.ShapeDtypeStruct((nk, g, w), F32)] + (
        [jax.ShapeDtypeStruct((g, w), F32)] * 2 if with_acc else [])
    return pl.pallas_call(
        body, name=name, grid=(g // gb,), in_specs=in_specs, out_specs=out_specs, out_shape=out_shape,
        compiler_params=_cparams(("parallel",)),
    )(*((s, fa, fb) + ((xin,) if with_acc else ())))


def _regroup(name, x, to_groups):
    if to_groups:
        t, sw = x.shape
    else:
        t, sw = x.shape[0] * S5_CHUNK, x.shape[1] // S5_CHUNK
    nk = t // S5_CHUNK
    wide = LANES * S5_CHUNK

    def place(tok):
        r = lax.broadcasted_iota(jnp.int32, (LANES, wide), 0)
        col = lax.broadcasted_iota(jnp.int32, (LANES, wide), 1)
        want = (r // SSM_GROUP_CH) * (S5_CHUNK * SSM_GROUP_CH) + tok * SSM_GROUP_CH + r % SSM_GROUP_CH
        return (col == want).astype(S5_DTYPE)

    def body(x_ref, o_ref):
        if to_groups:
            acc = jnp.zeros((nk, wide), F32)
            for tok in range(S5_CHUNK):
                rows = x_ref[pl.ds(tok, nk, stride=S5_CHUNK), :].astype(S5_DTYPE)
                acc = acc + lax.dot_general(rows, place(tok), NN_DIMS, preferred_element_type=F32)
            o_ref[...] = acc.astype(o_ref.dtype)
        else:
            xv = x_ref[...].astype(S5_DTYPE)
            for tok in range(S5_CHUNK):
                o_ref[pl.ds(tok, nk, stride=S5_CHUNK), :] = lax.dot_general(
                    xv, place(tok), NT_DIMS, preferred_element_type=F32).astype(o_ref.dtype)

    tokens = pl.BlockSpec((t, LANES), lambda i: (0, i))
    groups = pl.BlockSpec((nk, wide), lambda i: (0, i))
    return pl.pallas_call(
        body, name=name, grid=(sw // LANES,), in_specs=[tokens if to_groups else groups],
        out_specs=groups if to_groups else tokens,
        out_shape=jax.ShapeDtypeStruct((nk, sw * S5_CHUNK), S5_DTYPE) if to_groups else jax.ShapeDtypeStruct((t, sw), F32),
        compiler_params=_cparams(("parallel",)),
    )(x)


def _s5_fwd(u2, mats):
    mt, ws, wot, fa, fb = mats
    nk = u2.shape[0]
    g = mt.shape[0]
    y_intra = _gmm("s5_intra", u2, mt, 'nn', b_stacked=True)
    s = _gmm("s5_chunk_state", u2, ws, 'nn', b_stacked=True)
    (xin,) = _s5_scan("s5_scan", s.reshape(nk, g, -1), fa, fb, False)
    xin = xin.reshape(nk, -1)
    return _gmm("s5_inter", xin, wot, 'nt', b_stacked=True, add=y_intra, out_dtype=S5_DTYPE), xin


def _s5_bwd(u2, xin, mats, dy2):
    mt, ws, wot, fa, fb = mats
    nk = u2.shape[0]
    g = mt.shape[0]
    dxin = _gmm("s5_dxin", dy2, wot, 'nn', b_stacked=True)
    ds, pa, pb = _s5_scan("s5_scan_adj", dxin.reshape(nk, g, -1), fa, -fb, True, xin=xin.reshape(nk, g, -1))
    ds = ds.reshape(nk, -1)
    du_a = _gmm("s5_du_intra", dy2, mt, 'nt', b_stacked=True)
    du2 = _gmm("s5_du_state", ds, ws, 'nt', b_stacked=True, add=du_a, out_dtype=S5_DTYPE)
    dmt = _gmm("s5_dmt", u2, dy2, 'tn', o_stacked=True)
    dws = _gmm("s5_dws", u2, ds, 'tn', o_stacked=True)
    dwot = _gmm("s5_dwot", dy2, xin, 'tn', o_stacked=True)
    return du2, (dmt, dws, dwot, pa, pb)


def _stacked(g4):
    return g4.reshape((N_CHIPS, -1, g4.shape[-1]))


def _local_step(x, target, w_in4, late, small, reduce_late=None, reduce_mid=None):
    t, d = x.shape
    aw = w_in4.shape[2]
    sw = aw
    nh = aw // HEAD_DIM
    row = lambda v: v.reshape(1, -1)
    g_mix, g_ffn = row(small['g_mix']), row(small['g_ffn'])
    g_oa, g_os, b_glu = row(small['g_out_attn']), row(small['g_out_ssm']), row(small['b_glu'])
    qg2 = jnp.tile(row(small['q_gain']), (1, 2))
    kg2 = jnp.tile(row(small['k_gain']), (1, 2))

    (h,) = _ew("rms_mix", lambda xv, g: _rms(xv, g)[0], [('r', x), ('c', g_mix)], [('r', d, MXU_DTYPE)])
    bias_t = _bias_table(small['rpb'])
    if late[0] == 'halves':
        under_in, under_attn, under_gate = late[1][:2], late[1][2:4], late[1][4:]
        z4, got_in = _mm("in_proj", h, w_in4, contract='nn', b_mode='b', o_mode='b',
                         comm=_GatherChips(under_in), comm_arrs=under_in)
        ya, got_attn = _attn_fwd(z4, qg2, kg2, bias_t, comm=_GatherChips(under_attn), comm_arrs=under_attn)
        w_glu, w_out = (_stacked(g4).reshape(-1, g4.shape[-1]) for g4 in got_in)
        w_gate4, w_up4 = (_stacked(g4) for g4 in got_attn)
    else:
        z4 = _mm("in_proj", h, w_in4, contract='nn', b_mode='b', o_mode='b')
        ya, _ = _attn_fwd(z4, qg2, kg2, bias_t)
        w_glu, w_out, w_gate4, w_up4, w_down4 = late[1]
    ffs = w_gate4.shape[2]
    s5_params = tuple(small[n] for n in ('ssm_a_re', 'ssm_a_im', 'ssm_b_re', 'ssm_b_im', 'ssm_c_re', 'ssm_c_im',
                                         'ssm_log_step', 'ssm_d'))
    mats, mats_vjp = jax.vjp(_s5_mats, *s5_params)
    mats = tuple(m.astype(S5_DTYPE) for m in mats[:3]) + mats[3:]
    u2 = _regroup("s5_group_u", z4[3], True)
    ypre2, xin = _s5_fwd(u2, mats)
    ypre = _regroup("s5_ungroup_y", ypre2, False)
    (yb,) = _ew("gelu", _gelu, [('r', ypre)], [('r', sw, MXU_DTYPE)])
    a_glu = _mm("glu_proj", yb, w_glu, contract='nn')

    def mix_out(yav, ypv, av, bg, goa, gos):
        ys = _gelu(ypv) * _sigmoid(av + bg)
        return jnp.concatenate([_rms(yav, goa)[0], _rms(ys, gos)[0]], axis=1)
    (ycat,) = _ew("mix_out", mix_out, [('r', ya), ('r', ypre), ('r', a_glu), ('c', b_glu), ('c', g_oa), ('c', g_os)],
                  [('r', aw + sw, MXU_DTYPE)])
    x1 = _mm("out_proj", ycat, w_out, contract='nn', add=x)
    (h2,) = _ew("rms_ffn", lambda xv, g: _rms(xv, g)[0], [('r', x1), ('c', g_ffn)], [('r', d, MXU_DTYPE)])
    if late[0] == 'halves':
        gate4, got_gate = _mm("ffn_gate", h2, w_gate4, contract='nn', b_mode='b', o_mode='b', tn=ffs,
                              out_dtype=MXU_DTYPE, comm=_GatherChips(under_gate), comm_arrs=under_gate)
        w_down4 = _stacked(got_gate[0])
    else:
        gate4 = _mm("ffn_gate", h2, w_gate4, contract='nn', b_mode='b', o_mode='b', tn=ffs, out_dtype=MXU_DTYPE)
    up4 = _mm("ffn_up", h2, w_up4, contract='nn', b_mode='b', o_mode='b', tn=ffs, out_dtype=MXU_DTYPE)
    gate_f, up_f = gate4.reshape(4 * t, ffs), up4.reshape(4 * t, ffs)
    (act,) = _ew("swiglu", lambda gv, uv: gv * _sigmoid(gv) * uv, [('r', gate_f), ('r', up_f)],
                 [('r', ffs, MXU_DTYPE)])
    act4 = act.reshape(4, t, ffs)
    x2 = _mm("ffn_down", act4, w_down4, contract='nn', a_mode='c', b_mode='c', add=x1, tk=ffs)

    def loss_fn(xv, tv):
        diff = xv - tv
        return diff * (1.0 / d), diff * (1.0 / d), diff * diff
    dx2, dx2_b, sq = _ew("loss", loss_fn, [('r', x2), ('r', target)], [('r', d, F32), ('r', d, MXU_DTYPE), ('a', d)])

    dact4 = _mm("ffn_down_dx", dx2_b, w_down4, contract='nt', b_mode='b', o_mode='b', tn=ffs, out_dtype=MXU_DTYPE)
    d_w_down4 = _mm("ffn_down_dw", act4, dx2_b, contract='tn', a_mode='b', o_mode='b', tm=ffs, out_dtype=DW_DTYPE)

    def swiglu_bwd(dav, gv, uv):
        s = _sigmoid(gv)
        return dav * uv * s * (1.0 + gv * (1.0 - s)), dav * gv * s
    dgate, dup = _ew("swiglu_bwd", swiglu_bwd, [('r', dact4.reshape(4 * t, ffs)), ('r', gate_f), ('r', up_f)],
                     [('r', ffs, MXU_DTYPE), ('r', ffs, MXU_DTYPE)])
    dgate4, dup4 = dgate.reshape(4, t, ffs), dup.reshape(4, t, ffs)
    dh2 = _mm("ffn_gate_dx", dgate4, w_gate4, contract='nt', a_mode='c', b_mode='c', tk=ffs)
    dh2 = _mm("ffn_up_dx", dup4, w_up4, contract='nt', a_mode='c', b_mode='c', add=dh2, tk=ffs)
    d_w_gate4 = _mm("ffn_gate_dw", h2, dgate4, contract='tn', b_mode='b', o_mode='b', tn=ffs, out_dtype=DW_DTYPE)
    d_w_up4 = _mm("ffn_up_dw", h2, dup4, contract='tn', b_mode='b', o_mode='b', tn=ffs, out_dtype=DW_DTYPE)

    def rms_res_bwd(xv, g, dyv, resv):
        dx, dg = _rms_bwd(xv, g, dyv)
        return resv + dx, dg
    dx1, d_g_ffn = _ew("rms_ffn_bwd", rms_res_bwd, [('r', x1), ('c', g_ffn), ('r', dh2), ('r', dx2)],
                       [('r', d, F32), ('a', d)])

    dycat = _mm("out_proj_dx", dx1, w_out, contract='nt', out_dtype=MXU_DTYPE)
    d_w_out = _mm("out_proj_dw", ycat, dx1, contract='tn', out_dtype=DW_DTYPE)

    def mix_out_bwd(yav, ypv, av, bg, goa, gos, dca, dcs):
        dya, dgoa = _rms_bwd(yav, goa, dca)
        y = _gelu(ypv)
        s = _sigmoid(av + bg)
        dys, dgos = _rms_bwd(y * s, gos, dcs)
        da = dys * y * s * (1.0 - s)
        return dya, da, dys * s, dgoa, dgos, da
    dya, da, dy_direct, d_g_oa, d_g_os, d_b_glu = _ew(
        "mix_out_bwd", mix_out_bwd,
        [('r', ya), ('r', ypre), ('r', a_glu), ('c', b_glu), ('c', g_oa), ('c', g_os),
         ('r', dycat, 0, aw), ('r', dycat, 1, sw)],
        [('r', aw, F32), ('r', sw, MXU_DTYPE), ('r', sw, F32), ('a', aw), ('a', sw), ('a', sw)])
    dy = _mm("glu_proj_dx", da, w_glu, contract='nt', add=dy_direct)
    d_w_glu = _mm("glu_proj_dw", yb, da, contract='tn', out_dtype=DW_DTYPE)
    (dypre,) = _ew("gelu_bwd", lambda dyv, ypv: dyv * _gelu_grad(ypv), [('r', dy), ('r', ypre)],
                   [('r', sw, F32)])

    du2, dmats = _s5_bwd(u2, xin, mats, _regroup("s5_group_dy", dypre, True))
    d_s5 = mats_vjp(dmats)
    du = _regroup("s5_ungroup_du", du2, False)
    d_late = (d_w_glu, d_w_out, d_w_gate4, d_w_up4, d_w_down4)
    if reduce_late is not None:
        sums = reduce_late(d_late)
        (dz4, dbias_t, dqg, dkg), scattered = _attn_bwd(z4, qg2, kg2, bias_t, dya, comm=_ScatterChips(sums),
                                                       comm_arrs=sums)
        d_late = (sums, list(scattered))
    else:
        (dz4, dbias_t, dqg, dkg), _ = _attn_bwd(z4, qg2, kg2, bias_t, dya)
    d_rpb = _bias_table_grad(dbias_t)
    fold = lambda v: v.reshape(-1, 2, HEAD_DIM).sum(axis=(0, 1))
    dz4 = dz4.at[3].set(du)

    d_w_in4 = _mm("in_proj_dw", h, dz4, contract='tn', b_mode='b', o_mode='b', out_dtype=DW_DTYPE)
    d_mid = [d_w_in4] + [d_s5[i].reshape(-1, LANES) for i in (2, 3, 4, 5)]
    if reduce_late is not None:
        sums = reduce_mid(d_mid)
        dh, scattered = _mm("in_proj_dx", dz4, w_in4, contract='nt', a_mode='c', b_mode='c',
                            comm=_ScatterChips(sums), comm_arrs=sums)
        d_mid = (sums, list(scattered))
    else:
        dh = _mm("in_proj_dx", dz4, w_in4, contract='nt', a_mode='c', b_mode='c')
    dx, d_g_mix = _ew("rms_mix_bwd", rms_res_bwd, [('r', x), ('c', g_mix), ('r', dh), ('r', dx1)],
                      [('r', d, F32), ('a', d)])

    colsum = lambda v: v.sum(axis=0)
    d_small = {
        'g_mix': colsum(d_g_mix), 'q_gain': fold(dqg), 'k_gain': fold(dkg), 'rpb': d_rpb,
        'ssm_a_re': d_s5[0], 'ssm_a_im': d_s5[1], 'ssm_b_re': d_s5[2], 'ssm_b_im': d_s5[3],
        'ssm_c_re': d_s5[4], 'ssm_c_im': d_s5[5], 'ssm_log_step': d_s5[6], 'ssm_d': d_s5[7],
        'b_glu': colsum(d_b_glu), 'g_out_attn': colsum(d_g_oa), 'g_out_ssm': colsum(d_g_os), 'g_ffn': colsum(d_g_ffn),
    }
    return jnp.sum(sq), dx, d_late, d_mid, d_small


ANY = pl.BlockSpec(memory_space=pl.ANY)


def _place():
    x, y, c = lax.axis_index("x"), lax.axis_index("y"), lax.axis_index("c")
    other_chips = [(1 - x, y), (x, 1 - y), (1 - x, 1 - y)]
    return x, y, c, 2 * x + y, (x, y, 1 - c), other_chips


class _GatherChips:
    KINDS = 7

    def __init__(self, arrs):
        self.n = len(arrs)
        self.out_shape = [jax.ShapeDtypeStruct((N_CHIPS,) + a.shape, a.dtype) for a in arrs]
        self.scratch = [pltpu.SemaphoreType.DMA((self.n, self.KINDS)), pltpu.SemaphoreType.DMA((self.n, self.KINDS))]

    def _copies(self, ins, outs, send_sems, recv_sems):
        x, y, c, me, sibling, chips = _place()

        def remote(a, k, src, dst, to):
            return lambda: pltpu.make_async_remote_copy(src_ref=src, dst_ref=dst, send_sem=send_sems.at[a, k],
                                                        recv_sem=recv_sems.at[a, k], device_id=to, device_id_type=MESH)
        own, out, landed, passed, theirs = [], [], [], [], []
        for a in range(self.n):
            own.append(remote(a, 6, ins[a], outs[a].at[me], sibling))
            for j, (px, py) in enumerate(chips):
                there, here = outs[a].at[2 * px + py, c], outs[a].at[2 * px + py, 1 - c]
                out.append(remote(a, j, ins[a].at[c], outs[a].at[me, c], (px, py, c)))
                landed.append(remote(a, j, there, there, (px, py, c)))
                passed.append(remote(a, 3 + j, there, there, sibling))
                theirs.append(remote(a, 3 + j, here, here, sibling))
        return own, out, landed, passed, theirs

    def start(self, ins, outs, send_sems, recv_sems):
        own, out, _, _, _ = self._copies(ins, outs, send_sems, recv_sems)
        for make in own + out:
            make().start()

    def finish(self, ins, outs, send_sems, recv_sems):
        own, out, landed, passed, theirs = self._copies(ins, outs, send_sems, recv_sems)
        for arrived, onward in zip(landed, passed):
            arrived().wait_recv()
            onward().start()
        for make in theirs + own:
            make().wait_recv()
        for make in own + out + passed:
            make().wait_send()


class _ScatterChips:
    def __init__(self, sums):
        self.n = len(sums)
        self.out_shape = [jax.ShapeDtypeStruct(s.shape, s.dtype) for s in sums]
        self.scratch = [pltpu.SemaphoreType.DMA((self.n, 3)), pltpu.SemaphoreType.DMA((self.n, 3))]

    def _copies(self, ins, outs, send_sems, recv_sems):
        x, y, c, me, sibling, chips = _place()
        out, landed = [], []

        def remote(a, j, src, dst, to):
            return lambda: pltpu.make_async_remote_copy(src_ref=src, dst_ref=dst, send_sem=send_sems.at[a, j],
                                                        recv_sem=recv_sems.at[a, j], device_id=to, device_id_type=MESH)
        for a in range(self.n):
            for j, (px, py) in enumerate(chips):
                slot = outs[a].at[2 * px + py]
                out.append(remote(a, j, ins[a].at[2 * px + py], outs[a].at[me], (px, py, c)))
                landed.append(remote(a, j, slot, slot, (px, py, c)))
        return out, landed

    def start(self, ins, outs, send_sems, recv_sems):
        for make in self._copies(ins, outs, send_sems, recv_sems)[0]:
            make().start()

    def finish(self, ins, outs, send_sems, recv_sems):
        out, landed = self._copies(ins, outs, send_sems, recv_sems)
        for make in landed:
            make().wait_recv()
        for make in out:
            make().wait_send()


def _comm_call(name, comm, arrs):
    n = comm.n

    def body(*refs):
        parts = (refs[:n], refs[n:2 * n]) + tuple(refs[2 * n:])
        comm.start(*parts)
        comm.finish(*parts)

    return pl.pallas_call(body, name=name, in_specs=[ANY] * n, out_specs=[ANY] * n, out_shape=comm.out_shape,
                          scratch_shapes=comm.scratch)(*arrs)


def _gather_chips(name, arrs):
    return _comm_call(name, _GatherChips(arrs), arrs)


def _swap_halves(name, parts):
    n = len(parts)

    def body(*refs):
        ins, outs = refs[:n], refs[n:2 * n]
        send_sems, recv_sems = refs[2 * n:]
        x, y, c, me, sibling, chips = _place()
        cps = []
        for a in range(n):
            cp = pltpu.make_async_remote_copy(src_ref=ins[a].at[:, 1 - c], dst_ref=outs[a], send_sem=send_sems.at[a],
                                              recv_sem=recv_sems.at[a], device_id=sibling, device_id_type=MESH)
            cp.start()
            cps.append(cp)
        for cp in cps:
            cp.wait()

    return pl.pallas_call(
        body, name=name, in_specs=[ANY] * n, out_specs=[ANY] * n,
        out_shape=[jax.ShapeDtypeStruct((N_CHIPS,) + p.shape[2:], p.dtype) for p in parts],
        scratch_shapes=[pltpu.SemaphoreType.DMA((n,)), pltpu.SemaphoreType.DMA((n,))],
    )(*parts)


def _scatter_chips(name, sums):
    return _comm_call(name, _ScatterChips(sums), sums)


def _swap_reduced(name, halves):
    n = len(halves)

    def body(*refs):
        ins, outs = refs[:n], refs[n:2 * n]
        send_sems, recv_sems = refs[2 * n:]
        x, y, c, me, sibling, chips = _place()
        cps = []
        for a in range(n):
            cp = pltpu.make_async_remote_copy(src_ref=ins[a], dst_ref=outs[a], send_sem=send_sems.at[a],
                                              recv_sem=recv_sems.at[a], device_id=sibling, device_id_type=MESH)
            cp.start()
            cps.append(cp)
        for cp in cps:
            cp.wait()

    return pl.pallas_call(
        body, name=name, in_specs=[ANY] * n, out_specs=[ANY] * n,
        out_shape=[jax.ShapeDtypeStruct(h.shape, h.dtype) for h in halves],
        scratch_shapes=[pltpu.SemaphoreType.DMA((n,)), pltpu.SemaphoreType.DMA((n,))],
    )(*halves)


def _row_tile(r, want=256):
    t = (min(r, want) // SUBLANES) * SUBLANES
    while r % t:
        t -= SUBLANES
    return t


def _add_own_half(name, part, got, c, out_dtype):
    _, _, r, cols = part.shape
    tr = _row_tile(r)

    def body(c_ref, p_ref, g_ref, o_ref):
        o_ref[...] = (p_ref[...].astype(F32) + g_ref[...].astype(F32)).astype(o_ref.dtype)

    return pl.pallas_call(
        body, name=name,
        grid_spec=pltpu.PrefetchScalarGridSpec(
            num_scalar_prefetch=1, grid=(N_CHIPS, r // tr),
            in_specs=[pl.BlockSpec((None, None, tr, cols), lambda s, i, c_ref: (s, c_ref[0], i, 0)),
                      pl.BlockSpec((None, tr, cols), lambda s, i, c_ref: (s, i, 0))],
            out_specs=pl.BlockSpec((None, tr, cols), lambda s, i, c_ref: (s, i, 0))),
        out_shape=jax.ShapeDtypeStruct(got.shape, out_dtype),
        compiler_params=_cparams(("parallel", "parallel")),
    )(c.reshape(1).astype(jnp.int32), part, got)


def _sum_chips(name, got, own, me):
    _, r, cols = got.shape
    tr = _row_tile(r)

    def body(me_ref, r0, r1, r2, r3, own_ref, o_ref):
        pick = lambda s, ref: jnp.where(me_ref[0] == s, own_ref[...], ref[...]).astype(F32)
        o_ref[...] = ((pick(0, r0) + pick(1, r1)) + pick(2, r2)) + pick(3, r3)

    def slot(s):
        return pl.BlockSpec((None, tr, cols),
                            lambda i, me_ref: (jnp.where(me_ref[0] == s, (s + 1) % N_CHIPS, s), i, 0))

    return pl.pallas_call(
        body, name=name,
        grid_spec=pltpu.PrefetchScalarGridSpec(
            num_scalar_prefetch=1, grid=(r // tr,),
            in_specs=[slot(s) for s in range(N_CHIPS)]
            + [pl.BlockSpec((None, tr, cols), lambda i, me_ref: (me_ref[0], i, 0))],
            out_specs=pl.BlockSpec((tr, cols), lambda i, me_ref: (i, 0))),
        out_shape=jax.ShapeDtypeStruct((r, cols), F32),
        compiler_params=_cparams(("parallel",)),
    )(me.reshape(1).astype(jnp.int32), got, got, got, got, own)


def _adamw_math(wv, gv, mv, vv):
    mv = ADAM_B1 * mv + (1.0 - ADAM_B1) * gv
    vv = ADAM_B2 * vv + (1.0 - ADAM_B2) * (gv * gv)
    m_hat = mv / (1.0 - ADAM_B1 ** ADAM_STEP)
    v_hat = vv / (1.0 - ADAM_B2 ** ADAM_STEP)
    return -ADAM_LR * (m_hat / (jnp.sqrt(v_hat) + ADAM_EPS) + ADAM_WD * wv), mv, vv


def _adamw(name, w, g, m, v):
    cols = w.shape[1]
    return _ew(name, _adamw_math, [('r', w), ('r', g), ('r', m), ('r', v)], [('r', cols, F32)] * 3,
               tr=_row_tile(w.shape[0], 128))


def _adamw_halves(name, w, mine, theirs, m, v, c):
    r, cols = mine.shape
    tr = _row_tile(r, 128)
    nb = r // tr

    def body(c_ref, w_ref, a_ref, b_ref, m_ref, v_ref, g_out, d_out, m_out, v_out):
        g = jnp.where(pl.program_id(0) == c_ref[0], a_ref[...], b_ref[...])
        g_out[...] = g
        d_out[...], m_out[...], v_out[...] = _adamw_math(w_ref[...], g, m_ref[...], v_ref[...])

    whole = pl.BlockSpec((tr, cols), lambda h, i, c_ref: (h * nb + i, 0))
    half = pl.BlockSpec((tr, cols), lambda h, i, c_ref: (i, 0))
    return pl.pallas_call(
        body, name=name,
        grid_spec=pltpu.PrefetchScalarGridSpec(
            num_scalar_prefetch=1, grid=(2, nb),
            in_specs=[whole, half, half, whole, whole], out_specs=[whole] * 4),
        out_shape=[jax.ShapeDtypeStruct(w.shape, F32)] * 4,
        compiler_params=_cparams(("parallel", "parallel")),
    )(c.reshape(1).astype(jnp.int32), w, mine, theirs, m, v)


SMALL_ROWS_ALIGN = 2 * N_CHIPS * SUBLANES


MEDIUM_NAMES = ['ssm_b_re', 'ssm_b_im', 'ssm_c_re', 'ssm_c_im']
PACKED_NAMES = [n for n in SMALL_NAMES if n not in MEDIUM_NAMES]


def _pack_small(d):
    flat = jnp.concatenate([d[n].reshape(-1).astype(F32) for n in PACKED_NAMES])
    rows = -(-flat.shape[0] // (LANES * SMALL_ROWS_ALIGN)) * SMALL_ROWS_ALIGN
    return jnp.pad(flat, (0, rows * LANES - flat.shape[0])).reshape(rows, LANES)


def _unpack_small(packed, like):
    flat = packed.reshape(-1)
    out, off = {}, 0
    for n in PACKED_NAMES:
        size = like[n].size
        out[n] = flat[off:off + size].reshape(like[n].shape)
        off += size
    return out


def kernel(x, g_mix, w_in, q_gain, k_gain, rpb, ssm_a_re, ssm_a_im, ssm_b_re, ssm_b_im, ssm_c_re, ssm_c_im, ssm_log_step, ssm_d, w_glu, b_glu, g_out_attn, g_out_ssm, w_out, g_ffn, w_ffn_gate, w_ffn_up, w_ffn_down, loss_target, m_g_mix, m_w_in, m_q_gain, m_k_gain, m_rpb, m_ssm_a_re, m_ssm_a_im, m_ssm_b_re, m_ssm_b_im, m_ssm_c_re, m_ssm_c_im, m_ssm_log_step, m_ssm_d, m_w_glu, m_b_glu, m_g_out_attn, m_g_out_ssm, m_w_out, m_g_ffn, m_w_ffn_gate, m_w_ffn_up, m_w_ffn_down, v_g_mix, v_w_in, v_q_gain, v_k_gain, v_rpb, v_ssm_a_re, v_ssm_a_im, v_ssm_b_re, v_ssm_b_im, v_ssm_c_re, v_ssm_c_im, v_ssm_log_step, v_ssm_d, v_w_glu, v_b_glu, v_g_out_attn, v_g_out_ssm, v_w_out, v_g_ffn, v_w_ffn_gate, v_w_ffn_up, v_w_ffn_down):
    given = dict(locals())
    w = {n: given[n][0] for n in WEIGHT_NAMES}
    mom = {n: given["m_" + n][0] for n in WEIGHT_NAMES}
    var = {n: given["v_" + n][0] for n in WEIGHT_NAMES}
    d = x.shape[-1]
    c = lax.axis_index("c")

    halves = {n: w[n].astype(MXU_DTYPE).reshape((2, w[n].shape[0] // 2, w[n].shape[1])) for n in BIG_NAMES}
    (w_in4,) = _gather_chips("gather_w_in", [halves['w_in']])
    w_in4 = w_in4.reshape((N_CHIPS, -1, w_in4.shape[-1]))

    def chip_sums(tag, grads, payload):
        parts = [g.reshape((N_CHIPS, 2, -1, g.shape[-1])) for g in grads]
        got = _swap_halves("reduce_swap_halves_" + tag, parts)
        return [_add_own_half("reduce_add_%s_%d" % (tag, a), p, gt, c, dt)
                for a, (p, gt, dt) in enumerate(zip(parts, got, payload))]

    reduce_late = lambda grads: chip_sums("late", grads, [GRAD_PAYLOAD_DTYPE] * len(grads))
    reduce_mid = lambda grads: chip_sums("mid", grads, [GRAD_PAYLOAD_DTYPE] + [F32] * (len(grads) - 1))
    sq, dx, (sums_late, got_late), (sums_mid, got_mid), d_small = _local_step(
        x[0], loss_target[0], w_in4, ('halves', [halves[n] for n in LATE_NAMES]), {n: w[n] for n in SMALL_NAMES},
        reduce_late, reduce_mid)
    loss = lax.psum(0.5 * sq / d, ("x", "y", "c"))

    nbig = len(BIG_NAMES)
    sums_tiny = chip_sums("tiny", [_pack_small(d_small)], [F32])
    got_tiny = list(_scatter_chips("reduce_scatter_tiny", sums_tiny))
    sums = sums_mid[:1] + sums_late + sums_mid[1:] + sums_tiny
    got = got_mid[:1] + got_late + got_mid[1:] + got_tiny
    me = 2 * lax.axis_index("x") + lax.axis_index("y")
    mine = [_sum_chips("reduce_sum_%d" % a, gt, sm_, me) for a, (gt, sm_) in enumerate(zip(got, sums))]
    theirs = _swap_reduced("reduce_swap_reduced", mine)
    in_order = lambda a: jnp.where(c == 0, jnp.stack([mine[a], theirs[a]]), jnp.stack([theirs[a], mine[a]]))
    repl = _gather_chips("gather_small", [in_order(a) for a in range(nbig, len(mine))])
    repl = [r.reshape(-1, LANES) for r in repl]
    like = {n: w[n] for n in SMALL_NAMES}
    grad_small = _unpack_small(repl[-1], like)
    grad_small.update({n: r.reshape(w[n].shape) for n, r in zip(MEDIUM_NAMES, repl)})

    grad_big, delta, new_m, new_v = {}, {}, {}, {}
    for a, n in enumerate(BIG_NAMES):
        grad_big[n], delta[n], new_m[n], new_v[n] = _adamw_halves("adamw_%d" % a, w[n], mine[a], theirs[a],
                                                                  mom[n], var[n], c)
    for n, r in zip(MEDIUM_NAMES, repl):
        res = _adamw("adamw_" + n, w[n].reshape(-1, LANES), r, mom[n].reshape(-1, LANES), var[n].reshape(-1, LANES))
        delta[n], new_m[n], new_v[n] = (t.reshape(w[n].shape) for t in res)
    sd, sm, sv = _adamw("adamw_small", _pack_small(w), repl[-1], _pack_small(mom), _pack_small(var))
    delta.update(_unpack_small(sd, like))
    new_m.update(_unpack_small(sm, like))
    new_v.update(_unpack_small(sv, like))
    grads = {**grad_big, **grad_small}
    lead = lambda t: t[None]
    return (loss, dx[None], *[lead(grads[n]) for n in WEIGHT_NAMES], *[lead(delta[n]) for n in WEIGHT_NAMES],
            *[lead(new_m[n]) for n in WEIGHT_NAMES], *[lead(new_v[n]) for n in WEIGHT_NAMES])
```

```python
import functools
import math

import jax
import jax.numpy as jnp
from jax import lax
from jax.experimental import pallas as pl
from jax.experimental.pallas import tpu as pltpu

F32 = jnp.float32
BF16 = jnp.bfloat16
MXU_DTYPE = BF16
GRAD_PAYLOAD_DTYPE = BF16
DW_DTYPE = BF16
S5_DTYPE = BF16
HI = lax.Precision.HIGHEST
VMEM_LIMIT_V7X = 56 * 1024 * 1024
LANES = 128
SUBLANES = 8

GRID_W = 64
WIN_H = 8
WIN_W = 16
HEAD_DIM = 64
SSM_GROUP_CH = 16
SSM_STATE = 64
S5_CHUNK = 16
S5_GROUPS_PER_STEP = 8
RMS_EPS = 1e-6
NEG_INF = -1e30
N_CHIPS = 4
MESH = pl.DeviceIdType.MESH

ADAM_LR = 0.001
ADAM_B1 = 0.9
ADAM_B2 = 0.999
ADAM_EPS = 1e-08
ADAM_WD = 0.01
ADAM_STEP = 10

WEIGHT_NAMES = ['g_mix', 'w_in', 'q_gain', 'k_gain', 'rpb', 'ssm_a_re', 'ssm_a_im', 'ssm_b_re', 'ssm_b_im',
                'ssm_c_re', 'ssm_c_im', 'ssm_log_step', 'ssm_d', 'w_glu', 'b_glu', 'g_out_attn', 'g_out_ssm',
                'w_out', 'g_ffn', 'w_ffn_gate', 'w_ffn_up', 'w_ffn_down']
BIG_NAMES = ['w_in', 'w_glu', 'w_out', 'w_ffn_gate', 'w_ffn_up', 'w_ffn_down']
LATE_NAMES = BIG_NAMES[1:]
SMALL_NAMES = [n for n in WEIGHT_NAMES if n not in BIG_NAMES]


def _cparams(sem):
    return pltpu.CompilerParams(dimension_semantics=sem, vmem_limit_bytes=VMEM_LIMIT_V7X)


def _tile(n, want):
    if n <= want:
        return n
    t = (want // LANES) * LANES
    while t >= LANES:
        if n % t == 0:
            return t
        t -= LANES
    return n


def _mm(name, a, b, *, contract, a_mode='2', b_mode='2', o_mode='2', out_dtype=F32, add=None, exact=False,
        tm=1024, tn=1024, tk=2048, comm=None, comm_arrs=()):
    dn = {'nn': (((1,), (0,)), ((), ())), 'nt': (((1,), (1,)), ((), ())), 'tn': (((0,), (0,)), ((), ()))}[contract]
    ar, ac = a.shape[-2:]
    br, bc = b.shape[-2:]
    m, kdim = (ar, ac) if contract != 'tn' else (ac, ar)
    n = bc if contract != 'nt' else br
    assert kdim == (br if contract != 'nt' else bc), (name, a.shape, b.shape)
    nbatch = 1
    for arr, mode in ((a, a_mode), (b, b_mode)):
        if mode == 'b':
            nbatch = arr.shape[0]
    nstack = 1
    for arr, mode in ((a, a_mode), (b, b_mode)):
        if mode == 'c':
            nstack = arr.shape[0]
    tm, tn, tk = _tile(m, tm), _tile(n, tn), _tile(kdim, tk)
    nkin = kdim // tk
    nk = nstack * nkin
    grid = (nbatch, m // tm, n // tn, nk)

    def spec(mode, block, rc):
        def imap(s, i, j, kk):
            r, c = rc(i, j, kk % nkin)
            if mode == '2':
                return (r, c)
            return (s if mode == 'b' else kk // nkin, r, c)
        return pl.BlockSpec(block if mode == '2' else (None,) + block, imap)

    a_spec = spec(a_mode, (tm, tk) if contract != 'tn' else (tk, tm),
                  (lambda i, j, k: (i, k)) if contract != 'tn' else (lambda i, j, k: (k, i)))
    b_spec = spec(b_mode, (tk, tn) if contract != 'nt' else (tn, tk),
                  (lambda i, j, k: (k, j)) if contract != 'nt' else (lambda i, j, k: (j, k)))
    o_spec = spec(o_mode, (tm, tn), lambda i, j, k: (i, j))
    out_shape = (m, n) if o_mode == '2' else (nbatch, m, n)
    has_add = add is not None

    def product(a_ref, b_ref):
        if exact:
            return lax.dot_general(a_ref[...].astype(F32), b_ref[...].astype(F32), dn, precision=HI,
                                   preferred_element_type=F32)
        return lax.dot_general(a_ref[...].astype(MXU_DTYPE), b_ref[...].astype(MXU_DTYPE), dn,
                               preferred_element_type=F32)

    ncomm = len(comm_arrs)
    nacc = int(nk > 1)

    def body(*refs):
        a_ref, b_ref = refs[:2]
        add_ref = refs[2] if has_add else None
        c_ins = refs[2 + has_add:2 + has_add + ncomm]
        o_ref = refs[2 + has_add + ncomm]
        c_outs = refs[3 + has_add + ncomm:3 + has_add + 2 * ncomm]
        sems = refs[3 + has_add + 2 * ncomm + nacc:]
        ids = [pl.program_id(ax) for ax in range(4)]
        if comm is not None:
            @pl.when((ids[0] == 0) & (ids[1] == 0) & (ids[2] == 0) & (ids[3] == 0))
            def _():
                comm.start(c_ins, c_outs, *sems)

        def write(r):
            if has_add:
                r = r + add_ref[...].astype(F32)
            o_ref[...] = r.astype(o_ref.dtype)

        if nk == 1:
            write(product(a_ref, b_ref))
        else:
            acc_ref = refs[3 + has_add + 2 * ncomm]

            @pl.when(ids[3] == 0)
            def _():
                acc_ref[...] = jnp.zeros_like(acc_ref)

            acc_ref[...] += product(a_ref, b_ref)

            @pl.when(ids[3] == nk - 1)
            def _():
                write(acc_ref[...])

        if comm is not None:
            @pl.when((ids[0] == grid[0] - 1) & (ids[1] == grid[1] - 1) & (ids[2] == grid[2] - 1) & (ids[3] == nk - 1))
            def _():
                comm.finish(c_ins, c_outs, *sems)

    in_specs = [a_spec, b_spec] + ([o_spec] if has_add else []) + [ANY] * ncomm
    args = (a, b) + ((add,) if has_add else ()) + tuple(comm_arrs)
    res = pl.pallas_call(
        body, name=name, grid=grid, in_specs=in_specs, out_specs=[o_spec] + [ANY] * ncomm,
        out_shape=[jax.ShapeDtypeStruct(out_shape, out_dtype)] + (comm.out_shape if comm is not None else []),
        scratch_shapes=([pltpu.VMEM((tm, tn), F32)] if nk > 1 else []) + (comm.scratch if comm is not None else []),
        compiler_params=_cparams(("parallel", "parallel", "parallel", "arbitrary") if comm is None
                                 else ("arbitrary",) * 4),
    )(*args)
    return res[0] if comm is None else (res[0], res[1:])


def _ew(name, fn, ins, outs, tr=256):
    rows = next(x[1].shape[0] for x in ins if x[0] == 'r')
    tr = min(tr, rows)
    assert rows % tr == 0 and tr % SUBLANES == 0, (name, rows, tr)
    in_specs, args = [], []
    for x in ins:
        if x[0] == 'r' and len(x) == 2:
            in_specs.append(pl.BlockSpec((tr, x[1].shape[1]), lambda i: (i, 0)))
        elif x[0] == 'r':
            in_specs.append(pl.BlockSpec((tr, x[3]), functools.partial(lambda cb, i: (i, cb), x[2])))
        else:
            in_specs.append(pl.BlockSpec(x[1].shape, lambda i: (0, 0)))
        args.append(x[1])
    out_specs, out_shapes = [], []
    for o in outs:
        if o[0] == 'r':
            out_specs.append(pl.BlockSpec((tr, o[1]), lambda i: (i, 0)))
            out_shapes.append(jax.ShapeDtypeStruct((rows, o[1]), o[2]))
        else:
            out_specs.append(pl.BlockSpec((SUBLANES, o[1]), lambda i: (0, 0)))
            out_shapes.append(jax.ShapeDtypeStruct((SUBLANES, o[1]), F32))
    nin = len(ins)
    has_acc = any(o[0] == 'a' for o in outs)

    def body(*refs):
        vals = fn(*[r[...].astype(F32) for r in refs[:nin]])
        if not isinstance(vals, (tuple, list)):
            vals = (vals,)
        i = pl.program_id(0)
        for o, ref, v in zip(outs, refs[nin:], vals):
            if o[0] == 'r':
                ref[...] = v.astype(ref.dtype)
            else:
                part = v.astype(F32).reshape(tr // SUBLANES, SUBLANES, o[1]).sum(axis=0)

                @pl.when(i == 0)
                def _(ref=ref, part=part):
                    ref[...] = part

                @pl.when(i > 0)
                def _(ref=ref, part=part):
                    ref[...] += part

    res = pl.pallas_call(
        body, name=name, grid=(rows // tr,), in_specs=in_specs, out_specs=out_specs, out_shape=out_shapes,
        compiler_params=_cparams(("arbitrary",) if has_acc else ("parallel",)),
    )(*args)
    return res


def _rms(x, g):
    r = lax.rsqrt(jnp.mean(x * x, axis=-1, keepdims=True) + RMS_EPS)
    xr = x * r
    return xr * g, xr


def _rms_bwd(x, g, dy):
    r = lax.rsqrt(jnp.mean(x * x, axis=-1, keepdims=True) + RMS_EPS)
    xr = x * r
    gdy = g * dy
    dx = r * (gdy - xr * jnp.mean(xr * gdy, axis=-1, keepdims=True))
    return dx, dy * xr


def _sigmoid(x):
    return 0.5 * (jnp.tanh(0.5 * x) + 1.0)


_GELU_C = math.sqrt(2.0 / math.pi)


def _gelu(x):
    return 0.5 * x * (1.0 + jnp.tanh(_GELU_C * (x + 0.044715 * x * x * x)))


def _gelu_grad(x):
    t = jnp.tanh(_GELU_C * (x + 0.044715 * x * x * x))
    return 0.5 * (1.0 + t) + 0.5 * x * (1.0 - t * t) * _GELU_C * (1.0 + 3 * 0.044715 * x * x)


ATTN_ROWS_PER_STEP = 8
NT_DIMS = (((1,), (1,)), ((), ()))
NN_DIMS = (((1,), (0,)), ((), ()))
TN_DIMS = (((0,), (0,)), ((), ()))


def _attn_geometry(r, rows):
    row_start = jnp.clip(r - WIN_H // 2, 0, rows - WIN_H)
    key0 = pl.multiple_of(row_start * GRID_W, GRID_W)
    bias0 = pl.multiple_of((row_start - r + (WIN_H - 1)) * GRID_W, GRID_W)
    return key0, bias0


def _window_onehot():
    c = jnp.arange(GRID_W)
    col_start = jnp.clip(c - WIN_W // 2, 0, GRID_W - WIN_W)
    col_in = (c[None, :] >= col_start[:, None]) & (c[None, :] < col_start[:, None] + WIN_W)
    dc = jnp.clip(c[None, :] - c[:, None], -(WIN_W - 1), WIN_W - 1) + (WIN_W - 1)
    onehot = ((dc[:, :, None] == jnp.arange(2 * WIN_W - 1)[None, None, :]) & col_in[:, :, None]).astype(F32)
    return onehot, col_in


def _bias_table(rpb):
    onehot, col_in = _window_onehot()
    nh = rpb.shape[0]
    tab = jnp.einsum('perd,qkd->prkeq', rpb.reshape(nh // 2, 2, 2 * WIN_H - 1, 2 * WIN_W - 1), onehot, precision=HI)
    tab = tab + jnp.where(col_in, 0.0, NEG_INF).T[None, None, :, None, :]
    return tab.reshape(nh // 2, (2 * WIN_H - 1) * GRID_W, 2 * GRID_W)


def _bias_table_grad(dtab):
    onehot, _ = _window_onehot()
    npair = dtab.shape[0]
    d = dtab.reshape(npair, 2 * WIN_H - 1, GRID_W, 2, GRID_W)
    return jnp.einsum('prkeq,qkd->perd', d, onehot, precision=HI).reshape(2 * npair, 2 * WIN_H - 1, 2 * WIN_W - 1)


def _lane_lo(shape):
    return lax.broadcasted_iota(jnp.int32, shape, 1) < HEAD_DIM


def _half_sums(v):
    lo = _lane_lo(v.shape)
    s_lo = jnp.sum(jnp.where(lo, v, 0.0), axis=1, keepdims=True)
    s_hi = jnp.sum(jnp.where(lo, 0.0, v), axis=1, keepdims=True)
    return jnp.where(lo, s_lo, s_hi)


def _rms_pair(x, g):
    r = lax.rsqrt(_half_sums(x * x) * (1.0 / HEAD_DIM) + RMS_EPS)
    return x * r * g


def _rms_pair_bwd(x, g, dy):
    r = lax.rsqrt(_half_sums(x * x) * (1.0 / HEAD_DIM) + RMS_EPS)
    xr = x * r
    gdy = g * dy
    dx = r * (gdy - xr * (_half_sums(xr * gdy) * (1.0 / HEAD_DIM)))
    return dx, dy * xr


def _blockdiag(a):
    a2 = jnp.concatenate([a, a], axis=0)
    row_hi = lax.broadcasted_iota(jnp.int32, a2.shape, 0) >= GRID_W
    lane_hi = lax.broadcasted_iota(jnp.int32, a2.shape, 1) >= HEAD_DIM
    return jnp.where(row_hi == lane_hi, a2, 0.0).astype(MXU_DTYPE)


def _diag_blocks(m):
    return jnp.where(_lane_lo((GRID_W, 2 * HEAD_DIM)), m[:GRID_W], m[GRID_W:])


def _attn_scores(qb, kb, bias):
    st = lax.dot_general(kb, qb, NT_DIMS, preferred_element_type=F32)
    st = st * (1.0 / math.sqrt(HEAD_DIM)) + bias
    mx = jnp.max(st, axis=0, keepdims=True)
    p = jnp.exp(st - mx)
    return p * (1.0 / jnp.sum(p, axis=0, keepdims=True))


def _attn_fwd(z4, qg2, kg2, bias_t, comm=None, comm_arrs=()):
    _, t, aw = z4.shape
    rows = t // GRID_W
    npair = aw // (2 * HEAD_DIM)
    nkeys = WIN_H * GRID_W
    nb = bias_t.shape[1]
    rps = min(ATTN_ROWS_PER_STEP, rows)
    blk = rps * GRID_W
    nsteps = rows // rps
    ncomm = len(comm_arrs)

    def body(*refs):
        q_ref, k_ref, v_ref, qg_ref, kg_ref, b_ref = refs[:6]
        c_ins, o_ref, c_outs = refs[6:6 + ncomm], refs[6 + ncomm], refs[7 + ncomm:7 + 2 * ncomm]
        kn_ref, vb_ref = refs[7 + 2 * ncomm:9 + 2 * ncomm]
        sems = refs[9 + 2 * ncomm:]
        pr, rb = pl.program_id(0), pl.program_id(1)
        if comm is not None:
            @pl.when((pr == 0) & (rb == 0))
            def _():
                comm.start(c_ins, c_outs, *sems)

        @pl.when(rb == 0)
        def _():
            kn_ref[...] = _rms_pair(k_ref[...], kg_ref[...]).astype(MXU_DTYPE)
            vb_ref[...] = v_ref[...].astype(MXU_DTYPE)

        def row(i, carry):
            key0, bias0 = _attn_geometry(rb * rps + i, rows)
            at = pl.ds(pl.multiple_of(i * GRID_W, GRID_W), GRID_W)
            qb = _blockdiag(_rms_pair(q_ref[at, :], qg_ref[...]))
            pt = _attn_scores(qb, kn_ref[pl.ds(key0, nkeys), :], b_ref[pl.ds(bias0, nkeys), :])
            both = lax.dot_general(pt.astype(MXU_DTYPE), vb_ref[pl.ds(key0, nkeys), :], TN_DIMS,
                                   preferred_element_type=F32)
            o_ref[at, :] = _diag_blocks(both)
            return carry

        lax.fori_loop(0, rps, row, 0, unroll=4)
        if comm is not None:
            @pl.when((pr == npair - 1) & (rb == nsteps - 1))
            def _():
                comm.finish(c_ins, c_outs, *sems)

    pair_cols = lambda lead: pl.BlockSpec((None, t, 2 * HEAD_DIM), lambda p, r: (lead, 0, p))
    res = pl.pallas_call(
        body, name="attn_fwd", grid=(npair, nsteps),
        in_specs=[pl.BlockSpec((None, blk, 2 * HEAD_DIM), lambda p, r: (0, r, p)), pair_cols(1), pair_cols(2),
                  pl.BlockSpec((1, 2 * HEAD_DIM), lambda p, r: (0, 0)),
                  pl.BlockSpec((1, 2 * HEAD_DIM), lambda p, r: (0, 0)),
                  pl.BlockSpec((None, nb, 2 * GRID_W), lambda p, r: (p, 0, 0))] + [ANY] * ncomm,
        out_specs=[pl.BlockSpec((blk, 2 * HEAD_DIM), lambda p, r: (r, p))] + [ANY] * ncomm,
        out_shape=[jax.ShapeDtypeStruct((t, aw), F32)] + (comm.out_shape if comm is not None else []),
        scratch_shapes=[pltpu.VMEM((t, 2 * HEAD_DIM), MXU_DTYPE), pltpu.VMEM((t, 2 * HEAD_DIM), MXU_DTYPE)]
        + (comm.scratch if comm is not None else []),
        compiler_params=_cparams(("arbitrary", "arbitrary")),
    )(z4, z4, z4, qg2, kg2, bias_t, *comm_arrs)
    return res[0], res[1:]


def _attn_bwd(z4, qg2, kg2, bias_t, dya, comm=None, comm_arrs=()):
    _, t, aw = z4.shape
    rows = t // GRID_W
    npair = aw // (2 * HEAD_DIM)
    nkeys = WIN_H * GRID_W
    nb = bias_t.shape[1]
    rps = min(ATTN_ROWS_PER_STEP, rows)
    blk = rps * GRID_W
    nsteps = rows // rps
    scale = 1.0 / math.sqrt(HEAD_DIM)
    ncomm = len(comm_arrs)

    def body(*refs):
        q_ref, k_ref, v_ref, qg_ref, kg_ref, b_ref, do_ref = refs[:7]
        c_ins = refs[7:7 + ncomm]
        dz_ref, db_ref, dqg_ref, dkg_ref = refs[7 + ncomm:11 + ncomm]
        c_outs = refs[11 + ncomm:11 + 2 * ncomm]
        kn_ref, vb_ref, dkn_ref, dv_ref = refs[11 + 2 * ncomm:15 + 2 * ncomm]
        sems = refs[15 + 2 * ncomm:]
        pr, rb = pl.program_id(0), pl.program_id(1)
        if comm is not None:
            @pl.when((pr == 0) & (rb == 0))
            def _():
                comm.start(c_ins, c_outs, *sems)

        @pl.when(rb == 0)
        def _():
            kn_ref[...] = _rms_pair(k_ref[...], kg_ref[...]).astype(MXU_DTYPE)
            vb_ref[...] = v_ref[...].astype(MXU_DTYPE)
            dkn_ref[...] = jnp.zeros_like(dkn_ref)
            dv_ref[...] = jnp.zeros_like(dv_ref)
            db_ref[...] = jnp.zeros_like(db_ref)
            dqg_ref[...] = jnp.zeros_like(dqg_ref)

        def row(i, dqg_sum):
            r = rb * rps + i
            key0, bias0 = _attn_geometry(r, rows)
            keys = pl.ds(key0, nkeys)
            at = pl.ds(pl.multiple_of(i * GRID_W, GRID_W), GRID_W)
            q = q_ref[at, :]
            qb = _blockdiag(_rms_pair(q, qg_ref[...]))
            dob = _blockdiag(do_ref[at, :])
            kb = kn_ref[keys, :]
            pt = _attn_scores(qb, kb, b_ref[pl.ds(bias0, nkeys), :])
            dv_ref[keys, :] += lax.dot_general(pt.astype(MXU_DTYPE), dob, NN_DIMS, preferred_element_type=F32)
            dpt = lax.dot_general(vb_ref[keys, :], dob, NT_DIMS, preferred_element_type=F32)
            dst = pt * (dpt - jnp.sum(pt * dpt, axis=0, keepdims=True))
            db_ref[pl.ds(bias0, nkeys), :] += dst
            dsb = dst.astype(MXU_DTYPE)
            dkn_ref[keys, :] += scale * lax.dot_general(dsb, qb, NN_DIMS, preferred_element_type=F32)
            dqn = scale * _diag_blocks(lax.dot_general(dsb, kb, TN_DIMS, preferred_element_type=F32))
            dq, dqg = _rms_pair_bwd(q, qg_ref[...], dqn)
            dz_ref[0, pl.ds(pl.multiple_of(r * GRID_W, GRID_W), GRID_W), :] = dq.astype(dz_ref.dtype)
            return dqg_sum + jnp.sum(dqg, axis=0, keepdims=True)

        def rows4(i, acc):
            for j in range(4):
                acc = row(4 * i + j, acc)
            return acc

        dqg_ref[...] += lax.fori_loop(0, rps // 4, rows4, jnp.zeros((1, 2 * HEAD_DIM), F32))

        @pl.when(rb == nsteps - 1)
        def _():
            dk, dkg = _rms_pair_bwd(k_ref[...], kg_ref[...], dkn_ref[...])
            dz_ref[1] = dk.astype(dz_ref.dtype)
            dz_ref[2] = dv_ref[...].astype(dz_ref.dtype)
            dkg_ref[...] = jnp.sum(dkg, axis=0, keepdims=True)

        if comm is not None:
            @pl.when((pr == npair - 1) & (rb == nsteps - 1))
            def _():
                comm.finish(c_ins, c_outs, *sems)

    pair_cols = lambda lead: pl.BlockSpec((None, t, 2 * HEAD_DIM), lambda p, r: (lead, 0, p))
    pair_vec = pl.BlockSpec((None, 1, 2 * HEAD_DIM), lambda p, r: (p, 0, 0))
    res = pl.pallas_call(
        body, name="attn_bwd", grid=(npair, nsteps),
        in_specs=[pl.BlockSpec((None, blk, 2 * HEAD_DIM), lambda p, r: (0, r, p)), pair_cols(1), pair_cols(2),
                  pl.BlockSpec((1, 2 * HEAD_DIM), lambda p, r: (0, 0)),
                  pl.BlockSpec((1, 2 * HEAD_DIM), lambda p, r: (0, 0)),
                  pl.BlockSpec((None, nb, 2 * GRID_W), lambda p, r: (p, 0, 0)),
                  pl.BlockSpec((blk, 2 * HEAD_DIM), lambda p, r: (r, p))] + [ANY] * ncomm,
        out_specs=[pl.BlockSpec((3, t, 2 * HEAD_DIM), lambda p, r: (0, 0, p)),
                   pl.BlockSpec((None, nb, 2 * GRID_W), lambda p, r: (p, 0, 0)),
                   pair_vec, pair_vec] + [ANY] * ncomm,
        out_shape=[jax.ShapeDtypeStruct((4, t, aw), MXU_DTYPE), jax.ShapeDtypeStruct(bias_t.shape, F32),
                   jax.ShapeDtypeStruct((npair, 1, 2 * HEAD_DIM), F32),
                   jax.ShapeDtypeStruct((npair, 1, 2 * HEAD_DIM), F32)] + (comm.out_shape if comm is not None else []),
        scratch_shapes=[pltpu.VMEM((t, 2 * HEAD_DIM), MXU_DTYPE), pltpu.VMEM((t, 2 * HEAD_DIM), MXU_DTYPE),
                        pltpu.VMEM((t, 2 * HEAD_DIM), F32), pltpu.VMEM((t, 2 * HEAD_DIM), F32)]
        + (comm.scratch if comm is not None else []),
        compiler_params=_cparams(("arbitrary", "arbitrary")),
    )(z4, z4, z4, qg2, kg2, bias_t, dya, *comm_arrs)
    return res[:4], res[4:]


def _s5_mats(a_re, a_im, b_re, b_im, c_re, c_im, log_step, d_skip):
    nd, g, p = a_re.shape
    c = b_re.shape[-1]
    L = S5_CHUNK
    lr = jnp.minimum(a_re, -1e-4).transpose(1, 0, 2)
    li = a_im.transpose(1, 0, 2)
    dt = jnp.exp(log_step).T[..., None]
    n = jnp.arange(L + 1, dtype=F32)[None, :, None, None]
    mag = jnp.exp(n * (lr * dt)[:, None])
    ang = n * (li * dt)[:, None]
    pw_r, pw_i = mag * jnp.cos(ang), mag * jnp.sin(ang)
    den = lr * lr + li * li
    nr, ni = pw_r[:, 1] - 1.0, pw_i[:, 1]
    cr, ci = (nr * lr + ni * li) / den, (ni * lr - nr * li) / den
    bt_r, bt_i = b_re.transpose(1, 3, 0, 2), b_im.transpose(1, 3, 0, 2)
    bb_r = cr[:, None] * bt_r - ci[:, None] * bt_i
    bb_i = cr[:, None] * bt_i + ci[:, None] * bt_r
    ct_r, ct_i = c_re.transpose(1, 2, 0, 3), c_im.transpose(1, 2, 0, 3)

    def cols(x_re, x_im):
        return jnp.concatenate([x_re[..., 0, :], x_re[..., 1, :], x_im[..., 0, :], x_im[..., 1, :]], axis=-1)

    e_r = jnp.stack([pw_r[:, :L, 0][:, ::-1], pw_r[:, :L, 1]], axis=2)
    e_i = jnp.stack([pw_i[:, :L, 0][:, ::-1], pw_i[:, :L, 1]], axis=2)
    ws = (cols(e_r, e_r)[:, :, None] * cols(bb_r, bb_i)[:, None]
          + cols(e_i, e_i)[:, :, None] * cols(-bb_i, bb_r)[:, None]).reshape(g, L * c, 4 * p)
    f_r = jnp.stack([pw_r[:, 1:, 0], pw_r[:, 1:, 1][:, ::-1]], axis=2)
    f_i = jnp.stack([pw_i[:, 1:, 0], pw_i[:, 1:, 1][:, ::-1]], axis=2)
    wot = (cols(f_r, f_i)[:, :, None] * cols(ct_r, -ct_r)[:, None]
           + cols(f_i, f_r)[:, :, None] * cols(-ct_i, -ct_i)[:, None]).reshape(g, L * c, 4 * p)
    qr, qi = pw_r[:, :L, None], pw_i[:, :L, None]
    kp_r, kp_i = qr * bb_r[:, None] - qi * bb_i[:, None], qr * bb_i[:, None] + qi * bb_r[:, None]
    kern = [jnp.einsum('gnip,gop->gino', kp_r[:, :, :, d], ct_r[:, :, d], precision=HI)
            - jnp.einsum('gnip,gop->gino', kp_i[:, :, :, d], ct_i[:, :, d], precision=HI) for d in range(2)]
    skip = d_skip.reshape(g, c, 1, 1) * jnp.eye(c, dtype=F32)[None, :, None, :]
    by_offset = jnp.concatenate([kern[1][:, :, :0:-1], kern[0][:, :, :1] + kern[1][:, :, :1] + skip,
                                 kern[0][:, :, 1:]], axis=2).reshape(g, c, (2 * L - 1) * c)
    mt = jnp.stack([by_offset[:, :, (L - 1 - j) * c:(2 * L - 1 - j) * c] for j in range(L)], axis=1)
    mt = mt.reshape(g, L * c, L * c)
    lr16, li16 = pw_r[:, L], pw_i[:, L]
    fa = jnp.concatenate([lr16[:, 0], lr16[:, 1], lr16[:, 0], lr16[:, 1]], axis=-1)
    fb = jnp.concatenate([-li16[:, 0], -li16[:, 1], li16[:, 0], li16[:, 1]], axis=-1)
    return mt, ws, wot, fa, fb


def _gmm(name, a, b, contract, a_stacked=False, b_stacked=False, o_stacked=False, add=None, out_dtype=F32):
    w = S5_CHUNK * SSM_GROUP_CH
    g = (a.shape[0] if a_stacked else a.shape[1] // w)
    gpb = math.gcd(g, S5_GROUPS_PER_STEP)
    dn = {'nn': (((1,), (0,)), ((), ())), 'nt': (((1,), (1,)), ((), ())), 'tn': (((0,), (0,)), ((), ()))}[contract]

    def spec(arr, stacked):
        if stacked:
            return pl.BlockSpec((gpb,) + arr.shape[1:], lambda i: (i, 0, 0))
        return pl.BlockSpec((arr.shape[0], gpb * w), lambda i: (0, i))

    def take(ref, stacked, e):
        return ref[e] if stacked else ref[:, e * w:(e + 1) * w]

    m = (a.shape[1] if a_stacked else a.shape[0]) if contract != 'tn' else w
    n = w
    if o_stacked:
        o_spec = pl.BlockSpec((gpb, m, n), lambda i: (i, 0, 0))
        o_shape = (g, m, n)
    else:
        o_spec = pl.BlockSpec((m, gpb * n), lambda i: (0, i))
        o_shape = (m, g * n)
    has_add = add is not None

    def body(*refs):
        if has_add:
            a_ref, b_ref, add_ref, o_ref = refs
        else:
            a_ref, b_ref, o_ref = refs
        for e in range(gpb):
            r = lax.dot_general(take(a_ref, a_stacked, e).astype(S5_DTYPE), take(b_ref, b_stacked, e).astype(S5_DTYPE),
                                dn, precision=HI if S5_DTYPE == F32 else None, preferred_element_type=F32)
            if has_add:
                r = r + take(add_ref, o_stacked, e)
            if o_stacked:
                o_ref[e] = r.astype(o_ref.dtype)
            else:
                o_ref[:, e * w:(e + 1) * w] = r.astype(o_ref.dtype)

    in_specs = [spec(a, a_stacked), spec(b, b_stacked)] + ([o_spec] if has_add else [])
    return pl.pallas_call(
        body, name=name, grid=(g // gpb,), in_specs=in_specs, out_specs=o_spec,
        out_shape=jax.ShapeDtypeStruct(o_shape, out_dtype), compiler_params=_cparams(("parallel",)),
    )(*((a, b) + ((add,) if has_add else ())))


def _s5_scan(name, s, fa, fb, rev0, xin=None):
    nk, g, w = s.shape
    hw, qw = w // 2, w // 4
    gb = min(g, 16)
    with_acc = xin is not None

    def body(*refs):
        if with_acc:
            s_ref, a_ref, b_ref, x_ref, o_ref, pa_ref, pb_ref = refs
        else:
            s_ref, a_ref, b_ref, o_ref = refs
        fa_v, fb_v = a_ref[...], b_ref[...]
        dir0 = lax.broadcasted_iota(jnp.int32, (gb, w), 1) % hw < qw
        swap = lambda v: jnp.concatenate([v[:, hw:], v[:, :hw]], axis=1)

        def step(i, carry):
            x, pa, pb = carry
            k0 = (nk - 1 - i) if rev0 else i
            k1 = i if rev0 else (nk - 1 - i)
            for lo in (0, hw):
                o_ref[k0, :, lo:lo + qw] = x[:, lo:lo + qw]
                o_ref[k1, :, lo + qw:lo + hw] = x[:, lo + qw:lo + hw]
            if with_acc:
                xi = jnp.where(dir0, x_ref[k0], x_ref[k1])
                pa = pa + x * xi
                pb = pb + x * swap(xi)
            x = fa_v * x + fb_v * swap(x) + jnp.where(dir0, s_ref[k0], s_ref[k1])
            return x, pa, pb

        z = jnp.zeros((gb, w), F32)
        res = lax.fori_loop(0, nk, step, (z, z, z), unroll=2)
        if with_acc:
            pa_ref[...] = res[1]
            pb_ref[...] = res[2]

    seq = pl.BlockSpec((nk, gb, w), lambda i: (0, i, 0))
    vec = pl.BlockSpec((gb, w), lambda i: (i, 0))
    in_specs = [seq, vec, vec] + ([seq] if with_acc else [])
    out_specs = [seq] + ([vec, vec] if with_acc else [])
    out_shape = [jax.ShapeDtypeStruct((nk, g, w), F32)] + (
        [jax.ShapeDtypeStruct((g, w), F32)] * 2 if with_acc else [])
    return pl.pallas_call(
        body, name=name, grid=(g // gb,), in_specs=in_specs, out_specs=out_specs, out_shape=out_shape,
        compiler_params=_cparams(("parallel",)),
    )(*((s, fa, fb) + ((xin,) if with_acc else ())))


def _regroup(name, x, to_groups):
    if to_groups:
        t, sw = x.shape
    else:
        t, sw = x.shape[0] * S5_CHUNK, x.shape[1] // S5_CHUNK
    nk = t // S5_CHUNK
    wide = LANES * S5_CHUNK

    def place(tok):
        r = lax.broadcasted_iota(jnp.int32, (2 * LANES, wide), 0)
        col = lax.broadcasted_iota(jnp.int32, (2 * LANES, wide), 1)
        ch = r % LANES
        want = (ch // SSM_GROUP_CH) * (S5_CHUNK * SSM_GROUP_CH) + (tok + r // LANES) * SSM_GROUP_CH + ch % SSM_GROUP_CH
        return (col == want).astype(S5_DTYPE)

    def body(x_ref, o_ref):
        token = lambda tok: (pl.ds(tok, nk, stride=S5_CHUNK), slice(None))
        if to_groups:
            acc = jnp.zeros((nk, wide), F32)
            for tok in range(0, S5_CHUNK, 2):
                rows = jnp.concatenate([x_ref[token(tok)], x_ref[token(tok + 1)]], axis=1).astype(S5_DTYPE)
                acc = acc + lax.dot_general(rows, place(tok), NN_DIMS, preferred_element_type=F32)
            o_ref[...] = acc.astype(o_ref.dtype)
        else:
            xv = x_ref[...].astype(S5_DTYPE)
            for tok in range(0, S5_CHUNK, 2):
                both = lax.dot_general(xv, place(tok), NT_DIMS, preferred_element_type=F32).astype(o_ref.dtype)
                o_ref[token(tok)] = both[:, :LANES]
                o_ref[token(tok + 1)] = both[:, LANES:]

    tokens = pl.BlockSpec((t, LANES), lambda i: (0, i))
    groups = pl.BlockSpec((nk, wide), lambda i: (0, i))
    return pl.pallas_call(
        body, name=name, grid=(sw // LANES,), in_specs=[tokens if to_groups else groups],
        out_specs=groups if to_groups else tokens,
        out_shape=jax.ShapeDtypeStruct((nk, sw * S5_CHUNK), S5_DTYPE) if to_groups else jax.ShapeDtypeStruct((t, sw), F32),
        compiler_params=_cparams(("parallel",)),
    )(x)


def _s5_fwd(u2, mats):
    mt, ws, wot, fa, fb = mats
    nk = u2.shape[0]
    g = mt.shape[0]
    y_intra = _gmm("s5_intra", u2, mt, 'nn', b_stacked=True)
    s = _gmm("s5_chunk_state", u2, ws, 'nn', b_stacked=True)
    (xin,) = _s5_scan("s5_scan", s.reshape(nk, g, -1), fa, fb, False)
    xin = xin.reshape(nk, -1)
    return _gmm("s5_inter", xin, wot, 'nt', b_stacked=True, add=y_intra, out_dtype=S5_DTYPE), xin


def _s5_bwd(u2, xin, mats, dy2):
    mt, ws, wot, fa, fb = mats
    nk = u2.shape[0]
    g = mt.shape[0]
    dxin = _gmm("s5_dxin", dy2, wot, 'nn', b_stacked=True)
    ds, pa, pb = _s5_scan("s5_scan_adj", dxin.reshape(nk, g, -1), fa, -fb, True, xin=xin.reshape(nk, g, -1))
    ds = ds.reshape(nk, -1)
    du_a = _gmm("s5_du_intra", dy2, mt, 'nt', b_stacked=True)
    du2 = _gmm("s5_du_state", ds, ws, 'nt', b_stacked=True, add=du_a, out_dtype=S5_DTYPE)
    dmt = _gmm("s5_dmt", u2, dy2, 'tn', o_stacked=True)
    dws = _gmm("s5_dws", u2, ds, 'tn', o_stacked=True)
    dwot = _gmm("s5_dwot", dy2, xin, 'tn', o_stacked=True)
    return du2, (dmt, dws, dwot, pa, pb)


def _stacked(g4):
    return g4.reshape((N_CHIPS, -1, g4.shape[-1]))


def _local_step(x, target, w_in4, late, small, reduce_late=None, reduce_mid=None):
    t, d = x.shape
    aw = w_in4.shape[2]
    sw = aw
    nh = aw // HEAD_DIM
    row = lambda v: v.reshape(1, -1)
    g_mix, g_ffn = row(small['g_mix']), row(small['g_ffn'])
    g_oa, g_os, b_glu = row(small['g_out_attn']), row(small['g_out_ssm']), row(small['b_glu'])
    qg2 = jnp.tile(row(small['q_gain']), (1, 2))
    kg2 = jnp.tile(row(small['k_gain']), (1, 2))

    (h,) = _ew("rms_mix", lambda xv, g: _rms(xv, g)[0], [('r', x), ('c', g_mix)], [('r', d, MXU_DTYPE)])
    bias_t = _bias_table(small['rpb'])
    if late[0] == 'halves':
        under_in, under_attn, under_gate, under_up = late[1][:2], late[1][2:3], late[1][3:4], late[1][4:]
        z4, got_in = _mm("in_proj", h, w_in4, contract='nn', b_mode='b', o_mode='b',
                         comm=_GatherChips(under_in), comm_arrs=under_in)
        ya, got_attn = _attn_fwd(z4, qg2, kg2, bias_t, comm=_GatherChips(under_attn), comm_arrs=under_attn)
        w_glu, w_out = (_stacked(g4).reshape(-1, g4.shape[-1]) for g4 in got_in)
        w_gate4 = _stacked(got_attn[0])
    else:
        z4 = _mm("in_proj", h, w_in4, contract='nn', b_mode='b', o_mode='b')
        ya, _ = _attn_fwd(z4, qg2, kg2, bias_t)
        w_glu, w_out, w_gate4, w_up4, w_down4 = late[1]
    ffs = w_gate4.shape[2]
    s5_params = tuple(small[n] for n in ('ssm_a_re', 'ssm_a_im', 'ssm_b_re', 'ssm_b_im', 'ssm_c_re', 'ssm_c_im',
                                         'ssm_log_step', 'ssm_d'))
    mats, mats_vjp = jax.vjp(_s5_mats, *s5_params)
    mats = tuple(m.astype(S5_DTYPE) for m in mats[:3]) + mats[3:]
    u2 = _regroup("s5_group_u", z4[3], True)
    ypre2, xin = _s5_fwd(u2, mats)
    ypre = _regroup("s5_ungroup_y", ypre2, False)
    (yb,) = _ew("gelu", _gelu, [('r', ypre)], [('r', sw, MXU_DTYPE)])
    a_glu = _mm("glu_proj", yb, w_glu, contract='nn')

    def mix_out(yav, ypv, av, bg, goa, gos):
        ys = _gelu(ypv) * _sigmoid(av + bg)
        return jnp.concatenate([_rms(yav, goa)[0], _rms(ys, gos)[0]], axis=1)
    (ycat,) = _ew("mix_out", mix_out, [('r', ya), ('r', ypre), ('r', a_glu), ('c', b_glu), ('c', g_oa), ('c', g_os)],
                  [('r', aw + sw, MXU_DTYPE)])
    x1 = _mm("out_proj", ycat, w_out, contract='nn', add=x)
    (h2,) = _ew("rms_ffn", lambda xv, g: _rms(xv, g)[0], [('r', x1), ('c', g_ffn)], [('r', d, MXU_DTYPE)])
    if late[0] == 'halves':
        gate4, got_gate = _mm("ffn_gate", h2, w_gate4, contract='nn', b_mode='b', o_mode='b', tn=ffs,
                              out_dtype=MXU_DTYPE, comm=_GatherChips(under_gate), comm_arrs=under_gate)
        w_up4 = _stacked(got_gate[0])
        up4, got_up = _mm("ffn_up", h2, w_up4, contract='nn', b_mode='b', o_mode='b', tn=ffs,
                          out_dtype=MXU_DTYPE, comm=_GatherChips(under_up), comm_arrs=under_up)
        w_down4 = _stacked(got_up[0])
    else:
        gate4 = _mm("ffn_gate", h2, w_gate4, contract='nn', b_mode='b', o_mode='b', tn=ffs, out_dtype=MXU_DTYPE)
        up4 = _mm("ffn_up", h2, w_up4, contract='nn', b_mode='b', o_mode='b', tn=ffs, out_dtype=MXU_DTYPE)
    gate_f, up_f = gate4.reshape(4 * t, ffs), up4.reshape(4 * t, ffs)
    (act,) = _ew("swiglu", lambda gv, uv: gv * _sigmoid(gv) * uv, [('r', gate_f), ('r', up_f)],
                 [('r', ffs, MXU_DTYPE)], tr=512)
    act4 = act.reshape(4, t, ffs)
    x2 = _mm("ffn_down", act4, w_down4, contract='nn', a_mode='c', b_mode='c', add=x1, tk=ffs)

    def loss_fn(xv, tv):
        diff = xv - tv
        return diff * (1.0 / d), diff * (1.0 / d), diff * diff
    dx2, dx2_b, sq = _ew("loss", loss_fn, [('r', x2), ('r', target)], [('r', d, F32), ('r', d, MXU_DTYPE), ('a', d)])

    dact4 = _mm("ffn_down_dx", dx2_b, w_down4, contract='nt', b_mode='b', o_mode='b', tn=ffs, out_dtype=MXU_DTYPE)
    d_w_down4 = _mm("ffn_down_dw", act4, dx2_b, contract='tn', a_mode='b', o_mode='b', tm=ffs, out_dtype=DW_DTYPE)

    def swiglu_bwd(dav, gv, uv):
        s = _sigmoid(gv)
        return dav * uv * s * (1.0 + gv * (1.0 - s)), dav * gv * s
    dgate, dup = _ew("swiglu_bwd", swiglu_bwd, [('r', dact4.reshape(4 * t, ffs)), ('r', gate_f), ('r', up_f)],
                     [('r', ffs, MXU_DTYPE), ('r', ffs, MXU_DTYPE)], tr=512)
    dgate4, dup4 = dgate.reshape(4, t, ffs), dup.reshape(4, t, ffs)
    dh2 = _mm("ffn_gate_dx", dgate4, w_gate4, contract='nt', a_mode='c', b_mode='c', tk=ffs, tn=2048)
    dh2 = _mm("ffn_up_dx", dup4, w_up4, contract='nt', a_mode='c', b_mode='c', add=dh2, tk=ffs)
    d_w_gate4 = _mm("ffn_gate_dw", h2, dgate4, contract='tn', b_mode='b', o_mode='b', tn=ffs, out_dtype=DW_DTYPE)
    d_w_up4 = _mm("ffn_up_dw", h2, dup4, contract='tn', b_mode='b', o_mode='b', tn=ffs, out_dtype=DW_DTYPE)

    def rms_res_bwd(xv, g, dyv, resv):
        dx, dg = _rms_bwd(xv, g, dyv)
        return resv + dx, dg
    dx1, d_g_ffn = _ew("rms_ffn_bwd", rms_res_bwd, [('r', x1), ('c', g_ffn), ('r', dh2), ('r', dx2)],
                       [('r', d, F32), ('a', d)])

    dycat = _mm("out_proj_dx", dx1, w_out, contract='nt', out_dtype=MXU_DTYPE)
    d_w_out = _mm("out_proj_dw", ycat, dx1, contract='tn', out_dtype=DW_DTYPE)

    def mix_out_bwd(yav, ypv, av, bg, goa, gos, dca, dcs):
        dya, dgoa = _rms_bwd(yav, goa, dca)
        y = _gelu(ypv)
        s = _sigmoid(av + bg)
        dys, dgos = _rms_bwd(y * s, gos, dcs)
        da = dys * y * s * (1.0 - s)
        return dya, da, dys * s, dgoa, dgos, da
    dya, da, dy_direct, d_g_oa, d_g_os, d_b_glu = _ew(
        "mix_out_bwd", mix_out_bwd,
        [('r', ya), ('r', ypre), ('r', a_glu), ('c', b_glu), ('c', g_oa), ('c', g_os),
         ('r', dycat, 0, aw), ('r', dycat, 1, sw)],
        [('r', aw, F32), ('r', sw, MXU_DTYPE), ('r', sw, F32), ('a', aw), ('a', sw), ('a', sw)])
    dy = _mm("glu_proj_dx", da, w_glu, contract='nt', add=dy_direct)
    d_w_glu = _mm("glu_proj_dw", yb, da, contract='tn', out_dtype=DW_DTYPE)
    (dypre,) = _ew("gelu_bwd", lambda dyv, ypv: dyv * _gelu_grad(ypv), [('r', dy), ('r', ypre)],
                   [('r', sw, F32)])

    du2, dmats = _s5_bwd(u2, xin, mats, _regroup("s5_group_dy", dypre, True))
    d_s5 = mats_vjp(dmats)
    du = _regroup("s5_ungroup_du", du2, False)
    d_late = (d_w_glu, d_w_out, d_w_gate4, d_w_up4, d_w_down4)
    if reduce_late is not None:
        sums = reduce_late(d_late)
        (dz4, dbias_t, dqg, dkg), scattered = _attn_bwd(z4, qg2, kg2, bias_t, dya, comm=_ScatterChips(sums),
                                                       comm_arrs=sums)
        d_late = (sums, list(scattered))
    else:
        (dz4, dbias_t, dqg, dkg), _ = _attn_bwd(z4, qg2, kg2, bias_t, dya)
    d_rpb = _bias_table_grad(dbias_t)
    fold = lambda v: v.reshape(-1, 2, HEAD_DIM).sum(axis=(0, 1))
    dz4 = dz4.at[3].set(du.astype(dz4.dtype))

    d_w_in4 = _mm("in_proj_dw", h, dz4, contract='tn', b_mode='b', o_mode='b', out_dtype=DW_DTYPE)
    d_mid = [d_w_in4] + [d_s5[i].reshape(-1, LANES) for i in (2, 3, 4, 5)]
    if reduce_late is not None:
        sums = reduce_mid(d_mid)
        dh, scattered = _mm("in_proj_dx", dz4, w_in4, contract='nt', a_mode='c', b_mode='c',
                            comm=_ScatterChips(sums), comm_arrs=sums)
        d_mid = (sums, list(scattered))
    else:
        dh = _mm("in_proj_dx", dz4, w_in4, contract='nt', a_mode='c', b_mode='c')
    dx, d_g_mix = _ew("rms_mix_bwd", rms_res_bwd, [('r', x), ('c', g_mix), ('r', dh), ('r', dx1)],
                      [('r', d, F32), ('a', d)])

    colsum = lambda v: v.sum(axis=0)
    d_small = {
        'g_mix': colsum(d_g_mix), 'q_gain': fold(dqg), 'k_gain': fold(dkg), 'rpb': d_rpb,
        'ssm_a_re': d_s5[0], 'ssm_a_im': d_s5[1], 'ssm_b_re': d_s5[2], 'ssm_b_im': d_s5[3],
        'ssm_c_re': d_s5[4], 'ssm_c_im': d_s5[5], 'ssm_log_step': d_s5[6], 'ssm_d': d_s5[7],
        'b_glu': colsum(d_b_glu), 'g_out_attn': colsum(d_g_oa), 'g_out_ssm': colsum(d_g_os), 'g_ffn': colsum(d_g_ffn),
    }
    return jnp.sum(sq), dx, d_late, d_mid, d_small


ANY = pl.BlockSpec(memory_space=pl.ANY)


def _place():
    x, y, c = lax.axis_index("x"), lax.axis_index("y"), lax.axis_index("c")
    other_chips = [(1 - x, y), (x, 1 - y), (1 - x, 1 - y)]
    return x, y, c, 2 * x + y, (x, y, 1 - c), other_chips


class _GatherChips:
    KINDS = 7

    def __init__(self, arrs):
        self.n = len(arrs)
        self.out_shape = [jax.ShapeDtypeStruct((N_CHIPS,) + a.shape, a.dtype) for a in arrs]
        self.scratch = [pltpu.SemaphoreType.DMA((self.n, self.KINDS)), pltpu.SemaphoreType.DMA((self.n, self.KINDS))]

    def _copies(self, ins, outs, send_sems, recv_sems):
        x, y, c, me, sibling, chips = _place()

        def remote(a, k, src, dst, to):
            return lambda: pltpu.make_async_remote_copy(src_ref=src, dst_ref=dst, send_sem=send_sems.at[a, k],
                                                        recv_sem=recv_sems.at[a, k], device_id=to, device_id_type=MESH)
        own, out, landed, passed, theirs = [], [], [], [], []
        for a in range(self.n):
            own.append(remote(a, 6, ins[a], outs[a].at[me], sibling))
            for j, (px, py) in enumerate(chips):
                there, here = outs[a].at[2 * px + py, c], outs[a].at[2 * px + py, 1 - c]
                out.append(remote(a, j, ins[a].at[c], outs[a].at[me, c], (px, py, c)))
                landed.append(remote(a, j, there, there, (px, py, c)))
                passed.append(remote(a, 3 + j, there, there, sibling))
                theirs.append(remote(a, 3 + j, here, here, sibling))
        return own, out, landed, passed, theirs

    def start(self, ins, outs, send_sems, recv_sems):
        own, out, _, _, _ = self._copies(ins, outs, send_sems, recv_sems)
        for make in own + out:
            make().start()

    def finish(self, ins, outs, send_sems, recv_sems):
        own, out, landed, passed, theirs = self._copies(ins, outs, send_sems, recv_sems)
        for arrived, onward in zip(landed, passed):
            arrived().wait_recv()
            onward().start()
        for make in theirs + own:
            make().wait_recv()
        for make in own + out + passed:
            make().wait_send()


class _ScatterChips:
    def __init__(self, sums):
        self.n = len(sums)
        self.out_shape = [jax.ShapeDtypeStruct(s.shape, s.dtype) for s in sums]
        self.scratch = [pltpu.SemaphoreType.DMA((self.n, 3)), pltpu.SemaphoreType.DMA((self.n, 3))]

    def _copies(self, ins, outs, send_sems, recv_sems):
        x, y, c, me, sibling, chips = _place()
        out, landed = [], []

        def remote(a, j, src, dst, to):
            return lambda: pltpu.make_async_remote_copy(src_ref=src, dst_ref=dst, send_sem=send_sems.at[a, j],
                                                        recv_sem=recv_sems.at[a, j], device_id=to, device_id_type=MESH)
        for a in range(self.n):
            for j, (px, py) in enumerate(chips):
                slot = outs[a].at[2 * px + py]
                out.append(remote(a, j, ins[a].at[2 * px + py], outs[a].at[me], (px, py, c)))
                landed.append(remote(a, j, slot, slot, (px, py, c)))
        return out, landed

    def start(self, ins, outs, send_sems, recv_sems):
        for make in self._copies(ins, outs, send_sems, recv_sems)[0]:
            make().start()

    def finish(self, ins, outs, send_sems, recv_sems):
        out, landed = self._copies(ins, outs, send_sems, recv_sems)
        for make in landed:
            make().wait_recv()
        for make in out:
            make().wait_send()


def _comm_call(name, comm, arrs):
    n = comm.n

    def body(*refs):
        parts = (refs[:n], refs[n:2 * n]) + tuple(refs[2 * n:])
        comm.start(*parts)
        comm.finish(*parts)

    return pl.pallas_call(body, name=name, in_specs=[ANY] * n, out_specs=[ANY] * n, out_shape=comm.out_shape,
                          scratch_shapes=comm.scratch)(*arrs)


def _gather_chips(name, arrs):
    return _comm_call(name, _GatherChips(arrs), arrs)


def _swap_halves(name, parts):
    n = len(parts)

    def body(*refs):
        ins, outs = refs[:n], refs[n:2 * n]
        send_sems, recv_sems = refs[2 * n:]
        x, y, c, me, sibling, chips = _place()
        cps = []
        for a in range(n):
            cp = pltpu.make_async_remote_copy(src_ref=ins[a].at[:, 1 - c], dst_ref=outs[a], send_sem=send_sems.at[a],
                                              recv_sem=recv_sems.at[a], device_id=sibling, device_id_type=MESH)
            cp.start()
            cps.append(cp)
        for cp in cps:
            cp.wait()

    return pl.pallas_call(
        body, name=name, in_specs=[ANY] * n, out_specs=[ANY] * n,
        out_shape=[jax.ShapeDtypeStruct((N_CHIPS,) + p.shape[2:], p.dtype) for p in parts],
        scratch_shapes=[pltpu.SemaphoreType.DMA((n,)), pltpu.SemaphoreType.DMA((n,))],
    )(*parts)


def _scatter_chips(name, sums):
    return _comm_call(name, _ScatterChips(sums), sums)


def _swap_reduced(name, halves):
    n = len(halves)

    def body(*refs):
        ins, outs = refs[:n], refs[n:2 * n]
        send_sems, recv_sems = refs[2 * n:]
        x, y, c, me, sibling, chips = _place()
        cps = []
        for a in range(n):
            cp = pltpu.make_async_remote_copy(src_ref=ins[a], dst_ref=outs[a], send_sem=send_sems.at[a],
                                              recv_sem=recv_sems.at[a], device_id=sibling, device_id_type=MESH)
            cp.start()
            cps.append(cp)
        for cp in cps:
            cp.wait()

    return pl.pallas_call(
        body, name=name, in_specs=[ANY] * n, out_specs=[ANY] * n,
        out_shape=[jax.ShapeDtypeStruct(h.shape, h.dtype) for h in halves],
        scratch_shapes=[pltpu.SemaphoreType.DMA((n,)), pltpu.SemaphoreType.DMA((n,))],
    )(*halves)


def _row_tile(r, want=256):
    t = (min(r, want) // SUBLANES) * SUBLANES
    while r % t:
        t -= SUBLANES
    return t


def _add_own_half(name, part, got, c, out_dtype):
    _, _, r, cols = part.shape
    tr = _row_tile(r)

    def body(c_ref, p_ref, g_ref, o_ref):
        o_ref[...] = (p_ref[...].astype(F32) + g_ref[...].astype(F32)).astype(o_ref.dtype)

    return pl.pallas_call(
        body, name=name,
        grid_spec=pltpu.PrefetchScalarGridSpec(
            num_scalar_prefetch=1, grid=(N_CHIPS, r // tr),
            in_specs=[pl.BlockSpec((None, None, tr, cols), lambda s, i, c_ref: (s, c_ref[0], i, 0)),
                      pl.BlockSpec((None, tr, cols), lambda s, i, c_ref: (s, i, 0))],
            out_specs=pl.BlockSpec((None, tr, cols), lambda s, i, c_ref: (s, i, 0))),
        out_shape=jax.ShapeDtypeStruct(got.shape, out_dtype),
        compiler_params=_cparams(("parallel", "parallel")),
    )(c.reshape(1).astype(jnp.int32), part, got)


def _sum_chips(name, got, own, me):
    _, r, cols = got.shape
    tr = _row_tile(r)

    def body(me_ref, r0, r1, r2, r3, own_ref, o_ref):
        pick = lambda s, ref: jnp.where(me_ref[0] == s, own_ref[...], ref[...]).astype(F32)
        o_ref[...] = ((pick(0, r0) + pick(1, r1)) + pick(2, r2)) + pick(3, r3)

    def slot(s):
        return pl.BlockSpec((None, tr, cols),
                            lambda i, me_ref: (jnp.where(me_ref[0] == s, (s + 1) % N_CHIPS, s), i, 0))

    return pl.pallas_call(
        body, name=name,
        grid_spec=pltpu.PrefetchScalarGridSpec(
            num_scalar_prefetch=1, grid=(r // tr,),
            in_specs=[slot(s) for s in range(N_CHIPS)]
            + [pl.BlockSpec((None, tr, cols), lambda i, me_ref: (me_ref[0], i, 0))],
            out_specs=pl.BlockSpec((tr, cols), lambda i, me_ref: (i, 0))),
        out_shape=jax.ShapeDtypeStruct((r, cols), F32),
        compiler_params=_cparams(("parallel",)),
    )(me.reshape(1).astype(jnp.int32), got, got, got, got, own)


def _adamw_math(wv, gv, mv, vv):
    mv = ADAM_B1 * mv + (1.0 - ADAM_B1) * gv
    vv = ADAM_B2 * vv + (1.0 - ADAM_B2) * (gv * gv)
    m_hat = mv / (1.0 - ADAM_B1 ** ADAM_STEP)
    v_hat = vv / (1.0 - ADAM_B2 ** ADAM_STEP)
    return -ADAM_LR * (m_hat / (jnp.sqrt(v_hat) + ADAM_EPS) + ADAM_WD * wv), mv, vv


def _adamw(name, w, g, m, v):
    cols = w.shape[1]
    return _ew(name, _adamw_math, [('r', w), ('r', g), ('r', m), ('r', v)], [('r', cols, F32)] * 3,
               tr=_row_tile(w.shape[0], 128))


def _adamw_halves(name, w, mine, theirs, m, v, c):
    r, cols = mine.shape
    tr = _row_tile(r, 128)
    nb = r // tr

    def body(c_ref, w_ref, a_ref, b_ref, m_ref, v_ref, g_out, d_out, m_out, v_out):
        g = jnp.where(pl.program_id(0) == c_ref[0], a_ref[...], b_ref[...])
        g_out[...] = g
        d_out[...], m_out[...], v_out[...] = _adamw_math(w_ref[...], g, m_ref[...], v_ref[...])

    whole = pl.BlockSpec((tr, cols), lambda h, i, c_ref: (h * nb + i, 0))
    half = pl.BlockSpec((tr, cols), lambda h, i, c_ref: (i, 0))
    return pl.pallas_call(
        body, name=name,
        grid_spec=pltpu.PrefetchScalarGridSpec(
            num_scalar_prefetch=1, grid=(2, nb),
            in_specs=[whole, half, half, whole, whole], out_specs=[whole] * 4),
        out_shape=[jax.ShapeDtypeStruct(w.shape, F32)] * 4,
        compiler_params=_cparams(("parallel", "parallel")),
    )(c.reshape(1).astype(jnp.int32), w, mine, theirs, m, v)


SMALL_ROWS_ALIGN = 2 * N_CHIPS * SUBLANES


MEDIUM_NAMES = ['ssm_b_re', 'ssm_b_im', 'ssm_c_re', 'ssm_c_im']
PACKED_NAMES = [n for n in SMALL_NAMES if n not in MEDIUM_NAMES]


def _pack_small(d):
    flat = jnp.concatenate([d[n].reshape(-1).astype(F32) for n in PACKED_NAMES])
    rows = -(-flat.shape[0] // (LANES * SMALL_ROWS_ALIGN)) * SMALL_ROWS_ALIGN
    return jnp.pad(flat, (0, rows * LANES - flat.shape[0])).reshape(rows, LANES)


def _unpack_small(packed, like):
    flat = packed.reshape(-1)
    out, off = {}, 0
    for n in PACKED_NAMES:
        size = like[n].size
        out[n] = flat[off:off + size].reshape(like[n].shape)
        off += size
    return out


def kernel(x, g_mix, w_in, q_gain, k_gain, rpb, ssm_a_re, ssm_a_im, ssm_b_re, ssm_b_im, ssm_c_re, ssm_c_im, ssm_log_step, ssm_d, w_glu, b_glu, g_out_attn, g_out_ssm, w_out, g_ffn, w_ffn_gate, w_ffn_up, w_ffn_down, loss_target, m_g_mix, m_w_in, m_q_gain, m_k_gain, m_rpb, m_ssm_a_re, m_ssm_a_im, m_ssm_b_re, m_ssm_b_im, m_ssm_c_re, m_ssm_c_im, m_ssm_log_step, m_ssm_d, m_w_glu, m_b_glu, m_g_out_attn, m_g_out_ssm, m_w_out, m_g_ffn, m_w_ffn_gate, m_w_ffn_up, m_w_ffn_down, v_g_mix, v_w_in, v_q_gain, v_k_gain, v_rpb, v_ssm_a_re, v_ssm_a_im, v_ssm_b_re, v_ssm_b_im, v_ssm_c_re, v_ssm_c_im, v_ssm_log_step, v_ssm_d, v_w_glu, v_b_glu, v_g_out_attn, v_g_out_ssm, v_w_out, v_g_ffn, v_w_ffn_gate, v_w_ffn_up, v_w_ffn_down):
    given = dict(locals())
    w = {n: given[n][0] for n in WEIGHT_NAMES}
    mom = {n: given["m_" + n][0] for n in WEIGHT_NAMES}
    var = {n: given["v_" + n][0] for n in WEIGHT_NAMES}
    d = x.shape[-1]
    c = lax.axis_index("c")

    halves = {n: w[n].astype(MXU_DTYPE).reshape((2, w[n].shape[0] // 2, w[n].shape[1])) for n in BIG_NAMES}
    (w_in4,) = _gather_chips("gather_w_in", [halves['w_in']])
    w_in4 = w_in4.reshape((N_CHIPS, -1, w_in4.shape[-1]))

    def chip_sums(tag, grads, payload):
        parts = [g.reshape((N_CHIPS, 2, -1, g.shape[-1])) for g in grads]
        got = _swap_halves("reduce_swap_halves_" + tag, parts)
        return [_add_own_half("reduce_add_%s_%d" % (tag, a), p, gt, c, dt)
                for a, (p, gt, dt) in enumerate(zip(parts, got, payload))]

    reduce_late = lambda grads: chip_sums("late", grads, [GRAD_PAYLOAD_DTYPE] * len(grads))
    reduce_mid = lambda grads: chip_sums("mid", grads, [GRAD_PAYLOAD_DTYPE] + [F32] * (len(grads) - 1))
    sq, dx, (sums_late, got_late), (sums_mid, got_mid), d_small = _local_step(
        x[0], loss_target[0], w_in4, ('halves', [halves[n] for n in LATE_NAMES]), {n: w[n] for n in SMALL_NAMES},
        reduce_late, reduce_mid)
    loss = lax.psum(0.5 * sq / d, ("x", "y", "c"))

    nbig = len(BIG_NAMES)
    sums_tiny = chip_sums("tiny", [_pack_small(d_small)], [F32])
    got_tiny = list(_scatter_chips("reduce_scatter_tiny", sums_tiny))
    sums = sums_mid[:1] + sums_late + sums_mid[1:] + sums_tiny
    got = got_mid[:1] + got_late + got_mid[1:] + got_tiny
    me = 2 * lax.axis_index("x") + lax.axis_index("y")
    mine = [_sum_chips("reduce_sum_%d" % a, gt, sm_, me) for a, (gt, sm_) in enumerate(zip(got, sums))]
    theirs = _swap_reduced("reduce_swap_reduced", mine)
    in_order = lambda a: jnp.where(c == 0, jnp.stack([mine[a], theirs[a]]), jnp.stack([theirs[a], mine[a]]))
    repl = _gather_chips("gather_small", [in_order(a) for a in range(nbig, len(mine))])
    repl = [r.reshape(-1, LANES) for r in repl]
    like = {n: w[n] for n in SMALL_NAMES}
    grad_small = _unpack_small(repl[-1], like)
    grad_small.update({n: r.reshape(w[n].shape) for n, r in zip(MEDIUM_NAMES, repl)})

    grad_big, delta, new_m, new_v = {}, {}, {}, {}
    for a, n in enumerate(BIG_NAMES):
        grad_big[n], delta[n], new_m[n], new_v[n] = _adamw_halves("adamw_%d" % a, w[n], mine[a], theirs[a],
                                                                  mom[n], var[n], c)
    for n, r in zip(MEDIUM_NAMES, repl):
        res = _adamw("adamw_" + n, w[n].reshape(-1, LANES), r, mom[n].reshape(-1, LANES), var[n].reshape(-1, LANES))
        delta[n], new_m[n], new_v[n] = (t.reshape(w[n].shape) for t in res)
    sd, sm, sv = _adamw("adamw_small", _pack_small(w), repl[-1], _pack_small(mom), _pack_small(var))
    delta.update(_unpack_small(sd, like))
    new_m.update(_unpack_small(sm, like))
    new_v.update(_unpack_small(sv, like))
    grads = {**grad_big, **grad_small}
    lead = lambda t: t[None]
    return (loss, dx[None], *[lead(grads[n]) for n in WEIGHT_NAMES], *[lead(delta[n]) for n in WEIGHT_NAMES],
            *[lead(new_m[n]) for n in WEIGHT_NAMES], *[lead(new_v[n]) for n in WEIGHT_NAMES])
```

```python
import functools
import math

import jax
import jax.numpy as jnp
from jax import lax
from jax.experimental import pallas as pl
from jax.experimental.pallas import tpu as pltpu

F32 = jnp.float32
BF16 = jnp.bfloat16
MXU_DTYPE = BF16
GRAD_PAYLOAD_DTYPE = BF16
DW_DTYPE = BF16
S5_DTYPE = BF16
HI = lax.Precision.HIGHEST
VMEM_LIMIT_V7X = 56 * 1024 * 1024
LANES = 128
SUBLANES = 8

GRID_W = 64
WIN_H = 8
WIN_W = 16
HEAD_DIM = 64
SSM_GROUP_CH = 16
SSM_STATE = 64
S5_CHUNK = 16
S5_GROUPS_PER_STEP = 8
RMS_EPS = 1e-6
NEG_INF = -1e30
N_CHIPS = 4
MESH = pl.DeviceIdType.MESH

ADAM_LR = 0.001
ADAM_B1 = 0.9
ADAM_B2 = 0.999
ADAM_EPS = 1e-08
ADAM_WD = 0.01
ADAM_STEP = 10

WEIGHT_NAMES = ['g_mix', 'w_in', 'q_gain', 'k_gain', 'rpb', 'ssm_a_re', 'ssm_a_im', 'ssm_b_re', 'ssm_b_im',
                'ssm_c_re', 'ssm_c_im', 'ssm_log_step', 'ssm_d', 'w_glu', 'b_glu', 'g_out_attn', 'g_out_ssm',
                'w_out', 'g_ffn', 'w_ffn_gate', 'w_ffn_up', 'w_ffn_down']
BIG_NAMES = ['w_in', 'w_glu', 'w_out', 'w_ffn_gate', 'w_ffn_up', 'w_ffn_down']
LATE_NAMES = BIG_NAMES[1:]
SMALL_NAMES = [n for n in WEIGHT_NAMES if n not in BIG_NAMES]


def _cparams(sem):
    return pltpu.CompilerParams(dimension_semantics=sem, vmem_limit_bytes=VMEM_LIMIT_V7X)


def _tile(n, want):
    if n <= want:
        return n
    t = (want // LANES) * LANES
    while t >= LANES:
        if n % t == 0:
            return t
        t -= LANES
    return n


def _mm(name, a, b, *, contract, a_mode='2', b_mode='2', o_mode='2', out_dtype=F32, add=None, exact=False,
        tm=1024, tn=1024, tk=2048, comm=None, comm_arrs=()):
    dn = {'nn': (((1,), (0,)), ((), ())), 'nt': (((1,), (1,)), ((), ())), 'tn': (((0,), (0,)), ((), ()))}[contract]
    ar, ac = a.shape[-2:]
    br, bc = b.shape[-2:]
    m, kdim = (ar, ac) if contract != 'tn' else (ac, ar)
    n = bc if contract != 'nt' else br
    assert kdim == (br if contract != 'nt' else bc), (name, a.shape, b.shape)
    nbatch = 1
    for arr, mode in ((a, a_mode), (b, b_mode)):
        if mode == 'b':
            nbatch = arr.shape[0]
    nstack = 1
    for arr, mode in ((a, a_mode), (b, b_mode)):
        if mode == 'c':
            nstack = arr.shape[0]
    tm, tn, tk = _tile(m, tm), _tile(n, tn), _tile(kdim, tk)
    nkin = kdim // tk
    nk = nstack * nkin
    grid = (nbatch, m // tm, n // tn, nk)

    def spec(mode, block, rc):
        def imap(s, i, j, kk):
            r, c = rc(i, j, kk % nkin)
            if mode == '2':
                return (r, c)
            return (s if mode == 'b' else kk // nkin, r, c)
        return pl.BlockSpec(block if mode == '2' else (None,) + block, imap)

    a_spec = spec(a_mode, (tm, tk) if contract != 'tn' else (tk, tm),
                  (lambda i, j, k: (i, k)) if contract != 'tn' else (lambda i, j, k: (k, i)))
    b_spec = spec(b_mode, (tk, tn) if contract != 'nt' else (tn, tk),
                  (lambda i, j, k: (k, j)) if contract != 'nt' else (lambda i, j, k: (j, k)))
    o_spec = spec(o_mode, (tm, tn), lambda i, j, k: (i, j))
    out_shape = (m, n) if o_mode == '2' else (nbatch, m, n)
    has_add = add is not None

    def product(a_ref, b_ref):
        if exact:
            return lax.dot_general(a_ref[...].astype(F32), b_ref[...].astype(F32), dn, precision=HI,
                                   preferred_element_type=F32)
        return lax.dot_general(a_ref[...].astype(MXU_DTYPE), b_ref[...].astype(MXU_DTYPE), dn,
                               preferred_element_type=F32)

    ncomm = len(comm_arrs)
    nacc = int(nk > 1)

    def body(*refs):
        a_ref, b_ref = refs[:2]
        add_ref = refs[2] if has_add else None
        c_ins = refs[2 + has_add:2 + has_add + ncomm]
        o_ref = refs[2 + has_add + ncomm]
        c_outs = refs[3 + has_add + ncomm:3 + has_add + 2 * ncomm]
        sems = refs[3 + has_add + 2 * ncomm + nacc:]
        ids = [pl.program_id(ax) for ax in range(4)]
        if comm is not None:
            @pl.when((ids[0] == 0) & (ids[1] == 0) & (ids[2] == 0) & (ids[3] == 0))
            def _():
                comm.start(c_ins, c_outs, *sems)

        def write(r):
            if has_add:
                r = r + add_ref[...].astype(F32)
            o_ref[...] = r.astype(o_ref.dtype)

        if nk == 1:
            write(product(a_ref, b_ref))
        else:
            acc_ref = refs[3 + has_add + 2 * ncomm]

            @pl.when(ids[3] == 0)
            def _():
                acc_ref[...] = jnp.zeros_like(acc_ref)

            acc_ref[...] += product(a_ref, b_ref)

            @pl.when(ids[3] == nk - 1)
            def _():
                write(acc_ref[...])

        if comm is not None:
            @pl.when((ids[0] == grid[0] - 1) & (ids[1] == grid[1] - 1) & (ids[2] == grid[2] - 1) & (ids[3] == nk - 1))
            def _():
                comm.finish(c_ins, c_outs, *sems)

    in_specs = [a_spec, b_spec] + ([o_spec] if has_add else []) + [ANY] * ncomm
    args = (a, b) + ((add,) if has_add else ()) + tuple(comm_arrs)
    res = pl.pallas_call(
        body, name=name, grid=grid, in_specs=in_specs, out_specs=[o_spec] + [ANY] * ncomm,
        out_shape=[jax.ShapeDtypeStruct(out_shape, out_dtype)] + (comm.out_shape if comm is not None else []),
        scratch_shapes=([pltpu.VMEM((tm, tn), F32)] if nk > 1 else []) + (comm.scratch if comm is not None else []),
        compiler_params=_cparams(("parallel", "parallel", "parallel", "arbitrary") if comm is None
                                 else ("arbitrary",) * 4),
    )(*args)
    return res[0] if comm is None else (res[0], res[1:])


def _ew(name, fn, ins, outs, tr=256):
    rows = next(x[1].shape[0] for x in ins if x[0] == 'r')
    tr = min(tr, rows)
    assert rows % tr == 0 and tr % SUBLANES == 0, (name, rows, tr)
    in_specs, args = [], []
    for x in ins:
        if x[0] == 'r' and len(x) == 2:
            in_specs.append(pl.BlockSpec((tr, x[1].shape[1]), lambda i: (i, 0)))
        elif x[0] == 'r':
            in_specs.append(pl.BlockSpec((tr, x[3]), functools.partial(lambda cb, i: (i, cb), x[2])))
        else:
            in_specs.append(pl.BlockSpec(x[1].shape, lambda i: (0, 0)))
        args.append(x[1])
    out_specs, out_shapes = [], []
    for o in outs:
        if o[0] == 'r':
            out_specs.append(pl.BlockSpec((tr, o[1]), lambda i: (i, 0)))
            out_shapes.append(jax.ShapeDtypeStruct((rows, o[1]), o[2]))
        else:
            out_specs.append(pl.BlockSpec((SUBLANES, o[1]), lambda i: (0, 0)))
            out_shapes.append(jax.ShapeDtypeStruct((SUBLANES, o[1]), F32))
    nin = len(ins)
    has_acc = any(o[0] == 'a' for o in outs)

    def body(*refs):
        vals = fn(*[r[...].astype(F32) for r in refs[:nin]])
        if not isinstance(vals, (tuple, list)):
            vals = (vals,)
        i = pl.program_id(0)
        for o, ref, v in zip(outs, refs[nin:], vals):
            if o[0] == 'r':
                ref[...] = v.astype(ref.dtype)
            else:
                part = v.astype(F32).reshape(tr // SUBLANES, SUBLANES, o[1]).sum(axis=0)

                @pl.when(i == 0)
                def _(ref=ref, part=part):
                    ref[...] = part

                @pl.when(i > 0)
                def _(ref=ref, part=part):
                    ref[...] += part

    res = pl.pallas_call(
        body, name=name, grid=(rows // tr,), in_specs=in_specs, out_specs=out_specs, out_shape=out_shapes,
        compiler_params=_cparams(("arbitrary",) if has_acc else ("parallel",)),
    )(*args)
    return res


def _rms(x, g):
    r = lax.rsqrt(jnp.mean(x * x, axis=-1, keepdims=True) + RMS_EPS)
    xr = x * r
    return xr * g, xr


def _rms_bwd(x, g, dy):
    r = lax.rsqrt(jnp.mean(x * x, axis=-1, keepdims=True) + RMS_EPS)
    xr = x * r
    gdy = g * dy
    dx = r * (gdy - xr * jnp.mean(xr * gdy, axis=-1, keepdims=True))
    return dx, dy * xr


def _sigmoid(x):
    return 0.5 * (jnp.tanh(0.5 * x) + 1.0)


_GELU_C = math.sqrt(2.0 / math.pi)


def _gelu(x):
    return 0.5 * x * (1.0 + jnp.tanh(_GELU_C * (x + 0.044715 * x * x * x)))


def _gelu_grad(x):
    t = jnp.tanh(_GELU_C * (x + 0.044715 * x * x * x))
    return 0.5 * (1.0 + t) + 0.5 * x * (1.0 - t * t) * _GELU_C * (1.0 + 3 * 0.044715 * x * x)


ATTN_ROWS_PER_STEP = 8
NT_DIMS = (((1,), (1,)), ((), ()))
NN_DIMS = (((1,), (0,)), ((), ()))
TN_DIMS = (((0,), (0,)), ((), ()))


def _attn_geometry(r, rows):
    row_start = jnp.clip(r - WIN_H // 2, 0, rows - WIN_H)
    key0 = pl.multiple_of(row_start * GRID_W, GRID_W)
    bias0 = pl.multiple_of((row_start - r + (WIN_H - 1)) * GRID_W, GRID_W)
    return key0, bias0


def _window_onehot():
    c = jnp.arange(GRID_W)
    col_start = jnp.clip(c - WIN_W // 2, 0, GRID_W - WIN_W)
    col_in = (c[None, :] >= col_start[:, None]) & (c[None, :] < col_start[:, None] + WIN_W)
    dc = jnp.clip(c[None, :] - c[:, None], -(WIN_W - 1), WIN_W - 1) + (WIN_W - 1)
    onehot = ((dc[:, :, None] == jnp.arange(2 * WIN_W - 1)[None, None, :]) & col_in[:, :, None]).astype(F32)
    return onehot, col_in


def _bias_table(rpb):
    onehot, col_in = _window_onehot()
    nh = rpb.shape[0]
    pairs = rpb.reshape(nh // 2, 2, 2 * WIN_H - 1, 2 * WIN_W - 1)
    mask = jnp.where(col_in, 0.0, NEG_INF).T
    heads = [jnp.einsum('prd,qkd->prkq', pairs[:, e], onehot, precision=HI) + mask for e in range(2)]
    return jnp.concatenate(heads, axis=-1).reshape(nh // 2, (2 * WIN_H - 1) * GRID_W, 2 * GRID_W)


def _bias_table_grad(dtab):
    onehot, _ = _window_onehot()
    npair = dtab.shape[0]
    d = dtab.reshape(npair, 2 * WIN_H - 1, GRID_W, 2 * GRID_W)
    heads = [jnp.einsum('prkq,qkd->prd', d[..., e * GRID_W:(e + 1) * GRID_W], onehot, precision=HI) for e in range(2)]
    return jnp.stack(heads, axis=1).reshape(2 * npair, 2 * WIN_H - 1, 2 * WIN_W - 1)


def _lane_lo(shape):
    return lax.broadcasted_iota(jnp.int32, shape, 1) < HEAD_DIM


def _half_sums(v):
    lo = _lane_lo(v.shape)
    s_lo = jnp.sum(jnp.where(lo, v, 0.0), axis=1, keepdims=True)
    s_hi = jnp.sum(jnp.where(lo, 0.0, v), axis=1, keepdims=True)
    return jnp.where(lo, s_lo, s_hi)


def _rms_pair(x, g):
    r = lax.rsqrt(_half_sums(x * x) * (1.0 / HEAD_DIM) + RMS_EPS)
    return x * r * g


def _rms_pair_bwd(x, g, dy):
    r = lax.rsqrt(_half_sums(x * x) * (1.0 / HEAD_DIM) + RMS_EPS)
    xr = x * r
    gdy = g * dy
    dx = r * (gdy - xr * (_half_sums(xr * gdy) * (1.0 / HEAD_DIM)))
    return dx, dy * xr


def _blockdiag(a):
    a2 = jnp.concatenate([a, a], axis=0)
    row_hi = lax.broadcasted_iota(jnp.int32, a2.shape, 0) >= GRID_W
    lane_hi = lax.broadcasted_iota(jnp.int32, a2.shape, 1) >= HEAD_DIM
    return jnp.where(row_hi == lane_hi, a2, 0.0).astype(MXU_DTYPE)


def _diag_blocks(m):
    return jnp.where(_lane_lo((GRID_W, 2 * HEAD_DIM)), m[:GRID_W], m[GRID_W:])


def _attn_scores(qb, kb, bias):
    st = lax.dot_general(kb, qb, NT_DIMS, preferred_element_type=F32)
    st = st * (1.0 / math.sqrt(HEAD_DIM)) + bias
    mx = jnp.max(st, axis=0, keepdims=True)
    p = jnp.exp(st - mx)
    return p * (1.0 / jnp.sum(p, axis=0, keepdims=True))


def _attn_fwd(z4, qg2, kg2, bias_t, comm=None, comm_arrs=()):
    _, t, aw = z4.shape
    rows = t // GRID_W
    npair = aw // (2 * HEAD_DIM)
    nkeys = WIN_H * GRID_W
    nb = bias_t.shape[1]
    rps = min(ATTN_ROWS_PER_STEP, rows)
    blk = rps * GRID_W
    nsteps = rows // rps
    ncomm = len(comm_arrs)

    def body(*refs):
        q_ref, k_ref, v_ref, qg_ref, kg_ref, b_ref = refs[:6]
        c_ins, o_ref, c_outs = refs[6:6 + ncomm], refs[6 + ncomm], refs[7 + ncomm:7 + 2 * ncomm]
        kn_ref, vb_ref = refs[7 + 2 * ncomm:9 + 2 * ncomm]
        sems = refs[9 + 2 * ncomm:]
        pr, rb = pl.program_id(0), pl.program_id(1)
        if comm is not None:
            @pl.when((pr == 0) & (rb == 0))
            def _():
                comm.start(c_ins, c_outs, *sems)

        @pl.when(rb == 0)
        def _():
            kn_ref[...] = _rms_pair(k_ref[...], kg_ref[...]).astype(MXU_DTYPE)
            vb_ref[...] = v_ref[...].astype(MXU_DTYPE)

        def row(i, carry):
            key0, bias0 = _attn_geometry(rb * rps + i, rows)
            at = pl.ds(pl.multiple_of(i * GRID_W, GRID_W), GRID_W)
            qb = _blockdiag(_rms_pair(q_ref[at, :], qg_ref[...]))
            pt = _attn_scores(qb, kn_ref[pl.ds(key0, nkeys), :], b_ref[pl.ds(bias0, nkeys), :])
            both = lax.dot_general(pt.astype(MXU_DTYPE), vb_ref[pl.ds(key0, nkeys), :], TN_DIMS,
                                   preferred_element_type=F32)
            o_ref[at, :] = _diag_blocks(both)
            return carry

        lax.fori_loop(0, rps, row, 0, unroll=4)
        if comm is not None:
            @pl.when((pr == npair - 1) & (rb == nsteps - 1))
            def _():
                comm.finish(c_ins, c_outs, *sems)

    pair_cols = lambda lead: pl.BlockSpec((None, t, 2 * HEAD_DIM), lambda p, r: (lead, 0, p))
    res = pl.pallas_call(
        body, name="attn_fwd", grid=(npair, nsteps),
        in_specs=[pl.BlockSpec((None, blk, 2 * HEAD_DIM), lambda p, r: (0, r, p)), pair_cols(1), pair_cols(2),
                  pl.BlockSpec((1, 2 * HEAD_DIM), lambda p, r: (0, 0)),
                  pl.BlockSpec((1, 2 * HEAD_DIM), lambda p, r: (0, 0)),
                  pl.BlockSpec((None, nb, 2 * GRID_W), lambda p, r: (p, 0, 0))] + [ANY] * ncomm,
        out_specs=[pl.BlockSpec((blk, 2 * HEAD_DIM), lambda p, r: (r, p))] + [ANY] * ncomm,
        out_shape=[jax.ShapeDtypeStruct((t, aw), F32)] + (comm.out_shape if comm is not None else []),
        scratch_shapes=[pltpu.VMEM((t, 2 * HEAD_DIM), MXU_DTYPE), pltpu.VMEM((t, 2 * HEAD_DIM), MXU_DTYPE)]
        + (comm.scratch if comm is not None else []),
        compiler_params=_cparams(("arbitrary", "arbitrary")),
    )(z4, z4, z4, qg2, kg2, bias_t, *comm_arrs)
    return res[0], res[1:]


def _attn_bwd(z4, qg2, kg2, bias_t, dya, comm=None, comm_arrs=()):
    _, t, aw = z4.shape
    rows = t // GRID_W
    npair = aw // (2 * HEAD_DIM)
    nkeys = WIN_H * GRID_W
    nb = bias_t.shape[1]
    rps = min(ATTN_ROWS_PER_STEP, rows)
    blk = rps * GRID_W
    nsteps = rows // rps
    scale = 1.0 / math.sqrt(HEAD_DIM)
    ncomm = len(comm_arrs)

    def body(*refs):
        q_ref, k_ref, v_ref, qg_ref, kg_ref, b_ref, do_ref = refs[:7]
        c_ins = refs[7:7 + ncomm]
        dz_ref, db_ref, dqg_ref, dkg_ref = refs[7 + ncomm:11 + ncomm]
        c_outs = refs[11 + ncomm:11 + 2 * ncomm]
        kn_ref, vb_ref, dkn_ref, dv_ref = refs[11 + 2 * ncomm:15 + 2 * ncomm]
        sems = refs[15 + 2 * ncomm:]
        pr, rb = pl.program_id(0), pl.program_id(1)
        if comm is not None:
            @pl.when((pr == 0) & (rb == 0))
            def _():
                comm.start(c_ins, c_outs, *sems)

        @pl.when(rb == 0)
        def _():
            kn_ref[...] = _rms_pair(k_ref[...], kg_ref[...]).astype(MXU_DTYPE)
            vb_ref[...] = v_ref[...].astype(MXU_DTYPE)
            dkn_ref[...] = jnp.zeros_like(dkn_ref)
            dv_ref[...] = jnp.zeros_like(dv_ref)
            db_ref[...] = jnp.zeros_like(db_ref)
            dqg_ref[...] = jnp.zeros_like(dqg_ref)

        def row(i, dqg_sum):
            r = rb * rps + i
            key0, bias0 = _attn_geometry(r, rows)
            keys = pl.ds(key0, nkeys)
            at = pl.ds(pl.multiple_of(i * GRID_W, GRID_W), GRID_W)
            q = q_ref[at, :]
            qb = _blockdiag(_rms_pair(q, qg_ref[...]))
            dob = _blockdiag(do_ref[at, :])
            kb = kn_ref[keys, :]
            pt = _attn_scores(qb, kb, b_ref[pl.ds(bias0, nkeys), :])
            dv_ref[keys, :] += lax.dot_general(pt.astype(MXU_DTYPE), dob, NN_DIMS, preferred_element_type=F32)
            dpt = lax.dot_general(vb_ref[keys, :], dob, NT_DIMS, preferred_element_type=F32)
            dst = pt * (dpt - jnp.sum(pt * dpt, axis=0, keepdims=True))
            db_ref[pl.ds(bias0, nkeys), :] += dst
            dsb = dst.astype(MXU_DTYPE)
            dkn_ref[keys, :] += scale * lax.dot_general(dsb, qb, NN_DIMS, preferred_element_type=F32)
            dqn = scale * _diag_blocks(lax.dot_general(dsb, kb, TN_DIMS, preferred_element_type=F32))
            dq, dqg = _rms_pair_bwd(q, qg_ref[...], dqn)
            dz_ref[0, pl.ds(pl.multiple_of(r * GRID_W, GRID_W), GRID_W), :] = dq.astype(dz_ref.dtype)
            return dqg_sum + jnp.sum(dqg, axis=0, keepdims=True)

        def rows4(i, acc):
            for j in range(4):
                acc = row(4 * i + j, acc)
            return acc

        dqg_ref[...] += lax.fori_loop(0, rps // 4, rows4, jnp.zeros((1, 2 * HEAD_DIM), F32))

        @pl.when(rb == nsteps - 1)
        def _():
            dk, dkg = _rms_pair_bwd(k_ref[...], kg_ref[...], dkn_ref[...])
            dz_ref[1] = dk.astype(dz_ref.dtype)
            dz_ref[2] = dv_ref[...].astype(dz_ref.dtype)
            dkg_ref[...] = jnp.sum(dkg, axis=0, keepdims=True)

        if comm is not None:
            @pl.when((pr == npair - 1) & (rb == nsteps - 1))
            def _():
                comm.finish(c_ins, c_outs, *sems)

    pair_cols = lambda lead: pl.BlockSpec((None, t, 2 * HEAD_DIM), lambda p, r: (lead, 0, p))
    pair_vec = pl.BlockSpec((None, 1, 2 * HEAD_DIM), lambda p, r: (p, 0, 0))
    res = pl.pallas_call(
        body, name="attn_bwd", grid=(npair, nsteps),
        in_specs=[pl.BlockSpec((None, blk, 2 * HEAD_DIM), lambda p, r: (0, r, p)), pair_cols(1), pair_cols(2),
                  pl.BlockSpec((1, 2 * HEAD_DIM), lambda p, r: (0, 0)),
                  pl.BlockSpec((1, 2 * HEAD_DIM), lambda p, r: (0, 0)),
                  pl.BlockSpec((None, nb, 2 * GRID_W), lambda p, r: (p, 0, 0)),
                  pl.BlockSpec((blk, 2 * HEAD_DIM), lambda p, r: (r, p))] + [ANY] * ncomm,
        out_specs=[pl.BlockSpec((3, t, 2 * HEAD_DIM), lambda p, r: (0, 0, p)),
                   pl.BlockSpec((None, nb, 2 * GRID_W), lambda p, r: (p, 0, 0)),
                   pair_vec, pair_vec] + [ANY] * ncomm,
        out_shape=[jax.ShapeDtypeStruct((4, t, aw), MXU_DTYPE), jax.ShapeDtypeStruct(bias_t.shape, F32),
                   jax.ShapeDtypeStruct((npair, 1, 2 * HEAD_DIM), F32),
                   jax.ShapeDtypeStruct((npair, 1, 2 * HEAD_DIM), F32)] + (comm.out_shape if comm is not None else []),
        scratch_shapes=[pltpu.VMEM((t, 2 * HEAD_DIM), MXU_DTYPE), pltpu.VMEM((t, 2 * HEAD_DIM), MXU_DTYPE),
                        pltpu.VMEM((t, 2 * HEAD_DIM), F32), pltpu.VMEM((t, 2 * HEAD_DIM), F32)]
        + (comm.scratch if comm is not None else []),
        compiler_params=_cparams(("arbitrary", "arbitrary")),
    )(z4, z4, z4, qg2, kg2, bias_t, dya, *comm_arrs)
    return res[:4], res[4:]


def _bmm_exact(name, a, b, dims, per_step=16):
    nb = a.shape[0]
    per = math.gcd(nb, per_step)
    shape = jax.eval_shape(lambda u, v: lax.dot_general(u, v, dims), a[0], b[0]).shape

    def body(a_ref, b_ref, o_ref):
        for e in range(per):
            o_ref[e] = lax.dot_general(a_ref[e], b_ref[e], dims, precision=HI, preferred_element_type=F32)

    blk = lambda arr: pl.BlockSpec((per,) + arr.shape[1:], lambda i: (i, 0, 0))
    out = jax.ShapeDtypeStruct((nb,) + shape, F32)
    return pl.pallas_call(body, name=name, grid=(nb // per,), in_specs=[blk(a), blk(b)], out_specs=blk(out),
                          out_shape=out, compiler_params=_cparams(("parallel",)))(a, b)


@jax.custom_vjp
def _contract_last(a, b):
    return _bmm_exact("s5_kern", a, b, NT_DIMS)


def _contract_last_fwd(a, b):
    return _contract_last(a, b), (a, b)


def _contract_last_bwd(res, g):
    a, b = res
    return _bmm_exact("s5_kern_da", g, b, NN_DIMS), _bmm_exact("s5_kern_db", g, a, TN_DIMS)


_contract_last.defvjp(_contract_last_fwd, _contract_last_bwd)


def _s5_mats(a_re, a_im, b_re, b_im, c_re, c_im, log_step, d_skip):
    nd, g, p = a_re.shape
    c = b_re.shape[-1]
    L = S5_CHUNK
    lr = jnp.minimum(a_re, -1e-4).transpose(1, 0, 2)
    li = a_im.transpose(1, 0, 2)
    dt = jnp.exp(log_step).T[..., None]
    n = jnp.arange(L + 1, dtype=F32)[None, :, None, None]
    mag = jnp.exp(n * (lr * dt)[:, None])
    ang = n * (li * dt)[:, None]
    pw_r, pw_i = mag * jnp.cos(ang), mag * jnp.sin(ang)
    den = lr * lr + li * li
    nr, ni = pw_r[:, 1] - 1.0, pw_i[:, 1]
    cr, ci = (nr * lr + ni * li) / den, (ni * lr - nr * li) / den
    bt_r, bt_i = b_re.transpose(1, 3, 0, 2), b_im.transpose(1, 3, 0, 2)
    bb_r = cr[:, None] * bt_r - ci[:, None] * bt_i
    bb_i = cr[:, None] * bt_i + ci[:, None] * bt_r
    ct_r, ct_i = c_re.transpose(1, 2, 0, 3), c_im.transpose(1, 2, 0, 3)

    def cols(x_re, x_im):
        return jnp.concatenate([x_re[..., 0, :], x_re[..., 1, :], x_im[..., 0, :], x_im[..., 1, :]], axis=-1)

    e_r = jnp.stack([pw_r[:, :L, 0][:, ::-1], pw_r[:, :L, 1]], axis=2)
    e_i = jnp.stack([pw_i[:, :L, 0][:, ::-1], pw_i[:, :L, 1]], axis=2)
    ws = (cols(e_r, e_r)[:, :, None] * cols(bb_r, bb_i)[:, None]
          + cols(e_i, e_i)[:, :, None] * cols(-bb_i, bb_r)[:, None]).reshape(g, L * c, 4 * p)
    f_r = jnp.stack([pw_r[:, 1:, 0], pw_r[:, 1:, 1][:, ::-1]], axis=2)
    f_i = jnp.stack([pw_i[:, 1:, 0], pw_i[:, 1:, 1][:, ::-1]], axis=2)
    wot = (cols(f_r, f_i)[:, :, None] * cols(ct_r, -ct_r)[:, None]
           + cols(f_i, f_r)[:, :, None] * cols(-ct_i, -ct_i)[:, None]).reshape(g, L * c, 4 * p)
    qr, qi = pw_r[:, None, :L], pw_i[:, None, :L]
    br, bi = bb_r[:, :, None], bb_i[:, :, None]
    kp_r, kp_i = qr * br - qi * bi, qr * bi + qi * br
    lhs = jnp.stack([jnp.concatenate([kp_r[..., d, :], -kp_i[..., d, :]], axis=-1) for d in range(2)])
    rhs = jnp.stack([jnp.concatenate([ct_r[:, :, d], ct_i[:, :, d]], axis=-1) for d in range(2)])
    kern = _contract_last(lhs.reshape(2 * g, c * L, 2 * p), rhs.reshape(2 * g, c, 2 * p)).reshape(2, g, c, L, c)
    skip = d_skip.reshape(g, c, 1, 1) * jnp.eye(c, dtype=F32)[None, :, None, :]
    by_offset = jnp.concatenate([kern[1][:, :, :0:-1], kern[0][:, :, :1] + kern[1][:, :, :1] + skip,
                                 kern[0][:, :, 1:]], axis=2).reshape(g, c, (2 * L - 1) * c)
    mt = jnp.stack([by_offset[:, :, (L - 1 - j) * c:(2 * L - 1 - j) * c] for j in range(L)], axis=1)
    mt = mt.reshape(g, L * c, L * c)
    lr16, li16 = pw_r[:, L], pw_i[:, L]
    fa = jnp.concatenate([lr16[:, 0], lr16[:, 1], lr16[:, 0], lr16[:, 1]], axis=-1)
    fb = jnp.concatenate([-li16[:, 0], -li16[:, 1], li16[:, 0], li16[:, 1]], axis=-1)
    return mt, ws, wot, fa, fb


def _gmm(name, a, b, contract, a_stacked=False, b_stacked=False, o_stacked=False, add=None, out_dtype=F32):
    w = S5_CHUNK * SSM_GROUP_CH
    g = (a.shape[0] if a_stacked else a.shape[1] // w)
    gpb = math.gcd(g, S5_GROUPS_PER_STEP)
    dn = {'nn': (((1,), (0,)), ((), ())), 'nt': (((1,), (1,)), ((), ())), 'tn': (((0,), (0,)), ((), ()))}[contract]

    def spec(arr, stacked):
        if stacked:
            return pl.BlockSpec((gpb,) + arr.shape[1:], lambda i: (i, 0, 0))
        return pl.BlockSpec((arr.shape[0], gpb * w), lambda i: (0, i))

    def take(ref, stacked, e):
        return ref[e] if stacked else ref[:, e * w:(e + 1) * w]

    m = (a.shape[1] if a_stacked else a.shape[0]) if contract != 'tn' else w
    n = w
    if o_stacked:
        o_spec = pl.BlockSpec((gpb, m, n), lambda i: (i, 0, 0))
        o_shape = (g, m, n)
    else:
        o_spec = pl.BlockSpec((m, gpb * n), lambda i: (0, i))
        o_shape = (m, g * n)
    has_add = add is not None

    def body(*refs):
        if has_add:
            a_ref, b_ref, add_ref, o_ref = refs
        else:
            a_ref, b_ref, o_ref = refs
        for e in range(gpb):
            r = lax.dot_general(take(a_ref, a_stacked, e).astype(S5_DTYPE), take(b_ref, b_stacked, e).astype(S5_DTYPE),
                                dn, precision=HI if S5_DTYPE == F32 else None, preferred_element_type=F32)
            if has_add:
                r = r + take(add_ref, o_stacked, e)
            if o_stacked:
                o_ref[e] = r.astype(o_ref.dtype)
            else:
                o_ref[:, e * w:(e + 1) * w] = r.astype(o_ref.dtype)

    in_specs = [spec(a, a_stacked), spec(b, b_stacked)] + ([o_spec] if has_add else [])
    return pl.pallas_call(
        body, name=name, grid=(g // gpb,), in_specs=in_specs, out_specs=o_spec,
        out_shape=jax.ShapeDtypeStruct(o_shape, out_dtype), compiler_params=_cparams(("parallel",)),
    )(*((a, b) + ((add,) if has_add else ())))


def _s5_scan(name, s, fa, fb, rev0, xin=None):
    nk, g, w = s.shape
    hw, qw = w // 2, w // 4
    gb = min(g, 16)
    with_acc = xin is not None

    def body(*refs):
        if with_acc:
            s_ref, a_ref, b_ref, x_ref, o_ref, pa_ref, pb_ref = refs
        else:
            s_ref, a_ref, b_ref, o_ref = refs
        fa_v, fb_v = a_ref[...], b_ref[...]
        dir0 = lax.broadcasted_iota(jnp.int32, (gb, w), 1) % hw < qw
        swap = lambda v: jnp.concatenate([v[:, hw:], v[:, :hw]], axis=1)

        def step(i, carry):
            x, pa, pb = carry
            k0 = (nk - 1 - i) if rev0 else i
            k1 = i if rev0 else (nk - 1 - i)
            for lo in (0, hw):
                o_ref[k0, :, lo:lo + qw] = x[:, lo:lo + qw]
                o_ref[k1, :, lo + qw:lo + hw] = x[:, lo + qw:lo + hw]
            if with_acc:
                xi = jnp.where(dir0, x_ref[k0], x_ref[k1])
                pa = pa + x * xi
                pb = pb + x * swap(xi)
            x = fa_v * x + fb_v * swap(x) + jnp.where(dir0, s_ref[k0], s_ref[k1])
            return x, pa, pb

        z = jnp.zeros((gb, w), F32)
        res = lax.fori_loop(0, nk, step, (z, z, z), unroll=2)
        if with_acc:
            pa_ref[...] = res[1]
            pb_ref[...] = res[2]

    seq = pl.BlockSpec((nk, gb, w), lambda i: (0, i, 0))
    vec = pl.BlockSpec((gb, w), lambda i: (i, 0))
    in_specs = [seq, vec, vec] + ([seq] if with_acc else [])
    out_specs = [seq] + ([vec, vec] if with_acc else [])
    out_shape = [jax.ShapeDtypeStruct((nk, g, w), F32)] + (
        [jax.ShapeDtypeStruct((g, w), F32)] * 2 if with_acc else [])
    return pl.pallas_call(
        body, name=name, grid=(g // gb,), in_specs=in_specs, out_specs=out_specs, out_shape=out_shape,
        compiler_params=_cparams(("parallel",)),
    )(*((s, fa, fb) + ((xin,) if with_acc else ())))


def _regroup(name, x, to_groups):
    if to_groups:
        t, sw = x.shape
    else:
        t, sw = x.shape[0] * S5_CHUNK, x.shape[1] // S5_CHUNK
    nk = t // S5_CHUNK
    wide = LANES * S5_CHUNK

    def place(tok):
        r = lax.broadcasted_iota(jnp.int32, (2 * LANES, wide), 0)
        col = lax.broadcasted_iota(jnp.int32, (2 * LANES, wide), 1)
        ch = r % LANES
        want = (ch // SSM_GROUP_CH) * (S5_CHUNK * SSM_GROUP_CH) + (tok + r // LANES) * SSM_GROUP_CH + ch % SSM_GROUP_CH
        return (col == want).astype(S5_DTYPE)

    def body(x_ref, o_ref):
        token = lambda tok: (pl.ds(tok, nk, stride=S5_CHUNK), slice(None))
        if to_groups:
            acc = jnp.zeros((nk, wide), F32)
            for tok in range(0, S5_CHUNK, 2):
                rows = jnp.concatenate([x_ref[token(tok)], x_ref[token(tok + 1)]], axis=1).astype(S5_DTYPE)
                acc = acc + lax.dot_general(rows, place(tok), NN_DIMS, preferred_element_type=F32)
            o_ref[...] = acc.astype(o_ref.dtype)
        else:
            xv = x_ref[...].astype(S5_DTYPE)
            for tok in range(0, S5_CHUNK, 2):
                both = lax.dot_general(xv, place(tok), NT_DIMS, preferred_element_type=F32).astype(o_ref.dtype)
                o_ref[token(tok)] = both[:, :LANES]
                o_ref[token(tok + 1)] = both[:, LANES:]

    tokens = pl.BlockSpec((t, LANES), lambda i: (0, i))
    groups = pl.BlockSpec((nk, wide), lambda i: (0, i))
    return pl.pallas_call(
        body, name=name, grid=(sw // LANES,), in_specs=[tokens if to_groups else groups],
        out_specs=groups if to_groups else tokens,
        out_shape=jax.ShapeDtypeStruct((nk, sw * S5_CHUNK), S5_DTYPE) if to_groups else jax.ShapeDtypeStruct((t, sw), F32),
        compiler_params=_cparams(("parallel",)),
    )(x)


def _s5_fwd(u2, mats):
    mt, ws, wot, fa, fb = mats
    nk = u2.shape[0]
    g = mt.shape[0]
    y_intra = _gmm("s5_intra", u2, mt, 'nn', b_stacked=True)
    s = _gmm("s5_chunk_state", u2, ws, 'nn', b_stacked=True)
    (xin,) = _s5_scan("s5_scan", s.reshape(nk, g, -1), fa, fb, False)
    xin = xin.reshape(nk, -1)
    return _gmm("s5_inter", xin, wot, 'nt', b_stacked=True, add=y_intra, out_dtype=S5_DTYPE), xin


def _s5_bwd(u2, xin, mats, dy2):
    mt, ws, wot, fa, fb = mats
    nk = u2.shape[0]
    g = mt.shape[0]
    dxin = _gmm("s5_dxin", dy2, wot, 'nn', b_stacked=True)
    ds, pa, pb = _s5_scan("s5_scan_adj", dxin.reshape(nk, g, -1), fa, -fb, True, xin=xin.reshape(nk, g, -1))
    ds = ds.reshape(nk, -1)
    du_a = _gmm("s5_du_intra", dy2, mt, 'nt', b_stacked=True)
    du2 = _gmm("s5_du_state", ds, ws, 'nt', b_stacked=True, add=du_a, out_dtype=S5_DTYPE)
    dmt = _gmm("s5_dmt", u2, dy2, 'tn', o_stacked=True)
    dws = _gmm("s5_dws", u2, ds, 'tn', o_stacked=True)
    dwot = _gmm("s5_dwot", dy2, xin, 'tn', o_stacked=True)
    return du2, (dmt, dws, dwot, pa, pb)


def _stacked(g4):
    return g4.reshape((N_CHIPS, -1, g4.shape[-1]))


def _local_step(x, target, w_in4, late, small, reduce_late=None, reduce_mid=None):
    t, d = x.shape
    aw = w_in4.shape[2]
    sw = aw
    nh = aw // HEAD_DIM
    row = lambda v: v.reshape(1, -1)
    g_mix, g_ffn = row(small['g_mix']), row(small['g_ffn'])
    g_oa, g_os, b_glu = row(small['g_out_attn']), row(small['g_out_ssm']), row(small['b_glu'])
    qg2 = jnp.tile(row(small['q_gain']), (1, 2))
    kg2 = jnp.tile(row(small['k_gain']), (1, 2))

    (h,) = _ew("rms_mix", lambda xv, g: _rms(xv, g)[0], [('r', x), ('c', g_mix)], [('r', d, MXU_DTYPE)])
    bias_t = _bias_table(small['rpb'])
    if late[0] == 'halves':
        under_in, under_attn, under_gate, under_up = late[1][:2], late[1][2:3], late[1][3:4], late[1][4:]
        z4, got_in = _mm("in_proj", h, w_in4, contract='nn', b_mode='b', o_mode='b',
                         comm=_GatherChips(under_in), comm_arrs=under_in)
        ya, got_attn = _attn_fwd(z4, qg2, kg2, bias_t, comm=_GatherChips(under_attn), comm_arrs=under_attn)
        w_glu, w_out = (_stacked(g4).reshape(-1, g4.shape[-1]) for g4 in got_in)
        w_gate4 = _stacked(got_attn[0])
    else:
        z4 = _mm("in_proj", h, w_in4, contract='nn', b_mode='b', o_mode='b')
        ya, _ = _attn_fwd(z4, qg2, kg2, bias_t)
        w_glu, w_out, w_gate4, w_up4, w_down4 = late[1]
    ffs = w_gate4.shape[2]
    s5_params = tuple(small[n] for n in ('ssm_a_re', 'ssm_a_im', 'ssm_b_re', 'ssm_b_im', 'ssm_c_re', 'ssm_c_im',
                                         'ssm_log_step', 'ssm_d'))
    mats, mats_vjp = jax.vjp(_s5_mats, *s5_params)
    mats = tuple(m.astype(S5_DTYPE) for m in mats[:3]) + mats[3:]
    u2 = _regroup("s5_group_u", z4[3], True)
    ypre2, xin = _s5_fwd(u2, mats)
    ypre = _regroup("s5_ungroup_y", ypre2, False)
    (yb,) = _ew("gelu", _gelu, [('r', ypre)], [('r', sw, MXU_DTYPE)])
    a_glu = _mm("glu_proj", yb, w_glu, contract='nn')

    def mix_out(yav, ypv, av, bg, goa, gos):
        ys = _gelu(ypv) * _sigmoid(av + bg)
        return jnp.concatenate([_rms(yav, goa)[0], _rms(ys, gos)[0]], axis=1)
    (ycat,) = _ew("mix_out", mix_out, [('r', ya), ('r', ypre), ('r', a_glu), ('c', b_glu), ('c', g_oa), ('c', g_os)],
                  [('r', aw + sw, MXU_DTYPE)])
    x1 = _mm("out_proj", ycat, w_out, contract='nn', add=x)
    (h2,) = _ew("rms_ffn", lambda xv, g: _rms(xv, g)[0], [('r', x1), ('c', g_ffn)], [('r', d, MXU_DTYPE)])
    if late[0] == 'halves':
        gate4, got_gate = _mm("ffn_gate", h2, w_gate4, contract='nn', b_mode='b', o_mode='b', tn=ffs,
                              out_dtype=MXU_DTYPE, comm=_GatherChips(under_gate), comm_arrs=under_gate)
        w_up4 = _stacked(got_gate[0])
        up4, got_up = _mm("ffn_up", h2, w_up4, contract='nn', b_mode='b', o_mode='b', tn=ffs,
                          out_dtype=MXU_DTYPE, comm=_GatherChips(under_up), comm_arrs=under_up)
        w_down4 = _stacked(got_up[0])
    else:
        gate4 = _mm("ffn_gate", h2, w_gate4, contract='nn', b_mode='b', o_mode='b', tn=ffs, out_dtype=MXU_DTYPE)
        up4 = _mm("ffn_up", h2, w_up4, contract='nn', b_mode='b', o_mode='b', tn=ffs, out_dtype=MXU_DTYPE)
    gate_f, up_f = gate4.reshape(4 * t, ffs), up4.reshape(4 * t, ffs)
    (act,) = _ew("swiglu", lambda gv, uv: gv * _sigmoid(gv) * uv, [('r', gate_f), ('r', up_f)],
                 [('r', ffs, MXU_DTYPE)], tr=512)
    act4 = act.reshape(4, t, ffs)
    x2 = _mm("ffn_down", act4, w_down4, contract='nn', a_mode='c', b_mode='c', add=x1, tk=ffs)

    def loss_fn(xv, tv):
        diff = xv - tv
        return diff * (1.0 / d), diff * (1.0 / d), diff * diff
    dx2, dx2_b, sq = _ew("loss", loss_fn, [('r', x2), ('r', target)], [('r', d, F32), ('r', d, MXU_DTYPE), ('a', d)])

    dact4 = _mm("ffn_down_dx", dx2_b, w_down4, contract='nt', b_mode='b', o_mode='b', tn=ffs, out_dtype=MXU_DTYPE)
    d_w_down4 = _mm("ffn_down_dw", act4, dx2_b, contract='tn', a_mode='b', o_mode='b', tm=ffs, out_dtype=DW_DTYPE)

    def swiglu_bwd(dav, gv, uv):
        s = _sigmoid(gv)
        return dav * uv * s * (1.0 + gv * (1.0 - s)), dav * gv * s
    dgate, dup = _ew("swiglu_bwd", swiglu_bwd, [('r', dact4.reshape(4 * t, ffs)), ('r', gate_f), ('r', up_f)],
                     [('r', ffs, MXU_DTYPE), ('r', ffs, MXU_DTYPE)], tr=512)
    dgate4, dup4 = dgate.reshape(4, t, ffs), dup.reshape(4, t, ffs)
    dh2 = _mm("ffn_gate_dx", dgate4, w_gate4, contract='nt', a_mode='c', b_mode='c', tk=ffs, tn=2048)
    dh2 = _mm("ffn_up_dx", dup4, w_up4, contract='nt', a_mode='c', b_mode='c', add=dh2, tk=ffs)
    d_w_gate4 = _mm("ffn_gate_dw", h2, dgate4, contract='tn', b_mode='b', o_mode='b', tn=ffs, out_dtype=DW_DTYPE)
    d_w_up4 = _mm("ffn_up_dw", h2, dup4, contract='tn', b_mode='b', o_mode='b', tn=ffs, out_dtype=DW_DTYPE)

    def rms_res_bwd(xv, g, dyv, resv):
        dx, dg = _rms_bwd(xv, g, dyv)
        return resv + dx, dg
    dx1, d_g_ffn = _ew("rms_ffn_bwd", rms_res_bwd, [('r', x1), ('c', g_ffn), ('r', dh2), ('r', dx2)],
                       [('r', d, F32), ('a', d)])

    dycat = _mm("out_proj_dx", dx1, w_out, contract='nt', out_dtype=MXU_DTYPE)
    d_w_out = _mm("out_proj_dw", ycat, dx1, contract='tn', out_dtype=DW_DTYPE)

    def mix_out_bwd(yav, ypv, av, bg, goa, gos, dca, dcs):
        dya, dgoa = _rms_bwd(yav, goa, dca)
        y = _gelu(ypv)
        s = _sigmoid(av + bg)
        dys, dgos = _rms_bwd(y * s, gos, dcs)
        da = dys * y * s * (1.0 - s)
        return dya, da, dys * s, dgoa, dgos, da
    dya, da, dy_direct, d_g_oa, d_g_os, d_b_glu = _ew(
        "mix_out_bwd", mix_out_bwd,
        [('r', ya), ('r', ypre), ('r', a_glu), ('c', b_glu), ('c', g_oa), ('c', g_os),
         ('r', dycat, 0, aw), ('r', dycat, 1, sw)],
        [('r', aw, F32), ('r', sw, MXU_DTYPE), ('r', sw, F32), ('a', aw), ('a', sw), ('a', sw)])
    dy = _mm("glu_proj_dx", da, w_glu, contract='nt', add=dy_direct)
    d_w_glu = _mm("glu_proj_dw", yb, da, contract='tn', out_dtype=DW_DTYPE)
    (dypre,) = _ew("gelu_bwd", lambda dyv, ypv: dyv * _gelu_grad(ypv), [('r', dy), ('r', ypre)],
                   [('r', sw, F32)])

    du2, dmats = _s5_bwd(u2, xin, mats, _regroup("s5_group_dy", dypre, True))
    d_s5 = mats_vjp(dmats)
    du = _regroup("s5_ungroup_du", du2, False)
    d_late = (d_w_glu, d_w_out, d_w_gate4, d_w_up4, d_w_down4)
    if reduce_late is not None:
        sums = reduce_late(d_late)
        (dz4, dbias_t, dqg, dkg), scattered = _attn_bwd(z4, qg2, kg2, bias_t, dya, comm=_ScatterChips(sums),
                                                       comm_arrs=sums)
        d_late = (sums, list(scattered))
    else:
        (dz4, dbias_t, dqg, dkg), _ = _attn_bwd(z4, qg2, kg2, bias_t, dya)
    d_rpb = _bias_table_grad(dbias_t)
    fold = lambda v: v.reshape(-1, 2, HEAD_DIM).sum(axis=(0, 1))
    dz4 = dz4.at[3].set(du.astype(dz4.dtype))

    d_w_in4 = _mm("in_proj_dw", h, dz4, contract='tn', b_mode='b', o_mode='b', out_dtype=DW_DTYPE)
    d_mid = [d_w_in4] + [d_s5[i].reshape(-1, LANES) for i in (2, 3, 4, 5)]
    if reduce_late is not None:
        sums = reduce_mid(d_mid)
        dh, scattered = _mm("in_proj_dx", dz4, w_in4, contract='nt', a_mode='c', b_mode='c',
                            comm=_ScatterChips(sums), comm_arrs=sums)
        d_mid = (sums, list(scattered))
    else:
        dh = _mm("in_proj_dx", dz4, w_in4, contract='nt', a_mode='c', b_mode='c')
    dx, d_g_mix = _ew("rms_mix_bwd", rms_res_bwd, [('r', x), ('c', g_mix), ('r', dh), ('r', dx1)],
                      [('r', d, F32), ('a', d)])

    colsum = lambda v: v.sum(axis=0)
    d_small = {
        'g_mix': colsum(d_g_mix), 'q_gain': fold(dqg), 'k_gain': fold(dkg), 'rpb': d_rpb,
        'ssm_a_re': d_s5[0], 'ssm_a_im': d_s5[1], 'ssm_b_re': d_s5[2], 'ssm_b_im': d_s5[3],
        'ssm_c_re': d_s5[4], 'ssm_c_im': d_s5[5], 'ssm_log_step': d_s5[6], 'ssm_d': d_s5[7],
        'b_glu': colsum(d_b_glu), 'g_out_attn': colsum(d_g_oa), 'g_out_ssm': colsum(d_g_os), 'g_ffn': colsum(d_g_ffn),
    }
    return jnp.sum(sq), dx, d_late, d_mid, d_small


ANY = pl.BlockSpec(memory_space=pl.ANY)


def _place():
    x, y, c = lax.axis_index("x"), lax.axis_index("y"), lax.axis_index("c")
    other_chips = [(1 - x, y), (x, 1 - y), (1 - x, 1 - y)]
    return x, y, c, 2 * x + y, (x, y, 1 - c), other_chips


class _GatherChips:
    KINDS = 7

    def __init__(self, arrs):
        self.n = len(arrs)
        self.out_shape = [jax.ShapeDtypeStruct((N_CHIPS,) + a.shape, a.dtype) for a in arrs]
        self.scratch = [pltpu.SemaphoreType.DMA((self.n, self.KINDS)), pltpu.SemaphoreType.DMA((self.n, self.KINDS))]

    def _copies(self, ins, outs, send_sems, recv_sems):
        x, y, c, me, sibling, chips = _place()

        def remote(a, k, src, dst, to):
            return lambda: pltpu.make_async_remote_copy(src_ref=src, dst_ref=dst, send_sem=send_sems.at[a, k],
                                                        recv_sem=recv_sems.at[a, k], device_id=to, device_id_type=MESH)
        own, out, landed, passed, theirs = [], [], [], [], []
        for a in range(self.n):
            own.append(remote(a, 6, ins[a], outs[a].at[me], sibling))
            for j, (px, py) in enumerate(chips):
                there, here = outs[a].at[2 * px + py, c], outs[a].at[2 * px + py, 1 - c]
                out.append(remote(a, j, ins[a].at[c], outs[a].at[me, c], (px, py, c)))
                landed.append(remote(a, j, there, there, (px, py, c)))
                passed.append(remote(a, 3 + j, there, there, sibling))
                theirs.append(remote(a, 3 + j, here, here, sibling))
        return own, out, landed, passed, theirs

    def start(self, ins, outs, send_sems, recv_sems):
        own, out, _, _, _ = self._copies(ins, outs, send_sems, recv_sems)
        for make in own + out:
            make().start()

    def finish(self, ins, outs, send_sems, recv_sems):
        own, out, landed, passed, theirs = self._copies(ins, outs, send_sems, recv_sems)
        for arrived, onward in zip(landed, passed):
            arrived().wait_recv()
            onward().start()
        for make in theirs + own:
            make().wait_recv()
        for make in own + out + passed:
            make().wait_send()


class _ScatterChips:
    def __init__(self, sums):
        self.n = len(sums)
        self.out_shape = [jax.ShapeDtypeStruct(s.shape, s.dtype) for s in sums]
        self.scratch = [pltpu.SemaphoreType.DMA((self.n, 3)), pltpu.SemaphoreType.DMA((self.n, 3))]

    def _copies(self, ins, outs, send_sems, recv_sems):
        x, y, c, me, sibling, chips = _place()
        out, landed = [], []

        def remote(a, j, src, dst, to):
            return lambda: pltpu.make_async_remote_copy(src_ref=src, dst_ref=dst, send_sem=send_sems.at[a, j],
                                                        recv_sem=recv_sems.at[a, j], device_id=to, device_id_type=MESH)
        for a in range(self.n):
            for j, (px, py) in enumerate(chips):
                slot = outs[a].at[2 * px + py]
                out.append(remote(a, j, ins[a].at[2 * px + py], outs[a].at[me], (px, py, c)))
                landed.append(remote(a, j, slot, slot, (px, py, c)))
        return out, landed

    def start(self, ins, outs, send_sems, recv_sems):
        for make in self._copies(ins, outs, send_sems, recv_sems)[0]:
            make().start()

    def finish(self, ins, outs, send_sems, recv_sems):
        out, landed = self._copies(ins, outs, send_sems, recv_sems)
        for make in landed:
            make().wait_recv()
        for make in out:
            make().wait_send()


def _comm_call(name, comm, arrs):
    n = comm.n

    def body(*refs):
        parts = (refs[:n], refs[n:2 * n]) + tuple(refs[2 * n:])
        comm.start(*parts)
        comm.finish(*parts)

    return pl.pallas_call(body, name=name, in_specs=[ANY] * n, out_specs=[ANY] * n, out_shape=comm.out_shape,
                          scratch_shapes=comm.scratch)(*arrs)


def _gather_chips(name, arrs):
    return _comm_call(name, _GatherChips(arrs), arrs)


def _swap_halves(name, parts):
    n = len(parts)

    def body(*refs):
        ins, outs = refs[:n], refs[n:2 * n]
        send_sems, recv_sems = refs[2 * n:]
        x, y, c, me, sibling, chips = _place()
        cps = []
        for a in range(n):
            cp = pltpu.make_async_remote_copy(src_ref=ins[a].at[:, 1 - c], dst_ref=outs[a], send_sem=send_sems.at[a],
                                              recv_sem=recv_sems.at[a], device_id=sibling, device_id_type=MESH)
            cp.start()
            cps.append(cp)
        for cp in cps:
            cp.wait()

    return pl.pallas_call(
        body, name=name, in_specs=[ANY] * n, out_specs=[ANY] * n,
        out_shape=[jax.ShapeDtypeStruct((N_CHIPS,) + p.shape[2:], p.dtype) for p in parts],
        scratch_shapes=[pltpu.SemaphoreType.DMA((n,)), pltpu.SemaphoreType.DMA((n,))],
    )(*parts)


def _scatter_chips(name, sums):
    return _comm_call(name, _ScatterChips(sums), sums)


def _swap_reduced(name, halves):
    n = len(halves)

    def body(*refs):
        ins, outs = refs[:n], refs[n:2 * n]
        send_sems, recv_sems = refs[2 * n:]
        x, y, c, me, sibling, chips = _place()
        cps = []
        for a in range(n):
            cp = pltpu.make_async_remote_copy(src_ref=ins[a], dst_ref=outs[a], send_sem=send_sems.at[a],
                                              recv_sem=recv_sems.at[a], device_id=sibling, device_id_type=MESH)
            cp.start()
            cps.append(cp)
        for cp in cps:
            cp.wait()

    return pl.pallas_call(
        body, name=name, in_specs=[ANY] * n, out_specs=[ANY] * n,
        out_shape=[jax.ShapeDtypeStruct(h.shape, h.dtype) for h in halves],
        scratch_shapes=[pltpu.SemaphoreType.DMA((n,)), pltpu.SemaphoreType.DMA((n,))],
    )(*halves)


def _row_tile(r, want=256):
    t = (min(r, want) // SUBLANES) * SUBLANES
    while r % t:
        t -= SUBLANES
    return t


def _add_own_half(name, part, got, c, out_dtype):
    _, _, r, cols = part.shape
    tr = _row_tile(r)

    def body(c_ref, p_ref, g_ref, o_ref):
        o_ref[...] = (p_ref[...].astype(F32) + g_ref[...].astype(F32)).astype(o_ref.dtype)

    return pl.pallas_call(
        body, name=name,
        grid_spec=pltpu.PrefetchScalarGridSpec(
            num_scalar_prefetch=1, grid=(N_CHIPS, r // tr),
            in_specs=[pl.BlockSpec((None, None, tr, cols), lambda s, i, c_ref: (s, c_ref[0], i, 0)),
                      pl.BlockSpec((None, tr, cols), lambda s, i, c_ref: (s, i, 0))],
            out_specs=pl.BlockSpec((None, tr, cols), lambda s, i, c_ref: (s, i, 0))),
        out_shape=jax.ShapeDtypeStruct(got.shape, out_dtype),
        compiler_params=_cparams(("parallel", "parallel")),
    )(c.reshape(1).astype(jnp.int32), part, got)


def _sum_chips(name, got, own, me):
    _, r, cols = got.shape
    tr = _row_tile(r)

    def body(me_ref, r0, r1, r2, r3, own_ref, o_ref):
        pick = lambda s, ref: jnp.where(me_ref[0] == s, own_ref[...], ref[...]).astype(F32)
        o_ref[...] = ((pick(0, r0) + pick(1, r1)) + pick(2, r2)) + pick(3, r3)

    def slot(s):
        return pl.BlockSpec((None, tr, cols),
                            lambda i, me_ref: (jnp.where(me_ref[0] == s, (s + 1) % N_CHIPS, s), i, 0))

    return pl.pallas_call(
        body, name=name,
        grid_spec=pltpu.PrefetchScalarGridSpec(
            num_scalar_prefetch=1, grid=(r // tr,),
            in_specs=[slot(s) for s in range(N_CHIPS)]
            + [pl.BlockSpec((None, tr, cols), lambda i, me_ref: (me_ref[0], i, 0))],
            out_specs=pl.BlockSpec((tr, cols), lambda i, me_ref: (i, 0))),
        out_shape=jax.ShapeDtypeStruct((r, cols), F32),
        compiler_params=_cparams(("parallel",)),
    )(me.reshape(1).astype(jnp.int32), got, got, got, got, own)


def _adamw_math(wv, gv, mv, vv):
    mv = ADAM_B1 * mv + (1.0 - ADAM_B1) * gv
    vv = ADAM_B2 * vv + (1.0 - ADAM_B2) * (gv * gv)
    m_hat = mv / (1.0 - ADAM_B1 ** ADAM_STEP)
    v_hat = vv / (1.0 - ADAM_B2 ** ADAM_STEP)
    return -ADAM_LR * (m_hat / (jnp.sqrt(v_hat) + ADAM_EPS) + ADAM_WD * wv), mv, vv


def _adamw(name, w, g, m, v):
    cols = w.shape[1]
    return _ew(name, _adamw_math, [('r', w), ('r', g), ('r', m), ('r', v)], [('r', cols, F32)] * 3,
               tr=_row_tile(w.shape[0], max(LANES, LANES * LANES // cols)))


def _adamw_halves(name, w, mine, theirs, m, v, c):
    r, cols = mine.shape
    tr = _row_tile(r, 128)
    nb = r // tr

    def body(c_ref, w_ref, a_ref, b_ref, m_ref, v_ref, g_out, d_out, m_out, v_out):
        g = jnp.where(pl.program_id(0) == c_ref[0], a_ref[...], b_ref[...])
        g_out[...] = g
        d_out[...], m_out[...], v_out[...] = _adamw_math(w_ref[...], g, m_ref[...], v_ref[...])

    whole = pl.BlockSpec((tr, cols), lambda h, i, c_ref: (h * nb + i, 0))
    half = pl.BlockSpec((tr, cols), lambda h, i, c_ref: (i, 0))
    return pl.pallas_call(
        body, name=name,
        grid_spec=pltpu.PrefetchScalarGridSpec(
            num_scalar_prefetch=1, grid=(2, nb),
            in_specs=[whole, half, half, whole, whole], out_specs=[whole] * 4),
        out_shape=[jax.ShapeDtypeStruct(w.shape, F32)] * 4,
        compiler_params=_cparams(("parallel", "parallel")),
    )(c.reshape(1).astype(jnp.int32), w, mine, theirs, m, v)


SMALL_ROWS_ALIGN = 2 * N_CHIPS * SUBLANES


MEDIUM_NAMES = ['ssm_b_re', 'ssm_b_im', 'ssm_c_re', 'ssm_c_im']
PACKED_NAMES = [n for n in SMALL_NAMES if n not in MEDIUM_NAMES]


def _pack_small(d):
    flat = jnp.concatenate([d[n].reshape(-1).astype(F32) for n in PACKED_NAMES])
    rows = -(-flat.shape[0] // (LANES * SMALL_ROWS_ALIGN)) * SMALL_ROWS_ALIGN
    return jnp.pad(flat, (0, rows * LANES - flat.shape[0])).reshape(rows, LANES)


def _unpack_small(packed, like):
    flat = packed.reshape(-1)
    out, off = {}, 0
    for n in PACKED_NAMES:
        size = like[n].size
        out[n] = flat[off:off + size].reshape(like[n].shape)
        off += size
    return out


def kernel(x, g_mix, w_in, q_gain, k_gain, rpb, ssm_a_re, ssm_a_im, ssm_b_re, ssm_b_im, ssm_c_re, ssm_c_im, ssm_log_step, ssm_d, w_glu, b_glu, g_out_attn, g_out_ssm, w_out, g_ffn, w_ffn_gate, w_ffn_up, w_ffn_down, loss_target, m_g_mix, m_w_in, m_q_gain, m_k_gain, m_rpb, m_ssm_a_re, m_ssm_a_im, m_ssm_b_re, m_ssm_b_im, m_ssm_c_re, m_ssm_c_im, m_ssm_log_step, m_ssm_d, m_w_glu, m_b_glu, m_g_out_attn, m_g_out_ssm, m_w_out, m_g_ffn, m_w_ffn_gate, m_w_ffn_up, m_w_ffn_down, v_g_mix, v_w_in, v_q_gain, v_k_gain, v_rpb, v_ssm_a_re, v_ssm_a_im, v_ssm_b_re, v_ssm_b_im, v_ssm_c_re, v_ssm_c_im, v_ssm_log_step, v_ssm_d, v_w_glu, v_b_glu, v_g_out_attn, v_g_out_ssm, v_w_out, v_g_ffn, v_w_ffn_gate, v_w_ffn_up, v_w_ffn_down):
    given = dict(locals())
    w = {n: given[n][0] for n in WEIGHT_NAMES}
    mom = {n: given["m_" + n][0] for n in WEIGHT_NAMES}
    var = {n: given["v_" + n][0] for n in WEIGHT_NAMES}
    d = x.shape[-1]
    c = lax.axis_index("c")

    halves = {n: w[n].astype(MXU_DTYPE).reshape((2, w[n].shape[0] // 2, w[n].shape[1])) for n in BIG_NAMES}
    (w_in4,) = _gather_chips("gather_w_in", [halves['w_in']])
    w_in4 = w_in4.reshape((N_CHIPS, -1, w_in4.shape[-1]))

    def chip_sums(tag, grads, payload):
        parts = [g.reshape((N_CHIPS, 2, -1, g.shape[-1])) for g in grads]
        got = _swap_halves("reduce_swap_halves_" + tag, parts)
        return [_add_own_half("reduce_add_%s_%d" % (tag, a), p, gt, c, dt)
                for a, (p, gt, dt) in enumerate(zip(parts, got, payload))]

    reduce_late = lambda grads: chip_sums("late", grads, [GRAD_PAYLOAD_DTYPE] * len(grads))
    reduce_mid = lambda grads: chip_sums("mid", grads, [GRAD_PAYLOAD_DTYPE] + [F32] * (len(grads) - 1))
    sq, dx, (sums_late, got_late), (sums_mid, got_mid), d_small = _local_step(
        x[0], loss_target[0], w_in4, ('halves', [halves[n] for n in LATE_NAMES]), {n: w[n] for n in SMALL_NAMES},
        reduce_late, reduce_mid)
    loss = lax.psum(0.5 * sq / d, ("x", "y", "c"))

    nbig = len(BIG_NAMES)
    sums_tiny = chip_sums("tiny", [_pack_small(d_small)], [F32])
    got_tiny = list(_scatter_chips("reduce_scatter_tiny", sums_tiny))
    sums = sums_mid[:1] + sums_late + sums_mid[1:] + sums_tiny
    got = got_mid[:1] + got_late + got_mid[1:] + got_tiny
    me = 2 * lax.axis_index("x") + lax.axis_index("y")
    mine = [_sum_chips("reduce_sum_%d" % a, gt, sm_, me) for a, (gt, sm_) in enumerate(zip(got, sums))]
    theirs = _swap_reduced("reduce_swap_reduced", mine)
    in_order = lambda a: jnp.where(c == 0, jnp.stack([mine[a], theirs[a]]), jnp.stack([theirs[a], mine[a]]))
    repl = _gather_chips("gather_small", [in_order(a) for a in range(nbig, len(mine))])
    repl = [r.reshape(-1, LANES) for r in repl]
    like = {n: w[n] for n in SMALL_NAMES}
    grad_small = _unpack_small(repl[-1], like)
    grad_small.update({n: r.reshape(w[n].shape) for n, r in zip(MEDIUM_NAMES, repl)})

    grad_big, delta, new_m, new_v = {}, {}, {}, {}
    for a, n in enumerate(BIG_NAMES):
        grad_big[n], delta[n], new_m[n], new_v[n] = _adamw_halves("adamw_%d" % a, w[n], mine[a], theirs[a],
                                                                  mom[n], var[n], c)
    for n in MEDIUM_NAMES:
        flat = lambda t: t.reshape(-1, w[n].shape[-1])
        res = _adamw("adamw_" + n, flat(w[n]), flat(grad_small[n]), flat(mom[n]), flat(var[n]))
        delta[n], new_m[n], new_v[n] = (t.reshape(w[n].shape) for t in res)
    sd, sm, sv = _adamw("adamw_small", _pack_small(w), repl[-1], _pack_small(mom), _pack_small(var))
    delta.update(_unpack_small(sd, like))
    new_m.update(_unpack_small(sm, like))
    new_v.update(_unpack_small(sv, like))
    grads = {**grad_big, **grad_small}
    lead = lambda t: t[None]
    return (loss, dx[None], *[lead(grads[n]) for n in WEIGHT_NAMES], *[lead(delta[n]) for n in WEIGHT_NAMES],
            *[lead(new_m[n]) for n in WEIGHT_NAMES], *[lead(new_v[n]) for n in WEIGHT_NAMES])
```

```python
import functools
import math

import jax
import jax.numpy as jnp
from jax import lax
from jax.experimental import pallas as pl
from jax.experimental.pallas import tpu as pltpu

F32 = jnp.float32
BF16 = jnp.bfloat16
MXU_DTYPE = BF16
GRAD_PAYLOAD_DTYPE = BF16
DW_DTYPE = BF16
S5_DTYPE = BF16
HI = lax.Precision.HIGHEST
VMEM_LIMIT_V7X = 56 * 1024 * 1024
LANES = 128
SUBLANES = 8

GRID_W = 64
WIN_H = 8
WIN_W = 16
HEAD_DIM = 64
SSM_GROUP_CH = 16
SSM_STATE = 64
S5_CHUNK = 16
S5_GROUPS_PER_STEP = 16
RMS_EPS = 1e-6
NEG_INF = -1e30
N_CHIPS = 4
MESH = pl.DeviceIdType.MESH

ADAM_LR = 0.001
ADAM_B1 = 0.9
ADAM_B2 = 0.999
ADAM_EPS = 1e-08
ADAM_WD = 0.01
ADAM_STEP = 10

WEIGHT_NAMES = ['g_mix', 'w_in', 'q_gain', 'k_gain', 'rpb', 'ssm_a_re', 'ssm_a_im', 'ssm_b_re', 'ssm_b_im',
                'ssm_c_re', 'ssm_c_im', 'ssm_log_step', 'ssm_d', 'w_glu', 'b_glu', 'g_out_attn', 'g_out_ssm',
                'w_out', 'g_ffn', 'w_ffn_gate', 'w_ffn_up', 'w_ffn_down']
BIG_NAMES = ['w_in', 'w_glu', 'w_out', 'w_ffn_gate', 'w_ffn_up', 'w_ffn_down']
LATE_NAMES = BIG_NAMES[1:]
SMALL_NAMES = [n for n in WEIGHT_NAMES if n not in BIG_NAMES]


def _cparams(sem):
    return pltpu.CompilerParams(dimension_semantics=sem, vmem_limit_bytes=VMEM_LIMIT_V7X)


def _tile(n, want):
    if n <= want:
        return n
    t = (want // LANES) * LANES
    while t >= LANES:
        if n % t == 0:
            return t
        t -= LANES
    return n


def _mm(name, a, b, *, contract, a_mode='2', b_mode='2', o_mode='2', out_dtype=F32, add=None, exact=False,
        tm=1024, tn=1024, tk=2048, comm=None, comm_arrs=()):
    dn = {'nn': (((1,), (0,)), ((), ())), 'nt': (((1,), (1,)), ((), ())), 'tn': (((0,), (0,)), ((), ()))}[contract]
    ar, ac = a.shape[-2:]
    br, bc = b.shape[-2:]
    m, kdim = (ar, ac) if contract != 'tn' else (ac, ar)
    n = bc if contract != 'nt' else br
    assert kdim == (br if contract != 'nt' else bc), (name, a.shape, b.shape)
    nbatch = 1
    for arr, mode in ((a, a_mode), (b, b_mode)):
        if mode == 'b':
            nbatch = arr.shape[0]
    nstack = 1
    for arr, mode in ((a, a_mode), (b, b_mode)):
        if mode == 'c':
            nstack = arr.shape[0]
    tm, tn, tk = _tile(m, tm), _tile(n, tn), _tile(kdim, tk)
    nkin = kdim // tk
    nk = nstack * nkin
    grid = (nbatch, m // tm, n // tn, nk)

    def spec(mode, block, rc):
        def imap(s, i, j, kk):
            r, c = rc(i, j, kk % nkin)
            if mode == '2':
                return (r, c)
            return (s if mode == 'b' else kk // nkin, r, c)
        return pl.BlockSpec(block if mode == '2' else (None,) + block, imap)

    a_spec = spec(a_mode, (tm, tk) if contract != 'tn' else (tk, tm),
                  (lambda i, j, k: (i, k)) if contract != 'tn' else (lambda i, j, k: (k, i)))
    b_spec = spec(b_mode, (tk, tn) if contract != 'nt' else (tn, tk),
                  (lambda i, j, k: (k, j)) if contract != 'nt' else (lambda i, j, k: (j, k)))
    o_spec = spec(o_mode, (tm, tn), lambda i, j, k: (i, j))
    out_shape = (m, n) if o_mode == '2' else (nbatch, m, n)
    has_add = add is not None

    def product(a_ref, b_ref):
        if exact:
            return lax.dot_general(a_ref[...].astype(F32), b_ref[...].astype(F32), dn, precision=HI,
                                   preferred_element_type=F32)
        return lax.dot_general(a_ref[...].astype(MXU_DTYPE), b_ref[...].astype(MXU_DTYPE), dn,
                               preferred_element_type=F32)

    ncomm = len(comm_arrs)
    nacc = int(nk > 1)

    def body(*refs):
        a_ref, b_ref = refs[:2]
        add_ref = refs[2] if has_add else None
        c_ins = refs[2 + has_add:2 + has_add + ncomm]
        o_ref = refs[2 + has_add + ncomm]
        c_outs = refs[3 + has_add + ncomm:3 + has_add + 2 * ncomm]
        sems = refs[3 + has_add + 2 * ncomm + nacc:]
        ids = [pl.program_id(ax) for ax in range(4)]
        if comm is not None:
            @pl.when((ids[0] == 0) & (ids[1] == 0) & (ids[2] == 0) & (ids[3] == 0))
            def _():
                comm.start(c_ins, c_outs, *sems)

        def write(r):
            if has_add:
                r = r + add_ref[...].astype(F32)
            o_ref[...] = r.astype(o_ref.dtype)

        if nk == 1:
            write(product(a_ref, b_ref))
        else:
            acc_ref = refs[3 + has_add + 2 * ncomm]

            @pl.when(ids[3] == 0)
            def _():
                acc_ref[...] = jnp.zeros_like(acc_ref)

            acc_ref[...] += product(a_ref, b_ref)

            @pl.when(ids[3] == nk - 1)
            def _():
                write(acc_ref[...])

        if comm is not None:
            @pl.when((ids[0] == grid[0] - 1) & (ids[1] == grid[1] - 1) & (ids[2] == grid[2] - 1) & (ids[3] == nk - 1))
            def _():
                comm.finish(c_ins, c_outs, *sems)

    in_specs = [a_spec, b_spec] + ([o_spec] if has_add else []) + [ANY] * ncomm
    args = (a, b) + ((add,) if has_add else ()) + tuple(comm_arrs)
    res = pl.pallas_call(
        body, name=name, grid=grid, in_specs=in_specs, out_specs=[o_spec] + [ANY] * ncomm,
        out_shape=[jax.ShapeDtypeStruct(out_shape, out_dtype)] + (comm.out_shape if comm is not None else []),
        scratch_shapes=([pltpu.VMEM((tm, tn), F32)] if nk > 1 else []) + (comm.scratch if comm is not None else []),
        compiler_params=_cparams(("parallel", "parallel", "parallel", "arbitrary") if comm is None
                                 else ("arbitrary",) * 4),
    )(*args)
    return res[0] if comm is None else (res[0], res[1:])


def _ew(name, fn, ins, outs, tr=256):
    rows = next(x[1].shape[0] for x in ins if x[0] == 'r')
    tr = min(tr, rows)
    assert rows % tr == 0 and tr % SUBLANES == 0, (name, rows, tr)
    in_specs, args = [], []
    for x in ins:
        if x[0] == 'r' and len(x) == 2:
            in_specs.append(pl.BlockSpec((tr, x[1].shape[1]), lambda i: (i, 0)))
        elif x[0] == 'r':
            in_specs.append(pl.BlockSpec((tr, x[3]), functools.partial(lambda cb, i: (i, cb), x[2])))
        else:
            in_specs.append(pl.BlockSpec(x[1].shape, lambda i: (0, 0)))
        args.append(x[1])
    out_specs, out_shapes = [], []
    for o in outs:
        if o[0] == 'r':
            out_specs.append(pl.BlockSpec((tr, o[1]), lambda i: (i, 0)))
            out_shapes.append(jax.ShapeDtypeStruct((rows, o[1]), o[2]))
        else:
            out_specs.append(pl.BlockSpec((SUBLANES, o[1]), lambda i: (0, 0)))
            out_shapes.append(jax.ShapeDtypeStruct((SUBLANES, o[1]), F32))
    nin = len(ins)
    has_acc = any(o[0] == 'a' for o in outs)

    def body(*refs):
        vals = fn(*[r[...].astype(F32) for r in refs[:nin]])
        if not isinstance(vals, (tuple, list)):
            vals = (vals,)
        i = pl.program_id(0)
        for o, ref, v in zip(outs, refs[nin:], vals):
            if o[0] == 'r':
                ref[...] = v.astype(ref.dtype)
            else:
                part = v.astype(F32).reshape(tr // SUBLANES, SUBLANES, o[1]).sum(axis=0)

                @pl.when(i == 0)
                def _(ref=ref, part=part):
                    ref[...] = part

                @pl.when(i > 0)
                def _(ref=ref, part=part):
                    ref[...] += part

    res = pl.pallas_call(
        body, name=name, grid=(rows // tr,), in_specs=in_specs, out_specs=out_specs, out_shape=out_shapes,
        compiler_params=_cparams(("arbitrary",) if has_acc else ("parallel",)),
    )(*args)
    return res


def _rms(x, g):
    r = lax.rsqrt(jnp.mean(x * x, axis=-1, keepdims=True) + RMS_EPS)
    xr = x * r
    return xr * g, xr


def _rms_bwd(x, g, dy):
    r = lax.rsqrt(jnp.mean(x * x, axis=-1, keepdims=True) + RMS_EPS)
    xr = x * r
    gdy = g * dy
    dx = r * (gdy - xr * jnp.mean(xr * gdy, axis=-1, keepdims=True))
    return dx, dy * xr


def _sigmoid(x):
    return 0.5 * (jnp.tanh(0.5 * x) + 1.0)


_GELU_C = math.sqrt(2.0 / math.pi)


def _gelu(x):
    return 0.5 * x * (1.0 + jnp.tanh(_GELU_C * (x + 0.044715 * x * x * x)))


def _gelu_grad(x):
    t = jnp.tanh(_GELU_C * (x + 0.044715 * x * x * x))
    return 0.5 * (1.0 + t) + 0.5 * x * (1.0 - t * t) * _GELU_C * (1.0 + 3 * 0.044715 * x * x)


ATTN_ROWS_PER_STEP = 16
NT_DIMS = (((1,), (1,)), ((), ()))
NN_DIMS = (((1,), (0,)), ((), ()))
TN_DIMS = (((0,), (0,)), ((), ()))


def _attn_geometry(r, rows):
    row_start = jnp.clip(r - WIN_H // 2, 0, rows - WIN_H)
    key0 = pl.multiple_of(row_start * GRID_W, GRID_W)
    bias0 = pl.multiple_of((row_start - r + (WIN_H - 1)) * GRID_W, GRID_W)
    return key0, bias0


def _window_onehot():
    c = jnp.arange(GRID_W)
    col_start = jnp.clip(c - WIN_W // 2, 0, GRID_W - WIN_W)
    col_in = (c[None, :] >= col_start[:, None]) & (c[None, :] < col_start[:, None] + WIN_W)
    dc = jnp.clip(c[None, :] - c[:, None], -(WIN_W - 1), WIN_W - 1) + (WIN_W - 1)
    onehot = ((dc[:, :, None] == jnp.arange(2 * WIN_W - 1)[None, None, :]) & col_in[:, :, None]).astype(F32)
    return onehot, col_in


def _bias_table(rpb):
    onehot, col_in = _window_onehot()
    nh = rpb.shape[0]
    pairs = rpb.reshape(nh // 2, 2, 2 * WIN_H - 1, 2 * WIN_W - 1)
    mask = jnp.where(col_in, 0.0, NEG_INF).T
    heads = [jnp.einsum('prd,qkd->prkq', pairs[:, e], onehot, precision=HI) + mask for e in range(2)]
    return jnp.concatenate(heads, axis=-1).reshape(nh // 2, (2 * WIN_H - 1) * GRID_W, 2 * GRID_W)


def _bias_table_grad(dtab):
    onehot, _ = _window_onehot()
    npair = dtab.shape[0]
    d = dtab.reshape(npair, 2 * WIN_H - 1, GRID_W, 2 * GRID_W)
    heads = [jnp.einsum('prkq,qkd->prd', d[..., e * GRID_W:(e + 1) * GRID_W], onehot, precision=HI) for e in range(2)]
    return jnp.stack(heads, axis=1).reshape(2 * npair, 2 * WIN_H - 1, 2 * WIN_W - 1)


def _lane_lo(shape):
    return lax.broadcasted_iota(jnp.int32, shape, 1) < HEAD_DIM


def _half_sums(v):
    lo = _lane_lo(v.shape)
    s_lo = jnp.sum(jnp.where(lo, v, 0.0), axis=1, keepdims=True)
    s_hi = jnp.sum(jnp.where(lo, 0.0, v), axis=1, keepdims=True)
    return jnp.where(lo, s_lo, s_hi)


def _rms_pair(x, g):
    r = lax.rsqrt(_half_sums(x * x) * (1.0 / HEAD_DIM) + RMS_EPS)
    return x * r * g


def _rms_pair_bwd(x, g, dy):
    r = lax.rsqrt(_half_sums(x * x) * (1.0 / HEAD_DIM) + RMS_EPS)
    xr = x * r
    gdy = g * dy
    dx = r * (gdy - xr * (_half_sums(xr * gdy) * (1.0 / HEAD_DIM)))
    return dx, dy * xr


def _blockdiag(a):
    a2 = jnp.concatenate([a, a], axis=0)
    row_hi = lax.broadcasted_iota(jnp.int32, a2.shape, 0) >= GRID_W
    lane_hi = lax.broadcasted_iota(jnp.int32, a2.shape, 1) >= HEAD_DIM
    return jnp.where(row_hi == lane_hi, a2, 0.0).astype(MXU_DTYPE)


def _diag_blocks(m):
    return jnp.where(_lane_lo((GRID_W, 2 * HEAD_DIM)), m[:GRID_W], m[GRID_W:])


def _attn_scores(qb, kb, bias):
    st = lax.dot_general(kb, qb, NT_DIMS, preferred_element_type=F32)
    st = st * (1.0 / math.sqrt(HEAD_DIM)) + bias
    mx = jnp.max(st, axis=0, keepdims=True)
    p = jnp.exp(st - mx)
    return p * (1.0 / jnp.sum(p, axis=0, keepdims=True))


def _attn_fwd(z4, qg2, kg2, bias_t, comm=None, comm_arrs=()):
    _, t, aw = z4.shape
    rows = t // GRID_W
    npair = aw // (2 * HEAD_DIM)
    nkeys = WIN_H * GRID_W
    nb = bias_t.shape[1]
    rps = min(ATTN_ROWS_PER_STEP, rows)
    blk = rps * GRID_W
    nsteps = rows // rps
    ncomm = len(comm_arrs)

    def body(*refs):
        q_ref, k_ref, v_ref, qg_ref, kg_ref, b_ref = refs[:6]
        c_ins, o_ref, c_outs = refs[6:6 + ncomm], refs[6 + ncomm], refs[7 + ncomm:7 + 2 * ncomm]
        kn_ref, vb_ref = refs[7 + 2 * ncomm:9 + 2 * ncomm]
        sems = refs[9 + 2 * ncomm:]
        pr, rb = pl.program_id(0), pl.program_id(1)
        if comm is not None:
            @pl.when((pr == 0) & (rb == 0))
            def _():
                comm.start(c_ins, c_outs, *sems)

        @pl.when(rb == 0)
        def _():
            kn_ref[...] = _rms_pair(k_ref[...], kg_ref[...]).astype(MXU_DTYPE)
            vb_ref[...] = v_ref[...].astype(MXU_DTYPE)

        def row(i, carry):
            key0, bias0 = _attn_geometry(rb * rps + i, rows)
            at = pl.ds(pl.multiple_of(i * GRID_W, GRID_W), GRID_W)
            qb = _blockdiag(_rms_pair(q_ref[at, :], qg_ref[...]))
            pt = _attn_scores(qb, kn_ref[pl.ds(key0, nkeys), :], b_ref[pl.ds(bias0, nkeys), :])
            both = lax.dot_general(pt.astype(MXU_DTYPE), vb_ref[pl.ds(key0, nkeys), :], TN_DIMS,
                                   preferred_element_type=F32)
            o_ref[at, :] = _diag_blocks(both)
            return carry

        lax.fori_loop(0, rps, row, 0, unroll=4)
        if comm is not None:
            @pl.when((pr == npair - 1) & (rb == nsteps - 1))
            def _():
                comm.finish(c_ins, c_outs, *sems)

    pair_cols = lambda lead: pl.BlockSpec((None, t, 2 * HEAD_DIM), lambda p, r: (lead, 0, p))
    res = pl.pallas_call(
        body, name="attn_fwd", grid=(npair, nsteps),
        in_specs=[pl.BlockSpec((None, blk, 2 * HEAD_DIM), lambda p, r: (0, r, p)), pair_cols(1), pair_cols(2),
                  pl.BlockSpec((1, 2 * HEAD_DIM), lambda p, r: (0, 0)),
                  pl.BlockSpec((1, 2 * HEAD_DIM), lambda p, r: (0, 0)),
                  pl.BlockSpec((None, nb, 2 * GRID_W), lambda p, r: (p, 0, 0))] + [ANY] * ncomm,
        out_specs=[pl.BlockSpec((blk, 2 * HEAD_DIM), lambda p, r: (r, p))] + [ANY] * ncomm,
        out_shape=[jax.ShapeDtypeStruct((t, aw), F32)] + (comm.out_shape if comm is not None else []),
        scratch_shapes=[pltpu.VMEM((t, 2 * HEAD_DIM), MXU_DTYPE), pltpu.VMEM((t, 2 * HEAD_DIM), MXU_DTYPE)]
        + (comm.scratch if comm is not None else []),
        compiler_params=_cparams(("arbitrary", "arbitrary")),
    )(z4, z4, z4, qg2, kg2, bias_t, *comm_arrs)
    return res[0], res[1:]


def _attn_bwd(z4, qg2, kg2, bias_t, dya, comm=None, comm_arrs=()):
    _, t, aw = z4.shape
    rows = t // GRID_W
    npair = aw // (2 * HEAD_DIM)
    nkeys = WIN_H * GRID_W
    nb = bias_t.shape[1]
    rps = min(ATTN_ROWS_PER_STEP, rows)
    blk = rps * GRID_W
    nsteps = rows // rps
    scale = 1.0 / math.sqrt(HEAD_DIM)
    ncomm = len(comm_arrs)

    def body(*refs):
        q_ref, k_ref, v_ref, qg_ref, kg_ref, b_ref, do_ref = refs[:7]
        c_ins = refs[7:7 + ncomm]
        dz_ref, db_ref, dqg_ref, dkg_ref = refs[7 + ncomm:11 + ncomm]
        c_outs = refs[11 + ncomm:11 + 2 * ncomm]
        kn_ref, vb_ref, dkn_ref, dv_ref = refs[11 + 2 * ncomm:15 + 2 * ncomm]
        sems = refs[15 + 2 * ncomm:]
        pr, rb = pl.program_id(0), pl.program_id(1)
        if comm is not None:
            @pl.when((pr == 0) & (rb == 0))
            def _():
                comm.start(c_ins, c_outs, *sems)

        @pl.when(rb == 0)
        def _():
            kn_ref[...] = _rms_pair(k_ref[...], kg_ref[...]).astype(MXU_DTYPE)
            vb_ref[...] = v_ref[...].astype(MXU_DTYPE)
            dkn_ref[...] = jnp.zeros_like(dkn_ref)
            dv_ref[...] = jnp.zeros_like(dv_ref)
            db_ref[...] = jnp.zeros_like(db_ref)
            dqg_ref[...] = jnp.zeros_like(dqg_ref)

        def row(i, dqg_sum):
            r = rb * rps + i
            key0, bias0 = _attn_geometry(r, rows)
            keys = pl.ds(key0, nkeys)
            at = pl.ds(pl.multiple_of(i * GRID_W, GRID_W), GRID_W)
            q = q_ref[at, :]
            qb = _blockdiag(_rms_pair(q, qg_ref[...]))
            dob = _blockdiag(do_ref[at, :])
            kb = kn_ref[keys, :]
            pt = _attn_scores(qb, kb, b_ref[pl.ds(bias0, nkeys), :])
            dv_ref[keys, :] += lax.dot_general(pt.astype(MXU_DTYPE), dob, NN_DIMS, preferred_element_type=F32)
            dpt = lax.dot_general(vb_ref[keys, :], dob, NT_DIMS, preferred_element_type=F32)
            dst = pt * (dpt - jnp.sum(pt * dpt, axis=0, keepdims=True))
            db_ref[pl.ds(bias0, nkeys), :] += dst
            dsb = dst.astype(MXU_DTYPE)
            dkn_ref[keys, :] += scale * lax.dot_general(dsb, qb, NN_DIMS, preferred_element_type=F32)
            dqn = scale * _diag_blocks(lax.dot_general(dsb, kb, TN_DIMS, preferred_element_type=F32))
            dq, dqg = _rms_pair_bwd(q, qg_ref[...], dqn)
            dz_ref[0, pl.ds(pl.multiple_of(r * GRID_W, GRID_W), GRID_W), :] = dq.astype(dz_ref.dtype)
            return dqg_sum + jnp.sum(dqg, axis=0, keepdims=True)

        def rows4(i, acc):
            for j in range(4):
                acc = row(4 * i + j, acc)
            return acc

        dqg_ref[...] += lax.fori_loop(0, rps // 4, rows4, jnp.zeros((1, 2 * HEAD_DIM), F32))

        @pl.when(rb == nsteps - 1)
        def _():
            dk, dkg = _rms_pair_bwd(k_ref[...], kg_ref[...], dkn_ref[...])
            dz_ref[1] = dk.astype(dz_ref.dtype)
            dz_ref[2] = dv_ref[...].astype(dz_ref.dtype)
            dkg_ref[...] = jnp.sum(dkg, axis=0, keepdims=True)

        if comm is not None:
            @pl.when((pr == npair - 1) & (rb == nsteps - 1))
            def _():
                comm.finish(c_ins, c_outs, *sems)

    pair_cols = lambda lead: pl.BlockSpec((None, t, 2 * HEAD_DIM), lambda p, r: (lead, 0, p))
    pair_vec = pl.BlockSpec((None, 1, 2 * HEAD_DIM), lambda p, r: (p, 0, 0))
    res = pl.pallas_call(
        body, name="attn_bwd", grid=(npair, nsteps),
        in_specs=[pl.BlockSpec((None, blk, 2 * HEAD_DIM), lambda p, r: (0, r, p)), pair_cols(1), pair_cols(2),
                  pl.BlockSpec((1, 2 * HEAD_DIM), lambda p, r: (0, 0)),
                  pl.BlockSpec((1, 2 * HEAD_DIM), lambda p, r: (0, 0)),
                  pl.BlockSpec((None, nb, 2 * GRID_W), lambda p, r: (p, 0, 0)),
                  pl.BlockSpec((blk, 2 * HEAD_DIM), lambda p, r: (r, p))] + [ANY] * ncomm,
        out_specs=[pl.BlockSpec((3, t, 2 * HEAD_DIM), lambda p, r: (0, 0, p)),
                   pl.BlockSpec((None, nb, 2 * GRID_W), lambda p, r: (p, 0, 0)),
                   pair_vec, pair_vec] + [ANY] * ncomm,
        out_shape=[jax.ShapeDtypeStruct((4, t, aw), MXU_DTYPE), jax.ShapeDtypeStruct(bias_t.shape, F32),
                   jax.ShapeDtypeStruct((npair, 1, 2 * HEAD_DIM), F32),
                   jax.ShapeDtypeStruct((npair, 1, 2 * HEAD_DIM), F32)] + (comm.out_shape if comm is not None else []),
        scratch_shapes=[pltpu.VMEM((t, 2 * HEAD_DIM), MXU_DTYPE), pltpu.VMEM((t, 2 * HEAD_DIM), MXU_DTYPE),
                        pltpu.VMEM((t, 2 * HEAD_DIM), F32), pltpu.VMEM((t, 2 * HEAD_DIM), F32)]
        + (comm.scratch if comm is not None else []),
        compiler_params=_cparams(("arbitrary", "arbitrary")),
    )(z4, z4, z4, qg2, kg2, bias_t, dya, *comm_arrs)
    return res[:4], res[4:]


def _bmm_exact(name, a, b, dims, per_step=16):
    nb = a.shape[0]
    per = math.gcd(nb, per_step)
    shape = jax.eval_shape(lambda u, v: lax.dot_general(u, v, dims), a[0], b[0]).shape

    def body(a_ref, b_ref, o_ref):
        for e in range(per):
            o_ref[e] = lax.dot_general(a_ref[e], b_ref[e], dims, precision=HI, preferred_element_type=F32)

    blk = lambda arr: pl.BlockSpec((per,) + arr.shape[1:], lambda i: (i, 0, 0))
    out = jax.ShapeDtypeStruct((nb,) + shape, F32)
    return pl.pallas_call(body, name=name, grid=(nb // per,), in_specs=[blk(a), blk(b)], out_specs=blk(out),
                          out_shape=out, compiler_params=_cparams(("parallel",)))(a, b)


@jax.custom_vjp
def _contract_last(a, b):
    return _bmm_exact("s5_kern", a, b, NT_DIMS)


def _contract_last_fwd(a, b):
    return _contract_last(a, b), (a, b)


def _contract_last_bwd(res, g):
    a, b = res
    return _bmm_exact("s5_kern_da", g, b, NN_DIMS), _bmm_exact("s5_kern_db", g, a, TN_DIMS)


_contract_last.defvjp(_contract_last_fwd, _contract_last_bwd)


def _s5_mats(a_re, a_im, b_re, b_im, c_re, c_im, log_step, d_skip):
    nd, g, p = a_re.shape
    c = b_re.shape[-1]
    L = S5_CHUNK
    lr = jnp.minimum(a_re, -1e-4).transpose(1, 0, 2)
    li = a_im.transpose(1, 0, 2)
    dt = jnp.exp(log_step).T[..., None]
    n = jnp.arange(L + 1, dtype=F32)[None, :, None, None]
    mag = jnp.exp(n * (lr * dt)[:, None])
    ang = n * (li * dt)[:, None]
    pw_r, pw_i = mag * jnp.cos(ang), mag * jnp.sin(ang)
    den = lr * lr + li * li
    nr, ni = pw_r[:, 1] - 1.0, pw_i[:, 1]
    cr, ci = (nr * lr + ni * li) / den, (ni * lr - nr * li) / den
    bt_r, bt_i = b_re.transpose(1, 3, 0, 2), b_im.transpose(1, 3, 0, 2)
    bb_r = cr[:, None] * bt_r - ci[:, None] * bt_i
    bb_i = cr[:, None] * bt_i + ci[:, None] * bt_r
    ct_r, ct_i = c_re.transpose(1, 2, 0, 3), c_im.transpose(1, 2, 0, 3)

    def cols(x_re, x_im):
        return jnp.concatenate([x_re[..., 0, :], x_re[..., 1, :], x_im[..., 0, :], x_im[..., 1, :]], axis=-1)

    e_r = jnp.stack([pw_r[:, :L, 0][:, ::-1], pw_r[:, :L, 1]], axis=2)
    e_i = jnp.stack([pw_i[:, :L, 0][:, ::-1], pw_i[:, :L, 1]], axis=2)
    ws = (cols(e_r, e_r)[:, :, None] * cols(bb_r, bb_i)[:, None]
          + cols(e_i, e_i)[:, :, None] * cols(-bb_i, bb_r)[:, None]).reshape(g, L * c, 4 * p)
    f_r = jnp.stack([pw_r[:, 1:, 0], pw_r[:, 1:, 1][:, ::-1]], axis=2)
    f_i = jnp.stack([pw_i[:, 1:, 0], pw_i[:, 1:, 1][:, ::-1]], axis=2)
    wot = (cols(f_r, f_i)[:, :, None] * cols(ct_r, -ct_r)[:, None]
           + cols(f_i, f_r)[:, :, None] * cols(-ct_i, -ct_i)[:, None]).reshape(g, L * c, 4 * p)
    qr, qi = pw_r[:, None, :L], pw_i[:, None, :L]
    br, bi = bb_r[:, :, None], bb_i[:, :, None]
    kp_r, kp_i = qr * br - qi * bi, qr * bi + qi * br
    lhs = jnp.stack([jnp.concatenate([kp_r[..., d, :], -kp_i[..., d, :]], axis=-1) for d in range(2)])
    rhs = jnp.stack([jnp.concatenate([ct_r[:, :, d], ct_i[:, :, d]], axis=-1) for d in range(2)])
    kern = _contract_last(lhs.reshape(2 * g, c * L, 2 * p), rhs.reshape(2 * g, c, 2 * p)).reshape(2, g, c, L, c)
    skip = d_skip.reshape(g, c, 1, 1) * jnp.eye(c, dtype=F32)[None, :, None, :]
    by_offset = jnp.concatenate([kern[1][:, :, :0:-1], kern[0][:, :, :1] + kern[1][:, :, :1] + skip,
                                 kern[0][:, :, 1:]], axis=2).reshape(g, c, (2 * L - 1) * c)
    mt = jnp.stack([by_offset[:, :, (L - 1 - j) * c:(2 * L - 1 - j) * c] for j in range(L)], axis=1)
    mt = mt.reshape(g, L * c, L * c)
    lr16, li16 = pw_r[:, L], pw_i[:, L]
    fa = jnp.concatenate([lr16[:, 0], lr16[:, 1], lr16[:, 0], lr16[:, 1]], axis=-1)
    fb = jnp.concatenate([-li16[:, 0], -li16[:, 1], li16[:, 0], li16[:, 1]], axis=-1)
    return mt, ws, wot, fa, fb


def _gmm(name, a, b, contract, a_stacked=False, b_stacked=False, o_stacked=False, add=None, out_dtype=F32):
    w = S5_CHUNK * SSM_GROUP_CH
    g = (a.shape[0] if a_stacked else a.shape[1] // w)
    gpb = math.gcd(g, S5_GROUPS_PER_STEP)
    dn = {'nn': (((1,), (0,)), ((), ())), 'nt': (((1,), (1,)), ((), ())), 'tn': (((0,), (0,)), ((), ()))}[contract]

    def spec(arr, stacked):
        if stacked:
            return pl.BlockSpec((gpb,) + arr.shape[1:], lambda i: (i, 0, 0))
        return pl.BlockSpec((arr.shape[0], gpb * w), lambda i: (0, i))

    def take(ref, stacked, e):
        return ref[e] if stacked else ref[:, e * w:(e + 1) * w]

    m = (a.shape[1] if a_stacked else a.shape[0]) if contract != 'tn' else w
    n = w
    if o_stacked:
        o_spec = pl.BlockSpec((gpb, m, n), lambda i: (i, 0, 0))
        o_shape = (g, m, n)
    else:
        o_spec = pl.BlockSpec((m, gpb * n), lambda i: (0, i))
        o_shape = (m, g * n)
    has_add = add is not None

    def body(*refs):
        if has_add:
            a_ref, b_ref, add_ref, o_ref = refs
        else:
            a_ref, b_ref, o_ref = refs
        for e in range(gpb):
            r = lax.dot_general(take(a_ref, a_stacked, e).astype(S5_DTYPE), take(b_ref, b_stacked, e).astype(S5_DTYPE),
                                dn, precision=HI if S5_DTYPE == F32 else None, preferred_element_type=F32)
            if has_add:
                r = r + take(add_ref, o_stacked, e)
            if o_stacked:
                o_ref[e] = r.astype(o_ref.dtype)
            else:
                o_ref[:, e * w:(e + 1) * w] = r.astype(o_ref.dtype)

    in_specs = [spec(a, a_stacked), spec(b, b_stacked)] + ([o_spec] if has_add else [])
    return pl.pallas_call(
        body, name=name, grid=(g // gpb,), in_specs=in_specs, out_specs=o_spec,
        out_shape=jax.ShapeDtypeStruct(o_shape, out_dtype), compiler_params=_cparams(("parallel",)),
    )(*((a, b) + ((add,) if has_add else ())))


def _s5_scan(name, s, fa, fb, rev0, xin=None):
    nk, g, w = s.shape
    hw, qw = w // 2, w // 4
    gb = min(g, 16)
    with_acc = xin is not None

    def body(*refs):
        if with_acc:
            s_ref, a_ref, b_ref, x_ref, o_ref, pa_ref, pb_ref = refs
        else:
            s_ref, a_ref, b_ref, o_ref = refs
        fa_v, fb_v = a_ref[...], b_ref[...]
        dir0 = lax.broadcasted_iota(jnp.int32, (gb, w), 1) % hw < qw
        swap = lambda v: jnp.concatenate([v[:, hw:], v[:, :hw]], axis=1)

        def step(i, carry):
            x, pa, pb = carry
            k0 = (nk - 1 - i) if rev0 else i
            k1 = i if rev0 else (nk - 1 - i)
            for lo in (0, hw):
                o_ref[k0, :, lo:lo + qw] = x[:, lo:lo + qw]
                o_ref[k1, :, lo + qw:lo + hw] = x[:, lo + qw:lo + hw]
            if with_acc:
                xi = jnp.where(dir0, x_ref[k0], x_ref[k1])
                pa = pa + x * xi
                pb = pb + x * swap(xi)
            x = fa_v * x + fb_v * swap(x) + jnp.where(dir0, s_ref[k0], s_ref[k1])
            return x, pa, pb

        z = jnp.zeros((gb, w), F32)
        res = lax.fori_loop(0, nk, step, (z, z, z), unroll=2)
        if with_acc:
            pa_ref[...] = res[1]
            pb_ref[...] = res[2]

    seq = pl.BlockSpec((nk, gb, w), lambda i: (0, i, 0))
    vec = pl.BlockSpec((gb, w), lambda i: (i, 0))
    in_specs = [seq, vec, vec] + ([seq] if with_acc else [])
    out_specs = [seq] + ([vec, vec] if with_acc else [])
    out_shape = [jax.ShapeDtypeStruct((nk, g, w), F32)] + (
        [jax.ShapeDtypeStruct((g, w), F32)] * 2 if with_acc else [])
    return pl.pallas_call(
        body, name=name, grid=(g // gb,), in_specs=in_specs, out_specs=out_specs, out_shape=out_shape,
        compiler_params=_cparams(("parallel",)),
    )(*((s, fa, fb) + ((xin,) if with_acc else ())))


def _regroup(name, x, to_groups):
    if to_groups:
        t, sw = x.shape
    else:
        t, sw = x.shape[0] * S5_CHUNK, x.shape[1] // S5_CHUNK
    nk = t // S5_CHUNK
    wide = LANES * S5_CHUNK

    def place(tok):
        r = lax.broadcasted_iota(jnp.int32, (2 * LANES, wide), 0)
        col = lax.broadcasted_iota(jnp.int32, (2 * LANES, wide), 1)
        ch = r % LANES
        want = (ch // SSM_GROUP_CH) * (S5_CHUNK * SSM_GROUP_CH) + (tok + r // LANES) * SSM_GROUP_CH + ch % SSM_GROUP_CH
        return (col == want).astype(S5_DTYPE)

    def body(x_ref, o_ref):
        token = lambda tok: (pl.ds(tok, nk, stride=S5_CHUNK), slice(None))
        if to_groups:
            acc = jnp.zeros((nk, wide), F32)
            for tok in range(0, S5_CHUNK, 2):
                rows = jnp.concatenate([x_ref[token(tok)], x_ref[token(tok + 1)]], axis=1).astype(S5_DTYPE)
                acc = acc + lax.dot_general(rows, place(tok), NN_DIMS, preferred_element_type=F32)
            o_ref[...] = acc.astype(o_ref.dtype)
        else:
            xv = x_ref[...].astype(S5_DTYPE)
            for tok in range(0, S5_CHUNK, 2):
                both = lax.dot_general(xv, place(tok), NT_DIMS, preferred_element_type=F32).astype(o_ref.dtype)
                o_ref[token(tok)] = both[:, :LANES]
                o_ref[token(tok + 1)] = both[:, LANES:]

    tokens = pl.BlockSpec((t, LANES), lambda i: (0, i))
    groups = pl.BlockSpec((nk, wide), lambda i: (0, i))
    return pl.pallas_call(
        body, name=name, grid=(sw // LANES,), in_specs=[tokens if to_groups else groups],
        out_specs=groups if to_groups else tokens,
        out_shape=jax.ShapeDtypeStruct((nk, sw * S5_CHUNK), S5_DTYPE) if to_groups else jax.ShapeDtypeStruct((t, sw), F32),
        compiler_params=_cparams(("parallel",)),
    )(x)


def _s5_fwd(u2, mats):
    mt, ws, wot, fa, fb = mats
    nk = u2.shape[0]
    g = mt.shape[0]
    y_intra = _gmm("s5_intra", u2, mt, 'nn', b_stacked=True)
    s = _gmm("s5_chunk_state", u2, ws, 'nn', b_stacked=True)
    (xin,) = _s5_scan("s5_scan", s.reshape(nk, g, -1), fa, fb, False)
    xin = xin.reshape(nk, -1)
    return _gmm("s5_inter", xin, wot, 'nt', b_stacked=True, add=y_intra, out_dtype=S5_DTYPE), xin


def _s5_bwd(u2, xin, mats, dy2):
    mt, ws, wot, fa, fb = mats
    nk = u2.shape[0]
    g = mt.shape[0]
    dxin = _gmm("s5_dxin", dy2, wot, 'nn', b_stacked=True)
    ds, pa, pb = _s5_scan("s5_scan_adj", dxin.reshape(nk, g, -1), fa, -fb, True, xin=xin.reshape(nk, g, -1))
    ds = ds.reshape(nk, -1)
    du_a = _gmm("s5_du_intra", dy2, mt, 'nt', b_stacked=True)
    du2 = _gmm("s5_du_state", ds, ws, 'nt', b_stacked=True, add=du_a, out_dtype=S5_DTYPE)
    dmt = _gmm("s5_dmt", u2, dy2, 'tn', o_stacked=True)
    dws = _gmm("s5_dws", u2, ds, 'tn', o_stacked=True)
    dwot = _gmm("s5_dwot", dy2, xin, 'tn', o_stacked=True)
    return du2, (dmt, dws, dwot, pa, pb)


def _stacked(g4):
    return g4.reshape((N_CHIPS, -1, g4.shape[-1]))


def _local_step(x, target, w_in4, late, small, reduce_late=None, reduce_mid=None):
    t, d = x.shape
    aw = w_in4.shape[2]
    sw = aw
    nh = aw // HEAD_DIM
    row = lambda v: v.reshape(1, -1)
    g_mix, g_ffn = row(small['g_mix']), row(small['g_ffn'])
    g_oa, g_os, b_glu = row(small['g_out_attn']), row(small['g_out_ssm']), row(small['b_glu'])
    qg2 = jnp.tile(row(small['q_gain']), (1, 2))
    kg2 = jnp.tile(row(small['k_gain']), (1, 2))

    (h,) = _ew("rms_mix", lambda xv, g: _rms(xv, g)[0], [('r', x), ('c', g_mix)], [('r', d, MXU_DTYPE)])
    bias_t = _bias_table(small['rpb'])
    if late[0] == 'halves':
        under_in, under_attn, under_gate, under_up = late[1][:2], late[1][2:3], late[1][3:4], late[1][4:]
        z4, got_in = _mm("in_proj", h, w_in4, contract='nn', b_mode='b', o_mode='b',
                         comm=_GatherChips(under_in), comm_arrs=under_in)
        ya, got_attn = _attn_fwd(z4, qg2, kg2, bias_t, comm=_GatherChips(under_attn), comm_arrs=under_attn)
        w_glu, w_out = (_stacked(g4).reshape(-1, g4.shape[-1]) for g4 in got_in)
        w_gate4 = _stacked(got_attn[0])
    else:
        z4 = _mm("in_proj", h, w_in4, contract='nn', b_mode='b', o_mode='b')
        ya, _ = _attn_fwd(z4, qg2, kg2, bias_t)
        w_glu, w_out, w_gate4, w_up4, w_down4 = late[1]
    ffs = w_gate4.shape[2]
    s5_params = tuple(small[n] for n in ('ssm_a_re', 'ssm_a_im', 'ssm_b_re', 'ssm_b_im', 'ssm_c_re', 'ssm_c_im',
                                         'ssm_log_step', 'ssm_d'))
    mats, mats_vjp = jax.vjp(_s5_mats, *s5_params)
    mats = tuple(m.astype(S5_DTYPE) for m in mats[:3]) + mats[3:]
    u2 = _regroup("s5_group_u", z4[3], True)
    ypre2, xin = _s5_fwd(u2, mats)
    ypre = _regroup("s5_ungroup_y", ypre2, False)
    (yb,) = _ew("gelu", _gelu, [('r', ypre)], [('r', sw, MXU_DTYPE)])
    a_glu = _mm("glu_proj", yb, w_glu, contract='nn')

    def mix_out(yav, ypv, av, bg, goa, gos):
        ys = _gelu(ypv) * _sigmoid(av + bg)
        return jnp.concatenate([_rms(yav, goa)[0], _rms(ys, gos)[0]], axis=1)
    (ycat,) = _ew("mix_out", mix_out, [('r', ya), ('r', ypre), ('r', a_glu), ('c', b_glu), ('c', g_oa), ('c', g_os)],
                  [('r', aw + sw, MXU_DTYPE)])
    x1 = _mm("out_proj", ycat, w_out, contract='nn', add=x)
    (h2,) = _ew("rms_ffn", lambda xv, g: _rms(xv, g)[0], [('r', x1), ('c', g_ffn)], [('r', d, MXU_DTYPE)])
    if late[0] == 'halves':
        gate4, got_gate = _mm("ffn_gate", h2, w_gate4, contract='nn', b_mode='b', o_mode='b', tn=ffs,
                              out_dtype=MXU_DTYPE, comm=_GatherChips(under_gate), comm_arrs=under_gate)
        w_up4 = _stacked(got_gate[0])
        up4, got_up = _mm("ffn_up", h2, w_up4, contract='nn', b_mode='b', o_mode='b', tn=ffs,
                          out_dtype=MXU_DTYPE, comm=_GatherChips(under_up), comm_arrs=under_up)
        w_down4 = _stacked(got_up[0])
    else:
        gate4 = _mm("ffn_gate", h2, w_gate4, contract='nn', b_mode='b', o_mode='b', tn=ffs, out_dtype=MXU_DTYPE)
        up4 = _mm("ffn_up", h2, w_up4, contract='nn', b_mode='b', o_mode='b', tn=ffs, out_dtype=MXU_DTYPE)
    gate_f, up_f = gate4.reshape(4 * t, ffs), up4.reshape(4 * t, ffs)
    (act,) = _ew("swiglu", lambda gv, uv: gv * _sigmoid(gv) * uv, [('r', gate_f), ('r', up_f)],
                 [('r', ffs, MXU_DTYPE)], tr=512)
    act4 = act.reshape(4, t, ffs)
    x2 = _mm("ffn_down", act4, w_down4, contract='nn', a_mode='c', b_mode='c', add=x1, tk=ffs)

    def loss_fn(xv, tv):
        diff = xv - tv
        return diff * (1.0 / d), diff * (1.0 / d), diff * diff
    dx2, dx2_b, sq = _ew("loss", loss_fn, [('r', x2), ('r', target)], [('r', d, F32), ('r', d, MXU_DTYPE), ('a', d)])

    dact4 = _mm("ffn_down_dx", dx2_b, w_down4, contract='nt', b_mode='b', o_mode='b', tn=ffs, out_dtype=MXU_DTYPE)
    d_w_down4 = _mm("ffn_down_dw", act4, dx2_b, contract='tn', a_mode='b', o_mode='b', tm=ffs, out_dtype=DW_DTYPE)

    def swiglu_bwd(dav, gv, uv):
        s = _sigmoid(gv)
        return dav * uv * s * (1.0 + gv * (1.0 - s)), dav * gv * s
    dgate, dup = _ew("swiglu_bwd", swiglu_bwd, [('r', dact4.reshape(4 * t, ffs)), ('r', gate_f), ('r', up_f)],
                     [('r', ffs, MXU_DTYPE), ('r', ffs, MXU_DTYPE)], tr=512)
    dgate4, dup4 = dgate.reshape(4, t, ffs), dup.reshape(4, t, ffs)
    if reduce_late is not None:
        sums_down = reduce_late("down", [d_w_down4])
        dh2, got_down = _mm("ffn_gate_dx", dgate4, w_gate4, contract='nt', a_mode='c', b_mode='c', tk=ffs, tn=2048,
                            comm=_ScatterChips(sums_down), comm_arrs=sums_down)
    else:
        dh2 = _mm("ffn_gate_dx", dgate4, w_gate4, contract='nt', a_mode='c', b_mode='c', tk=ffs, tn=2048)
    dh2 = _mm("ffn_up_dx", dup4, w_up4, contract='nt', a_mode='c', b_mode='c', add=dh2, tk=ffs)
    d_w_gate4 = _mm("ffn_gate_dw", h2, dgate4, contract='tn', b_mode='b', o_mode='b', tn=ffs, out_dtype=DW_DTYPE)
    d_w_up4 = _mm("ffn_up_dw", h2, dup4, contract='tn', b_mode='b', o_mode='b', tn=ffs, out_dtype=DW_DTYPE)

    def rms_res_bwd(xv, g, dyv, resv):
        dx, dg = _rms_bwd(xv, g, dyv)
        return resv + dx, dg
    dx1, d_g_ffn = _ew("rms_ffn_bwd", rms_res_bwd, [('r', x1), ('c', g_ffn), ('r', dh2), ('r', dx2)],
                       [('r', d, F32), ('a', d)])

    dycat = _mm("out_proj_dx", dx1, w_out, contract='nt', out_dtype=MXU_DTYPE)
    d_w_out = _mm("out_proj_dw", ycat, dx1, contract='tn', out_dtype=DW_DTYPE)

    def mix_out_bwd(yav, ypv, av, bg, goa, gos, dca, dcs):
        dya, dgoa = _rms_bwd(yav, goa, dca)
        y = _gelu(ypv)
        s = _sigmoid(av + bg)
        dys, dgos = _rms_bwd(y * s, gos, dcs)
        da = dys * y * s * (1.0 - s)
        return dya, da, dys * s, dgoa, dgos, da
    dya, da, dy_direct, d_g_oa, d_g_os, d_b_glu = _ew(
        "mix_out_bwd", mix_out_bwd,
        [('r', ya), ('r', ypre), ('r', a_glu), ('c', b_glu), ('c', g_oa), ('c', g_os),
         ('r', dycat, 0, aw), ('r', dycat, 1, sw)],
        [('r', aw, F32), ('r', sw, MXU_DTYPE), ('r', sw, F32), ('a', aw), ('a', sw), ('a', sw)])
    dy = _mm("glu_proj_dx", da, w_glu, contract='nt', add=dy_direct)
    d_w_glu = _mm("glu_proj_dw", yb, da, contract='tn', out_dtype=DW_DTYPE)
    (dypre,) = _ew("gelu_bwd", lambda dyv, ypv: dyv * _gelu_grad(ypv), [('r', dy), ('r', ypre)],
                   [('r', sw, F32)])

    du2, dmats = _s5_bwd(u2, xin, mats, _regroup("s5_group_dy", dypre, True))
    d_s5 = mats_vjp(dmats)
    du = _regroup("s5_ungroup_du", du2, False)
    d_late = (d_w_glu, d_w_out, d_w_gate4, d_w_up4, d_w_down4)
    if reduce_late is not None:
        sums = reduce_late("late", d_late[:4])
        (dz4, dbias_t, dqg, dkg), scattered = _attn_bwd(z4, qg2, kg2, bias_t, dya, comm=_ScatterChips(sums),
                                                       comm_arrs=sums)
        d_late = (sums + sums_down, list(scattered) + list(got_down))
    else:
        (dz4, dbias_t, dqg, dkg), _ = _attn_bwd(z4, qg2, kg2, bias_t, dya)
    d_rpb = _bias_table_grad(dbias_t)
    fold = lambda v: v.reshape(-1, 2, HEAD_DIM).sum(axis=(0, 1))
    dz4 = dz4.at[3].set(du.astype(dz4.dtype))

    d_w_in4 = _mm("in_proj_dw", h, dz4, contract='tn', b_mode='b', o_mode='b', out_dtype=DW_DTYPE)
    d_mid = [d_w_in4] + [d_s5[i].reshape(-1, LANES) for i in (2, 3, 4, 5)]
    if reduce_late is not None:
        sums = reduce_mid(d_mid)
        dh, scattered = _mm("in_proj_dx", dz4, w_in4, contract='nt', a_mode='c', b_mode='c',
                            comm=_ScatterChips(sums), comm_arrs=sums)
        d_mid = (sums, list(scattered))
    else:
        dh = _mm("in_proj_dx", dz4, w_in4, contract='nt', a_mode='c', b_mode='c')
    dx, d_g_mix = _ew("rms_mix_bwd", rms_res_bwd, [('r', x), ('c', g_mix), ('r', dh), ('r', dx1)],
                      [('r', d, F32), ('a', d)])

    colsum = lambda v: v.sum(axis=0)
    d_small = {
        'g_mix': colsum(d_g_mix), 'q_gain': fold(dqg), 'k_gain': fold(dkg), 'rpb': d_rpb,
        'ssm_a_re': d_s5[0], 'ssm_a_im': d_s5[1], 'ssm_b_re': d_s5[2], 'ssm_b_im': d_s5[3],
        'ssm_c_re': d_s5[4], 'ssm_c_im': d_s5[5], 'ssm_log_step': d_s5[6], 'ssm_d': d_s5[7],
        'b_glu': colsum(d_b_glu), 'g_out_attn': colsum(d_g_oa), 'g_out_ssm': colsum(d_g_os), 'g_ffn': colsum(d_g_ffn),
    }
    return jnp.sum(sq), dx, d_late, d_mid, d_small


ANY = pl.BlockSpec(memory_space=pl.ANY)


def _place():
    x, y, c = lax.axis_index("x"), lax.axis_index("y"), lax.axis_index("c")
    other_chips = [(1 - x, y), (x, 1 - y), (1 - x, 1 - y)]
    return x, y, c, 2 * x + y, (x, y, 1 - c), other_chips


class _GatherChips:
    KINDS = 7

    def __init__(self, arrs):
        self.n = len(arrs)
        self.out_shape = [jax.ShapeDtypeStruct((N_CHIPS,) + a.shape, a.dtype) for a in arrs]
        self.scratch = [pltpu.SemaphoreType.DMA((self.n, self.KINDS)), pltpu.SemaphoreType.DMA((self.n, self.KINDS))]

    def _copies(self, ins, outs, send_sems, recv_sems):
        x, y, c, me, sibling, chips = _place()

        def remote(a, k, src, dst, to):
            return lambda: pltpu.make_async_remote_copy(src_ref=src, dst_ref=dst, send_sem=send_sems.at[a, k],
                                                        recv_sem=recv_sems.at[a, k], device_id=to, device_id_type=MESH)
        own, out, landed, passed, theirs = [], [], [], [], []
        for a in range(self.n):
            own.append(remote(a, 6, ins[a], outs[a].at[me], sibling))
            for j, (px, py) in enumerate(chips):
                there, here = outs[a].at[2 * px + py, c], outs[a].at[2 * px + py, 1 - c]
                out.append(remote(a, j, ins[a].at[c], outs[a].at[me, c], (px, py, c)))
                landed.append(remote(a, j, there, there, (px, py, c)))
                passed.append(remote(a, 3 + j, there, there, sibling))
                theirs.append(remote(a, 3 + j, here, here, sibling))
        return own, out, landed, passed, theirs

    def start(self, ins, outs, send_sems, recv_sems):
        own, out, _, _, _ = self._copies(ins, outs, send_sems, recv_sems)
        for make in own + out:
            make().start()

    def finish(self, ins, outs, send_sems, recv_sems):
        own, out, landed, passed, theirs = self._copies(ins, outs, send_sems, recv_sems)
        for arrived, onward in zip(landed, passed):
            arrived().wait_recv()
            onward().start()
        for make in theirs + own:
            make().wait_recv()
        for make in own + out + passed:
            make().wait_send()


class _ScatterChips:
    def __init__(self, sums):
        self.n = len(sums)
        self.out_shape = [jax.ShapeDtypeStruct(s.shape, s.dtype) for s in sums]
        self.scratch = [pltpu.SemaphoreType.DMA((self.n, 3)), pltpu.SemaphoreType.DMA((self.n, 3))]

    def _copies(self, ins, outs, send_sems, recv_sems):
        x, y, c, me, sibling, chips = _place()
        out, landed = [], []

        def remote(a, j, src, dst, to):
            return lambda: pltpu.make_async_remote_copy(src_ref=src, dst_ref=dst, send_sem=send_sems.at[a, j],
                                                        recv_sem=recv_sems.at[a, j], device_id=to, device_id_type=MESH)
        for a in range(self.n):
            for j, (px, py) in enumerate(chips):
                slot = outs[a].at[2 * px + py]
                out.append(remote(a, j, ins[a].at[2 * px + py], outs[a].at[me], (px, py, c)))
                landed.append(remote(a, j, slot, slot, (px, py, c)))
        return out, landed

    def start(self, ins, outs, send_sems, recv_sems):
        for make in self._copies(ins, outs, send_sems, recv_sems)[0]:
            make().start()

    def finish(self, ins, outs, send_sems, recv_sems):
        out, landed = self._copies(ins, outs, send_sems, recv_sems)
        for make in landed:
            make().wait_recv()
        for make in out:
            make().wait_send()


def _comm_call(name, comm, arrs):
    n = comm.n

    def body(*refs):
        parts = (refs[:n], refs[n:2 * n]) + tuple(refs[2 * n:])
        comm.start(*parts)
        comm.finish(*parts)

    return pl.pallas_call(body, name=name, in_specs=[ANY] * n, out_specs=[ANY] * n, out_shape=comm.out_shape,
                          scratch_shapes=comm.scratch)(*arrs)


def _gather_chips(name, arrs):
    return _comm_call(name, _GatherChips(arrs), arrs)


def _swap_halves(name, parts):
    n = len(parts)

    def body(*refs):
        ins, outs = refs[:n], refs[n:2 * n]
        send_sems, recv_sems = refs[2 * n:]
        x, y, c, me, sibling, chips = _place()
        cps = []
        for a in range(n):
            cp = pltpu.make_async_remote_copy(src_ref=ins[a].at[:, 1 - c], dst_ref=outs[a], send_sem=send_sems.at[a],
                                              recv_sem=recv_sems.at[a], device_id=sibling, device_id_type=MESH)
            cp.start()
            cps.append(cp)
        for cp in cps:
            cp.wait()

    return pl.pallas_call(
        body, name=name, in_specs=[ANY] * n, out_specs=[ANY] * n,
        out_shape=[jax.ShapeDtypeStruct((N_CHIPS,) + p.shape[2:], p.dtype) for p in parts],
        scratch_shapes=[pltpu.SemaphoreType.DMA((n,)), pltpu.SemaphoreType.DMA((n,))],
    )(*parts)


def _scatter_chips(name, sums):
    return _comm_call(name, _ScatterChips(sums), sums)


def _swap_reduced(name, halves):
    n = len(halves)

    def body(*refs):
        ins, outs = refs[:n], refs[n:2 * n]
        send_sems, recv_sems = refs[2 * n:]
        x, y, c, me, sibling, chips = _place()
        cps = []
        for a in range(n):
            cp = pltpu.make_async_remote_copy(src_ref=ins[a], dst_ref=outs[a], send_sem=send_sems.at[a],
                                              recv_sem=recv_sems.at[a], device_id=sibling, device_id_type=MESH)
            cp.start()
            cps.append(cp)
        for cp in cps:
            cp.wait()

    return pl.pallas_call(
        body, name=name, in_specs=[ANY] * n, out_specs=[ANY] * n,
        out_shape=[jax.ShapeDtypeStruct(h.shape, h.dtype) for h in halves],
        scratch_shapes=[pltpu.SemaphoreType.DMA((n,)), pltpu.SemaphoreType.DMA((n,))],
    )(*halves)


def _row_tile(r, want=256):
    t = (min(r, want) // SUBLANES) * SUBLANES
    while r % t:
        t -= SUBLANES
    return t


def _add_own_half(name, part, got, c, out_dtype):
    _, _, r, cols = part.shape
    tr = _row_tile(r)

    def body(c_ref, p_ref, g_ref, o_ref):
        o_ref[...] = (p_ref[...].astype(F32) + g_ref[...].astype(F32)).astype(o_ref.dtype)

    return pl.pallas_call(
        body, name=name,
        grid_spec=pltpu.PrefetchScalarGridSpec(
            num_scalar_prefetch=1, grid=(N_CHIPS, r // tr),
            in_specs=[pl.BlockSpec((None, None, tr, cols), lambda s, i, c_ref: (s, c_ref[0], i, 0)),
                      pl.BlockSpec((None, tr, cols), lambda s, i, c_ref: (s, i, 0))],
            out_specs=pl.BlockSpec((None, tr, cols), lambda s, i, c_ref: (s, i, 0))),
        out_shape=jax.ShapeDtypeStruct(got.shape, out_dtype),
        compiler_params=_cparams(("parallel", "parallel")),
    )(c.reshape(1).astype(jnp.int32), part, got)


def _sum_chips(name, got, own, me):
    _, r, cols = got.shape
    tr = _row_tile(r)

    def body(me_ref, r0, r1, r2, r3, own_ref, o_ref):
        pick = lambda s, ref: jnp.where(me_ref[0] == s, own_ref[...], ref[...]).astype(F32)
        o_ref[...] = ((pick(0, r0) + pick(1, r1)) + pick(2, r2)) + pick(3, r3)

    def slot(s):
        return pl.BlockSpec((None, tr, cols),
                            lambda i, me_ref: (jnp.where(me_ref[0] == s, (s + 1) % N_CHIPS, s), i, 0))

    return pl.pallas_call(
        body, name=name,
        grid_spec=pltpu.PrefetchScalarGridSpec(
            num_scalar_prefetch=1, grid=(r // tr,),
            in_specs=[slot(s) for s in range(N_CHIPS)]
            + [pl.BlockSpec((None, tr, cols), lambda i, me_ref: (me_ref[0], i, 0))],
            out_specs=pl.BlockSpec((tr, cols), lambda i, me_ref: (i, 0))),
        out_shape=jax.ShapeDtypeStruct((r, cols), F32),
        compiler_params=_cparams(("parallel",)),
    )(me.reshape(1).astype(jnp.int32), got, got, got, got, own)


def _adamw_math(wv, gv, mv, vv):
    mv = ADAM_B1 * mv + (1.0 - ADAM_B1) * gv
    vv = ADAM_B2 * vv + (1.0 - ADAM_B2) * (gv * gv)
    m_hat = mv / (1.0 - ADAM_B1 ** ADAM_STEP)
    v_hat = vv / (1.0 - ADAM_B2 ** ADAM_STEP)
    return -ADAM_LR * (m_hat / (jnp.sqrt(v_hat) + ADAM_EPS) + ADAM_WD * wv), mv, vv


def _adamw(name, w, g, m, v):
    cols = w.shape[1]
    return _ew(name, _adamw_math, [('r', w), ('r', g), ('r', m), ('r', v)], [('r', cols, F32)] * 3,
               tr=_row_tile(w.shape[0], max(LANES, LANES * LANES // cols)))


def _adamw_halves(name, w, mine, theirs, m, v, c):
    r, cols = mine.shape
    tr = _row_tile(r, 128)
    nb = r // tr

    def body(c_ref, w_ref, a_ref, b_ref, m_ref, v_ref, g_out, d_out, m_out, v_out):
        g = jnp.where(pl.program_id(0) == c_ref[0], a_ref[...], b_ref[...])
        g_out[...] = g
        d_out[...], m_out[...], v_out[...] = _adamw_math(w_ref[...], g, m_ref[...], v_ref[...])

    whole = pl.BlockSpec((tr, cols), lambda h, i, c_ref: (h * nb + i, 0))
    half = pl.BlockSpec((tr, cols), lambda h, i, c_ref: (i, 0))
    return pl.pallas_call(
        body, name=name,
        grid_spec=pltpu.PrefetchScalarGridSpec(
            num_scalar_prefetch=1, grid=(2, nb),
            in_specs=[whole, half, half, whole, whole], out_specs=[whole] * 4),
        out_shape=[jax.ShapeDtypeStruct(w.shape, F32)] * 4,
        compiler_params=_cparams(("parallel", "parallel")),
    )(c.reshape(1).astype(jnp.int32), w, mine, theirs, m, v)


SMALL_ROWS_ALIGN = 2 * N_CHIPS * SUBLANES


MEDIUM_NAMES = ['ssm_b_re', 'ssm_b_im', 'ssm_c_re', 'ssm_c_im']
PACKED_NAMES = [n for n in SMALL_NAMES if n not in MEDIUM_NAMES]


def _pack_small(d):
    flat = jnp.concatenate([d[n].reshape(-1).astype(F32) for n in PACKED_NAMES])
    rows = -(-flat.shape[0] // (LANES * SMALL_ROWS_ALIGN)) * SMALL_ROWS_ALIGN
    return jnp.pad(flat, (0, rows * LANES - flat.shape[0])).reshape(rows, LANES)


def _unpack_small(packed, like):
    flat = packed.reshape(-1)
    out, off = {}, 0
    for n in PACKED_NAMES:
        size = like[n].size
        out[n] = flat[off:off + size].reshape(like[n].shape)
        off += size
    return out


def kernel(x, g_mix, w_in, q_gain, k_gain, rpb, ssm_a_re, ssm_a_im, ssm_b_re, ssm_b_im, ssm_c_re, ssm_c_im, ssm_log_step, ssm_d, w_glu, b_glu, g_out_attn, g_out_ssm, w_out, g_ffn, w_ffn_gate, w_ffn_up, w_ffn_down, loss_target, m_g_mix, m_w_in, m_q_gain, m_k_gain, m_rpb, m_ssm_a_re, m_ssm_a_im, m_ssm_b_re, m_ssm_b_im, m_ssm_c_re, m_ssm_c_im, m_ssm_log_step, m_ssm_d, m_w_glu, m_b_glu, m_g_out_attn, m_g_out_ssm, m_w_out, m_g_ffn, m_w_ffn_gate, m_w_ffn_up, m_w_ffn_down, v_g_mix, v_w_in, v_q_gain, v_k_gain, v_rpb, v_ssm_a_re, v_ssm_a_im, v_ssm_b_re, v_ssm_b_im, v_ssm_c_re, v_ssm_c_im, v_ssm_log_step, v_ssm_d, v_w_glu, v_b_glu, v_g_out_attn, v_g_out_ssm, v_w_out, v_g_ffn, v_w_ffn_gate, v_w_ffn_up, v_w_ffn_down):
    given = dict(locals())
    w = {n: given[n][0] for n in WEIGHT_NAMES}
    mom = {n: given["m_" + n][0] for n in WEIGHT_NAMES}
    var = {n: given["v_" + n][0] for n in WEIGHT_NAMES}
    d = x.shape[-1]
    c = lax.axis_index("c")

    halves = {n: w[n].astype(MXU_DTYPE).reshape((2, w[n].shape[0] // 2, w[n].shape[1])) for n in BIG_NAMES}
    (w_in4,) = _gather_chips("gather_w_in", [halves['w_in']])
    w_in4 = w_in4.reshape((N_CHIPS, -1, w_in4.shape[-1]))

    def chip_sums(tag, grads, payload):
        parts = [g.reshape((N_CHIPS, 2, -1, g.shape[-1])) for g in grads]
        got = _swap_halves("reduce_swap_halves_" + tag, parts)
        return [_add_own_half("reduce_add_%s_%d" % (tag, a), p, gt, c, dt)
                for a, (p, gt, dt) in enumerate(zip(parts, got, payload))]

    reduce_late = lambda tag, grads: chip_sums(tag, grads, [GRAD_PAYLOAD_DTYPE] * len(grads))
    reduce_mid = lambda grads: chip_sums("mid", grads, [GRAD_PAYLOAD_DTYPE] + [F32] * (len(grads) - 1))
    sq, dx, (sums_late, got_late), (sums_mid, got_mid), d_small = _local_step(
        x[0], loss_target[0], w_in4, ('halves', [halves[n] for n in LATE_NAMES]), {n: w[n] for n in SMALL_NAMES},
        reduce_late, reduce_mid)
    loss = lax.psum(0.5 * sq / d, ("x", "y", "c"))

    nbig = len(BIG_NAMES)
    sums_tiny = chip_sums("tiny", [_pack_small(d_small)], [F32])
    got_tiny = list(_scatter_chips("reduce_scatter_tiny", sums_tiny))
    sums = sums_mid[:1] + sums_late + sums_mid[1:] + sums_tiny
    got = got_mid[:1] + got_late + got_mid[1:] + got_tiny
    me = 2 * lax.axis_index("x") + lax.axis_index("y")
    mine = [_sum_chips("reduce_sum_%d" % a, gt, sm_, me) for a, (gt, sm_) in enumerate(zip(got, sums))]
    theirs = _swap_reduced("reduce_swap_reduced", mine)
    in_order = lambda a: jnp.where(c == 0, jnp.stack([mine[a], theirs[a]]), jnp.stack([theirs[a], mine[a]]))
    repl = _gather_chips("gather_small", [in_order(a) for a in range(nbig, len(mine))])
    repl = [r.reshape(-1, LANES) for r in repl]
    like = {n: w[n] for n in SMALL_NAMES}
    grad_small = _unpack_small(repl[-1], like)
    grad_small.update({n: r.reshape(w[n].shape) for n, r in zip(MEDIUM_NAMES, repl)})

    grad_big, delta, new_m, new_v = {}, {}, {}, {}
    for a, n in enumerate(BIG_NAMES):
        grad_big[n], delta[n], new_m[n], new_v[n] = _adamw_halves("adamw_%d" % a, w[n], mine[a], theirs[a],
                                                                  mom[n], var[n], c)
    for n in MEDIUM_NAMES:
        flat = lambda t: t.reshape(-1, w[n].shape[-1])
        res = _adamw("adamw_" + n, flat(w[n]), flat(grad_small[n]), flat(mom[n]), flat(var[n]))
        delta[n], new_m[n], new_v[n] = (t.reshape(w[n].shape) for t in res)
    sd, sm, sv = _adamw("adamw_small", _pack_small(w), repl[-1], _pack_small(mom), _pack_small(var))
    delta.update(_unpack_small(sd, like))
    new_m.update(_unpack_small(sm, like))
    new_v.update(_unpack_small(sv, like))
    grads = {**grad_big, **grad_small}
    lead = lambda t: t[None]
    return (loss, dx[None], *[lead(grads[n]) for n in WEIGHT_NAMES], *[lead(delta[n]) for n in WEIGHT_NAMES],
            *[lead(new_m[n]) for n in WEIGHT_NAMES], *[lead(new_v[n]) for n in WEIGHT_NAMES])
```

```python
import functools
import math

import jax
import jax.numpy as jnp
from jax import lax
from jax.experimental import pallas as pl
from jax.experimental.pallas import tpu as pltpu

F32 = jnp.float32
BF16 = jnp.bfloat16
MXU_DTYPE = BF16
GRAD_PAYLOAD_DTYPE = BF16
DW_DTYPE = BF16
S5_DTYPE = BF16
HI = lax.Precision.HIGHEST
VMEM_LIMIT_V7X = 56 * 1024 * 1024
LANES = 128
SUBLANES = 8

GRID_W = 64
WIN_H = 8
WIN_W = 16
HEAD_DIM = 64
SSM_GROUP_CH = 16
SSM_STATE = 64
S5_CHUNK = 16
S5_GROUPS_PER_STEP = 16
RMS_EPS = 1e-6
NEG_INF = -1e30
N_CHIPS = 4
MESH = pl.DeviceIdType.MESH

ADAM_LR = 0.001
ADAM_B1 = 0.9
ADAM_B2 = 0.999
ADAM_EPS = 1e-08
ADAM_WD = 0.01
ADAM_STEP = 10

WEIGHT_NAMES = ['g_mix', 'w_in', 'q_gain', 'k_gain', 'rpb', 'ssm_a_re', 'ssm_a_im', 'ssm_b_re', 'ssm_b_im',
                'ssm_c_re', 'ssm_c_im', 'ssm_log_step', 'ssm_d', 'w_glu', 'b_glu', 'g_out_attn', 'g_out_ssm',
                'w_out', 'g_ffn', 'w_ffn_gate', 'w_ffn_up', 'w_ffn_down']
BIG_NAMES = ['w_in', 'w_glu', 'w_out', 'w_ffn_gate', 'w_ffn_up', 'w_ffn_down']
LATE_NAMES = BIG_NAMES[1:]
SMALL_NAMES = [n for n in WEIGHT_NAMES if n not in BIG_NAMES]


def _cparams(sem):
    return pltpu.CompilerParams(dimension_semantics=sem, vmem_limit_bytes=VMEM_LIMIT_V7X)


def _tile(n, want):
    if n <= want:
        return n
    t = (want // LANES) * LANES
    while t >= LANES:
        if n % t == 0:
            return t
        t -= LANES
    return n


def _mm(name, a, b, *, contract, a_mode='2', b_mode='2', o_mode='2', out_dtype=F32, add=None, exact=False,
        tm=1024, tn=1024, tk=2048, comm=None, comm_arrs=()):
    dn = {'nn': (((1,), (0,)), ((), ())), 'nt': (((1,), (1,)), ((), ())), 'tn': (((0,), (0,)), ((), ()))}[contract]
    ar, ac = a.shape[-2:]
    br, bc = b.shape[-2:]
    m, kdim = (ar, ac) if contract != 'tn' else (ac, ar)
    n = bc if contract != 'nt' else br
    assert kdim == (br if contract != 'nt' else bc), (name, a.shape, b.shape)
    nbatch = 1
    for arr, mode in ((a, a_mode), (b, b_mode)):
        if mode == 'b':
            nbatch = arr.shape[0]
    nstack = 1
    for arr, mode in ((a, a_mode), (b, b_mode)):
        if mode == 'c':
            nstack = arr.shape[0]
    tm, tn, tk = _tile(m, tm), _tile(n, tn), _tile(kdim, tk)
    nkin = kdim // tk
    nk = nstack * nkin
    grid = (nbatch, m // tm, n // tn, nk)

    def spec(mode, block, rc):
        def imap(s, i, j, kk):
            r, c = rc(i, j, kk % nkin)
            if mode == '2':
                return (r, c)
            return (s if mode == 'b' else kk // nkin, r, c)
        return pl.BlockSpec(block if mode == '2' else (None,) + block, imap)

    a_spec = spec(a_mode, (tm, tk) if contract != 'tn' else (tk, tm),
                  (lambda i, j, k: (i, k)) if contract != 'tn' else (lambda i, j, k: (k, i)))
    b_spec = spec(b_mode, (tk, tn) if contract != 'nt' else (tn, tk),
                  (lambda i, j, k: (k, j)) if contract != 'nt' else (lambda i, j, k: (j, k)))
    o_spec = spec(o_mode, (tm, tn), lambda i, j, k: (i, j))
    out_shape = (m, n) if o_mode == '2' else (nbatch, m, n)
    has_add = add is not None

    def product(a_ref, b_ref):
        if exact:
            return lax.dot_general(a_ref[...].astype(F32), b_ref[...].astype(F32), dn, precision=HI,
                                   preferred_element_type=F32)
        return lax.dot_general(a_ref[...].astype(MXU_DTYPE), b_ref[...].astype(MXU_DTYPE), dn,
                               preferred_element_type=F32)

    ncomm = len(comm_arrs)
    nacc = int(nk > 1)

    def body(*refs):
        a_ref, b_ref = refs[:2]
        add_ref = refs[2] if has_add else None
        c_ins = refs[2 + has_add:2 + has_add + ncomm]
        o_ref = refs[2 + has_add + ncomm]
        c_outs = refs[3 + has_add + ncomm:3 + has_add + 2 * ncomm]
        sems = refs[3 + has_add + 2 * ncomm + nacc:]
        ids = [pl.program_id(ax) for ax in range(4)]
        if comm is not None:
            @pl.when((ids[0] == 0) & (ids[1] == 0) & (ids[2] == 0) & (ids[3] == 0))
            def _():
                comm.start(c_ins, c_outs, *sems)

        def write(r):
            if has_add:
                r = r + add_ref[...].astype(F32)
            o_ref[...] = r.astype(o_ref.dtype)

        if nk == 1:
            write(product(a_ref, b_ref))
        else:
            acc_ref = refs[3 + has_add + 2 * ncomm]

            @pl.when(ids[3] == 0)
            def _():
                acc_ref[...] = jnp.zeros_like(acc_ref)

            acc_ref[...] += product(a_ref, b_ref)

            @pl.when(ids[3] == nk - 1)
            def _():
                write(acc_ref[...])

        if comm is not None:
            @pl.when((ids[0] == grid[0] - 1) & (ids[1] == grid[1] - 1) & (ids[2] == grid[2] - 1) & (ids[3] == nk - 1))
            def _():
                comm.finish(c_ins, c_outs, *sems)

    in_specs = [a_spec, b_spec] + ([o_spec] if has_add else []) + [ANY] * ncomm
    args = (a, b) + ((add,) if has_add else ()) + tuple(comm_arrs)
    res = pl.pallas_call(
        body, name=name, grid=grid, in_specs=in_specs, out_specs=[o_spec] + [ANY] * ncomm,
        out_shape=[jax.ShapeDtypeStruct(out_shape, out_dtype)] + (comm.out_shape if comm is not None else []),
        scratch_shapes=([pltpu.VMEM((tm, tn), F32)] if nk > 1 else []) + (comm.scratch if comm is not None else []),
        compiler_params=_cparams(("parallel", "parallel", "parallel", "arbitrary") if comm is None
                                 else ("arbitrary",) * 4),
    )(*args)
    return res[0] if comm is None else (res[0], res[1:])


def _ew(name, fn, ins, outs, tr=256):
    rows = next(x[1].shape[0] for x in ins if x[0] == 'r')
    tr = min(tr, rows)
    assert rows % tr == 0 and tr % SUBLANES == 0, (name, rows, tr)
    in_specs, args = [], []
    for x in ins:
        if x[0] == 'r' and len(x) == 2:
            in_specs.append(pl.BlockSpec((tr, x[1].shape[1]), lambda i: (i, 0)))
        elif x[0] == 'r':
            in_specs.append(pl.BlockSpec((tr, x[3]), functools.partial(lambda cb, i: (i, cb), x[2])))
        else:
            in_specs.append(pl.BlockSpec(x[1].shape, lambda i: (0, 0)))
        args.append(x[1])
    out_specs, out_shapes = [], []
    for o in outs:
        if o[0] == 'r':
            out_specs.append(pl.BlockSpec((tr, o[1]), lambda i: (i, 0)))
            out_shapes.append(jax.ShapeDtypeStruct((rows, o[1]), o[2]))
        else:
            out_specs.append(pl.BlockSpec((SUBLANES, o[1]), lambda i: (0, 0)))
            out_shapes.append(jax.ShapeDtypeStruct((SUBLANES, o[1]), F32))
    nin = len(ins)
    has_acc = any(o[0] == 'a' for o in outs)

    def body(*refs):
        vals = fn(*[r[...].astype(F32) for r in refs[:nin]])
        if not isinstance(vals, (tuple, list)):
            vals = (vals,)
        i = pl.program_id(0)
        for o, ref, v in zip(outs, refs[nin:], vals):
            if o[0] == 'r':
                ref[...] = v.astype(ref.dtype)
            else:
                part = v.astype(F32).reshape(tr // SUBLANES, SUBLANES, o[1]).sum(axis=0)

                @pl.when(i == 0)
                def _(ref=ref, part=part):
                    ref[...] = part

                @pl.when(i > 0)
                def _(ref=ref, part=part):
                    ref[...] += part

    res = pl.pallas_call(
        body, name=name, grid=(rows // tr,), in_specs=in_specs, out_specs=out_specs, out_shape=out_shapes,
        compiler_params=_cparams(("arbitrary",) if has_acc else ("parallel",)),
    )(*args)
    return res


def _rms(x, g):
    r = lax.rsqrt(jnp.mean(x * x, axis=-1, keepdims=True) + RMS_EPS)
    xr = x * r
    return xr * g, xr


def _rms_bwd(x, g, dy):
    r = lax.rsqrt(jnp.mean(x * x, axis=-1, keepdims=True) + RMS_EPS)
    xr = x * r
    gdy = g * dy
    dx = r * (gdy - xr * jnp.mean(xr * gdy, axis=-1, keepdims=True))
    return dx, dy * xr


def _sigmoid(x):
    return 0.5 * (jnp.tanh(0.5 * x) + 1.0)


_GELU_C = math.sqrt(2.0 / math.pi)


def _gelu(x):
    return 0.5 * x * (1.0 + jnp.tanh(_GELU_C * (x + 0.044715 * x * x * x)))


def _gelu_grad(x):
    t = jnp.tanh(_GELU_C * (x + 0.044715 * x * x * x))
    return 0.5 * (1.0 + t) + 0.5 * x * (1.0 - t * t) * _GELU_C * (1.0 + 3 * 0.044715 * x * x)


ATTN_ROWS_PER_STEP = 16
NT_DIMS = (((1,), (1,)), ((), ()))
NN_DIMS = (((1,), (0,)), ((), ()))
TN_DIMS = (((0,), (0,)), ((), ()))


def _attn_geometry(r, rows):
    row_start = jnp.clip(r - WIN_H // 2, 0, rows - WIN_H)
    key0 = pl.multiple_of(row_start * GRID_W, GRID_W)
    bias0 = pl.multiple_of((row_start - r + (WIN_H - 1)) * GRID_W, GRID_W)
    return key0, bias0


def _window_onehot():
    c = jnp.arange(GRID_W)
    col_start = jnp.clip(c - WIN_W // 2, 0, GRID_W - WIN_W)
    col_in = (c[None, :] >= col_start[:, None]) & (c[None, :] < col_start[:, None] + WIN_W)
    dc = jnp.clip(c[None, :] - c[:, None], -(WIN_W - 1), WIN_W - 1) + (WIN_W - 1)
    onehot = ((dc[:, :, None] == jnp.arange(2 * WIN_W - 1)[None, None, :]) & col_in[:, :, None]).astype(F32)
    return onehot, col_in


def _bias_table(rpb):
    onehot, col_in = _window_onehot()
    nh = rpb.shape[0]
    pairs = rpb.reshape(nh // 2, 2, 2 * WIN_H - 1, 2 * WIN_W - 1)
    mask = jnp.where(col_in, 0.0, NEG_INF).T
    heads = [jnp.einsum('prd,qkd->prkq', pairs[:, e], onehot, precision=HI) + mask for e in range(2)]
    return jnp.concatenate(heads, axis=-1).reshape(nh // 2, (2 * WIN_H - 1) * GRID_W, 2 * GRID_W)


def _bias_table_grad(dtab):
    onehot, _ = _window_onehot()
    npair = dtab.shape[0]
    d = dtab.reshape(npair, 2 * WIN_H - 1, GRID_W, 2 * GRID_W)
    heads = [jnp.einsum('prkq,qkd->prd', d[..., e * GRID_W:(e + 1) * GRID_W], onehot, precision=HI) for e in range(2)]
    return jnp.stack(heads, axis=1).reshape(2 * npair, 2 * WIN_H - 1, 2 * WIN_W - 1)


def _lane_lo(shape):
    return lax.broadcasted_iota(jnp.int32, shape, 1) < HEAD_DIM


def _half_sums(v):
    lo = _lane_lo(v.shape)
    s_lo = jnp.sum(jnp.where(lo, v, 0.0), axis=1, keepdims=True)
    s_hi = jnp.sum(jnp.where(lo, 0.0, v), axis=1, keepdims=True)
    return jnp.where(lo, s_lo, s_hi)


def _rms_pair(x, g):
    r = lax.rsqrt(_half_sums(x * x) * (1.0 / HEAD_DIM) + RMS_EPS)
    return x * r * g


def _rms_pair_bwd(x, g, dy):
    r = lax.rsqrt(_half_sums(x * x) * (1.0 / HEAD_DIM) + RMS_EPS)
    xr = x * r
    gdy = g * dy
    dx = r * (gdy - xr * (_half_sums(xr * gdy) * (1.0 / HEAD_DIM)))
    return dx, dy * xr


def _blockdiag(a):
    a2 = jnp.concatenate([a, a], axis=0)
    row_hi = lax.broadcasted_iota(jnp.int32, a2.shape, 0) >= GRID_W
    lane_hi = lax.broadcasted_iota(jnp.int32, a2.shape, 1) >= HEAD_DIM
    return jnp.where(row_hi == lane_hi, a2, 0.0).astype(MXU_DTYPE)


def _diag_blocks(m):
    return jnp.where(_lane_lo((GRID_W, 2 * HEAD_DIM)), m[:GRID_W], m[GRID_W:])


def _attn_scores(qb, kb, bias):
    st = lax.dot_general(kb, qb, NT_DIMS, preferred_element_type=F32)
    st = st * (1.0 / math.sqrt(HEAD_DIM)) + bias
    mx = jnp.max(st, axis=0, keepdims=True)
    p = jnp.exp(st - mx)
    return p * (1.0 / jnp.sum(p, axis=0, keepdims=True))


def _attn_fwd(z4, qg2, kg2, bias_t, comm=None, comm_arrs=()):
    _, t, aw = z4.shape
    rows = t // GRID_W
    npair = aw // (2 * HEAD_DIM)
    nkeys = WIN_H * GRID_W
    nb = bias_t.shape[1]
    rps = min(ATTN_ROWS_PER_STEP, rows)
    blk = rps * GRID_W
    nsteps = rows // rps
    ncomm = len(comm_arrs)

    def body(*refs):
        q_ref, k_ref, v_ref, qg_ref, kg_ref, b_ref = refs[:6]
        c_ins, o_ref, c_outs = refs[6:6 + ncomm], refs[6 + ncomm], refs[7 + ncomm:7 + 2 * ncomm]
        kn_ref, vb_ref = refs[7 + 2 * ncomm:9 + 2 * ncomm]
        sems = refs[9 + 2 * ncomm:]
        pr, rb = pl.program_id(0), pl.program_id(1)
        if comm is not None:
            @pl.when((pr == 0) & (rb == 0))
            def _():
                comm.start(c_ins, c_outs, *sems)

        @pl.when(rb == 0)
        def _():
            kn_ref[...] = _rms_pair(k_ref[...], kg_ref[...]).astype(MXU_DTYPE)
            vb_ref[...] = v_ref[...].astype(MXU_DTYPE)

        def row(i, carry):
            key0, bias0 = _attn_geometry(rb * rps + i, rows)
            at = pl.ds(pl.multiple_of(i * GRID_W, GRID_W), GRID_W)
            qb = _blockdiag(_rms_pair(q_ref[at, :], qg_ref[...]))
            pt = _attn_scores(qb, kn_ref[pl.ds(key0, nkeys), :], b_ref[pl.ds(bias0, nkeys), :])
            both = lax.dot_general(pt.astype(MXU_DTYPE), vb_ref[pl.ds(key0, nkeys), :], TN_DIMS,
                                   preferred_element_type=F32)
            o_ref[at, :] = _diag_blocks(both)
            return carry

        lax.fori_loop(0, rps, row, 0, unroll=4)
        if comm is not None:
            @pl.when((pr == npair - 1) & (rb == nsteps - 1))
            def _():
                comm.finish(c_ins, c_outs, *sems)

    pair_cols = lambda lead: pl.BlockSpec((None, t, 2 * HEAD_DIM), lambda p, r: (lead, 0, p))
    res = pl.pallas_call(
        body, name="attn_fwd", grid=(npair, nsteps),
        in_specs=[pl.BlockSpec((None, blk, 2 * HEAD_DIM), lambda p, r: (0, r, p)), pair_cols(1), pair_cols(2),
                  pl.BlockSpec((1, 2 * HEAD_DIM), lambda p, r: (0, 0)),
                  pl.BlockSpec((1, 2 * HEAD_DIM), lambda p, r: (0, 0)),
                  pl.BlockSpec((None, nb, 2 * GRID_W), lambda p, r: (p, 0, 0))] + [ANY] * ncomm,
        out_specs=[pl.BlockSpec((blk, 2 * HEAD_DIM), lambda p, r: (r, p))] + [ANY] * ncomm,
        out_shape=[jax.ShapeDtypeStruct((t, aw), F32)] + (comm.out_shape if comm is not None else []),
        scratch_shapes=[pltpu.VMEM((t, 2 * HEAD_DIM), MXU_DTYPE), pltpu.VMEM((t, 2 * HEAD_DIM), MXU_DTYPE)]
        + (comm.scratch if comm is not None else []),
        compiler_params=_cparams(("arbitrary", "arbitrary")),
    )(z4, z4, z4, qg2, kg2, bias_t, *comm_arrs)
    return res[0], res[1:]


def _attn_bwd(z4, qg2, kg2, bias_t, dya, comm=None, comm_arrs=()):
    _, t, aw = z4.shape
    rows = t // GRID_W
    npair = aw // (2 * HEAD_DIM)
    nkeys = WIN_H * GRID_W
    nb = bias_t.shape[1]
    rps = min(ATTN_ROWS_PER_STEP, rows)
    blk = rps * GRID_W
    nsteps = rows // rps
    scale = 1.0 / math.sqrt(HEAD_DIM)
    ncomm = len(comm_arrs)

    def body(*refs):
        q_ref, k_ref, v_ref, qg_ref, kg_ref, b_ref, do_ref = refs[:7]
        c_ins = refs[7:7 + ncomm]
        dz_ref, db_ref, dqg_ref, dkg_ref = refs[7 + ncomm:11 + ncomm]
        c_outs = refs[11 + ncomm:11 + 2 * ncomm]
        kn_ref, vb_ref, dkn_ref, dv_ref = refs[11 + 2 * ncomm:15 + 2 * ncomm]
        sems = refs[15 + 2 * ncomm:]
        pr, rb = pl.program_id(0), pl.program_id(1)
        if comm is not None:
            @pl.when((pr == 0) & (rb == 0))
            def _():
                comm.start(c_ins, c_outs, *sems)

        @pl.when(rb == 0)
        def _():
            kn_ref[...] = _rms_pair(k_ref[...], kg_ref[...]).astype(MXU_DTYPE)
            vb_ref[...] = v_ref[...].astype(MXU_DTYPE)
            dkn_ref[...] = jnp.zeros_like(dkn_ref)
            dv_ref[...] = jnp.zeros_like(dv_ref)
            db_ref[...] = jnp.zeros_like(db_ref)
            dqg_ref[...] = jnp.zeros_like(dqg_ref)

        def row(i, dqg_sum):
            r = rb * rps + i
            key0, bias0 = _attn_geometry(r, rows)
            keys = pl.ds(key0, nkeys)
            at = pl.ds(pl.multiple_of(i * GRID_W, GRID_W), GRID_W)
            q = q_ref[at, :]
            qb = _blockdiag(_rms_pair(q, qg_ref[...]))
            dob = _blockdiag(do_ref[at, :])
            kb = kn_ref[keys, :]
            pt = _attn_scores(qb, kb, b_ref[pl.ds(bias0, nkeys), :])
            dv_ref[keys, :] += lax.dot_general(pt.astype(MXU_DTYPE), dob, NN_DIMS, preferred_element_type=F32)
            dpt = lax.dot_general(vb_ref[keys, :], dob, NT_DIMS, preferred_element_type=F32)
            dst = pt * (dpt - jnp.sum(pt * dpt, axis=0, keepdims=True))
            db_ref[pl.ds(bias0, nkeys), :] += dst
            dsb = dst.astype(MXU_DTYPE)
            dkn_ref[keys, :] += scale * lax.dot_general(dsb, qb, NN_DIMS, preferred_element_type=F32)
            dqn = scale * _diag_blocks(lax.dot_general(dsb, kb, TN_DIMS, preferred_element_type=F32))
            dq, dqg = _rms_pair_bwd(q, qg_ref[...], dqn)
            dz_ref[0, pl.ds(pl.multiple_of(r * GRID_W, GRID_W), GRID_W), :] = dq.astype(dz_ref.dtype)
            return dqg_sum + jnp.sum(dqg, axis=0, keepdims=True)

        def rows4(i, acc):
            for j in range(4):
                acc = row(4 * i + j, acc)
            return acc

        dqg_ref[...] += lax.fori_loop(0, rps // 4, rows4, jnp.zeros((1, 2 * HEAD_DIM), F32))

        @pl.when(rb == nsteps - 1)
        def _():
            dk, dkg = _rms_pair_bwd(k_ref[...], kg_ref[...], dkn_ref[...])
            dz_ref[1] = dk.astype(dz_ref.dtype)
            dz_ref[2] = dv_ref[...].astype(dz_ref.dtype)
            dkg_ref[...] = jnp.sum(dkg, axis=0, keepdims=True)

        if comm is not None:
            @pl.when((pr == npair - 1) & (rb == nsteps - 1))
            def _():
                comm.finish(c_ins, c_outs, *sems)

    pair_cols = lambda lead: pl.BlockSpec((None, t, 2 * HEAD_DIM), lambda p, r: (lead, 0, p))
    pair_vec = pl.BlockSpec((None, 1, 2 * HEAD_DIM), lambda p, r: (p, 0, 0))
    res = pl.pallas_call(
        body, name="attn_bwd", grid=(npair, nsteps),
        in_specs=[pl.BlockSpec((None, blk, 2 * HEAD_DIM), lambda p, r: (0, r, p)), pair_cols(1), pair_cols(2),
                  pl.BlockSpec((1, 2 * HEAD_DIM), lambda p, r: (0, 0)),
                  pl.BlockSpec((1, 2 * HEAD_DIM), lambda p, r: (0, 0)),
                  pl.BlockSpec((None, nb, 2 * GRID_W), lambda p, r: (p, 0, 0)),
                  pl.BlockSpec((blk, 2 * HEAD_DIM), lambda p, r: (r, p))] + [ANY] * ncomm,
        out_specs=[pl.BlockSpec((3, t, 2 * HEAD_DIM), lambda p, r: (0, 0, p)),
                   pl.BlockSpec((None, nb, 2 * GRID_W), lambda p, r: (p, 0, 0)),
                   pair_vec, pair_vec] + [ANY] * ncomm,
        out_shape=[jax.ShapeDtypeStruct((4, t, aw), MXU_DTYPE), jax.ShapeDtypeStruct(bias_t.shape, F32),
                   jax.ShapeDtypeStruct((npair, 1, 2 * HEAD_DIM), F32),
                   jax.ShapeDtypeStruct((npair, 1, 2 * HEAD_DIM), F32)] + (comm.out_shape if comm is not None else []),
        scratch_shapes=[pltpu.VMEM((t, 2 * HEAD_DIM), MXU_DTYPE), pltpu.VMEM((t, 2 * HEAD_DIM), MXU_DTYPE),
                        pltpu.VMEM((t, 2 * HEAD_DIM), F32), pltpu.VMEM((t, 2 * HEAD_DIM), F32)]
        + (comm.scratch if comm is not None else []),
        compiler_params=_cparams(("arbitrary", "arbitrary")),
    )(z4, z4, z4, qg2, kg2, bias_t, dya, *comm_arrs)
    return res[:4], res[4:]


def _bmm_exact(name, a, b, dims, per_step=16):
    nb = a.shape[0]
    per = math.gcd(nb, per_step)
    shape = jax.eval_shape(lambda u, v: lax.dot_general(u, v, dims), a[0], b[0]).shape

    def body(a_ref, b_ref, o_ref):
        for e in range(per):
            o_ref[e] = lax.dot_general(a_ref[e], b_ref[e], dims, precision=HI, preferred_element_type=F32)

    blk = lambda arr: pl.BlockSpec((per,) + arr.shape[1:], lambda i: (i, 0, 0))
    out = jax.ShapeDtypeStruct((nb,) + shape, F32)
    return pl.pallas_call(body, name=name, grid=(nb // per,), in_specs=[blk(a), blk(b)], out_specs=blk(out),
                          out_shape=out, compiler_params=_cparams(("parallel",)))(a, b)


@jax.custom_vjp
def _contract_last(a, b):
    return _bmm_exact("s5_kern", a, b, NT_DIMS)


def _contract_last_fwd(a, b):
    return _contract_last(a, b), (a, b)


def _contract_last_bwd(res, g):
    a, b = res
    return _bmm_exact("s5_kern_da", g, b, NN_DIMS), _bmm_exact("s5_kern_db", g, a, TN_DIMS)


_contract_last.defvjp(_contract_last_fwd, _contract_last_bwd)


def _s5_mats(a_re, a_im, b_re, b_im, c_re, c_im, log_step, d_skip):
    nd, g, p = a_re.shape
    c = b_re.shape[-1]
    L = S5_CHUNK
    lr = jnp.minimum(a_re, -1e-4).transpose(1, 0, 2)
    li = a_im.transpose(1, 0, 2)
    dt = jnp.exp(log_step).T[..., None]
    n = jnp.arange(L + 1, dtype=F32)[None, :, None, None]
    mag = jnp.exp(n * (lr * dt)[:, None])
    ang = n * (li * dt)[:, None]
    pw_r, pw_i = mag * jnp.cos(ang), mag * jnp.sin(ang)
    den = lr * lr + li * li
    nr, ni = pw_r[:, 1] - 1.0, pw_i[:, 1]
    cr, ci = (nr * lr + ni * li) / den, (ni * lr - nr * li) / den
    bt_r, bt_i = b_re.transpose(1, 3, 0, 2), b_im.transpose(1, 3, 0, 2)
    bb_r = cr[:, None] * bt_r - ci[:, None] * bt_i
    bb_i = cr[:, None] * bt_i + ci[:, None] * bt_r
    ct_r, ct_i = c_re.transpose(1, 2, 0, 3), c_im.transpose(1, 2, 0, 3)

    def cols(x_re, x_im):
        return jnp.concatenate([x_re[..., 0, :], x_re[..., 1, :], x_im[..., 0, :], x_im[..., 1, :]], axis=-1)

    e_r = jnp.stack([pw_r[:, :L, 0][:, ::-1], pw_r[:, :L, 1]], axis=2)
    e_i = jnp.stack([pw_i[:, :L, 0][:, ::-1], pw_i[:, :L, 1]], axis=2)
    ws = (cols(e_r, e_r)[:, :, None] * cols(bb_r, bb_i)[:, None]
          + cols(e_i, e_i)[:, :, None] * cols(-bb_i, bb_r)[:, None]).reshape(g, L * c, 4 * p)
    f_r = jnp.stack([pw_r[:, 1:, 0], pw_r[:, 1:, 1][:, ::-1]], axis=2)
    f_i = jnp.stack([pw_i[:, 1:, 0], pw_i[:, 1:, 1][:, ::-1]], axis=2)
    wot = (cols(f_r, f_i)[:, :, None] * cols(ct_r, -ct_r)[:, None]
           + cols(f_i, f_r)[:, :, None] * cols(-ct_i, -ct_i)[:, None]).reshape(g, L * c, 4 * p)
    qr, qi = pw_r[:, None, :L], pw_i[:, None, :L]
    br, bi = bb_r[:, :, None], bb_i[:, :, None]
    kp_r, kp_i = qr * br - qi * bi, qr * bi + qi * br
    lhs = jnp.stack([jnp.concatenate([kp_r[..., d, :], -kp_i[..., d, :]], axis=-1) for d in range(2)])
    rhs = jnp.stack([jnp.concatenate([ct_r[:, :, d], ct_i[:, :, d]], axis=-1) for d in range(2)])
    kern = _contract_last(lhs.reshape(2 * g, c * L, 2 * p), rhs.reshape(2 * g, c, 2 * p)).reshape(2, g, c, L, c)
    skip = d_skip.reshape(g, c, 1, 1) * jnp.eye(c, dtype=F32)[None, :, None, :]
    by_offset = jnp.concatenate([kern[1][:, :, :0:-1], kern[0][:, :, :1] + kern[1][:, :, :1] + skip,
                                 kern[0][:, :, 1:]], axis=2).reshape(g, c, (2 * L - 1) * c)
    mt = jnp.stack([by_offset[:, :, (L - 1 - j) * c:(2 * L - 1 - j) * c] for j in range(L)], axis=1)
    mt = mt.reshape(g, L * c, L * c)
    lr16, li16 = pw_r[:, L], pw_i[:, L]
    fa = jnp.concatenate([lr16[:, 0], lr16[:, 1], lr16[:, 0], lr16[:, 1]], axis=-1)
    fb = jnp.concatenate([-li16[:, 0], -li16[:, 1], li16[:, 0], li16[:, 1]], axis=-1)
    return mt, ws, wot, fa, fb


def _gmm(name, a, b, contract, a_stacked=False, b_stacked=False, o_stacked=False, add=None, out_dtype=F32):
    w = S5_CHUNK * SSM_GROUP_CH
    g = (a.shape[0] if a_stacked else a.shape[1] // w)
    gpb = math.gcd(g, S5_GROUPS_PER_STEP)
    dn = {'nn': (((1,), (0,)), ((), ())), 'nt': (((1,), (1,)), ((), ())), 'tn': (((0,), (0,)), ((), ()))}[contract]

    def spec(arr, stacked):
        if stacked:
            return pl.BlockSpec((gpb,) + arr.shape[1:], lambda i: (i, 0, 0))
        return pl.BlockSpec((arr.shape[0], gpb * w), lambda i: (0, i))

    def take(ref, stacked, e):
        return ref[e] if stacked else ref[:, e * w:(e + 1) * w]

    m = (a.shape[1] if a_stacked else a.shape[0]) if contract != 'tn' else w
    n = w
    if o_stacked:
        o_spec = pl.BlockSpec((gpb, m, n), lambda i: (i, 0, 0))
        o_shape = (g, m, n)
    else:
        o_spec = pl.BlockSpec((m, gpb * n), lambda i: (0, i))
        o_shape = (m, g * n)
    has_add = add is not None

    def body(*refs):
        if has_add:
            a_ref, b_ref, add_ref, o_ref = refs
        else:
            a_ref, b_ref, o_ref = refs
        for e in range(gpb):
            r = lax.dot_general(take(a_ref, a_stacked, e).astype(S5_DTYPE), take(b_ref, b_stacked, e).astype(S5_DTYPE),
                                dn, precision=HI if S5_DTYPE == F32 else None, preferred_element_type=F32)
            if has_add:
                r = r + take(add_ref, o_stacked, e)
            if o_stacked:
                o_ref[e] = r.astype(o_ref.dtype)
            else:
                o_ref[:, e * w:(e + 1) * w] = r.astype(o_ref.dtype)

    in_specs = [spec(a, a_stacked), spec(b, b_stacked)] + ([o_spec] if has_add else [])
    return pl.pallas_call(
        body, name=name, grid=(g // gpb,), in_specs=in_specs, out_specs=o_spec,
        out_shape=jax.ShapeDtypeStruct(o_shape, out_dtype), compiler_params=_cparams(("parallel",)),
    )(*((a, b) + ((add,) if has_add else ())))


def _s5_scan(name, s, fa, fb, rev0, xin=None):
    nk, g, w = s.shape
    hw, qw = w // 2, w // 4
    gb = min(g, 16)
    with_acc = xin is not None

    def body(*refs):
        if with_acc:
            s_ref, a_ref, b_ref, x_ref, o_ref, pa_ref, pb_ref = refs
        else:
            s_ref, a_ref, b_ref, o_ref = refs
        fa_v, fb_v = a_ref[...], b_ref[...]
        dir0 = lax.broadcasted_iota(jnp.int32, (gb, w), 1) % hw < qw
        swap = lambda v: jnp.concatenate([v[:, hw:], v[:, :hw]], axis=1)

        def step(i, carry):
            x, pa, pb = carry
            k0 = (nk - 1 - i) if rev0 else i
            k1 = i if rev0 else (nk - 1 - i)
            for lo in (0, hw):
                o_ref[k0, :, lo:lo + qw] = x[:, lo:lo + qw]
                o_ref[k1, :, lo + qw:lo + hw] = x[:, lo + qw:lo + hw]
            if with_acc:
                xi = jnp.where(dir0, x_ref[k0], x_ref[k1])
                pa = pa + x * xi
                pb = pb + x * swap(xi)
            x = fa_v * x + fb_v * swap(x) + jnp.where(dir0, s_ref[k0], s_ref[k1])
            return x, pa, pb

        z = jnp.zeros((gb, w), F32)
        res = lax.fori_loop(0, nk, step, (z, z, z), unroll=2)
        if with_acc:
            pa_ref[...] = res[1]
            pb_ref[...] = res[2]

    seq = pl.BlockSpec((nk, gb, w), lambda i: (0, i, 0))
    vec = pl.BlockSpec((gb, w), lambda i: (i, 0))
    in_specs = [seq, vec, vec] + ([seq] if with_acc else [])
    out_specs = [seq] + ([vec, vec] if with_acc else [])
    out_shape = [jax.ShapeDtypeStruct((nk, g, w), F32)] + (
        [jax.ShapeDtypeStruct((g, w), F32)] * 2 if with_acc else [])
    return pl.pallas_call(
        body, name=name, grid=(g // gb,), in_specs=in_specs, out_specs=out_specs, out_shape=out_shape,
        compiler_params=_cparams(("parallel",)),
    )(*((s, fa, fb) + ((xin,) if with_acc else ())))


def _regroup(name, x, to_groups):
    if to_groups:
        t, sw = x.shape
    else:
        t, sw = x.shape[0] * S5_CHUNK, x.shape[1] // S5_CHUNK
    nk = t // S5_CHUNK
    wide = LANES * S5_CHUNK

    def place(tok):
        r = lax.broadcasted_iota(jnp.int32, (2 * LANES, wide), 0)
        col = lax.broadcasted_iota(jnp.int32, (2 * LANES, wide), 1)
        ch = r % LANES
        want = (ch // SSM_GROUP_CH) * (S5_CHUNK * SSM_GROUP_CH) + (tok + r // LANES) * SSM_GROUP_CH + ch % SSM_GROUP_CH
        return (col == want).astype(S5_DTYPE)

    def body(x_ref, o_ref):
        token = lambda tok: (pl.ds(tok, nk, stride=S5_CHUNK), slice(None))
        if to_groups:
            acc = jnp.zeros((nk, wide), F32)
            for tok in range(0, S5_CHUNK, 2):
                rows = jnp.concatenate([x_ref[token(tok)], x_ref[token(tok + 1)]], axis=1).astype(S5_DTYPE)
                acc = acc + lax.dot_general(rows, place(tok), NN_DIMS, preferred_element_type=F32)
            o_ref[...] = acc.astype(o_ref.dtype)
        else:
            xv = x_ref[...].astype(S5_DTYPE)
            for tok in range(0, S5_CHUNK, 2):
                both = lax.dot_general(xv, place(tok), NT_DIMS, preferred_element_type=F32).astype(o_ref.dtype)
                o_ref[token(tok)] = both[:, :LANES]
                o_ref[token(tok + 1)] = both[:, LANES:]

    tokens = pl.BlockSpec((t, LANES), lambda i: (0, i))
    groups = pl.BlockSpec((nk, wide), lambda i: (0, i))
    return pl.pallas_call(
        body, name=name, grid=(sw // LANES,), in_specs=[tokens if to_groups else groups],
        out_specs=groups if to_groups else tokens,
        out_shape=jax.ShapeDtypeStruct((nk, sw * S5_CHUNK), S5_DTYPE) if to_groups else jax.ShapeDtypeStruct((t, sw), F32),
        compiler_params=_cparams(("parallel",)),
    )(x)


def _s5_fwd(u2, mats):
    mt, ws, wot, fa, fb = mats
    nk = u2.shape[0]
    g = mt.shape[0]
    y_intra = _gmm("s5_intra", u2, mt, 'nn', b_stacked=True)
    s = _gmm("s5_chunk_state", u2, ws, 'nn', b_stacked=True)
    (xin,) = _s5_scan("s5_scan", s.reshape(nk, g, -1), fa, fb, False)
    xin = xin.reshape(nk, -1)
    return _gmm("s5_inter", xin, wot, 'nt', b_stacked=True, add=y_intra, out_dtype=S5_DTYPE), xin


def _s5_bwd(u2, xin, mats, dy2):
    mt, ws, wot, fa, fb = mats
    nk = u2.shape[0]
    g = mt.shape[0]
    dxin = _gmm("s5_dxin", dy2, wot, 'nn', b_stacked=True)
    ds, pa, pb = _s5_scan("s5_scan_adj", dxin.reshape(nk, g, -1), fa, -fb, True, xin=xin.reshape(nk, g, -1))
    ds = ds.reshape(nk, -1)
    du_a = _gmm("s5_du_intra", dy2, mt, 'nt', b_stacked=True)
    du2 = _gmm("s5_du_state", ds, ws, 'nt', b_stacked=True, add=du_a, out_dtype=S5_DTYPE)
    dmt = _gmm("s5_dmt", u2, dy2, 'tn', o_stacked=True)
    dws = _gmm("s5_dws", u2, ds, 'tn', o_stacked=True)
    dwot = _gmm("s5_dwot", dy2, xin, 'tn', o_stacked=True)
    return du2, (dmt, dws, dwot, pa, pb)


def _stacked(g4):
    return g4.reshape((N_CHIPS, -1, g4.shape[-1]))


def _local_step(x, target, w_in4, late, small, reduce_late=None, reduce_mid=None):
    t, d = x.shape
    aw = w_in4.shape[2]
    sw = aw
    nh = aw // HEAD_DIM
    row = lambda v: v.reshape(1, -1)
    g_mix, g_ffn = row(small['g_mix']), row(small['g_ffn'])
    g_oa, g_os, b_glu = row(small['g_out_attn']), row(small['g_out_ssm']), row(small['b_glu'])
    qg2 = jnp.tile(row(small['q_gain']), (1, 2))
    kg2 = jnp.tile(row(small['k_gain']), (1, 2))

    (h,) = _ew("rms_mix", lambda xv, g: _rms(xv, g)[0], [('r', x), ('c', g_mix)], [('r', d, MXU_DTYPE)])
    bias_t = _bias_table(small['rpb'])
    if late[0] == 'halves':
        under_in, under_attn, under_gate, under_up = late[1][:2], late[1][2:3], late[1][3:4], late[1][4:]
        z4, got_in = _mm("in_proj", h, w_in4, contract='nn', b_mode='b', o_mode='b',
                         comm=_GatherChips(under_in), comm_arrs=under_in)
        ya, got_attn = _attn_fwd(z4, qg2, kg2, bias_t, comm=_GatherChips(under_attn), comm_arrs=under_attn)
        w_glu, w_out = (_stacked(g4).reshape(-1, g4.shape[-1]) for g4 in got_in)
        w_gate4 = _stacked(got_attn[0])
    else:
        z4 = _mm("in_proj", h, w_in4, contract='nn', b_mode='b', o_mode='b')
        ya, _ = _attn_fwd(z4, qg2, kg2, bias_t)
        w_glu, w_out, w_gate4, w_up4, w_down4 = late[1]
    ffs = w_gate4.shape[2]
    s5_params = tuple(small[n] for n in ('ssm_a_re', 'ssm_a_im', 'ssm_b_re', 'ssm_b_im', 'ssm_c_re', 'ssm_c_im',
                                         'ssm_log_step', 'ssm_d'))
    mats, mats_vjp = jax.vjp(_s5_mats, *s5_params)
    mats = tuple(m.astype(S5_DTYPE) for m in mats[:3]) + mats[3:]
    u2 = _regroup("s5_group_u", z4[3], True)
    ypre2, xin = _s5_fwd(u2, mats)
    ypre = _regroup("s5_ungroup_y", ypre2, False)
    (yb,) = _ew("gelu", _gelu, [('r', ypre)], [('r', sw, MXU_DTYPE)])
    a_glu = _mm("glu_proj", yb, w_glu, contract='nn')

    def mix_out(yav, ypv, av, bg, goa, gos):
        ys = _gelu(ypv) * _sigmoid(av + bg)
        return jnp.concatenate([_rms(yav, goa)[0], _rms(ys, gos)[0]], axis=1)
    (ycat,) = _ew("mix_out", mix_out, [('r', ya), ('r', ypre), ('r', a_glu), ('c', b_glu), ('c', g_oa), ('c', g_os)],
                  [('r', aw + sw, MXU_DTYPE)])
    x1 = _mm("out_proj", ycat, w_out, contract='nn', add=x)
    (h2,) = _ew("rms_ffn", lambda xv, g: _rms(xv, g)[0], [('r', x1), ('c', g_ffn)], [('r', d, MXU_DTYPE)])
    if late[0] == 'halves':
        gate4, got_gate = _mm("ffn_gate", h2, w_gate4, contract='nn', b_mode='b', o_mode='b', tn=ffs,
                              out_dtype=MXU_DTYPE, comm=_GatherChips(under_gate), comm_arrs=under_gate)
        w_up4 = _stacked(got_gate[0])
        up4, got_up = _mm("ffn_up", h2, w_up4, contract='nn', b_mode='b', o_mode='b', tn=ffs,
                          out_dtype=MXU_DTYPE, comm=_GatherChips(under_up), comm_arrs=under_up)
        w_down4 = _stacked(got_up[0])
    else:
        gate4 = _mm("ffn_gate", h2, w_gate4, contract='nn', b_mode='b', o_mode='b', tn=ffs, out_dtype=MXU_DTYPE)
        up4 = _mm("ffn_up", h2, w_up4, contract='nn', b_mode='b', o_mode='b', tn=ffs, out_dtype=MXU_DTYPE)
    gate_f, up_f = gate4.reshape(4 * t, ffs), up4.reshape(4 * t, ffs)
    (act,) = _ew("swiglu", lambda gv, uv: gv * _sigmoid(gv) * uv, [('r', gate_f), ('r', up_f)],
                 [('r', ffs, MXU_DTYPE)], tr=512)
    act4 = act.reshape(4, t, ffs)
    x2 = _mm("ffn_down", act4, w_down4, contract='nn', a_mode='c', b_mode='c', add=x1, tk=ffs)

    def loss_fn(xv, tv):
        diff = xv - tv
        return diff * (1.0 / d), diff * (1.0 / d), diff * diff
    dx2, dx2_b, sq = _ew("loss", loss_fn, [('r', x2), ('r', target)], [('r', d, F32), ('r', d, MXU_DTYPE), ('a', d)])

    dact4 = _mm("ffn_down_dx", dx2_b, w_down4, contract='nt', b_mode='b', o_mode='b', tn=ffs, out_dtype=MXU_DTYPE)
    d_w_down4 = _mm("ffn_down_dw", act4, dx2_b, contract='tn', a_mode='b', o_mode='b', tm=ffs, out_dtype=DW_DTYPE)

    def swiglu_bwd(dav, gv, uv):
        s = _sigmoid(gv)
        return dav * uv * s * (1.0 + gv * (1.0 - s)), dav * gv * s
    dgate, dup = _ew("swiglu_bwd", swiglu_bwd, [('r', dact4.reshape(4 * t, ffs)), ('r', gate_f), ('r', up_f)],
                     [('r', ffs, MXU_DTYPE), ('r', ffs, MXU_DTYPE)], tr=512)
    dgate4, dup4 = dgate.reshape(4, t, ffs), dup.reshape(4, t, ffs)
    if reduce_late is not None:
        sums_down = reduce_late("down", [d_w_down4])
        dh2, got_down = _mm("ffn_gate_dx", dgate4, w_gate4, contract='nt', a_mode='c', b_mode='c', tk=ffs, tn=2048,
                            comm=_ScatterChips(sums_down), comm_arrs=sums_down)
    else:
        dh2 = _mm("ffn_gate_dx", dgate4, w_gate4, contract='nt', a_mode='c', b_mode='c', tk=ffs, tn=2048)
    dh2 = _mm("ffn_up_dx", dup4, w_up4, contract='nt', a_mode='c', b_mode='c', add=dh2, tk=ffs)
    d_w_gate4 = _mm("ffn_gate_dw", h2, dgate4, contract='tn', b_mode='b', o_mode='b', tn=ffs, out_dtype=DW_DTYPE)
    if reduce_late is not None:
        sums_gate = reduce_late("gate", [d_w_gate4])
        d_w_up4, got_gate = _mm("ffn_up_dw", h2, dup4, contract='tn', b_mode='b', o_mode='b', tn=ffs,
                                out_dtype=DW_DTYPE, comm=_ScatterChips(sums_gate), comm_arrs=sums_gate)
    else:
        d_w_up4 = _mm("ffn_up_dw", h2, dup4, contract='tn', b_mode='b', o_mode='b', tn=ffs, out_dtype=DW_DTYPE)

    def rms_res_bwd(xv, g, dyv, resv):
        dx, dg = _rms_bwd(xv, g, dyv)
        return resv + dx, dg
    dx1, d_g_ffn = _ew("rms_ffn_bwd", rms_res_bwd, [('r', x1), ('c', g_ffn), ('r', dh2), ('r', dx2)],
                       [('r', d, F32), ('a', d)])

    dycat = _mm("out_proj_dx", dx1, w_out, contract='nt', out_dtype=MXU_DTYPE)
    d_w_out = _mm("out_proj_dw", ycat, dx1, contract='tn', out_dtype=DW_DTYPE)

    def mix_out_bwd(yav, ypv, av, bg, goa, gos, dca, dcs):
        dya, dgoa = _rms_bwd(yav, goa, dca)
        y = _gelu(ypv)
        s = _sigmoid(av + bg)
        dys, dgos = _rms_bwd(y * s, gos, dcs)
        da = dys * y * s * (1.0 - s)
        return dya, da, dys * s, dgoa, dgos, da
    dya, da, dy_direct, d_g_oa, d_g_os, d_b_glu = _ew(
        "mix_out_bwd", mix_out_bwd,
        [('r', ya), ('r', ypre), ('r', a_glu), ('c', b_glu), ('c', g_oa), ('c', g_os),
         ('r', dycat, 0, aw), ('r', dycat, 1, sw)],
        [('r', aw, F32), ('r', sw, MXU_DTYPE), ('r', sw, F32), ('a', aw), ('a', sw), ('a', sw)])
    dy = _mm("glu_proj_dx", da, w_glu, contract='nt', add=dy_direct)
    d_w_glu = _mm("glu_proj_dw", yb, da, contract='tn', out_dtype=DW_DTYPE)
    (dypre,) = _ew("gelu_bwd", lambda dyv, ypv: dyv * _gelu_grad(ypv), [('r', dy), ('r', ypre)],
                   [('r', sw, F32)])

    du2, dmats = _s5_bwd(u2, xin, mats, _regroup("s5_group_dy", dypre, True))
    d_s5 = mats_vjp(dmats)
    du = _regroup("s5_ungroup_du", du2, False)
    d_late = (d_w_glu, d_w_out, d_w_gate4, d_w_up4, d_w_down4)
    if reduce_late is not None:
        sums = reduce_late("late", [d_w_glu, d_w_out, d_w_up4])
        (dz4, dbias_t, dqg, dkg), scattered = _attn_bwd(z4, qg2, kg2, bias_t, dya, comm=_ScatterChips(sums),
                                                       comm_arrs=sums)
        d_late = (sums[:2] + sums_gate + sums[2:] + sums_down,
                  list(scattered[:2]) + list(got_gate) + list(scattered[2:]) + list(got_down))
    else:
        (dz4, dbias_t, dqg, dkg), _ = _attn_bwd(z4, qg2, kg2, bias_t, dya)
    d_rpb = _bias_table_grad(dbias_t)
    fold = lambda v: v.reshape(-1, 2, HEAD_DIM).sum(axis=(0, 1))
    dz4 = dz4.at[3].set(du.astype(dz4.dtype))

    d_w_in4 = _mm("in_proj_dw", h, dz4, contract='tn', b_mode='b', o_mode='b', out_dtype=DW_DTYPE)
    d_mid = [d_w_in4] + [d_s5[i].reshape(-1, LANES) for i in (2, 3, 4, 5)]
    if reduce_late is not None:
        sums = reduce_mid(d_mid)
        dh, scattered = _mm("in_proj_dx", dz4, w_in4, contract='nt', a_mode='c', b_mode='c',
                            comm=_ScatterChips(sums), comm_arrs=sums)
        d_mid = (sums, list(scattered))
    else:
        dh = _mm("in_proj_dx", dz4, w_in4, contract='nt', a_mode='c', b_mode='c')
    dx, d_g_mix = _ew("rms_mix_bwd", rms_res_bwd, [('r', x), ('c', g_mix), ('r', dh), ('r', dx1)],
                      [('r', d, F32), ('a', d)])

    colsum = lambda v: v.sum(axis=0)
    d_small = {
        'g_mix': colsum(d_g_mix), 'q_gain': fold(dqg), 'k_gain': fold(dkg), 'rpb': d_rpb,
        'ssm_a_re': d_s5[0], 'ssm_a_im': d_s5[1], 'ssm_b_re': d_s5[2], 'ssm_b_im': d_s5[3],
        'ssm_c_re': d_s5[4], 'ssm_c_im': d_s5[5], 'ssm_log_step': d_s5[6], 'ssm_d': d_s5[7],
        'b_glu': colsum(d_b_glu), 'g_out_attn': colsum(d_g_oa), 'g_out_ssm': colsum(d_g_os), 'g_ffn': colsum(d_g_ffn),
    }
    return jnp.sum(sq), dx, d_late, d_mid, d_small


ANY = pl.BlockSpec(memory_space=pl.ANY)


def _place():
    x, y, c = lax.axis_index("x"), lax.axis_index("y"), lax.axis_index("c")
    other_chips = [(1 - x, y), (x, 1 - y), (1 - x, 1 - y)]
    return x, y, c, 2 * x + y, (x, y, 1 - c), other_chips


class _GatherChips:
    KINDS = 7

    def __init__(self, arrs):
        self.n = len(arrs)
        self.out_shape = [jax.ShapeDtypeStruct((N_CHIPS,) + a.shape, a.dtype) for a in arrs]
        self.scratch = [pltpu.SemaphoreType.DMA((self.n, self.KINDS)), pltpu.SemaphoreType.DMA((self.n, self.KINDS))]

    def _copies(self, ins, outs, send_sems, recv_sems):
        x, y, c, me, sibling, chips = _place()

        def remote(a, k, src, dst, to):
            return lambda: pltpu.make_async_remote_copy(src_ref=src, dst_ref=dst, send_sem=send_sems.at[a, k],
                                                        recv_sem=recv_sems.at[a, k], device_id=to, device_id_type=MESH)
        own, out, landed, passed, theirs = [], [], [], [], []
        for a in range(self.n):
            own.append(remote(a, 6, ins[a], outs[a].at[me], sibling))
            for j, (px, py) in enumerate(chips):
                there, here = outs[a].at[2 * px + py, c], outs[a].at[2 * px + py, 1 - c]
                out.append(remote(a, j, ins[a].at[c], outs[a].at[me, c], (px, py, c)))
                landed.append(remote(a, j, there, there, (px, py, c)))
                passed.append(remote(a, 3 + j, there, there, sibling))
                theirs.append(remote(a, 3 + j, here, here, sibling))
        return own, out, landed, passed, theirs

    def start(self, ins, outs, send_sems, recv_sems):
        own, out, _, _, _ = self._copies(ins, outs, send_sems, recv_sems)
        for make in own + out:
            make().start()

    def finish(self, ins, outs, send_sems, recv_sems):
        own, out, landed, passed, theirs = self._copies(ins, outs, send_sems, recv_sems)
        for arrived, onward in zip(landed, passed):
            arrived().wait_recv()
            onward().start()
        for make in theirs + own:
            make().wait_recv()
        for make in own + out + passed:
            make().wait_send()


class _ScatterChips:
    def __init__(self, sums):
        self.n = len(sums)
        self.out_shape = [jax.ShapeDtypeStruct(s.shape, s.dtype) for s in sums]
        self.scratch = [pltpu.SemaphoreType.DMA((self.n, 3)), pltpu.SemaphoreType.DMA((self.n, 3))]

    def _copies(self, ins, outs, send_sems, recv_sems):
        x, y, c, me, sibling, chips = _place()
        out, landed = [], []

        def remote(a, j, src, dst, to):
            return lambda: pltpu.make_async_remote_copy(src_ref=src, dst_ref=dst, send_sem=send_sems.at[a, j],
                                                        recv_sem=recv_sems.at[a, j], device_id=to, device_id_type=MESH)
        for a in range(self.n):
            for j, (px, py) in enumerate(chips):
                slot = outs[a].at[2 * px + py]
                out.append(remote(a, j, ins[a].at[2 * px + py], outs[a].at[me], (px, py, c)))
                landed.append(remote(a, j, slot, slot, (px, py, c)))
        return out, landed

    def start(self, ins, outs, send_sems, recv_sems):
        for make in self._copies(ins, outs, send_sems, recv_sems)[0]:
            make().start()

    def finish(self, ins, outs, send_sems, recv_sems):
        out, landed = self._copies(ins, outs, send_sems, recv_sems)
        for make in landed:
            make().wait_recv()
        for make in out:
            make().wait_send()


def _comm_call(name, comm, arrs):
    n = comm.n

    def body(*refs):
        parts = (refs[:n], refs[n:2 * n]) + tuple(refs[2 * n:])
        comm.start(*parts)
        comm.finish(*parts)

    return pl.pallas_call(body, name=name, in_specs=[ANY] * n, out_specs=[ANY] * n, out_shape=comm.out_shape,
                          scratch_shapes=comm.scratch)(*arrs)


def _gather_chips(name, arrs):
    return _comm_call(name, _GatherChips(arrs), arrs)


def _swap_halves(name, parts):
    n = len(parts)

    def body(*refs):
        ins, outs = refs[:n], refs[n:2 * n]
        send_sems, recv_sems = refs[2 * n:]
        x, y, c, me, sibling, chips = _place()
        cps = []
        for a in range(n):
            cp = pltpu.make_async_remote_copy(src_ref=ins[a].at[:, 1 - c], dst_ref=outs[a], send_sem=send_sems.at[a],
                                              recv_sem=recv_sems.at[a], device_id=sibling, device_id_type=MESH)
            cp.start()
            cps.append(cp)
        for cp in cps:
            cp.wait()

    return pl.pallas_call(
        body, name=name, in_specs=[ANY] * n, out_specs=[ANY] * n,
        out_shape=[jax.ShapeDtypeStruct((N_CHIPS,) + p.shape[2:], p.dtype) for p in parts],
        scratch_shapes=[pltpu.SemaphoreType.DMA((n,)), pltpu.SemaphoreType.DMA((n,))],
    )(*parts)


def _scatter_chips(name, sums):
    return _comm_call(name, _ScatterChips(sums), sums)


def _swap_reduced(name, halves):
    n = len(halves)

    def body(*refs):
        ins, outs = refs[:n], refs[n:2 * n]
        send_sems, recv_sems = refs[2 * n:]
        x, y, c, me, sibling, chips = _place()
        cps = []
        for a in range(n):
            cp = pltpu.make_async_remote_copy(src_ref=ins[a], dst_ref=outs[a], send_sem=send_sems.at[a],
                                              recv_sem=recv_sems.at[a], device_id=sibling, device_id_type=MESH)
            cp.start()
            cps.append(cp)
        for cp in cps:
            cp.wait()

    return pl.pallas_call(
        body, name=name, in_specs=[ANY] * n, out_specs=[ANY] * n,
        out_shape=[jax.ShapeDtypeStruct(h.shape, h.dtype) for h in halves],
        scratch_shapes=[pltpu.SemaphoreType.DMA((n,)), pltpu.SemaphoreType.DMA((n,))],
    )(*halves)


def _row_tile(r, want=256):
    t = (min(r, want) // SUBLANES) * SUBLANES
    while r % t:
        t -= SUBLANES
    return t


def _add_own_half(name, part, got, c, out_dtype):
    _, _, r, cols = part.shape
    tr = _row_tile(r)

    def body(c_ref, p_ref, g_ref, o_ref):
        o_ref[...] = (p_ref[...].astype(F32) + g_ref[...].astype(F32)).astype(o_ref.dtype)

    return pl.pallas_call(
        body, name=name,
        grid_spec=pltpu.PrefetchScalarGridSpec(
            num_scalar_prefetch=1, grid=(N_CHIPS, r // tr),
            in_specs=[pl.BlockSpec((None, None, tr, cols), lambda s, i, c_ref: (s, c_ref[0], i, 0)),
                      pl.BlockSpec((None, tr, cols), lambda s, i, c_ref: (s, i, 0))],
            out_specs=pl.BlockSpec((None, tr, cols), lambda s, i, c_ref: (s, i, 0))),
        out_shape=jax.ShapeDtypeStruct(got.shape, out_dtype),
        compiler_params=_cparams(("parallel", "parallel")),
    )(c.reshape(1).astype(jnp.int32), part, got)


def _sum_chips(name, got, own, me):
    _, r, cols = got.shape
    tr = _row_tile(r)

    def body(me_ref, r0, r1, r2, r3, own_ref, o_ref):
        pick = lambda s, ref: jnp.where(me_ref[0] == s, own_ref[...], ref[...]).astype(F32)
        o_ref[...] = ((pick(0, r0) + pick(1, r1)) + pick(2, r2)) + pick(3, r3)

    def slot(s):
        return pl.BlockSpec((None, tr, cols),
                            lambda i, me_ref: (jnp.where(me_ref[0] == s, (s + 1) % N_CHIPS, s), i, 0))

    return pl.pallas_call(
        body, name=name,
        grid_spec=pltpu.PrefetchScalarGridSpec(
            num_scalar_prefetch=1, grid=(r // tr,),
            in_specs=[slot(s) for s in range(N_CHIPS)]
            + [pl.BlockSpec((None, tr, cols), lambda i, me_ref: (me_ref[0], i, 0))],
            out_specs=pl.BlockSpec((tr, cols), lambda i, me_ref: (i, 0))),
        out_shape=jax.ShapeDtypeStruct((r, cols), F32),
        compiler_params=_cparams(("parallel",)),
    )(me.reshape(1).astype(jnp.int32), got, got, got, got, own)


def _adamw_math(wv, gv, mv, vv):
    mv = ADAM_B1 * mv + (1.0 - ADAM_B1) * gv
    vv = ADAM_B2 * vv + (1.0 - ADAM_B2) * (gv * gv)
    m_hat = mv / (1.0 - ADAM_B1 ** ADAM_STEP)
    v_hat = vv / (1.0 - ADAM_B2 ** ADAM_STEP)
    return -ADAM_LR * (m_hat / (jnp.sqrt(v_hat) + ADAM_EPS) + ADAM_WD * wv), mv, vv


def _adamw(name, w, g, m, v):
    cols = w.shape[1]
    return _ew(name, _adamw_math, [('r', w), ('r', g), ('r', m), ('r', v)], [('r', cols, F32)] * 3,
               tr=_row_tile(w.shape[0], max(LANES, LANES * LANES // cols)))


def _adamw_halves(name, w, mine, theirs, m, v, c):
    r, cols = mine.shape
    tr = _row_tile(r, 256)
    nb = r // tr

    def body(c_ref, w_ref, a_ref, b_ref, m_ref, v_ref, g_out, d_out, m_out, v_out):
        g = jnp.where(pl.program_id(0) == c_ref[0], a_ref[...], b_ref[...])
        g_out[...] = g
        d_out[...], m_out[...], v_out[...] = _adamw_math(w_ref[...], g, m_ref[...], v_ref[...])

    whole = pl.BlockSpec((tr, cols), lambda h, i, c_ref: (h * nb + i, 0))
    half = pl.BlockSpec((tr, cols), lambda h, i, c_ref: (i, 0))
    return pl.pallas_call(
        body, name=name,
        grid_spec=pltpu.PrefetchScalarGridSpec(
            num_scalar_prefetch=1, grid=(2, nb),
            in_specs=[whole, half, half, whole, whole], out_specs=[whole] * 4),
        out_shape=[jax.ShapeDtypeStruct(w.shape, F32)] * 4,
        compiler_params=_cparams(("parallel", "parallel")),
    )(c.reshape(1).astype(jnp.int32), w, mine, theirs, m, v)


SMALL_ROWS_ALIGN = 2 * N_CHIPS * SUBLANES


MEDIUM_NAMES = ['ssm_b_re', 'ssm_b_im', 'ssm_c_re', 'ssm_c_im']
PACKED_NAMES = [n for n in SMALL_NAMES if n not in MEDIUM_NAMES]


def _pack_small(d):
    flat = jnp.concatenate([d[n].reshape(-1).astype(F32) for n in PACKED_NAMES])
    rows = -(-flat.shape[0] // (LANES * SMALL_ROWS_ALIGN)) * SMALL_ROWS_ALIGN
    return jnp.pad(flat, (0, rows * LANES - flat.shape[0])).reshape(rows, LANES)


def _unpack_small(packed, like):
    flat = packed.reshape(-1)
    out, off = {}, 0
    for n in PACKED_NAMES:
        size = like[n].size
        out[n] = flat[off:off + size].reshape(like[n].shape)
        off += size
    return out


def kernel(x, g_mix, w_in, q_gain, k_gain, rpb, ssm_a_re, ssm_a_im, ssm_b_re, ssm_b_im, ssm_c_re, ssm_c_im, ssm_log_step, ssm_d, w_glu, b_glu, g_out_attn, g_out_ssm, w_out, g_ffn, w_ffn_gate, w_ffn_up, w_ffn_down, loss_target, m_g_mix, m_w_in, m_q_gain, m_k_gain, m_rpb, m_ssm_a_re, m_ssm_a_im, m_ssm_b_re, m_ssm_b_im, m_ssm_c_re, m_ssm_c_im, m_ssm_log_step, m_ssm_d, m_w_glu, m_b_glu, m_g_out_attn, m_g_out_ssm, m_w_out, m_g_ffn, m_w_ffn_gate, m_w_ffn_up, m_w_ffn_down, v_g_mix, v_w_in, v_q_gain, v_k_gain, v_rpb, v_ssm_a_re, v_ssm_a_im, v_ssm_b_re, v_ssm_b_im, v_ssm_c_re, v_ssm_c_im, v_ssm_log_step, v_ssm_d, v_w_glu, v_b_glu, v_g_out_attn, v_g_out_ssm, v_w_out, v_g_ffn, v_w_ffn_gate, v_w_ffn_up, v_w_ffn_down):
    given = dict(locals())
    w = {n: given[n][0] for n in WEIGHT_NAMES}
    mom = {n: given["m_" + n][0] for n in WEIGHT_NAMES}
    var = {n: given["v_" + n][0] for n in WEIGHT_NAMES}
    d = x.shape[-1]
    c = lax.axis_index("c")

    halves = {n: w[n].astype(MXU_DTYPE).reshape((2, w[n].shape[0] // 2, w[n].shape[1])) for n in BIG_NAMES}
    (w_in4,) = _gather_chips("gather_w_in", [halves['w_in']])
    w_in4 = w_in4.reshape((N_CHIPS, -1, w_in4.shape[-1]))

    def chip_sums(tag, grads, payload):
        parts = [g.reshape((N_CHIPS, 2, -1, g.shape[-1])) for g in grads]
        got = _swap_halves("reduce_swap_halves_" + tag, parts)
        return [_add_own_half("reduce_add_%s_%d" % (tag, a), p, gt, c, dt)
                for a, (p, gt, dt) in enumerate(zip(parts, got, payload))]

    reduce_late = lambda tag, grads: chip_sums(tag, grads, [GRAD_PAYLOAD_DTYPE] * len(grads))
    reduce_mid = lambda grads: chip_sums("mid", grads, [GRAD_PAYLOAD_DTYPE] + [F32] * (len(grads) - 1))
    sq, dx, (sums_late, got_late), (sums_mid, got_mid), d_small = _local_step(
        x[0], loss_target[0], w_in4, ('halves', [halves[n] for n in LATE_NAMES]), {n: w[n] for n in SMALL_NAMES},
        reduce_late, reduce_mid)
    loss = lax.psum(0.5 * sq / d, ("x", "y", "c"))

    nbig = len(BIG_NAMES)
    sums_tiny = chip_sums("tiny", [_pack_small(d_small)], [F32])
    got_tiny = list(_scatter_chips("reduce_scatter_tiny", sums_tiny))
    sums = sums_mid[:1] + sums_late + sums_mid[1:] + sums_tiny
    got = got_mid[:1] + got_late + got_mid[1:] + got_tiny
    me = 2 * lax.axis_index("x") + lax.axis_index("y")
    mine = [_sum_chips("reduce_sum_%d" % a, gt, sm_, me) for a, (gt, sm_) in enumerate(zip(got, sums))]
    theirs = _swap_reduced("reduce_swap_reduced", mine)
    in_order = lambda a: jnp.where(c == 0, jnp.stack([mine[a], theirs[a]]), jnp.stack([theirs[a], mine[a]]))
    repl = _gather_chips("gather_small", [in_order(a) for a in range(nbig, len(mine))])
    repl = [r.reshape(-1, LANES) for r in repl]
    like = {n: w[n] for n in SMALL_NAMES}
    grad_small = _unpack_small(repl[-1], like)
    grad_small.update({n: r.reshape(w[n].shape) for n, r in zip(MEDIUM_NAMES, repl)})

    grad_big, delta, new_m, new_v = {}, {}, {}, {}
    for a, n in enumerate(BIG_NAMES):
        grad_big[n], delta[n], new_m[n], new_v[n] = _adamw_halves("adamw_%d" % a, w[n], mine[a], theirs[a],
                                                                  mom[n], var[n], c)
    for n in MEDIUM_NAMES:
        flat = lambda t: t.reshape(-1, w[n].shape[-1])
        res = _adamw("adamw_" + n, flat(w[n]), flat(grad_small[n]), flat(mom[n]), flat(var[n]))
        delta[n], new_m[n], new_v[n] = (t.reshape(w[n].shape) for t in res)
    sd, sm, sv = _adamw("adamw_small", _pack_small(w), repl[-1], _pack_small(mom), _pack_small(var))
    delta.update(_unpack_small(sd, like))
    new_m.update(_unpack_small(sm, like))
    new_v.update(_unpack_small(sv, like))
    grads = {**grad_big, **grad_small}
    lead = lambda t: t[None]
    return (loss, dx[None], *[lead(grads[n]) for n in WEIGHT_NAMES], *[lead(delta[n]) for n in WEIGHT_NAMES],
            *[lead(new_m[n]) for n in WEIGHT_NAMES], *[lead(new_v[n]) for n in WEIGHT_NAMES])
```

```python
import functools
import math

import jax
import jax.numpy as jnp
from jax import lax
from jax.experimental import pallas as pl
from jax.experimental.pallas import tpu as pltpu

F32 = jnp.float32
BF16 = jnp.bfloat16
MXU_DTYPE = BF16
GRAD_PAYLOAD_DTYPE = BF16
DW_DTYPE = BF16
S5_DTYPE = BF16
HI = lax.Precision.HIGHEST
VMEM_LIMIT_V7X = 56 * 1024 * 1024
LANES = 128
SUBLANES = 8

GRID_W = 64
WIN_H = 8
WIN_W = 16
HEAD_DIM = 64
SSM_GROUP_CH = 16
SSM_STATE = 64
S5_CHUNK = 16
S5_GROUPS_PER_STEP = 16
RMS_EPS = 1e-6
NEG_INF = -1e30
N_CHIPS = 4
MESH = pl.DeviceIdType.MESH

ADAM_LR = 0.001
ADAM_B1 = 0.9
ADAM_B2 = 0.999
ADAM_EPS = 1e-08
ADAM_WD = 0.01
ADAM_STEP = 10

WEIGHT_NAMES = ['g_mix', 'w_in', 'q_gain', 'k_gain', 'rpb', 'ssm_a_re', 'ssm_a_im', 'ssm_b_re', 'ssm_b_im',
                'ssm_c_re', 'ssm_c_im', 'ssm_log_step', 'ssm_d', 'w_glu', 'b_glu', 'g_out_attn', 'g_out_ssm',
                'w_out', 'g_ffn', 'w_ffn_gate', 'w_ffn_up', 'w_ffn_down']
BIG_NAMES = ['w_in', 'w_glu', 'w_out', 'w_ffn_gate', 'w_ffn_up', 'w_ffn_down']
LATE_NAMES = BIG_NAMES[1:]
SMALL_NAMES = [n for n in WEIGHT_NAMES if n not in BIG_NAMES]


def _cparams(sem):
    return pltpu.CompilerParams(dimension_semantics=sem, vmem_limit_bytes=VMEM_LIMIT_V7X)


def _tile(n, want):
    if n <= want:
        return n
    t = (want // LANES) * LANES
    while t >= LANES:
        if n % t == 0:
            return t
        t -= LANES
    return n


def _mm(name, a, b, *, contract, a_mode='2', b_mode='2', o_mode='2', out_dtype=F32, add=None, exact=False,
        tm=1024, tn=1024, tk=2048, comm=None, comm_arrs=()):
    dn = {'nn': (((1,), (0,)), ((), ())), 'nt': (((1,), (1,)), ((), ())), 'tn': (((0,), (0,)), ((), ()))}[contract]
    ar, ac = a.shape[-2:]
    br, bc = b.shape[-2:]
    m, kdim = (ar, ac) if contract != 'tn' else (ac, ar)
    n = bc if contract != 'nt' else br
    assert kdim == (br if contract != 'nt' else bc), (name, a.shape, b.shape)
    nbatch = 1
    for arr, mode in ((a, a_mode), (b, b_mode)):
        if mode == 'b':
            nbatch = arr.shape[0]
    nstack = 1
    for arr, mode in ((a, a_mode), (b, b_mode)):
        if mode == 'c':
            nstack = arr.shape[0]
    tm, tn, tk = _tile(m, tm), _tile(n, tn), _tile(kdim, tk)
    nkin = kdim // tk
    nk = nstack * nkin
    grid = (nbatch, m // tm, n // tn, nk)

    def spec(mode, block, rc):
        def imap(s, i, j, kk):
            r, c = rc(i, j, kk % nkin)
            if mode == '2':
                return (r, c)
            return (s if mode == 'b' else kk // nkin, r, c)
        return pl.BlockSpec(block if mode == '2' else (None,) + block, imap)

    a_spec = spec(a_mode, (tm, tk) if contract != 'tn' else (tk, tm),
                  (lambda i, j, k: (i, k)) if contract != 'tn' else (lambda i, j, k: (k, i)))
    b_spec = spec(b_mode, (tk, tn) if contract != 'nt' else (tn, tk),
                  (lambda i, j, k: (k, j)) if contract != 'nt' else (lambda i, j, k: (j, k)))
    o_spec = spec(o_mode, (tm, tn), lambda i, j, k: (i, j))
    out_shape = (m, n) if o_mode == '2' else (nbatch, m, n)
    has_add = add is not None

    def product(a_ref, b_ref):
        if exact:
            return lax.dot_general(a_ref[...].astype(F32), b_ref[...].astype(F32), dn, precision=HI,
                                   preferred_element_type=F32)
        return lax.dot_general(a_ref[...].astype(MXU_DTYPE), b_ref[...].astype(MXU_DTYPE), dn,
                               preferred_element_type=F32)

    ncomm = len(comm_arrs)
    nacc = int(nk > 1)

    def body(*refs):
        a_ref, b_ref = refs[:2]
        add_ref = refs[2] if has_add else None
        c_ins = refs[2 + has_add:2 + has_add + ncomm]
        o_ref = refs[2 + has_add + ncomm]
        c_outs = refs[3 + has_add + ncomm:3 + has_add + 2 * ncomm]
        sems = refs[3 + has_add + 2 * ncomm + nacc:]
        ids = [pl.program_id(ax) for ax in range(4)]
        if comm is not None:
            @pl.when((ids[0] == 0) & (ids[1] == 0) & (ids[2] == 0) & (ids[3] == 0))
            def _():
                comm.start(c_ins, c_outs, *sems)

        def write(r):
            if has_add:
                r = r + add_ref[...].astype(F32)
            o_ref[...] = r.astype(o_ref.dtype)

        if nk == 1:
            write(product(a_ref, b_ref))
        else:
            acc_ref = refs[3 + has_add + 2 * ncomm]

            @pl.when(ids[3] == 0)
            def _():
                acc_ref[...] = jnp.zeros_like(acc_ref)

            acc_ref[...] += product(a_ref, b_ref)

            @pl.when(ids[3] == nk - 1)
            def _():
                write(acc_ref[...])

        if comm is not None:
            @pl.when((ids[0] == grid[0] - 1) & (ids[1] == grid[1] - 1) & (ids[2] == grid[2] - 1) & (ids[3] == nk - 1))
            def _():
                comm.finish(c_ins, c_outs, *sems)

    in_specs = [a_spec, b_spec] + ([o_spec] if has_add else []) + [ANY] * ncomm
    args = (a, b) + ((add,) if has_add else ()) + tuple(comm_arrs)
    res = pl.pallas_call(
        body, name=name, grid=grid, in_specs=in_specs, out_specs=[o_spec] + [ANY] * ncomm,
        out_shape=[jax.ShapeDtypeStruct(out_shape, out_dtype)] + (comm.out_shape if comm is not None else []),
        scratch_shapes=([pltpu.VMEM((tm, tn), F32)] if nk > 1 else []) + (comm.scratch if comm is not None else []),
        compiler_params=_cparams(("parallel", "parallel", "parallel", "arbitrary") if comm is None
                                 else ("arbitrary",) * 4),
    )(*args)
    return res[0] if comm is None else (res[0], res[1:])


def _ew(name, fn, ins, outs, tr=512):
    rows = next(x[1].shape[0] for x in ins if x[0] == 'r')
    tr = min(tr, rows)
    assert rows % tr == 0 and tr % SUBLANES == 0, (name, rows, tr)
    in_specs, args = [], []
    for x in ins:
        if x[0] == 'r' and len(x) == 2:
            in_specs.append(pl.BlockSpec((tr, x[1].shape[1]), lambda i: (i, 0)))
        elif x[0] == 'r':
            in_specs.append(pl.BlockSpec((tr, x[3]), functools.partial(lambda cb, i: (i, cb), x[2])))
        else:
            in_specs.append(pl.BlockSpec(x[1].shape, lambda i: (0, 0)))
        args.append(x[1])
    out_specs, out_shapes = [], []
    for o in outs:
        if o[0] == 'r':
            out_specs.append(pl.BlockSpec((tr, o[1]), lambda i: (i, 0)))
            out_shapes.append(jax.ShapeDtypeStruct((rows, o[1]), o[2]))
        else:
            out_specs.append(pl.BlockSpec((SUBLANES, o[1]), lambda i: (0, 0)))
            out_shapes.append(jax.ShapeDtypeStruct((SUBLANES, o[1]), F32))
    nin = len(ins)
    has_acc = any(o[0] == 'a' for o in outs)

    def body(*refs):
        vals = fn(*[r[...].astype(F32) for r in refs[:nin]])
        if not isinstance(vals, (tuple, list)):
            vals = (vals,)
        i = pl.program_id(0)
        for o, ref, v in zip(outs, refs[nin:], vals):
            if o[0] == 'r':
                ref[...] = v.astype(ref.dtype)
            else:
                part = v.astype(F32).reshape(tr // SUBLANES, SUBLANES, o[1]).sum(axis=0)

                @pl.when(i == 0)
                def _(ref=ref, part=part):
                    ref[...] = part

                @pl.when(i > 0)
                def _(ref=ref, part=part):
                    ref[...] += part

    res = pl.pallas_call(
        body, name=name, grid=(rows // tr,), in_specs=in_specs, out_specs=out_specs, out_shape=out_shapes,
        compiler_params=_cparams(("arbitrary",) if has_acc else ("parallel",)),
    )(*args)
    return res


def _rms(x, g):
    r = lax.rsqrt(jnp.mean(x * x, axis=-1, keepdims=True) + RMS_EPS)
    xr = x * r
    return xr * g, xr


def _rms_bwd(x, g, dy):
    r = lax.rsqrt(jnp.mean(x * x, axis=-1, keepdims=True) + RMS_EPS)
    xr = x * r
    gdy = g * dy
    dx = r * (gdy - xr * jnp.mean(xr * gdy, axis=-1, keepdims=True))
    return dx, dy * xr


def _sigmoid(x):
    return 0.5 * (jnp.tanh(0.5 * x) + 1.0)


_GELU_C = math.sqrt(2.0 / math.pi)


def _gelu(x):
    return 0.5 * x * (1.0 + jnp.tanh(_GELU_C * (x + 0.044715 * x * x * x)))


def _gelu_grad(x):
    t = jnp.tanh(_GELU_C * (x + 0.044715 * x * x * x))
    return 0.5 * (1.0 + t) + 0.5 * x * (1.0 - t * t) * _GELU_C * (1.0 + 3 * 0.044715 * x * x)


ATTN_ROWS_PER_STEP = 16
NT_DIMS = (((1,), (1,)), ((), ()))
NN_DIMS = (((1,), (0,)), ((), ()))
TN_DIMS = (((0,), (0,)), ((), ()))


def _attn_geometry(r, rows):
    row_start = jnp.clip(r - WIN_H // 2, 0, rows - WIN_H)
    key0 = pl.multiple_of(row_start * GRID_W, GRID_W)
    bias0 = pl.multiple_of((row_start - r + (WIN_H - 1)) * GRID_W, GRID_W)
    return key0, bias0


def _window_onehot():
    c = jnp.arange(GRID_W)
    col_start = jnp.clip(c - WIN_W // 2, 0, GRID_W - WIN_W)
    col_in = (c[None, :] >= col_start[:, None]) & (c[None, :] < col_start[:, None] + WIN_W)
    dc = jnp.clip(c[None, :] - c[:, None], -(WIN_W - 1), WIN_W - 1) + (WIN_W - 1)
    onehot = ((dc[:, :, None] == jnp.arange(2 * WIN_W - 1)[None, None, :]) & col_in[:, :, None]).astype(F32)
    return onehot, col_in


def _bias_table(rpb):
    onehot, col_in = _window_onehot()
    nh = rpb.shape[0]
    pairs = rpb.reshape(nh // 2, 2, 2 * WIN_H - 1, 2 * WIN_W - 1)
    mask = jnp.where(col_in, 0.0, NEG_INF).T
    heads = [jnp.einsum('prd,qkd->prkq', pairs[:, e], onehot, precision=HI) + mask for e in range(2)]
    return jnp.concatenate(heads, axis=-1).reshape(nh // 2, (2 * WIN_H - 1) * GRID_W, 2 * GRID_W)


def _bias_table_grad(dtab):
    onehot, _ = _window_onehot()
    npair = dtab.shape[0]
    d = dtab.reshape(npair, 2 * WIN_H - 1, GRID_W, 2 * GRID_W)
    heads = [jnp.einsum('prkq,qkd->prd', d[..., e * GRID_W:(e + 1) * GRID_W], onehot, precision=HI) for e in range(2)]
    return jnp.stack(heads, axis=1).reshape(2 * npair, 2 * WIN_H - 1, 2 * WIN_W - 1)


def _lane_lo(shape):
    return lax.broadcasted_iota(jnp.int32, shape, 1) < HEAD_DIM


def _half_sums(v):
    lo = _lane_lo(v.shape)
    s_lo = jnp.sum(jnp.where(lo, v, 0.0), axis=1, keepdims=True)
    s_hi = jnp.sum(jnp.where(lo, 0.0, v), axis=1, keepdims=True)
    return jnp.where(lo, s_lo, s_hi)


def _rms_pair(x, g):
    r = lax.rsqrt(_half_sums(x * x) * (1.0 / HEAD_DIM) + RMS_EPS)
    return x * r * g


def _rms_pair_bwd(x, g, dy):
    r = lax.rsqrt(_half_sums(x * x) * (1.0 / HEAD_DIM) + RMS_EPS)
    xr = x * r
    gdy = g * dy
    dx = r * (gdy - xr * (_half_sums(xr * gdy) * (1.0 / HEAD_DIM)))
    return dx, dy * xr


def _blockdiag(a):
    a2 = jnp.concatenate([a, a], axis=0)
    row_hi = lax.broadcasted_iota(jnp.int32, a2.shape, 0) >= GRID_W
    lane_hi = lax.broadcasted_iota(jnp.int32, a2.shape, 1) >= HEAD_DIM
    return jnp.where(row_hi == lane_hi, a2, 0.0).astype(MXU_DTYPE)


def _diag_blocks(m):
    return jnp.where(_lane_lo((GRID_W, 2 * HEAD_DIM)), m[:GRID_W], m[GRID_W:])


def _attn_scores(qb, kb, bias):
    st = lax.dot_general(kb, qb, NT_DIMS, preferred_element_type=F32)
    st = st * (1.0 / math.sqrt(HEAD_DIM)) + bias
    mx = jnp.max(st, axis=0, keepdims=True)
    p = jnp.exp(st - mx)
    return p * (1.0 / jnp.sum(p, axis=0, keepdims=True))


def _attn_fwd(z4, qg2, kg2, bias_t, comm=None, comm_arrs=()):
    _, t, aw = z4.shape
    rows = t // GRID_W
    npair = aw // (2 * HEAD_DIM)
    nkeys = WIN_H * GRID_W
    nb = bias_t.shape[1]
    rps = min(ATTN_ROWS_PER_STEP, rows)
    blk = rps * GRID_W
    nsteps = rows // rps
    ncomm = len(comm_arrs)

    def body(*refs):
        q_ref, k_ref, v_ref, qg_ref, kg_ref, b_ref = refs[:6]
        c_ins, o_ref, c_outs = refs[6:6 + ncomm], refs[6 + ncomm], refs[7 + ncomm:7 + 2 * ncomm]
        kn_ref, vb_ref = refs[7 + 2 * ncomm:9 + 2 * ncomm]
        sems = refs[9 + 2 * ncomm:]
        pr, rb = pl.program_id(0), pl.program_id(1)
        if comm is not None:
            @pl.when((pr == 0) & (rb == 0))
            def _():
                comm.start(c_ins, c_outs, *sems)

        @pl.when(rb == 0)
        def _():
            kn_ref[...] = _rms_pair(k_ref[...], kg_ref[...]).astype(MXU_DTYPE)
            vb_ref[...] = v_ref[...].astype(MXU_DTYPE)

        def row(i, carry):
            key0, bias0 = _attn_geometry(rb * rps + i, rows)
            at = pl.ds(pl.multiple_of(i * GRID_W, GRID_W), GRID_W)
            qb = _blockdiag(_rms_pair(q_ref[at, :], qg_ref[...]))
            pt = _attn_scores(qb, kn_ref[pl.ds(key0, nkeys), :], b_ref[pl.ds(bias0, nkeys), :])
            both = lax.dot_general(pt.astype(MXU_DTYPE), vb_ref[pl.ds(key0, nkeys), :], TN_DIMS,
                                   preferred_element_type=F32)
            o_ref[at, :] = _diag_blocks(both)
            return carry

        lax.fori_loop(0, rps, row, 0, unroll=4)
        if comm is not None:
            @pl.when((pr == npair - 1) & (rb == nsteps - 1))
            def _():
                comm.finish(c_ins, c_outs, *sems)

    pair_cols = lambda lead: pl.BlockSpec((None, t, 2 * HEAD_DIM), lambda p, r: (lead, 0, p))
    res = pl.pallas_call(
        body, name="attn_fwd", grid=(npair, nsteps),
        in_specs=[pl.BlockSpec((None, blk, 2 * HEAD_DIM), lambda p, r: (0, r, p)), pair_cols(1), pair_cols(2),
                  pl.BlockSpec((1, 2 * HEAD_DIM), lambda p, r: (0, 0)),
                  pl.BlockSpec((1, 2 * HEAD_DIM), lambda p, r: (0, 0)),
                  pl.BlockSpec((None, nb, 2 * GRID_W), lambda p, r: (p, 0, 0))] + [ANY] * ncomm,
        out_specs=[pl.BlockSpec((blk, 2 * HEAD_DIM), lambda p, r: (r, p))] + [ANY] * ncomm,
        out_shape=[jax.ShapeDtypeStruct((t, aw), F32)] + (comm.out_shape if comm is not None else []),
        scratch_shapes=[pltpu.VMEM((t, 2 * HEAD_DIM), MXU_DTYPE), pltpu.VMEM((t, 2 * HEAD_DIM), MXU_DTYPE)]
        + (comm.scratch if comm is not None else []),
        compiler_params=_cparams(("arbitrary", "arbitrary")),
    )(z4, z4, z4, qg2, kg2, bias_t, *comm_arrs)
    return res[0], res[1:]


def _attn_bwd(z4, qg2, kg2, bias_t, dya, comm=None, comm_arrs=()):
    _, t, aw = z4.shape
    rows = t // GRID_W
    npair = aw // (2 * HEAD_DIM)
    nkeys = WIN_H * GRID_W
    nb = bias_t.shape[1]
    rps = min(ATTN_ROWS_PER_STEP, rows)
    blk = rps * GRID_W
    nsteps = rows // rps
    scale = 1.0 / math.sqrt(HEAD_DIM)
    ncomm = len(comm_arrs)

    def body(*refs):
        q_ref, k_ref, v_ref, qg_ref, kg_ref, b_ref, do_ref = refs[:7]
        c_ins = refs[7:7 + ncomm]
        dz_ref, db_ref, dqg_ref, dkg_ref = refs[7 + ncomm:11 + ncomm]
        c_outs = refs[11 + ncomm:11 + 2 * ncomm]
        kn_ref, vb_ref, dkn_ref, dv_ref = refs[11 + 2 * ncomm:15 + 2 * ncomm]
        sems = refs[15 + 2 * ncomm:]
        pr, rb = pl.program_id(0), pl.program_id(1)
        if comm is not None:
            @pl.when((pr == 0) & (rb == 0))
            def _():
                comm.start(c_ins, c_outs, *sems)

        @pl.when(rb == 0)
        def _():
            kn_ref[...] = _rms_pair(k_ref[...], kg_ref[...]).astype(MXU_DTYPE)
            vb_ref[...] = v_ref[...].astype(MXU_DTYPE)
            dkn_ref[...] = jnp.zeros_like(dkn_ref)
            dv_ref[...] = jnp.zeros_like(dv_ref)
            db_ref[...] = jnp.zeros_like(db_ref)
            dqg_ref[...] = jnp.zeros_like(dqg_ref)

        def row(i, dqg_sum):
            r = rb * rps + i
            key0, bias0 = _attn_geometry(r, rows)
            keys = pl.ds(key0, nkeys)
            at = pl.ds(pl.multiple_of(i * GRID_W, GRID_W), GRID_W)
            q = q_ref[at, :]
            qb = _blockdiag(_rms_pair(q, qg_ref[...]))
            dob = _blockdiag(do_ref[at, :])
            kb = kn_ref[keys, :]
            pt = _attn_scores(qb, kb, b_ref[pl.ds(bias0, nkeys), :])
            dv_ref[keys, :] += lax.dot_general(pt.astype(MXU_DTYPE), dob, NN_DIMS, preferred_element_type=F32)
            dpt = lax.dot_general(vb_ref[keys, :], dob, NT_DIMS, preferred_element_type=F32)
            dst = pt * (dpt - jnp.sum(pt * dpt, axis=0, keepdims=True))
            db_ref[pl.ds(bias0, nkeys), :] += dst
            dsb = dst.astype(MXU_DTYPE)
            dkn_ref[keys, :] += scale * lax.dot_general(dsb, qb, NN_DIMS, preferred_element_type=F32)
            dqn = scale * _diag_blocks(lax.dot_general(dsb, kb, TN_DIMS, preferred_element_type=F32))
            dq, dqg = _rms_pair_bwd(q, qg_ref[...], dqn)
            dz_ref[0, pl.ds(pl.multiple_of(r * GRID_W, GRID_W), GRID_W), :] = dq.astype(dz_ref.dtype)
            return dqg_sum + jnp.sum(dqg, axis=0, keepdims=True)

        def rows4(i, acc):
            for j in range(4):
                acc = row(4 * i + j, acc)
            return acc

        dqg_ref[...] += lax.fori_loop(0, rps // 4, rows4, jnp.zeros((1, 2 * HEAD_DIM), F32))

        @pl.when(rb == nsteps - 1)
        def _():
            dk, dkg = _rms_pair_bwd(k_ref[...], kg_ref[...], dkn_ref[...])
            dz_ref[1] = dk.astype(dz_ref.dtype)
            dz_ref[2] = dv_ref[...].astype(dz_ref.dtype)
            dkg_ref[...] = jnp.sum(dkg, axis=0, keepdims=True)

        if comm is not None:
            @pl.when((pr == npair - 1) & (rb == nsteps - 1))
            def _():
                comm.finish(c_ins, c_outs, *sems)

    pair_cols = lambda lead: pl.BlockSpec((None, t, 2 * HEAD_DIM), lambda p, r: (lead, 0, p))
    pair_vec = pl.BlockSpec((None, 1, 2 * HEAD_DIM), lambda p, r: (p, 0, 0))
    res = pl.pallas_call(
        body, name="attn_bwd", grid=(npair, nsteps),
        in_specs=[pl.BlockSpec((None, blk, 2 * HEAD_DIM), lambda p, r: (0, r, p)), pair_cols(1), pair_cols(2),
                  pl.BlockSpec((1, 2 * HEAD_DIM), lambda p, r: (0, 0)),
                  pl.BlockSpec((1, 2 * HEAD_DIM), lambda p, r: (0, 0)),
                  pl.BlockSpec((None, nb, 2 * GRID_W), lambda p, r: (p, 0, 0)),
                  pl.BlockSpec((blk, 2 * HEAD_DIM), lambda p, r: (r, p))] + [ANY] * ncomm,
        out_specs=[pl.BlockSpec((3, t, 2 * HEAD_DIM), lambda p, r: (0, 0, p)),
                   pl.BlockSpec((None, nb, 2 * GRID_W), lambda p, r: (p, 0, 0)),
                   pair_vec, pair_vec] + [ANY] * ncomm,
        out_shape=[jax.ShapeDtypeStruct((4, t, aw), MXU_DTYPE), jax.ShapeDtypeStruct(bias_t.shape, F32),
                   jax.ShapeDtypeStruct((npair, 1, 2 * HEAD_DIM), F32),
                   jax.ShapeDtypeStruct((npair, 1, 2 * HEAD_DIM), F32)] + (comm.out_shape if comm is not None else []),
        scratch_shapes=[pltpu.VMEM((t, 2 * HEAD_DIM), MXU_DTYPE), pltpu.VMEM((t, 2 * HEAD_DIM), MXU_DTYPE),
                        pltpu.VMEM((t, 2 * HEAD_DIM), F32), pltpu.VMEM((t, 2 * HEAD_DIM), F32)]
        + (comm.scratch if comm is not None else []),
        compiler_params=_cparams(("arbitrary", "arbitrary")),
    )(z4, z4, z4, qg2, kg2, bias_t, dya, *comm_arrs)
    return res[:4], res[4:]


def _bmm_exact(name, a, b, dims, per_step=16):
    nb = a.shape[0]
    per = math.gcd(nb, per_step)
    shape = jax.eval_shape(lambda u, v: lax.dot_general(u, v, dims), a[0], b[0]).shape

    def body(a_ref, b_ref, o_ref):
        for e in range(per):
            o_ref[e] = lax.dot_general(a_ref[e], b_ref[e], dims, precision=HI, preferred_element_type=F32)

    blk = lambda arr: pl.BlockSpec((per,) + arr.shape[1:], lambda i: (i, 0, 0))
    out = jax.ShapeDtypeStruct((nb,) + shape, F32)
    return pl.pallas_call(body, name=name, grid=(nb // per,), in_specs=[blk(a), blk(b)], out_specs=blk(out),
                          out_shape=out, compiler_params=_cparams(("parallel",)))(a, b)


@jax.custom_vjp
def _contract_last(a, b):
    return _bmm_exact("s5_kern", a, b, NT_DIMS)


def _contract_last_fwd(a, b):
    return _contract_last(a, b), (a, b)


def _contract_last_bwd(res, g):
    a, b = res
    return _bmm_exact("s5_kern_da", g, b, NN_DIMS), _bmm_exact("s5_kern_db", g, a, TN_DIMS)


_contract_last.defvjp(_contract_last_fwd, _contract_last_bwd)


def _s5_mats(a_re, a_im, b_re, b_im, c_re, c_im, log_step, d_skip):
    nd, g, p = a_re.shape
    c = b_re.shape[-1]
    L = S5_CHUNK
    lr = jnp.minimum(a_re, -1e-4).transpose(1, 0, 2)
    li = a_im.transpose(1, 0, 2)
    dt = jnp.exp(log_step).T[..., None]
    n = jnp.arange(L + 1, dtype=F32)[None, :, None, None]
    mag = jnp.exp(n * (lr * dt)[:, None])
    ang = n * (li * dt)[:, None]
    pw_r, pw_i = mag * jnp.cos(ang), mag * jnp.sin(ang)
    den = lr * lr + li * li
    nr, ni = pw_r[:, 1] - 1.0, pw_i[:, 1]
    cr, ci = (nr * lr + ni * li) / den, (ni * lr - nr * li) / den
    bt_r, bt_i = b_re.transpose(1, 3, 0, 2), b_im.transpose(1, 3, 0, 2)
    bb_r = cr[:, None] * bt_r - ci[:, None] * bt_i
    bb_i = cr[:, None] * bt_i + ci[:, None] * bt_r
    ct_r, ct_i = c_re.transpose(1, 2, 0, 3), c_im.transpose(1, 2, 0, 3)

    def cols(x_re, x_im):
        return jnp.concatenate([x_re[..., 0, :], x_re[..., 1, :], x_im[..., 0, :], x_im[..., 1, :]], axis=-1)

    e_r = jnp.stack([pw_r[:, :L, 0][:, ::-1], pw_r[:, :L, 1]], axis=2)
    e_i = jnp.stack([pw_i[:, :L, 0][:, ::-1], pw_i[:, :L, 1]], axis=2)
    ws = (cols(e_r, e_r)[:, :, None] * cols(bb_r, bb_i)[:, None]
          + cols(e_i, e_i)[:, :, None] * cols(-bb_i, bb_r)[:, None]).reshape(g, L * c, 4 * p)
    f_r = jnp.stack([pw_r[:, 1:, 0], pw_r[:, 1:, 1][:, ::-1]], axis=2)
    f_i = jnp.stack([pw_i[:, 1:, 0], pw_i[:, 1:, 1][:, ::-1]], axis=2)
    wot = (cols(f_r, f_i)[:, :, None] * cols(ct_r, -ct_r)[:, None]
           + cols(f_i, f_r)[:, :, None] * cols(-ct_i, -ct_i)[:, None]).reshape(g, L * c, 4 * p)
    qr, qi = pw_r[:, None, :L], pw_i[:, None, :L]
    br, bi = bb_r[:, :, None], bb_i[:, :, None]
    kp_r, kp_i = qr * br - qi * bi, qr * bi + qi * br
    lhs = jnp.stack([jnp.concatenate([kp_r[..., d, :], -kp_i[..., d, :]], axis=-1) for d in range(2)])
    rhs = jnp.stack([jnp.concatenate([ct_r[:, :, d], ct_i[:, :, d]], axis=-1) for d in range(2)])
    kern = _contract_last(lhs.reshape(2 * g, c * L, 2 * p), rhs.reshape(2 * g, c, 2 * p)).reshape(2, g, c, L, c)
    skip = d_skip.reshape(g, c, 1, 1) * jnp.eye(c, dtype=F32)[None, :, None, :]
    by_offset = jnp.concatenate([kern[1][:, :, :0:-1], kern[0][:, :, :1] + kern[1][:, :, :1] + skip,
                                 kern[0][:, :, 1:]], axis=2).reshape(g, c, (2 * L - 1) * c)
    mt = jnp.stack([by_offset[:, :, (L - 1 - j) * c:(2 * L - 1 - j) * c] for j in range(L)], axis=1)
    mt = mt.reshape(g, L * c, L * c)
    lr16, li16 = pw_r[:, L], pw_i[:, L]
    fa = jnp.concatenate([lr16[:, 0], lr16[:, 1], lr16[:, 0], lr16[:, 1]], axis=-1)
    fb = jnp.concatenate([-li16[:, 0], -li16[:, 1], li16[:, 0], li16[:, 1]], axis=-1)
    return mt, ws, wot, fa, fb


def _gmm(name, a, b, contract, a_stacked=False, b_stacked=False, o_stacked=False, add=None, out_dtype=F32):
    w = S5_CHUNK * SSM_GROUP_CH
    g = (a.shape[0] if a_stacked else a.shape[1] // w)
    gpb = math.gcd(g, S5_GROUPS_PER_STEP)
    dn = {'nn': (((1,), (0,)), ((), ())), 'nt': (((1,), (1,)), ((), ())), 'tn': (((0,), (0,)), ((), ()))}[contract]

    def spec(arr, stacked):
        if stacked:
            return pl.BlockSpec((gpb,) + arr.shape[1:], lambda i: (i, 0, 0))
        return pl.BlockSpec((arr.shape[0], gpb * w), lambda i: (0, i))

    def take(ref, stacked, e):
        return ref[e] if stacked else ref[:, e * w:(e + 1) * w]

    m = (a.shape[1] if a_stacked else a.shape[0]) if contract != 'tn' else w
    n = w
    if o_stacked:
        o_spec = pl.BlockSpec((gpb, m, n), lambda i: (i, 0, 0))
        o_shape = (g, m, n)
    else:
        o_spec = pl.BlockSpec((m, gpb * n), lambda i: (0, i))
        o_shape = (m, g * n)
    has_add = add is not None

    def body(*refs):
        if has_add:
            a_ref, b_ref, add_ref, o_ref = refs
        else:
            a_ref, b_ref, o_ref = refs
        for e in range(gpb):
            r = lax.dot_general(take(a_ref, a_stacked, e).astype(S5_DTYPE), take(b_ref, b_stacked, e).astype(S5_DTYPE),
                                dn, precision=HI if S5_DTYPE == F32 else None, preferred_element_type=F32)
            if has_add:
                r = r + take(add_ref, o_stacked, e)
            if o_stacked:
                o_ref[e] = r.astype(o_ref.dtype)
            else:
                o_ref[:, e * w:(e + 1) * w] = r.astype(o_ref.dtype)

    in_specs = [spec(a, a_stacked), spec(b, b_stacked)] + ([o_spec] if has_add else [])
    return pl.pallas_call(
        body, name=name, grid=(g // gpb,), in_specs=in_specs, out_specs=o_spec,
        out_shape=jax.ShapeDtypeStruct(o_shape, out_dtype), compiler_params=_cparams(("parallel",)),
    )(*((a, b) + ((add,) if has_add else ())))


def _s5_scan(name, s, fa, fb, rev0, xin=None):
    nk, g, w = s.shape
    hw, qw = w // 2, w // 4
    gb = min(g, 16)
    with_acc = xin is not None

    def body(*refs):
        if with_acc:
            s_ref, a_ref, b_ref, x_ref, o_ref, pa_ref, pb_ref = refs
        else:
            s_ref, a_ref, b_ref, o_ref = refs
        fa_v, fb_v = a_ref[...], b_ref[...]
        dir0 = lax.broadcasted_iota(jnp.int32, (gb, w), 1) % hw < qw
        swap = lambda v: jnp.concatenate([v[:, hw:], v[:, :hw]], axis=1)

        def step(i, carry):
            x, pa, pb = carry
            k0 = (nk - 1 - i) if rev0 else i
            k1 = i if rev0 else (nk - 1 - i)
            for lo in (0, hw):
                o_ref[k0, :, lo:lo + qw] = x[:, lo:lo + qw]
                o_ref[k1, :, lo + qw:lo + hw] = x[:, lo + qw:lo + hw]
            if with_acc:
                xi = jnp.where(dir0, x_ref[k0], x_ref[k1])
                pa = pa + x * xi
                pb = pb + x * swap(xi)
            x = fa_v * x + fb_v * swap(x) + jnp.where(dir0, s_ref[k0], s_ref[k1])
            return x, pa, pb

        z = jnp.zeros((gb, w), F32)
        res = lax.fori_loop(0, nk, step, (z, z, z), unroll=2)
        if with_acc:
            pa_ref[...] = res[1]
            pb_ref[...] = res[2]

    seq = pl.BlockSpec((nk, gb, w), lambda i: (0, i, 0))
    vec = pl.BlockSpec((gb, w), lambda i: (i, 0))
    in_specs = [seq, vec, vec] + ([seq] if with_acc else [])
    out_specs = [seq] + ([vec, vec] if with_acc else [])
    out_shape = [jax.ShapeDtypeStruct((nk, g, w), F32)] + (
        [jax.ShapeDtypeStruct((g, w), F32)] * 2 if with_acc else [])
    return pl.pallas_call(
        body, name=name, grid=(g // gb,), in_specs=in_specs, out_specs=out_specs, out_shape=out_shape,
        compiler_params=_cparams(("parallel",)),
    )(*((s, fa, fb) + ((xin,) if with_acc else ())))


def _regroup(name, x, to_groups):
    if to_groups:
        t, sw = x.shape
    else:
        t, sw = x.shape[0] * S5_CHUNK, x.shape[1] // S5_CHUNK
    nk = t // S5_CHUNK
    wide = LANES * S5_CHUNK

    def place(tok):
        r = lax.broadcasted_iota(jnp.int32, (2 * LANES, wide), 0)
        col = lax.broadcasted_iota(jnp.int32, (2 * LANES, wide), 1)
        ch = r % LANES
        want = (ch // SSM_GROUP_CH) * (S5_CHUNK * SSM_GROUP_CH) + (tok + r // LANES) * SSM_GROUP_CH + ch % SSM_GROUP_CH
        return (col == want).astype(S5_DTYPE)

    def body(x_ref, o_ref):
        token = lambda tok: (pl.ds(tok, nk, stride=S5_CHUNK), slice(None))
        if to_groups:
            acc = jnp.zeros((nk, wide), F32)
            for tok in range(0, S5_CHUNK, 2):
                rows = jnp.concatenate([x_ref[token(tok)], x_ref[token(tok + 1)]], axis=1).astype(S5_DTYPE)
                acc = acc + lax.dot_general(rows, place(tok), NN_DIMS, preferred_element_type=F32)
            o_ref[...] = acc.astype(o_ref.dtype)
        else:
            xv = x_ref[...].astype(S5_DTYPE)
            for tok in range(0, S5_CHUNK, 2):
                both = lax.dot_general(xv, place(tok), NT_DIMS, preferred_element_type=F32).astype(o_ref.dtype)
                o_ref[token(tok)] = both[:, :LANES]
                o_ref[token(tok + 1)] = both[:, LANES:]

    tokens = pl.BlockSpec((t, LANES), lambda i: (0, i))
    groups = pl.BlockSpec((nk, wide), lambda i: (0, i))
    return pl.pallas_call(
        body, name=name, grid=(sw // LANES,), in_specs=[tokens if to_groups else groups],
        out_specs=groups if to_groups else tokens,
        out_shape=jax.ShapeDtypeStruct((nk, sw * S5_CHUNK), S5_DTYPE) if to_groups else jax.ShapeDtypeStruct((t, sw), F32),
        compiler_params=_cparams(("parallel",)),
    )(x)


def _s5_fwd(u2, mats):
    mt, ws, wot, fa, fb = mats
    nk = u2.shape[0]
    g = mt.shape[0]
    y_intra = _gmm("s5_intra", u2, mt, 'nn', b_stacked=True)
    s = _gmm("s5_chunk_state", u2, ws, 'nn', b_stacked=True)
    (xin,) = _s5_scan("s5_scan", s.reshape(nk, g, -1), fa, fb, False)
    xin = xin.reshape(nk, -1)
    return _gmm("s5_inter", xin, wot, 'nt', b_stacked=True, add=y_intra, out_dtype=S5_DTYPE), xin


def _s5_bwd(u2, xin, mats, dy2):
    mt, ws, wot, fa, fb = mats
    nk = u2.shape[0]
    g = mt.shape[0]
    dxin = _gmm("s5_dxin", dy2, wot, 'nn', b_stacked=True)
    ds, pa, pb = _s5_scan("s5_scan_adj", dxin.reshape(nk, g, -1), fa, -fb, True, xin=xin.reshape(nk, g, -1))
    ds = ds.reshape(nk, -1)
    du_a = _gmm("s5_du_intra", dy2, mt, 'nt', b_stacked=True)
    du2 = _gmm("s5_du_state", ds, ws, 'nt', b_stacked=True, add=du_a, out_dtype=S5_DTYPE)
    dmt = _gmm("s5_dmt", u2, dy2, 'tn', o_stacked=True)
    dws = _gmm("s5_dws", u2, ds, 'tn', o_stacked=True)
    dwot = _gmm("s5_dwot", dy2, xin, 'tn', o_stacked=True)
    return du2, (dmt, dws, dwot, pa, pb)


def _stacked(g4):
    return g4.reshape((N_CHIPS, -1, g4.shape[-1]))


def _local_step(x, target, w_in4, late, small, reduce_late=None, reduce_mid=None):
    t, d = x.shape
    aw = w_in4.shape[2]
    sw = aw
    nh = aw // HEAD_DIM
    row = lambda v: v.reshape(1, -1)
    g_mix, g_ffn = row(small['g_mix']), row(small['g_ffn'])
    g_oa, g_os, b_glu = row(small['g_out_attn']), row(small['g_out_ssm']), row(small['b_glu'])
    qg2 = jnp.tile(row(small['q_gain']), (1, 2))
    kg2 = jnp.tile(row(small['k_gain']), (1, 2))

    (h,) = _ew("rms_mix", lambda xv, g: _rms(xv, g)[0], [('r', x), ('c', g_mix)], [('r', d, MXU_DTYPE)])
    bias_t = _bias_table(small['rpb'])
    if late[0] == 'halves':
        under_in, under_attn, under_gate, under_up = late[1][:2], late[1][2:3], late[1][3:4], late[1][4:]
        z4, got_in = _mm("in_proj", h, w_in4, contract='nn', b_mode='b', o_mode='b',
                         comm=_GatherChips(under_in), comm_arrs=under_in)
        ya, got_attn = _attn_fwd(z4, qg2, kg2, bias_t, comm=_GatherChips(under_attn), comm_arrs=under_attn)
        w_glu, w_out = (_stacked(g4).reshape(-1, g4.shape[-1]) for g4 in got_in)
        w_gate4 = _stacked(got_attn[0])
    else:
        z4 = _mm("in_proj", h, w_in4, contract='nn', b_mode='b', o_mode='b')
        ya, _ = _attn_fwd(z4, qg2, kg2, bias_t)
        w_glu, w_out, w_gate4, w_up4, w_down4 = late[1]
    ffs = w_gate4.shape[2]
    s5_params = tuple(small[n] for n in ('ssm_a_re', 'ssm_a_im', 'ssm_b_re', 'ssm_b_im', 'ssm_c_re', 'ssm_c_im',
                                         'ssm_log_step', 'ssm_d'))
    mats, mats_vjp = jax.vjp(_s5_mats, *s5_params)
    mats = tuple(m.astype(S5_DTYPE) for m in mats[:3]) + mats[3:]
    u2 = _regroup("s5_group_u", z4[3], True)
    ypre2, xin = _s5_fwd(u2, mats)
    ypre = _regroup("s5_ungroup_y", ypre2, False)
    (yb,) = _ew("gelu", _gelu, [('r', ypre)], [('r', sw, MXU_DTYPE)])
    a_glu = _mm("glu_proj", yb, w_glu, contract='nn')

    def mix_out(yav, ypv, av, bg, goa, gos):
        ys = _gelu(ypv) * _sigmoid(av + bg)
        return jnp.concatenate([_rms(yav, goa)[0], _rms(ys, gos)[0]], axis=1)
    (ycat,) = _ew("mix_out", mix_out, [('r', ya), ('r', ypre), ('r', a_glu), ('c', b_glu), ('c', g_oa), ('c', g_os)],
                  [('r', aw + sw, MXU_DTYPE)])
    x1 = _mm("out_proj", ycat, w_out, contract='nn', add=x)
    (h2,) = _ew("rms_ffn", lambda xv, g: _rms(xv, g)[0], [('r', x1), ('c', g_ffn)], [('r', d, MXU_DTYPE)])
    if late[0] == 'halves':
        gate4, got_gate = _mm("ffn_gate", h2, w_gate4, contract='nn', b_mode='b', o_mode='b', tn=ffs,
                              out_dtype=MXU_DTYPE, comm=_GatherChips(under_gate), comm_arrs=under_gate)
        w_up4 = _stacked(got_gate[0])
        up4, got_up = _mm("ffn_up", h2, w_up4, contract='nn', b_mode='b', o_mode='b', tn=ffs,
                          out_dtype=MXU_DTYPE, comm=_GatherChips(under_up), comm_arrs=under_up)
        w_down4 = _stacked(got_up[0])
    else:
        gate4 = _mm("ffn_gate", h2, w_gate4, contract='nn', b_mode='b', o_mode='b', tn=ffs, out_dtype=MXU_DTYPE)
        up4 = _mm("ffn_up", h2, w_up4, contract='nn', b_mode='b', o_mode='b', tn=ffs, out_dtype=MXU_DTYPE)
    gate_f, up_f = gate4.reshape(4 * t, ffs), up4.reshape(4 * t, ffs)
    (act,) = _ew("swiglu", lambda gv, uv: gv * _sigmoid(gv) * uv, [('r', gate_f), ('r', up_f)],
                 [('r', ffs, MXU_DTYPE)], tr=1024)
    act4 = act.reshape(4, t, ffs)
    x2 = _mm("ffn_down", act4, w_down4, contract='nn', a_mode='c', b_mode='c', add=x1, tk=ffs)

    def loss_fn(xv, tv):
        diff = xv - tv
        return diff * (1.0 / d), diff * (1.0 / d), diff * diff
    dx2, dx2_b, sq = _ew("loss", loss_fn, [('r', x2), ('r', target)], [('r', d, F32), ('r', d, MXU_DTYPE), ('a', d)])

    dact4 = _mm("ffn_down_dx", dx2_b, w_down4, contract='nt', b_mode='b', o_mode='b', tn=ffs, out_dtype=MXU_DTYPE)
    d_w_down4 = _mm("ffn_down_dw", act4, dx2_b, contract='tn', a_mode='b', o_mode='b', tm=ffs, out_dtype=DW_DTYPE)

    def swiglu_bwd(dav, gv, uv):
        s = _sigmoid(gv)
        return dav * uv * s * (1.0 + gv * (1.0 - s)), dav * gv * s
    dgate, dup = _ew("swiglu_bwd", swiglu_bwd, [('r', dact4.reshape(4 * t, ffs)), ('r', gate_f), ('r', up_f)],
                     [('r', ffs, MXU_DTYPE), ('r', ffs, MXU_DTYPE)], tr=1024)
    dgate4, dup4 = dgate.reshape(4, t, ffs), dup.reshape(4, t, ffs)
    if reduce_late is not None:
        sums_down = reduce_late("down", [d_w_down4])
        dh2, got_down = _mm("ffn_gate_dx", dgate4, w_gate4, contract='nt', a_mode='c', b_mode='c', tk=ffs, tn=2048,
                            comm=_ScatterChips(sums_down), comm_arrs=sums_down)
    else:
        dh2 = _mm("ffn_gate_dx", dgate4, w_gate4, contract='nt', a_mode='c', b_mode='c', tk=ffs, tn=2048)
    dh2 = _mm("ffn_up_dx", dup4, w_up4, contract='nt', a_mode='c', b_mode='c', add=dh2, tk=ffs)
    d_w_gate4 = _mm("ffn_gate_dw", h2, dgate4, contract='tn', b_mode='b', o_mode='b', tn=ffs, out_dtype=DW_DTYPE)
    d_w_up4 = _mm("ffn_up_dw", h2, dup4, contract='tn', b_mode='b', o_mode='b', tn=ffs, out_dtype=DW_DTYPE)

    def rms_res_bwd(xv, g, dyv, resv):
        dx, dg = _rms_bwd(xv, g, dyv)
        return resv + dx, dg
    dx1, d_g_ffn = _ew("rms_ffn_bwd", rms_res_bwd, [('r', x1), ('c', g_ffn), ('r', dh2), ('r', dx2)],
                       [('r', d, F32), ('a', d)])

    dycat = _mm("out_proj_dx", dx1, w_out, contract='nt', out_dtype=MXU_DTYPE)
    d_w_out = _mm("out_proj_dw", ycat, dx1, contract='tn', out_dtype=DW_DTYPE)

    def mix_out_bwd(yav, ypv, av, bg, goa, gos, dca, dcs):
        dya, dgoa = _rms_bwd(yav, goa, dca)
        y = _gelu(ypv)
        s = _sigmoid(av + bg)
        dys, dgos = _rms_bwd(y * s, gos, dcs)
        da = dys * y * s * (1.0 - s)
        return dya, da, dys * s, dgoa, dgos, da
    dya, da, dy_direct, d_g_oa, d_g_os, d_b_glu = _ew(
        "mix_out_bwd", mix_out_bwd,
        [('r', ya), ('r', ypre), ('r', a_glu), ('c', b_glu), ('c', g_oa), ('c', g_os),
         ('r', dycat, 0, aw), ('r', dycat, 1, sw)],
        [('r', aw, F32), ('r', sw, MXU_DTYPE), ('r', sw, F32), ('a', aw), ('a', sw), ('a', sw)])
    dy = _mm("glu_proj_dx", da, w_glu, contract='nt', add=dy_direct)
    d_w_glu = _mm("glu_proj_dw", yb, da, contract='tn', out_dtype=DW_DTYPE)
    (dypre,) = _ew("gelu_bwd", lambda dyv, ypv: dyv * _gelu_grad(ypv), [('r', dy), ('r', ypre)],
                   [('r', sw, F32)])

    du2, dmats = _s5_bwd(u2, xin, mats, _regroup("s5_group_dy", dypre, True))
    d_s5 = mats_vjp(dmats)
    du = _regroup("s5_ungroup_du", du2, False)
    d_late = (d_w_glu, d_w_out, d_w_gate4, d_w_up4, d_w_down4)
    if reduce_late is not None:
        sums = reduce_late("late", d_late[:4])
        (dz4, dbias_t, dqg, dkg), scattered = _attn_bwd(z4, qg2, kg2, bias_t, dya, comm=_ScatterChips(sums),
                                                       comm_arrs=sums)
        d_late = (sums + sums_down, list(scattered) + list(got_down))
    else:
        (dz4, dbias_t, dqg, dkg), _ = _attn_bwd(z4, qg2, kg2, bias_t, dya)
    d_rpb = _bias_table_grad(dbias_t)
    fold = lambda v: v.reshape(-1, 2, HEAD_DIM).sum(axis=(0, 1))
    dz4 = dz4.at[3].set(du.astype(dz4.dtype))

    d_w_in4 = _mm("in_proj_dw", h, dz4, contract='tn', b_mode='b', o_mode='b', out_dtype=DW_DTYPE)
    d_mid = [d_w_in4] + [d_s5[i].reshape(-1, LANES) for i in (2, 3, 4, 5)]
    if reduce_late is not None:
        sums = reduce_mid(d_mid)
        dh, scattered = _mm("in_proj_dx", dz4, w_in4, contract='nt', a_mode='c', b_mode='c',
                            comm=_ScatterChips(sums), comm_arrs=sums)
        d_mid = (sums, list(scattered))
    else:
        dh = _mm("in_proj_dx", dz4, w_in4, contract='nt', a_mode='c', b_mode='c')
    dx, d_g_mix = _ew("rms_mix_bwd", rms_res_bwd, [('r', x), ('c', g_mix), ('r', dh), ('r', dx1)],
                      [('r', d, F32), ('a', d)])

    colsum = lambda v: v.sum(axis=0)
    d_small = {
        'g_mix': colsum(d_g_mix), 'q_gain': fold(dqg), 'k_gain': fold(dkg), 'rpb': d_rpb,
        'ssm_a_re': d_s5[0], 'ssm_a_im': d_s5[1], 'ssm_b_re': d_s5[2], 'ssm_b_im': d_s5[3],
        'ssm_c_re': d_s5[4], 'ssm_c_im': d_s5[5], 'ssm_log_step': d_s5[6], 'ssm_d': d_s5[7],
        'b_glu': colsum(d_b_glu), 'g_out_attn': colsum(d_g_oa), 'g_out_ssm': colsum(d_g_os), 'g_ffn': colsum(d_g_ffn),
    }
    return jnp.sum(sq), dx, d_late, d_mid, d_small


ANY = pl.BlockSpec(memory_space=pl.ANY)


def _place():
    x, y, c = lax.axis_index("x"), lax.axis_index("y"), lax.axis_index("c")
    other_chips = [(1 - x, y), (x, 1 - y), (1 - x, 1 - y)]
    return x, y, c, 2 * x + y, (x, y, 1 - c), other_chips


class _GatherChips:
    KINDS = 7

    def __init__(self, arrs):
        self.n = len(arrs)
        self.out_shape = [jax.ShapeDtypeStruct((N_CHIPS,) + a.shape, a.dtype) for a in arrs]
        self.scratch = [pltpu.SemaphoreType.DMA((self.n, self.KINDS)), pltpu.SemaphoreType.DMA((self.n, self.KINDS))]

    def _copies(self, ins, outs, send_sems, recv_sems):
        x, y, c, me, sibling, chips = _place()

        def remote(a, k, src, dst, to):
            return lambda: pltpu.make_async_remote_copy(src_ref=src, dst_ref=dst, send_sem=send_sems.at[a, k],
                                                        recv_sem=recv_sems.at[a, k], device_id=to, device_id_type=MESH)
        own, out, landed, passed, theirs = [], [], [], [], []
        for a in range(self.n):
            own.append(remote(a, 6, ins[a], outs[a].at[me], sibling))
            for j, (px, py) in enumerate(chips):
                there, here = outs[a].at[2 * px + py, c], outs[a].at[2 * px + py, 1 - c]
                out.append(remote(a, j, ins[a].at[c], outs[a].at[me, c], (px, py, c)))
                landed.append(remote(a, j, there, there, (px, py, c)))
                passed.append(remote(a, 3 + j, there, there, sibling))
                theirs.append(remote(a, 3 + j, here, here, sibling))
        return own, out, landed, passed, theirs

    def start(self, ins, outs, send_sems, recv_sems):
        own, out, _, _, _ = self._copies(ins, outs, send_sems, recv_sems)
        for make in own + out:
            make().start()

    def finish(self, ins, outs, send_sems, recv_sems):
        own, out, landed, passed, theirs = self._copies(ins, outs, send_sems, recv_sems)
        for arrived, onward in zip(landed, passed):
            arrived().wait_recv()
            onward().start()
        for make in theirs + own:
            make().wait_recv()
        for make in own + out + passed:
            make().wait_send()


class _ScatterChips:
    def __init__(self, sums):
        self.n = len(sums)
        self.out_shape = [jax.ShapeDtypeStruct(s.shape, s.dtype) for s in sums]
        self.scratch = [pltpu.SemaphoreType.DMA((self.n, 3)), pltpu.SemaphoreType.DMA((self.n, 3))]

    def _copies(self, ins, outs, send_sems, recv_sems):
        x, y, c, me, sibling, chips = _place()
        out, landed = [], []

        def remote(a, j, src, dst, to):
            return lambda: pltpu.make_async_remote_copy(src_ref=src, dst_ref=dst, send_sem=send_sems.at[a, j],
                                                        recv_sem=recv_sems.at[a, j], device_id=to, device_id_type=MESH)
        for a in range(self.n):
            for j, (px, py) in enumerate(chips):
                slot = outs[a].at[2 * px + py]
                out.append(remote(a, j, ins[a].at[2 * px + py], outs[a].at[me], (px, py, c)))
                landed.append(remote(a, j, slot, slot, (px, py, c)))
        return out, landed

    def start(self, ins, outs, send_sems, recv_sems):
        for make in self._copies(ins, outs, send_sems, recv_sems)[0]:
            make().start()

    def finish(self, ins, outs, send_sems, recv_sems):
        out, landed = self._copies(ins, outs, send_sems, recv_sems)
        for make in landed:
            make().wait_recv()
        for make in out:
            make().wait_send()


def _comm_call(name, comm, arrs):
    n = comm.n

    def body(*refs):
        parts = (refs[:n], refs[n:2 * n]) + tuple(refs[2 * n:])
        comm.start(*parts)
        comm.finish(*parts)

    return pl.pallas_call(body, name=name, in_specs=[ANY] * n, out_specs=[ANY] * n, out_shape=comm.out_shape,
                          scratch_shapes=comm.scratch)(*arrs)


def _gather_chips(name, arrs):
    return _comm_call(name, _GatherChips(arrs), arrs)


def _swap_halves(name, parts):
    n = len(parts)

    def body(*refs):
        ins, outs = refs[:n], refs[n:2 * n]
        send_sems, recv_sems = refs[2 * n:]
        x, y, c, me, sibling, chips = _place()
        cps = []
        for a in range(n):
            cp = pltpu.make_async_remote_copy(src_ref=ins[a].at[:, 1 - c], dst_ref=outs[a], send_sem=send_sems.at[a],
                                              recv_sem=recv_sems.at[a], device_id=sibling, device_id_type=MESH)
            cp.start()
            cps.append(cp)
        for cp in cps:
            cp.wait()

    return pl.pallas_call(
        body, name=name, in_specs=[ANY] * n, out_specs=[ANY] * n,
        out_shape=[jax.ShapeDtypeStruct((N_CHIPS,) + p.shape[2:], p.dtype) for p in parts],
        scratch_shapes=[pltpu.SemaphoreType.DMA((n,)), pltpu.SemaphoreType.DMA((n,))],
    )(*parts)


def _scatter_chips(name, sums):
    return _comm_call(name, _ScatterChips(sums), sums)


def _swap_reduced(name, halves):
    n = len(halves)

    def body(*refs):
        ins, outs = refs[:n], refs[n:2 * n]
        send_sems, recv_sems = refs[2 * n:]
        x, y, c, me, sibling, chips = _place()
        cps = []
        for a in range(n):
            cp = pltpu.make_async_remote_copy(src_ref=ins[a], dst_ref=outs[a], send_sem=send_sems.at[a],
                                              recv_sem=recv_sems.at[a], device_id=sibling, device_id_type=MESH)
            cp.start()
            cps.append(cp)
        for cp in cps:
            cp.wait()

    return pl.pallas_call(
        body, name=name, in_specs=[ANY] * n, out_specs=[ANY] * n,
        out_shape=[jax.ShapeDtypeStruct(h.shape, h.dtype) for h in halves],
        scratch_shapes=[pltpu.SemaphoreType.DMA((n,)), pltpu.SemaphoreType.DMA((n,))],
    )(*halves)


def _row_tile(r, want=256):
    t = (min(r, want) // SUBLANES) * SUBLANES
    while r % t:
        t -= SUBLANES
    return t


def _add_own_half(name, part, got, c, out_dtype):
    _, _, r, cols = part.shape
    tr = _row_tile(r)

    def body(c_ref, p_ref, g_ref, o_ref):
        o_ref[...] = (p_ref[...].astype(F32) + g_ref[...].astype(F32)).astype(o_ref.dtype)

    return pl.pallas_call(
        body, name=name,
        grid_spec=pltpu.PrefetchScalarGridSpec(
            num_scalar_prefetch=1, grid=(N_CHIPS, r // tr),
            in_specs=[pl.BlockSpec((None, None, tr, cols), lambda s, i, c_ref: (s, c_ref[0], i, 0)),
                      pl.BlockSpec((None, tr, cols), lambda s, i, c_ref: (s, i, 0))],
            out_specs=pl.BlockSpec((None, tr, cols), lambda s, i, c_ref: (s, i, 0))),
        out_shape=jax.ShapeDtypeStruct(got.shape, out_dtype),
        compiler_params=_cparams(("parallel", "parallel")),
    )(c.reshape(1).astype(jnp.int32), part, got)


def _sum_chips(name, got, own, me):
    _, r, cols = got.shape
    tr = _row_tile(r)

    def body(me_ref, r0, r1, r2, r3, own_ref, o_ref):
        pick = lambda s, ref: jnp.where(me_ref[0] == s, own_ref[...], ref[...]).astype(F32)
        o_ref[...] = ((pick(0, r0) + pick(1, r1)) + pick(2, r2)) + pick(3, r3)

    def slot(s):
        return pl.BlockSpec((None, tr, cols),
                            lambda i, me_ref: (jnp.where(me_ref[0] == s, (s + 1) % N_CHIPS, s), i, 0))

    return pl.pallas_call(
        body, name=name,
        grid_spec=pltpu.PrefetchScalarGridSpec(
            num_scalar_prefetch=1, grid=(r // tr,),
            in_specs=[slot(s) for s in range(N_CHIPS)]
            + [pl.BlockSpec((None, tr, cols), lambda i, me_ref: (me_ref[0], i, 0))],
            out_specs=pl.BlockSpec((tr, cols), lambda i, me_ref: (i, 0))),
        out_shape=jax.ShapeDtypeStruct((r, cols), F32),
        compiler_params=_cparams(("parallel",)),
    )(me.reshape(1).astype(jnp.int32), got, got, got, got, own)


def _adamw_math(wv, gv, mv, vv):
    mv = ADAM_B1 * mv + (1.0 - ADAM_B1) * gv
    vv = ADAM_B2 * vv + (1.0 - ADAM_B2) * (gv * gv)
    m_hat = mv / (1.0 - ADAM_B1 ** ADAM_STEP)
    v_hat = vv / (1.0 - ADAM_B2 ** ADAM_STEP)
    return -ADAM_LR * (m_hat / (jnp.sqrt(v_hat) + ADAM_EPS) + ADAM_WD * wv), mv, vv


def _adamw(name, w, g, m, v):
    cols = w.shape[1]
    return _ew(name, _adamw_math, [('r', w), ('r', g), ('r', m), ('r', v)], [('r', cols, F32)] * 3,
               tr=_row_tile(w.shape[0], max(LANES, LANES * LANES // cols)))


def _adamw_halves(name, w, mine, theirs, m, v, c):
    r, cols = mine.shape
    tr = _row_tile(r, 256)
    nb = r // tr

    def body(c_ref, w_ref, a_ref, b_ref, m_ref, v_ref, g_out, d_out, m_out, v_out):
        g = jnp.where(pl.program_id(0) == c_ref[0], a_ref[...], b_ref[...])
        g_out[...] = g
        d_out[...], m_out[...], v_out[...] = _adamw_math(w_ref[...], g, m_ref[...], v_ref[...])

    whole = pl.BlockSpec((tr, cols), lambda h, i, c_ref: (h * nb + i, 0))
    half = pl.BlockSpec((tr, cols), lambda h, i, c_ref: (i, 0))
    return pl.pallas_call(
        body, name=name,
        grid_spec=pltpu.PrefetchScalarGridSpec(
            num_scalar_prefetch=1, grid=(2, nb),
            in_specs=[whole, half, half, whole, whole], out_specs=[whole] * 4),
        out_shape=[jax.ShapeDtypeStruct(w.shape, F32)] * 4,
        compiler_params=_cparams(("parallel", "parallel")),
    )(c.reshape(1).astype(jnp.int32), w, mine, theirs, m, v)


SMALL_ROWS_ALIGN = 2 * N_CHIPS * SUBLANES


MEDIUM_NAMES = ['ssm_b_re', 'ssm_b_im', 'ssm_c_re', 'ssm_c_im']
PACKED_NAMES = [n for n in SMALL_NAMES if n not in MEDIUM_NAMES]


def _pack_small(d):
    flat = jnp.concatenate([d[n].reshape(-1).astype(F32) for n in PACKED_NAMES])
    rows = -(-flat.shape[0] // (LANES * SMALL_ROWS_ALIGN)) * SMALL_ROWS_ALIGN
    return jnp.pad(flat, (0, rows * LANES - flat.shape[0])).reshape(rows, LANES)


def _unpack_small(packed, like):
    flat = packed.reshape(-1)
    out, off = {}, 0
    for n in PACKED_NAMES:
        size = like[n].size
        out[n] = flat[off:off + size].reshape(like[n].shape)
        off += size
    return out


def kernel(x, g_mix, w_in, q_gain, k_gain, rpb, ssm_a_re, ssm_a_im, ssm_b_re, ssm_b_im, ssm_c_re, ssm_c_im, ssm_log_step, ssm_d, w_glu, b_glu, g_out_attn, g_out_ssm, w_out, g_ffn, w_ffn_gate, w_ffn_up, w_ffn_down, loss_target, m_g_mix, m_w_in, m_q_gain, m_k_gain, m_rpb, m_ssm_a_re, m_ssm_a_im, m_ssm_b_re, m_ssm_b_im, m_ssm_c_re, m_ssm_c_im, m_ssm_log_step, m_ssm_d, m_w_glu, m_b_glu, m_g_out_attn, m_g_out_ssm, m_w_out, m_g_ffn, m_w_ffn_gate, m_w_ffn_up, m_w_ffn_down, v_g_mix, v_w_in, v_q_gain, v_k_gain, v_rpb, v_ssm_a_re, v_ssm_a_im, v_ssm_b_re, v_ssm_b_im, v_ssm_c_re, v_ssm_c_im, v_ssm_log_step, v_ssm_d, v_w_glu, v_b_glu, v_g_out_attn, v_g_out_ssm, v_w_out, v_g_ffn, v_w_ffn_gate, v_w_ffn_up, v_w_ffn_down):
    given = dict(locals())
    w = {n: given[n][0] for n in WEIGHT_NAMES}
    mom = {n: given["m_" + n][0] for n in WEIGHT_NAMES}
    var = {n: given["v_" + n][0] for n in WEIGHT_NAMES}
    d = x.shape[-1]
    c = lax.axis_index("c")

    halves = {n: w[n].astype(MXU_DTYPE).reshape((2, w[n].shape[0] // 2, w[n].shape[1])) for n in BIG_NAMES}
    (w_in4,) = _gather_chips("gather_w_in", [halves['w_in']])
    w_in4 = w_in4.reshape((N_CHIPS, -1, w_in4.shape[-1]))

    def chip_sums(tag, grads, payload):
        parts = [g.reshape((N_CHIPS, 2, -1, g.shape[-1])) for g in grads]
        got = _swap_halves("reduce_swap_halves_" + tag, parts)
        return [_add_own_half("reduce_add_%s_%d" % (tag, a), p, gt, c, dt)
                for a, (p, gt, dt) in enumerate(zip(parts, got, payload))]

    reduce_late = lambda tag, grads: chip_sums(tag, grads, [GRAD_PAYLOAD_DTYPE] * len(grads))
    reduce_mid = lambda grads: chip_sums("mid", grads, [GRAD_PAYLOAD_DTYPE] + [F32] * (len(grads) - 1))
    sq, dx, (sums_late, got_late), (sums_mid, got_mid), d_small = _local_step(
        x[0], loss_target[0], w_in4, ('halves', [halves[n] for n in LATE_NAMES]), {n: w[n] for n in SMALL_NAMES},
        reduce_late, reduce_mid)
    loss = lax.psum(0.5 * sq / d, ("x", "y", "c"))

    nbig = len(BIG_NAMES)
    sums_tiny = chip_sums("tiny", [_pack_small(d_small)], [F32])
    got_tiny = list(_scatter_chips("reduce_scatter_tiny", sums_tiny))
    sums = sums_mid[:1] + sums_late + sums_mid[1:] + sums_tiny
    got = got_mid[:1] + got_late + got_mid[1:] + got_tiny
    me = 2 * lax.axis_index("x") + lax.axis_index("y")
    mine = [_sum_chips("reduce_sum_%d" % a, gt, sm_, me) for a, (gt, sm_) in enumerate(zip(got, sums))]
    theirs = _swap_reduced("reduce_swap_reduced", mine)
    in_order = lambda a: jnp.where(c == 0, jnp.stack([mine[a], theirs[a]]), jnp.stack([theirs[a], mine[a]]))
    repl = _gather_chips("gather_small", [in_order(a) for a in range(nbig, len(mine))])
    repl = [r.reshape(-1, LANES) for r in repl]
    like = {n: w[n] for n in SMALL_NAMES}
    grad_small = _unpack_small(repl[-1], like)
    grad_small.update({n: r.reshape(w[n].shape) for n, r in zip(MEDIUM_NAMES, repl)})

    grad_big, delta, new_m, new_v = {}, {}, {}, {}
    for a, n in enumerate(BIG_NAMES):
        grad_big[n], delta[n], new_m[n], new_v[n] = _adamw_halves("adamw_%d" % a, w[n], mine[a], theirs[a],
                                                                  mom[n], var[n], c)
    for n in MEDIUM_NAMES:
        flat = lambda t: t.reshape(-1, w[n].shape[-1])
        res = _adamw("adamw_" + n, flat(w[n]), flat(grad_small[n]), flat(mom[n]), flat(var[n]))
        delta[n], new_m[n], new_v[n] = (t.reshape(w[n].shape) for t in res)
    sd, sm, sv = _adamw("adamw_small", _pack_small(w), repl[-1], _pack_small(mom), _pack_small(var))
    delta.update(_unpack_small(sd, like))
    new_m.update(_unpack_small(sm, like))
    new_v.update(_unpack_small(sv, like))
    grads = {**grad_big, **grad_small}
    lead = lambda t: t[None]
    return (loss, dx[None], *[lead(grads[n]) for n in WEIGHT_NAMES], *[lead(delta[n]) for n in WEIGHT_NAMES],
            *[lead(new_m[n]) for n in WEIGHT_NAMES], *[lead(new_v[n]) for n in WEIGHT_NAMES])
```

```python
import functools
import math

import jax
import jax.numpy as jnp
from jax import lax
from jax.experimental import pallas as pl
from jax.experimental.pallas import tpu as pltpu

F32 = jnp.float32
BF16 = jnp.bfloat16
MXU_DTYPE = BF16
GRAD_PAYLOAD_DTYPE = BF16
DW_DTYPE = BF16
S5_DTYPE = BF16
HI = lax.Precision.HIGHEST
VMEM_LIMIT_V7X = 56 * 1024 * 1024
LANES = 128
SUBLANES = 8

GRID_W = 64
WIN_H = 8
WIN_W = 16
HEAD_DIM = 64
SSM_GROUP_CH = 16
SSM_STATE = 64
S5_CHUNK = 16
S5_GROUPS_PER_STEP = 16
RMS_EPS = 1e-6
NEG_INF = -1e30
N_CHIPS = 4
MESH = pl.DeviceIdType.MESH

ADAM_LR = 0.001
ADAM_B1 = 0.9
ADAM_B2 = 0.999
ADAM_EPS = 1e-08
ADAM_WD = 0.01
ADAM_STEP = 10

WEIGHT_NAMES = ['g_mix', 'w_in', 'q_gain', 'k_gain', 'rpb', 'ssm_a_re', 'ssm_a_im', 'ssm_b_re', 'ssm_b_im',
                'ssm_c_re', 'ssm_c_im', 'ssm_log_step', 'ssm_d', 'w_glu', 'b_glu', 'g_out_attn', 'g_out_ssm',
                'w_out', 'g_ffn', 'w_ffn_gate', 'w_ffn_up', 'w_ffn_down']
BIG_NAMES = ['w_in', 'w_glu', 'w_out', 'w_ffn_gate', 'w_ffn_up', 'w_ffn_down']
LATE_NAMES = BIG_NAMES[1:]
SMALL_NAMES = [n for n in WEIGHT_NAMES if n not in BIG_NAMES]


def _cparams(sem):
    return pltpu.CompilerParams(dimension_semantics=sem, vmem_limit_bytes=VMEM_LIMIT_V7X)


def _tile(n, want):
    if n <= want:
        return n
    t = (want // LANES) * LANES
    while t >= LANES:
        if n % t == 0:
            return t
        t -= LANES
    return n


def _mm(name, a, b, *, contract, a_mode='2', b_mode='2', o_mode='2', out_dtype=F32, add=None, exact=False,
        tm=1024, tn=1024, tk=2048, comm=None, comm_arrs=()):
    dn = {'nn': (((1,), (0,)), ((), ())), 'nt': (((1,), (1,)), ((), ())), 'tn': (((0,), (0,)), ((), ()))}[contract]
    ar, ac = a.shape[-2:]
    br, bc = b.shape[-2:]
    m, kdim = (ar, ac) if contract != 'tn' else (ac, ar)
    n = bc if contract != 'nt' else br
    assert kdim == (br if contract != 'nt' else bc), (name, a.shape, b.shape)
    nbatch = 1
    for arr, mode in ((a, a_mode), (b, b_mode)):
        if mode == 'b':
            nbatch = arr.shape[0]
    nstack = 1
    for arr, mode in ((a, a_mode), (b, b_mode)):
        if mode == 'c':
            nstack = arr.shape[0]
    tm, tn, tk = _tile(m, tm), _tile(n, tn), _tile(kdim, tk)
    nkin = kdim // tk
    nk = nstack * nkin
    grid = (nbatch, m // tm, n // tn, nk)

    def spec(mode, block, rc):
        def imap(s, i, j, kk):
            r, c = rc(i, j, kk % nkin)
            if mode == '2':
                return (r, c)
            return (s if mode == 'b' else kk // nkin, r, c)
        return pl.BlockSpec(block if mode == '2' else (None,) + block, imap)

    a_spec = spec(a_mode, (tm, tk) if contract != 'tn' else (tk, tm),
                  (lambda i, j, k: (i, k)) if contract != 'tn' else (lambda i, j, k: (k, i)))
    b_spec = spec(b_mode, (tk, tn) if contract != 'nt' else (tn, tk),
                  (lambda i, j, k: (k, j)) if contract != 'nt' else (lambda i, j, k: (j, k)))
    o_spec = spec(o_mode, (tm, tn), lambda i, j, k: (i, j))
    out_shape = (m, n) if o_mode == '2' else (nbatch, m, n)
    has_add = add is not None

    def product(a_ref, b_ref):
        if exact:
            return lax.dot_general(a_ref[...].astype(F32), b_ref[...].astype(F32), dn, precision=HI,
                                   preferred_element_type=F32)
        return lax.dot_general(a_ref[...].astype(MXU_DTYPE), b_ref[...].astype(MXU_DTYPE), dn,
                               preferred_element_type=F32)

    ncomm = len(comm_arrs)
    nacc = int(nk > 1)

    def body(*refs):
        a_ref, b_ref = refs[:2]
        add_ref = refs[2] if has_add else None
        c_ins = refs[2 + has_add:2 + has_add + ncomm]
        o_ref = refs[2 + has_add + ncomm]
        c_outs = refs[3 + has_add + ncomm:3 + has_add + 2 * ncomm]
        sems = refs[3 + has_add + 2 * ncomm + nacc:]
        ids = [pl.program_id(ax) for ax in range(4)]
        if comm is not None:
            @pl.when((ids[0] == 0) & (ids[1] == 0) & (ids[2] == 0) & (ids[3] == 0))
            def _():
                comm.start(c_ins, c_outs, *sems)

        def write(r):
            if has_add:
                r = r + add_ref[...].astype(F32)
            o_ref[...] = r.astype(o_ref.dtype)

        if nk == 1:
            write(product(a_ref, b_ref))
        else:
            acc_ref = refs[3 + has_add + 2 * ncomm]

            @pl.when(ids[3] == 0)
            def _():
                acc_ref[...] = jnp.zeros_like(acc_ref)

            acc_ref[...] += product(a_ref, b_ref)

            @pl.when(ids[3] == nk - 1)
            def _():
                write(acc_ref[...])

        if comm is not None:
            @pl.when((ids[0] == grid[0] - 1) & (ids[1] == grid[1] - 1) & (ids[2] == grid[2] - 1) & (ids[3] == nk - 1))
            def _():
                comm.finish(c_ins, c_outs, *sems)

    in_specs = [a_spec, b_spec] + ([o_spec] if has_add else []) + [ANY] * ncomm
    args = (a, b) + ((add,) if has_add else ()) + tuple(comm_arrs)
    res = pl.pallas_call(
        body, name=name, grid=grid, in_specs=in_specs, out_specs=[o_spec] + [ANY] * ncomm,
        out_shape=[jax.ShapeDtypeStruct(out_shape, out_dtype)] + (comm.out_shape if comm is not None else []),
        scratch_shapes=([pltpu.VMEM((tm, tn), F32)] if nk > 1 else []) + (comm.scratch if comm is not None else []),
        compiler_params=_cparams(("parallel", "parallel", "parallel", "arbitrary") if comm is None
                                 else ("arbitrary",) * 4),
    )(*args)
    return res[0] if comm is None else (res[0], res[1:])


def _ew(name, fn, ins, outs, tr=512):
    rows = next(x[1].shape[0] for x in ins if x[0] == 'r')
    tr = min(tr, rows)
    assert rows % tr == 0 and tr % SUBLANES == 0, (name, rows, tr)
    in_specs, args = [], []
    for x in ins:
        if x[0] == 'r' and len(x) == 2:
            in_specs.append(pl.BlockSpec((tr, x[1].shape[1]), lambda i: (i, 0)))
        elif x[0] == 'r':
            in_specs.append(pl.BlockSpec((tr, x[3]), functools.partial(lambda cb, i: (i, cb), x[2])))
        else:
            in_specs.append(pl.BlockSpec(x[1].shape, lambda i: (0, 0)))
        args.append(x[1])
    out_specs, out_shapes = [], []
    for o in outs:
        if o[0] == 'r':
            out_specs.append(pl.BlockSpec((tr, o[1]), lambda i: (i, 0)))
            out_shapes.append(jax.ShapeDtypeStruct((rows, o[1]), o[2]))
        else:
            out_specs.append(pl.BlockSpec((SUBLANES, o[1]), lambda i: (0, 0)))
            out_shapes.append(jax.ShapeDtypeStruct((SUBLANES, o[1]), F32))
    nin = len(ins)
    has_acc = any(o[0] == 'a' for o in outs)

    def body(*refs):
        vals = fn(*[r[...].astype(F32) for r in refs[:nin]])
        if not isinstance(vals, (tuple, list)):
            vals = (vals,)
        i = pl.program_id(0)
        for o, ref, v in zip(outs, refs[nin:], vals):
            if o[0] == 'r':
                ref[...] = v.astype(ref.dtype)
            else:
                part = v.astype(F32).reshape(tr // SUBLANES, SUBLANES, o[1]).sum(axis=0)

                @pl.when(i == 0)
                def _(ref=ref, part=part):
                    ref[...] = part

                @pl.when(i > 0)
                def _(ref=ref, part=part):
                    ref[...] += part

    res = pl.pallas_call(
        body, name=name, grid=(rows // tr,), in_specs=in_specs, out_specs=out_specs, out_shape=out_shapes,
        compiler_params=_cparams(("arbitrary",) if has_acc else ("parallel",)),
    )(*args)
    return res


def _rms(x, g):
    r = lax.rsqrt(jnp.mean(x * x, axis=-1, keepdims=True) + RMS_EPS)
    xr = x * r
    return xr * g, xr


def _rms_bwd(x, g, dy):
    r = lax.rsqrt(jnp.mean(x * x, axis=-1, keepdims=True) + RMS_EPS)
    xr = x * r
    gdy = g * dy
    dx = r * (gdy - xr * jnp.mean(xr * gdy, axis=-1, keepdims=True))
    return dx, dy * xr


def _sigmoid(x):
    return 0.5 * (jnp.tanh(0.5 * x) + 1.0)


_GELU_C = math.sqrt(2.0 / math.pi)


def _gelu(x):
    return 0.5 * x * (1.0 + jnp.tanh(_GELU_C * (x + 0.044715 * x * x * x)))


def _gelu_grad(x):
    t = jnp.tanh(_GELU_C * (x + 0.044715 * x * x * x))
    return 0.5 * (1.0 + t) + 0.5 * x * (1.0 - t * t) * _GELU_C * (1.0 + 3 * 0.044715 * x * x)


ATTN_ROWS_PER_STEP = 16
ATTN_BWD_ROWS_TOGETHER = 8
NT_DIMS = (((1,), (1,)), ((), ()))
NN_DIMS = (((1,), (0,)), ((), ()))
TN_DIMS = (((0,), (0,)), ((), ()))


def _attn_geometry(r, rows):
    row_start = jnp.clip(r - WIN_H // 2, 0, rows - WIN_H)
    key0 = pl.multiple_of(row_start * GRID_W, GRID_W)
    bias0 = pl.multiple_of((row_start - r + (WIN_H - 1)) * GRID_W, GRID_W)
    return key0, bias0


def _window_onehot():
    c = jnp.arange(GRID_W)
    col_start = jnp.clip(c - WIN_W // 2, 0, GRID_W - WIN_W)
    col_in = (c[None, :] >= col_start[:, None]) & (c[None, :] < col_start[:, None] + WIN_W)
    dc = jnp.clip(c[None, :] - c[:, None], -(WIN_W - 1), WIN_W - 1) + (WIN_W - 1)
    onehot = ((dc[:, :, None] == jnp.arange(2 * WIN_W - 1)[None, None, :]) & col_in[:, :, None]).astype(F32)
    return onehot, col_in


def _bias_table(rpb):
    onehot, col_in = _window_onehot()
    nh = rpb.shape[0]
    pairs = rpb.reshape(nh // 2, 2, 2 * WIN_H - 1, 2 * WIN_W - 1)
    mask = jnp.where(col_in, 0.0, NEG_INF).T
    heads = [jnp.einsum('prd,qkd->prkq', pairs[:, e], onehot, precision=HI) + mask for e in range(2)]
    return jnp.concatenate(heads, axis=-1).reshape(nh // 2, (2 * WIN_H - 1) * GRID_W, 2 * GRID_W)


def _bias_table_grad(dtab):
    onehot, _ = _window_onehot()
    npair = dtab.shape[0]
    d = dtab.reshape(npair, 2 * WIN_H - 1, GRID_W, 2 * GRID_W)
    heads = [jnp.einsum('prkq,qkd->prd', d[..., e * GRID_W:(e + 1) * GRID_W], onehot, precision=HI) for e in range(2)]
    return jnp.stack(heads, axis=1).reshape(2 * npair, 2 * WIN_H - 1, 2 * WIN_W - 1)


def _lane_lo(shape):
    return lax.broadcasted_iota(jnp.int32, shape, 1) < HEAD_DIM


def _half_sums(v):
    lo = _lane_lo(v.shape)
    s_lo = jnp.sum(jnp.where(lo, v, 0.0), axis=1, keepdims=True)
    s_hi = jnp.sum(jnp.where(lo, 0.0, v), axis=1, keepdims=True)
    return jnp.where(lo, s_lo, s_hi)


def _rms_pair(x, g):
    r = lax.rsqrt(_half_sums(x * x) * (1.0 / HEAD_DIM) + RMS_EPS)
    return x * r * g


def _rms_pair_bwd(x, g, dy):
    r = lax.rsqrt(_half_sums(x * x) * (1.0 / HEAD_DIM) + RMS_EPS)
    xr = x * r
    gdy = g * dy
    dx = r * (gdy - xr * (_half_sums(xr * gdy) * (1.0 / HEAD_DIM)))
    return dx, dy * xr


def _blockdiag(a):
    a2 = jnp.concatenate([a, a], axis=0)
    row_hi = lax.broadcasted_iota(jnp.int32, a2.shape, 0) >= GRID_W
    lane_hi = lax.broadcasted_iota(jnp.int32, a2.shape, 1) >= HEAD_DIM
    return jnp.where(row_hi == lane_hi, a2, 0.0).astype(MXU_DTYPE)


def _diag_blocks(m):
    return jnp.where(_lane_lo((GRID_W, 2 * HEAD_DIM)), m[:GRID_W], m[GRID_W:])


def _attn_scores(qb, kb, bias):
    st = lax.dot_general(kb, qb, NT_DIMS, preferred_element_type=F32)
    st = st * (1.0 / math.sqrt(HEAD_DIM)) + bias
    mx = jnp.max(st, axis=0, keepdims=True)
    p = jnp.exp(st - mx)
    return p * (1.0 / jnp.sum(p, axis=0, keepdims=True))


def _attn_fwd(z4, qg2, kg2, bias_t, comm=None, comm_arrs=()):
    _, t, aw = z4.shape
    rows = t // GRID_W
    npair = aw // (2 * HEAD_DIM)
    nkeys = WIN_H * GRID_W
    nb = bias_t.shape[1]
    rps = min(ATTN_ROWS_PER_STEP, rows)
    blk = rps * GRID_W
    nsteps = rows // rps
    ncomm = len(comm_arrs)

    def body(*refs):
        q_ref, k_ref, v_ref, qg_ref, kg_ref, b_ref = refs[:6]
        c_ins, o_ref, c_outs = refs[6:6 + ncomm], refs[6 + ncomm], refs[7 + ncomm:7 + 2 * ncomm]
        kn_ref, vb_ref = refs[7 + 2 * ncomm:9 + 2 * ncomm]
        sems = refs[9 + 2 * ncomm:]
        pr, rb = pl.program_id(0), pl.program_id(1)
        if comm is not None:
            @pl.when((pr == 0) & (rb == 0))
            def _():
                comm.start(c_ins, c_outs, *sems)

        @pl.when(rb == 0)
        def _():
            kn_ref[...] = _rms_pair(k_ref[...], kg_ref[...]).astype(MXU_DTYPE)
            vb_ref[...] = v_ref[...].astype(MXU_DTYPE)

        def row(i, carry):
            key0, bias0 = _attn_geometry(rb * rps + i, rows)
            at = pl.ds(pl.multiple_of(i * GRID_W, GRID_W), GRID_W)
            qb = _blockdiag(_rms_pair(q_ref[at, :], qg_ref[...]))
            pt = _attn_scores(qb, kn_ref[pl.ds(key0, nkeys), :], b_ref[pl.ds(bias0, nkeys), :])
            both = lax.dot_general(pt.astype(MXU_DTYPE), vb_ref[pl.ds(key0, nkeys), :], TN_DIMS,
                                   preferred_element_type=F32)
            o_ref[at, :] = _diag_blocks(both)
            return carry

        lax.fori_loop(0, rps, row, 0, unroll=8)
        if comm is not None:
            @pl.when((pr == npair - 1) & (rb == nsteps - 1))
            def _():
                comm.finish(c_ins, c_outs, *sems)

    pair_cols = lambda lead: pl.BlockSpec((None, t, 2 * HEAD_DIM), lambda p, r: (lead, 0, p))
    res = pl.pallas_call(
        body, name="attn_fwd", grid=(npair, nsteps),
        in_specs=[pl.BlockSpec((None, blk, 2 * HEAD_DIM), lambda p, r: (0, r, p)), pair_cols(1), pair_cols(2),
                  pl.BlockSpec((1, 2 * HEAD_DIM), lambda p, r: (0, 0)),
                  pl.BlockSpec((1, 2 * HEAD_DIM), lambda p, r: (0, 0)),
                  pl.BlockSpec((None, nb, 2 * GRID_W), lambda p, r: (p, 0, 0))] + [ANY] * ncomm,
        out_specs=[pl.BlockSpec((blk, 2 * HEAD_DIM), lambda p, r: (r, p))] + [ANY] * ncomm,
        out_shape=[jax.ShapeDtypeStruct((t, aw), F32)] + (comm.out_shape if comm is not None else []),
        scratch_shapes=[pltpu.VMEM((t, 2 * HEAD_DIM), MXU_DTYPE), pltpu.VMEM((t, 2 * HEAD_DIM), MXU_DTYPE)]
        + (comm.scratch if comm is not None else []),
        compiler_params=_cparams(("arbitrary", "arbitrary")),
    )(z4, z4, z4, qg2, kg2, bias_t, *comm_arrs)
    return res[0], res[1:]


def _attn_bwd(z4, qg2, kg2, bias_t, dya, comm=None, comm_arrs=()):
    _, t, aw = z4.shape
    rows = t // GRID_W
    npair = aw // (2 * HEAD_DIM)
    nkeys = WIN_H * GRID_W
    nb = bias_t.shape[1]
    rps = min(ATTN_ROWS_PER_STEP, rows)
    blk = rps * GRID_W
    nsteps = rows // rps
    scale = 1.0 / math.sqrt(HEAD_DIM)
    ncomm = len(comm_arrs)

    def body(*refs):
        q_ref, k_ref, v_ref, qg_ref, kg_ref, b_ref, do_ref = refs[:7]
        c_ins = refs[7:7 + ncomm]
        dz_ref, db_ref, dqg_ref, dkg_ref = refs[7 + ncomm:11 + ncomm]
        c_outs = refs[11 + ncomm:11 + 2 * ncomm]
        kn_ref, vb_ref, dkn_ref, dv_ref = refs[11 + 2 * ncomm:15 + 2 * ncomm]
        sems = refs[15 + 2 * ncomm:]
        pr, rb = pl.program_id(0), pl.program_id(1)
        if comm is not None:
            @pl.when((pr == 0) & (rb == 0))
            def _():
                comm.start(c_ins, c_outs, *sems)

        @pl.when(rb == 0)
        def _():
            kn_ref[...] = _rms_pair(k_ref[...], kg_ref[...]).astype(MXU_DTYPE)
            vb_ref[...] = v_ref[...].astype(MXU_DTYPE)
            dkn_ref[...] = jnp.zeros_like(dkn_ref)
            dv_ref[...] = jnp.zeros_like(dv_ref)
            db_ref[...] = jnp.zeros_like(db_ref)
            dqg_ref[...] = jnp.zeros_like(dqg_ref)

        def row(i, dqg_sum):
            r = rb * rps + i
            key0, bias0 = _attn_geometry(r, rows)
            keys = pl.ds(key0, nkeys)
            at = pl.ds(pl.multiple_of(i * GRID_W, GRID_W), GRID_W)
            q = q_ref[at, :]
            qb = _blockdiag(_rms_pair(q, qg_ref[...]))
            dob = _blockdiag(do_ref[at, :])
            kb = kn_ref[keys, :]
            pt = _attn_scores(qb, kb, b_ref[pl.ds(bias0, nkeys), :])
            dv_ref[keys, :] += lax.dot_general(pt.astype(MXU_DTYPE), dob, NN_DIMS, preferred_element_type=F32)
            dpt = lax.dot_general(vb_ref[keys, :], dob, NT_DIMS, preferred_element_type=F32)
            dst = pt * (dpt - jnp.sum(pt * dpt, axis=0, keepdims=True))
            db_ref[pl.ds(bias0, nkeys), :] += dst
            dsb = dst.astype(MXU_DTYPE)
            dkn_ref[keys, :] += scale * lax.dot_general(dsb, qb, NN_DIMS, preferred_element_type=F32)
            dqn = scale * _diag_blocks(lax.dot_general(dsb, kb, TN_DIMS, preferred_element_type=F32))
            dq, dqg = _rms_pair_bwd(q, qg_ref[...], dqn)
            dz_ref[0, pl.ds(pl.multiple_of(r * GRID_W, GRID_W), GRID_W), :] = dq.astype(dz_ref.dtype)
            return dqg_sum + jnp.sum(dqg, axis=0, keepdims=True)

        def rows_together(i, acc):
            for j in range(ATTN_BWD_ROWS_TOGETHER):
                acc = row(ATTN_BWD_ROWS_TOGETHER * i + j, acc)
            return acc

        dqg_ref[...] += lax.fori_loop(0, rps // ATTN_BWD_ROWS_TOGETHER, rows_together,
                                      jnp.zeros((1, 2 * HEAD_DIM), F32))

        @pl.when(rb == nsteps - 1)
        def _():
            dk, dkg = _rms_pair_bwd(k_ref[...], kg_ref[...], dkn_ref[...])
            dz_ref[1] = dk.astype(dz_ref.dtype)
            dz_ref[2] = dv_ref[...].astype(dz_ref.dtype)
            dkg_ref[...] = jnp.sum(dkg, axis=0, keepdims=True)

        if comm is not None:
            @pl.when((pr == npair - 1) & (rb == nsteps - 1))
            def _():
                comm.finish(c_ins, c_outs, *sems)

    pair_cols = lambda lead: pl.BlockSpec((None, t, 2 * HEAD_DIM), lambda p, r: (lead, 0, p))
    pair_vec = pl.BlockSpec((None, 1, 2 * HEAD_DIM), lambda p, r: (p, 0, 0))
    res = pl.pallas_call(
        body, name="attn_bwd", grid=(npair, nsteps),
        in_specs=[pl.BlockSpec((None, blk, 2 * HEAD_DIM), lambda p, r: (0, r, p)), pair_cols(1), pair_cols(2),
                  pl.BlockSpec((1, 2 * HEAD_DIM), lambda p, r: (0, 0)),
                  pl.BlockSpec((1, 2 * HEAD_DIM), lambda p, r: (0, 0)),
                  pl.BlockSpec((None, nb, 2 * GRID_W), lambda p, r: (p, 0, 0)),
                  pl.BlockSpec((blk, 2 * HEAD_DIM), lambda p, r: (r, p))] + [ANY] * ncomm,
        out_specs=[pl.BlockSpec((3, t, 2 * HEAD_DIM), lambda p, r: (0, 0, p)),
                   pl.BlockSpec((None, nb, 2 * GRID_W), lambda p, r: (p, 0, 0)),
                   pair_vec, pair_vec] + [ANY] * ncomm,
        out_shape=[jax.ShapeDtypeStruct((4, t, aw), MXU_DTYPE), jax.ShapeDtypeStruct(bias_t.shape, F32),
                   jax.ShapeDtypeStruct((npair, 1, 2 * HEAD_DIM), F32),
                   jax.ShapeDtypeStruct((npair, 1, 2 * HEAD_DIM), F32)] + (comm.out_shape if comm is not None else []),
        scratch_shapes=[pltpu.VMEM((t, 2 * HEAD_DIM), MXU_DTYPE), pltpu.VMEM((t, 2 * HEAD_DIM), MXU_DTYPE),
                        pltpu.VMEM((t, 2 * HEAD_DIM), F32), pltpu.VMEM((t, 2 * HEAD_DIM), F32)]
        + (comm.scratch if comm is not None else []),
        compiler_params=_cparams(("arbitrary", "arbitrary")),
    )(z4, z4, z4, qg2, kg2, bias_t, dya, *comm_arrs)
    return res[:4], res[4:]


def _bmm_exact(name, a, b, dims, per_step=16):
    nb = a.shape[0]
    per = math.gcd(nb, per_step)
    shape = jax.eval_shape(lambda u, v: lax.dot_general(u, v, dims), a[0], b[0]).shape

    def body(a_ref, b_ref, o_ref):
        for e in range(per):
            o_ref[e] = lax.dot_general(a_ref[e], b_ref[e], dims, precision=HI, preferred_element_type=F32)

    blk = lambda arr: pl.BlockSpec((per,) + arr.shape[1:], lambda i: (i, 0, 0))
    out = jax.ShapeDtypeStruct((nb,) + shape, F32)
    return pl.pallas_call(body, name=name, grid=(nb // per,), in_specs=[blk(a), blk(b)], out_specs=blk(out),
                          out_shape=out, compiler_params=_cparams(("parallel",)))(a, b)


@jax.custom_vjp
def _contract_last(a, b):
    return _bmm_exact("s5_kern", a, b, NT_DIMS)


def _contract_last_fwd(a, b):
    return _contract_last(a, b), (a, b)


def _contract_last_bwd(res, g):
    a, b = res
    return _bmm_exact("s5_kern_da", g, b, NN_DIMS), _bmm_exact("s5_kern_db", g, a, TN_DIMS)


_contract_last.defvjp(_contract_last_fwd, _contract_last_bwd)


def _s5_mats(a_re, a_im, b_re, b_im, c_re, c_im, log_step, d_skip):
    nd, g, p = a_re.shape
    c = b_re.shape[-1]
    L = S5_CHUNK
    lr = jnp.minimum(a_re, -1e-4).transpose(1, 0, 2)
    li = a_im.transpose(1, 0, 2)
    dt = jnp.exp(log_step).T[..., None]
    n = jnp.arange(L + 1, dtype=F32)[None, :, None, None]
    mag = jnp.exp(n * (lr * dt)[:, None])
    ang = n * (li * dt)[:, None]
    pw_r, pw_i = mag * jnp.cos(ang), mag * jnp.sin(ang)
    den = lr * lr + li * li
    nr, ni = pw_r[:, 1] - 1.0, pw_i[:, 1]
    cr, ci = (nr * lr + ni * li) / den, (ni * lr - nr * li) / den
    bt_r, bt_i = b_re.transpose(1, 3, 0, 2), b_im.transpose(1, 3, 0, 2)
    bb_r = cr[:, None] * bt_r - ci[:, None] * bt_i
    bb_i = cr[:, None] * bt_i + ci[:, None] * bt_r
    ct_r, ct_i = c_re.transpose(1, 2, 0, 3), c_im.transpose(1, 2, 0, 3)

    def cols(x_re, x_im):
        return jnp.concatenate([x_re[..., 0, :], x_re[..., 1, :], x_im[..., 0, :], x_im[..., 1, :]], axis=-1)

    e_r = jnp.stack([pw_r[:, :L, 0][:, ::-1], pw_r[:, :L, 1]], axis=2)
    e_i = jnp.stack([pw_i[:, :L, 0][:, ::-1], pw_i[:, :L, 1]], axis=2)
    ws = (cols(e_r, e_r)[:, :, None] * cols(bb_r, bb_i)[:, None]
          + cols(e_i, e_i)[:, :, None] * cols(-bb_i, bb_r)[:, None]).reshape(g, L * c, 4 * p)
    f_r = jnp.stack([pw_r[:, 1:, 0], pw_r[:, 1:, 1][:, ::-1]], axis=2)
    f_i = jnp.stack([pw_i[:, 1:, 0], pw_i[:, 1:, 1][:, ::-1]], axis=2)
    wot = (cols(f_r, f_i)[:, :, None] * cols(ct_r, -ct_r)[:, None]
           + cols(f_i, f_r)[:, :, None] * cols(-ct_i, -ct_i)[:, None]).reshape(g, L * c, 4 * p)
    qr, qi = pw_r[:, None, :L], pw_i[:, None, :L]
    br, bi = bb_r[:, :, None], bb_i[:, :, None]
    kp_r, kp_i = qr * br - qi * bi, qr * bi + qi * br
    lhs = jnp.stack([jnp.concatenate([kp_r[..., d, :], -kp_i[..., d, :]], axis=-1) for d in range(2)])
    rhs = jnp.stack([jnp.concatenate([ct_r[:, :, d], ct_i[:, :, d]], axis=-1) for d in range(2)])
    kern = _contract_last(lhs.reshape(2 * g, c * L, 2 * p), rhs.reshape(2 * g, c, 2 * p)).reshape(2, g, c, L, c)
    skip = d_skip.reshape(g, c, 1, 1) * jnp.eye(c, dtype=F32)[None, :, None, :]
    by_offset = jnp.concatenate([kern[1][:, :, :0:-1], kern[0][:, :, :1] + kern[1][:, :, :1] + skip,
                                 kern[0][:, :, 1:]], axis=2).reshape(g, c, (2 * L - 1) * c)
    mt = jnp.stack([by_offset[:, :, (L - 1 - j) * c:(2 * L - 1 - j) * c] for j in range(L)], axis=1)
    mt = mt.reshape(g, L * c, L * c)
    lr16, li16 = pw_r[:, L], pw_i[:, L]
    fa = jnp.concatenate([lr16[:, 0], lr16[:, 1], lr16[:, 0], lr16[:, 1]], axis=-1)
    fb = jnp.concatenate([-li16[:, 0], -li16[:, 1], li16[:, 0], li16[:, 1]], axis=-1)
    return mt, ws, wot, fa, fb


def _gmm(name, a, b, contract, a_stacked=False, b_stacked=False, o_stacked=False, add=None, out_dtype=F32):
    w = S5_CHUNK * SSM_GROUP_CH
    g = (a.shape[0] if a_stacked else a.shape[1] // w)
    gpb = math.gcd(g, S5_GROUPS_PER_STEP)
    dn = {'nn': (((1,), (0,)), ((), ())), 'nt': (((1,), (1,)), ((), ())), 'tn': (((0,), (0,)), ((), ()))}[contract]

    def spec(arr, stacked):
        if stacked:
            return pl.BlockSpec((gpb,) + arr.shape[1:], lambda i: (i, 0, 0))
        return pl.BlockSpec((arr.shape[0], gpb * w), lambda i: (0, i))

    def take(ref, stacked, e):
        return ref[e] if stacked else ref[:, e * w:(e + 1) * w]

    m = (a.shape[1] if a_stacked else a.shape[0]) if contract != 'tn' else w
    n = w
    if o_stacked:
        o_spec = pl.BlockSpec((gpb, m, n), lambda i: (i, 0, 0))
        o_shape = (g, m, n)
    else:
        o_spec = pl.BlockSpec((m, gpb * n), lambda i: (0, i))
        o_shape = (m, g * n)
    has_add = add is not None

    def body(*refs):
        if has_add:
            a_ref, b_ref, add_ref, o_ref = refs
        else:
            a_ref, b_ref, o_ref = refs
        for e in range(gpb):
            r = lax.dot_general(take(a_ref, a_stacked, e).astype(S5_DTYPE), take(b_ref, b_stacked, e).astype(S5_DTYPE),
                                dn, precision=HI if S5_DTYPE == F32 else None, preferred_element_type=F32)
            if has_add:
                r = r + take(add_ref, o_stacked, e)
            if o_stacked:
                o_ref[e] = r.astype(o_ref.dtype)
            else:
                o_ref[:, e * w:(e + 1) * w] = r.astype(o_ref.dtype)

    in_specs = [spec(a, a_stacked), spec(b, b_stacked)] + ([o_spec] if has_add else [])
    return pl.pallas_call(
        body, name=name, grid=(g // gpb,), in_specs=in_specs, out_specs=o_spec,
        out_shape=jax.ShapeDtypeStruct(o_shape, out_dtype), compiler_params=_cparams(("parallel",)),
    )(*((a, b) + ((add,) if has_add else ())))


def _s5_scan(name, s, fa, fb, rev0, xin=None):
    nk, g, w = s.shape
    hw, qw = w // 2, w // 4
    gb = min(g, 16)
    with_acc = xin is not None

    def body(*refs):
        if with_acc:
            s_ref, a_ref, b_ref, x_ref, o_ref, pa_ref, pb_ref = refs
        else:
            s_ref, a_ref, b_ref, o_ref = refs
        fa_v, fb_v = a_ref[...], b_ref[...]
        dir0 = lax.broadcasted_iota(jnp.int32, (gb, w), 1) % hw < qw
        swap = lambda v: jnp.concatenate([v[:, hw:], v[:, :hw]], axis=1)

        def step(i, carry):
            x, pa, pb = carry
            k0 = (nk - 1 - i) if rev0 else i
            k1 = i if rev0 else (nk - 1 - i)
            for lo in (0, hw):
                o_ref[k0, :, lo:lo + qw] = x[:, lo:lo + qw]
                o_ref[k1, :, lo + qw:lo + hw] = x[:, lo + qw:lo + hw]
            if with_acc:
                xi = jnp.where(dir0, x_ref[k0], x_ref[k1])
                pa = pa + x * xi
                pb = pb + x * swap(xi)
            x = fa_v * x + fb_v * swap(x) + jnp.where(dir0, s_ref[k0], s_ref[k1])
            return x, pa, pb

        z = jnp.zeros((gb, w), F32)
        res = lax.fori_loop(0, nk, step, (z, z, z), unroll=2)
        if with_acc:
            pa_ref[...] = res[1]
            pb_ref[...] = res[2]

    seq = pl.BlockSpec((nk, gb, w), lambda i: (0, i, 0))
    vec = pl.BlockSpec((gb, w), lambda i: (i, 0))
    in_specs = [seq, vec, vec] + ([seq] if with_acc else [])
    out_specs = [seq] + ([vec, vec] if with_acc else [])
    out_shape = [jax.ShapeDtypeStruct((nk, g, w), F32)] + (
        [jax.ShapeDtypeStruct((g, w), F32)] * 2 if with_acc else [])
    return pl.pallas_call(
        body, name=name, grid=(g // gb,), in_specs=in_specs, out_specs=out_specs, out_shape=out_shape,
        compiler_params=_cparams(("parallel",)),
    )(*((s, fa, fb) + ((xin,) if with_acc else ())))


def _regroup(name, x, to_groups):
    if to_groups:
        t, sw = x.shape
    else:
        t, sw = x.shape[0] * S5_CHUNK, x.shape[1] // S5_CHUNK
    nk = t // S5_CHUNK
    wide = LANES * S5_CHUNK

    def place(tok):
        r = lax.broadcasted_iota(jnp.int32, (2 * LANES, wide), 0)
        col = lax.broadcasted_iota(jnp.int32, (2 * LANES, wide), 1)
        ch = r % LANES
        want = (ch // SSM_GROUP_CH) * (S5_CHUNK * SSM_GROUP_CH) + (tok + r // LANES) * SSM_GROUP_CH + ch % SSM_GROUP_CH
        return (col == want).astype(S5_DTYPE)

    def body(x_ref, o_ref):
        token = lambda tok: (pl.ds(tok, nk, stride=S5_CHUNK), slice(None))
        if to_groups:
            acc = jnp.zeros((nk, wide), F32)
            for tok in range(0, S5_CHUNK, 2):
                rows = jnp.concatenate([x_ref[token(tok)], x_ref[token(tok + 1)]], axis=1).astype(S5_DTYPE)
                acc = acc + lax.dot_general(rows, place(tok), NN_DIMS, preferred_element_type=F32)
            o_ref[...] = acc.astype(o_ref.dtype)
        else:
            xv = x_ref[...].astype(S5_DTYPE)
            for tok in range(0, S5_CHUNK, 2):
                both = lax.dot_general(xv, place(tok), NT_DIMS, preferred_element_type=F32).astype(o_ref.dtype)
                o_ref[token(tok)] = both[:, :LANES]
                o_ref[token(tok + 1)] = both[:, LANES:]

    tokens = pl.BlockSpec((t, LANES), lambda i: (0, i))
    groups = pl.BlockSpec((nk, wide), lambda i: (0, i))
    return pl.pallas_call(
        body, name=name, grid=(sw // LANES,), in_specs=[tokens if to_groups else groups],
        out_specs=groups if to_groups else tokens,
        out_shape=jax.ShapeDtypeStruct((nk, sw * S5_CHUNK), S5_DTYPE) if to_groups else jax.ShapeDtypeStruct((t, sw), F32),
        compiler_params=_cparams(("parallel",)),
    )(x)


def _s5_fwd(u2, mats):
    mt, ws, wot, fa, fb = mats
    nk = u2.shape[0]
    g = mt.shape[0]
    y_intra = _gmm("s5_intra", u2, mt, 'nn', b_stacked=True)
    s = _gmm("s5_chunk_state", u2, ws, 'nn', b_stacked=True)
    (xin,) = _s5_scan("s5_scan", s.reshape(nk, g, -1), fa, fb, False)
    xin = xin.reshape(nk, -1)
    return _gmm("s5_inter", xin, wot, 'nt', b_stacked=True, add=y_intra, out_dtype=S5_DTYPE), xin


def _s5_bwd(u2, xin, mats, dy2):
    mt, ws, wot, fa, fb = mats
    nk = u2.shape[0]
    g = mt.shape[0]
    dxin = _gmm("s5_dxin", dy2, wot, 'nn', b_stacked=True)
    ds, pa, pb = _s5_scan("s5_scan_adj", dxin.reshape(nk, g, -1), fa, -fb, True, xin=xin.reshape(nk, g, -1))
    ds = ds.reshape(nk, -1)
    du_a = _gmm("s5_du_intra", dy2, mt, 'nt', b_stacked=True)
    du2 = _gmm("s5_du_state", ds, ws, 'nt', b_stacked=True, add=du_a, out_dtype=S5_DTYPE)
    dmt = _gmm("s5_dmt", u2, dy2, 'tn', o_stacked=True)
    dws = _gmm("s5_dws", u2, ds, 'tn', o_stacked=True)
    dwot = _gmm("s5_dwot", dy2, xin, 'tn', o_stacked=True)
    return du2, (dmt, dws, dwot, pa, pb)


def _stacked(g4):
    return g4.reshape((N_CHIPS, -1, g4.shape[-1]))


def _local_step(x, target, w_in4, late, small, reduce_late=None, reduce_mid=None):
    t, d = x.shape
    aw = w_in4.shape[2]
    sw = aw
    nh = aw // HEAD_DIM
    row = lambda v: v.reshape(1, -1)
    g_mix, g_ffn = row(small['g_mix']), row(small['g_ffn'])
    g_oa, g_os, b_glu = row(small['g_out_attn']), row(small['g_out_ssm']), row(small['b_glu'])
    qg2 = jnp.tile(row(small['q_gain']), (1, 2))
    kg2 = jnp.tile(row(small['k_gain']), (1, 2))

    (h,) = _ew("rms_mix", lambda xv, g: _rms(xv, g)[0], [('r', x), ('c', g_mix)], [('r', d, MXU_DTYPE)])
    bias_t = _bias_table(small['rpb'])
    if late[0] == 'halves':
        under_in, under_attn, under_gate, under_up = late[1][:2], late[1][2:3], late[1][3:4], late[1][4:]
        z4, got_in = _mm("in_proj", h, w_in4, contract='nn', b_mode='b', o_mode='b',
                         comm=_GatherChips(under_in), comm_arrs=under_in)
        ya, got_attn = _attn_fwd(z4, qg2, kg2, bias_t, comm=_GatherChips(under_attn), comm_arrs=under_attn)
        w_glu, w_out = (_stacked(g4).reshape(-1, g4.shape[-1]) for g4 in got_in)
        w_gate4 = _stacked(got_attn[0])
    else:
        z4 = _mm("in_proj", h, w_in4, contract='nn', b_mode='b', o_mode='b')
        ya, _ = _attn_fwd(z4, qg2, kg2, bias_t)
        w_glu, w_out, w_gate4, w_up4, w_down4 = late[1]
    ffs = w_gate4.shape[2]
    s5_params = tuple(small[n] for n in ('ssm_a_re', 'ssm_a_im', 'ssm_b_re', 'ssm_b_im', 'ssm_c_re', 'ssm_c_im',
                                         'ssm_log_step', 'ssm_d'))
    mats, mats_vjp = jax.vjp(_s5_mats, *s5_params)
    mats = tuple(m.astype(S5_DTYPE) for m in mats[:3]) + mats[3:]
    u2 = _regroup("s5_group_u", z4[3], True)
    ypre2, xin = _s5_fwd(u2, mats)
    ypre = _regroup("s5_ungroup_y", ypre2, False)
    (yb,) = _ew("gelu", _gelu, [('r', ypre)], [('r', sw, MXU_DTYPE)])
    a_glu = _mm("glu_proj", yb, w_glu, contract='nn')

    def mix_out(yav, ypv, av, bg, goa, gos):
        ys = _gelu(ypv) * _sigmoid(av + bg)
        return jnp.concatenate([_rms(yav, goa)[0], _rms(ys, gos)[0]], axis=1)
    (ycat,) = _ew("mix_out", mix_out, [('r', ya), ('r', ypre), ('r', a_glu), ('c', b_glu), ('c', g_oa), ('c', g_os)],
                  [('r', aw + sw, MXU_DTYPE)])
    x1 = _mm("out_proj", ycat, w_out, contract='nn', add=x)
    (h2,) = _ew("rms_ffn", lambda xv, g: _rms(xv, g)[0], [('r', x1), ('c', g_ffn)], [('r', d, MXU_DTYPE)])
    if late[0] == 'halves':
        gate4, got_gate = _mm("ffn_gate", h2, w_gate4, contract='nn', b_mode='b', o_mode='b', tn=ffs,
                              out_dtype=MXU_DTYPE, comm=_GatherChips(under_gate), comm_arrs=under_gate)
        w_up4 = _stacked(got_gate[0])
        up4, got_up = _mm("ffn_up", h2, w_up4, contract='nn', b_mode='b', o_mode='b', tn=ffs,
                          out_dtype=MXU_DTYPE, comm=_GatherChips(under_up), comm_arrs=under_up)
        w_down4 = _stacked(got_up[0])
    else:
        gate4 = _mm("ffn_gate", h2, w_gate4, contract='nn', b_mode='b', o_mode='b', tn=ffs, out_dtype=MXU_DTYPE)
        up4 = _mm("ffn_up", h2, w_up4, contract='nn', b_mode='b', o_mode='b', tn=ffs, out_dtype=MXU_DTYPE)
    gate_f, up_f = gate4.reshape(4 * t, ffs), up4.reshape(4 * t, ffs)
    (act,) = _ew("swiglu", lambda gv, uv: gv * _sigmoid(gv) * uv, [('r', gate_f), ('r', up_f)],
                 [('r', ffs, MXU_DTYPE)], tr=1024)
    act4 = act.reshape(4, t, ffs)
    x2 = _mm("ffn_down", act4, w_down4, contract='nn', a_mode='c', b_mode='c', add=x1, tk=ffs)

    def loss_fn(xv, tv):
        diff = xv - tv
        return diff * (1.0 / d), diff * (1.0 / d), diff * diff
    dx2, dx2_b, sq = _ew("loss", loss_fn, [('r', x2), ('r', target)], [('r', d, F32), ('r', d, MXU_DTYPE), ('a', d)])

    dact4 = _mm("ffn_down_dx", dx2_b, w_down4, contract='nt', b_mode='b', o_mode='b', tn=ffs, out_dtype=MXU_DTYPE)
    d_w_down4 = _mm("ffn_down_dw", act4, dx2_b, contract='tn', a_mode='b', o_mode='b', tm=ffs, out_dtype=DW_DTYPE)

    def swiglu_bwd(dav, gv, uv):
        s = _sigmoid(gv)
        return dav * uv * s * (1.0 + gv * (1.0 - s)), dav * gv * s
    dgate, dup = _ew("swiglu_bwd", swiglu_bwd, [('r', dact4.reshape(4 * t, ffs)), ('r', gate_f), ('r', up_f)],
                     [('r', ffs, MXU_DTYPE), ('r', ffs, MXU_DTYPE)], tr=1024)
    dgate4, dup4 = dgate.reshape(4, t, ffs), dup.reshape(4, t, ffs)
    if reduce_late is not None:
        sums_down = reduce_late("down", [d_w_down4])
        dh2, got_down = _mm("ffn_gate_dx", dgate4, w_gate4, contract='nt', a_mode='c', b_mode='c', tk=ffs, tn=2048,
                            comm=_ScatterChips(sums_down), comm_arrs=sums_down)
    else:
        dh2 = _mm("ffn_gate_dx", dgate4, w_gate4, contract='nt', a_mode='c', b_mode='c', tk=ffs, tn=2048)
    dh2 = _mm("ffn_up_dx", dup4, w_up4, contract='nt', a_mode='c', b_mode='c', add=dh2, tk=ffs)
    d_w_gate4 = _mm("ffn_gate_dw", h2, dgate4, contract='tn', b_mode='b', o_mode='b', tn=ffs, out_dtype=DW_DTYPE)
    d_w_up4 = _mm("ffn_up_dw", h2, dup4, contract='tn', b_mode='b', o_mode='b', tn=ffs, out_dtype=DW_DTYPE)

    def rms_res_bwd(xv, g, dyv, resv):
        dx, dg = _rms_bwd(xv, g, dyv)
        return resv + dx, dg
    dx1, d_g_ffn = _ew("rms_ffn_bwd", rms_res_bwd, [('r', x1), ('c', g_ffn), ('r', dh2), ('r', dx2)],
                       [('r', d, F32), ('a', d)])

    dycat = _mm("out_proj_dx", dx1, w_out, contract='nt', out_dtype=MXU_DTYPE)
    d_w_out = _mm("out_proj_dw", ycat, dx1, contract='tn', out_dtype=DW_DTYPE)

    def mix_out_bwd(yav, ypv, av, bg, goa, gos, dca, dcs):
        dya, dgoa = _rms_bwd(yav, goa, dca)
        y = _gelu(ypv)
        s = _sigmoid(av + bg)
        dys, dgos = _rms_bwd(y * s, gos, dcs)
        da = dys * y * s * (1.0 - s)
        return dya, da, dys * s, dgoa, dgos, da
    dya, da, dy_direct, d_g_oa, d_g_os, d_b_glu = _ew(
        "mix_out_bwd", mix_out_bwd,
        [('r', ya), ('r', ypre), ('r', a_glu), ('c', b_glu), ('c', g_oa), ('c', g_os),
         ('r', dycat, 0, aw), ('r', dycat, 1, sw)],
        [('r', aw, F32), ('r', sw, MXU_DTYPE), ('r', sw, F32), ('a', aw), ('a', sw), ('a', sw)])
    dy = _mm("glu_proj_dx", da, w_glu, contract='nt', add=dy_direct)
    d_w_glu = _mm("glu_proj_dw", yb, da, contract='tn', out_dtype=DW_DTYPE)
    (dypre,) = _ew("gelu_bwd", lambda dyv, ypv: dyv * _gelu_grad(ypv), [('r', dy), ('r', ypre)],
                   [('r', sw, F32)])

    du2, dmats = _s5_bwd(u2, xin, mats, _regroup("s5_group_dy", dypre, True))
    d_s5 = mats_vjp(dmats)
    du = _regroup("s5_ungroup_du", du2, False)
    d_late = (d_w_glu, d_w_out, d_w_gate4, d_w_up4, d_w_down4)
    if reduce_late is not None:
        sums = reduce_late("late", d_late[:4])
        (dz4, dbias_t, dqg, dkg), scattered = _attn_bwd(z4, qg2, kg2, bias_t, dya, comm=_ScatterChips(sums),
                                                       comm_arrs=sums)
        d_late = (sums + sums_down, list(scattered) + list(got_down))
    else:
        (dz4, dbias_t, dqg, dkg), _ = _attn_bwd(z4, qg2, kg2, bias_t, dya)
    d_rpb = _bias_table_grad(dbias_t)
    fold = lambda v: v.reshape(-1, 2, HEAD_DIM).sum(axis=(0, 1))
    dz4 = dz4.at[3].set(du.astype(dz4.dtype))

    d_w_in4 = _mm("in_proj_dw", h, dz4, contract='tn', b_mode='b', o_mode='b', out_dtype=DW_DTYPE)
    d_mid = [d_w_in4] + [d_s5[i].reshape(-1, LANES) for i in (2, 3, 4, 5)]
    if reduce_late is not None:
        sums = reduce_mid(d_mid)
        dh, scattered = _mm("in_proj_dx", dz4, w_in4, contract='nt', a_mode='c', b_mode='c',
                            comm=_ScatterChips(sums), comm_arrs=sums)
        d_mid = (sums, list(scattered))
    else:
        dh = _mm("in_proj_dx", dz4, w_in4, contract='nt', a_mode='c', b_mode='c')
    dx, d_g_mix = _ew("rms_mix_bwd", rms_res_bwd, [('r', x), ('c', g_mix), ('r', dh), ('r', dx1)],
                      [('r', d, F32), ('a', d)])

    colsum = lambda v: v.sum(axis=0)
    d_small = {
        'g_mix': colsum(d_g_mix), 'q_gain': fold(dqg), 'k_gain': fold(dkg), 'rpb': d_rpb,
        'ssm_a_re': d_s5[0], 'ssm_a_im': d_s5[1], 'ssm_b_re': d_s5[2], 'ssm_b_im': d_s5[3],
        'ssm_c_re': d_s5[4], 'ssm_c_im': d_s5[5], 'ssm_log_step': d_s5[6], 'ssm_d': d_s5[7],
        'b_glu': colsum(d_b_glu), 'g_out_attn': colsum(d_g_oa), 'g_out_ssm': colsum(d_g_os), 'g_ffn': colsum(d_g_ffn),
    }
    return jnp.sum(sq), dx, d_late, d_mid, d_small


ANY = pl.BlockSpec(memory_space=pl.ANY)


def _place():
    x, y, c = lax.axis_index("x"), lax.axis_index("y"), lax.axis_index("c")
    other_chips = [(1 - x, y), (x, 1 - y), (1 - x, 1 - y)]
    return x, y, c, 2 * x + y, (x, y, 1 - c), other_chips


class _GatherChips:
    KINDS = 7

    def __init__(self, arrs):
        self.n = len(arrs)
        self.out_shape = [jax.ShapeDtypeStruct((N_CHIPS,) + a.shape, a.dtype) for a in arrs]
        self.scratch = [pltpu.SemaphoreType.DMA((self.n, self.KINDS)), pltpu.SemaphoreType.DMA((self.n, self.KINDS))]

    def _copies(self, ins, outs, send_sems, recv_sems):
        x, y, c, me, sibling, chips = _place()

        def remote(a, k, src, dst, to):
            return lambda: pltpu.make_async_remote_copy(src_ref=src, dst_ref=dst, send_sem=send_sems.at[a, k],
                                                        recv_sem=recv_sems.at[a, k], device_id=to, device_id_type=MESH)
        own, out, landed, passed, theirs = [], [], [], [], []
        for a in range(self.n):
            own.append(remote(a, 6, ins[a], outs[a].at[me], sibling))
            for j, (px, py) in enumerate(chips):
                there, here = outs[a].at[2 * px + py, c], outs[a].at[2 * px + py, 1 - c]
                out.append(remote(a, j, ins[a].at[c], outs[a].at[me, c], (px, py, c)))
                landed.append(remote(a, j, there, there, (px, py, c)))
                passed.append(remote(a, 3 + j, there, there, sibling))
                theirs.append(remote(a, 3 + j, here, here, sibling))
        return own, out, landed, passed, theirs

    def start(self, ins, outs, send_sems, recv_sems):
        own, out, _, _, _ = self._copies(ins, outs, send_sems, recv_sems)
        for make in own + out:
            make().start()

    def finish(self, ins, outs, send_sems, recv_sems):
        own, out, landed, passed, theirs = self._copies(ins, outs, send_sems, recv_sems)
        for arrived, onward in zip(landed, passed):
            arrived().wait_recv()
            onward().start()
        for make in theirs + own:
            make().wait_recv()
        for make in own + out + passed:
            make().wait_send()


class _ScatterChips:
    def __init__(self, sums):
        self.n = len(sums)
        self.out_shape = [jax.ShapeDtypeStruct(s.shape, s.dtype) for s in sums]
        self.scratch = [pltpu.SemaphoreType.DMA((self.n, 3)), pltpu.SemaphoreType.DMA((self.n, 3))]

    def _copies(self, ins, outs, send_sems, recv_sems):
        x, y, c, me, sibling, chips = _place()
        out, landed = [], []

        def remote(a, j, src, dst, to):
            return lambda: pltpu.make_async_remote_copy(src_ref=src, dst_ref=dst, send_sem=send_sems.at[a, j],
                                                        recv_sem=recv_sems.at[a, j], device_id=to, device_id_type=MESH)
        for a in range(self.n):
            for j, (px, py) in enumerate(chips):
                slot = outs[a].at[2 * px + py]
                out.append(remote(a, j, ins[a].at[2 * px + py], outs[a].at[me], (px, py, c)))
                landed.append(remote(a, j, slot, slot, (px, py, c)))
        return out, landed

    def start(self, ins, outs, send_sems, recv_sems):
        for make in self._copies(ins, outs, send_sems, recv_sems)[0]:
            make().start()

    def finish(self, ins, outs, send_sems, recv_sems):
        out, landed = self._copies(ins, outs, send_sems, recv_sems)
        for make in landed:
            make().wait_recv()
        for make in out:
            make().wait_send()


def _comm_call(name, comm, arrs):
    n = comm.n

    def body(*refs):
        parts = (refs[:n], refs[n:2 * n]) + tuple(refs[2 * n:])
        comm.start(*parts)
        comm.finish(*parts)

    return pl.pallas_call(body, name=name, in_specs=[ANY] * n, out_specs=[ANY] * n, out_shape=comm.out_shape,
                          scratch_shapes=comm.scratch)(*arrs)


def _gather_chips(name, arrs):
    return _comm_call(name, _GatherChips(arrs), arrs)


def _swap_halves(name, parts):
    n = len(parts)

    def body(*refs):
        ins, outs = refs[:n], refs[n:2 * n]
        send_sems, recv_sems = refs[2 * n:]
        x, y, c, me, sibling, chips = _place()
        cps = []
        for a in range(n):
            cp = pltpu.make_async_remote_copy(src_ref=ins[a].at[:, 1 - c], dst_ref=outs[a], send_sem=send_sems.at[a],
                                              recv_sem=recv_sems.at[a], device_id=sibling, device_id_type=MESH)
            cp.start()
            cps.append(cp)
        for cp in cps:
            cp.wait()

    return pl.pallas_call(
        body, name=name, in_specs=[ANY] * n, out_specs=[ANY] * n,
        out_shape=[jax.ShapeDtypeStruct((N_CHIPS,) + p.shape[2:], p.dtype) for p in parts],
        scratch_shapes=[pltpu.SemaphoreType.DMA((n,)), pltpu.SemaphoreType.DMA((n,))],
    )(*parts)


def _scatter_chips(name, sums):
    return _comm_call(name, _ScatterChips(sums), sums)


def _swap_reduced(name, halves):
    n = len(halves)

    def body(*refs):
        ins, outs = refs[:n], refs[n:2 * n]
        send_sems, recv_sems = refs[2 * n:]
        x, y, c, me, sibling, chips = _place()
        cps = []
        for a in range(n):
            cp = pltpu.make_async_remote_copy(src_ref=ins[a], dst_ref=outs[a], send_sem=send_sems.at[a],
                                              recv_sem=recv_sems.at[a], device_id=sibling, device_id_type=MESH)
            cp.start()
            cps.append(cp)
        for cp in cps:
            cp.wait()

    return pl.pallas_call(
        body, name=name, in_specs=[ANY] * n, out_specs=[ANY] * n,
        out_shape=[jax.ShapeDtypeStruct(h.shape, h.dtype) for h in halves],
        scratch_shapes=[pltpu.SemaphoreType.DMA((n,)), pltpu.SemaphoreType.DMA((n,))],
    )(*halves)


def _row_tile(r, want=256):
    t = (min(r, want) // SUBLANES) * SUBLANES
    while r % t:
        t -= SUBLANES
    return t


def _add_own_half(name, part, got, c, out_dtype):
    _, _, r, cols = part.shape
    tr = _row_tile(r)

    def body(c_ref, p_ref, g_ref, o_ref):
        o_ref[...] = (p_ref[...].astype(F32) + g_ref[...].astype(F32)).astype(o_ref.dtype)

    return pl.pallas_call(
        body, name=name,
        grid_spec=pltpu.PrefetchScalarGridSpec(
            num_scalar_prefetch=1, grid=(N_CHIPS, r // tr),
            in_specs=[pl.BlockSpec((None, None, tr, cols), lambda s, i, c_ref: (s, c_ref[0], i, 0)),
                      pl.BlockSpec((None, tr, cols), lambda s, i, c_ref: (s, i, 0))],
            out_specs=pl.BlockSpec((None, tr, cols), lambda s, i, c_ref: (s, i, 0))),
        out_shape=jax.ShapeDtypeStruct(got.shape, out_dtype),
        compiler_params=_cparams(("parallel", "parallel")),
    )(c.reshape(1).astype(jnp.int32), part, got)


def _sum_chips(name, got, own, me):
    _, r, cols = got.shape
    tr = _row_tile(r)

    def body(me_ref, r0, r1, r2, r3, own_ref, o_ref):
        pick = lambda s, ref: jnp.where(me_ref[0] == s, own_ref[...], ref[...]).astype(F32)
        o_ref[...] = ((pick(0, r0) + pick(1, r1)) + pick(2, r2)) + pick(3, r3)

    def slot(s):
        return pl.BlockSpec((None, tr, cols),
                            lambda i, me_ref: (jnp.where(me_ref[0] == s, (s + 1) % N_CHIPS, s), i, 0))

    return pl.pallas_call(
        body, name=name,
        grid_spec=pltpu.PrefetchScalarGridSpec(
            num_scalar_prefetch=1, grid=(r // tr,),
            in_specs=[slot(s) for s in range(N_CHIPS)]
            + [pl.BlockSpec((None, tr, cols), lambda i, me_ref: (me_ref[0], i, 0))],
            out_specs=pl.BlockSpec((tr, cols), lambda i, me_ref: (i, 0))),
        out_shape=jax.ShapeDtypeStruct((r, cols), F32),
        compiler_params=_cparams(("parallel",)),
    )(me.reshape(1).astype(jnp.int32), got, got, got, got, own)


def _adamw_math(wv, gv, mv, vv):
    mv = ADAM_B1 * mv + (1.0 - ADAM_B1) * gv
    vv = ADAM_B2 * vv + (1.0 - ADAM_B2) * (gv * gv)
    m_hat = mv / (1.0 - ADAM_B1 ** ADAM_STEP)
    v_hat = vv / (1.0 - ADAM_B2 ** ADAM_STEP)
    return -ADAM_LR * (m_hat / (jnp.sqrt(v_hat) + ADAM_EPS) + ADAM_WD * wv), mv, vv


def _adamw(name, w, g, m, v):
    cols = w.shape[1]
    return _ew(name, _adamw_math, [('r', w), ('r', g), ('r', m), ('r', v)], [('r', cols, F32)] * 3,
               tr=_row_tile(w.shape[0], max(LANES, LANES * LANES // cols)))


def _adamw_halves(name, w, mine, theirs, m, v, c):
    r, cols = mine.shape
    tr = _row_tile(r, 256)
    nb = r // tr

    def body(c_ref, w_ref, a_ref, b_ref, m_ref, v_ref, g_out, d_out, m_out, v_out):
        g = jnp.where(pl.program_id(0) == c_ref[0], a_ref[...], b_ref[...])
        g_out[...] = g
        d_out[...], m_out[...], v_out[...] = _adamw_math(w_ref[...], g, m_ref[...], v_ref[...])

    whole = pl.BlockSpec((tr, cols), lambda h, i, c_ref: (h * nb + i, 0))
    half = pl.BlockSpec((tr, cols), lambda h, i, c_ref: (i, 0))
    return pl.pallas_call(
        body, name=name,
        grid_spec=pltpu.PrefetchScalarGridSpec(
            num_scalar_prefetch=1, grid=(2, nb),
            in_specs=[whole, half, half, whole, whole], out_specs=[whole] * 4),
        out_shape=[jax.ShapeDtypeStruct(w.shape, F32)] * 4,
        compiler_params=_cparams(("parallel", "parallel")),
    )(c.reshape(1).astype(jnp.int32), w, mine, theirs, m, v)


SMALL_ROWS_ALIGN = 2 * N_CHIPS * SUBLANES


MEDIUM_NAMES = ['ssm_b_re', 'ssm_b_im', 'ssm_c_re', 'ssm_c_im']
PACKED_NAMES = [n for n in SMALL_NAMES if n not in MEDIUM_NAMES]


def _pack_small(d):
    flat = jnp.concatenate([d[n].reshape(-1).astype(F32) for n in PACKED_NAMES])
    rows = -(-flat.shape[0] // (LANES * SMALL_ROWS_ALIGN)) * SMALL_ROWS_ALIGN
    return jnp.pad(flat, (0, rows * LANES - flat.shape[0])).reshape(rows, LANES)


def _unpack_small(packed, like):
    flat = packed.reshape(-1)
    out, off = {}, 0
    for n in PACKED_NAMES:
        size = like[n].size
        out[n] = flat[off:off + size].reshape(like[n].shape)
        off += size
    return out


def kernel(x, g_mix, w_in, q_gain, k_gain, rpb, ssm_a_re, ssm_a_im, ssm_b_re, ssm_b_im, ssm_c_re, ssm_c_im, ssm_log_step, ssm_d, w_glu, b_glu, g_out_attn, g_out_ssm, w_out, g_ffn, w_ffn_gate, w_ffn_up, w_ffn_down, loss_target, m_g_mix, m_w_in, m_q_gain, m_k_gain, m_rpb, m_ssm_a_re, m_ssm_a_im, m_ssm_b_re, m_ssm_b_im, m_ssm_c_re, m_ssm_c_im, m_ssm_log_step, m_ssm_d, m_w_glu, m_b_glu, m_g_out_attn, m_g_out_ssm, m_w_out, m_g_ffn, m_w_ffn_gate, m_w_ffn_up, m_w_ffn_down, v_g_mix, v_w_in, v_q_gain, v_k_gain, v_rpb, v_ssm_a_re, v_ssm_a_im, v_ssm_b_re, v_ssm_b_im, v_ssm_c_re, v_ssm_c_im, v_ssm_log_step, v_ssm_d, v_w_glu, v_b_glu, v_g_out_attn, v_g_out_ssm, v_w_out, v_g_ffn, v_w_ffn_gate, v_w_ffn_up, v_w_ffn_down):
    given = dict(locals())
    w = {n: given[n][0] for n in WEIGHT_NAMES}
    mom = {n: given["m_" + n][0] for n in WEIGHT_NAMES}
    var = {n: given["v_" + n][0] for n in WEIGHT_NAMES}
    d = x.shape[-1]
    c = lax.axis_index("c")

    halves = {n: w[n].astype(MXU_DTYPE).reshape((2, w[n].shape[0] // 2, w[n].shape[1])) for n in BIG_NAMES}
    (w_in4,) = _gather_chips("gather_w_in", [halves['w_in']])
    w_in4 = w_in4.reshape((N_CHIPS, -1, w_in4.shape[-1]))

    def chip_sums(tag, grads, payload):
        parts = [g.reshape((N_CHIPS, 2, -1, g.shape[-1])) for g in grads]
        got = _swap_halves("reduce_swap_halves_" + tag, parts)
        return [_add_own_half("reduce_add_%s_%d" % (tag, a), p, gt, c, dt)
                for a, (p, gt, dt) in enumerate(zip(parts, got, payload))]

    reduce_late = lambda tag, grads: chip_sums(tag, grads, [GRAD_PAYLOAD_DTYPE] * len(grads))
    reduce_mid = lambda grads: chip_sums("mid", grads, [GRAD_PAYLOAD_DTYPE] + [F32] * (len(grads) - 1))
    sq, dx, (sums_late, got_late), (sums_mid, got_mid), d_small = _local_step(
        x[0], loss_target[0], w_in4, ('halves', [halves[n] for n in LATE_NAMES]), {n: w[n] for n in SMALL_NAMES},
        reduce_late, reduce_mid)
    loss = lax.psum(0.5 * sq / d, ("x", "y", "c"))

    nbig = len(BIG_NAMES)
    sums_tiny = chip_sums("tiny", [_pack_small(d_small)], [F32])
    got_tiny = list(_scatter_chips("reduce_scatter_tiny", sums_tiny))
    sums = sums_mid[:1] + sums_late + sums_mid[1:] + sums_tiny
    got = got_mid[:1] + got_late + got_mid[1:] + got_tiny
    me = 2 * lax.axis_index("x") + lax.axis_index("y")
    mine = [_sum_chips("reduce_sum_%d" % a, gt, sm_, me) for a, (gt, sm_) in enumerate(zip(got, sums))]
    theirs = _swap_reduced("reduce_swap_reduced", mine)
    in_order = lambda a: jnp.where(c == 0, jnp.stack([mine[a], theirs[a]]), jnp.stack([theirs[a], mine[a]]))
    repl = _gather_chips("gather_small", [in_order(a) for a in range(nbig, len(mine))])
    repl = [r.reshape(-1, LANES) for r in repl]
    like = {n: w[n] for n in SMALL_NAMES}
    grad_small = _unpack_small(repl[-1], like)
    grad_small.update({n: r.reshape(w[n].shape) for n, r in zip(MEDIUM_NAMES, repl)})

    grad_big, delta, new_m, new_v = {}, {}, {}, {}
    for a, n in enumerate(BIG_NAMES):
        grad_big[n], delta[n], new_m[n], new_v[n] = _adamw_halves("adamw_%d" % a, w[n], mine[a], theirs[a],
                                                                  mom[n], var[n], c)
    for n in MEDIUM_NAMES:
        flat = lambda t: t.reshape(-1, w[n].shape[-1])
        res = _adamw("adamw_" + n, flat(w[n]), flat(grad_small[n]), flat(mom[n]), flat(var[n]))
        delta[n], new_m[n], new_v[n] = (t.reshape(w[n].shape) for t in res)
    sd, sm, sv = _adamw("adamw_small", _pack_small(w), repl[-1], _pack_small(mom), _pack_small(var))
    delta.update(_unpack_small(sd, like))
    new_m.update(_unpack_small(sm, like))
    new_v.update(_unpack_small(sv, like))
    grads = {**grad_big, **grad_small}
    lead = lambda t: t[None]
    return (loss, dx[None], *[lead(grads[n]) for n in WEIGHT_NAMES], *[lead(delta[n]) for n in WEIGHT_NAMES],
            *[lead(new_m[n]) for n in WEIGHT_NAMES], *[lead(new_v[n]) for n in WEIGHT_NAMES])
```

```python
import functools
import math

import jax
import jax.numpy as jnp
from jax import lax
from jax.experimental import pallas as pl
from jax.experimental.pallas import tpu as pltpu

F32 = jnp.float32
BF16 = jnp.bfloat16
MXU_DTYPE = BF16
GRAD_PAYLOAD_DTYPE = BF16
DW_DTYPE = BF16
S5_DTYPE = BF16
HI = lax.Precision.HIGHEST
VMEM_LIMIT_V7X = 56 * 1024 * 1024
LANES = 128
SUBLANES = 8

GRID_W = 64
WIN_H = 8
WIN_W = 16
HEAD_DIM = 64
SSM_GROUP_CH = 16
SSM_STATE = 64
S5_CHUNK = 16
S5_GROUPS_PER_STEP = 16
RMS_EPS = 1e-6
NEG_INF = -1e30
N_CHIPS = 4
MESH = pl.DeviceIdType.MESH

ADAM_LR = 0.001
ADAM_B1 = 0.9
ADAM_B2 = 0.999
ADAM_EPS = 1e-08
ADAM_WD = 0.01
ADAM_STEP = 10

WEIGHT_NAMES = ['g_mix', 'w_in', 'q_gain', 'k_gain', 'rpb', 'ssm_a_re', 'ssm_a_im', 'ssm_b_re', 'ssm_b_im',
                'ssm_c_re', 'ssm_c_im', 'ssm_log_step', 'ssm_d', 'w_glu', 'b_glu', 'g_out_attn', 'g_out_ssm',
                'w_out', 'g_ffn', 'w_ffn_gate', 'w_ffn_up', 'w_ffn_down']
BIG_NAMES = ['w_in', 'w_glu', 'w_out', 'w_ffn_gate', 'w_ffn_up', 'w_ffn_down']
LATE_NAMES = BIG_NAMES[1:]
SMALL_NAMES = [n for n in WEIGHT_NAMES if n not in BIG_NAMES]


def _cparams(sem):
    return pltpu.CompilerParams(dimension_semantics=sem, vmem_limit_bytes=VMEM_LIMIT_V7X)


def _tile(n, want):
    if n <= want:
        return n
    t = (want // LANES) * LANES
    while t >= LANES:
        if n % t == 0:
            return t
        t -= LANES
    return n


def _mm(name, a, b, *, contract, a_mode='2', b_mode='2', o_mode='2', out_dtype=F32, add=None, exact=False,
        tm=1024, tn=1024, tk=2048, comm=None, comm_arrs=()):
    dn = {'nn': (((1,), (0,)), ((), ())), 'nt': (((1,), (1,)), ((), ())), 'tn': (((0,), (0,)), ((), ()))}[contract]
    ar, ac = a.shape[-2:]
    br, bc = b.shape[-2:]
    m, kdim = (ar, ac) if contract != 'tn' else (ac, ar)
    n = bc if contract != 'nt' else br
    assert kdim == (br if contract != 'nt' else bc), (name, a.shape, b.shape)
    nbatch = 1
    for arr, mode in ((a, a_mode), (b, b_mode)):
        if mode == 'b':
            nbatch = arr.shape[0]
    nstack = 1
    for arr, mode in ((a, a_mode), (b, b_mode)):
        if mode == 'c':
            nstack = arr.shape[0]
    tm, tn, tk = _tile(m, tm), _tile(n, tn), _tile(kdim, tk)
    nkin = kdim // tk
    nk = nstack * nkin
    grid = (nbatch, m // tm, n // tn, nk)

    def spec(mode, block, rc):
        def imap(s, i, j, kk):
            r, c = rc(i, j, kk % nkin)
            if mode == '2':
                return (r, c)
            return (s if mode == 'b' else kk // nkin, r, c)
        return pl.BlockSpec(block if mode == '2' else (None,) + block, imap)

    a_spec = spec(a_mode, (tm, tk) if contract != 'tn' else (tk, tm),
                  (lambda i, j, k: (i, k)) if contract != 'tn' else (lambda i, j, k: (k, i)))
    b_spec = spec(b_mode, (tk, tn) if contract != 'nt' else (tn, tk),
                  (lambda i, j, k: (k, j)) if contract != 'nt' else (lambda i, j, k: (j, k)))
    o_spec = spec(o_mode, (tm, tn), lambda i, j, k: (i, j))
    out_shape = (m, n) if o_mode == '2' else (nbatch, m, n)
    has_add = add is not None

    def product(a_ref, b_ref):
        if exact:
            return lax.dot_general(a_ref[...].astype(F32), b_ref[...].astype(F32), dn, precision=HI,
                                   preferred_element_type=F32)
        return lax.dot_general(a_ref[...].astype(MXU_DTYPE), b_ref[...].astype(MXU_DTYPE), dn,
                               preferred_element_type=F32)

    ncomm = len(comm_arrs)
    nacc = int(nk > 1)

    def body(*refs):
        a_ref, b_ref = refs[:2]
        add_ref = refs[2] if has_add else None
        c_ins = refs[2 + has_add:2 + has_add + ncomm]
        o_ref = refs[2 + has_add + ncomm]
        c_outs = refs[3 + has_add + ncomm:3 + has_add + 2 * ncomm]
        sems = refs[3 + has_add + 2 * ncomm + nacc:]
        ids = [pl.program_id(ax) for ax in range(4)]
        if comm is not None:
            @pl.when((ids[0] == 0) & (ids[1] == 0) & (ids[2] == 0) & (ids[3] == 0))
            def _():
                comm.start(c_ins, c_outs, *sems)

        def write(r):
            if has_add:
                r = r + add_ref[...].astype(F32)
            o_ref[...] = r.astype(o_ref.dtype)

        if nk == 1:
            write(product(a_ref, b_ref))
        else:
            acc_ref = refs[3 + has_add + 2 * ncomm]

            @pl.when(ids[3] == 0)
            def _():
                acc_ref[...] = jnp.zeros_like(acc_ref)

            acc_ref[...] += product(a_ref, b_ref)

            @pl.when(ids[3] == nk - 1)
            def _():
                write(acc_ref[...])

        if comm is not None:
            @pl.when((ids[0] == grid[0] - 1) & (ids[1] == grid[1] - 1) & (ids[2] == grid[2] - 1) & (ids[3] == nk - 1))
            def _():
                comm.finish(c_ins, c_outs, *sems)

    in_specs = [a_spec, b_spec] + ([o_spec] if has_add else []) + [ANY] * ncomm
    args = (a, b) + ((add,) if has_add else ()) + tuple(comm_arrs)
    res = pl.pallas_call(
        body, name=name, grid=grid, in_specs=in_specs, out_specs=[o_spec] + [ANY] * ncomm,
        out_shape=[jax.ShapeDtypeStruct(out_shape, out_dtype)] + (comm.out_shape if comm is not None else []),
        scratch_shapes=([pltpu.VMEM((tm, tn), F32)] if nk > 1 else []) + (comm.scratch if comm is not None else []),
        compiler_params=_cparams(("parallel", "parallel", "parallel", "arbitrary") if comm is None
                                 else ("arbitrary",) * 4),
    )(*args)
    return res[0] if comm is None else (res[0], res[1:])


def _ew(name, fn, ins, outs, tr=512):
    rows = next(x[1].shape[0] for x in ins if x[0] == 'r')
    tr = min(tr, rows)
    assert rows % tr == 0 and tr % SUBLANES == 0, (name, rows, tr)
    in_specs, args = [], []
    for x in ins:
        if x[0] == 'r' and len(x) == 2:
            in_specs.append(pl.BlockSpec((tr, x[1].shape[1]), lambda i: (i, 0)))
        elif x[0] == 'r':
            in_specs.append(pl.BlockSpec((tr, x[3]), functools.partial(lambda cb, i: (i, cb), x[2])))
        else:
            in_specs.append(pl.BlockSpec(x[1].shape, lambda i: (0, 0)))
        args.append(x[1])
    out_specs, out_shapes = [], []
    for o in outs:
        if o[0] == 'r':
            out_specs.append(pl.BlockSpec((tr, o[1]), lambda i: (i, 0)))
            out_shapes.append(jax.ShapeDtypeStruct((rows, o[1]), o[2]))
        else:
            out_specs.append(pl.BlockSpec((SUBLANES, o[1]), lambda i: (0, 0)))
            out_shapes.append(jax.ShapeDtypeStruct((SUBLANES, o[1]), F32))
    nin = len(ins)
    has_acc = any(o[0] == 'a' for o in outs)

    def body(*refs):
        vals = fn(*[r[...].astype(F32) for r in refs[:nin]])
        if not isinstance(vals, (tuple, list)):
            vals = (vals,)
        i = pl.program_id(0)
        for o, ref, v in zip(outs, refs[nin:], vals):
            if o[0] == 'r':
                ref[...] = v.astype(ref.dtype)
            else:
                part = v.astype(F32).reshape(tr // SUBLANES, SUBLANES, o[1]).sum(axis=0)

                @pl.when(i == 0)
                def _(ref=ref, part=part):
                    ref[...] = part

                @pl.when(i > 0)
                def _(ref=ref, part=part):
                    ref[...] += part

    res = pl.pallas_call(
        body, name=name, grid=(rows // tr,), in_specs=in_specs, out_specs=out_specs, out_shape=out_shapes,
        compiler_params=_cparams(("arbitrary",) if has_acc else ("parallel",)),
    )(*args)
    return res


def _ffn_down_loss(act4, w_down4, x1, target, tm=1024, tn=1024):
    ns, t, ffs = act4.shape
    d = w_down4.shape[2]
    tm, tn = _tile(t, tm), _tile(d, tn)

    def body(a_ref, b_ref, x1_ref, t_ref, g_ref, gb_ref, sq_ref, acc_ref):
        k = pl.program_id(2)

        @pl.when(k == 0)
        def _():
            acc_ref[...] = jnp.zeros_like(acc_ref)

        acc_ref[...] += lax.dot_general(a_ref[...].astype(MXU_DTYPE), b_ref[...].astype(MXU_DTYPE), NN_DIMS,
                                        preferred_element_type=F32)

        @pl.when(k == ns - 1)
        def _():
            diff = (acc_ref[...] + x1_ref[...]) - t_ref[...]
            g = diff * (1.0 / d)
            g_ref[...] = g
            gb_ref[...] = g.astype(gb_ref.dtype)
            sq_ref[...] = (diff * diff).reshape(tm // SUBLANES, SUBLANES, tn).sum(axis=0)

    tile = pl.BlockSpec((tm, tn), lambda i, j, k: (i, j))
    return pl.pallas_call(
        body, name="ffn_down_loss", grid=(t // tm, d // tn, ns),
        in_specs=[pl.BlockSpec((None, tm, ffs), lambda i, j, k: (k, i, 0)),
                  pl.BlockSpec((None, ffs, tn), lambda i, j, k: (k, 0, j)), tile, tile],
        out_specs=[tile, tile, pl.BlockSpec((SUBLANES, tn), lambda i, j, k: (i, j))],
        out_shape=[jax.ShapeDtypeStruct((t, d), F32), jax.ShapeDtypeStruct((t, d), MXU_DTYPE),
                   jax.ShapeDtypeStruct((t // tm * SUBLANES, d), F32)],
        scratch_shapes=[pltpu.VMEM((tm, tn), F32)],
        compiler_params=_cparams(("parallel", "parallel", "arbitrary")),
    )(act4, w_down4, x1, target)


def _rms(x, g):
    r = lax.rsqrt(jnp.mean(x * x, axis=-1, keepdims=True) + RMS_EPS)
    xr = x * r
    return xr * g, xr


def _rms_bwd(x, g, dy):
    r = lax.rsqrt(jnp.mean(x * x, axis=-1, keepdims=True) + RMS_EPS)
    xr = x * r
    gdy = g * dy
    dx = r * (gdy - xr * jnp.mean(xr * gdy, axis=-1, keepdims=True))
    return dx, dy * xr


def _sigmoid(x):
    return 0.5 * (jnp.tanh(0.5 * x) + 1.0)


_GELU_C = math.sqrt(2.0 / math.pi)


def _gelu(x):
    return 0.5 * x * (1.0 + jnp.tanh(_GELU_C * (x + 0.044715 * x * x * x)))


def _gelu_grad(x):
    t = jnp.tanh(_GELU_C * (x + 0.044715 * x * x * x))
    return 0.5 * (1.0 + t) + 0.5 * x * (1.0 - t * t) * _GELU_C * (1.0 + 3 * 0.044715 * x * x)


ATTN_ROWS_PER_STEP = 16
ATTN_BWD_ROWS_TOGETHER = 8
NT_DIMS = (((1,), (1,)), ((), ()))
NN_DIMS = (((1,), (0,)), ((), ()))
TN_DIMS = (((0,), (0,)), ((), ()))


def _attn_geometry(r, rows):
    row_start = jnp.clip(r - WIN_H // 2, 0, rows - WIN_H)
    key0 = pl.multiple_of(row_start * GRID_W, GRID_W)
    bias0 = pl.multiple_of((row_start - r + (WIN_H - 1)) * GRID_W, GRID_W)
    return key0, bias0


def _window_onehot():
    c = jnp.arange(GRID_W)
    col_start = jnp.clip(c - WIN_W // 2, 0, GRID_W - WIN_W)
    col_in = (c[None, :] >= col_start[:, None]) & (c[None, :] < col_start[:, None] + WIN_W)
    dc = jnp.clip(c[None, :] - c[:, None], -(WIN_W - 1), WIN_W - 1) + (WIN_W - 1)
    onehot = ((dc[:, :, None] == jnp.arange(2 * WIN_W - 1)[None, None, :]) & col_in[:, :, None]).astype(F32)
    return onehot, col_in


def _bias_table(rpb):
    onehot, col_in = _window_onehot()
    nh = rpb.shape[0]
    pairs = rpb.reshape(nh // 2, 2, 2 * WIN_H - 1, 2 * WIN_W - 1)
    mask = jnp.where(col_in, 0.0, NEG_INF).T
    heads = [jnp.einsum('prd,qkd->prkq', pairs[:, e], onehot, precision=HI) + mask for e in range(2)]
    return jnp.concatenate(heads, axis=-1).reshape(nh // 2, (2 * WIN_H - 1) * GRID_W, 2 * GRID_W)


def _bias_table_grad(dtab):
    onehot, _ = _window_onehot()
    npair = dtab.shape[0]
    d = dtab.reshape(npair, 2 * WIN_H - 1, GRID_W, 2 * GRID_W)
    heads = [jnp.einsum('prkq,qkd->prd', d[..., e * GRID_W:(e + 1) * GRID_W], onehot, precision=HI) for e in range(2)]
    return jnp.stack(heads, axis=1).reshape(2 * npair, 2 * WIN_H - 1, 2 * WIN_W - 1)


def _lane_lo(shape):
    return lax.broadcasted_iota(jnp.int32, shape, 1) < HEAD_DIM


def _half_sums(v):
    lo = _lane_lo(v.shape)
    s_lo = jnp.sum(jnp.where(lo, v, 0.0), axis=1, keepdims=True)
    s_hi = jnp.sum(jnp.where(lo, 0.0, v), axis=1, keepdims=True)
    return jnp.where(lo, s_lo, s_hi)


def _rms_pair(x, g):
    r = lax.rsqrt(_half_sums(x * x) * (1.0 / HEAD_DIM) + RMS_EPS)
    return x * r * g


def _rms_pair_bwd(x, g, dy):
    r = lax.rsqrt(_half_sums(x * x) * (1.0 / HEAD_DIM) + RMS_EPS)
    xr = x * r
    gdy = g * dy
    dx = r * (gdy - xr * (_half_sums(xr * gdy) * (1.0 / HEAD_DIM)))
    return dx, dy * xr


def _blockdiag(a):
    a2 = jnp.concatenate([a, a], axis=0)
    row_hi = lax.broadcasted_iota(jnp.int32, a2.shape, 0) >= GRID_W
    lane_hi = lax.broadcasted_iota(jnp.int32, a2.shape, 1) >= HEAD_DIM
    return jnp.where(row_hi == lane_hi, a2, 0.0).astype(MXU_DTYPE)


def _diag_blocks(m):
    return jnp.where(_lane_lo((GRID_W, 2 * HEAD_DIM)), m[:GRID_W], m[GRID_W:])


def _attn_scores(qb, kb, bias):
    st = lax.dot_general(kb, qb, NT_DIMS, preferred_element_type=F32)
    st = st * (1.0 / math.sqrt(HEAD_DIM)) + bias
    mx = jnp.max(st, axis=0, keepdims=True)
    p = jnp.exp(st - mx)
    return p * (1.0 / jnp.sum(p, axis=0, keepdims=True))


def _attn_fwd(z4, qg2, kg2, bias_t, comm=None, comm_arrs=()):
    _, t, aw = z4.shape
    rows = t // GRID_W
    npair = aw // (2 * HEAD_DIM)
    nkeys = WIN_H * GRID_W
    nb = bias_t.shape[1]
    rps = min(ATTN_ROWS_PER_STEP, rows)
    blk = rps * GRID_W
    nsteps = rows // rps
    ncomm = len(comm_arrs)

    def body(*refs):
        q_ref, k_ref, v_ref, qg_ref, kg_ref, b_ref = refs[:6]
        c_ins, o_ref, c_outs = refs[6:6 + ncomm], refs[6 + ncomm], refs[7 + ncomm:7 + 2 * ncomm]
        kn_ref, vb_ref = refs[7 + 2 * ncomm:9 + 2 * ncomm]
        sems = refs[9 + 2 * ncomm:]
        pr, rb = pl.program_id(0), pl.program_id(1)
        if comm is not None:
            @pl.when((pr == 0) & (rb == 0))
            def _():
                comm.start(c_ins, c_outs, *sems)

        @pl.when(rb == 0)
        def _():
            kn_ref[...] = _rms_pair(k_ref[...], kg_ref[...]).astype(MXU_DTYPE)
            vb_ref[...] = v_ref[...].astype(MXU_DTYPE)

        def row(i, carry):
            key0, bias0 = _attn_geometry(rb * rps + i, rows)
            at = pl.ds(pl.multiple_of(i * GRID_W, GRID_W), GRID_W)
            qb = _blockdiag(_rms_pair(q_ref[at, :], qg_ref[...]))
            pt = _attn_scores(qb, kn_ref[pl.ds(key0, nkeys), :], b_ref[pl.ds(bias0, nkeys), :])
            both = lax.dot_general(pt.astype(MXU_DTYPE), vb_ref[pl.ds(key0, nkeys), :], TN_DIMS,
                                   preferred_element_type=F32)
            o_ref[at, :] = _diag_blocks(both)
            return carry

        lax.fori_loop(0, rps, row, 0, unroll=8)
        if comm is not None:
            @pl.when((pr == npair - 1) & (rb == nsteps - 1))
            def _():
                comm.finish(c_ins, c_outs, *sems)

    pair_cols = lambda lead: pl.BlockSpec((None, t, 2 * HEAD_DIM), lambda p, r: (lead, 0, p))
    res = pl.pallas_call(
        body, name="attn_fwd", grid=(npair, nsteps),
        in_specs=[pl.BlockSpec((None, blk, 2 * HEAD_DIM), lambda p, r: (0, r, p)), pair_cols(1), pair_cols(2),
                  pl.BlockSpec((1, 2 * HEAD_DIM), lambda p, r: (0, 0)),
                  pl.BlockSpec((1, 2 * HEAD_DIM), lambda p, r: (0, 0)),
                  pl.BlockSpec((None, nb, 2 * GRID_W), lambda p, r: (p, 0, 0))] + [ANY] * ncomm,
        out_specs=[pl.BlockSpec((blk, 2 * HEAD_DIM), lambda p, r: (r, p))] + [ANY] * ncomm,
        out_shape=[jax.ShapeDtypeStruct((t, aw), F32)] + (comm.out_shape if comm is not None else []),
        scratch_shapes=[pltpu.VMEM((t, 2 * HEAD_DIM), MXU_DTYPE), pltpu.VMEM((t, 2 * HEAD_DIM), MXU_DTYPE)]
        + (comm.scratch if comm is not None else []),
        compiler_params=_cparams(("arbitrary", "arbitrary")),
    )(z4, z4, z4, qg2, kg2, bias_t, *comm_arrs)
    return res[0], res[1:]


def _attn_bwd(z4, qg2, kg2, bias_t, dya, comm=None, comm_arrs=()):
    _, t, aw = z4.shape
    rows = t // GRID_W
    npair = aw // (2 * HEAD_DIM)
    nkeys = WIN_H * GRID_W
    nb = bias_t.shape[1]
    rps = min(ATTN_ROWS_PER_STEP, rows)
    blk = rps * GRID_W
    nsteps = rows // rps
    scale = 1.0 / math.sqrt(HEAD_DIM)
    ncomm = len(comm_arrs)

    def body(*refs):
        q_ref, k_ref, v_ref, qg_ref, kg_ref, b_ref, do_ref = refs[:7]
        c_ins = refs[7:7 + ncomm]
        dz_ref, db_ref, dqg_ref, dkg_ref = refs[7 + ncomm:11 + ncomm]
        c_outs = refs[11 + ncomm:11 + 2 * ncomm]
        kn_ref, vb_ref, dkn_ref, dv_ref = refs[11 + 2 * ncomm:15 + 2 * ncomm]
        sems = refs[15 + 2 * ncomm:]
        pr, rb = pl.program_id(0), pl.program_id(1)
        if comm is not None:
            @pl.when((pr == 0) & (rb == 0))
            def _():
                comm.start(c_ins, c_outs, *sems)

        @pl.when(rb == 0)
        def _():
            kn_ref[...] = _rms_pair(k_ref[...], kg_ref[...]).astype(MXU_DTYPE)
            vb_ref[...] = v_ref[...].astype(MXU_DTYPE)
            dkn_ref[...] = jnp.zeros_like(dkn_ref)
            dv_ref[...] = jnp.zeros_like(dv_ref)
            db_ref[...] = jnp.zeros_like(db_ref)
            dqg_ref[...] = jnp.zeros_like(dqg_ref)

        def row(i, dqg_sum):
            r = rb * rps + i
            key0, bias0 = _attn_geometry(r, rows)
            keys = pl.ds(key0, nkeys)
            at = pl.ds(pl.multiple_of(i * GRID_W, GRID_W), GRID_W)
            q = q_ref[at, :]
            qb = _blockdiag(_rms_pair(q, qg_ref[...]))
            dob = _blockdiag(do_ref[at, :])
            kb = kn_ref[keys, :]
            pt = _attn_scores(qb, kb, b_ref[pl.ds(bias0, nkeys), :])
            dv_ref[keys, :] += lax.dot_general(pt.astype(MXU_DTYPE), dob, NN_DIMS, preferred_element_type=F32)
            dpt = lax.dot_general(vb_ref[keys, :], dob, NT_DIMS, preferred_element_type=F32)
            dst = pt * (dpt - jnp.sum(pt * dpt, axis=0, keepdims=True))
            db_ref[pl.ds(bias0, nkeys), :] += dst
            dsb = dst.astype(MXU_DTYPE)
            dkn_ref[keys, :] += scale * lax.dot_general(dsb, qb, NN_DIMS, preferred_element_type=F32)
            dqn = scale * _diag_blocks(lax.dot_general(dsb, kb, TN_DIMS, preferred_element_type=F32))
            dq, dqg = _rms_pair_bwd(q, qg_ref[...], dqn)
            dz_ref[0, pl.ds(pl.multiple_of(r * GRID_W, GRID_W), GRID_W), :] = dq.astype(dz_ref.dtype)
            return dqg_sum + jnp.sum(dqg, axis=0, keepdims=True)

        def rows_together(i, acc):
            for j in range(ATTN_BWD_ROWS_TOGETHER):
                acc = row(ATTN_BWD_ROWS_TOGETHER * i + j, acc)
            return acc

        dqg_ref[...] += lax.fori_loop(0, rps // ATTN_BWD_ROWS_TOGETHER, rows_together,
                                      jnp.zeros((1, 2 * HEAD_DIM), F32))

        @pl.when(rb == nsteps - 1)
        def _():
            dk, dkg = _rms_pair_bwd(k_ref[...], kg_ref[...], dkn_ref[...])
            dz_ref[1] = dk.astype(dz_ref.dtype)
            dz_ref[2] = dv_ref[...].astype(dz_ref.dtype)
            dkg_ref[...] = jnp.sum(dkg, axis=0, keepdims=True)

        if comm is not None:
            @pl.when((pr == npair - 1) & (rb == nsteps - 1))
            def _():
                comm.finish(c_ins, c_outs, *sems)

    pair_cols = lambda lead: pl.BlockSpec((None, t, 2 * HEAD_DIM), lambda p, r: (lead, 0, p))
    pair_vec = pl.BlockSpec((None, 1, 2 * HEAD_DIM), lambda p, r: (p, 0, 0))
    res = pl.pallas_call(
        body, name="attn_bwd", grid=(npair, nsteps),
        in_specs=[pl.BlockSpec((None, blk, 2 * HEAD_DIM), lambda p, r: (0, r, p)), pair_cols(1), pair_cols(2),
                  pl.BlockSpec((1, 2 * HEAD_DIM), lambda p, r: (0, 0)),
                  pl.BlockSpec((1, 2 * HEAD_DIM), lambda p, r: (0, 0)),
                  pl.BlockSpec((None, nb, 2 * GRID_W), lambda p, r: (p, 0, 0)),
                  pl.BlockSpec((blk, 2 * HEAD_DIM), lambda p, r: (r, p))] + [ANY] * ncomm,
        out_specs=[pl.BlockSpec((3, t, 2 * HEAD_DIM), lambda p, r: (0, 0, p)),
                   pl.BlockSpec((None, nb, 2 * GRID_W), lambda p, r: (p, 0, 0)),
                   pair_vec, pair_vec] + [ANY] * ncomm,
        out_shape=[jax.ShapeDtypeStruct((4, t, aw), MXU_DTYPE), jax.ShapeDtypeStruct(bias_t.shape, F32),
                   jax.ShapeDtypeStruct((npair, 1, 2 * HEAD_DIM), F32),
                   jax.ShapeDtypeStruct((npair, 1, 2 * HEAD_DIM), F32)] + (comm.out_shape if comm is not None else []),
        scratch_shapes=[pltpu.VMEM((t, 2 * HEAD_DIM), MXU_DTYPE), pltpu.VMEM((t, 2 * HEAD_DIM), MXU_DTYPE),
                        pltpu.VMEM((t, 2 * HEAD_DIM), F32), pltpu.VMEM((t, 2 * HEAD_DIM), F32)]
        + (comm.scratch if comm is not None else []),
        compiler_params=_cparams(("arbitrary", "arbitrary")),
    )(z4, z4, z4, qg2, kg2, bias_t, dya, *comm_arrs)
    return res[:4], res[4:]


def _bmm_exact(name, a, b, dims, per_step=16):
    nb = a.shape[0]
    per = math.gcd(nb, per_step)
    shape = jax.eval_shape(lambda u, v: lax.dot_general(u, v, dims), a[0], b[0]).shape

    def body(a_ref, b_ref, o_ref):
        for e in range(per):
            o_ref[e] = lax.dot_general(a_ref[e], b_ref[e], dims, precision=HI, preferred_element_type=F32)

    blk = lambda arr: pl.BlockSpec((per,) + arr.shape[1:], lambda i: (i, 0, 0))
    out = jax.ShapeDtypeStruct((nb,) + shape, F32)
    return pl.pallas_call(body, name=name, grid=(nb // per,), in_specs=[blk(a), blk(b)], out_specs=blk(out),
                          out_shape=out, compiler_params=_cparams(("parallel",)))(a, b)


@jax.custom_vjp
def _contract_last(a, b):
    return _bmm_exact("s5_kern", a, b, NT_DIMS)


def _contract_last_fwd(a, b):
    return _contract_last(a, b), (a, b)


def _contract_last_bwd(res, g):
    a, b = res
    return _bmm_exact("s5_kern_da", g, b, NN_DIMS), _bmm_exact("s5_kern_db", g, a, TN_DIMS)


_contract_last.defvjp(_contract_last_fwd, _contract_last_bwd)


def _s5_mats(a_re, a_im, b_re, b_im, c_re, c_im, log_step, d_skip):
    nd, g, p = a_re.shape
    c = b_re.shape[-1]
    L = S5_CHUNK
    lr = jnp.minimum(a_re, -1e-4).transpose(1, 0, 2)
    li = a_im.transpose(1, 0, 2)
    dt = jnp.exp(log_step).T[..., None]
    n = jnp.arange(L + 1, dtype=F32)[None, :, None, None]
    mag = jnp.exp(n * (lr * dt)[:, None])
    ang = n * (li * dt)[:, None]
    pw_r, pw_i = mag * jnp.cos(ang), mag * jnp.sin(ang)
    den = lr * lr + li * li
    nr, ni = pw_r[:, 1] - 1.0, pw_i[:, 1]
    cr, ci = (nr * lr + ni * li) / den, (ni * lr - nr * li) / den
    bt_r, bt_i = b_re.transpose(1, 3, 0, 2), b_im.transpose(1, 3, 0, 2)
    bb_r = cr[:, None] * bt_r - ci[:, None] * bt_i
    bb_i = cr[:, None] * bt_i + ci[:, None] * bt_r
    ct_r, ct_i = c_re.transpose(1, 2, 0, 3), c_im.transpose(1, 2, 0, 3)

    def cols(x_re, x_im):
        return jnp.concatenate([x_re[..., 0, :], x_re[..., 1, :], x_im[..., 0, :], x_im[..., 1, :]], axis=-1)

    e_r = jnp.stack([pw_r[:, :L, 0][:, ::-1], pw_r[:, :L, 1]], axis=2)
    e_i = jnp.stack([pw_i[:, :L, 0][:, ::-1], pw_i[:, :L, 1]], axis=2)
    ws = (cols(e_r, e_r)[:, :, None] * cols(bb_r, bb_i)[:, None]
          + cols(e_i, e_i)[:, :, None] * cols(-bb_i, bb_r)[:, None]).reshape(g, L * c, 4 * p)
    f_r = jnp.stack([pw_r[:, 1:, 0], pw_r[:, 1:, 1][:, ::-1]], axis=2)
    f_i = jnp.stack([pw_i[:, 1:, 0], pw_i[:, 1:, 1][:, ::-1]], axis=2)
    wot = (cols(f_r, f_i)[:, :, None] * cols(ct_r, -ct_r)[:, None]
           + cols(f_i, f_r)[:, :, None] * cols(-ct_i, -ct_i)[:, None]).reshape(g, L * c, 4 * p)
    qr, qi = pw_r[:, None, :L], pw_i[:, None, :L]
    br, bi = bb_r[:, :, None], bb_i[:, :, None]
    kp_r, kp_i = qr * br - qi * bi, qr * bi + qi * br
    lhs = jnp.stack([jnp.concatenate([kp_r[..., d, :], -kp_i[..., d, :]], axis=-1) for d in range(2)])
    rhs = jnp.stack([jnp.concatenate([ct_r[:, :, d], ct_i[:, :, d]], axis=-1) for d in range(2)])
    kern = _contract_last(lhs.reshape(2 * g, c * L, 2 * p), rhs.reshape(2 * g, c, 2 * p)).reshape(2, g, c, L, c)
    skip = d_skip.reshape(g, c, 1, 1) * jnp.eye(c, dtype=F32)[None, :, None, :]
    by_offset = jnp.concatenate([kern[1][:, :, :0:-1], kern[0][:, :, :1] + kern[1][:, :, :1] + skip,
                                 kern[0][:, :, 1:]], axis=2).reshape(g, c, (2 * L - 1) * c)
    mt = jnp.stack([by_offset[:, :, (L - 1 - j) * c:(2 * L - 1 - j) * c] for j in range(L)], axis=1)
    mt = mt.reshape(g, L * c, L * c)
    lr16, li16 = pw_r[:, L], pw_i[:, L]
    fa = jnp.concatenate([lr16[:, 0], lr16[:, 1], lr16[:, 0], lr16[:, 1]], axis=-1)
    fb = jnp.concatenate([-li16[:, 0], -li16[:, 1], li16[:, 0], li16[:, 1]], axis=-1)
    return mt, ws, wot, fa, fb


def _gmm(name, a, b, contract, a_stacked=False, b_stacked=False, o_stacked=False, add=None, out_dtype=F32):
    w = S5_CHUNK * SSM_GROUP_CH
    g = (a.shape[0] if a_stacked else a.shape[1] // w)
    gpb = math.gcd(g, S5_GROUPS_PER_STEP)
    dn = {'nn': (((1,), (0,)), ((), ())), 'nt': (((1,), (1,)), ((), ())), 'tn': (((0,), (0,)), ((), ()))}[contract]

    def spec(arr, stacked):
        if stacked:
            return pl.BlockSpec((gpb,) + arr.shape[1:], lambda i: (i, 0, 0))
        return pl.BlockSpec((arr.shape[0], gpb * w), lambda i: (0, i))

    def take(ref, stacked, e):
        return ref[e] if stacked else ref[:, e * w:(e + 1) * w]

    m = (a.shape[1] if a_stacked else a.shape[0]) if contract != 'tn' else w
    n = w
    if o_stacked:
        o_spec = pl.BlockSpec((gpb, m, n), lambda i: (i, 0, 0))
        o_shape = (g, m, n)
    else:
        o_spec = pl.BlockSpec((m, gpb * n), lambda i: (0, i))
        o_shape = (m, g * n)
    has_add = add is not None

    def body(*refs):
        if has_add:
            a_ref, b_ref, add_ref, o_ref = refs
        else:
            a_ref, b_ref, o_ref = refs
        for e in range(gpb):
            r = lax.dot_general(take(a_ref, a_stacked, e).astype(S5_DTYPE), take(b_ref, b_stacked, e).astype(S5_DTYPE),
                                dn, precision=HI if S5_DTYPE == F32 else None, preferred_element_type=F32)
            if has_add:
                r = r + take(add_ref, o_stacked, e)
            if o_stacked:
                o_ref[e] = r.astype(o_ref.dtype)
            else:
                o_ref[:, e * w:(e + 1) * w] = r.astype(o_ref.dtype)

    in_specs = [spec(a, a_stacked), spec(b, b_stacked)] + ([o_spec] if has_add else [])
    return pl.pallas_call(
        body, name=name, grid=(g // gpb,), in_specs=in_specs, out_specs=o_spec,
        out_shape=jax.ShapeDtypeStruct(o_shape, out_dtype), compiler_params=_cparams(("parallel",)),
    )(*((a, b) + ((add,) if has_add else ())))


def _s5_scan(name, s, fa, fb, rev0, xin=None):
    nk, g, w = s.shape
    hw, qw = w // 2, w // 4
    gb = min(g, 16)
    with_acc = xin is not None

    def body(*refs):
        if with_acc:
            s_ref, a_ref, b_ref, x_ref, o_ref, pa_ref, pb_ref = refs
        else:
            s_ref, a_ref, b_ref, o_ref = refs
        fa_v, fb_v = a_ref[...], b_ref[...]
        dir0 = lax.broadcasted_iota(jnp.int32, (gb, w), 1) % hw < qw
        swap = lambda v: jnp.concatenate([v[:, hw:], v[:, :hw]], axis=1)

        def step(i, carry):
            x, pa, pb = carry
            k0 = (nk - 1 - i) if rev0 else i
            k1 = i if rev0 else (nk - 1 - i)
            for lo in (0, hw):
                o_ref[k0, :, lo:lo + qw] = x[:, lo:lo + qw]
                o_ref[k1, :, lo + qw:lo + hw] = x[:, lo + qw:lo + hw]
            if with_acc:
                xi = jnp.where(dir0, x_ref[k0], x_ref[k1])
                pa = pa + x * xi
                pb = pb + x * swap(xi)
            x = fa_v * x + fb_v * swap(x) + jnp.where(dir0, s_ref[k0], s_ref[k1])
            return x, pa, pb

        z = jnp.zeros((gb, w), F32)
        res = lax.fori_loop(0, nk, step, (z, z, z), unroll=2)
        if with_acc:
            pa_ref[...] = res[1]
            pb_ref[...] = res[2]

    seq = pl.BlockSpec((nk, gb, w), lambda i: (0, i, 0))
    vec = pl.BlockSpec((gb, w), lambda i: (i, 0))
    in_specs = [seq, vec, vec] + ([seq] if with_acc else [])
    out_specs = [seq] + ([vec, vec] if with_acc else [])
    out_shape = [jax.ShapeDtypeStruct((nk, g, w), F32)] + (
        [jax.ShapeDtypeStruct((g, w), F32)] * 2 if with_acc else [])
    return pl.pallas_call(
        body, name=name, grid=(g // gb,), in_specs=in_specs, out_specs=out_specs, out_shape=out_shape,
        compiler_params=_cparams(("parallel",)),
    )(*((s, fa, fb) + ((xin,) if with_acc else ())))


def _regroup(name, x, to_groups):
    if to_groups:
        t, sw = x.shape
    else:
        t, sw = x.shape[0] * S5_CHUNK, x.shape[1] // S5_CHUNK
    nk = t // S5_CHUNK
    wide = LANES * S5_CHUNK

    def place(tok):
        r = lax.broadcasted_iota(jnp.int32, (2 * LANES, wide), 0)
        col = lax.broadcasted_iota(jnp.int32, (2 * LANES, wide), 1)
        ch = r % LANES
        want = (ch // SSM_GROUP_CH) * (S5_CHUNK * SSM_GROUP_CH) + (tok + r // LANES) * SSM_GROUP_CH + ch % SSM_GROUP_CH
        return (col == want).astype(S5_DTYPE)

    def body(x_ref, o_ref):
        token = lambda tok: (pl.ds(tok, nk, stride=S5_CHUNK), slice(None))
        if to_groups:
            acc = jnp.zeros((nk, wide), F32)
            for tok in range(0, S5_CHUNK, 2):
                rows = jnp.concatenate([x_ref[token(tok)], x_ref[token(tok + 1)]], axis=1).astype(S5_DTYPE)
                acc = acc + lax.dot_general(rows, place(tok), NN_DIMS, preferred_element_type=F32)
            o_ref[...] = acc.astype(o_ref.dtype)
        else:
            xv = x_ref[...].astype(S5_DTYPE)
            for tok in range(0, S5_CHUNK, 2):
                both = lax.dot_general(xv, place(tok), NT_DIMS, preferred_element_type=F32).astype(o_ref.dtype)
                o_ref[token(tok)] = both[:, :LANES]
                o_ref[token(tok + 1)] = both[:, LANES:]

    tokens = pl.BlockSpec((t, LANES), lambda i: (0, i))
    groups = pl.BlockSpec((nk, wide), lambda i: (0, i))
    return pl.pallas_call(
        body, name=name, grid=(sw // LANES,), in_specs=[tokens if to_groups else groups],
        out_specs=groups if to_groups else tokens,
        out_shape=jax.ShapeDtypeStruct((nk, sw * S5_CHUNK), S5_DTYPE) if to_groups else jax.ShapeDtypeStruct((t, sw), F32),
        compiler_params=_cparams(("parallel",)),
    )(x)


def _s5_fwd(u2, mats):
    mt, ws, wot, fa, fb = mats
    nk = u2.shape[0]
    g = mt.shape[0]
    y_intra = _gmm("s5_intra", u2, mt, 'nn', b_stacked=True)
    s = _gmm("s5_chunk_state", u2, ws, 'nn', b_stacked=True)
    (xin,) = _s5_scan("s5_scan", s.reshape(nk, g, -1), fa, fb, False)
    xin = xin.reshape(nk, -1)
    return _gmm("s5_inter", xin, wot, 'nt', b_stacked=True, add=y_intra, out_dtype=S5_DTYPE), xin


def _s5_bwd(u2, xin, mats, dy2):
    mt, ws, wot, fa, fb = mats
    nk = u2.shape[0]
    g = mt.shape[0]
    dxin = _gmm("s5_dxin", dy2, wot, 'nn', b_stacked=True)
    ds, pa, pb = _s5_scan("s5_scan_adj", dxin.reshape(nk, g, -1), fa, -fb, True, xin=xin.reshape(nk, g, -1))
    ds = ds.reshape(nk, -1)
    du_a = _gmm("s5_du_intra", dy2, mt, 'nt', b_stacked=True)
    du2 = _gmm("s5_du_state", ds, ws, 'nt', b_stacked=True, add=du_a, out_dtype=S5_DTYPE)
    dmt = _gmm("s5_dmt", u2, dy2, 'tn', o_stacked=True)
    dws = _gmm("s5_dws", u2, ds, 'tn', o_stacked=True)
    dwot = _gmm("s5_dwot", dy2, xin, 'tn', o_stacked=True)
    return du2, (dmt, dws, dwot, pa, pb)


def _stacked(g4):
    return g4.reshape((N_CHIPS, -1, g4.shape[-1]))


def _local_step(x, target, w_in4, late, small, reduce_late=None, reduce_mid=None):
    t, d = x.shape
    aw = w_in4.shape[2]
    sw = aw
    nh = aw // HEAD_DIM
    row = lambda v: v.reshape(1, -1)
    g_mix, g_ffn = row(small['g_mix']), row(small['g_ffn'])
    g_oa, g_os, b_glu = row(small['g_out_attn']), row(small['g_out_ssm']), row(small['b_glu'])
    qg2 = jnp.tile(row(small['q_gain']), (1, 2))
    kg2 = jnp.tile(row(small['k_gain']), (1, 2))

    (h,) = _ew("rms_mix", lambda xv, g: _rms(xv, g)[0], [('r', x), ('c', g_mix)], [('r', d, MXU_DTYPE)])
    bias_t = _bias_table(small['rpb'])
    if late[0] == 'halves':
        under_in, under_attn, under_gate, under_up = late[1][:2], late[1][2:3], late[1][3:4], late[1][4:]
        z4, got_in = _mm("in_proj", h, w_in4, contract='nn', b_mode='b', o_mode='b',
                         comm=_GatherChips(under_in), comm_arrs=under_in)
        ya, got_attn = _attn_fwd(z4, qg2, kg2, bias_t, comm=_GatherChips(under_attn), comm_arrs=under_attn)
        w_glu, w_out = (_stacked(g4).reshape(-1, g4.shape[-1]) for g4 in got_in)
        w_gate4 = _stacked(got_attn[0])
    else:
        z4 = _mm("in_proj", h, w_in4, contract='nn', b_mode='b', o_mode='b')
        ya, _ = _attn_fwd(z4, qg2, kg2, bias_t)
        w_glu, w_out, w_gate4, w_up4, w_down4 = late[1]
    ffs = w_gate4.shape[2]
    s5_params = tuple(small[n] for n in ('ssm_a_re', 'ssm_a_im', 'ssm_b_re', 'ssm_b_im', 'ssm_c_re', 'ssm_c_im',
                                         'ssm_log_step', 'ssm_d'))
    mats, mats_vjp = jax.vjp(_s5_mats, *s5_params)
    mats = tuple(m.astype(S5_DTYPE) for m in mats[:3]) + mats[3:]
    u2 = _regroup("s5_group_u", z4[3], True)
    ypre2, xin = _s5_fwd(u2, mats)
    ypre = _regroup("s5_ungroup_y", ypre2, False)
    (yb,) = _ew("gelu", _gelu, [('r', ypre)], [('r', sw, MXU_DTYPE)])
    a_glu = _mm("glu_proj", yb, w_glu, contract='nn')

    def mix_out(yav, ypv, av, bg, goa, gos):
        ys = _gelu(ypv) * _sigmoid(av + bg)
        return jnp.concatenate([_rms(yav, goa)[0], _rms(ys, gos)[0]], axis=1)
    (ycat,) = _ew("mix_out", mix_out, [('r', ya), ('r', ypre), ('r', a_glu), ('c', b_glu), ('c', g_oa), ('c', g_os)],
                  [('r', aw + sw, MXU_DTYPE)])
    x1 = _mm("out_proj", ycat, w_out, contract='nn', add=x)
    (h2,) = _ew("rms_ffn", lambda xv, g: _rms(xv, g)[0], [('r', x1), ('c', g_ffn)], [('r', d, MXU_DTYPE)])
    if late[0] == 'halves':
        gate4, got_gate = _mm("ffn_gate", h2, w_gate4, contract='nn', b_mode='b', o_mode='b', tn=ffs,
                              out_dtype=MXU_DTYPE, comm=_GatherChips(under_gate), comm_arrs=under_gate)
        w_up4 = _stacked(got_gate[0])
        up4, got_up = _mm("ffn_up", h2, w_up4, contract='nn', b_mode='b', o_mode='b', tn=ffs,
                          out_dtype=MXU_DTYPE, comm=_GatherChips(under_up), comm_arrs=under_up)
        w_down4 = _stacked(got_up[0])
    else:
        gate4 = _mm("ffn_gate", h2, w_gate4, contract='nn', b_mode='b', o_mode='b', tn=ffs, out_dtype=MXU_DTYPE)
        up4 = _mm("ffn_up", h2, w_up4, contract='nn', b_mode='b', o_mode='b', tn=ffs, out_dtype=MXU_DTYPE)
    gate_f, up_f = gate4.reshape(4 * t, ffs), up4.reshape(4 * t, ffs)
    (act,) = _ew("swiglu", lambda gv, uv: gv * _sigmoid(gv) * uv, [('r', gate_f), ('r', up_f)],
                 [('r', ffs, MXU_DTYPE)], tr=1024)
    act4 = act.reshape(4, t, ffs)
    dx2, dx2_b, sq = _ffn_down_loss(act4, w_down4, x1, target)

    dact4 = _mm("ffn_down_dx", dx2_b, w_down4, contract='nt', b_mode='b', o_mode='b', tn=ffs, out_dtype=MXU_DTYPE)
    d_w_down4 = _mm("ffn_down_dw", act4, dx2_b, contract='tn', a_mode='b', o_mode='b', tm=ffs, out_dtype=DW_DTYPE)

    def swiglu_bwd(dav, gv, uv):
        s = _sigmoid(gv)
        return dav * uv * s * (1.0 + gv * (1.0 - s)), dav * gv * s
    dgate, dup = _ew("swiglu_bwd", swiglu_bwd, [('r', dact4.reshape(4 * t, ffs)), ('r', gate_f), ('r', up_f)],
                     [('r', ffs, MXU_DTYPE), ('r', ffs, MXU_DTYPE)], tr=1024)
    dgate4, dup4 = dgate.reshape(4, t, ffs), dup.reshape(4, t, ffs)
    if reduce_late is not None:
        sums_down = reduce_late("down", [d_w_down4])
        dh2, got_down = _mm("ffn_gate_dx", dgate4, w_gate4, contract='nt', a_mode='c', b_mode='c', tk=ffs, tn=2048,
                            comm=_ScatterChips(sums_down), comm_arrs=sums_down)
    else:
        dh2 = _mm("ffn_gate_dx", dgate4, w_gate4, contract='nt', a_mode='c', b_mode='c', tk=ffs, tn=2048)
    dh2 = _mm("ffn_up_dx", dup4, w_up4, contract='nt', a_mode='c', b_mode='c', add=dh2, tk=ffs)
    d_w_gate4 = _mm("ffn_gate_dw", h2, dgate4, contract='tn', b_mode='b', o_mode='b', tn=ffs, out_dtype=DW_DTYPE)
    d_w_up4 = _mm("ffn_up_dw", h2, dup4, contract='tn', b_mode='b', o_mode='b', tn=ffs, out_dtype=DW_DTYPE)

    def rms_res_bwd(xv, g, dyv, resv):
        dx, dg = _rms_bwd(xv, g, dyv)
        return resv + dx, dg
    dx1, d_g_ffn = _ew("rms_ffn_bwd", rms_res_bwd, [('r', x1), ('c', g_ffn), ('r', dh2), ('r', dx2)],
                       [('r', d, F32), ('a', d)])

    dycat = _mm("out_proj_dx", dx1, w_out, contract='nt', out_dtype=MXU_DTYPE)
    d_w_out = _mm("out_proj_dw", ycat, dx1, contract='tn', out_dtype=DW_DTYPE)

    def mix_out_bwd(yav, ypv, av, bg, goa, gos, dca, dcs):
        dya, dgoa = _rms_bwd(yav, goa, dca)
        y = _gelu(ypv)
        s = _sigmoid(av + bg)
        dys, dgos = _rms_bwd(y * s, gos, dcs)
        da = dys * y * s * (1.0 - s)
        return dya, da, dys * s, dgoa, dgos, da
    dya, da, dy_direct, d_g_oa, d_g_os, d_b_glu = _ew(
        "mix_out_bwd", mix_out_bwd,
        [('r', ya), ('r', ypre), ('r', a_glu), ('c', b_glu), ('c', g_oa), ('c', g_os),
         ('r', dycat, 0, aw), ('r', dycat, 1, sw)],
        [('r', aw, F32), ('r', sw, MXU_DTYPE), ('r', sw, F32), ('a', aw), ('a', sw), ('a', sw)])
    dy = _mm("glu_proj_dx", da, w_glu, contract='nt', add=dy_direct)
    d_w_glu = _mm("glu_proj_dw", yb, da, contract='tn', out_dtype=DW_DTYPE)
    (dypre,) = _ew("gelu_bwd", lambda dyv, ypv: dyv * _gelu_grad(ypv), [('r', dy), ('r', ypre)],
                   [('r', sw, F32)])

    du2, dmats = _s5_bwd(u2, xin, mats, _regroup("s5_group_dy", dypre, True))
    d_s5 = mats_vjp(dmats)
    du = _regroup("s5_ungroup_du", du2, False)
    d_late = (d_w_glu, d_w_out, d_w_gate4, d_w_up4, d_w_down4)
    if reduce_late is not None:
        sums = reduce_late("late", d_late[:4])
        (dz4, dbias_t, dqg, dkg), scattered = _attn_bwd(z4, qg2, kg2, bias_t, dya, comm=_ScatterChips(sums),
                                                       comm_arrs=sums)
        d_late = (sums + sums_down, list(scattered) + list(got_down))
    else:
        (dz4, dbias_t, dqg, dkg), _ = _attn_bwd(z4, qg2, kg2, bias_t, dya)
    d_rpb = _bias_table_grad(dbias_t)
    fold = lambda v: v.reshape(-1, 2, HEAD_DIM).sum(axis=(0, 1))
    dz4 = dz4.at[3].set(du.astype(dz4.dtype))

    d_w_in4 = _mm("in_proj_dw", h, dz4, contract='tn', b_mode='b', o_mode='b', out_dtype=DW_DTYPE)
    d_mid = [d_w_in4] + [d_s5[i].reshape(-1, LANES) for i in (2, 3, 4, 5)]
    if reduce_late is not None:
        sums = reduce_mid(d_mid)
        dh, scattered = _mm("in_proj_dx", dz4, w_in4, contract='nt', a_mode='c', b_mode='c',
                            comm=_ScatterChips(sums), comm_arrs=sums)
        d_mid = (sums, list(scattered))
    else:
        dh = _mm("in_proj_dx", dz4, w_in4, contract='nt', a_mode='c', b_mode='c')
    dx, d_g_mix = _ew("rms_mix_bwd", rms_res_bwd, [('r', x), ('c', g_mix), ('r', dh), ('r', dx1)],
                      [('r', d, F32), ('a', d)])

    colsum = lambda v: v.sum(axis=0)
    d_small = {
        'g_mix': colsum(d_g_mix), 'q_gain': fold(dqg), 'k_gain': fold(dkg), 'rpb': d_rpb,
        'ssm_a_re': d_s5[0], 'ssm_a_im': d_s5[1], 'ssm_b_re': d_s5[2], 'ssm_b_im': d_s5[3],
        'ssm_c_re': d_s5[4], 'ssm_c_im': d_s5[5], 'ssm_log_step': d_s5[6], 'ssm_d': d_s5[7],
        'b_glu': colsum(d_b_glu), 'g_out_attn': colsum(d_g_oa), 'g_out_ssm': colsum(d_g_os), 'g_ffn': colsum(d_g_ffn),
    }
    return jnp.sum(sq), dx, d_late, d_mid, d_small


ANY = pl.BlockSpec(memory_space=pl.ANY)


def _place():
    x, y, c = lax.axis_index("x"), lax.axis_index("y"), lax.axis_index("c")
    other_chips = [(1 - x, y), (x, 1 - y), (1 - x, 1 - y)]
    return x, y, c, 2 * x + y, (x, y, 1 - c), other_chips


class _GatherChips:
    KINDS = 7

    def __init__(self, arrs):
        self.n = len(arrs)
        self.out_shape = [jax.ShapeDtypeStruct((N_CHIPS,) + a.shape, a.dtype) for a in arrs]
        self.scratch = [pltpu.SemaphoreType.DMA((self.n, self.KINDS)), pltpu.SemaphoreType.DMA((self.n, self.KINDS))]

    def _copies(self, ins, outs, send_sems, recv_sems):
        x, y, c, me, sibling, chips = _place()

        def remote(a, k, src, dst, to):
            return lambda: pltpu.make_async_remote_copy(src_ref=src, dst_ref=dst, send_sem=send_sems.at[a, k],
                                                        recv_sem=recv_sems.at[a, k], device_id=to, device_id_type=MESH)
        own, out, landed, passed, theirs = [], [], [], [], []
        for a in range(self.n):
            own.append(remote(a, 6, ins[a], outs[a].at[me], sibling))
            for j, (px, py) in enumerate(chips):
                there, here = outs[a].at[2 * px + py, c], outs[a].at[2 * px + py, 1 - c]
                out.append(remote(a, j, ins[a].at[c], outs[a].at[me, c], (px, py, c)))
                landed.append(remote(a, j, there, there, (px, py, c)))
                passed.append(remote(a, 3 + j, there, there, sibling))
                theirs.append(remote(a, 3 + j, here, here, sibling))
        return own, out, landed, passed, theirs

    def start(self, ins, outs, send_sems, recv_sems):
        own, out, _, _, _ = self._copies(ins, outs, send_sems, recv_sems)
        for make in own + out:
            make().start()

    def finish(self, ins, outs, send_sems, recv_sems):
        own, out, landed, passed, theirs = self._copies(ins, outs, send_sems, recv_sems)
        for arrived, onward in zip(landed, passed):
            arrived().wait_recv()
            onward().start()
        for make in theirs + own:
            make().wait_recv()
        for make in own + out + passed:
            make().wait_send()


class _ScatterChips:
    def __init__(self, sums):
        self.n = len(sums)
        self.out_shape = [jax.ShapeDtypeStruct(s.shape, s.dtype) for s in sums]
        self.scratch = [pltpu.SemaphoreType.DMA((self.n, 3)), pltpu.SemaphoreType.DMA((self.n, 3))]

    def _copies(self, ins, outs, send_sems, recv_sems):
        x, y, c, me, sibling, chips = _place()
        out, landed = [], []

        def remote(a, j, src, dst, to):
            return lambda: pltpu.make_async_remote_copy(src_ref=src, dst_ref=dst, send_sem=send_sems.at[a, j],
                                                        recv_sem=recv_sems.at[a, j], device_id=to, device_id_type=MESH)
        for a in range(self.n):
            for j, (px, py) in enumerate(chips):
                slot = outs[a].at[2 * px + py]
                out.append(remote(a, j, ins[a].at[2 * px + py], outs[a].at[me], (px, py, c)))
                landed.append(remote(a, j, slot, slot, (px, py, c)))
        return out, landed

    def start(self, ins, outs, send_sems, recv_sems):
        for make in self._copies(ins, outs, send_sems, recv_sems)[0]:
            make().start()

    def finish(self, ins, outs, send_sems, recv_sems):
        out, landed = self._copies(ins, outs, send_sems, recv_sems)
        for make in landed:
            make().wait_recv()
        for make in out:
            make().wait_send()


def _comm_call(name, comm, arrs):
    n = comm.n

    def body(*refs):
        parts = (refs[:n], refs[n:2 * n]) + tuple(refs[2 * n:])
        comm.start(*parts)
        comm.finish(*parts)

    return pl.pallas_call(body, name=name, in_specs=[ANY] * n, out_specs=[ANY] * n, out_shape=comm.out_shape,
                          scratch_shapes=comm.scratch)(*arrs)


def _gather_chips(name, arrs):
    return _comm_call(name, _GatherChips(arrs), arrs)


def _swap_halves(name, parts):
    n = len(parts)

    def body(*refs):
        ins, outs = refs[:n], refs[n:2 * n]
        send_sems, recv_sems = refs[2 * n:]
        x, y, c, me, sibling, chips = _place()
        cps = []
        for a in range(n):
            cp = pltpu.make_async_remote_copy(src_ref=ins[a].at[:, 1 - c], dst_ref=outs[a], send_sem=send_sems.at[a],
                                              recv_sem=recv_sems.at[a], device_id=sibling, device_id_type=MESH)
            cp.start()
            cps.append(cp)
        for cp in cps:
            cp.wait()

    return pl.pallas_call(
        body, name=name, in_specs=[ANY] * n, out_specs=[ANY] * n,
        out_shape=[jax.ShapeDtypeStruct((N_CHIPS,) + p.shape[2:], p.dtype) for p in parts],
        scratch_shapes=[pltpu.SemaphoreType.DMA((n,)), pltpu.SemaphoreType.DMA((n,))],
    )(*parts)


def _scatter_chips(name, sums):
    return _comm_call(name, _ScatterChips(sums), sums)


def _swap_reduced(name, halves):
    n = len(halves)

    def body(*refs):
        ins, outs = refs[:n], refs[n:2 * n]
        send_sems, recv_sems = refs[2 * n:]
        x, y, c, me, sibling, chips = _place()
        cps = []
        for a in range(n):
            cp = pltpu.make_async_remote_copy(src_ref=ins[a], dst_ref=outs[a], send_sem=send_sems.at[a],
                                              recv_sem=recv_sems.at[a], device_id=sibling, device_id_type=MESH)
            cp.start()
            cps.append(cp)
        for cp in cps:
            cp.wait()

    return pl.pallas_call(
        body, name=name, in_specs=[ANY] * n, out_specs=[ANY] * n,
        out_shape=[jax.ShapeDtypeStruct(h.shape, h.dtype) for h in halves],
        scratch_shapes=[pltpu.SemaphoreType.DMA((n,)), pltpu.SemaphoreType.DMA((n,))],
    )(*halves)


def _row_tile(r, want=256):
    t = (min(r, want) // SUBLANES) * SUBLANES
    while r % t:
        t -= SUBLANES
    return t


def _add_own_half(name, part, got, c, out_dtype):
    _, _, r, cols = part.shape
    tr = _row_tile(r)

    def body(c_ref, p_ref, g_ref, o_ref):
        o_ref[...] = (p_ref[...].astype(F32) + g_ref[...].astype(F32)).astype(o_ref.dtype)

    return pl.pallas_call(
        body, name=name,
        grid_spec=pltpu.PrefetchScalarGridSpec(
            num_scalar_prefetch=1, grid=(N_CHIPS, r // tr),
            in_specs=[pl.BlockSpec((None, None, tr, cols), lambda s, i, c_ref: (s, c_ref[0], i, 0)),
                      pl.BlockSpec((None, tr, cols), lambda s, i, c_ref: (s, i, 0))],
            out_specs=pl.BlockSpec((None, tr, cols), lambda s, i, c_ref: (s, i, 0))),
        out_shape=jax.ShapeDtypeStruct(got.shape, out_dtype),
        compiler_params=_cparams(("parallel", "parallel")),
    )(c.reshape(1).astype(jnp.int32), part, got)


def _sum_chips(name, got, own, me):
    _, r, cols = got.shape
    tr = _row_tile(r)

    def body(me_ref, r0, r1, r2, r3, own_ref, o_ref):
        pick = lambda s, ref: jnp.where(me_ref[0] == s, own_ref[...], ref[...]).astype(F32)
        o_ref[...] = ((pick(0, r0) + pick(1, r1)) + pick(2, r2)) + pick(3, r3)

    def slot(s):
        return pl.BlockSpec((None, tr, cols),
                            lambda i, me_ref: (jnp.where(me_ref[0] == s, (s + 1) % N_CHIPS, s), i, 0))

    return pl.pallas_call(
        body, name=name,
        grid_spec=pltpu.PrefetchScalarGridSpec(
            num_scalar_prefetch=1, grid=(r // tr,),
            in_specs=[slot(s) for s in range(N_CHIPS)]
            + [pl.BlockSpec((None, tr, cols), lambda i, me_ref: (me_ref[0], i, 0))],
            out_specs=pl.BlockSpec((tr, cols), lambda i, me_ref: (i, 0))),
        out_shape=jax.ShapeDtypeStruct((r, cols), F32),
        compiler_params=_cparams(("parallel",)),
    )(me.reshape(1).astype(jnp.int32), got, got, got, got, own)


def _adamw_math(wv, gv, mv, vv):
    mv = ADAM_B1 * mv + (1.0 - ADAM_B1) * gv
    vv = ADAM_B2 * vv + (1.0 - ADAM_B2) * (gv * gv)
    m_hat = mv / (1.0 - ADAM_B1 ** ADAM_STEP)
    v_hat = vv / (1.0 - ADAM_B2 ** ADAM_STEP)
    return -ADAM_LR * (m_hat / (jnp.sqrt(v_hat) + ADAM_EPS) + ADAM_WD * wv), mv, vv


def _adamw(name, w, g, m, v):
    cols = w.shape[1]
    return _ew(name, _adamw_math, [('r', w), ('r', g), ('r', m), ('r', v)], [('r', cols, F32)] * 3,
               tr=_row_tile(w.shape[0], max(LANES, LANES * LANES // cols)))


def _adamw_halves(name, w, mine, theirs, m, v, c):
    r, cols = mine.shape
    tr = _row_tile(r, 256)
    nb = r // tr

    def body(c_ref, w_ref, a_ref, b_ref, m_ref, v_ref, g_out, d_out, m_out, v_out):
        g = jnp.where(pl.program_id(0) == c_ref[0], a_ref[...], b_ref[...])
        g_out[...] = g
        d_out[...], m_out[...], v_out[...] = _adamw_math(w_ref[...], g, m_ref[...], v_ref[...])

    whole = pl.BlockSpec((tr, cols), lambda h, i, c_ref: (h * nb + i, 0))
    half = pl.BlockSpec((tr, cols), lambda h, i, c_ref: (i, 0))
    return pl.pallas_call(
        body, name=name,
        grid_spec=pltpu.PrefetchScalarGridSpec(
            num_scalar_prefetch=1, grid=(2, nb),
            in_specs=[whole, half, half, whole, whole], out_specs=[whole] * 4),
        out_shape=[jax.ShapeDtypeStruct(w.shape, F32)] * 4,
        compiler_params=_cparams(("parallel", "parallel")),
    )(c.reshape(1).astype(jnp.int32), w, mine, theirs, m, v)


SMALL_ROWS_ALIGN = 2 * N_CHIPS * SUBLANES


MEDIUM_NAMES = ['ssm_b_re', 'ssm_b_im', 'ssm_c_re', 'ssm_c_im']
PACKED_NAMES = [n for n in SMALL_NAMES if n not in MEDIUM_NAMES]


def _pack_small(d):
    flat = jnp.concatenate([d[n].reshape(-1).astype(F32) for n in PACKED_NAMES])
    rows = -(-flat.shape[0] // (LANES * SMALL_ROWS_ALIGN)) * SMALL_ROWS_ALIGN
    return jnp.pad(flat, (0, rows * LANES - flat.shape[0])).reshape(rows, LANES)


def _unpack_small(packed, like):
    flat = packed.reshape(-1)
    out, off = {}, 0
    for n in PACKED_NAMES:
        size = like[n].size
        out[n] = flat[off:off + size].reshape(like[n].shape)
        off += size
    return out


def kernel(x, g_mix, w_in, q_gain, k_gain, rpb, ssm_a_re, ssm_a_im, ssm_b_re, ssm_b_im, ssm_c_re, ssm_c_im, ssm_log_step, ssm_d, w_glu, b_glu, g_out_attn, g_out_ssm, w_out, g_ffn, w_ffn_gate, w_ffn_up, w_ffn_down, loss_target, m_g_mix, m_w_in, m_q_gain, m_k_gain, m_rpb, m_ssm_a_re, m_ssm_a_im, m_ssm_b_re, m_ssm_b_im, m_ssm_c_re, m_ssm_c_im, m_ssm_log_step, m_ssm_d, m_w_glu, m_b_glu, m_g_out_attn, m_g_out_ssm, m_w_out, m_g_ffn, m_w_ffn_gate, m_w_ffn_up, m_w_ffn_down, v_g_mix, v_w_in, v_q_gain, v_k_gain, v_rpb, v_ssm_a_re, v_ssm_a_im, v_ssm_b_re, v_ssm_b_im, v_ssm_c_re, v_ssm_c_im, v_ssm_log_step, v_ssm_d, v_w_glu, v_b_glu, v_g_out_attn, v_g_out_ssm, v_w_out, v_g_ffn, v_w_ffn_gate, v_w_ffn_up, v_w_ffn_down):
    given = dict(locals())
    w = {n: given[n][0] for n in WEIGHT_NAMES}
    mom = {n: given["m_" + n][0] for n in WEIGHT_NAMES}
    var = {n: given["v_" + n][0] for n in WEIGHT_NAMES}
    d = x.shape[-1]
    c = lax.axis_index("c")

    halves = {n: w[n].astype(MXU_DTYPE).reshape((2, w[n].shape[0] // 2, w[n].shape[1])) for n in BIG_NAMES}
    (w_in4,) = _gather_chips("gather_w_in", [halves['w_in']])
    w_in4 = w_in4.reshape((N_CHIPS, -1, w_in4.shape[-1]))

    def chip_sums(tag, grads, payload):
        parts = [g.reshape((N_CHIPS, 2, -1, g.shape[-1])) for g in grads]
        got = _swap_halves("reduce_swap_halves_" + tag, parts)
        return [_add_own_half("reduce_add_%s_%d" % (tag, a), p, gt, c, dt)
                for a, (p, gt, dt) in enumerate(zip(parts, got, payload))]

    reduce_late = lambda tag, grads: chip_sums(tag, grads, [GRAD_PAYLOAD_DTYPE] * len(grads))
    reduce_mid = lambda grads: chip_sums("mid", grads, [GRAD_PAYLOAD_DTYPE] + [F32] * (len(grads) - 1))
    sq, dx, (sums_late, got_late), (sums_mid, got_mid), d_small = _local_step(
        x[0], loss_target[0], w_in4, ('halves', [halves[n] for n in LATE_NAMES]), {n: w[n] for n in SMALL_NAMES},
        reduce_late, reduce_mid)
    loss = lax.psum(0.5 * sq / d, ("x", "y", "c"))

    nbig = len(BIG_NAMES)
    sums_tiny = chip_sums("tiny", [_pack_small(d_small)], [F32])
    got_tiny = list(_scatter_chips("reduce_scatter_tiny", sums_tiny))
    sums = sums_mid[:1] + sums_late + sums_mid[1:] + sums_tiny
    got = got_mid[:1] + got_late + got_mid[1:] + got_tiny
    me = 2 * lax.axis_index("x") + lax.axis_index("y")
    mine = [_sum_chips("reduce_sum_%d" % a, gt, sm_, me) for a, (gt, sm_) in enumerate(zip(got, sums))]
    theirs = _swap_reduced("reduce_swap_reduced", mine)
    in_order = lambda a: jnp.where(c == 0, jnp.stack([mine[a], theirs[a]]), jnp.stack([theirs[a], mine[a]]))
    repl = _gather_chips("gather_small", [in_order(a) for a in range(nbig, len(mine))])
    repl = [r.reshape(-1, LANES) for r in repl]
    like = {n: w[n] for n in SMALL_NAMES}
    grad_small = _unpack_small(repl[-1], like)
    grad_small.update({n: r.reshape(w[n].shape) for n, r in zip(MEDIUM_NAMES, repl)})

    grad_big, delta, new_m, new_v = {}, {}, {}, {}
    for a, n in enumerate(BIG_NAMES):
        grad_big[n], delta[n], new_m[n], new_v[n] = _adamw_halves("adamw_%d" % a, w[n], mine[a], theirs[a],
                                                                  mom[n], var[n], c)
    for n in MEDIUM_NAMES:
        flat = lambda t: t.reshape(-1, w[n].shape[-1])
        res = _adamw("adamw_" + n, flat(w[n]), flat(grad_small[n]), flat(mom[n]), flat(var[n]))
        delta[n], new_m[n], new_v[n] = (t.reshape(w[n].shape) for t in res)
    sd, sm, sv = _adamw("adamw_small", _pack_small(w), repl[-1], _pack_small(mom), _pack_small(var))
    delta.update(_unpack_small(sd, like))
    new_m.update(_unpack_small(sm, like))
    new_v.update(_unpack_small(sv, like))
    grads = {**grad_big, **grad_small}
    lead = lambda t: t[None]
    return (loss, dx[None], *[lead(grads[n]) for n in WEIGHT_NAMES], *[lead(delta[n]) for n in WEIGHT_NAMES],
            *[lead(new_m[n]) for n in WEIGHT_NAMES], *[lead(new_v[n]) for n in WEIGHT_NAMES])
```

```python
import functools
import math

import jax
import jax.numpy as jnp
from jax import lax
from jax.experimental import pallas as pl
from jax.experimental.pallas import tpu as pltpu

F32 = jnp.float32
BF16 = jnp.bfloat16
MXU_DTYPE = BF16
GRAD_PAYLOAD_DTYPE = BF16
DW_DTYPE = BF16
S5_DTYPE = BF16
HI = lax.Precision.HIGHEST
VMEM_LIMIT_V7X = 56 * 1024 * 1024
LANES = 128
SUBLANES = 8

GRID_W = 64
WIN_H = 8
WIN_W = 16
HEAD_DIM = 64
SSM_GROUP_CH = 16
SSM_STATE = 64
S5_CHUNK = 16
S5_GROUPS_PER_STEP = 16
RMS_EPS = 1e-6
NEG_INF = -1e30
N_CHIPS = 4
MESH = pl.DeviceIdType.MESH

ADAM_LR = 0.001
ADAM_B1 = 0.9
ADAM_B2 = 0.999
ADAM_EPS = 1e-08
ADAM_WD = 0.01
ADAM_STEP = 10

WEIGHT_NAMES = ['g_mix', 'w_in', 'q_gain', 'k_gain', 'rpb', 'ssm_a_re', 'ssm_a_im', 'ssm_b_re', 'ssm_b_im',
                'ssm_c_re', 'ssm_c_im', 'ssm_log_step', 'ssm_d', 'w_glu', 'b_glu', 'g_out_attn', 'g_out_ssm',
                'w_out', 'g_ffn', 'w_ffn_gate', 'w_ffn_up', 'w_ffn_down']
BIG_NAMES = ['w_in', 'w_glu', 'w_out', 'w_ffn_gate', 'w_ffn_up', 'w_ffn_down']
LATE_NAMES = BIG_NAMES[1:]
SMALL_NAMES = [n for n in WEIGHT_NAMES if n not in BIG_NAMES]


def _cparams(sem):
    return pltpu.CompilerParams(dimension_semantics=sem, vmem_limit_bytes=VMEM_LIMIT_V7X)


def _tile(n, want):
    if n <= want:
        return n
    t = (want // LANES) * LANES
    while t >= LANES:
        if n % t == 0:
            return t
        t -= LANES
    return n


def _mm(name, a, b, *, contract, a_mode='2', b_mode='2', o_mode='2', out_dtype=F32, add=None, exact=False,
        tm=1024, tn=1024, tk=2048, comm=None, comm_arrs=(), post=None):
    dn = {'nn': (((1,), (0,)), ((), ())), 'nt': (((1,), (1,)), ((), ())), 'tn': (((0,), (0,)), ((), ()))}[contract]
    ar, ac = a.shape[-2:]
    br, bc = b.shape[-2:]
    m, kdim = (ar, ac) if contract != 'tn' else (ac, ar)
    n = bc if contract != 'nt' else br
    assert kdim == (br if contract != 'nt' else bc), (name, a.shape, b.shape)
    nbatch = 1
    for arr, mode in ((a, a_mode), (b, b_mode)):
        if mode == 'b':
            nbatch = arr.shape[0]
    nstack = 1
    for arr, mode in ((a, a_mode), (b, b_mode)):
        if mode == 'c':
            nstack = arr.shape[0]
    tm, tn, tk = _tile(m, tm), _tile(n, tn), _tile(kdim, tk)
    nkin = kdim // tk
    nk = nstack * nkin
    grid = (nbatch, m // tm, n // tn, nk)

    def spec(mode, block, rc):
        def imap(s, i, j, kk):
            r, c = rc(i, j, kk % nkin)
            if mode == '2':
                return (r, c)
            return (s if mode == 'b' else kk // nkin, r, c)
        return pl.BlockSpec(block if mode == '2' else (None,) + block, imap)

    a_spec = spec(a_mode, (tm, tk) if contract != 'tn' else (tk, tm),
                  (lambda i, j, k: (i, k)) if contract != 'tn' else (lambda i, j, k: (k, i)))
    b_spec = spec(b_mode, (tk, tn) if contract != 'nt' else (tn, tk),
                  (lambda i, j, k: (k, j)) if contract != 'nt' else (lambda i, j, k: (j, k)))
    o_spec = spec(o_mode, (tm, tn), lambda i, j, k: (i, j))
    out_shape = (m, n) if o_mode == '2' else (nbatch, m, n)
    has_add = add is not None

    def product(a_ref, b_ref):
        if exact:
            return lax.dot_general(a_ref[...].astype(F32), b_ref[...].astype(F32), dn, precision=HI,
                                   preferred_element_type=F32)
        return lax.dot_general(a_ref[...].astype(MXU_DTYPE), b_ref[...].astype(MXU_DTYPE), dn,
                               preferred_element_type=F32)

    ncomm = len(comm_arrs)
    nacc = int(nk > 1)
    post_fn, post_ins, post_dtypes = post if post is not None else (None, (), ())
    nx, nxo = len(post_ins), len(post_dtypes)
    base = 2 + has_add + nx

    def body(*refs):
        a_ref, b_ref = refs[:2]
        add_ref = refs[2] if has_add else None
        x_ins = refs[2 + has_add:base]
        c_ins = refs[base:base + ncomm]
        o_ref = refs[base + ncomm]
        x_outs = refs[base + ncomm + 1:base + ncomm + 1 + nxo]
        c_outs = refs[base + ncomm + 1 + nxo:base + 2 * ncomm + 1 + nxo]
        sems = refs[base + 2 * ncomm + 1 + nxo + nacc:]
        ids = [pl.program_id(ax) for ax in range(4)]
        if comm is not None:
            @pl.when((ids[0] == 0) & (ids[1] == 0) & (ids[2] == 0) & (ids[3] == 0))
            def _():
                comm.start(c_ins, c_outs, *sems)

        def write(r):
            if has_add:
                r = r + add_ref[...].astype(F32)
            if post_fn is not None:
                vals = post_fn(r, *[x[...].astype(F32) for x in x_ins])
                r = vals[0]
                for ref, v in zip(x_outs, vals[1:]):
                    ref[...] = v.astype(ref.dtype)
            o_ref[...] = r.astype(o_ref.dtype)

        if nk == 1:
            write(product(a_ref, b_ref))
        else:
            acc_ref = refs[base + 2 * ncomm + 1 + nxo]

            @pl.when(ids[3] == 0)
            def _():
                acc_ref[...] = jnp.zeros_like(acc_ref)

            acc_ref[...] += product(a_ref, b_ref)

            @pl.when(ids[3] == nk - 1)
            def _():
                write(acc_ref[...])

        if comm is not None:
            @pl.when((ids[0] == grid[0] - 1) & (ids[1] == grid[1] - 1) & (ids[2] == grid[2] - 1) & (ids[3] == nk - 1))
            def _():
                comm.finish(c_ins, c_outs, *sems)

    in_specs = [a_spec, b_spec] + ([o_spec] if has_add else []) + [o_spec] * nx + [ANY] * ncomm
    args = (a, b) + ((add,) if has_add else ()) + tuple(post_ins) + tuple(comm_arrs)
    res = pl.pallas_call(
        body, name=name, grid=grid, in_specs=in_specs, out_specs=[o_spec] * (1 + nxo) + [ANY] * ncomm,
        out_shape=[jax.ShapeDtypeStruct(out_shape, dt) for dt in (out_dtype,) + tuple(post_dtypes)]
        + (comm.out_shape if comm is not None else []),
        scratch_shapes=([pltpu.VMEM((tm, tn), F32)] if nk > 1 else []) + (comm.scratch if comm is not None else []),
        compiler_params=_cparams(("parallel", "parallel", "parallel", "arbitrary") if comm is None
                                 else ("arbitrary",) * 4),
    )(*args)
    outs = res[0] if nxo == 0 else tuple(res[:1 + nxo])
    return outs if comm is None else (outs, res[1 + nxo:])


def _ew(name, fn, ins, outs, tr=512):
    rows = next(x[1].shape[0] for x in ins if x[0] == 'r')
    tr = min(tr, rows)
    assert rows % tr == 0 and tr % SUBLANES == 0, (name, rows, tr)
    in_specs, args = [], []
    for x in ins:
        if x[0] == 'r' and len(x) == 2:
            in_specs.append(pl.BlockSpec((tr, x[1].shape[1]), lambda i: (i, 0)))
        elif x[0] == 'r':
            in_specs.append(pl.BlockSpec((tr, x[3]), functools.partial(lambda cb, i: (i, cb), x[2])))
        else:
            in_specs.append(pl.BlockSpec(x[1].shape, lambda i: (0, 0)))
        args.append(x[1])
    out_specs, out_shapes = [], []
    for o in outs:
        if o[0] == 'r':
            out_specs.append(pl.BlockSpec((tr, o[1]), lambda i: (i, 0)))
            out_shapes.append(jax.ShapeDtypeStruct((rows, o[1]), o[2]))
        else:
            out_specs.append(pl.BlockSpec((SUBLANES, o[1]), lambda i: (0, 0)))
            out_shapes.append(jax.ShapeDtypeStruct((SUBLANES, o[1]), F32))
    nin = len(ins)
    has_acc = any(o[0] == 'a' for o in outs)

    def body(*refs):
        vals = fn(*[r[...].astype(F32) for r in refs[:nin]])
        if not isinstance(vals, (tuple, list)):
            vals = (vals,)
        i = pl.program_id(0)
        for o, ref, v in zip(outs, refs[nin:], vals):
            if o[0] == 'r':
                ref[...] = v.astype(ref.dtype)
            else:
                part = v.astype(F32).reshape(tr // SUBLANES, SUBLANES, o[1]).sum(axis=0)

                @pl.when(i == 0)
                def _(ref=ref, part=part):
                    ref[...] = part

                @pl.when(i > 0)
                def _(ref=ref, part=part):
                    ref[...] += part

    res = pl.pallas_call(
        body, name=name, grid=(rows // tr,), in_specs=in_specs, out_specs=out_specs, out_shape=out_shapes,
        compiler_params=_cparams(("arbitrary",) if has_acc else ("parallel",)),
    )(*args)
    return res


def _ffn_down_loss(act4, w_down4, x1, target, tm=1024, tn=1024):
    ns, t, ffs = act4.shape
    d = w_down4.shape[2]
    tm, tn = _tile(t, tm), _tile(d, tn)

    def body(a_ref, b_ref, x1_ref, t_ref, g_ref, gb_ref, sq_ref, acc_ref):
        k = pl.program_id(2)

        @pl.when(k == 0)
        def _():
            acc_ref[...] = jnp.zeros_like(acc_ref)

        acc_ref[...] += lax.dot_general(a_ref[...].astype(MXU_DTYPE), b_ref[...].astype(MXU_DTYPE), NN_DIMS,
                                        preferred_element_type=F32)

        @pl.when(k == ns - 1)
        def _():
            diff = (acc_ref[...] + x1_ref[...]) - t_ref[...]
            g = diff * (1.0 / d)
            g_ref[...] = g
            gb_ref[...] = g.astype(gb_ref.dtype)
            sq_ref[...] = (diff * diff).reshape(tm // SUBLANES, SUBLANES, tn).sum(axis=0)

    tile = pl.BlockSpec((tm, tn), lambda i, j, k: (i, j))
    return pl.pallas_call(
        body, name="ffn_down_loss", grid=(t // tm, d // tn, ns),
        in_specs=[pl.BlockSpec((None, tm, ffs), lambda i, j, k: (k, i, 0)),
                  pl.BlockSpec((None, ffs, tn), lambda i, j, k: (k, 0, j)), tile, tile],
        out_specs=[tile, tile, pl.BlockSpec((SUBLANES, tn), lambda i, j, k: (i, j))],
        out_shape=[jax.ShapeDtypeStruct((t, d), F32), jax.ShapeDtypeStruct((t, d), MXU_DTYPE),
                   jax.ShapeDtypeStruct((t // tm * SUBLANES, d), F32)],
        scratch_shapes=[pltpu.VMEM((tm, tn), F32)],
        compiler_params=_cparams(("parallel", "parallel", "arbitrary")),
    )(act4, w_down4, x1, target)


def _rms(x, g):
    r = lax.rsqrt(jnp.mean(x * x, axis=-1, keepdims=True) + RMS_EPS)
    xr = x * r
    return xr * g, xr


def _rms_bwd(x, g, dy):
    r = lax.rsqrt(jnp.mean(x * x, axis=-1, keepdims=True) + RMS_EPS)
    xr = x * r
    gdy = g * dy
    dx = r * (gdy - xr * jnp.mean(xr * gdy, axis=-1, keepdims=True))
    return dx, dy * xr


def _sigmoid(x):
    return 0.5 * (jnp.tanh(0.5 * x) + 1.0)


_GELU_C = math.sqrt(2.0 / math.pi)


def _gelu(x):
    return 0.5 * x * (1.0 + jnp.tanh(_GELU_C * (x + 0.044715 * x * x * x)))


def _gelu_grad(x):
    t = jnp.tanh(_GELU_C * (x + 0.044715 * x * x * x))
    return 0.5 * (1.0 + t) + 0.5 * x * (1.0 - t * t) * _GELU_C * (1.0 + 3 * 0.044715 * x * x)


ATTN_ROWS_PER_STEP = 16
ATTN_BWD_ROWS_TOGETHER = 8
NT_DIMS = (((1,), (1,)), ((), ()))
NN_DIMS = (((1,), (0,)), ((), ()))
TN_DIMS = (((0,), (0,)), ((), ()))


def _attn_geometry(r, rows):
    row_start = jnp.clip(r - WIN_H // 2, 0, rows - WIN_H)
    key0 = pl.multiple_of(row_start * GRID_W, GRID_W)
    bias0 = pl.multiple_of((row_start - r + (WIN_H - 1)) * GRID_W, GRID_W)
    return key0, bias0


def _window_onehot():
    c = jnp.arange(GRID_W)
    col_start = jnp.clip(c - WIN_W // 2, 0, GRID_W - WIN_W)
    col_in = (c[None, :] >= col_start[:, None]) & (c[None, :] < col_start[:, None] + WIN_W)
    dc = jnp.clip(c[None, :] - c[:, None], -(WIN_W - 1), WIN_W - 1) + (WIN_W - 1)
    onehot = ((dc[:, :, None] == jnp.arange(2 * WIN_W - 1)[None, None, :]) & col_in[:, :, None]).astype(F32)
    return onehot, col_in


def _bias_table(rpb):
    onehot, col_in = _window_onehot()
    nh = rpb.shape[0]
    pairs = rpb.reshape(nh // 2, 2, 2 * WIN_H - 1, 2 * WIN_W - 1)
    mask = jnp.where(col_in, 0.0, NEG_INF).T
    heads = [jnp.einsum('prd,qkd->prkq', pairs[:, e], onehot, precision=HI) + mask for e in range(2)]
    return jnp.concatenate(heads, axis=-1).reshape(nh // 2, (2 * WIN_H - 1) * GRID_W, 2 * GRID_W)


def _bias_table_grad(dtab):
    onehot, _ = _window_onehot()
    npair = dtab.shape[0]
    d = dtab.reshape(npair, 2 * WIN_H - 1, GRID_W, 2 * GRID_W)
    heads = [jnp.einsum('prkq,qkd->prd', d[..., e * GRID_W:(e + 1) * GRID_W], onehot, precision=HI) for e in range(2)]
    return jnp.stack(heads, axis=1).reshape(2 * npair, 2 * WIN_H - 1, 2 * WIN_W - 1)


def _lane_lo(shape):
    return lax.broadcasted_iota(jnp.int32, shape, 1) < HEAD_DIM


def _half_sums(v):
    lo = _lane_lo(v.shape)
    s_lo = jnp.sum(jnp.where(lo, v, 0.0), axis=1, keepdims=True)
    s_hi = jnp.sum(jnp.where(lo, 0.0, v), axis=1, keepdims=True)
    return jnp.where(lo, s_lo, s_hi)


def _rms_pair(x, g):
    r = lax.rsqrt(_half_sums(x * x) * (1.0 / HEAD_DIM) + RMS_EPS)
    return x * r * g


def _rms_pair_bwd(x, g, dy):
    r = lax.rsqrt(_half_sums(x * x) * (1.0 / HEAD_DIM) + RMS_EPS)
    xr = x * r
    gdy = g * dy
    dx = r * (gdy - xr * (_half_sums(xr * gdy) * (1.0 / HEAD_DIM)))
    return dx, dy * xr


def _blockdiag(a):
    a2 = jnp.concatenate([a, a], axis=0)
    row_hi = lax.broadcasted_iota(jnp.int32, a2.shape, 0) >= GRID_W
    lane_hi = lax.broadcasted_iota(jnp.int32, a2.shape, 1) >= HEAD_DIM
    return jnp.where(row_hi == lane_hi, a2, 0.0).astype(MXU_DTYPE)


def _diag_blocks(m):
    return jnp.where(_lane_lo((GRID_W, 2 * HEAD_DIM)), m[:GRID_W], m[GRID_W:])


def _attn_scores(qb, kb, bias):
    st = lax.dot_general(kb, qb, NT_DIMS, preferred_element_type=F32)
    st = st * (1.0 / math.sqrt(HEAD_DIM)) + bias
    mx = jnp.max(st, axis=0, keepdims=True)
    p = jnp.exp(st - mx)
    return p * (1.0 / jnp.sum(p, axis=0, keepdims=True))


def _attn_fwd(z4, qg2, kg2, bias_t, comm=None, comm_arrs=()):
    _, t, aw = z4.shape
    rows = t // GRID_W
    npair = aw // (2 * HEAD_DIM)
    nkeys = WIN_H * GRID_W
    nb = bias_t.shape[1]
    rps = min(ATTN_ROWS_PER_STEP, rows)
    blk = rps * GRID_W
    nsteps = rows // rps
    ncomm = len(comm_arrs)

    def body(*refs):
        q_ref, k_ref, v_ref, qg_ref, kg_ref, b_ref = refs[:6]
        c_ins, o_ref, c_outs = refs[6:6 + ncomm], refs[6 + ncomm], refs[7 + ncomm:7 + 2 * ncomm]
        kn_ref, vb_ref = refs[7 + 2 * ncomm:9 + 2 * ncomm]
        sems = refs[9 + 2 * ncomm:]
        pr, rb = pl.program_id(0), pl.program_id(1)
        if comm is not None:
            @pl.when((pr == 0) & (rb == 0))
            def _():
                comm.start(c_ins, c_outs, *sems)

        @pl.when(rb == 0)
        def _():
            kn_ref[...] = _rms_pair(k_ref[...], kg_ref[...]).astype(MXU_DTYPE)
            vb_ref[...] = v_ref[...].astype(MXU_DTYPE)

        def row(i, carry):
            key0, bias0 = _attn_geometry(rb * rps + i, rows)
            at = pl.ds(pl.multiple_of(i * GRID_W, GRID_W), GRID_W)
            qb = _blockdiag(_rms_pair(q_ref[at, :], qg_ref[...]))
            pt = _attn_scores(qb, kn_ref[pl.ds(key0, nkeys), :], b_ref[pl.ds(bias0, nkeys), :])
            both = lax.dot_general(pt.astype(MXU_DTYPE), vb_ref[pl.ds(key0, nkeys), :], TN_DIMS,
                                   preferred_element_type=F32)
            o_ref[at, :] = _diag_blocks(both)
            return carry

        lax.fori_loop(0, rps, row, 0, unroll=8)
        if comm is not None:
            @pl.when((pr == npair - 1) & (rb == nsteps - 1))
            def _():
                comm.finish(c_ins, c_outs, *sems)

    pair_cols = lambda lead: pl.BlockSpec((None, t, 2 * HEAD_DIM), lambda p, r: (lead, 0, p))
    res = pl.pallas_call(
        body, name="attn_fwd", grid=(npair, nsteps),
        in_specs=[pl.BlockSpec((None, blk, 2 * HEAD_DIM), lambda p, r: (0, r, p)), pair_cols(1), pair_cols(2),
                  pl.BlockSpec((1, 2 * HEAD_DIM), lambda p, r: (0, 0)),
                  pl.BlockSpec((1, 2 * HEAD_DIM), lambda p, r: (0, 0)),
                  pl.BlockSpec((None, nb, 2 * GRID_W), lambda p, r: (p, 0, 0))] + [ANY] * ncomm,
        out_specs=[pl.BlockSpec((blk, 2 * HEAD_DIM), lambda p, r: (r, p))] + [ANY] * ncomm,
        out_shape=[jax.ShapeDtypeStruct((t, aw), F32)] + (comm.out_shape if comm is not None else []),
        scratch_shapes=[pltpu.VMEM((t, 2 * HEAD_DIM), MXU_DTYPE), pltpu.VMEM((t, 2 * HEAD_DIM), MXU_DTYPE)]
        + (comm.scratch if comm is not None else []),
        compiler_params=_cparams(("arbitrary", "arbitrary")),
    )(z4, z4, z4, qg2, kg2, bias_t, *comm_arrs)
    return res[0], res[1:]


def _attn_bwd(z4, qg2, kg2, bias_t, dya, comm=None, comm_arrs=()):
    _, t, aw = z4.shape
    rows = t // GRID_W
    npair = aw // (2 * HEAD_DIM)
    nkeys = WIN_H * GRID_W
    nb = bias_t.shape[1]
    rps = min(ATTN_ROWS_PER_STEP, rows)
    blk = rps * GRID_W
    nsteps = rows // rps
    scale = 1.0 / math.sqrt(HEAD_DIM)
    ncomm = len(comm_arrs)

    def body(*refs):
        q_ref, k_ref, v_ref, qg_ref, kg_ref, b_ref, do_ref = refs[:7]
        c_ins = refs[7:7 + ncomm]
        dz_ref, db_ref, dqg_ref, dkg_ref = refs[7 + ncomm:11 + ncomm]
        c_outs = refs[11 + ncomm:11 + 2 * ncomm]
        kn_ref, vb_ref, dkn_ref, dv_ref = refs[11 + 2 * ncomm:15 + 2 * ncomm]
        sems = refs[15 + 2 * ncomm:]
        pr, rb = pl.program_id(0), pl.program_id(1)
        if comm is not None:
            @pl.when((pr == 0) & (rb == 0))
            def _():
                comm.start(c_ins, c_outs, *sems)

        @pl.when(rb == 0)
        def _():
            kn_ref[...] = _rms_pair(k_ref[...], kg_ref[...]).astype(MXU_DTYPE)
            vb_ref[...] = v_ref[...].astype(MXU_DTYPE)
            dkn_ref[...] = jnp.zeros_like(dkn_ref)
            dv_ref[...] = jnp.zeros_like(dv_ref)
            db_ref[...] = jnp.zeros_like(db_ref)
            dqg_ref[...] = jnp.zeros_like(dqg_ref)

        def row(i, dqg_sum):
            r = rb * rps + i
            key0, bias0 = _attn_geometry(r, rows)
            keys = pl.ds(key0, nkeys)
            at = pl.ds(pl.multiple_of(i * GRID_W, GRID_W), GRID_W)
            q = q_ref[at, :]
            qb = _blockdiag(_rms_pair(q, qg_ref[...]))
            dob = _blockdiag(do_ref[at, :])
            kb = kn_ref[keys, :]
            pt = _attn_scores(qb, kb, b_ref[pl.ds(bias0, nkeys), :])
            dv_ref[keys, :] += lax.dot_general(pt.astype(MXU_DTYPE), dob, NN_DIMS, preferred_element_type=F32)
            dpt = lax.dot_general(vb_ref[keys, :], dob, NT_DIMS, preferred_element_type=F32)
            dst = pt * (dpt - jnp.sum(pt * dpt, axis=0, keepdims=True))
            db_ref[pl.ds(bias0, nkeys), :] += dst
            dsb = dst.astype(MXU_DTYPE)
            dkn_ref[keys, :] += scale * lax.dot_general(dsb, qb, NN_DIMS, preferred_element_type=F32)
            dqn = scale * _diag_blocks(lax.dot_general(dsb, kb, TN_DIMS, preferred_element_type=F32))
            dq, dqg = _rms_pair_bwd(q, qg_ref[...], dqn)
            dz_ref[0, pl.ds(pl.multiple_of(r * GRID_W, GRID_W), GRID_W), :] = dq.astype(dz_ref.dtype)
            return dqg_sum + jnp.sum(dqg, axis=0, keepdims=True)

        def rows_together(i, acc):
            for j in range(ATTN_BWD_ROWS_TOGETHER):
                acc = row(ATTN_BWD_ROWS_TOGETHER * i + j, acc)
            return acc

        dqg_ref[...] += lax.fori_loop(0, rps // ATTN_BWD_ROWS_TOGETHER, rows_together,
                                      jnp.zeros((1, 2 * HEAD_DIM), F32))

        @pl.when(rb == nsteps - 1)
        def _():
            dk, dkg = _rms_pair_bwd(k_ref[...], kg_ref[...], dkn_ref[...])
            dz_ref[1] = dk.astype(dz_ref.dtype)
            dz_ref[2] = dv_ref[...].astype(dz_ref.dtype)
            dkg_ref[...] = jnp.sum(dkg, axis=0, keepdims=True)

        if comm is not None:
            @pl.when((pr == npair - 1) & (rb == nsteps - 1))
            def _():
                comm.finish(c_ins, c_outs, *sems)

    pair_cols = lambda lead: pl.BlockSpec((None, t, 2 * HEAD_DIM), lambda p, r: (lead, 0, p))
    pair_vec = pl.BlockSpec((None, 1, 2 * HEAD_DIM), lambda p, r: (p, 0, 0))
    res = pl.pallas_call(
        body, name="attn_bwd", grid=(npair, nsteps),
        in_specs=[pl.BlockSpec((None, blk, 2 * HEAD_DIM), lambda p, r: (0, r, p)), pair_cols(1), pair_cols(2),
                  pl.BlockSpec((1, 2 * HEAD_DIM), lambda p, r: (0, 0)),
                  pl.BlockSpec((1, 2 * HEAD_DIM), lambda p, r: (0, 0)),
                  pl.BlockSpec((None, nb, 2 * GRID_W), lambda p, r: (p, 0, 0)),
                  pl.BlockSpec((blk, 2 * HEAD_DIM), lambda p, r: (r, p))] + [ANY] * ncomm,
        out_specs=[pl.BlockSpec((3, t, 2 * HEAD_DIM), lambda p, r: (0, 0, p)),
                   pl.BlockSpec((None, nb, 2 * GRID_W), lambda p, r: (p, 0, 0)),
                   pair_vec, pair_vec] + [ANY] * ncomm,
        out_shape=[jax.ShapeDtypeStruct((4, t, aw), MXU_DTYPE), jax.ShapeDtypeStruct(bias_t.shape, F32),
                   jax.ShapeDtypeStruct((npair, 1, 2 * HEAD_DIM), F32),
                   jax.ShapeDtypeStruct((npair, 1, 2 * HEAD_DIM), F32)] + (comm.out_shape if comm is not None else []),
        scratch_shapes=[pltpu.VMEM((t, 2 * HEAD_DIM), MXU_DTYPE), pltpu.VMEM((t, 2 * HEAD_DIM), MXU_DTYPE),
                        pltpu.VMEM((t, 2 * HEAD_DIM), F32), pltpu.VMEM((t, 2 * HEAD_DIM), F32)]
        + (comm.scratch if comm is not None else []),
        compiler_params=_cparams(("arbitrary", "arbitrary")),
    )(z4, z4, z4, qg2, kg2, bias_t, dya, *comm_arrs)
    return res[:4], res[4:]


def _bmm_exact(name, a, b, dims, per_step=16):
    nb = a.shape[0]
    per = math.gcd(nb, per_step)
    shape = jax.eval_shape(lambda u, v: lax.dot_general(u, v, dims), a[0], b[0]).shape

    def body(a_ref, b_ref, o_ref):
        for e in range(per):
            o_ref[e] = lax.dot_general(a_ref[e], b_ref[e], dims, precision=HI, preferred_element_type=F32)

    blk = lambda arr: pl.BlockSpec((per,) + arr.shape[1:], lambda i: (i, 0, 0))
    out = jax.ShapeDtypeStruct((nb,) + shape, F32)
    return pl.pallas_call(body, name=name, grid=(nb // per,), in_specs=[blk(a), blk(b)], out_specs=blk(out),
                          out_shape=out, compiler_params=_cparams(("parallel",)))(a, b)


@jax.custom_vjp
def _contract_last(a, b):
    return _bmm_exact("s5_kern", a, b, NT_DIMS)


def _contract_last_fwd(a, b):
    return _contract_last(a, b), (a, b)


def _contract_last_bwd(res, g):
    a, b = res
    return _bmm_exact("s5_kern_da", g, b, NN_DIMS), _bmm_exact("s5_kern_db", g, a, TN_DIMS)


_contract_last.defvjp(_contract_last_fwd, _contract_last_bwd)


def _s5_mats(a_re, a_im, b_re, b_im, c_re, c_im, log_step, d_skip):
    nd, g, p = a_re.shape
    c = b_re.shape[-1]
    L = S5_CHUNK
    lr = jnp.minimum(a_re, -1e-4).transpose(1, 0, 2)
    li = a_im.transpose(1, 0, 2)
    dt = jnp.exp(log_step).T[..., None]
    n = jnp.arange(L + 1, dtype=F32)[None, :, None, None]
    mag = jnp.exp(n * (lr * dt)[:, None])
    ang = n * (li * dt)[:, None]
    pw_r, pw_i = mag * jnp.cos(ang), mag * jnp.sin(ang)
    den = lr * lr + li * li
    nr, ni = pw_r[:, 1] - 1.0, pw_i[:, 1]
    cr, ci = (nr * lr + ni * li) / den, (ni * lr - nr * li) / den
    bt_r, bt_i = b_re.transpose(1, 3, 0, 2), b_im.transpose(1, 3, 0, 2)
    bb_r = cr[:, None] * bt_r - ci[:, None] * bt_i
    bb_i = cr[:, None] * bt_i + ci[:, None] * bt_r
    ct_r, ct_i = c_re.transpose(1, 2, 0, 3), c_im.transpose(1, 2, 0, 3)

    def cols(x_re, x_im):
        return jnp.concatenate([x_re[..., 0, :], x_re[..., 1, :], x_im[..., 0, :], x_im[..., 1, :]], axis=-1)

    e_r = jnp.stack([pw_r[:, :L, 0][:, ::-1], pw_r[:, :L, 1]], axis=2)
    e_i = jnp.stack([pw_i[:, :L, 0][:, ::-1], pw_i[:, :L, 1]], axis=2)
    ws = (cols(e_r, e_r)[:, :, None] * cols(bb_r, bb_i)[:, None]
          + cols(e_i, e_i)[:, :, None] * cols(-bb_i, bb_r)[:, None]).reshape(g, L * c, 4 * p)
    f_r = jnp.stack([pw_r[:, 1:, 0], pw_r[:, 1:, 1][:, ::-1]], axis=2)
    f_i = jnp.stack([pw_i[:, 1:, 0], pw_i[:, 1:, 1][:, ::-1]], axis=2)
    wot = (cols(f_r, f_i)[:, :, None] * cols(ct_r, -ct_r)[:, None]
           + cols(f_i, f_r)[:, :, None] * cols(-ct_i, -ct_i)[:, None]).reshape(g, L * c, 4 * p)
    qr, qi = pw_r[:, None, :L], pw_i[:, None, :L]
    br, bi = bb_r[:, :, None], bb_i[:, :, None]
    kp_r, kp_i = qr * br - qi * bi, qr * bi + qi * br
    lhs = jnp.stack([jnp.concatenate([kp_r[..., d, :], -kp_i[..., d, :]], axis=-1) for d in range(2)])
    rhs = jnp.stack([jnp.concatenate([ct_r[:, :, d], ct_i[:, :, d]], axis=-1) for d in range(2)])
    kern = _contract_last(lhs.reshape(2 * g, c * L, 2 * p), rhs.reshape(2 * g, c, 2 * p)).reshape(2, g, c, L, c)
    skip = d_skip.reshape(g, c, 1, 1) * jnp.eye(c, dtype=F32)[None, :, None, :]
    by_offset = jnp.concatenate([kern[1][:, :, :0:-1], kern[0][:, :, :1] + kern[1][:, :, :1] + skip,
                                 kern[0][:, :, 1:]], axis=2).reshape(g, c, (2 * L - 1) * c)
    mt = jnp.stack([by_offset[:, :, (L - 1 - j) * c:(2 * L - 1 - j) * c] for j in range(L)], axis=1)
    mt = mt.reshape(g, L * c, L * c)
    lr16, li16 = pw_r[:, L], pw_i[:, L]
    fa = jnp.concatenate([lr16[:, 0], lr16[:, 1], lr16[:, 0], lr16[:, 1]], axis=-1)
    fb = jnp.concatenate([-li16[:, 0], -li16[:, 1], li16[:, 0], li16[:, 1]], axis=-1)
    return mt, ws, wot, fa, fb


def _gmm(name, a, b, contract, a_stacked=False, b_stacked=False, o_stacked=False, add=None, out_dtype=F32):
    w = S5_CHUNK * SSM_GROUP_CH
    g = (a.shape[0] if a_stacked else a.shape[1] // w)
    gpb = math.gcd(g, S5_GROUPS_PER_STEP)
    dn = {'nn': (((1,), (0,)), ((), ())), 'nt': (((1,), (1,)), ((), ())), 'tn': (((0,), (0,)), ((), ()))}[contract]

    def spec(arr, stacked):
        if stacked:
            return pl.BlockSpec((gpb,) + arr.shape[1:], lambda i: (i, 0, 0))
        return pl.BlockSpec((arr.shape[0], gpb * w), lambda i: (0, i))

    def take(ref, stacked, e):
        return ref[e] if stacked else ref[:, e * w:(e + 1) * w]

    m = (a.shape[1] if a_stacked else a.shape[0]) if contract != 'tn' else w
    n = w
    if o_stacked:
        o_spec = pl.BlockSpec((gpb, m, n), lambda i: (i, 0, 0))
        o_shape = (g, m, n)
    else:
        o_spec = pl.BlockSpec((m, gpb * n), lambda i: (0, i))
        o_shape = (m, g * n)
    has_add = add is not None

    def body(*refs):
        if has_add:
            a_ref, b_ref, add_ref, o_ref = refs
        else:
            a_ref, b_ref, o_ref = refs
        for e in range(gpb):
            r = lax.dot_general(take(a_ref, a_stacked, e).astype(S5_DTYPE), take(b_ref, b_stacked, e).astype(S5_DTYPE),
                                dn, precision=HI if S5_DTYPE == F32 else None, preferred_element_type=F32)
            if has_add:
                r = r + take(add_ref, o_stacked, e)
            if o_stacked:
                o_ref[e] = r.astype(o_ref.dtype)
            else:
                o_ref[:, e * w:(e + 1) * w] = r.astype(o_ref.dtype)

    in_specs = [spec(a, a_stacked), spec(b, b_stacked)] + ([o_spec] if has_add else [])
    return pl.pallas_call(
        body, name=name, grid=(g // gpb,), in_specs=in_specs, out_specs=o_spec,
        out_shape=jax.ShapeDtypeStruct(o_shape, out_dtype), compiler_params=_cparams(("parallel",)),
    )(*((a, b) + ((add,) if has_add else ())))


def _s5_scan(name, s, fa, fb, rev0, xin=None):
    nk, g, w = s.shape
    hw, qw = w // 2, w // 4
    gb = min(g, 16)
    with_acc = xin is not None

    def body(*refs):
        if with_acc:
            s_ref, a_ref, b_ref, x_ref, o_ref, pa_ref, pb_ref = refs
        else:
            s_ref, a_ref, b_ref, o_ref = refs
        fa_v, fb_v = a_ref[...], b_ref[...]
        dir0 = lax.broadcasted_iota(jnp.int32, (gb, w), 1) % hw < qw
        swap = lambda v: jnp.concatenate([v[:, hw:], v[:, :hw]], axis=1)

        def step(i, carry):
            x, pa, pb = carry
            k0 = (nk - 1 - i) if rev0 else i
            k1 = i if rev0 else (nk - 1 - i)
            for lo in (0, hw):
                o_ref[k0, :, lo:lo + qw] = x[:, lo:lo + qw]
                o_ref[k1, :, lo + qw:lo + hw] = x[:, lo + qw:lo + hw]
            if with_acc:
                xi = jnp.where(dir0, x_ref[k0], x_ref[k1])
                pa = pa + x * xi
                pb = pb + x * swap(xi)
            x = fa_v * x + fb_v * swap(x) + jnp.where(dir0, s_ref[k0], s_ref[k1])
            return x, pa, pb

        z = jnp.zeros((gb, w), F32)
        res = lax.fori_loop(0, nk, step, (z, z, z), unroll=2)
        if with_acc:
            pa_ref[...] = res[1]
            pb_ref[...] = res[2]

    seq = pl.BlockSpec((nk, gb, w), lambda i: (0, i, 0))
    vec = pl.BlockSpec((gb, w), lambda i: (i, 0))
    in_specs = [seq, vec, vec] + ([seq] if with_acc else [])
    out_specs = [seq] + ([vec, vec] if with_acc else [])
    out_shape = [jax.ShapeDtypeStruct((nk, g, w), F32)] + (
        [jax.ShapeDtypeStruct((g, w), F32)] * 2 if with_acc else [])
    return pl.pallas_call(
        body, name=name, grid=(g // gb,), in_specs=in_specs, out_specs=out_specs, out_shape=out_shape,
        compiler_params=_cparams(("parallel",)),
    )(*((s, fa, fb) + ((xin,) if with_acc else ())))


def _regroup(name, x, to_groups):
    if to_groups:
        t, sw = x.shape
    else:
        t, sw = x.shape[0] * S5_CHUNK, x.shape[1] // S5_CHUNK
    nk = t // S5_CHUNK
    wide = LANES * S5_CHUNK

    def place(tok):
        r = lax.broadcasted_iota(jnp.int32, (2 * LANES, wide), 0)
        col = lax.broadcasted_iota(jnp.int32, (2 * LANES, wide), 1)
        ch = r % LANES
        want = (ch // SSM_GROUP_CH) * (S5_CHUNK * SSM_GROUP_CH) + (tok + r // LANES) * SSM_GROUP_CH + ch % SSM_GROUP_CH
        return (col == want).astype(S5_DTYPE)

    def body(x_ref, o_ref):
        token = lambda tok: (pl.ds(tok, nk, stride=S5_CHUNK), slice(None))
        if to_groups:
            acc = jnp.zeros((nk, wide), F32)
            for tok in range(0, S5_CHUNK, 2):
                rows = jnp.concatenate([x_ref[token(tok)], x_ref[token(tok + 1)]], axis=1).astype(S5_DTYPE)
                acc = acc + lax.dot_general(rows, place(tok), NN_DIMS, preferred_element_type=F32)
            o_ref[...] = acc.astype(o_ref.dtype)
        else:
            xv = x_ref[...].astype(S5_DTYPE)
            for tok in range(0, S5_CHUNK, 2):
                both = lax.dot_general(xv, place(tok), NT_DIMS, preferred_element_type=F32).astype(o_ref.dtype)
                o_ref[token(tok)] = both[:, :LANES]
                o_ref[token(tok + 1)] = both[:, LANES:]

    tokens = pl.BlockSpec((t, LANES), lambda i: (0, i))
    groups = pl.BlockSpec((nk, wide), lambda i: (0, i))
    return pl.pallas_call(
        body, name=name, grid=(sw // LANES,), in_specs=[tokens if to_groups else groups],
        out_specs=groups if to_groups else tokens,
        out_shape=jax.ShapeDtypeStruct((nk, sw * S5_CHUNK), S5_DTYPE) if to_groups else jax.ShapeDtypeStruct((t, sw), F32),
        compiler_params=_cparams(("parallel",)),
    )(x)


def _s5_fwd(u2, mats):
    mt, ws, wot, fa, fb = mats
    nk = u2.shape[0]
    g = mt.shape[0]
    y_intra = _gmm("s5_intra", u2, mt, 'nn', b_stacked=True)
    s = _gmm("s5_chunk_state", u2, ws, 'nn', b_stacked=True)
    (xin,) = _s5_scan("s5_scan", s.reshape(nk, g, -1), fa, fb, False)
    xin = xin.reshape(nk, -1)
    return _gmm("s5_inter", xin, wot, 'nt', b_stacked=True, add=y_intra, out_dtype=S5_DTYPE), xin


def _s5_bwd(u2, xin, mats, dy2):
    mt, ws, wot, fa, fb = mats
    nk = u2.shape[0]
    g = mt.shape[0]
    dxin = _gmm("s5_dxin", dy2, wot, 'nn', b_stacked=True)
    ds, pa, pb = _s5_scan("s5_scan_adj", dxin.reshape(nk, g, -1), fa, -fb, True, xin=xin.reshape(nk, g, -1))
    ds = ds.reshape(nk, -1)
    du_a = _gmm("s5_du_intra", dy2, mt, 'nt', b_stacked=True)
    du2 = _gmm("s5_du_state", ds, ws, 'nt', b_stacked=True, add=du_a, out_dtype=S5_DTYPE)
    dmt = _gmm("s5_dmt", u2, dy2, 'tn', o_stacked=True)
    dws = _gmm("s5_dws", u2, ds, 'tn', o_stacked=True)
    dwot = _gmm("s5_dwot", dy2, xin, 'tn', o_stacked=True)
    return du2, (dmt, dws, dwot, pa, pb)


def _stacked(g4):
    return g4.reshape((N_CHIPS, -1, g4.shape[-1]))


def _local_step(x, target, w_in4, late, small, reduce_late=None, reduce_mid=None):
    t, d = x.shape
    aw = w_in4.shape[2]
    sw = aw
    nh = aw // HEAD_DIM
    row = lambda v: v.reshape(1, -1)
    g_mix, g_ffn = row(small['g_mix']), row(small['g_ffn'])
    g_oa, g_os, b_glu = row(small['g_out_attn']), row(small['g_out_ssm']), row(small['b_glu'])
    qg2 = jnp.tile(row(small['q_gain']), (1, 2))
    kg2 = jnp.tile(row(small['k_gain']), (1, 2))

    (h,) = _ew("rms_mix", lambda xv, g: _rms(xv, g)[0], [('r', x), ('c', g_mix)], [('r', d, MXU_DTYPE)])
    bias_t = _bias_table(small['rpb'])
    if late[0] == 'halves':
        under_in, under_attn, under_gate, under_up = late[1][:2], late[1][2:3], late[1][3:4], late[1][4:]
        z4, got_in = _mm("in_proj", h, w_in4, contract='nn', b_mode='b', o_mode='b',
                         comm=_GatherChips(under_in), comm_arrs=under_in)
        ya, got_attn = _attn_fwd(z4, qg2, kg2, bias_t, comm=_GatherChips(under_attn), comm_arrs=under_attn)
        w_glu, w_out = (_stacked(g4).reshape(-1, g4.shape[-1]) for g4 in got_in)
        w_gate4 = _stacked(got_attn[0])
    else:
        z4 = _mm("in_proj", h, w_in4, contract='nn', b_mode='b', o_mode='b')
        ya, _ = _attn_fwd(z4, qg2, kg2, bias_t)
        w_glu, w_out, w_gate4, w_up4, w_down4 = late[1]
    ffs = w_gate4.shape[2]
    s5_params = tuple(small[n] for n in ('ssm_a_re', 'ssm_a_im', 'ssm_b_re', 'ssm_b_im', 'ssm_c_re', 'ssm_c_im',
                                         'ssm_log_step', 'ssm_d'))
    mats, mats_vjp = jax.vjp(_s5_mats, *s5_params)
    mats = tuple(m.astype(S5_DTYPE) for m in mats[:3]) + mats[3:]
    u2 = _regroup("s5_group_u", z4[3], True)
    ypre2, xin = _s5_fwd(u2, mats)
    ypre = _regroup("s5_ungroup_y", ypre2, False)
    (yb,) = _ew("gelu", _gelu, [('r', ypre)], [('r', sw, MXU_DTYPE)])
    a_glu = _mm("glu_proj", yb, w_glu, contract='nn')

    def mix_out(yav, ypv, av, bg, goa, gos):
        ys = _gelu(ypv) * _sigmoid(av + bg)
        return jnp.concatenate([_rms(yav, goa)[0], _rms(ys, gos)[0]], axis=1)
    (ycat,) = _ew("mix_out", mix_out, [('r', ya), ('r', ypre), ('r', a_glu), ('c', b_glu), ('c', g_oa), ('c', g_os)],
                  [('r', aw + sw, MXU_DTYPE)])
    x1 = _mm("out_proj", ycat, w_out, contract='nn', add=x)
    (h2,) = _ew("rms_ffn", lambda xv, g: _rms(xv, g)[0], [('r', x1), ('c', g_ffn)], [('r', d, MXU_DTYPE)])
    swiglu = lambda uv, gv: (uv, gv * _sigmoid(gv) * uv)
    if late[0] == 'halves':
        gate4, got_gate = _mm("ffn_gate", h2, w_gate4, contract='nn', b_mode='b', o_mode='b', tn=ffs,
                              out_dtype=MXU_DTYPE, comm=_GatherChips(under_gate), comm_arrs=under_gate)
        w_up4 = _stacked(got_gate[0])
        (up4, act4), got_up = _mm("ffn_up", h2, w_up4, contract='nn', b_mode='b', o_mode='b', tn=ffs,
                                  out_dtype=MXU_DTYPE, comm=_GatherChips(under_up), comm_arrs=under_up,
                                  post=(swiglu, [gate4], [MXU_DTYPE]))
        w_down4 = _stacked(got_up[0])
    else:
        gate4 = _mm("ffn_gate", h2, w_gate4, contract='nn', b_mode='b', o_mode='b', tn=ffs, out_dtype=MXU_DTYPE)
        up4, act4 = _mm("ffn_up", h2, w_up4, contract='nn', b_mode='b', o_mode='b', tn=ffs, out_dtype=MXU_DTYPE,
                        post=(swiglu, [gate4], [MXU_DTYPE]))
    gate_f, up_f = gate4.reshape(4 * t, ffs), up4.reshape(4 * t, ffs)
    dx2, dx2_b, sq = _ffn_down_loss(act4, w_down4, x1, target)

    dact4 = _mm("ffn_down_dx", dx2_b, w_down4, contract='nt', b_mode='b', o_mode='b', tn=ffs, out_dtype=MXU_DTYPE)
    d_w_down4 = _mm("ffn_down_dw", act4, dx2_b, contract='tn', a_mode='b', o_mode='b', tm=ffs, out_dtype=DW_DTYPE)

    def swiglu_bwd(dav, gv, uv):
        s = _sigmoid(gv)
        return dav * uv * s * (1.0 + gv * (1.0 - s)), dav * gv * s
    dgate, dup = _ew("swiglu_bwd", swiglu_bwd, [('r', dact4.reshape(4 * t, ffs)), ('r', gate_f), ('r', up_f)],
                     [('r', ffs, MXU_DTYPE), ('r', ffs, MXU_DTYPE)], tr=1024)
    dgate4, dup4 = dgate.reshape(4, t, ffs), dup.reshape(4, t, ffs)
    if reduce_late is not None:
        sums_down = reduce_late("down", [d_w_down4])
        dh2, got_down = _mm("ffn_gate_dx", dgate4, w_gate4, contract='nt', a_mode='c', b_mode='c', tk=ffs, tn=2048,
                            comm=_ScatterChips(sums_down), comm_arrs=sums_down)
    else:
        dh2 = _mm("ffn_gate_dx", dgate4, w_gate4, contract='nt', a_mode='c', b_mode='c', tk=ffs, tn=2048)
    dh2 = _mm("ffn_up_dx", dup4, w_up4, contract='nt', a_mode='c', b_mode='c', add=dh2, tk=ffs)
    d_w_gate4 = _mm("ffn_gate_dw", h2, dgate4, contract='tn', b_mode='b', o_mode='b', tn=ffs, out_dtype=DW_DTYPE)
    d_w_up4 = _mm("ffn_up_dw", h2, dup4, contract='tn', b_mode='b', o_mode='b', tn=ffs, out_dtype=DW_DTYPE)

    def rms_res_bwd(xv, g, dyv, resv):
        dx, dg = _rms_bwd(xv, g, dyv)
        return resv + dx, dg
    dx1, d_g_ffn = _ew("rms_ffn_bwd", rms_res_bwd, [('r', x1), ('c', g_ffn), ('r', dh2), ('r', dx2)],
                       [('r', d, F32), ('a', d)])

    dycat = _mm("out_proj_dx", dx1, w_out, contract='nt', out_dtype=MXU_DTYPE)
    d_w_out = _mm("out_proj_dw", ycat, dx1, contract='tn', out_dtype=DW_DTYPE)

    def mix_out_bwd(yav, ypv, av, bg, goa, gos, dca, dcs):
        dya, dgoa = _rms_bwd(yav, goa, dca)
        y = _gelu(ypv)
        s = _sigmoid(av + bg)
        dys, dgos = _rms_bwd(y * s, gos, dcs)
        da = dys * y * s * (1.0 - s)
        return dya, da, dys * s, dgoa, dgos, da
    dya, da, dy_direct, d_g_oa, d_g_os, d_b_glu = _ew(
        "mix_out_bwd", mix_out_bwd,
        [('r', ya), ('r', ypre), ('r', a_glu), ('c', b_glu), ('c', g_oa), ('c', g_os),
         ('r', dycat, 0, aw), ('r', dycat, 1, sw)],
        [('r', aw, F32), ('r', sw, MXU_DTYPE), ('r', sw, F32), ('a', aw), ('a', sw), ('a', sw)])
    dy = _mm("glu_proj_dx", da, w_glu, contract='nt', add=dy_direct)
    d_w_glu = _mm("glu_proj_dw", yb, da, contract='tn', out_dtype=DW_DTYPE)
    (dypre,) = _ew("gelu_bwd", lambda dyv, ypv: dyv * _gelu_grad(ypv), [('r', dy), ('r', ypre)],
                   [('r', sw, F32)])

    du2, dmats = _s5_bwd(u2, xin, mats, _regroup("s5_group_dy", dypre, True))
    d_s5 = mats_vjp(dmats)
    du = _regroup("s5_ungroup_du", du2, False)
    d_late = (d_w_glu, d_w_out, d_w_gate4, d_w_up4, d_w_down4)
    if reduce_late is not None:
        sums = reduce_late("late", d_late[:4])
        (dz4, dbias_t, dqg, dkg), scattered = _attn_bwd(z4, qg2, kg2, bias_t, dya, comm=_ScatterChips(sums),
                                                       comm_arrs=sums)
        d_late = (sums + sums_down, list(scattered) + list(got_down))
    else:
        (dz4, dbias_t, dqg, dkg), _ = _attn_bwd(z4, qg2, kg2, bias_t, dya)
    d_rpb = _bias_table_grad(dbias_t)
    fold = lambda v: v.reshape(-1, 2, HEAD_DIM).sum(axis=(0, 1))
    dz4 = dz4.at[3].set(du.astype(dz4.dtype))

    d_w_in4 = _mm("in_proj_dw", h, dz4, contract='tn', b_mode='b', o_mode='b', out_dtype=DW_DTYPE)
    d_mid = [d_w_in4] + [d_s5[i].reshape(-1, LANES) for i in (2, 3, 4, 5)]
    if reduce_late is not None:
        sums = reduce_mid(d_mid)
        dh, scattered = _mm("in_proj_dx", dz4, w_in4, contract='nt', a_mode='c', b_mode='c',
                            comm=_ScatterChips(sums), comm_arrs=sums)
        d_mid = (sums, list(scattered))
    else:
        dh = _mm("in_proj_dx", dz4, w_in4, contract='nt', a_mode='c', b_mode='c')
    dx, d_g_mix = _ew("rms_mix_bwd", rms_res_bwd, [('r', x), ('c', g_mix), ('r', dh), ('r', dx1)],
                      [('r', d, F32), ('a', d)])

    colsum = lambda v: v.sum(axis=0)
    d_small = {
        'g_mix': colsum(d_g_mix), 'q_gain': fold(dqg), 'k_gain': fold(dkg), 'rpb': d_rpb,
        'ssm_a_re': d_s5[0], 'ssm_a_im': d_s5[1], 'ssm_b_re': d_s5[2], 'ssm_b_im': d_s5[3],
        'ssm_c_re': d_s5[4], 'ssm_c_im': d_s5[5], 'ssm_log_step': d_s5[6], 'ssm_d': d_s5[7],
        'b_glu': colsum(d_b_glu), 'g_out_attn': colsum(d_g_oa), 'g_out_ssm': colsum(d_g_os), 'g_ffn': colsum(d_g_ffn),
    }
    return jnp.sum(sq), dx, d_late, d_mid, d_small


ANY = pl.BlockSpec(memory_space=pl.ANY)


def _place():
    x, y, c = lax.axis_index("x"), lax.axis_index("y"), lax.axis_index("c")
    other_chips = [(1 - x, y), (x, 1 - y), (1 - x, 1 - y)]
    return x, y, c, 2 * x + y, (x, y, 1 - c), other_chips


class _GatherChips:
    KINDS = 7

    def __init__(self, arrs):
        self.n = len(arrs)
        self.out_shape = [jax.ShapeDtypeStruct((N_CHIPS,) + a.shape, a.dtype) for a in arrs]
        self.scratch = [pltpu.SemaphoreType.DMA((self.n, self.KINDS)), pltpu.SemaphoreType.DMA((self.n, self.KINDS))]

    def _copies(self, ins, outs, send_sems, recv_sems):
        x, y, c, me, sibling, chips = _place()

        def remote(a, k, src, dst, to):
            return lambda: pltpu.make_async_remote_copy(src_ref=src, dst_ref=dst, send_sem=send_sems.at[a, k],
                                                        recv_sem=recv_sems.at[a, k], device_id=to, device_id_type=MESH)
        own, out, landed, passed, theirs = [], [], [], [], []
        for a in range(self.n):
            own.append(remote(a, 6, ins[a], outs[a].at[me], sibling))
            for j, (px, py) in enumerate(chips):
                there, here = outs[a].at[2 * px + py, c], outs[a].at[2 * px + py, 1 - c]
                out.append(remote(a, j, ins[a].at[c], outs[a].at[me, c], (px, py, c)))
                landed.append(remote(a, j, there, there, (px, py, c)))
                passed.append(remote(a, 3 + j, there, there, sibling))
                theirs.append(remote(a, 3 + j, here, here, sibling))
        return own, out, landed, passed, theirs

    def start(self, ins, outs, send_sems, recv_sems):
        own, out, _, _, _ = self._copies(ins, outs, send_sems, recv_sems)
        for make in own + out:
            make().start()

    def finish(self, ins, outs, send_sems, recv_sems):
        own, out, landed, passed, theirs = self._copies(ins, outs, send_sems, recv_sems)
        for arrived, onward in zip(landed, passed):
            arrived().wait_recv()
            onward().start()
        for make in theirs + own:
            make().wait_recv()
        for make in own + out + passed:
            make().wait_send()


class _ScatterChips:
    def __init__(self, sums):
        self.n = len(sums)
        self.out_shape = [jax.ShapeDtypeStruct(s.shape, s.dtype) for s in sums]
        self.scratch = [pltpu.SemaphoreType.DMA((self.n, 3)), pltpu.SemaphoreType.DMA((self.n, 3))]

    def _copies(self, ins, outs, send_sems, recv_sems):
        x, y, c, me, sibling, chips = _place()
        out, landed = [], []

        def remote(a, j, src, dst, to):
            return lambda: pltpu.make_async_remote_copy(src_ref=src, dst_ref=dst, send_sem=send_sems.at[a, j],
                                                        recv_sem=recv_sems.at[a, j], device_id=to, device_id_type=MESH)
        for a in range(self.n):
            for j, (px, py) in enumerate(chips):
                slot = outs[a].at[2 * px + py]
                out.append(remote(a, j, ins[a].at[2 * px + py], outs[a].at[me], (px, py, c)))
                landed.append(remote(a, j, slot, slot, (px, py, c)))
        return out, landed

    def start(self, ins, outs, send_sems, recv_sems):
        for make in self._copies(ins, outs, send_sems, recv_sems)[0]:
            make().start()

    def finish(self, ins, outs, send_sems, recv_sems):
        out, landed = self._copies(ins, outs, send_sems, recv_sems)
        for make in landed:
            make().wait_recv()
        for make in out:
            make().wait_send()


def _comm_call(name, comm, arrs):
    n = comm.n

    def body(*refs):
        parts = (refs[:n], refs[n:2 * n]) + tuple(refs[2 * n:])
        comm.start(*parts)
        comm.finish(*parts)

    return pl.pallas_call(body, name=name, in_specs=[ANY] * n, out_specs=[ANY] * n, out_shape=comm.out_shape,
                          scratch_shapes=comm.scratch)(*arrs)


def _gather_chips(name, arrs):
    return _comm_call(name, _GatherChips(arrs), arrs)


def _swap_halves(name, parts):
    n = len(parts)

    def body(*refs):
        ins, outs = refs[:n], refs[n:2 * n]
        send_sems, recv_sems = refs[2 * n:]
        x, y, c, me, sibling, chips = _place()
        cps = []
        for a in range(n):
            cp = pltpu.make_async_remote_copy(src_ref=ins[a].at[:, 1 - c], dst_ref=outs[a], send_sem=send_sems.at[a],
                                              recv_sem=recv_sems.at[a], device_id=sibling, device_id_type=MESH)
            cp.start()
            cps.append(cp)
        for cp in cps:
            cp.wait()

    return pl.pallas_call(
        body, name=name, in_specs=[ANY] * n, out_specs=[ANY] * n,
        out_shape=[jax.ShapeDtypeStruct((N_CHIPS,) + p.shape[2:], p.dtype) for p in parts],
        scratch_shapes=[pltpu.SemaphoreType.DMA((n,)), pltpu.SemaphoreType.DMA((n,))],
    )(*parts)


def _scatter_chips(name, sums):
    return _comm_call(name, _ScatterChips(sums), sums)


def _swap_reduced(name, halves):
    n = len(halves)

    def body(*refs):
        ins, outs = refs[:n], refs[n:2 * n]
        send_sems, recv_sems = refs[2 * n:]
        x, y, c, me, sibling, chips = _place()
        cps = []
        for a in range(n):
            cp = pltpu.make_async_remote_copy(src_ref=ins[a], dst_ref=outs[a], send_sem=send_sems.at[a],
                                              recv_sem=recv_sems.at[a], device_id=sibling, device_id_type=MESH)
            cp.start()
            cps.append(cp)
        for cp in cps:
            cp.wait()

    return pl.pallas_call(
        body, name=name, in_specs=[ANY] * n, out_specs=[ANY] * n,
        out_shape=[jax.ShapeDtypeStruct(h.shape, h.dtype) for h in halves],
        scratch_shapes=[pltpu.SemaphoreType.DMA((n,)), pltpu.SemaphoreType.DMA((n,))],
    )(*halves)


def _row_tile(r, want=256):
    t = (min(r, want) // SUBLANES) * SUBLANES
    while r % t:
        t -= SUBLANES
    return t


def _add_own_half(name, part, got, c, out_dtype):
    _, _, r, cols = part.shape
    tr = _row_tile(r)

    def body(c_ref, p_ref, g_ref, o_ref):
        o_ref[...] = (p_ref[...].astype(F32) + g_ref[...].astype(F32)).astype(o_ref.dtype)

    return pl.pallas_call(
        body, name=name,
        grid_spec=pltpu.PrefetchScalarGridSpec(
            num_scalar_prefetch=1, grid=(N_CHIPS, r // tr),
            in_specs=[pl.BlockSpec((None, None, tr, cols), lambda s, i, c_ref: (s, c_ref[0], i, 0)),
                      pl.BlockSpec((None, tr, cols), lambda s, i, c_ref: (s, i, 0))],
            out_specs=pl.BlockSpec((None, tr, cols), lambda s, i, c_ref: (s, i, 0))),
        out_shape=jax.ShapeDtypeStruct(got.shape, out_dtype),
        compiler_params=_cparams(("parallel", "parallel")),
    )(c.reshape(1).astype(jnp.int32), part, got)


def _sum_chips(name, got, own, me):
    _, r, cols = got.shape
    tr = _row_tile(r)

    def body(me_ref, r0, r1, r2, r3, own_ref, o_ref):
        pick = lambda s, ref: jnp.where(me_ref[0] == s, own_ref[...], ref[...]).astype(F32)
        o_ref[...] = ((pick(0, r0) + pick(1, r1)) + pick(2, r2)) + pick(3, r3)

    def slot(s):
        return pl.BlockSpec((None, tr, cols),
                            lambda i, me_ref: (jnp.where(me_ref[0] == s, (s + 1) % N_CHIPS, s), i, 0))

    return pl.pallas_call(
        body, name=name,
        grid_spec=pltpu.PrefetchScalarGridSpec(
            num_scalar_prefetch=1, grid=(r // tr,),
            in_specs=[slot(s) for s in range(N_CHIPS)]
            + [pl.BlockSpec((None, tr, cols), lambda i, me_ref: (me_ref[0], i, 0))],
            out_specs=pl.BlockSpec((tr, cols), lambda i, me_ref: (i, 0))),
        out_shape=jax.ShapeDtypeStruct((r, cols), F32),
        compiler_params=_cparams(("parallel",)),
    )(me.reshape(1).astype(jnp.int32), got, got, got, got, own)


def _adamw_math(wv, gv, mv, vv):
    mv = ADAM_B1 * mv + (1.0 - ADAM_B1) * gv
    vv = ADAM_B2 * vv + (1.0 - ADAM_B2) * (gv * gv)
    m_hat = mv / (1.0 - ADAM_B1 ** ADAM_STEP)
    v_hat = vv / (1.0 - ADAM_B2 ** ADAM_STEP)
    return -ADAM_LR * (m_hat / (jnp.sqrt(v_hat) + ADAM_EPS) + ADAM_WD * wv), mv, vv


def _adamw(name, w, g, m, v):
    cols = w.shape[1]
    return _ew(name, _adamw_math, [('r', w), ('r', g), ('r', m), ('r', v)], [('r', cols, F32)] * 3,
               tr=_row_tile(w.shape[0], max(LANES, LANES * LANES // cols)))


def _adamw_halves(name, w, mine, theirs, m, v, c):
    r, cols = mine.shape
    tr = _row_tile(r, 256)
    nb = r // tr

    def body(c_ref, w_ref, a_ref, b_ref, m_ref, v_ref, g_out, d_out, m_out, v_out):
        g = jnp.where(pl.program_id(0) == c_ref[0], a_ref[...], b_ref[...])
        g_out[...] = g
        d_out[...], m_out[...], v_out[...] = _adamw_math(w_ref[...], g, m_ref[...], v_ref[...])

    whole = pl.BlockSpec((tr, cols), lambda h, i, c_ref: (h * nb + i, 0))
    half = pl.BlockSpec((tr, cols), lambda h, i, c_ref: (i, 0))
    return pl.pallas_call(
        body, name=name,
        grid_spec=pltpu.PrefetchScalarGridSpec(
            num_scalar_prefetch=1, grid=(2, nb),
            in_specs=[whole, half, half, whole, whole], out_specs=[whole] * 4),
        out_shape=[jax.ShapeDtypeStruct(w.shape, F32)] * 4,
        compiler_params=_cparams(("parallel", "parallel")),
    )(c.reshape(1).astype(jnp.int32), w, mine, theirs, m, v)


SMALL_ROWS_ALIGN = 2 * N_CHIPS * SUBLANES


MEDIUM_NAMES = ['ssm_b_re', 'ssm_b_im', 'ssm_c_re', 'ssm_c_im']
PACKED_NAMES = [n for n in SMALL_NAMES if n not in MEDIUM_NAMES]


def _pack_small(d):
    flat = jnp.concatenate([d[n].reshape(-1).astype(F32) for n in PACKED_NAMES])
    rows = -(-flat.shape[0] // (LANES * SMALL_ROWS_ALIGN)) * SMALL_ROWS_ALIGN
    return jnp.pad(flat, (0, rows * LANES - flat.shape[0])).reshape(rows, LANES)


def _unpack_small(packed, like):
    flat = packed.reshape(-1)
    out, off = {}, 0
    for n in PACKED_NAMES:
        size = like[n].size
        out[n] = flat[off:off + size].reshape(like[n].shape)
        off += size
    return out


def kernel(x, g_mix, w_in, q_gain, k_gain, rpb, ssm_a_re, ssm_a_im, ssm_b_re, ssm_b_im, ssm_c_re, ssm_c_im, ssm_log_step, ssm_d, w_glu, b_glu, g_out_attn, g_out_ssm, w_out, g_ffn, w_ffn_gate, w_ffn_up, w_ffn_down, loss_target, m_g_mix, m_w_in, m_q_gain, m_k_gain, m_rpb, m_ssm_a_re, m_ssm_a_im, m_ssm_b_re, m_ssm_b_im, m_ssm_c_re, m_ssm_c_im, m_ssm_log_step, m_ssm_d, m_w_glu, m_b_glu, m_g_out_attn, m_g_out_ssm, m_w_out, m_g_ffn, m_w_ffn_gate, m_w_ffn_up, m_w_ffn_down, v_g_mix, v_w_in, v_q_gain, v_k_gain, v_rpb, v_ssm_a_re, v_ssm_a_im, v_ssm_b_re, v_ssm_b_im, v_ssm_c_re, v_ssm_c_im, v_ssm_log_step, v_ssm_d, v_w_glu, v_b_glu, v_g_out_attn, v_g_out_ssm, v_w_out, v_g_ffn, v_w_ffn_gate, v_w_ffn_up, v_w_ffn_down):
    given = dict(locals())
    w = {n: given[n][0] for n in WEIGHT_NAMES}
    mom = {n: given["m_" + n][0] for n in WEIGHT_NAMES}
    var = {n: given["v_" + n][0] for n in WEIGHT_NAMES}
    d = x.shape[-1]
    c = lax.axis_index("c")

    halves = {n: w[n].astype(MXU_DTYPE).reshape((2, w[n].shape[0] // 2, w[n].shape[1])) for n in BIG_NAMES}
    (w_in4,) = _gather_chips("gather_w_in", [halves['w_in']])
    w_in4 = w_in4.reshape((N_CHIPS, -1, w_in4.shape[-1]))

    def chip_sums(tag, grads, payload):
        parts = [g.reshape((N_CHIPS, 2, -1, g.shape[-1])) for g in grads]
        got = _swap_halves("reduce_swap_halves_" + tag, parts)
        return [_add_own_half("reduce_add_%s_%d" % (tag, a), p, gt, c, dt)
                for a, (p, gt, dt) in enumerate(zip(parts, got, payload))]

    reduce_late = lambda tag, grads: chip_sums(tag, grads, [GRAD_PAYLOAD_DTYPE] * len(grads))
    reduce_mid = lambda grads: chip_sums("mid", grads, [GRAD_PAYLOAD_DTYPE] + [F32] * (len(grads) - 1))
    sq, dx, (sums_late, got_late), (sums_mid, got_mid), d_small = _local_step(
        x[0], loss_target[0], w_in4, ('halves', [halves[n] for n in LATE_NAMES]), {n: w[n] for n in SMALL_NAMES},
        reduce_late, reduce_mid)
    loss = lax.psum(0.5 * sq / d, ("x", "y", "c"))

    nbig = len(BIG_NAMES)
    sums_tiny = chip_sums("tiny", [_pack_small(d_small)], [F32])
    got_tiny = list(_scatter_chips("reduce_scatter_tiny", sums_tiny))
    sums = sums_mid[:1] + sums_late + sums_mid[1:] + sums_tiny
    got = got_mid[:1] + got_late + got_mid[1:] + got_tiny
    me = 2 * lax.axis_index("x") + lax.axis_index("y")
    mine = [_sum_chips("reduce_sum_%d" % a, gt, sm_, me) for a, (gt, sm_) in enumerate(zip(got, sums))]
    theirs = _swap_reduced("reduce_swap_reduced", mine)
    in_order = lambda a: jnp.where(c == 0, jnp.stack([mine[a], theirs[a]]), jnp.stack([theirs[a], mine[a]]))
    repl = _gather_chips("gather_small", [in_order(a) for a in range(nbig, len(mine))])
    repl = [r.reshape(-1, LANES) for r in repl]
    like = {n: w[n] for n in SMALL_NAMES}
    grad_small = _unpack_small(repl[-1], like)
    grad_small.update({n: r.reshape(w[n].shape) for n, r in zip(MEDIUM_NAMES, repl)})

    grad_big, delta, new_m, new_v = {}, {}, {}, {}
    for a, n in enumerate(BIG_NAMES):
        grad_big[n], delta[n], new_m[n], new_v[n] = _adamw_halves("adamw_%d" % a, w[n], mine[a], theirs[a],
                                                                  mom[n], var[n], c)
    for n in MEDIUM_NAMES:
        flat = lambda t: t.reshape(-1, w[n].shape[-1])
        res = _adamw("adamw_" + n, flat(w[n]), flat(grad_small[n]), flat(mom[n]), flat(var[n]))
        delta[n], new_m[n], new_v[n] = (t.reshape(w[n].shape) for t in res)
    sd, sm, sv = _adamw("adamw_small", _pack_small(w), repl[-1], _pack_small(mom), _pack_small(var))
    delta.update(_unpack_small(sd, like))
    new_m.update(_unpack_small(sm, like))
    new_v.update(_unpack_small(sv, like))
    grads = {**grad_big, **grad_small}
    lead = lambda t: t[None]
    return (loss, dx[None], *[lead(grads[n]) for n in WEIGHT_NAMES], *[lead(delta[n]) for n in WEIGHT_NAMES],
            *[lead(new_m[n]) for n in WEIGHT_NAMES], *[lead(new_v[n]) for n in WEIGHT_NAMES])
```
